```python
import math
import jax, jax.numpy as jnp
from jax import lax
import numpy as np

D_MODEL = 1024
BATCH = 16
SEQ = 2048
DEPTH = 1
DEC_BATCH = 32
DEC_SEQ = 64
PAST_LEN = 2048

CHUNK = 64
EPS = 1e-6
GDN_HEADS = 4
GDN_DK = 128
GDN_DV = 128
CONV_W = 4
GDN_QK = GDN_HEADS * GDN_DK
GDN_VW = GDN_HEADS * GDN_DV
CONV_DIM = 2 * GDN_QK + GDN_VW
MLA_HEADS = 4
Q_RANK = 384
KV_RANK = 256
NOPE_DIM = 128
ROPE_DIM = 64
V_DIM = 128
QK_DIM = NOPE_DIM + ROPE_DIM
ROPE_THETA = 10000.0
Q_BLOCK = 128
MIX_WIDTH = GDN_VW + MLA_HEADS * V_DIM
IN_WIDTH = CONV_DIM + GDN_VW + 2 * GDN_HEADS + Q_RANK + KV_RANK + ROPE_DIM
N_MEM = 256
MEM_HEADS = 4
MEM_HEAD_DIM = 128
N_EXPERTS = 32
TOP_K = 4
D_FF = D_MODEL
SWIGLU_ALPHA = 1.702
SWIGLU_LIMIT = 7.0
MOE_BLOCK = 128

kernel_name = 'hybrid_gdn_mla_memxattn_moe_stream_step'

F32 = jnp.float32


def rms_norm(x, gain):
    xf = x.astype(F32)
    y = xf * lax.rsqrt(jnp.mean(xf * xf, axis=-1, keepdims=True) + EPS)
    return (y * gain.astype(F32)).astype(x.dtype)


def l2_norm(x):
    xf = x.astype(F32)
    return xf * lax.rsqrt(jnp.sum(xf * xf, axis=-1, keepdims=True) + EPS)


def rope(x, pos):
    half = ROPE_DIM // 2
    inv = ROPE_THETA ** (-jnp.arange(half, dtype=F32) / half)
    ang = pos.astype(F32)[:, None] * inv[None, :]
    cos = jnp.cos(ang)[:, None, :]
    sin = jnp.sin(ang)[:, None, :]
    xf = x.astype(F32)
    x1, x2 = xf[..., :half], xf[..., half:]
    return jnp.concatenate([x1 * cos - x2 * sin, x2 * cos + x1 * sin], axis=-1).astype(x.dtype)


def causal_conv(u, past, w):
    S = u.shape[1]
    upad = jnp.concatenate([past.astype(u.dtype), u], axis=1)
    out = upad[:, 0:S] * w[0]
    for j in range(1, CONV_W):
        out = out + upad[:, j:j + S] * w[j]
    return jax.nn.silu(out), upad[:, upad.shape[1] - (CONV_W - 1):]


def gated_delta_rule(q, k, v, g, beta, s0, chunk):
    B, S, H, DK = q.shape
    DV = v.shape[-1]
    N = S // chunk
    q = l2_norm(q) * (DK ** -0.5)
    k = l2_norm(k)
    v = v.astype(F32)

    def blocks(t):
        return t.reshape(B, N, chunk, H, t.shape[-1]).transpose(1, 0, 3, 2, 4)

    qc, kc, vc = blocks(q), blocks(k), blocks(v)
    gc = g.astype(F32).reshape(B, N, chunk, H).transpose(1, 0, 3, 2)
    bc = beta.astype(F32).reshape(B, N, chunk, H).transpose(1, 0, 3, 2)
    gcum = jnp.cumsum(gc, axis=-1)
    tril = jnp.tril(jnp.ones((chunk, chunk), bool))
    strict = jnp.tril(jnp.ones((chunk, chunk), bool), -1)
    diff = gcum[..., :, None] - gcum[..., None, :]
    decay = jnp.where(tril, jnp.exp(jnp.where(tril, diff, 0.0)), 0.0)
    kb = kc * bc[..., None]
    vb = vc * bc[..., None]
    m = jnp.where(strict, jnp.einsum('nbhid,nbhjd->nbhij', kb, kc) * decay, 0.0)
    eye = jnp.eye(chunk, dtype=F32)
    tinv = lax.linalg.triangular_solve(eye + m, jnp.broadcast_to(eye, m.shape), left_side=True, lower=True)
    u = tinv @ vb
    w = tinv @ (kb * jnp.exp(gcum)[..., None])
    qk = jnp.where(tril, jnp.einsum('nbhid,nbhjd->nbhij', qc, kc) * decay, 0.0)

    def step(state, xs):
        q_i, k_i, u_i, w_i, qk_i, g_i = xs
        v_new = u_i - w_i @ state
        out = (q_i * jnp.exp(g_i)[..., None]) @ state + qk_i @ v_new
        g_last = g_i[..., -1]
        k_dec = k_i * jnp.exp(g_last[..., None] - g_i)[..., None]
        state = state * jnp.exp(g_last)[..., None, None] + jnp.einsum('bhld,bhle->bhde', k_dec, v_new)
        return state, out

    s_final, outs = lax.scan(step, s0.astype(F32), (qc, kc, u, w, qk, gcum))
    o = outs.transpose(1, 0, 3, 2, 4).reshape(B, S, H, DV)
    return o, s_final


def block_causal_attention(q, k, v, q_pos, k_pos):
    B, Sq, H, _ = q.shape
    qb = min(Q_BLOCK, Sq)
    nb = Sq // qb
    qs = q.reshape(B, nb, qb, H, q.shape[-1]).transpose(1, 0, 2, 3, 4)
    ps = q_pos.reshape(nb, qb)
    k_chunk = k_pos // CHUNK

    def one(args):
        qi, pi = args
        s = jnp.einsum('bqhd,bkhd->bhqk', qi, k).astype(F32)
        mask = k_chunk[None, :] <= (pi // CHUNK)[:, None]
        s = jnp.where(mask[None, None], s, -jnp.inf)
        pr = jax.nn.softmax(s, axis=-1).astype(v.dtype)
        return jnp.einsum('bhqk,bkhd->bqhd', pr, v)

    out = lax.map(one, (qs, ps))
    return out.transpose(1, 0, 2, 3, 4).reshape(B, Sq, H, v.shape[-1])


def memory_kv(mem, mem_norm, w_mk, w_mv, mk_norm):
    B = mem.shape[0]
    m = rms_norm(mem, mem_norm)
    k = rms_norm((m @ w_mk).reshape(B, N_MEM, MEM_HEADS, MEM_HEAD_DIM), mk_norm)
    v = (m @ w_mv).reshape(B, N_MEM, MEM_HEADS, MEM_HEAD_DIM)
    return k, v


def moe_ffn(h, w_router, b_router, w_gu, b_gu, w_down, b_down):
    T, D = h.shape
    logits = (h @ w_router + b_router).astype(F32)
    top_val, top_idx = lax.top_k(logits, TOP_K)
    gate = jax.nn.softmax(top_val, axis=-1)
    n_assign = T * TOP_K
    flat_e = top_idx.reshape(-1).astype(jnp.int32)
    flat_tok = jnp.arange(n_assign, dtype=jnp.int32) // TOP_K
    order = jnp.argsort(flat_e)
    e_sorted = flat_e[order]
    tok_sorted = flat_tok[order]
    gate_sorted = gate.reshape(-1)[order]
    counts = jnp.zeros((N_EXPERTS,), jnp.int32).at[flat_e].add(1)
    padded = (counts + MOE_BLOCK - 1) // MOE_BLOCK * MOE_BLOCK
    pad_end = jnp.cumsum(padded)
    pad_start = pad_end - padded
    start = jnp.cumsum(counts) - counts
    dest = pad_start[e_sorted] + jnp.arange(n_assign, dtype=jnp.int32) - start[e_sorted]
    n_blocks = -(-n_assign // MOE_BLOCK) + N_EXPERTS
    row_tok = jnp.full((n_blocks * MOE_BLOCK,), T, jnp.int32).at[dest].set(tok_sorted)
    h_pad = jnp.concatenate([h, jnp.zeros((1, D), h.dtype)], axis=0)
    rows = h_pad[row_tok].reshape(n_blocks, MOE_BLOCK, D)
    block_e = jnp.minimum(jnp.searchsorted(pad_end, jnp.arange(n_blocks, dtype=jnp.int32) * MOE_BLOCK, side='right'), N_EXPERTS - 1)

    def expert_block(args):
        xb, e = args
        gu = xb @ w_gu[e] + b_gu[e]
        gt = jnp.minimum(gu[:, :D_FF], SWIGLU_LIMIT)
        up = jnp.clip(gu[:, D_FF:], -SWIGLU_LIMIT, SWIGLU_LIMIT)
        act = gt * jax.nn.sigmoid(SWIGLU_ALPHA * gt) * (up + 1.0)
        return act @ w_down[e] + b_down[e]

    y_rows = lax.map(expert_block, (rows, block_e)).reshape(n_blocks * MOE_BLOCK, D)
    y = y_rows[dest] * gate_sorted[:, None].astype(h.dtype)
    return jax.ops.segment_sum(y, tok_sorted, num_segments=T)


def trunk_layer(x, lat_past, kpe_past, s0, conv_past, mem_k, mem_v, p):
    B, S, D = x.shape
    P = lat_past.shape[1]
    q_pos = P + jnp.arange(S, dtype=jnp.int32)
    k_pos = jnp.arange(P + S, dtype=jnp.int32)

    h = rms_norm(x, p['norm_mix'])
    proj = h @ p['w_in']
    idx = [int(i) for i in np.cumsum([CONV_DIM, GDN_VW, GDN_HEADS, GDN_HEADS, Q_RANK, KV_RANK])]
    u, z, a, b, cq, ckv, kpe = jnp.split(proj, idx, axis=-1)

    uc, conv_new = causal_conv(u, conv_past, p['conv_w'])
    qg = uc[..., :GDN_QK].reshape(B, S, GDN_HEADS, GDN_DK)
    kg = uc[..., GDN_QK:2 * GDN_QK].reshape(B, S, GDN_HEADS, GDN_DK)
    vg = uc[..., 2 * GDN_QK:].reshape(B, S, GDN_HEADS, GDN_DV)
    beta = jax.nn.sigmoid(b.astype(F32))
    g = -jnp.exp(p['a_log'].astype(F32)) * jax.nn.softplus(a.astype(F32) + p['dt_bias'].astype(F32))
    o_gdn, s_new = gated_delta_rule(qg, kg, vg, g, beta, s0, min(CHUNK, S))
    o_gdn = rms_norm(o_gdn.astype(x.dtype), p['gdn_norm']) * jax.nn.silu(z.reshape(B, S, GDN_HEADS, GDN_DV))

    c_q = rms_norm(cq, p['q_a_norm'])
    q = rms_norm((c_q @ p['w_qb']).reshape(B, S, MLA_HEADS, QK_DIM), p['q_norm'])
    q = jnp.concatenate([q[..., :NOPE_DIM], rope(q[..., NOPE_DIM:], q_pos)], axis=-1)
    lat_new = rms_norm(ckv, p['kv_a_norm'])
    kpe_new = rope(rms_norm(kpe, p['k_rope_norm'])[:, :, None, :], q_pos)[:, :, 0, :]
    lat_all = jnp.concatenate([lat_past.astype(x.dtype), lat_new], axis=1)
    kpe_all = jnp.concatenate([kpe_past.astype(x.dtype), kpe_new], axis=1)
    Sk = P + S
    kv = (lat_all @ p['w_kvb']).reshape(Sk and B, Sk, MLA_HEADS, NOPE_DIM + V_DIM)
    k_nope = rms_norm(kv[..., :NOPE_DIM], p['k_nope_norm'])
    v_mla = kv[..., NOPE_DIM:]
    k_mla = jnp.concatenate([k_nope, jnp.broadcast_to(kpe_all[:, :, None, :], (B, Sk, MLA_HEADS, ROPE_DIM))], axis=-1)
    o_mla = block_causal_attention(q * (QK_DIM ** -0.5), k_mla, v_mla, q_pos, k_pos)

    mix = jnp.concatenate([o_gdn.reshape(B, S, GDN_VW), o_mla.reshape(B, S, MLA_HEADS * V_DIM)], axis=-1)
    x = x + mix @ p['w_out']

    h = rms_norm(x, p['norm_mem'])
    qm = rms_norm((h @ p['w_mq']).reshape(B, S, MEM_HEADS, MEM_HEAD_DIM), p['mq_norm']) * (MEM_HEAD_DIM ** -0.5)
    sm = jnp.einsum('bqhd,bkhd->bhqk', qm, mem_k.astype(x.dtype)).astype(F32)
    pm = jax.nn.softmax(sm, axis=-1).astype(x.dtype)
    om = jnp.einsum('bhqk,bkhd->bqhd', pm, mem_v.astype(x.dtype))
    x = x + om.reshape(B, S, MEM_HEADS * MEM_HEAD_DIM) @ p['w_mo']

    h = rms_norm(x, p['norm_ffn'])
    y = moe_ffn(h.reshape(B * S, D), p['w_router'], p['b_router'], p['w_gu'], p['b_gu'], p['w_down'], p['b_down'])
    x = x + y.reshape(B, S, D)
    return x, lat_new, kpe_new, s_new.astype(s0.dtype), conv_new


def setup_inputs(seed: int = 0) -> dict:
    key = jax.random.key(seed)
    ks = iter(jax.random.split(key, 48))

    def nrm(shape, scale):
        return jax.random.normal(next(ks), shape, F32) * scale

    def gain(n):
        return 1.0 + nrm((DEPTH, n), 0.02)

    L = DEPTH
    dt = jnp.exp(jax.random.uniform(next(ks), (L, GDN_HEADS), F32, math.log(1e-3), math.log(1e-1)))
    dt_bias = dt + jnp.log(-jnp.expm1(-dt))
    a_log = jnp.log(jax.random.uniform(next(ks), (L, GDN_HEADS), F32, 1.0, 16.0))
    return {
        'x_prompt': nrm((BATCH, SEQ, D_MODEL), 1.0),
        'x_sample': nrm((DEC_BATCH, DEC_SEQ, D_MODEL), 1.0),
        'cache_kv_latent': nrm((L, DEC_BATCH, PAST_LEN, KV_RANK), 1.0),
        'cache_k_rope': nrm((L, DEC_BATCH, PAST_LEN, ROPE_DIM), 1.0),
        'state_gdn': nrm((L, DEC_BATCH, GDN_HEADS, GDN_DK, GDN_DV), 0.1),
        'state_conv': nrm((L, DEC_BATCH, CONV_W - 1, CONV_DIM), 1.0),
        'cache_mem_k': nrm((L, DEC_BATCH, N_MEM, MEM_HEADS, MEM_HEAD_DIM), 1.0),
        'cache_mem_v': nrm((L, DEC_BATCH, N_MEM, MEM_HEADS, MEM_HEAD_DIM), 1.0),
        'mem_prompt': nrm((BATCH, N_MEM, D_MODEL), 1.0),
        'norm_mix': gain(D_MODEL),
        'w_in': nrm((L, D_MODEL, IN_WIDTH), D_MODEL ** -0.5),
        'conv_w': nrm((L, CONV_W, CONV_DIM), CONV_W ** -0.5),
        'a_log': a_log,
        'dt_bias': dt_bias,
        'gdn_norm': gain(GDN_DV),
        'q_a_norm': gain(Q_RANK),
        'w_qb': nrm((L, Q_RANK, MLA_HEADS * QK_DIM), Q_RANK ** -0.5),
        'kv_a_norm': gain(KV_RANK),
        'w_kvb': nrm((L, KV_RANK, MLA_HEADS * (NOPE_DIM + V_DIM)), KV_RANK ** -0.5),
        'q_norm': gain(QK_DIM),
        'k_nope_norm': gain(NOPE_DIM),
        'k_rope_norm': gain(ROPE_DIM),
        'w_out': nrm((L, MIX_WIDTH, D_MODEL), MIX_WIDTH ** -0.5),
        'norm_mem': gain(D_MODEL),
        'mem_norm': gain(D_MODEL),
        'w_mq': nrm((L, D_MODEL, MEM_HEADS * MEM_HEAD_DIM), D_MODEL ** -0.5),
        'w_mk': nrm((L, D_MODEL, MEM_HEADS * MEM_HEAD_DIM), D_MODEL ** -0.5),
        'w_mv': nrm((L, D_MODEL, MEM_HEADS * MEM_HEAD_DIM), D_MODEL ** -0.5),
        'mq_norm': gain(MEM_HEAD_DIM),
        'mk_norm': gain(MEM_HEAD_DIM),
        'w_mo': nrm((L, MEM_HEADS * MEM_HEAD_DIM, D_MODEL), (MEM_HEADS * MEM_HEAD_DIM) ** -0.5),
        'norm_ffn': gain(D_MODEL),
        'w_router': nrm((L, D_MODEL, N_EXPERTS), D_MODEL ** -0.5),
        'b_router': nrm((L, N_EXPERTS), 0.01),
        'w_gu': nrm((L, N_EXPERTS, D_MODEL, 2 * D_FF), D_MODEL ** -0.5),
        'b_gu': nrm((L, N_EXPERTS, 2 * D_FF), 0.01),
        'w_down': nrm((L, N_EXPERTS, D_FF, D_MODEL), D_FF ** -0.5),
        'b_down': nrm((L, N_EXPERTS, D_MODEL), 0.01),
    }


def reference(x_prompt, x_sample, cache_kv_latent, cache_k_rope, state_gdn, state_conv, cache_mem_k, cache_mem_v,
              mem_prompt, norm_mix, w_in, conv_w, a_log, dt_bias, gdn_norm, q_a_norm, w_qb, kv_a_norm, w_kvb,
              q_norm, k_nope_norm, k_rope_norm, w_out, norm_mem, mem_norm, w_mq, w_mk, w_mv, mq_norm, mk_norm,
              w_mo, norm_ffn, w_router, b_router, w_gu, b_gu, w_down, b_down):
    yp, ys = x_prompt, x_sample
    bp = x_prompt.shape[0]
    dt = x_prompt.dtype
    p_lat, p_kpe, p_s, p_conv, p_mk, p_mv = [], [], [], [], [], []
    s_lat, s_kpe, s_s, s_conv = [], [], [], []
    for l in range(DEPTH):
        p = {
            'norm_mix': norm_mix[l], 'w_in': w_in[l], 'conv_w': conv_w[l], 'a_log': a_log[l],
            'dt_bias': dt_bias[l], 'gdn_norm': gdn_norm[l], 'q_a_norm': q_a_norm[l], 'w_qb': w_qb[l],
            'kv_a_norm': kv_a_norm[l], 'w_kvb': w_kvb[l], 'q_norm': q_norm[l], 'k_nope_norm': k_nope_norm[l],
            'k_rope_norm': k_rope_norm[l], 'w_out': w_out[l], 'norm_mem': norm_mem[l], 'w_mq': w_mq[l],
            'mq_norm': mq_norm[l], 'w_mo': w_mo[l], 'norm_ffn': norm_ffn[l], 'w_router': w_router[l],
            'b_router': b_router[l], 'w_gu': w_gu[l], 'b_gu': b_gu[l], 'w_down': w_down[l], 'b_down': b_down[l],
        }
        mk, mv = memory_kv(mem_prompt, mem_norm[l], w_mk[l], w_mv[l], mk_norm[l])
        yp, lat, kpe, s_fin, cv = trunk_layer(
            yp, jnp.zeros((bp, 0, KV_RANK), dt), jnp.zeros((bp, 0, ROPE_DIM), dt),
            jnp.zeros((bp, GDN_HEADS, GDN_DK, GDN_DV), dt), jnp.zeros((bp, CONV_W - 1, CONV_DIM), dt), mk, mv, p)
        p_lat.append(lat); p_kpe.append(kpe); p_s.append(s_fin); p_conv.append(cv); p_mk.append(mk); p_mv.append(mv)
        ys, lat2, kpe2, s_fin2, cv2 = trunk_layer(
            ys, cache_kv_latent[l], cache_k_rope[l], state_gdn[l], state_conv[l], cache_mem_k[l], cache_mem_v[l], p)
        s_lat.append(lat2); s_kpe.append(kpe2); s_s.append(s_fin2); s_conv.append(cv2)
    return (yp, ys, jnp.stack(p_lat), jnp.stack(p_kpe), jnp.stack(p_s), jnp.stack(p_conv), jnp.stack(p_mk),
            jnp.stack(p_mv), jnp.stack(s_lat), jnp.stack(s_kpe), jnp.stack(s_s), jnp.stack(s_conv))
```

```python
import functools
import math

import numpy as np
import jax
import jax.numpy as jnp
from jax import lax
from jax.experimental import pallas as pl
from jax.experimental.pallas import tpu as pltpu

F32 = jnp.float32
BF16 = jnp.bfloat16

D_MODEL = 1024
CHUNK = 64
EPS = 1e-6
GDN_HEADS = 4
GDN_DK = 128
GDN_DV = 128
CONV_W = 4
GDN_QK = GDN_HEADS * GDN_DK
GDN_VW = GDN_HEADS * GDN_DV
CONV_DIM = 2 * GDN_QK + GDN_VW
MLA_HEADS = 4
Q_RANK = 384
KV_RANK = 256
NOPE_DIM = 128
ROPE_DIM = 64
V_DIM = 128
QK_DIM = NOPE_DIM + ROPE_DIM
ROPE_THETA = 10000.0
N_MEM = 256
MEM_HEADS = 4
MEM_HEAD_DIM = 128
N_EXPERTS = 32
TOP_K = 4
D_FF = D_MODEL
SWIGLU_ALPHA = 1.702
SWIGLU_LIMIT = 7.0

LANES = 128
Q_SLAB = 2 * LANES
NEG_BIG = -1e30
VMEM_LIMIT = 56 * 1024 * 1024
MOE_ROWS = 256


def _cparams(sem):
    return pltpu.CompilerParams(dimension_semantics=sem, vmem_limit_bytes=VMEM_LIMIT)


def _dot(a, b):
    return jnp.dot(a, b, preferred_element_type=F32)


def _dot_nt(a, b):
    return lax.dot_general(a, b, (((1,), (1,)), ((), ())), preferred_element_type=F32)


def _dot_tn(a, b):
    return lax.dot_general(a, b, (((0,), (0,)), ((), ())), preferred_element_type=F32)


def _rms(x, gain, n=None):
    n = x.shape[-1] if n is None else n
    ss = jnp.sum(x * x, axis=-1, keepdims=True) * (1.0 / n)
    return (x * lax.rsqrt(ss + EPS)) * gain


def _sigmoid(x):
    return 1.0 / (1.0 + jnp.exp(-x))


def _rope128(r, cos, sna, snb):
    return r * cos + pltpu.roll(r, 96, 1) * sna + pltpu.roll(r, 32, 1) * snb


def _const_spec(shape):
    nd = len(shape)
    return pl.BlockSpec(shape, lambda *_: (0,) * nd)


def _inproj_kernel(x_ref, nmix_ref, wu_ref, wz_ref, wcq_ref, wckv_ref, ws_ref, qan_ref, wqb_ref, qn_ref,
                   kvan_ref, krn_ref, alog_ref, dtb_ref, cos_ref, sna_ref, snb_ref,
                   u_ref, z_ref, q_ref, lat_ref, small_ref):
    x = x_ref[...]
    hb = _rms(x, nmix_ref[...]).astype(BF16)
    u_ref[...] = _dot(hb, wu_ref[...])
    z_ref[...] = _dot(hb, wz_ref[...])
    cos, sna, snb = cos_ref[...], sna_ref[...], snb_ref[...]

    cq = _rms(_dot(hb, wcq_ref[...]), qan_ref[...]).astype(BF16)
    qf = _dot(cq, wqb_ref[...])
    scale = QK_DIM ** -0.5
    for h in range(MLA_HEADS):
        slab = qf[:, h * Q_SLAB:(h + 1) * Q_SLAB]
        slab = _rms(slab, qn_ref[...], n=QK_DIM)
        nope = slab[:, :LANES]
        ropd = _rope128(slab[:, LANES:], cos, sna, snb)
        q_ref[:, h * Q_SLAB:h * Q_SLAB + LANES] = (nope * scale).astype(BF16)
        q_ref[:, h * Q_SLAB + LANES:(h + 1) * Q_SLAB] = (ropd * scale).astype(BF16)

    lat_ref[...] = _rms(_dot(hb, wckv_ref[...]), kvan_ref[...])

    sm = _dot(hb, ws_ref[...])
    lane = lax.broadcasted_iota(jnp.int32, sm.shape, 1)
    kp = jnp.where(lane < ROPE_DIM, sm, 0.0)
    kpe = _rope128(_rms(kp, krn_ref[...], n=ROPE_DIM), cos, sna, snb)
    sp = sm + dtb_ref[...]
    softplus = jnp.maximum(sp, 0.0) + jnp.log1p(jnp.exp(-jnp.abs(sp)))
    g = -jnp.exp(alog_ref[...]) * softplus
    beta = _sigmoid(sm)
    small_ref[...] = jnp.where(lane < ROPE_DIM, kpe,
                               jnp.where(lane < ROPE_DIM + GDN_HEADS, g,
                                         jnp.where(lane < ROPE_DIM + 2 * GDN_HEADS, beta, 0.0)))


def _inproj(x2d, S, tm, pw, tabs):
    T = x2d.shape[0]
    nblk_s = S // tm
    row = lambda n: pl.BlockSpec((tm, n), lambda i: (i, 0))
    tab = pl.BlockSpec((tm, LANES), lambda i: (i % nblk_s, 0))
    consts = [pw['nmix'], pw['w_u'], pw['w_z'], pw['w_cq'], pw['w_ckv'], pw['w_s'], pw['qan'], pw['w_qb'],
              pw['qn'], pw['kvan'], pw['krn'], pw['alog'], pw['dtb']]
    return pl.pallas_call(
        _inproj_kernel,
        grid=(T // tm,),
        in_specs=[row(D_MODEL)] + [_const_spec(c.shape) for c in consts] + [tab, tab, tab],
        out_specs=[row(CONV_DIM), row(GDN_VW), row(MLA_HEADS * Q_SLAB), row(KV_RANK), row(LANES)],
        out_shape=[jax.ShapeDtypeStruct((T, CONV_DIM), F32), jax.ShapeDtypeStruct((T, GDN_VW), F32),
                   jax.ShapeDtypeStruct((T, MLA_HEADS * Q_SLAB), BF16), jax.ShapeDtypeStruct((T, KV_RANK), F32),
                   jax.ShapeDtypeStruct((T, LANES), F32)],
        compiler_params=_cparams(("parallel",)),
        name="inproj",
    )(x2d, *consts, *tabs)


def _split3(x):
    hi = x.astype(BF16)
    r = x - hi.astype(F32)
    mid = r.astype(BF16)
    lo = (r - mid.astype(F32)).astype(BF16)
    return hi, mid, lo


def _gdn_kernel(u_ref, small_ref, z_ref, convw_ref, cpast_ref, s0_ref, gnorm_ref,
                o_ref, sfin_ref, cnew_ref, ext_ref, uc_ref, state_ref, *, LC):
    j = pl.program_id(1)
    nj = pl.num_programs(1)
    PADR = 8

    @pl.when(j == 0)
    def _():
        state_ref[...] = s0_ref[...]
        ext_ref[PADR - (CONV_W - 1):PADR, :] = cpast_ref[...]

    ext_ref[PADR:PADR + LC, :] = u_ref[...]
    w = convw_ref[...]
    acc = ext_ref[PADR:PADR + LC, :] * w[CONV_W - 1:CONV_W, :]
    for t in range(1, CONV_W):
        acc = acc + ext_ref[PADR - t:PADR - t + LC, :] * w[CONV_W - 1 - t:CONV_W - t, :]
    uc_ref[...] = acc * _sigmoid(acc)
    ext_ref[0:PADR, :] = ext_ref[LC:LC + PADR, :]

    @pl.when(j == nj - 1)
    def _():
        cnew_ref[...] = ext_ref[PADR - (CONV_W - 1):PADR, :]

    ti = lax.broadcasted_iota(jnp.int32, (CHUNK, CHUNK), 0)
    tj = lax.broadcasted_iota(jnp.int32, (CHUNK, CHUNK), 1)
    tril = ti >= tj
    strict = ti > tj
    trilb = jnp.where(tril, 1.0, 0.0).astype(BF16)
    eye = jnp.where(ti == tj, 1.0, 0.0)
    gnorm = gnorm_ref[...]
    g0 = ROPE_DIM
    b0 = ROPE_DIM + GDN_HEADS

    def chunk(c, carry):
        r0 = pl.multiple_of(c * CHUNK, CHUNK)
        rows = pl.ds(r0, CHUNK)
        sm = small_ref[rows, :]
        gc = sum(_dot(trilb, part) for part in _split3(sm))
        gct = gc.T
        for h in range(GDN_HEADS):
            q = uc_ref[rows, h * GDN_DK:(h + 1) * GDN_DK]
            k = uc_ref[rows, GDN_QK + h * GDN_DK:GDN_QK + (h + 1) * GDN_DK]
            v = uc_ref[rows, 2 * GDN_QK + h * GDN_DV:2 * GDN_QK + (h + 1) * GDN_DV]
            q = (q * lax.rsqrt(jnp.sum(q * q, -1, keepdims=True) + EPS)) * (GDN_DK ** -0.5)
            k = k * lax.rsqrt(jnp.sum(k * k, -1, keepdims=True) + EPS)
            gcol = gc[:, g0 + h:g0 + h + 1]
            grow = gct[g0 + h:g0 + h + 1, :]
            beta = sm[:, b0 + h:b0 + h + 1]
            glast = grow[:, CHUNK - 1:CHUNK]
            decay = jnp.where(tril, jnp.exp(jnp.where(tril, gcol - grow, 0.0)), 0.0)
            kb = k * beta
            vb = v * beta
            kbf = k.astype(BF16)
            m = jnp.where(strict, _dot_nt(kb.astype(BF16), kbf) * decay, 0.0)
            bi, bj = ti // 2, tj // 2
            xinv = eye - jnp.where(bi == bj, m, 0.0)
            blk = 2
            while blk < CHUNK:
                off = jnp.where((ti // (2 * blk) == tj // (2 * blk)) & (ti // blk != tj // blk), m, 0.0)
                xb = xinv.astype(BF16)
                t1 = _dot(off.astype(BF16), xb)
                xinv = xinv - _dot(xb, t1.astype(BF16))
                blk *= 2
            xb = xinv.astype(BF16)
            egc = jnp.exp(gcol)
            uu = _dot(xb, vb.astype(BF16))
            ww = _dot(xb, (kb * egc).astype(BF16))
            qk = jnp.where(tril, _dot_nt(q.astype(BF16), kbf) * decay, 0.0)
            st = state_ref[h]
            stb = st.astype(BF16)
            v_new = uu - _dot(ww.astype(BF16), stb)
            out = _dot((q * egc).astype(BF16), stb) + _dot(qk.astype(BF16), v_new.astype(BF16))
            k_dec = k * jnp.exp(glast - gcol)
            state_ref[h] = st * jnp.exp(glast) + _dot_tn(k_dec.astype(BF16), v_new.astype(BF16))
            zz = z_ref[rows, h * GDN_DV:(h + 1) * GDN_DV]
            og = _rms(out, gnorm) * (zz * _sigmoid(zz))
            o_ref[rows, h * GDN_DV:(h + 1) * GDN_DV] = og.astype(BF16)
        return carry

    lax.fori_loop(0, LC // CHUNK, chunk, 0)

    @pl.when(j == nj - 1)
    def _():
        sfin_ref[...] = state_ref[...]


def _gdn(u3, small3, z3, conv_w, conv_past, s0, gnorm, LC):
    B, S, _ = u3.shape
    tile = lambda n: pl.BlockSpec((None, LC, n), lambda b, j: (b, j, 0))
    return pl.pallas_call(
        functools.partial(_gdn_kernel, LC=LC),
        grid=(B, S // LC),
        in_specs=[tile(CONV_DIM), tile(LANES), tile(GDN_VW), _const_spec(conv_w.shape),
                  pl.BlockSpec((None, CONV_W - 1, CONV_DIM), lambda b, j: (b, 0, 0)),
                  pl.BlockSpec((None, GDN_HEADS, GDN_DK, GDN_DV), lambda b, j: (b, 0, 0, 0)),
                  _const_spec(gnorm.shape)],
        out_specs=[tile(GDN_VW),
                   pl.BlockSpec((None, GDN_HEADS, GDN_DK, GDN_DV), lambda b, j: (b, 0, 0, 0)),
                   pl.BlockSpec((None, CONV_W - 1, CONV_DIM), lambda b, j: (b, 0, 0))],
        out_shape=[jax.ShapeDtypeStruct((B, S, GDN_VW), BF16),
                   jax.ShapeDtypeStruct((B, GDN_HEADS, GDN_DK, GDN_DV), F32),
                   jax.ShapeDtypeStruct((B, CONV_W - 1, CONV_DIM), F32)],
        scratch_shapes=[pltpu.VMEM((LC + 8, CONV_DIM), F32), pltpu.VMEM((LC, CONV_DIM), F32),
                        pltpu.VMEM((GDN_HEADS, GDN_DK, GDN_DV), F32)],
        compiler_params=_cparams(("parallel", "arbitrary")),
        name="gdn",
    )(u3, small3, z3, conv_w, conv_past, s0, gnorm)


def _kvproj_kernel(lat_ref, kpe_ref, wkvb_ref, knn_ref, kn_ref, kr_ref, v_ref):
    kv = _dot(lat_ref[...].astype(BF16), wkvb_ref[...])
    hw = MLA_HEADS * NOPE_DIM
    for h in range(MLA_HEADS):
        kh = kv[:, h * NOPE_DIM:(h + 1) * NOPE_DIM]
        kn_ref[:, h * NOPE_DIM:(h + 1) * NOPE_DIM] = _rms(kh, knn_ref[...]).astype(BF16)
    v_ref[...] = kv[:, hw:].astype(BF16)
    kp = kpe_ref[...]
    lane = lax.broadcasted_iota(jnp.int32, kp.shape, 1)
    kr_ref[...] = jnp.where(lane < ROPE_DIM, kp, 0.0).astype(BF16)


def _kvproj(lat2d, kpe2d, w_kvb, knn, tm):
    T = lat2d.shape[0]
    row = lambda n: pl.BlockSpec((tm, n), lambda i: (i, 0))
    return pl.pallas_call(
        _kvproj_kernel,
        grid=(T // tm,),
        in_specs=[row(KV_RANK), row(LANES), _const_spec(w_kvb.shape), _const_spec(knn.shape)],
        out_specs=[row(MLA_HEADS * NOPE_DIM), row(LANES), row(MLA_HEADS * V_DIM)],
        out_shape=[jax.ShapeDtypeStruct((T, MLA_HEADS * NOPE_DIM), BF16), jax.ShapeDtypeStruct((T, LANES), BF16),
                   jax.ShapeDtypeStruct((T, MLA_HEADS * V_DIM), BF16)],
        compiler_params=_cparams(("parallel",)),
        name="kvproj",
    )(lat2d, kpe2d, w_kvb, knn)


def _last_kblock(qi, tq, tk, P, nk):
    last_key = ((P + qi * tq + tq - 1) // CHUNK) * CHUNK + CHUNK - 1
    return jnp.minimum(last_key // tk, nk - 1)


def _flash_kernel(q_ref, kn_ref, kr_ref, v_ref, o_ref, m_ref, l_ref, acc_ref, *, tq, tk, P, S, nk):
    qi = pl.program_id(1)
    ki = pl.program_id(2)

    @pl.when(ki == 0)
    def _():
        m_ref[...] = jnp.full(m_ref.shape, NEG_BIG, F32)
        l_ref[...] = jnp.zeros(l_ref.shape, F32)
        acc_ref[...] = jnp.zeros(acc_ref.shape, F32)

    @pl.when(ki <= _last_kblock(qi, tq, tk, P, nk))
    def _():
        qpos = P + qi * tq + lax.broadcasted_iota(jnp.int32, (tq, tk), 0)
        kpos = ki * tk + lax.broadcasted_iota(jnp.int32, (tq, tk), 1)
        mask = ((kpos // CHUNK) <= (qpos // CHUNK)) & (kpos < P + S)
        kr = kr_ref[...]
        for h in range(MLA_HEADS):
            qh = q_ref[:, h * Q_SLAB:(h + 1) * Q_SLAB]
            kh = jnp.concatenate([kn_ref[:, h * NOPE_DIM:(h + 1) * NOPE_DIM], kr], axis=1)
            s = jnp.where(mask, _dot_nt(qh, kh), NEG_BIG)
            m_prev = m_ref[h]
            m_new = jnp.maximum(m_prev, jnp.max(s, axis=-1, keepdims=True))
            alpha = jnp.exp(m_prev - m_new)
            p = jnp.exp(s - m_new)
            l_ref[h] = alpha * l_ref[h] + jnp.sum(p, axis=-1, keepdims=True)
            acc_ref[h] = alpha * acc_ref[h] + _dot(p.astype(BF16), v_ref[:, h * V_DIM:(h + 1) * V_DIM])
            m_ref[h] = m_new

    @pl.when(ki == nk - 1)
    def _():
        for h in range(MLA_HEADS):
            o_ref[:, h * V_DIM:(h + 1) * V_DIM] = (acc_ref[h] / l_ref[h]).astype(BF16)


def _flash(q3, kn3, kr3, v3, P, S, tq, tk):
    B = q3.shape[0]
    skp = kn3.shape[1]
    nk = skp // tk
    kmap = lambda b, qi, ki: (b, jnp.minimum(ki, _last_kblock(qi, tq, tk, P, nk)), 0)
    return pl.pallas_call(
        functools.partial(_flash_kernel, tq=tq, tk=tk, P=P, S=S, nk=nk),
        grid=(B, S // tq, nk),
        in_specs=[pl.BlockSpec((None, tq, MLA_HEADS * Q_SLAB), lambda b, qi, ki: (b, qi, 0)),
                  pl.BlockSpec((None, tk, MLA_HEADS * NOPE_DIM), kmap),
                  pl.BlockSpec((None, tk, LANES), kmap),
                  pl.BlockSpec((None, tk, MLA_HEADS * V_DIM), kmap)],
        out_specs=pl.BlockSpec((None, tq, MLA_HEADS * V_DIM), lambda b, qi, ki: (b, qi, 0)),
        out_shape=jax.ShapeDtypeStruct((B, S, MLA_HEADS * V_DIM), BF16),
        scratch_shapes=[pltpu.VMEM((MLA_HEADS, tq, 1), F32), pltpu.VMEM((MLA_HEADS, tq, 1), F32),
                        pltpu.VMEM((MLA_HEADS, tq, V_DIM), F32)],
        compiler_params=_cparams(("parallel", "parallel", "arbitrary")),
        name="mla_attn",
    )(q3, kn3, kr3, v3)


def _memkv_kernel(mem_ref, mnorm_ref, wmk_ref, wmv_ref, mkn_ref, k_ref, v_ref):
    mb = _rms(mem_ref[...], mnorm_ref[...]).astype(BF16)
    kf = _dot(mb, wmk_ref[...])
    for h in range(MEM_HEADS):
        sl = slice(h * MEM_HEAD_DIM, (h + 1) * MEM_HEAD_DIM)
        k_ref[:, sl] = _rms(kf[:, sl], mkn_ref[...])
    v_ref[...] = _dot(mb, wmv_ref[...])


def _memkv(mem2d, mnorm, w_mk, w_mv, mkn, tm):
    T = mem2d.shape[0]
    hw = MEM_HEADS * MEM_HEAD_DIM
    row = lambda n: pl.BlockSpec((tm, n), lambda i: (i, 0))
    return pl.pallas_call(
        _memkv_kernel,
        grid=(T // tm,),
        in_specs=[row(D_MODEL), _const_spec(mnorm.shape), _const_spec(w_mk.shape), _const_spec(w_mv.shape),
                  _const_spec(mkn.shape)],
        out_specs=[row(hw), row(hw)],
        out_shape=[jax.ShapeDtypeStruct((T, hw), F32), jax.ShapeDtypeStruct((T, hw), F32)],
        compiler_params=_cparams(("parallel",)),
        name="mem_kv",
    )(mem2d, mnorm, w_mk, w_mv, mkn)


def _mix_kernel(x_ref, og_ref, om_ref, mk_ref, mv_ref, wout_ref, nmem_ref, wmq_ref, mqn_ref, wmo_ref, nffn_ref,
                wrh_ref, wrl_ref, br_ref, x2_ref, h3_ref, idx_ref, gate_ref):
    x1 = x_ref[...] + _dot(og_ref[...], wout_ref[0:GDN_VW, :]) + _dot(om_ref[...], wout_ref[GDN_VW:, :])
    hb = _rms(x1, nmem_ref[...]).astype(BF16)
    qm = _dot(hb, wmq_ref[...])
    heads = []
    for h in range(MEM_HEADS):
        sl = slice(h * MEM_HEAD_DIM, (h + 1) * MEM_HEAD_DIM)
        qh = (_rms(qm[:, sl], mqn_ref[...]) * (MEM_HEAD_DIM ** -0.5)).astype(BF16)
        s = _dot_nt(qh, mk_ref[:, sl].astype(BF16))
        p = jnp.exp(s - jnp.max(s, axis=-1, keepdims=True))
        p = p / jnp.sum(p, axis=-1, keepdims=True)
        heads.append(_dot(p.astype(BF16), mv_ref[:, sl].astype(BF16)).astype(BF16))
    x2 = x1 + _dot(jnp.concatenate(heads, axis=1), wmo_ref[...])
    x2_ref[...] = x2
    h3 = _rms(x2, nffn_ref[...])
    hi = h3.astype(BF16)
    h3_ref[...] = hi
    lo = (h3 - hi.astype(F32)).astype(BF16)
    wrh = wrh_ref[...]
    logits = _dot(hi, wrh) + _dot(lo, wrh) + _dot(hi, wrl_ref[...]) + br_ref[...]

    lane = lax.broadcasted_iota(jnp.int32, logits.shape, 1).astype(F32)
    vals, idxs = [], []
    cur = logits
    for _ in range(TOP_K):
        mx = jnp.max(cur, axis=-1, keepdims=True)
        ix = jnp.min(jnp.where(cur == mx, lane, float(LANES)), axis=-1, keepdims=True)
        vals.append(mx)
        idxs.append(ix)
        cur = jnp.where(lane == ix, -3e38, cur)
    es = [jnp.exp(v - vals[0]) for v in vals]
    den = es[0] + es[1] + es[2] + es[3]
    idx_out = jnp.zeros(logits.shape, F32)
    gate_out = jnp.zeros(logits.shape, F32)
    for k in range(TOP_K):
        idx_out = jnp.where(lane == float(k), idxs[k], idx_out)
        gate_out = jnp.where(lane == float(k), es[k] / den, gate_out)
    idx_ref[...] = idx_out.astype(jnp.int32)
    gate_ref[...] = gate_out


def _mix(x3, og3, om3, mk3, mv3, pw, tm):
    B, S, _ = x3.shape
    hw = MEM_HEADS * MEM_HEAD_DIM
    tile = lambda n: pl.BlockSpec((None, tm, n), lambda b, i: (b, i, 0))
    memspec = pl.BlockSpec((None, N_MEM, hw), lambda b, i: (b, 0, 0))
    consts = [pw['w_out'], pw['nmem'], pw['w_mq'], pw['mqn'], pw['w_mo'], pw['nffn'], pw['wr_hi'], pw['wr_lo'],
              pw['b_r']]
    return pl.pallas_call(
        _mix_kernel,
        grid=(B, S // tm),
        in_specs=[tile(D_MODEL), tile(GDN_VW), tile(MLA_HEADS * V_DIM), memspec, memspec]
                 + [_const_spec(c.shape) for c in consts],
        out_specs=[tile(D_MODEL), tile(D_MODEL), tile(LANES), tile(LANES)],
        out_shape=[jax.ShapeDtypeStruct((B, S, D_MODEL), F32), jax.ShapeDtypeStruct((B, S, D_MODEL), BF16),
                   jax.ShapeDtypeStruct((B, S, LANES), jnp.int32), jax.ShapeDtypeStruct((B, S, LANES), F32)],
        compiler_params=_cparams(("parallel", "parallel")),
        name="mix_mem_router",
    )(x3, og3, om3, mk3, mv3, *consts)


def _expert_kernel(be_ref, nu_ref, rows_ref, wgu_ref, bgu_ref, wd_ref, bd_ref, y_ref):
    i = pl.program_id(0)

    @pl.when(i < nu_ref[0])
    def _():
        gu = _dot(rows_ref[...], wgu_ref[...]) + bgu_ref[...]
        gt = jnp.minimum(gu[:, :D_FF], SWIGLU_LIMIT)
        up = jnp.clip(gu[:, D_FF:], -SWIGLU_LIMIT, SWIGLU_LIMIT)
        act = gt * _sigmoid(SWIGLU_ALPHA * gt) * (up + 1.0)
        y_ref[...] = _dot(act.astype(BF16), wd_ref[...]) + bd_ref[...]

    @pl.when(i >= nu_ref[0])
    def _():
        y_ref[...] = jnp.zeros(y_ref.shape, F32)


def _experts(block_e, n_used, rows, w_gu, b_gu, w_down, b_down):
    n_rows = rows.shape[0]
    nb = n_rows // MOE_ROWS
    gs = pltpu.PrefetchScalarGridSpec(
        num_scalar_prefetch=2,
        grid=(nb,),
        in_specs=[pl.BlockSpec((MOE_ROWS, D_MODEL), lambda i, be, nu: (i, 0)),
                  pl.BlockSpec((None, D_MODEL, 2 * D_FF), lambda i, be, nu: (be[i], 0, 0)),
                  pl.BlockSpec((None, 1, 2 * D_FF), lambda i, be, nu: (be[i], 0, 0)),
                  pl.BlockSpec((None, D_FF, D_MODEL), lambda i, be, nu: (be[i], 0, 0)),
                  pl.BlockSpec((None, 1, D_MODEL), lambda i, be, nu: (be[i], 0, 0))],
        out_specs=pl.BlockSpec((MOE_ROWS, D_MODEL), lambda i, be, nu: (i, 0)),
    )
    return pl.pallas_call(
        _expert_kernel,
        grid_spec=gs,
        out_shape=jax.ShapeDtypeStruct((n_rows, D_MODEL), F32),
        compiler_params=_cparams(("arbitrary",)),
        name="moe_experts",
    )(block_e, n_used, rows, w_gu, b_gu, w_down, b_down)


def _moe(h3, idx, gate, x2, ew):
    T = h3.shape[0]
    n_assign = T * TOP_K
    flat_e = idx.reshape(-1)
    onehot = (flat_e[:, None] == jnp.arange(N_EXPERTS, dtype=jnp.int32)[None, :]).astype(jnp.int32)
    csum = jnp.cumsum(onehot, axis=0)
    rank = jnp.sum(csum * onehot, axis=1) - 1
    counts = csum[-1]
    padded = (counts + MOE_ROWS - 1) // MOE_ROWS * MOE_ROWS
    pad_end = jnp.cumsum(padded)
    pad_start = pad_end - padded
    pos = pad_start[flat_e] + rank
    nb = -(-n_assign // MOE_ROWS) + N_EXPERTS
    row_tok = jnp.full((nb * MOE_ROWS,), T, jnp.int32).at[pos].set(jnp.arange(n_assign, dtype=jnp.int32) // TOP_K)
    block_e = jnp.minimum(jnp.searchsorted(pad_end, jnp.arange(nb, dtype=jnp.int32) * MOE_ROWS, side='right'),
                          N_EXPERTS - 1).astype(jnp.int32)
    n_used = (pad_end[-1] // MOE_ROWS).astype(jnp.int32).reshape(1)
    h_pad = jnp.concatenate([h3, jnp.zeros((1, D_MODEL), h3.dtype)], axis=0)
    rows = h_pad[row_tok]
    y_rows = _experts(block_e, n_used, rows, ew['w_gu'], ew['b_gu'], ew['w_down'], ew['b_down'])
    yg = y_rows[pos].reshape(T, TOP_K, D_MODEL)
    return x2 + jnp.sum(yg * gate[:, :, None], axis=1)


def _pad_lanes(v, n=LANES, fill=0.0):
    return jnp.pad(v, (0, n - v.shape[0]), constant_values=fill).reshape(1, n)


def _prep_weights(norm_mix, w_in, conv_w, a_log, dt_bias, gdn_norm, q_a_norm, w_qb, kv_a_norm, w_kvb, q_norm,
                  k_nope_norm, k_rope_norm, w_out, norm_mem, mem_norm, w_mq, w_mk, w_mv, mq_norm, mk_norm, w_mo,
                  norm_ffn, w_router, b_router, w_gu, b_gu, w_down, b_down):
    c = np.cumsum([CONV_DIM, GDN_VW, GDN_HEADS, GDN_HEADS, Q_RANK, KV_RANK])
    w_u, w_z, w_a, w_b, w_cq, w_ckv, w_kpe = [w_in[:, lo:hi] for lo, hi in
                                              zip([0, *c], [*c, w_in.shape[1]])]
    w_s = jnp.concatenate([w_kpe, w_a, w_b], axis=1)
    w_s = jnp.pad(w_s, ((0, 0), (0, LANES - w_s.shape[1])))
    wq = w_qb.reshape(Q_RANK, MLA_HEADS, QK_DIM)
    wq = jnp.pad(wq, ((0, 0), (0, 0), (0, Q_SLAB - QK_DIM))).reshape(Q_RANK, MLA_HEADS * Q_SLAB)
    wkv = w_kvb.reshape(KV_RANK, MLA_HEADS, NOPE_DIM + V_DIM)
    wkv = jnp.concatenate([wkv[:, :, :NOPE_DIM].reshape(KV_RANK, -1), wkv[:, :, NOPE_DIM:].reshape(KV_RANK, -1)], 1)
    wr = jnp.pad(w_router, ((0, 0), (0, LANES - N_EXPERTS)))
    wr_hi = wr.astype(BF16)
    wr_lo = (wr - wr_hi.astype(F32)).astype(BF16)
    row = lambda v: v.reshape(1, -1)
    gpad = ROPE_DIM
    pw = dict(
        nmix=row(norm_mix), w_u=w_u.astype(BF16), w_z=w_z.astype(BF16), w_cq=w_cq.astype(BF16),
        w_ckv=w_ckv.astype(BF16), w_s=w_s.astype(BF16), qan=row(q_a_norm), w_qb=wq.astype(BF16),
        qn=_pad_lanes(q_norm, Q_SLAB), kvan=row(kv_a_norm), krn=_pad_lanes(k_rope_norm),
        alog=jnp.pad(a_log, (gpad, LANES - gpad - GDN_HEADS)).reshape(1, LANES),
        dtb=jnp.pad(dt_bias, (gpad, LANES - gpad - GDN_HEADS)).reshape(1, LANES),
        conv_w=conv_w, gnorm=row(gdn_norm), w_kvb=wkv.astype(BF16), knn=row(k_nope_norm),
        w_out=w_out.astype(BF16), nmem=row(norm_mem), w_mq=w_mq.astype(BF16), mqn=row(mq_norm),
        w_mo=w_mo.astype(BF16), nffn=row(norm_ffn), wr_hi=wr_hi, wr_lo=wr_lo,
        b_r=_pad_lanes(b_router, LANES, NEG_BIG),
        mnorm=row(mem_norm), w_mk=w_mk.astype(BF16), w_mv=w_mv.astype(BF16), mkn=row(mk_norm),
    )
    ew = dict(w_gu=w_gu.astype(BF16), b_gu=b_gu.reshape(N_EXPERTS, 1, 2 * D_FF), w_down=w_down.astype(BF16),
              b_down=b_down.reshape(N_EXPERTS, 1, D_MODEL))
    return pw, ew


def _rope_tables(P, S):
    half = ROPE_DIM // 2
    inv = ROPE_THETA ** (-jnp.arange(half, dtype=F32) / half)
    ang = (P + jnp.arange(S, dtype=jnp.int32)).astype(F32)[:, None] * inv[None, :]
    cos, sin = jnp.cos(ang), jnp.sin(ang)
    zh = jnp.zeros((S, half), F32)
    zz = jnp.zeros((S, LANES - ROPE_DIM), F32)
    return (jnp.concatenate([cos, cos, zz], 1), jnp.concatenate([-sin, zh, zz], 1),
            jnp.concatenate([zh, sin, zz], 1))


def _pick(n, prefs):
    for t in prefs:
        if n % t == 0:
            return t
    return n


def _trunk(x, lat_past, kpe_past, s0, conv_past, mem_k, mem_v, pw, ew):
    B, S, D = x.shape
    P = lat_past.shape[1]
    T = B * S
    tm = _pick(S, (512, 256, 128, 64))
    u, z, q, lat_new, small = _inproj(x.reshape(T, D), S, tm, pw, _rope_tables(P, S))

    LC = _pick(S, (256, 128, 64))
    o_gdn, s_new, conv_new = _gdn(u.reshape(B, S, CONV_DIM), small.reshape(B, S, LANES), z.reshape(B, S, GDN_VW),
                                  pw['conv_w'], conv_past, s0, pw['gnorm'], LC)

    sk = P + S
    tk = 512 if S >= 512 else -(-sk // LANES) * LANES
    skp = -(-sk // tk) * tk
    lat_all = jnp.concatenate([lat_past, lat_new.reshape(B, S, KV_RANK)], axis=1)
    kpe_new = small[:, :ROPE_DIM].reshape(B, S, ROPE_DIM)
    kpe_all = jnp.concatenate([kpe_past, kpe_new], axis=1)
    lat_all = jnp.pad(lat_all, ((0, 0), (0, skp - sk), (0, 0)))
    kpe_all = jnp.pad(kpe_all, ((0, 0), (0, skp - sk), (0, LANES - ROPE_DIM)))
    tkv = _pick(B * skp, (512, 256, 128, 64, 8))
    kn, kr, v = _kvproj(lat_all.reshape(B * skp, KV_RANK), kpe_all.reshape(B * skp, LANES), pw['w_kvb'], pw['knn'],
                        tkv)
    hw = MLA_HEADS * NOPE_DIM
    tq = _pick(S, (512, 256, 128, 64))
    o_mla = _flash(q.reshape(B, S, MLA_HEADS * Q_SLAB), kn.reshape(B, skp, hw), kr.reshape(B, skp, LANES),
                   v.reshape(B, skp, hw), P, S, tq, tk)

    x2, h3, idx, gate = _mix(x, o_gdn, o_mla, mem_k, mem_v, pw, tm)
    y = _moe(h3.reshape(T, D), idx.reshape(T, LANES)[:, :TOP_K], gate.reshape(T, LANES)[:, :TOP_K],
             x2.reshape(T, D), ew)
    return y.reshape(B, S, D), lat_new.reshape(B, S, KV_RANK), kpe_new, s_new, conv_new


def kernel(x_prompt, x_sample, cache_kv_latent, cache_k_rope, state_gdn, state_conv, cache_mem_k, cache_mem_v, mem_prompt, norm_mix, w_in, conv_w, a_log, dt_bias, gdn_norm, q_a_norm, w_qb, kv_a_norm, w_kvb, q_norm, k_nope_norm, k_rope_norm, w_out, norm_mem, mem_norm, w_mq, w_mk, w_mv, mq_norm, mk_norm, w_mo, norm_ffn, w_router, b_router, w_gu, b_gu, w_down, b_down):
    depth = norm_mix.shape[0]
    yp, ys = x_prompt, x_sample
    bp = x_prompt.shape[0]
    hw = MEM_HEADS * MEM_HEAD_DIM
    outs = [[] for _ in range(10)]
    for l in range(depth):
        pw, ew = _prep_weights(norm_mix[l], w_in[l], conv_w[l], a_log[l], dt_bias[l], gdn_norm[l], q_a_norm[l],
                               w_qb[l], kv_a_norm[l], w_kvb[l], q_norm[l], k_nope_norm[l], k_rope_norm[l], w_out[l],
                               norm_mem[l], mem_norm[l], w_mq[l], w_mk[l], w_mv[l], mq_norm[l], mk_norm[l], w_mo[l],
                               norm_ffn[l], w_router[l], b_router[l], w_gu[l], b_gu[l], w_down[l], b_down[l])
        nm = mem_prompt.shape[1]
        mk, mv = _memkv(mem_prompt.reshape(bp * nm, D_MODEL), pw['mnorm'], pw['w_mk'], pw['w_mv'], pw['mkn'],
                        _pick(bp * nm, (512, 256)))
        mk = mk.reshape(bp, nm, hw)
        mv = mv.reshape(bp, nm, hw)
        yp, lat, kpe, s_fin, cv = _trunk(
            yp, jnp.zeros((bp, 0, KV_RANK), F32), jnp.zeros((bp, 0, ROPE_DIM), F32),
            jnp.zeros((bp, GDN_HEADS, GDN_DK, GDN_DV), F32), jnp.zeros((bp, CONV_W - 1, CONV_DIM), F32), mk, mv,
            pw, ew)
        bs = x_sample.shape[0]
        ys, lat2, kpe2, s_fin2, cv2 = _trunk(
            ys, cache_kv_latent[l], cache_k_rope[l], state_gdn[l], state_conv[l],
            cache_mem_k[l].reshape(bs, nm, hw), cache_mem_v[l].reshape(bs, nm, hw), pw, ew)
        for lst, val in zip(outs, (lat, kpe, s_fin, cv, mk.reshape(bp, nm, MEM_HEADS, MEM_HEAD_DIM),
                                   mv.reshape(bp, nm, MEM_HEADS, MEM_HEAD_DIM), lat2, kpe2, s_fin2, cv2)):
            lst.append(val)
    return (yp, ys) + tuple(jnp.stack(o) for o in outs)
```

```python
import functools
import math

import numpy as np
import jax
import jax.numpy as jnp
from jax import lax
from jax.experimental import pallas as pl
from jax.experimental.pallas import tpu as pltpu

F32 = jnp.float32
BF16 = jnp.bfloat16

D_MODEL = 1024
CHUNK = 64
EPS = 1e-6
GDN_HEADS = 4
GDN_DK = 128
GDN_DV = 128
CONV_W = 4
GDN_QK = GDN_HEADS * GDN_DK
GDN_VW = GDN_HEADS * GDN_DV
CONV_DIM = 2 * GDN_QK + GDN_VW
MLA_HEADS = 4
Q_RANK = 384
KV_RANK = 256
NOPE_DIM = 128
ROPE_DIM = 64
V_DIM = 128
QK_DIM = NOPE_DIM + ROPE_DIM
ROPE_THETA = 10000.0
N_MEM = 256
MEM_HEADS = 4
MEM_HEAD_DIM = 128
N_EXPERTS = 32
TOP_K = 4
D_FF = D_MODEL
SWIGLU_ALPHA = 1.702
SWIGLU_LIMIT = 7.0

LANES = 128
Q_SLAB = 2 * LANES
NEG_BIG = -1e30
VMEM_LIMIT = 56 * 1024 * 1024
MOE_ROWS = 256
GDN_UNITS = 8


def _cparams(sem):
    return pltpu.CompilerParams(dimension_semantics=sem, vmem_limit_bytes=VMEM_LIMIT)


def _dot(a, b):
    return jnp.dot(a, b, preferred_element_type=F32)


def _dot_nt(a, b):
    return lax.dot_general(a, b, (((1,), (1,)), ((), ())), preferred_element_type=F32)


def _dot_tn(a, b):
    return lax.dot_general(a, b, (((0,), (0,)), ((), ())), preferred_element_type=F32)


def _rms(x, gain, n=None):
    n = x.shape[-1] if n is None else n
    ss = jnp.sum(x * x, axis=-1, keepdims=True) * (1.0 / n)
    return (x * lax.rsqrt(ss + EPS)) * gain


def _sigmoid(x):
    return 1.0 / (1.0 + jnp.exp(-x))


def _rope128(r, cos, sna, snb):
    return r * cos + pltpu.roll(r, 96, 1) * sna + pltpu.roll(r, 32, 1) * snb


def _const_spec(shape):
    nd = len(shape)
    return pl.BlockSpec(shape, lambda *_: (0,) * nd)


def _inproj_kernel(x_ref, nmix_ref, wu_ref, wz_ref, wcq_ref, wckv_ref, ws_ref, qan_ref, wqb_ref, qn_ref,
                   kvan_ref, krn_ref, alog_ref, dtb_ref, cos_ref, sna_ref, snb_ref,
                   u_ref, z_ref, q_ref, lat_ref, small_ref):
    x = x_ref[...]
    hb = _rms(x, nmix_ref[...]).astype(BF16)
    u_ref[...] = _dot(hb, wu_ref[...])
    z_ref[...] = _dot(hb, wz_ref[...])
    cos, sna, snb = cos_ref[...], sna_ref[...], snb_ref[...]

    cq = _rms(_dot(hb, wcq_ref[...]), qan_ref[...]).astype(BF16)
    qf = _dot(cq, wqb_ref[...])
    scale = QK_DIM ** -0.5
    for h in range(MLA_HEADS):
        slab = qf[:, h * Q_SLAB:(h + 1) * Q_SLAB]
        slab = _rms(slab, qn_ref[...], n=QK_DIM)
        nope = slab[:, :LANES]
        ropd = _rope128(slab[:, LANES:], cos, sna, snb)
        q_ref[:, h * Q_SLAB:h * Q_SLAB + LANES] = (nope * scale).astype(BF16)
        q_ref[:, h * Q_SLAB + LANES:(h + 1) * Q_SLAB] = (ropd * scale).astype(BF16)

    lat_ref[...] = _rms(_dot(hb, wckv_ref[...]), kvan_ref[...])

    sm = _dot(hb, ws_ref[...])
    lane = lax.broadcasted_iota(jnp.int32, sm.shape, 1)
    kp = jnp.where(lane < ROPE_DIM, sm, 0.0)
    kpe = _rope128(_rms(kp, krn_ref[...], n=ROPE_DIM), cos, sna, snb)
    sp = sm + dtb_ref[...]
    softplus = jnp.maximum(sp, 0.0) + jnp.log1p(jnp.exp(-jnp.abs(sp)))
    g = -jnp.exp(alog_ref[...]) * softplus
    beta = _sigmoid(sm)
    small_ref[...] = jnp.where(lane < ROPE_DIM, kpe,
                               jnp.where(lane < ROPE_DIM + GDN_HEADS, g,
                                         jnp.where(lane < ROPE_DIM + 2 * GDN_HEADS, beta, 0.0)))


def _inproj(x2d, S, tm, pw, tabs):
    T = x2d.shape[0]
    nblk_s = S // tm
    row = lambda n: pl.BlockSpec((tm, n), lambda i: (i, 0))
    tab = pl.BlockSpec((tm, LANES), lambda i: (i % nblk_s, 0))
    consts = [pw['nmix'], pw['w_u'], pw['w_z'], pw['w_cq'], pw['w_ckv'], pw['w_s'], pw['qan'], pw['w_qb'],
              pw['qn'], pw['kvan'], pw['krn'], pw['alog'], pw['dtb']]
    return pl.pallas_call(
        _inproj_kernel,
        grid=(T // tm,),
        in_specs=[row(D_MODEL)] + [_const_spec(c.shape) for c in consts] + [tab, tab, tab],
        out_specs=[row(CONV_DIM), row(GDN_VW), row(MLA_HEADS * Q_SLAB), row(KV_RANK), row(LANES)],
        out_shape=[jax.ShapeDtypeStruct((T, CONV_DIM), F32), jax.ShapeDtypeStruct((T, GDN_VW), F32),
                   jax.ShapeDtypeStruct((T, MLA_HEADS * Q_SLAB), BF16), jax.ShapeDtypeStruct((T, KV_RANK), F32),
                   jax.ShapeDtypeStruct((T, LANES), F32)],
        compiler_params=_cparams(("parallel",)),
        name="inproj",
    )(x2d, *consts, *tabs)


def _split3(x):
    hi = x.astype(BF16)
    r = x - hi.astype(F32)
    mid = r.astype(BF16)
    lo = (r - mid.astype(F32)).astype(BF16)
    return hi, mid, lo


def _gdn_kernel(u_ref, small_ref, z_ref, convw_ref, cpast_ref, s0_ref, gnorm_ref,
                to_ref, trilm_ref, strictm_ref, same2_ref, lvl_ref,
                o_ref, sfin_ref, cnew_ref, ext_ref, uc_ref, state_ref,
                qf_ref, kf_ref, vb_ref, bt_ref, gg_ref, gcum_ref, glast_ref, kk_ref, qk_ref, mb_ref, x_ref, qkb_ref,
                kbe_ref, qg_ref, kdec_ref, egl_ref, t1_ref, uu_ref, ww_ref, vn_ref, qs_ref, *, NB, LC):
    j = pl.program_id(1)
    nj = pl.num_programs(1)
    PADR = 8
    C = LC // CHUNK
    U = NB * C
    HR = GDN_HEADS * CHUNK

    @pl.when(j == 0)
    def _():
        state_ref[...] = s0_ref[...]
        ext_ref[:, PADR - (CONV_W - 1):PADR, :] = cpast_ref[...]

    w = convw_ref[...]
    for nb in range(NB):
        ext_ref[nb, PADR:PADR + LC, :] = u_ref[nb]
        acc = ext_ref[nb, PADR:PADR + LC, :] * w[CONV_W - 1:CONV_W, :]
        for t in range(1, CONV_W):
            acc = acc + ext_ref[nb, PADR - t:PADR - t + LC, :] * w[CONV_W - 1 - t:CONV_W - t, :]
        uc_ref[nb] = acc * _sigmoid(acc)
        ext_ref[nb, 0:PADR, :] = ext_ref[nb, LC:LC + PADR, :]

    @pl.when(j == nj - 1)
    def _():
        cnew_ref[...] = ext_ref[:, PADR - (CONV_W - 1):PADR, :]

    units = [(nb, c) for nb in range(NB) for c in range(C)]
    g0 = ROPE_DIM
    b0 = ROPE_DIM + GDN_HEADS
    for u, (nb, c) in enumerate(units):
        rows = slice(c * CHUNK, (c + 1) * CHUNK)
        sm = small_ref[nb, rows, :]
        for h in range(GDN_HEADS):
            hr = slice(h * CHUNK, (h + 1) * CHUNK)
            q = uc_ref[nb, rows, h * GDN_DK:(h + 1) * GDN_DK]
            k = uc_ref[nb, rows, GDN_QK + h * GDN_DK:GDN_QK + (h + 1) * GDN_DK]
            v = uc_ref[nb, rows, 2 * GDN_QK + h * GDN_DV:2 * GDN_QK + (h + 1) * GDN_DV]
            beta = jnp.broadcast_to(sm[:, b0 + h:b0 + h + 1], (CHUNK, LANES))
            qf_ref[u, hr, :] = (q * lax.rsqrt(jnp.sum(q * q, -1, keepdims=True) + EPS)) * (GDN_DK ** -0.5)
            kf_ref[u, hr, :] = k * lax.rsqrt(jnp.sum(k * k, -1, keepdims=True) + EPS)
            vb_ref[u, hr, :] = (v * beta).astype(BF16)
            bt_ref[u, hr, :] = beta
            gg_ref[u, hr, :] = jnp.broadcast_to(sm[:, g0 + h:g0 + h + 1], (CHUNK, LANES))

    to = to_ref[...]
    for u in range(U):
        gl = sum(_dot(to, part) for part in _split3(gg_ref[u]))
        gcum_ref[u] = gl[:HR, :]
        glast_ref[u] = gl[HR:, :]
        k = kf_ref[u]
        kbf = k.astype(BF16)
        kk_ref[u] = _dot_nt((k * bt_ref[u]).astype(BF16), kbf)
        qk_ref[u] = _dot_nt(qf_ref[u].astype(BF16), kbf)

    tril = trilm_ref[...] > 0.0
    strict = strictm_ref[...] > 0.0
    same2 = same2_ref[...] > 0.0
    eye = trilm_ref[...] - strictm_ref[...]
    for u in range(U):
        gcum = gcum_ref[u]
        grow = gcum.T[0:1, :]
        gcol = jnp.concatenate([gcum, gcum], axis=1)
        decay = jnp.where(tril, jnp.exp(jnp.where(tril, gcol - grow, 0.0)), 0.0)
        m = jnp.where(strict, kk_ref[u] * decay, 0.0)
        mb_ref[u] = m.astype(BF16)
        x_ref[u] = eye - jnp.where(same2, m, 0.0)
        qkb_ref[u] = jnp.where(tril, qk_ref[u] * decay, 0.0).astype(BF16)
        egc = jnp.exp(gcum)
        k = kf_ref[u]
        kbe_ref[u] = (k * bt_ref[u] * egc).astype(BF16)
        qg_ref[u] = (qf_ref[u] * egc).astype(BF16)
        kdec_ref[u] = (k * jnp.exp(glast_ref[u] - gcum)).astype(BF16)
        egl_ref[u] = jnp.exp(glast_ref[u])

    for lvl in range(lvl_ref.shape[0]):
        lm = lvl_ref[lvl]
        for u in range(U):
            t1_ref[u] = _dot(mb_ref[u] * lm, x_ref[u].astype(BF16)).astype(BF16)
        for u in range(U):
            x = x_ref[u]
            x_ref[u] = x - _dot(x.astype(BF16), t1_ref[u])

    for u in range(U):
        xb = x_ref[u].astype(BF16)
        uu_ref[u] = _dot(xb, vb_ref[u])
        ww_ref[u] = _dot(xb, kbe_ref[u]).astype(BF16)

    gnorm = gnorm_ref[...]
    for c in range(C):
        for nb in range(NB):
            u = nb * C + c
            rows = slice(c * CHUNK, (c + 1) * CHUNK)
            for h in range(GDN_HEADS):
                hr = slice(h * CHUNK, (h + 1) * CHUNK)
                stb = state_ref[nb, h].astype(BF16)
                r = _dot(jnp.concatenate([ww_ref[u, hr, :], qg_ref[u, hr, :]], axis=0), stb)
                vn_ref[u, hr, :] = (uu_ref[u, hr, :] - r[:CHUNK]).astype(BF16)
                qs_ref[u, hr, :] = r[CHUNK:]
            out = qs_ref[u] + _dot(qkb_ref[u], vn_ref[u])
            for h in range(GDN_HEADS):
                hr = slice(h * CHUNK, (h + 1) * CHUNK)
                state_ref[nb, h] = (state_ref[nb, h] * egl_ref[u, h * CHUNK:h * CHUNK + 1, :]
                                    + _dot_tn(kdec_ref[u, hr, :], vn_ref[u, hr, :]))
                zz = z_ref[nb, rows, h * GDN_DV:(h + 1) * GDN_DV]
                og = _rms(out[hr, :], gnorm) * (zz * _sigmoid(zz))
                o_ref[nb, rows, h * GDN_DV:(h + 1) * GDN_DV] = og.astype(BF16)

    @pl.when(j == nj - 1)
    def _():
        sfin_ref[...] = state_ref[...]


def _gdn_masks():
    hr = GDN_HEADS * CHUNK
    i = np.arange(hr)[:, None]
    j = np.arange(hr)[None, :]
    same_head = (i // CHUNK) == (j // CHUNK)
    tril = same_head & (i >= j)
    strict = same_head & (i > j)
    same2 = strict & ((i // 2) == (j // 2))
    lvls = []
    blk = 2
    while blk < CHUNK:
        lvls.append(strict & ((i // (2 * blk)) == (j // (2 * blk))) & ((i // blk) != (j // blk)))
        blk *= 2
    to = np.concatenate([tril, same_head], axis=0)
    f = lambda a: jnp.asarray(a.astype(np.float32))
    return (jnp.asarray(to.astype(np.float32), dtype=BF16), f(tril), f(strict), f(same2),
            jnp.asarray(np.stack(lvls).astype(np.float32), dtype=BF16))


def _gdn(u3, small3, z3, conv_w, conv_past, s0, gnorm, NB, LC):
    B, S, _ = u3.shape
    C = LC // CHUNK
    U = NB * C
    HR = GDN_HEADS * CHUNK
    masks = _gdn_masks()
    tile = lambda n: pl.BlockSpec((NB, LC, n), lambda b, j: (b, j, 0))
    stspec = pl.BlockSpec((NB, GDN_HEADS, GDN_DK, GDN_DV), lambda b, j: (b, 0, 0, 0))
    cvspec = pl.BlockSpec((NB, CONV_W - 1, CONV_DIM), lambda b, j: (b, 0, 0))
    vm = lambda shape, dt: pltpu.VMEM(shape, dt)
    return pl.pallas_call(
        functools.partial(_gdn_kernel, NB=NB, LC=LC),
        grid=(B // NB, S // LC),
        in_specs=[tile(CONV_DIM), tile(LANES), tile(GDN_VW), _const_spec(conv_w.shape), cvspec, stspec,
                  _const_spec(gnorm.shape)] + [_const_spec(m.shape) for m in masks],
        out_specs=[tile(GDN_VW), stspec, cvspec],
        out_shape=[jax.ShapeDtypeStruct((B, S, GDN_VW), BF16),
                   jax.ShapeDtypeStruct((B, GDN_HEADS, GDN_DK, GDN_DV), F32),
                   jax.ShapeDtypeStruct((B, CONV_W - 1, CONV_DIM), F32)],
        scratch_shapes=[vm((NB, LC + 8, CONV_DIM), F32), vm((NB, LC, CONV_DIM), F32),
                        vm((NB, GDN_HEADS, GDN_DK, GDN_DV), F32),
                        vm((U, HR, LANES), F32), vm((U, HR, LANES), F32), vm((U, HR, LANES), BF16),
                        vm((U, HR, LANES), F32), vm((U, HR, LANES), F32),
                        vm((U, HR, LANES), F32), vm((U, HR, LANES), F32),
                        vm((U, HR, HR), F32), vm((U, HR, HR), F32),
                        vm((U, HR, HR), BF16), vm((U, HR, HR), F32), vm((U, HR, HR), BF16),
                        vm((U, HR, LANES), BF16), vm((U, HR, LANES), BF16), vm((U, HR, LANES), BF16),
                        vm((U, HR, LANES), F32), vm((U, HR, HR), BF16),
                        vm((U, HR, LANES), F32), vm((U, HR, LANES), BF16),
                        vm((U, HR, LANES), BF16), vm((U, HR, LANES), F32)],
        compiler_params=_cparams(("parallel", "arbitrary")),
        name="gdn",
    )(u3, small3, z3, conv_w, conv_past, s0, gnorm, *masks)


def _kvproj_kernel(lat_ref, kpe_ref, wkvb_ref, knn_ref, kn_ref, kr_ref, v_ref):
    kv = _dot(lat_ref[...].astype(BF16), wkvb_ref[...])
    hw = MLA_HEADS * NOPE_DIM
    for h in range(MLA_HEADS):
        kh = kv[:, h * NOPE_DIM:(h + 1) * NOPE_DIM]
        kn_ref[:, h * NOPE_DIM:(h + 1) * NOPE_DIM] = _rms(kh, knn_ref[...]).astype(BF16)
    v_ref[...] = kv[:, hw:].astype(BF16)
    kp = kpe_ref[...]
    lane = lax.broadcasted_iota(jnp.int32, kp.shape, 1)
    kr_ref[...] = jnp.where(lane < ROPE_DIM, kp, 0.0).astype(BF16)


def _kvproj(lat2d, kpe2d, w_kvb, knn, tm):
    T = lat2d.shape[0]
    row = lambda n: pl.BlockSpec((tm, n), lambda i: (i, 0))
    return pl.pallas_call(
        _kvproj_kernel,
        grid=(T // tm,),
        in_specs=[row(KV_RANK), row(LANES), _const_spec(w_kvb.shape), _const_spec(knn.shape)],
        out_specs=[row(MLA_HEADS * NOPE_DIM), row(LANES), row(MLA_HEADS * V_DIM)],
        out_shape=[jax.ShapeDtypeStruct((T, MLA_HEADS * NOPE_DIM), BF16), jax.ShapeDtypeStruct((T, LANES), BF16),
                   jax.ShapeDtypeStruct((T, MLA_HEADS * V_DIM), BF16)],
        compiler_params=_cparams(("parallel",)),
        name="kvproj",
    )(lat2d, kpe2d, w_kvb, knn)


def _last_kblock(qi, tq, tk, P, nk):
    last_key = ((P + qi * tq + tq - 1) // CHUNK) * CHUNK + CHUNK - 1
    return jnp.minimum(last_key // tk, nk - 1)


def _flash_kernel(q_ref, kn_ref, kr_ref, v_ref, o_ref, m_ref, l_ref, acc_ref, *, tq, tk, P, S, nk):
    qi = pl.program_id(1)
    ki = pl.program_id(2)

    @pl.when(ki == 0)
    def _():
        m_ref[...] = jnp.full(m_ref.shape, NEG_BIG, F32)
        l_ref[...] = jnp.zeros(l_ref.shape, F32)
        acc_ref[...] = jnp.zeros(acc_ref.shape, F32)

    @pl.when(ki <= _last_kblock(qi, tq, tk, P, nk))
    def _():
        qpos = P + qi * tq + lax.broadcasted_iota(jnp.int32, (tq, tk), 0)
        kpos = ki * tk + lax.broadcasted_iota(jnp.int32, (tq, tk), 1)
        mask = ((kpos // CHUNK) <= (qpos // CHUNK)) & (kpos < P + S)
        kr = kr_ref[...]
        for h in range(MLA_HEADS):
            qh = q_ref[:, h * Q_SLAB:(h + 1) * Q_SLAB]
            kh = jnp.concatenate([kn_ref[:, h * NOPE_DIM:(h + 1) * NOPE_DIM], kr], axis=1)
            s = jnp.where(mask, _dot_nt(qh, kh), NEG_BIG)
            m_prev = m_ref[h]
            m_new = jnp.maximum(m_prev, jnp.max(s, axis=-1, keepdims=True))
            alpha = jnp.exp(m_prev - m_new)
            p = jnp.exp(s - m_new)
            l_ref[h] = alpha * l_ref[h] + jnp.sum(p, axis=-1, keepdims=True)
            acc_ref[h] = alpha * acc_ref[h] + _dot(p.astype(BF16), v_ref[:, h * V_DIM:(h + 1) * V_DIM])
            m_ref[h] = m_new

    @pl.when(ki == nk - 1)
    def _():
        for h in range(MLA_HEADS):
            o_ref[:, h * V_DIM:(h + 1) * V_DIM] = (acc_ref[h] / l_ref[h]).astype(BF16)


def _flash(q3, kn3, kr3, v3, P, S, tq, tk):
    B = q3.shape[0]
    skp = kn3.shape[1]
    nk = skp // tk
    kmap = lambda b, qi, ki: (b, jnp.minimum(ki, _last_kblock(qi, tq, tk, P, nk)), 0)
    return pl.pallas_call(
        functools.partial(_flash_kernel, tq=tq, tk=tk, P=P, S=S, nk=nk),
        grid=(B, S // tq, nk),
        in_specs=[pl.BlockSpec((None, tq, MLA_HEADS * Q_SLAB), lambda b, qi, ki: (b, qi, 0)),
                  pl.BlockSpec((None, tk, MLA_HEADS * NOPE_DIM), kmap),
                  pl.BlockSpec((None, tk, LANES), kmap),
                  pl.BlockSpec((None, tk, MLA_HEADS * V_DIM), kmap)],
        out_specs=pl.BlockSpec((None, tq, MLA_HEADS * V_DIM), lambda b, qi, ki: (b, qi, 0)),
        out_shape=jax.ShapeDtypeStruct((B, S, MLA_HEADS * V_DIM), BF16),
        scratch_shapes=[pltpu.VMEM((MLA_HEADS, tq, 1), F32), pltpu.VMEM((MLA_HEADS, tq, 1), F32),
                        pltpu.VMEM((MLA_HEADS, tq, V_DIM), F32)],
        compiler_params=_cparams(("parallel", "parallel", "arbitrary")),
        name="mla_attn",
    )(q3, kn3, kr3, v3)


def _memkv_kernel(mem_ref, mnorm_ref, wmk_ref, wmv_ref, mkn_ref, k_ref, v_ref):
    mb = _rms(mem_ref[...], mnorm_ref[...]).astype(BF16)
    kf = _dot(mb, wmk_ref[...])
    for h in range(MEM_HEADS):
        sl = slice(h * MEM_HEAD_DIM, (h + 1) * MEM_HEAD_DIM)
        k_ref[:, sl] = _rms(kf[:, sl], mkn_ref[...])
    v_ref[...] = _dot(mb, wmv_ref[...])


def _memkv(mem2d, mnorm, w_mk, w_mv, mkn, tm):
    T = mem2d.shape[0]
    hw = MEM_HEADS * MEM_HEAD_DIM
    row = lambda n: pl.BlockSpec((tm, n), lambda i: (i, 0))
    return pl.pallas_call(
        _memkv_kernel,
        grid=(T // tm,),
        in_specs=[row(D_MODEL), _const_spec(mnorm.shape), _const_spec(w_mk.shape), _const_spec(w_mv.shape),
                  _const_spec(mkn.shape)],
        out_specs=[row(hw), row(hw)],
        out_shape=[jax.ShapeDtypeStruct((T, hw), F32), jax.ShapeDtypeStruct((T, hw), F32)],
        compiler_params=_cparams(("parallel",)),
        name="mem_kv",
    )(mem2d, mnorm, w_mk, w_mv, mkn)


def _mix_kernel(x_ref, og_ref, om_ref, mk_ref, mv_ref, wout_ref, nmem_ref, wmq_ref, mqn_ref, wmo_ref, nffn_ref,
                wrh_ref, wrl_ref, br_ref, x2_ref, h3_ref, idx_ref, gate_ref):
    x1 = x_ref[...] + _dot(og_ref[...], wout_ref[0:GDN_VW, :]) + _dot(om_ref[...], wout_ref[GDN_VW:, :])
    hb = _rms(x1, nmem_ref[...]).astype(BF16)
    qm = _dot(hb, wmq_ref[...])
    heads = []
    for h in range(MEM_HEADS):
        sl = slice(h * MEM_HEAD_DIM, (h + 1) * MEM_HEAD_DIM)
        qh = (_rms(qm[:, sl], mqn_ref[...]) * (MEM_HEAD_DIM ** -0.5)).astype(BF16)
        s = _dot_nt(qh, mk_ref[:, sl].astype(BF16))
        p = jnp.exp(s - jnp.max(s, axis=-1, keepdims=True))
        p = p / jnp.sum(p, axis=-1, keepdims=True)
        heads.append(_dot(p.astype(BF16), mv_ref[:, sl].astype(BF16)).astype(BF16))
    x2 = x1 + _dot(jnp.concatenate(heads, axis=1), wmo_ref[...])
    x2_ref[...] = x2
    h3 = _rms(x2, nffn_ref[...])
    hi = h3.astype(BF16)
    h3_ref[...] = hi
    lo = (h3 - hi.astype(F32)).astype(BF16)
    wrh = wrh_ref[...]
    logits = _dot(hi, wrh) + _dot(lo, wrh) + _dot(hi, wrl_ref[...]) + br_ref[...]

    lane = lax.broadcasted_iota(jnp.int32, logits.shape, 1).astype(F32)
    vals, idxs = [], []
    cur = logits
    for _ in range(TOP_K):
        mx = jnp.max(cur, axis=-1, keepdims=True)
        ix = jnp.min(jnp.where(cur == mx, lane, float(LANES)), axis=-1, keepdims=True)
        vals.append(mx)
        idxs.append(ix)
        cur = jnp.where(lane == ix, -3e38, cur)
    es = [jnp.exp(v - vals[0]) for v in vals]
    den = es[0] + es[1] + es[2] + es[3]
    idx_out = jnp.zeros(logits.shape, F32)
    gate_out = jnp.zeros(logits.shape, F32)
    for k in range(TOP_K):
        idx_out = jnp.where(lane == float(k), idxs[k], idx_out)
        gate_out = jnp.where(lane == float(k), es[k] / den, gate_out)
    idx_ref[...] = idx_out.astype(jnp.int32)
    gate_ref[...] = gate_out


def _mix(x3, og3, om3, mk3, mv3, pw, tm):
    B, S, _ = x3.shape
    hw = MEM_HEADS * MEM_HEAD_DIM
    tile = lambda n: pl.BlockSpec((None, tm, n), lambda b, i: (b, i, 0))
    memspec = pl.BlockSpec((None, N_MEM, hw), lambda b, i: (b, 0, 0))
    consts = [pw['w_out'], pw['nmem'], pw['w_mq'], pw['mqn'], pw['w_mo'], pw['nffn'], pw['wr_hi'], pw['wr_lo'],
              pw['b_r']]
    return pl.pallas_call(
        _mix_kernel,
        grid=(B, S // tm),
        in_specs=[tile(D_MODEL), tile(GDN_VW), tile(MLA_HEADS * V_DIM), memspec, memspec]
                 + [_const_spec(c.shape) for c in consts],
        out_specs=[tile(D_MODEL), tile(D_MODEL), tile(LANES), tile(LANES)],
        out_shape=[jax.ShapeDtypeStruct((B, S, D_MODEL), F32), jax.ShapeDtypeStruct((B, S, D_MODEL), BF16),
                   jax.ShapeDtypeStruct((B, S, LANES), jnp.int32), jax.ShapeDtypeStruct((B, S, LANES), F32)],
        compiler_params=_cparams(("parallel", "parallel")),
        name="mix_mem_router",
    )(x3, og3, om3, mk3, mv3, *consts)


def _expert_kernel(be_ref, nu_ref, rows_ref, wgu_ref, bgu_ref, wd_ref, bd_ref, y_ref):
    i = pl.program_id(0)

    @pl.when(i < nu_ref[0])
    def _():
        gu = _dot(rows_ref[...], wgu_ref[...]) + bgu_ref[...]
        gt = jnp.minimum(gu[:, :D_FF], SWIGLU_LIMIT)
        up = jnp.clip(gu[:, D_FF:], -SWIGLU_LIMIT, SWIGLU_LIMIT)
        act = gt * _sigmoid(SWIGLU_ALPHA * gt) * (up + 1.0)
        y_ref[...] = _dot(act.astype(BF16), wd_ref[...]) + bd_ref[...]

    @pl.when(i >= nu_ref[0])
    def _():
        y_ref[...] = jnp.zeros(y_ref.shape, F32)


def _experts(block_e, n_used, rows, w_gu, b_gu, w_down, b_down):
    n_rows = rows.shape[0]
    nb = n_rows // MOE_ROWS
    gs = pltpu.PrefetchScalarGridSpec(
        num_scalar_prefetch=2,
        grid=(nb,),
        in_specs=[pl.BlockSpec((MOE_ROWS, D_MODEL), lambda i, be, nu: (i, 0)),
                  pl.BlockSpec((None, D_MODEL, 2 * D_FF), lambda i, be, nu: (be[i], 0, 0)),
                  pl.BlockSpec((None, 1, 2 * D_FF), lambda i, be, nu: (be[i], 0, 0)),
                  pl.BlockSpec((None, D_FF, D_MODEL), lambda i, be, nu: (be[i], 0, 0)),
                  pl.BlockSpec((None, 1, D_MODEL), lambda i, be, nu: (be[i], 0, 0))],
        out_specs=pl.BlockSpec((MOE_ROWS, D_MODEL), lambda i, be, nu: (i, 0)),
    )
    return pl.pallas_call(
        _expert_kernel,
        grid_spec=gs,
        out_shape=jax.ShapeDtypeStruct((n_rows, D_MODEL), F32),
        compiler_params=_cparams(("arbitrary",)),
        name="moe_experts",
    )(block_e, n_used, rows, w_gu, b_gu, w_down, b_down)


def _moe(h3, idx, gate, x2, ew):
    T = h3.shape[0]
    n_assign = T * TOP_K
    flat_e = idx.reshape(-1)
    onehot = (flat_e[:, None] == jnp.arange(N_EXPERTS, dtype=jnp.int32)[None, :]).astype(jnp.int32)
    csum = jnp.cumsum(onehot, axis=0)
    rank = jnp.sum(csum * onehot, axis=1) - 1
    counts = csum[-1]
    padded = (counts + MOE_ROWS - 1) // MOE_ROWS * MOE_ROWS
    pad_end = jnp.cumsum(padded)
    pad_start = pad_end - padded
    pos = pad_start[flat_e] + rank
    nb = -(-n_assign // MOE_ROWS) + N_EXPERTS
    row_tok = jnp.full((nb * MOE_ROWS,), T, jnp.int32).at[pos].set(jnp.arange(n_assign, dtype=jnp.int32) // TOP_K)
    block_e = jnp.minimum(jnp.searchsorted(pad_end, jnp.arange(nb, dtype=jnp.int32) * MOE_ROWS, side='right'),
                          N_EXPERTS - 1).astype(jnp.int32)
    n_used = (pad_end[-1] // MOE_ROWS).astype(jnp.int32).reshape(1)
    h_pad = jnp.concatenate([h3, jnp.zeros((1, D_MODEL), h3.dtype)], axis=0)
    rows = h_pad[row_tok]
    y_rows = _experts(block_e, n_used, rows, ew['w_gu'], ew['b_gu'], ew['w_down'], ew['b_down'])
    yg = y_rows[pos].reshape(T, TOP_K, D_MODEL)
    return x2 + jnp.sum(yg * gate[:, :, None], axis=1)


def _pad_lanes(v, n=LANES, fill=0.0):
    return jnp.pad(v, (0, n - v.shape[0]), constant_values=fill).reshape(1, n)


def _prep_weights(norm_mix, w_in, conv_w, a_log, dt_bias, gdn_norm, q_a_norm, w_qb, kv_a_norm, w_kvb, q_norm,
                  k_nope_norm, k_rope_norm, w_out, norm_mem, mem_norm, w_mq, w_mk, w_mv, mq_norm, mk_norm, w_mo,
                  norm_ffn, w_router, b_router, w_gu, b_gu, w_down, b_down):
    c = np.cumsum([CONV_DIM, GDN_VW, GDN_HEADS, GDN_HEADS, Q_RANK, KV_RANK])
    w_u, w_z, w_a, w_b, w_cq, w_ckv, w_kpe = [w_in[:, lo:hi] for lo, hi in
                                              zip([0, *c], [*c, w_in.shape[1]])]
    w_s = jnp.concatenate([w_kpe, w_a, w_b], axis=1)
    w_s = jnp.pad(w_s, ((0, 0), (0, LANES - w_s.shape[1])))
    wq = w_qb.reshape(Q_RANK, MLA_HEADS, QK_DIM)
    wq = jnp.pad(wq, ((0, 0), (0, 0), (0, Q_SLAB - QK_DIM))).reshape(Q_RANK, MLA_HEADS * Q_SLAB)
    wkv = w_kvb.reshape(KV_RANK, MLA_HEADS, NOPE_DIM + V_DIM)
    wkv = jnp.concatenate([wkv[:, :, :NOPE_DIM].reshape(KV_RANK, -1), wkv[:, :, NOPE_DIM:].reshape(KV_RANK, -1)], 1)
    wr = jnp.pad(w_router, ((0, 0), (0, LANES - N_EXPERTS)))
    wr_hi = wr.astype(BF16)
    wr_lo = (wr - wr_hi.astype(F32)).astype(BF16)
    row = lambda v: v.reshape(1, -1)
    gpad = ROPE_DIM
    pw = dict(
        nmix=row(norm_mix), w_u=w_u.astype(BF16), w_z=w_z.astype(BF16), w_cq=w_cq.astype(BF16),
        w_ckv=w_ckv.astype(BF16), w_s=w_s.astype(BF16), qan=row(q_a_norm), w_qb=wq.astype(BF16),
        qn=_pad_lanes(q_norm, Q_SLAB), kvan=row(kv_a_norm), krn=_pad_lanes(k_rope_norm),
        alog=jnp.pad(a_log, (gpad, LANES - gpad - GDN_HEADS)).reshape(1, LANES),
        dtb=jnp.pad(dt_bias, (gpad, LANES - gpad - GDN_HEADS)).reshape(1, LANES),
        conv_w=conv_w, gnorm=row(gdn_norm), w_kvb=wkv.astype(BF16), knn=row(k_nope_norm),
        w_out=w_out.astype(BF16), nmem=row(norm_mem), w_mq=w_mq.astype(BF16), mqn=row(mq_norm),
        w_mo=w_mo.astype(BF16), nffn=row(norm_ffn), wr_hi=wr_hi, wr_lo=wr_lo,
        b_r=_pad_lanes(b_router, LANES, NEG_BIG),
        mnorm=row(mem_norm), w_mk=w_mk.astype(BF16), w_mv=w_mv.astype(BF16), mkn=row(mk_norm),
    )
    ew = dict(w_gu=w_gu.astype(BF16), b_gu=b_gu.reshape(N_EXPERTS, 1, 2 * D_FF), w_down=w_down.astype(BF16),
              b_down=b_down.reshape(N_EXPERTS, 1, D_MODEL))
    return pw, ew


def _rope_tables(P, S):
    half = ROPE_DIM // 2
    inv = ROPE_THETA ** (-jnp.arange(half, dtype=F32) / half)
    ang = (P + jnp.arange(S, dtype=jnp.int32)).astype(F32)[:, None] * inv[None, :]
    cos, sin = jnp.cos(ang), jnp.sin(ang)
    zh = jnp.zeros((S, half), F32)
    zz = jnp.zeros((S, LANES - ROPE_DIM), F32)
    return (jnp.concatenate([cos, cos, zz], 1), jnp.concatenate([-sin, zh, zz], 1),
            jnp.concatenate([zh, sin, zz], 1))


def _pick(n, prefs):
    for t in prefs:
        if n % t == 0:
            return t
    return n


def _trunk(x, lat_past, kpe_past, s0, conv_past, mem_k, mem_v, pw, ew):
    B, S, D = x.shape
    P = lat_past.shape[1]
    T = B * S
    tm = _pick(S, (512, 256, 128, 64))
    u, z, q, lat_new, small = _inproj(x.reshape(T, D), S, tm, pw, _rope_tables(P, S))

    LC = _pick(S, (256, 128, 64))
    NB = _pick(B, tuple(n for n in (8, 4, 2, 1) if n * (LC // CHUNK) <= GDN_UNITS))
    o_gdn, s_new, conv_new = _gdn(u.reshape(B, S, CONV_DIM), small.reshape(B, S, LANES), z.reshape(B, S, GDN_VW),
                                  pw['conv_w'], conv_past, s0, pw['gnorm'], NB, LC)

    sk = P + S
    tk = 512 if S >= 512 else -(-sk // LANES) * LANES
    skp = -(-sk // tk) * tk
    lat_all = jnp.concatenate([lat_past, lat_new.reshape(B, S, KV_RANK)], axis=1)
    kpe_new = small[:, :ROPE_DIM].reshape(B, S, ROPE_DIM)
    kpe_all = jnp.concatenate([kpe_past, kpe_new], axis=1)
    lat_all = jnp.pad(lat_all, ((0, 0), (0, skp - sk), (0, 0)))
    kpe_all = jnp.pad(kpe_all, ((0, 0), (0, skp - sk), (0, LANES - ROPE_DIM)))
    tkv = _pick(B * skp, (512, 256, 128, 64, 8))
    kn, kr, v = _kvproj(lat_all.reshape(B * skp, KV_RANK), kpe_all.reshape(B * skp, LANES), pw['w_kvb'], pw['knn'],
                        tkv)
    hw = MLA_HEADS * NOPE_DIM
    tq = _pick(S, (512, 256, 128, 64))
    o_mla = _flash(q.reshape(B, S, MLA_HEADS * Q_SLAB), kn.reshape(B, skp, hw), kr.reshape(B, skp, LANES),
                   v.reshape(B, skp, hw), P, S, tq, tk)

    x2, h3, idx, gate = _mix(x, o_gdn, o_mla, mem_k, mem_v, pw, tm)
    y = _moe(h3.reshape(T, D), idx.reshape(T, LANES)[:, :TOP_K], gate.reshape(T, LANES)[:, :TOP_K],
             x2.reshape(T, D), ew)
    return y.reshape(B, S, D), lat_new.reshape(B, S, KV_RANK), kpe_new, s_new, conv_new


def kernel(x_prompt, x_sample, cache_kv_latent, cache_k_rope, state_gdn, state_conv, cache_mem_k, cache_mem_v, mem_prompt, norm_mix, w_in, conv_w, a_log, dt_bias, gdn_norm, q_a_norm, w_qb, kv_a_norm, w_kvb, q_norm, k_nope_norm, k_rope_norm, w_out, norm_mem, mem_norm, w_mq, w_mk, w_mv, mq_norm, mk_norm, w_mo, norm_ffn, w_router, b_router, w_gu, b_gu, w_down, b_down):
    depth = norm_mix.shape[0]
    yp, ys = x_prompt, x_sample
    bp = x_prompt.shape[0]
    hw = MEM_HEADS * MEM_HEAD_DIM
    outs = [[] for _ in range(10)]
    for l in range(depth):
        pw, ew = _prep_weights(norm_mix[l], w_in[l], conv_w[l], a_log[l], dt_bias[l], gdn_norm[l], q_a_norm[l],
                               w_qb[l], kv_a_norm[l], w_kvb[l], q_norm[l], k_nope_norm[l], k_rope_norm[l], w_out[l],
                               norm_mem[l], mem_norm[l], w_mq[l], w_mk[l], w_mv[l], mq_norm[l], mk_norm[l], w_mo[l],
                               norm_ffn[l], w_router[l], b_router[l], w_gu[l], b_gu[l], w_down[l], b_down[l])
        nm = mem_prompt.shape[1]
        mk, mv = _memkv(mem_prompt.reshape(bp * nm, D_MODEL), pw['mnorm'], pw['w_mk'], pw['w_mv'], pw['mkn'],
                        _pick(bp * nm, (512, 256)))
        mk = mk.reshape(bp, nm, hw)
        mv = mv.reshape(bp, nm, hw)
        yp, lat, kpe, s_fin, cv = _trunk(
            yp, jnp.zeros((bp, 0, KV_RANK), F32), jnp.zeros((bp, 0, ROPE_DIM), F32),
            jnp.zeros((bp, GDN_HEADS, GDN_DK, GDN_DV), F32), jnp.zeros((bp, CONV_W - 1, CONV_DIM), F32), mk, mv,
            pw, ew)
        bs = x_sample.shape[0]
        ys, lat2, kpe2, s_fin2, cv2 = _trunk(
            ys, cache_kv_latent[l], cache_k_rope[l], state_gdn[l], state_conv[l],
            cache_mem_k[l].reshape(bs, nm, hw), cache_mem_v[l].reshape(bs, nm, hw), pw, ew)
        for lst, val in zip(outs, (lat, kpe, s_fin, cv, mk.reshape(bp, nm, MEM_HEADS, MEM_HEAD_DIM),
                                   mv.reshape(bp, nm, MEM_HEADS, MEM_HEAD_DIM), lat2, kpe2, s_fin2, cv2)):
            lst.append(val)
    return (yp, ys) + tuple(jnp.stack(o) for o in outs)
```

```python
import functools
import math

import numpy as np
import jax
import jax.numpy as jnp
from jax import lax
from jax.experimental import pallas as pl
from jax.experimental.pallas import tpu as pltpu
from jax.experimental.pallas import tpu_sc as plsc

F32 = jnp.float32
BF16 = jnp.bfloat16

D_MODEL = 1024
CHUNK = 64
EPS = 1e-6
GDN_HEADS = 4
GDN_DK = 128
GDN_DV = 128
CONV_W = 4
GDN_QK = GDN_HEADS * GDN_DK
GDN_VW = GDN_HEADS * GDN_DV
CONV_DIM = 2 * GDN_QK + GDN_VW
MLA_HEADS = 4
Q_RANK = 384
KV_RANK = 256
NOPE_DIM = 128
ROPE_DIM = 64
V_DIM = 128
QK_DIM = NOPE_DIM + ROPE_DIM
ROPE_THETA = 10000.0
N_MEM = 256
MEM_HEADS = 4
MEM_HEAD_DIM = 128
N_EXPERTS = 32
TOP_K = 4
D_FF = D_MODEL
SWIGLU_ALPHA = 1.702
SWIGLU_LIMIT = 7.0

LANES = 128
Q_SLAB = 2 * LANES
NEG_BIG = -1e30
VMEM_LIMIT = 56 * 1024 * 1024
MOE_ROWS = 256
GDN_UNITS = 8
FF_CHUNK = 256
SC_WINDOW = 128
PK_CHUNKS = D_MODEL // 2 // LANES
Y_CHUNKS = D_MODEL // LANES


def _cparams(sem):
    return pltpu.CompilerParams(dimension_semantics=sem, vmem_limit_bytes=VMEM_LIMIT)


def _dot(a, b):
    return jnp.dot(a, b, preferred_element_type=F32)


def _dot_nt(a, b):
    return lax.dot_general(a, b, (((1,), (1,)), ((), ())), preferred_element_type=F32)


def _dot_tn(a, b):
    return lax.dot_general(a, b, (((0,), (0,)), ((), ())), preferred_element_type=F32)


def _rms(x, gain, n=None):
    n = x.shape[-1] if n is None else n
    ss = jnp.sum(x * x, axis=-1, keepdims=True) * (1.0 / n)
    return (x * lax.rsqrt(ss + EPS)) * gain


def _sigmoid(x):
    return 1.0 / (1.0 + jnp.exp(-x))


def _rope128(r, cos, sna, snb):
    return r * cos + pltpu.roll(r, 96, 1) * sna + pltpu.roll(r, 32, 1) * snb


def _pack_bf16_pairs(x):
    n = x.shape[1] // 2
    lo = pltpu.bitcast(x[:, :n], jnp.uint32) >> 16
    hi = pltpu.bitcast(x[:, n:], jnp.uint32) & jnp.uint32(0xFFFF0000)
    return lo | hi


def _unpack_bf16_pairs(p):
    lo = pltpu.bitcast(p << 16, F32)
    hi = pltpu.bitcast(p & jnp.uint32(0xFFFF0000), F32)
    return jnp.concatenate([lo, hi], axis=1)


def _const_spec(shape):
    nd = len(shape)
    return pl.BlockSpec(shape, lambda *_: (0,) * nd)


def _inproj_kernel(x_ref, nmix_ref, wu_ref, wz_ref, wcq_ref, wckv_ref, ws_ref, qan_ref, wqb_ref, qn_ref,
                   kvan_ref, krn_ref, alog_ref, dtb_ref, cos_ref, sna_ref, snb_ref,
                   u_ref, z_ref, q_ref, lat_ref, small_ref):
    x = x_ref[...]
    hb = _rms(x, nmix_ref[...]).astype(BF16)
    u_ref[...] = _dot(hb, wu_ref[...])
    z_ref[...] = _dot(hb, wz_ref[...])
    cos, sna, snb = cos_ref[...], sna_ref[...], snb_ref[...]

    cq = _rms(_dot(hb, wcq_ref[...]), qan_ref[...]).astype(BF16)
    qf = _dot(cq, wqb_ref[...])
    scale = QK_DIM ** -0.5
    for h in range(MLA_HEADS):
        slab = qf[:, h * Q_SLAB:(h + 1) * Q_SLAB]
        slab = _rms(slab, qn_ref[...], n=QK_DIM)
        nope = slab[:, :LANES]
        ropd = _rope128(slab[:, LANES:], cos, sna, snb)
        q_ref[:, h * Q_SLAB:h * Q_SLAB + LANES] = (nope * scale).astype(BF16)
        q_ref[:, h * Q_SLAB + LANES:(h + 1) * Q_SLAB] = (ropd * scale).astype(BF16)

    lat_ref[...] = _rms(_dot(hb, wckv_ref[...]), kvan_ref[...])

    sm = _dot(hb, ws_ref[...])
    lane = lax.broadcasted_iota(jnp.int32, sm.shape, 1)
    kp = jnp.where(lane < ROPE_DIM, sm, 0.0)
    kpe = _rope128(_rms(kp, krn_ref[...], n=ROPE_DIM), cos, sna, snb)
    sp = sm + dtb_ref[...]
    softplus = jnp.maximum(sp, 0.0) + jnp.log1p(jnp.exp(-jnp.abs(sp)))
    g = -jnp.exp(alog_ref[...]) * softplus
    beta = _sigmoid(sm)
    small_ref[...] = jnp.where(lane < ROPE_DIM, kpe,
                               jnp.where(lane < ROPE_DIM + GDN_HEADS, g,
                                         jnp.where(lane < ROPE_DIM + 2 * GDN_HEADS, beta, 0.0)))


def _inproj(x2d, S, tm, pw, tabs):
    T = x2d.shape[0]
    nblk_s = S // tm
    row = lambda n: pl.BlockSpec((tm, n), lambda i: (i, 0))
    tab = pl.BlockSpec((tm, LANES), lambda i: (i % nblk_s, 0))
    consts = [pw['nmix'], pw['w_u'], pw['w_z'], pw['w_cq'], pw['w_ckv'], pw['w_s'], pw['qan'], pw['w_qb'],
              pw['qn'], pw['kvan'], pw['krn'], pw['alog'], pw['dtb']]
    return pl.pallas_call(
        _inproj_kernel,
        grid=(T // tm,),
        in_specs=[row(D_MODEL)] + [_const_spec(c.shape) for c in consts] + [tab, tab, tab],
        out_specs=[row(CONV_DIM), row(GDN_VW), row(MLA_HEADS * Q_SLAB), row(KV_RANK), row(LANES)],
        out_shape=[jax.ShapeDtypeStruct((T, CONV_DIM), F32), jax.ShapeDtypeStruct((T, GDN_VW), F32),
                   jax.ShapeDtypeStruct((T, MLA_HEADS * Q_SLAB), BF16), jax.ShapeDtypeStruct((T, KV_RANK), F32),
                   jax.ShapeDtypeStruct((T, LANES), F32)],
        compiler_params=_cparams(("parallel",)),
        name="inproj",
    )(x2d, *consts, *tabs)


def _split3(x):
    hi = x.astype(BF16)
    r = x - hi.astype(F32)
    mid = r.astype(BF16)
    lo = (r - mid.astype(F32)).astype(BF16)
    return hi, mid, lo


def _gdn_kernel(u_ref, small_ref, z_ref, convw_ref, cpast_ref, s0_ref, gnorm_ref,
                to_ref, trilm_ref, strictm_ref, same2_ref, lvl_ref,
                o_ref, sfin_ref, cnew_ref, ext_ref, uc_ref, state_ref,
                qf_ref, kf_ref, vb_ref, bt_ref, gg_ref, gcum_ref, glast_ref, kk_ref, qk_ref, mb_ref, x_ref, qkb_ref,
                kbe_ref, qg_ref, kdec_ref, egl_ref, t1_ref, uu_ref, ww_ref, vn_ref, qs_ref, *, NB, LC):
    j = pl.program_id(1)
    nj = pl.num_programs(1)
    PADR = 8
    C = LC // CHUNK
    U = NB * C
    HR = GDN_HEADS * CHUNK

    @pl.when(j == 0)
    def _():
        state_ref[...] = s0_ref[...]
        ext_ref[:, PADR - (CONV_W - 1):PADR, :] = cpast_ref[...]

    w = convw_ref[...]
    for nb in range(NB):
        ext_ref[nb, PADR:PADR + LC, :] = u_ref[nb]
        acc = ext_ref[nb, PADR:PADR + LC, :] * w[CONV_W - 1:CONV_W, :]
        for t in range(1, CONV_W):
            acc = acc + ext_ref[nb, PADR - t:PADR - t + LC, :] * w[CONV_W - 1 - t:CONV_W - t, :]
        uc_ref[nb] = acc * _sigmoid(acc)
        ext_ref[nb, 0:PADR, :] = ext_ref[nb, LC:LC + PADR, :]

    @pl.when(j == nj - 1)
    def _():
        cnew_ref[...] = ext_ref[:, PADR - (CONV_W - 1):PADR, :]

    units = [(nb, c) for nb in range(NB) for c in range(C)]
    g0 = ROPE_DIM
    b0 = ROPE_DIM + GDN_HEADS
    for u, (nb, c) in enumerate(units):
        rows = slice(c * CHUNK, (c + 1) * CHUNK)
        sm = small_ref[nb, rows, :]
        for h in range(GDN_HEADS):
            hr = slice(h * CHUNK, (h + 1) * CHUNK)
            q = uc_ref[nb, rows, h * GDN_DK:(h + 1) * GDN_DK]
            k = uc_ref[nb, rows, GDN_QK + h * GDN_DK:GDN_QK + (h + 1) * GDN_DK]
            v = uc_ref[nb, rows, 2 * GDN_QK + h * GDN_DV:2 * GDN_QK + (h + 1) * GDN_DV]
            beta = jnp.broadcast_to(sm[:, b0 + h:b0 + h + 1], (CHUNK, LANES))
            qf_ref[u, hr, :] = (q * lax.rsqrt(jnp.sum(q * q, -1, keepdims=True) + EPS)) * (GDN_DK ** -0.5)
            kf_ref[u, hr, :] = k * lax.rsqrt(jnp.sum(k * k, -1, keepdims=True) + EPS)
            vb_ref[u, hr, :] = (v * beta).astype(BF16)
            bt_ref[u, hr, :] = beta
            gg_ref[u, hr, :] = jnp.broadcast_to(sm[:, g0 + h:g0 + h + 1], (CHUNK, LANES))

    to = to_ref[...]
    for u in range(U):
        gl = sum(_dot(to, part) for part in _split3(gg_ref[u]))
        gcum_ref[u] = gl[:HR, :]
        glast_ref[u] = gl[HR:, :]
        k = kf_ref[u]
        kbf = k.astype(BF16)
        kk_ref[u] = _dot_nt((k * bt_ref[u]).astype(BF16), kbf)
        qk_ref[u] = _dot_nt(qf_ref[u].astype(BF16), kbf)

    tril = trilm_ref[...] > 0.0
    strict = strictm_ref[...] > 0.0
    same2 = same2_ref[...] > 0.0
    eye = trilm_ref[...] - strictm_ref[...]
    for u in range(U):
        gcum = gcum_ref[u]
        grow = gcum.T[0:1, :]
        gcol = jnp.concatenate([gcum, gcum], axis=1)
        decay = jnp.where(tril, jnp.exp(jnp.where(tril, gcol - grow, 0.0)), 0.0)
        m = jnp.where(strict, kk_ref[u] * decay, 0.0)
        mb_ref[u] = m.astype(BF16)
        x_ref[u] = eye - jnp.where(same2, m, 0.0)
        qkb_ref[u] = jnp.where(tril, qk_ref[u] * decay, 0.0).astype(BF16)
        egc = jnp.exp(gcum)
        k = kf_ref[u]
        kbe_ref[u] = (k * bt_ref[u] * egc).astype(BF16)
        qg_ref[u] = (qf_ref[u] * egc).astype(BF16)
        kdec_ref[u] = (k * jnp.exp(glast_ref[u] - gcum)).astype(BF16)
        egl_ref[u] = jnp.exp(glast_ref[u])

    for lvl in range(lvl_ref.shape[0]):
        lm = lvl_ref[lvl]
        for u in range(U):
            t1_ref[u] = _dot(mb_ref[u] * lm, x_ref[u].astype(BF16)).astype(BF16)
        for u in range(U):
            x = x_ref[u]
            x_ref[u] = x - _dot(x.astype(BF16), t1_ref[u])

    for u in range(U):
        xb = x_ref[u].astype(BF16)
        uu_ref[u] = _dot(xb, vb_ref[u])
        ww_ref[u] = _dot(xb, kbe_ref[u]).astype(BF16)

    gnorm = gnorm_ref[...]
    for c in range(C):
        for nb in range(NB):
            u = nb * C + c
            rows = slice(c * CHUNK, (c + 1) * CHUNK)
            for h in range(GDN_HEADS):
                hr = slice(h * CHUNK, (h + 1) * CHUNK)
                stb = state_ref[nb, h].astype(BF16)
                r = _dot(jnp.concatenate([ww_ref[u, hr, :], qg_ref[u, hr, :]], axis=0), stb)
                vn_ref[u, hr, :] = (uu_ref[u, hr, :] - r[:CHUNK]).astype(BF16)
                qs_ref[u, hr, :] = r[CHUNK:]
            out = qs_ref[u] + _dot(qkb_ref[u], vn_ref[u])
            for h in range(GDN_HEADS):
                hr = slice(h * CHUNK, (h + 1) * CHUNK)
                state_ref[nb, h] = (state_ref[nb, h] * egl_ref[u, h * CHUNK:h * CHUNK + 1, :]
                                    + _dot_tn(kdec_ref[u, hr, :], vn_ref[u, hr, :]))
                zz = z_ref[nb, rows, h * GDN_DV:(h + 1) * GDN_DV]
                og = _rms(out[hr, :], gnorm) * (zz * _sigmoid(zz))
                o_ref[nb, rows, h * GDN_DV:(h + 1) * GDN_DV] = og.astype(BF16)

    @pl.when(j == nj - 1)
    def _():
        sfin_ref[...] = state_ref[...]


def _gdn_masks():
    hr = GDN_HEADS * CHUNK
    i = np.arange(hr)[:, None]
    j = np.arange(hr)[None, :]
    same_head = (i // CHUNK) == (j // CHUNK)
    tril = same_head & (i >= j)
    strict = same_head & (i > j)
    same2 = strict & ((i // 2) == (j // 2))
    lvls = []
    blk = 2
    while blk < CHUNK:
        lvls.append(strict & ((i // (2 * blk)) == (j // (2 * blk))) & ((i // blk) != (j // blk)))
        blk *= 2
    to = np.concatenate([tril, same_head], axis=0)
    f = lambda a: jnp.asarray(a.astype(np.float32))
    return (jnp.asarray(to.astype(np.float32), dtype=BF16), f(tril), f(strict), f(same2),
            jnp.asarray(np.stack(lvls).astype(np.float32), dtype=BF16))


def _gdn(u3, small3, z3, conv_w, conv_past, s0, gnorm, NB, LC):
    B, S, _ = u3.shape
    C = LC // CHUNK
    U = NB * C
    HR = GDN_HEADS * CHUNK
    masks = _gdn_masks()
    tile = lambda n: pl.BlockSpec((NB, LC, n), lambda b, j: (b, j, 0))
    stspec = pl.BlockSpec((NB, GDN_HEADS, GDN_DK, GDN_DV), lambda b, j: (b, 0, 0, 0))
    cvspec = pl.BlockSpec((NB, CONV_W - 1, CONV_DIM), lambda b, j: (b, 0, 0))
    vm = lambda shape, dt: pltpu.VMEM(shape, dt)
    return pl.pallas_call(
        functools.partial(_gdn_kernel, NB=NB, LC=LC),
        grid=(B // NB, S // LC),
        in_specs=[tile(CONV_DIM), tile(LANES), tile(GDN_VW), _const_spec(conv_w.shape), cvspec, stspec,
                  _const_spec(gnorm.shape)] + [_const_spec(m.shape) for m in masks],
        out_specs=[tile(GDN_VW), stspec, cvspec],
        out_shape=[jax.ShapeDtypeStruct((B, S, GDN_VW), BF16),
                   jax.ShapeDtypeStruct((B, GDN_HEADS, GDN_DK, GDN_DV), F32),
                   jax.ShapeDtypeStruct((B, CONV_W - 1, CONV_DIM), F32)],
        scratch_shapes=[vm((NB, LC + 8, CONV_DIM), F32), vm((NB, LC, CONV_DIM), F32),
                        vm((NB, GDN_HEADS, GDN_DK, GDN_DV), F32),
                        vm((U, HR, LANES), F32), vm((U, HR, LANES), F32), vm((U, HR, LANES), BF16),
                        vm((U, HR, LANES), F32), vm((U, HR, LANES), F32),
                        vm((U, HR, LANES), F32), vm((U, HR, LANES), F32),
                        vm((U, HR, HR), F32), vm((U, HR, HR), F32),
                        vm((U, HR, HR), BF16), vm((U, HR, HR), F32), vm((U, HR, HR), BF16),
                        vm((U, HR, LANES), BF16), vm((U, HR, LANES), BF16), vm((U, HR, LANES), BF16),
                        vm((U, HR, LANES), F32), vm((U, HR, HR), BF16),
                        vm((U, HR, LANES), F32), vm((U, HR, LANES), BF16),
                        vm((U, HR, LANES), BF16), vm((U, HR, LANES), F32)],
        compiler_params=_cparams(("parallel", "arbitrary")),
        name="gdn",
    )(u3, small3, z3, conv_w, conv_past, s0, gnorm, *masks)


def _kvproj_kernel(lat_ref, kpe_ref, wkvb_ref, knn_ref, kn_ref, kr_ref, v_ref):
    kv = _dot(lat_ref[...].astype(BF16), wkvb_ref[...])
    hw = MLA_HEADS * NOPE_DIM
    for h in range(MLA_HEADS):
        kh = kv[:, h * NOPE_DIM:(h + 1) * NOPE_DIM]
        kn_ref[:, h * NOPE_DIM:(h + 1) * NOPE_DIM] = _rms(kh, knn_ref[...]).astype(BF16)
    v_ref[...] = kv[:, hw:].astype(BF16)
    kp = kpe_ref[...]
    lane = lax.broadcasted_iota(jnp.int32, kp.shape, 1)
    kr_ref[...] = jnp.where(lane < ROPE_DIM, kp, 0.0).astype(BF16)


def _kvproj(lat2d, kpe2d, w_kvb, knn, tm):
    T = lat2d.shape[0]
    row = lambda n: pl.BlockSpec((tm, n), lambda i: (i, 0))
    return pl.pallas_call(
        _kvproj_kernel,
        grid=(T // tm,),
        in_specs=[row(KV_RANK), row(LANES), _const_spec(w_kvb.shape), _const_spec(knn.shape)],
        out_specs=[row(MLA_HEADS * NOPE_DIM), row(LANES), row(MLA_HEADS * V_DIM)],
        out_shape=[jax.ShapeDtypeStruct((T, MLA_HEADS * NOPE_DIM), BF16), jax.ShapeDtypeStruct((T, LANES), BF16),
                   jax.ShapeDtypeStruct((T, MLA_HEADS * V_DIM), BF16)],
        compiler_params=_cparams(("parallel",)),
        name="kvproj",
    )(lat2d, kpe2d, w_kvb, knn)


def _last_kblock(qi, tq, tk, P, nk):
    last_key = ((P + qi * tq + tq - 1) // CHUNK) * CHUNK + CHUNK - 1
    return jnp.minimum(last_key // tk, nk - 1)


def _flash_kernel(q_ref, kn_ref, kr_ref, v_ref, o_ref, m_ref, l_ref, acc_ref, *, tq, tk, P, S, nk):
    qi = pl.program_id(1)
    ki = pl.program_id(2)

    @pl.when(ki == 0)
    def _():
        m_ref[...] = jnp.full(m_ref.shape, NEG_BIG, F32)
        l_ref[...] = jnp.zeros(l_ref.shape, F32)
        acc_ref[...] = jnp.zeros(acc_ref.shape, F32)

    @pl.when(ki <= _last_kblock(qi, tq, tk, P, nk))
    def _():
        qpos = P + qi * tq + lax.broadcasted_iota(jnp.int32, (tq, tk), 0)
        kpos = ki * tk + lax.broadcasted_iota(jnp.int32, (tq, tk), 1)
        mask = ((kpos // CHUNK) <= (qpos // CHUNK)) & (kpos < P + S)
        kr = kr_ref[...]
        for h in range(MLA_HEADS):
            qh = q_ref[:, h * Q_SLAB:(h + 1) * Q_SLAB]
            kh = jnp.concatenate([kn_ref[:, h * NOPE_DIM:(h + 1) * NOPE_DIM], kr], axis=1)
            s = jnp.where(mask, _dot_nt(qh, kh), NEG_BIG)
            m_prev = m_ref[h]
            m_new = jnp.maximum(m_prev, jnp.max(s, axis=-1, keepdims=True))
            alpha = jnp.exp(m_prev - m_new)
            p = jnp.exp(s - m_new)
            l_ref[h] = alpha * l_ref[h] + jnp.sum(p, axis=-1, keepdims=True)
            acc_ref[h] = alpha * acc_ref[h] + _dot(p.astype(BF16), v_ref[:, h * V_DIM:(h + 1) * V_DIM])
            m_ref[h] = m_new

    @pl.when(ki == nk - 1)
    def _():
        for h in range(MLA_HEADS):
            o_ref[:, h * V_DIM:(h + 1) * V_DIM] = (acc_ref[h] / l_ref[h]).astype(BF16)


def _flash(q3, kn3, kr3, v3, P, S, tq, tk):
    B = q3.shape[0]
    skp = kn3.shape[1]
    nk = skp // tk
    kmap = lambda b, qi, ki: (b, jnp.minimum(ki, _last_kblock(qi, tq, tk, P, nk)), 0)
    return pl.pallas_call(
        functools.partial(_flash_kernel, tq=tq, tk=tk, P=P, S=S, nk=nk),
        grid=(B, S // tq, nk),
        in_specs=[pl.BlockSpec((None, tq, MLA_HEADS * Q_SLAB), lambda b, qi, ki: (b, qi, 0)),
                  pl.BlockSpec((None, tk, MLA_HEADS * NOPE_DIM), kmap),
                  pl.BlockSpec((None, tk, LANES), kmap),
                  pl.BlockSpec((None, tk, MLA_HEADS * V_DIM), kmap)],
        out_specs=pl.BlockSpec((None, tq, MLA_HEADS * V_DIM), lambda b, qi, ki: (b, qi, 0)),
        out_shape=jax.ShapeDtypeStruct((B, S, MLA_HEADS * V_DIM), BF16),
        scratch_shapes=[pltpu.VMEM((MLA_HEADS, tq, 1), F32), pltpu.VMEM((MLA_HEADS, tq, 1), F32),
                        pltpu.VMEM((MLA_HEADS, tq, V_DIM), F32)],
        compiler_params=_cparams(("parallel", "parallel", "arbitrary")),
        name="mla_attn",
    )(q3, kn3, kr3, v3)


def _memkv_kernel(mem_ref, mnorm_ref, wmk_ref, wmv_ref, mkn_ref, k_ref, v_ref):
    mb = _rms(mem_ref[...], mnorm_ref[...]).astype(BF16)
    kf = _dot(mb, wmk_ref[...])
    for h in range(MEM_HEADS):
        sl = slice(h * MEM_HEAD_DIM, (h + 1) * MEM_HEAD_DIM)
        k_ref[:, sl] = _rms(kf[:, sl], mkn_ref[...])
    v_ref[...] = _dot(mb, wmv_ref[...])


def _memkv(mem2d, mnorm, w_mk, w_mv, mkn, tm):
    T = mem2d.shape[0]
    hw = MEM_HEADS * MEM_HEAD_DIM
    row = lambda n: pl.BlockSpec((tm, n), lambda i: (i, 0))
    return pl.pallas_call(
        _memkv_kernel,
        grid=(T // tm,),
        in_specs=[row(D_MODEL), _const_spec(mnorm.shape), _const_spec(w_mk.shape), _const_spec(w_mv.shape),
                  _const_spec(mkn.shape)],
        out_specs=[row(hw), row(hw)],
        out_shape=[jax.ShapeDtypeStruct((T, hw), F32), jax.ShapeDtypeStruct((T, hw), F32)],
        compiler_params=_cparams(("parallel",)),
        name="mem_kv",
    )(mem2d, mnorm, w_mk, w_mv, mkn)


def _mix_kernel(x_ref, og_ref, om_ref, mk_ref, mv_ref, wout_ref, nmem_ref, wmq_ref, mqn_ref, wmo_ref, nffn_ref,
                wrh_ref, wrl_ref, br_ref, x2_ref, h3_ref, idx_ref, gate_ref, counts_ref, cnt_ref):
    x1 = x_ref[...] + _dot(og_ref[...], wout_ref[0:GDN_VW, :]) + _dot(om_ref[...], wout_ref[GDN_VW:, :])
    hb = _rms(x1, nmem_ref[...]).astype(BF16)
    qm = _dot(hb, wmq_ref[...])
    heads = []
    for h in range(MEM_HEADS):
        sl = slice(h * MEM_HEAD_DIM, (h + 1) * MEM_HEAD_DIM)
        qh = (_rms(qm[:, sl], mqn_ref[...]) * (MEM_HEAD_DIM ** -0.5)).astype(BF16)
        s = _dot_nt(qh, mk_ref[:, sl].astype(BF16))
        p = jnp.exp(s - jnp.max(s, axis=-1, keepdims=True))
        p = p / jnp.sum(p, axis=-1, keepdims=True)
        heads.append(_dot(p.astype(BF16), mv_ref[:, sl].astype(BF16)).astype(BF16))
    x2 = x1 + _dot(jnp.concatenate(heads, axis=1), wmo_ref[...])
    x2_ref[...] = x2
    h3 = _rms(x2, nffn_ref[...])
    hi = h3.astype(BF16)
    packed = _pack_bf16_pairs(hi.astype(F32))
    for c in range(h3_ref.shape[0]):
        h3_ref[c] = packed[:, c * LANES:(c + 1) * LANES]
    lo = (h3 - hi.astype(F32)).astype(BF16)
    wrh = wrh_ref[...]
    logits = _dot(hi, wrh) + _dot(lo, wrh) + _dot(hi, wrl_ref[...]) + br_ref[...]

    lane = lax.broadcasted_iota(jnp.int32, logits.shape, 1).astype(F32)
    vals, idxs = [], []
    cur = logits
    for _ in range(TOP_K):
        mx = jnp.max(cur, axis=-1, keepdims=True)
        ix = jnp.min(jnp.where(cur == mx, lane, float(LANES)), axis=-1, keepdims=True)
        vals.append(mx)
        idxs.append(ix)
        cur = jnp.where(lane == ix, -3e38, cur)
    es = [jnp.exp(v - vals[0]) for v in vals]
    den = es[0] + es[1] + es[2] + es[3]

    first = (pl.program_id(0) == 0) & (pl.program_id(1) == 0)

    @pl.when(first)
    def _():
        cnt_ref[...] = jnp.zeros(cnt_ref.shape, F32)

    sel = jnp.zeros(logits.shape, F32)
    for k in range(TOP_K):
        sel = sel + jnp.where(lane == idxs[k], 1.0, 0.0)
    tm = logits.shape[0]
    ri = lax.broadcasted_iota(jnp.int32, (tm, tm), 0)
    ci = lax.broadcasted_iota(jnp.int32, (tm, tm), 1)
    before = jnp.where(ri > ci, 1.0, 0.0).astype(BF16)
    excl = _dot(before, sel.astype(BF16)) + cnt_ref[...]
    cnt_ref[...] = cnt_ref[...] + jnp.sum(sel, axis=0, keepdims=True)
    counts_ref[...] = cnt_ref[...].astype(jnp.int32)

    idx_out = jnp.zeros(logits.shape, F32)
    gate_out = jnp.zeros(logits.shape, F32)
    for k in range(TOP_K):
        rank = jnp.sum(jnp.where(lane == idxs[k], excl, 0.0), axis=-1, keepdims=True)
        idx_out = jnp.where(lane == float(k), idxs[k], idx_out)
        idx_out = jnp.where(lane == float(TOP_K + k), rank, idx_out)
        gate_out = jnp.where(lane == float(k), es[k] / den, gate_out)
    idx_ref[...] = idx_out.astype(jnp.int32)
    gate_ref[...] = gate_out


def _mix(x3, og3, om3, mk3, mv3, pw, tm):
    B, S, _ = x3.shape
    hw = MEM_HEADS * MEM_HEAD_DIM
    tile = lambda n: pl.BlockSpec((None, tm, n), lambda b, i: (b, i, 0))
    memspec = pl.BlockSpec((None, N_MEM, hw), lambda b, i: (b, 0, 0))
    consts = [pw['w_out'], pw['nmem'], pw['w_mq'], pw['mqn'], pw['w_mo'], pw['nffn'], pw['wr_hi'], pw['wr_lo'],
              pw['b_r']]
    return pl.pallas_call(
        _mix_kernel,
        grid=(B, S // tm),
        in_specs=[tile(D_MODEL), tile(GDN_VW), tile(MLA_HEADS * V_DIM), memspec, memspec]
                 + [_const_spec(c.shape) for c in consts],
        out_specs=[tile(D_MODEL), pl.BlockSpec((PK_CHUNKS, tm, LANES), lambda b, i: (0, b * (S // tm) + i, 0)),
                   tile(LANES), tile(LANES), _const_spec((1, LANES))],
        out_shape=[jax.ShapeDtypeStruct((B, S, D_MODEL), F32),
                   jax.ShapeDtypeStruct((PK_CHUNKS, B * S, LANES), jnp.uint32),
                   jax.ShapeDtypeStruct((B, S, LANES), jnp.int32), jax.ShapeDtypeStruct((B, S, LANES), F32),
                   jax.ShapeDtypeStruct((1, LANES), jnp.int32)],
        scratch_shapes=[pltpu.VMEM((1, LANES), F32)],
        compiler_params=_cparams(("arbitrary", "arbitrary")),
        name="mix_mem_router",
    )(x3, og3, om3, mk3, mv3, *consts)


def _expert_kernel(be_ref, nu_ref, rows_ref, wgu_ref, bgu_ref, wd_ref, bd_ref, y_ref, wgub_ref, wdb_ref):
    i = pl.program_id(0)
    used = i < nu_ref[0]
    new_expert = (i == 0) | (be_ref[i] != be_ref[jnp.maximum(i - 1, 0)])

    @pl.when(used & new_expert)
    def _():
        def cast(r, carry):
            rs = pl.ds(pl.multiple_of(r * LANES, LANES), LANES)
            wgub_ref[rs, :] = wgu_ref[rs, :].astype(BF16)
            wdb_ref[rs, :] = wd_ref[rs, :].astype(BF16)
            return carry
        lax.fori_loop(0, D_MODEL // LANES, cast, 0)

    @pl.when(used)
    def _():
        packed = jnp.concatenate([rows_ref[c] for c in range(PK_CHUNKS)], axis=1)
        x = _unpack_bf16_pairs(packed).astype(BF16)
        acc = None
        for c in range(D_FF // FF_CHUNK):
            gs_ = slice(c * FF_CHUNK, (c + 1) * FF_CHUNK)
            us_ = slice(D_FF + c * FF_CHUNK, D_FF + (c + 1) * FF_CHUNK)
            gt = jnp.minimum(_dot(x, wgub_ref[:, gs_]) + bgu_ref[:, gs_], SWIGLU_LIMIT)
            up = jnp.clip(_dot(x, wgub_ref[:, us_]) + bgu_ref[:, us_], -SWIGLU_LIMIT, SWIGLU_LIMIT)
            act = gt * _sigmoid(SWIGLU_ALPHA * gt) * (up + 1.0)
            part = _dot(act.astype(BF16), wdb_ref[gs_, :])
            acc = part if acc is None else acc + part
        y = acc + bd_ref[...]
        for c in range(Y_CHUNKS):
            y_ref[c] = y[:, c * LANES:(c + 1) * LANES]

    @pl.when(jnp.logical_not(used))
    def _():
        y_ref[...] = jnp.zeros(y_ref.shape, F32)


def _experts(block_e, n_used, rows, w_gu, b_gu, w_down, b_down):
    n_rows = rows.shape[1]
    nb = n_rows // MOE_ROWS
    gs = pltpu.PrefetchScalarGridSpec(
        num_scalar_prefetch=2,
        grid=(nb,),
        in_specs=[pl.BlockSpec((PK_CHUNKS, MOE_ROWS, LANES), lambda i, be, nu: (0, i, 0)),
                  pl.BlockSpec((None, D_MODEL, 2 * D_FF), lambda i, be, nu: (be[i], 0, 0)),
                  pl.BlockSpec((None, 1, 2 * D_FF), lambda i, be, nu: (be[i], 0, 0)),
                  pl.BlockSpec((None, D_FF, D_MODEL), lambda i, be, nu: (be[i], 0, 0)),
                  pl.BlockSpec((None, 1, D_MODEL), lambda i, be, nu: (be[i], 0, 0))],
        out_specs=pl.BlockSpec((Y_CHUNKS, MOE_ROWS, LANES), lambda i, be, nu: (0, i, 0)),
        scratch_shapes=[pltpu.VMEM((D_MODEL, 2 * D_FF), BF16), pltpu.VMEM((D_FF, D_MODEL), BF16)],
    )
    return pl.pallas_call(
        _expert_kernel,
        grid_spec=gs,
        out_shape=jax.ShapeDtypeStruct((Y_CHUNKS, n_rows, LANES), F32),
        compiler_params=_cparams(("arbitrary",)),
        name="moe_experts",
    )(block_e, n_used, rows, w_gu, b_gu, w_down, b_down)


def _sc_mesh():
    return plsc.VectorSubcoreMesh(core_axis_name="core", subcore_axis_name="subcore")


def _sc_scatter_rows(x3, pos_t, n_rows):
    C, T, L = x3.shape
    K = pos_t.shape[0]
    nwin = T // SC_WINDOW

    @functools.partial(pl.kernel, out_type=jax.ShapeDtypeStruct((C, n_rows, L), x3.dtype), mesh=_sc_mesh(),
                       scratch_types=[])
    def scatter(x_hbm, i_hbm, o_hbm):
        for c in range(C):
            def body(x_vmem, i_vmem, c=c):
                for k in range(K):
                    pltpu.sync_copy(x_vmem, o_hbm.at[c].at[i_vmem.at[k]])

            pltpu.emit_pipeline(
                body, grid=(nwin,),
                in_specs=[pl.BlockSpec((SC_WINDOW, L), lambda i, c=c: (c * nwin + i, 0)),
                          pl.BlockSpec((K, SC_WINDOW), lambda i: (0, i))],
                out_specs=[], core_axis_name=("core", "subcore"), dimension_semantics=(pltpu.PARALLEL,),
            )(x_hbm, i_hbm)

    return scatter(x3.reshape(C * T, L), pos_t)


def _sc_gather_rows(table3, idx):
    C, _, L = table3.shape
    n = idx.shape[0]
    nwin = n // SC_WINDOW

    @functools.partial(pl.kernel, out_type=jax.ShapeDtypeStruct((C * n, L), table3.dtype), mesh=_sc_mesh(),
                       scratch_types=[])
    def gather(t_hbm, i_hbm, o_hbm):
        for c in range(C):
            def body(i_vmem, o_vmem, c=c):
                pltpu.sync_copy(t_hbm.at[c].at[i_vmem.at[0]], o_vmem)

            pltpu.emit_pipeline(
                body, grid=(nwin,),
                in_specs=[pl.BlockSpec((1, SC_WINDOW), lambda i: (0, i))],
                out_specs=[pl.BlockSpec((SC_WINDOW, L), lambda i, c=c: (c * nwin + i, 0))],
                core_axis_name=("core", "subcore"), dimension_semantics=(pltpu.PARALLEL,),
            )(i_hbm, o_hbm)

    return gather(table3, idx.reshape(1, n)).reshape(C, n, L)


def _combine_kernel(x2_ref, g_ref, gate_ref, o_ref):
    gate = gate_ref[...]
    for c in range(Y_CHUNKS):
        cs = slice(c * LANES, (c + 1) * LANES)
        acc = x2_ref[:, cs]
        for k in range(TOP_K):
            acc = acc + g_ref[c, k] * gate[:, k:k + 1]
        o_ref[:, cs] = acc


def _combine(x2, g4, gate, tm):
    T = x2.shape[0]
    return pl.pallas_call(
        _combine_kernel,
        grid=(T // tm,),
        in_specs=[pl.BlockSpec((tm, D_MODEL), lambda i: (i, 0)),
                  pl.BlockSpec((Y_CHUNKS, TOP_K, tm, LANES), lambda i: (0, 0, i, 0)),
                  pl.BlockSpec((tm, LANES), lambda i: (i, 0))],
        out_specs=pl.BlockSpec((tm, D_MODEL), lambda i: (i, 0)),
        out_shape=jax.ShapeDtypeStruct((T, D_MODEL), F32),
        compiler_params=_cparams(("parallel",)),
        name="moe_combine",
    )(x2, g4, gate)


def _moe(h3p, idxr, gate, counts, x2, ew):
    T = h3p.shape[1]
    idx = idxr[:, :TOP_K]
    rank = idxr[:, TOP_K:2 * TOP_K]
    cnt = counts[0, :N_EXPERTS]
    padded = (cnt + MOE_ROWS - 1) // MOE_ROWS * MOE_ROWS
    pad_end = jnp.cumsum(padded)
    pad_start = pad_end - padded
    onehot = idx[:, :, None] == jnp.arange(N_EXPERTS, dtype=jnp.int32)[None, None, :]
    pos_t = (jnp.sum(jnp.where(onehot, pad_start[None, None, :], 0), axis=-1) + rank).T
    nb = -(-T * TOP_K // MOE_ROWS) + N_EXPERTS
    starts = jnp.arange(nb, dtype=jnp.int32) * MOE_ROWS
    block_e = jnp.minimum(jnp.sum((pad_end[None, :] <= starts[:, None]).astype(jnp.int32), axis=1), N_EXPERTS - 1)
    n_used = (pad_end[-1] // MOE_ROWS).astype(jnp.int32).reshape(1)
    rows = _sc_scatter_rows(h3p, pos_t, nb * MOE_ROWS)
    y_rows = _experts(block_e, n_used, rows, ew['w_gu'], ew['b_gu'], ew['w_down'], ew['b_down'])
    g = _sc_gather_rows(y_rows, pos_t.reshape(-1))
    return _combine(x2, g.reshape(Y_CHUNKS, TOP_K, T, LANES), gate, _pick(T, (512, 256, 128, 64)))


def _pad_lanes(v, n=LANES, fill=0.0):
    return jnp.pad(v, (0, n - v.shape[0]), constant_values=fill).reshape(1, n)


def _prep_weights(norm_mix, w_in, conv_w, a_log, dt_bias, gdn_norm, q_a_norm, w_qb, kv_a_norm, w_kvb, q_norm,
                  k_nope_norm, k_rope_norm, w_out, norm_mem, mem_norm, w_mq, w_mk, w_mv, mq_norm, mk_norm, w_mo,
                  norm_ffn, w_router, b_router, w_gu, b_gu, w_down, b_down):
    c = np.cumsum([CONV_DIM, GDN_VW, GDN_HEADS, GDN_HEADS, Q_RANK, KV_RANK])
    w_u, w_z, w_a, w_b, w_cq, w_ckv, w_kpe = [w_in[:, lo:hi] for lo, hi in
                                              zip([0, *c], [*c, w_in.shape[1]])]
    w_s = jnp.concatenate([w_kpe, w_a, w_b], axis=1)
    w_s = jnp.pad(w_s, ((0, 0), (0, LANES - w_s.shape[1])))
    wq = w_qb.reshape(Q_RANK, MLA_HEADS, QK_DIM)
    wq = jnp.pad(wq, ((0, 0), (0, 0), (0, Q_SLAB - QK_DIM))).reshape(Q_RANK, MLA_HEADS * Q_SLAB)
    wkv = w_kvb.reshape(KV_RANK, MLA_HEADS, NOPE_DIM + V_DIM)
    wkv = jnp.concatenate([wkv[:, :, :NOPE_DIM].reshape(KV_RANK, -1), wkv[:, :, NOPE_DIM:].reshape(KV_RANK, -1)], 1)
    wr = jnp.pad(w_router, ((0, 0), (0, LANES - N_EXPERTS)))
    wr_hi = wr.astype(BF16)
    wr_lo = (wr - wr_hi.astype(F32)).astype(BF16)
    row = lambda v: v.reshape(1, -1)
    gpad = ROPE_DIM
    pw = dict(
        nmix=row(norm_mix), w_u=w_u.astype(BF16), w_z=w_z.astype(BF16), w_cq=w_cq.astype(BF16),
        w_ckv=w_ckv.astype(BF16), w_s=w_s.astype(BF16), qan=row(q_a_norm), w_qb=wq.astype(BF16),
        qn=_pad_lanes(q_norm, Q_SLAB), kvan=row(kv_a_norm), krn=_pad_lanes(k_rope_norm),
        alog=jnp.pad(a_log, (gpad, LANES - gpad - GDN_HEADS)).reshape(1, LANES),
        dtb=jnp.pad(dt_bias, (gpad, LANES - gpad - GDN_HEADS)).reshape(1, LANES),
        conv_w=conv_w, gnorm=row(gdn_norm), w_kvb=wkv.astype(BF16), knn=row(k_nope_norm),
        w_out=w_out.astype(BF16), nmem=row(norm_mem), w_mq=w_mq.astype(BF16), mqn=row(mq_norm),
        w_mo=w_mo.astype(BF16), nffn=row(norm_ffn), wr_hi=wr_hi, wr_lo=wr_lo,
        b_r=_pad_lanes(b_router, LANES, NEG_BIG),
        mnorm=row(mem_norm), w_mk=w_mk.astype(BF16), w_mv=w_mv.astype(BF16), mkn=row(mk_norm),
    )
    ew = dict(w_gu=w_gu, b_gu=b_gu.reshape(N_EXPERTS, 1, 2 * D_FF), w_down=w_down,
              b_down=b_down.reshape(N_EXPERTS, 1, D_MODEL))
    return pw, ew


def _rope_tables(P, S):
    half = ROPE_DIM // 2
    inv = ROPE_THETA ** (-jnp.arange(half, dtype=F32) / half)
    ang = (P + jnp.arange(S, dtype=jnp.int32)).astype(F32)[:, None] * inv[None, :]
    cos, sin = jnp.cos(ang), jnp.sin(ang)
    zh = jnp.zeros((S, half), F32)
    zz = jnp.zeros((S, LANES - ROPE_DIM), F32)
    return (jnp.concatenate([cos, cos, zz], 1), jnp.concatenate([-sin, zh, zz], 1),
            jnp.concatenate([zh, sin, zz], 1))


def _pick(n, prefs):
    for t in prefs:
        if n % t == 0:
            return t
    return n


def _trunk(x, lat_past, kpe_past, s0, conv_past, mem_k, mem_v, pw, ew):
    B, S, D = x.shape
    P = lat_past.shape[1]
    T = B * S
    tm = _pick(S, (512, 256, 128, 64))
    u, z, q, lat_new, small = _inproj(x.reshape(T, D), S, tm, pw, _rope_tables(P, S))

    LC = _pick(S, (256, 128, 64))
    NB = _pick(B, tuple(n for n in (8, 4, 2, 1) if n * (LC // CHUNK) <= GDN_UNITS))
    o_gdn, s_new, conv_new = _gdn(u.reshape(B, S, CONV_DIM), small.reshape(B, S, LANES), z.reshape(B, S, GDN_VW),
                                  pw['conv_w'], conv_past, s0, pw['gnorm'], NB, LC)

    sk = P + S
    tk = 512 if S >= 512 else -(-sk // LANES) * LANES
    skp = -(-sk // tk) * tk
    lat_all = jnp.concatenate([lat_past, lat_new.reshape(B, S, KV_RANK)], axis=1)
    kpe_new = small[:, :ROPE_DIM].reshape(B, S, ROPE_DIM)
    kpe_all = jnp.concatenate([kpe_past, kpe_new], axis=1)
    lat_all = jnp.pad(lat_all, ((0, 0), (0, skp - sk), (0, 0)))
    kpe_all = jnp.pad(kpe_all, ((0, 0), (0, skp - sk), (0, LANES - ROPE_DIM)))
    tkv = _pick(B * skp, (512, 256, 128, 64, 8))
    kn, kr, v = _kvproj(lat_all.reshape(B * skp, KV_RANK), kpe_all.reshape(B * skp, LANES), pw['w_kvb'], pw['knn'],
                        tkv)
    hw = MLA_HEADS * NOPE_DIM
    tq = _pick(S, (512, 256, 128, 64))
    o_mla = _flash(q.reshape(B, S, MLA_HEADS * Q_SLAB), kn.reshape(B, skp, hw), kr.reshape(B, skp, LANES),
                   v.reshape(B, skp, hw), P, S, tq, tk)

    x2, h3p, idxr, gate, counts = _mix(x, o_gdn, o_mla, mem_k, mem_v, pw, tm)
    y = _moe(h3p, idxr.reshape(T, LANES), gate.reshape(T, LANES), counts, x2.reshape(T, D), ew)
    return y.reshape(B, S, D), lat_new.reshape(B, S, KV_RANK), kpe_new, s_new, conv_new


def kernel(x_prompt, x_sample, cache_kv_latent, cache_k_rope, state_gdn, state_conv, cache_mem_k, cache_mem_v, mem_prompt, norm_mix, w_in, conv_w, a_log, dt_bias, gdn_norm, q_a_norm, w_qb, kv_a_norm, w_kvb, q_norm, k_nope_norm, k_rope_norm, w_out, norm_mem, mem_norm, w_mq, w_mk, w_mv, mq_norm, mk_norm, w_mo, norm_ffn, w_router, b_router, w_gu, b_gu, w_down, b_down):
    depth = norm_mix.shape[0]
    yp, ys = x_prompt, x_sample
    bp = x_prompt.shape[0]
    hw = MEM_HEADS * MEM_HEAD_DIM
    outs = [[] for _ in range(10)]
    for l in range(depth):
        pw, ew = _prep_weights(norm_mix[l], w_in[l], conv_w[l], a_log[l], dt_bias[l], gdn_norm[l], q_a_norm[l],
                               w_qb[l], kv_a_norm[l], w_kvb[l], q_norm[l], k_nope_norm[l], k_rope_norm[l], w_out[l],
                               norm_mem[l], mem_norm[l], w_mq[l], w_mk[l], w_mv[l], mq_norm[l], mk_norm[l], w_mo[l],
                               norm_ffn[l], w_router[l], b_router[l], w_gu[l], b_gu[l], w_down[l], b_down[l])
        nm = mem_prompt.shape[1]
        mk, mv = _memkv(mem_prompt.reshape(bp * nm, D_MODEL), pw['mnorm'], pw['w_mk'], pw['w_mv'], pw['mkn'],
                        _pick(bp * nm, (512, 256)))
        mk = mk.reshape(bp, nm, hw)
        mv = mv.reshape(bp, nm, hw)
        yp, lat, kpe, s_fin, cv = _trunk(
            yp, jnp.zeros((bp, 0, KV_RANK), F32), jnp.zeros((bp, 0, ROPE_DIM), F32),
            jnp.zeros((bp, GDN_HEADS, GDN_DK, GDN_DV), F32), jnp.zeros((bp, CONV_W - 1, CONV_DIM), F32), mk, mv,
            pw, ew)
        bs = x_sample.shape[0]
        ys, lat2, kpe2, s_fin2, cv2 = _trunk(
            ys, cache_kv_latent[l], cache_k_rope[l], state_gdn[l], state_conv[l],
            cache_mem_k[l].reshape(bs, nm, hw), cache_mem_v[l].reshape(bs, nm, hw), pw, ew)
        for lst, val in zip(outs, (lat, kpe, s_fin, cv, mk.reshape(bp, nm, MEM_HEADS, MEM_HEAD_DIM),
                                   mv.reshape(bp, nm, MEM_HEADS, MEM_HEAD_DIM), lat2, kpe2, s_fin2, cv2)):
            lst.append(val)
    return (yp, ys) + tuple(jnp.stack(o) for o in outs)
```

```python
import functools
import math

import numpy as np
import jax
import jax.numpy as jnp
from jax import lax
from jax.experimental import pallas as pl
from jax.experimental.pallas import tpu as pltpu
from jax.experimental.pallas import tpu_sc as plsc

F32 = jnp.float32
BF16 = jnp.bfloat16

D_MODEL = 1024
CHUNK = 64
EPS = 1e-6
GDN_HEADS = 4
GDN_DK = 128
GDN_DV = 128
CONV_W = 4
GDN_QK = GDN_HEADS * GDN_DK
GDN_VW = GDN_HEADS * GDN_DV
CONV_DIM = 2 * GDN_QK + GDN_VW
MLA_HEADS = 4
Q_RANK = 384
KV_RANK = 256
NOPE_DIM = 128
ROPE_DIM = 64
V_DIM = 128
QK_DIM = NOPE_DIM + ROPE_DIM
ROPE_THETA = 10000.0
N_MEM = 256
MEM_HEADS = 4
MEM_HEAD_DIM = 128
N_EXPERTS = 32
TOP_K = 4
D_FF = D_MODEL
SWIGLU_ALPHA = 1.702
SWIGLU_LIMIT = 7.0

LANES = 128
Q_SLAB = 2 * LANES
NEG_BIG = -1e30
VMEM_LIMIT = 56 * 1024 * 1024
MOE_ROWS = 512
GDN_UNITS = 8
FF_CHUNK = 256
SC_WINDOW = 128
PK_CHUNKS = D_MODEL // 2 // LANES
Y_CHUNKS = D_MODEL // LANES


def _cparams(sem):
    return pltpu.CompilerParams(dimension_semantics=sem, vmem_limit_bytes=VMEM_LIMIT)


def _dot(a, b):
    return jnp.dot(a, b, preferred_element_type=F32)


def _dot_nt(a, b):
    return lax.dot_general(a, b, (((1,), (1,)), ((), ())), preferred_element_type=F32)


def _dot_tn(a, b):
    return lax.dot_general(a, b, (((0,), (0,)), ((), ())), preferred_element_type=F32)


def _rms(x, gain, n=None):
    n = x.shape[-1] if n is None else n
    ss = jnp.sum(x * x, axis=-1, keepdims=True) * (1.0 / n)
    return (x * lax.rsqrt(ss + EPS)) * gain


def _sigmoid(x):
    return 1.0 / (1.0 + jnp.exp(-x))


def _rope128(r, cos, sna, snb):
    return r * cos + pltpu.roll(r, 96, 1) * sna + pltpu.roll(r, 32, 1) * snb


def _pack_bf16_pairs(x):
    n = x.shape[1] // 2
    lo = pltpu.bitcast(x[:, :n], jnp.uint32) >> 16
    hi = pltpu.bitcast(x[:, n:], jnp.uint32) & jnp.uint32(0xFFFF0000)
    return lo | hi


def _unpack_bf16_pairs(p):
    lo = pltpu.bitcast(p << 16, F32)
    hi = pltpu.bitcast(p & jnp.uint32(0xFFFF0000), F32)
    return jnp.concatenate([lo, hi], axis=1)


def _const_spec(shape):
    nd = len(shape)
    return pl.BlockSpec(shape, lambda *_: (0,) * nd)


def _inproj_kernel(x_ref, nmix_ref, wu_ref, wz_ref, wcq_ref, wckv_ref, ws_ref, qan_ref, wqb_ref, qn_ref,
                   kvan_ref, krn_ref, alog_ref, dtb_ref, cos_ref, sna_ref, snb_ref,
                   u_ref, z_ref, q_ref, lat_ref, small_ref):
    x = x_ref[...]
    hb = _rms(x, nmix_ref[...]).astype(BF16)
    u_ref[...] = _dot(hb, wu_ref[...])
    z_ref[...] = _dot(hb, wz_ref[...])
    cos, sna, snb = cos_ref[...], sna_ref[...], snb_ref[...]

    cq = _rms(_dot(hb, wcq_ref[...]), qan_ref[...]).astype(BF16)
    qf = _dot(cq, wqb_ref[...])
    scale = QK_DIM ** -0.5
    for h in range(MLA_HEADS):
        slab = qf[:, h * Q_SLAB:(h + 1) * Q_SLAB]
        slab = _rms(slab, qn_ref[...], n=QK_DIM)
        nope = slab[:, :LANES]
        ropd = _rope128(slab[:, LANES:], cos, sna, snb)
        q_ref[:, h * Q_SLAB:h * Q_SLAB + LANES] = (nope * scale).astype(BF16)
        q_ref[:, h * Q_SLAB + LANES:(h + 1) * Q_SLAB] = (ropd * scale).astype(BF16)

    lat_ref[...] = _rms(_dot(hb, wckv_ref[...]), kvan_ref[...])

    sm = _dot(hb, ws_ref[...])
    lane = lax.broadcasted_iota(jnp.int32, sm.shape, 1)
    kp = jnp.where(lane < ROPE_DIM, sm, 0.0)
    kpe = _rope128(_rms(kp, krn_ref[...], n=ROPE_DIM), cos, sna, snb)
    sp = sm + dtb_ref[...]
    softplus = jnp.maximum(sp, 0.0) + jnp.log1p(jnp.exp(-jnp.abs(sp)))
    g = -jnp.exp(alog_ref[...]) * softplus
    beta = _sigmoid(sm)
    small_ref[...] = jnp.where(lane < ROPE_DIM, kpe,
                               jnp.where(lane < ROPE_DIM + GDN_HEADS, g,
                                         jnp.where(lane < ROPE_DIM + 2 * GDN_HEADS, beta, 0.0)))


def _inproj(x2d, S, tm, pw, tabs):
    T = x2d.shape[0]
    nblk_s = S // tm
    row = lambda n: pl.BlockSpec((tm, n), lambda i: (i, 0))
    tab = pl.BlockSpec((tm, LANES), lambda i: (i % nblk_s, 0))
    consts = [pw['nmix'], pw['w_u'], pw['w_z'], pw['w_cq'], pw['w_ckv'], pw['w_s'], pw['qan'], pw['w_qb'],
              pw['qn'], pw['kvan'], pw['krn'], pw['alog'], pw['dtb']]
    return pl.pallas_call(
        _inproj_kernel,
        grid=(T // tm,),
        in_specs=[row(D_MODEL)] + [_const_spec(c.shape) for c in consts] + [tab, tab, tab],
        out_specs=[row(CONV_DIM), row(GDN_VW), row(MLA_HEADS * Q_SLAB), row(KV_RANK), row(LANES)],
        out_shape=[jax.ShapeDtypeStruct((T, CONV_DIM), F32), jax.ShapeDtypeStruct((T, GDN_VW), F32),
                   jax.ShapeDtypeStruct((T, MLA_HEADS * Q_SLAB), BF16), jax.ShapeDtypeStruct((T, KV_RANK), F32),
                   jax.ShapeDtypeStruct((T, LANES), F32)],
        compiler_params=_cparams(("parallel",)),
        name="inproj",
    )(x2d, *consts, *tabs)


def _split3(x):
    hi = x.astype(BF16)
    r = x - hi.astype(F32)
    mid = r.astype(BF16)
    lo = (r - mid.astype(F32)).astype(BF16)
    return hi, mid, lo


def _gdn_kernel(u_ref, small_ref, z_ref, convw_ref, cpast_ref, s0_ref, gnorm_ref,
                to_ref, trilm_ref, strictm_ref, same2_ref, lvl_ref,
                o_ref, sfin_ref, cnew_ref, ext_ref, uc_ref, state_ref,
                qf_ref, kf_ref, vb_ref, bt_ref, gg_ref, gcum_ref, glast_ref, kk_ref, qk_ref, mb_ref, x_ref, qkb_ref,
                kbe_ref, qg_ref, kdec_ref, egl_ref, t1_ref, uu_ref, ww_ref, vn_ref, qs_ref, *, NB, LC):
    j = pl.program_id(1)
    nj = pl.num_programs(1)
    PADR = 8
    C = LC // CHUNK
    U = NB * C
    HR = GDN_HEADS * CHUNK

    @pl.when(j == 0)
    def _():
        state_ref[...] = s0_ref[...]
        ext_ref[:, PADR - (CONV_W - 1):PADR, :] = cpast_ref[...]

    w = convw_ref[...]
    for nb in range(NB):
        ext_ref[nb, PADR:PADR + LC, :] = u_ref[nb]
        acc = ext_ref[nb, PADR:PADR + LC, :] * w[CONV_W - 1:CONV_W, :]
        for t in range(1, CONV_W):
            acc = acc + ext_ref[nb, PADR - t:PADR - t + LC, :] * w[CONV_W - 1 - t:CONV_W - t, :]
        uc_ref[nb] = acc * _sigmoid(acc)
        ext_ref[nb, 0:PADR, :] = ext_ref[nb, LC:LC + PADR, :]

    @pl.when(j == nj - 1)
    def _():
        cnew_ref[...] = ext_ref[:, PADR - (CONV_W - 1):PADR, :]

    units = [(nb, c) for nb in range(NB) for c in range(C)]
    g0 = ROPE_DIM
    b0 = ROPE_DIM + GDN_HEADS
    for u, (nb, c) in enumerate(units):
        rows = slice(c * CHUNK, (c + 1) * CHUNK)
        sm = small_ref[nb, rows, :]
        for h in range(GDN_HEADS):
            hr = slice(h * CHUNK, (h + 1) * CHUNK)
            q = uc_ref[nb, rows, h * GDN_DK:(h + 1) * GDN_DK]
            k = uc_ref[nb, rows, GDN_QK + h * GDN_DK:GDN_QK + (h + 1) * GDN_DK]
            v = uc_ref[nb, rows, 2 * GDN_QK + h * GDN_DV:2 * GDN_QK + (h + 1) * GDN_DV]
            beta = jnp.broadcast_to(sm[:, b0 + h:b0 + h + 1], (CHUNK, LANES))
            qf_ref[u, hr, :] = (q * lax.rsqrt(jnp.sum(q * q, -1, keepdims=True) + EPS)) * (GDN_DK ** -0.5)
            kf_ref[u, hr, :] = k * lax.rsqrt(jnp.sum(k * k, -1, keepdims=True) + EPS)
            vb_ref[u, hr, :] = (v * beta).astype(BF16)
            bt_ref[u, hr, :] = beta
            gg_ref[u, hr, :] = jnp.broadcast_to(sm[:, g0 + h:g0 + h + 1], (CHUNK, LANES))

    to = to_ref[...]
    for u in range(U):
        gl = sum(_dot(to, part) for part in _split3(gg_ref[u]))
        gcum_ref[u] = gl[:HR, :]
        glast_ref[u] = gl[HR:, :]
        k = kf_ref[u]
        kbf = k.astype(BF16)
        kk_ref[u] = _dot_nt((k * bt_ref[u]).astype(BF16), kbf)
        qk_ref[u] = _dot_nt(qf_ref[u].astype(BF16), kbf)

    tril = trilm_ref[...] > 0.0
    strict = strictm_ref[...] > 0.0
    same2 = same2_ref[...] > 0.0
    eye = trilm_ref[...] - strictm_ref[...]
    for u in range(U):
        gcum = gcum_ref[u]
        grow = gcum.T[0:1, :]
        gcol = jnp.concatenate([gcum, gcum], axis=1)
        decay = jnp.where(tril, jnp.exp(jnp.where(tril, gcol - grow, 0.0)), 0.0)
        m = jnp.where(strict, kk_ref[u] * decay, 0.0)
        mb_ref[u] = m.astype(BF16)
        x_ref[u] = eye - jnp.where(same2, m, 0.0)
        qkb_ref[u] = jnp.where(tril, qk_ref[u] * decay, 0.0).astype(BF16)
        egc = jnp.exp(gcum)
        k = kf_ref[u]
        kbe_ref[u] = (k * bt_ref[u] * egc).astype(BF16)
        qg_ref[u] = (qf_ref[u] * egc).astype(BF16)
        kdec_ref[u] = (k * jnp.exp(glast_ref[u] - gcum)).astype(BF16)
        egl_ref[u] = jnp.exp(glast_ref[u])

    for lvl in range(lvl_ref.shape[0]):
        lm = lvl_ref[lvl]
        for u in range(U):
            t1_ref[u] = _dot(mb_ref[u] * lm, x_ref[u].astype(BF16)).astype(BF16)
        for u in range(U):
            x = x_ref[u]
            x_ref[u] = x - _dot(x.astype(BF16), t1_ref[u])

    for u in range(U):
        xb = x_ref[u].astype(BF16)
        uu_ref[u] = _dot(xb, vb_ref[u])
        ww_ref[u] = _dot(xb, kbe_ref[u]).astype(BF16)

    gnorm = gnorm_ref[...]
    for c in range(C):
        for nb in range(NB):
            u = nb * C + c
            rows = slice(c * CHUNK, (c + 1) * CHUNK)
            for h in range(GDN_HEADS):
                hr = slice(h * CHUNK, (h + 1) * CHUNK)
                stb = state_ref[nb, h].astype(BF16)
                r = _dot(jnp.concatenate([ww_ref[u, hr, :], qg_ref[u, hr, :]], axis=0), stb)
                vn_ref[u, hr, :] = (uu_ref[u, hr, :] - r[:CHUNK]).astype(BF16)
                qs_ref[u, hr, :] = r[CHUNK:]
            out = qs_ref[u] + _dot(qkb_ref[u], vn_ref[u])
            for h in range(GDN_HEADS):
                hr = slice(h * CHUNK, (h + 1) * CHUNK)
                state_ref[nb, h] = (state_ref[nb, h] * egl_ref[u, h * CHUNK:h * CHUNK + 1, :]
                                    + _dot_tn(kdec_ref[u, hr, :], vn_ref[u, hr, :]))
                zz = z_ref[nb, rows, h * GDN_DV:(h + 1) * GDN_DV]
                og = _rms(out[hr, :], gnorm) * (zz * _sigmoid(zz))
                o_ref[nb, rows, h * GDN_DV:(h + 1) * GDN_DV] = og.astype(BF16)

    @pl.when(j == nj - 1)
    def _():
        sfin_ref[...] = state_ref[...]


def _gdn_masks():
    hr = GDN_HEADS * CHUNK
    i = np.arange(hr)[:, None]
    j = np.arange(hr)[None, :]
    same_head = (i // CHUNK) == (j // CHUNK)
    tril = same_head & (i >= j)
    strict = same_head & (i > j)
    same2 = strict & ((i // 2) == (j // 2))
    lvls = []
    blk = 2
    while blk < CHUNK:
        lvls.append(strict & ((i // (2 * blk)) == (j // (2 * blk))) & ((i // blk) != (j // blk)))
        blk *= 2
    to = np.concatenate([tril, same_head], axis=0)
    f = lambda a: jnp.asarray(a.astype(np.float32))
    return (jnp.asarray(to.astype(np.float32), dtype=BF16), f(tril), f(strict), f(same2),
            jnp.asarray(np.stack(lvls).astype(np.float32), dtype=BF16))


def _gdn(u3, small3, z3, conv_w, conv_past, s0, gnorm, NB, LC):
    B, S, _ = u3.shape
    C = LC // CHUNK
    U = NB * C
    HR = GDN_HEADS * CHUNK
    masks = _gdn_masks()
    tile = lambda n: pl.BlockSpec((NB, LC, n), lambda b, j: (b, j, 0))
    stspec = pl.BlockSpec((NB, GDN_HEADS, GDN_DK, GDN_DV), lambda b, j: (b, 0, 0, 0))
    cvspec = pl.BlockSpec((NB, CONV_W - 1, CONV_DIM), lambda b, j: (b, 0, 0))
    vm = lambda shape, dt: pltpu.VMEM(shape, dt)
    return pl.pallas_call(
        functools.partial(_gdn_kernel, NB=NB, LC=LC),
        grid=(B // NB, S // LC),
        in_specs=[tile(CONV_DIM), tile(LANES), tile(GDN_VW), _const_spec(conv_w.shape), cvspec, stspec,
                  _const_spec(gnorm.shape)] + [_const_spec(m.shape) for m in masks],
        out_specs=[tile(GDN_VW), stspec, cvspec],
        out_shape=[jax.ShapeDtypeStruct((B, S, GDN_VW), BF16),
                   jax.ShapeDtypeStruct((B, GDN_HEADS, GDN_DK, GDN_DV), F32),
                   jax.ShapeDtypeStruct((B, CONV_W - 1, CONV_DIM), F32)],
        scratch_shapes=[vm((NB, LC + 8, CONV_DIM), F32), vm((NB, LC, CONV_DIM), F32),
                        vm((NB, GDN_HEADS, GDN_DK, GDN_DV), F32),
                        vm((U, HR, LANES), F32), vm((U, HR, LANES), F32), vm((U, HR, LANES), BF16),
                        vm((U, HR, LANES), F32), vm((U, HR, LANES), F32),
                        vm((U, HR, LANES), F32), vm((U, HR, LANES), F32),
                        vm((U, HR, HR), F32), vm((U, HR, HR), F32),
                        vm((U, HR, HR), BF16), vm((U, HR, HR), F32), vm((U, HR, HR), BF16),
                        vm((U, HR, LANES), BF16), vm((U, HR, LANES), BF16), vm((U, HR, LANES), BF16),
                        vm((U, HR, LANES), F32), vm((U, HR, HR), BF16),
                        vm((U, HR, LANES), F32), vm((U, HR, LANES), BF16),
                        vm((U, HR, LANES), BF16), vm((U, HR, LANES), F32)],
        compiler_params=_cparams(("parallel", "arbitrary")),
        name="gdn",
    )(u3, small3, z3, conv_w, conv_past, s0, gnorm, *masks)


def _kvproj_kernel(lat_ref, kpe_ref, wkvb_ref, knn_ref, kn_ref, kr_ref, v_ref, *, v_transposed):
    kv = _dot(lat_ref[...].astype(BF16), wkvb_ref[...])
    hw = MLA_HEADS * NOPE_DIM
    for h in range(MLA_HEADS):
        kh = kv[:, h * NOPE_DIM:(h + 1) * NOPE_DIM]
        kn_ref[:, h * NOPE_DIM:(h + 1) * NOPE_DIM] = _rms(kh, knn_ref[...]).astype(BF16)
    v = kv[:, hw:]
    v_ref[...] = (v.T if v_transposed else v).astype(BF16)
    kp = kpe_ref[...]
    lane = lax.broadcasted_iota(jnp.int32, kp.shape, 1)
    kr_ref[...] = jnp.where(lane < ROPE_DIM, kp, 0.0).astype(BF16)


def _kvproj(lat3, kpe3, w_kvb, knn, tm, v_transposed):
    B, sk, _ = lat3.shape
    hw = MLA_HEADS * V_DIM
    row = lambda n: pl.BlockSpec((None, tm, n), lambda b, i: (b, i, 0))
    if v_transposed:
        vspec, vshape = pl.BlockSpec((None, hw, tm), lambda b, i: (b, 0, i)), (B, hw, sk)
    else:
        vspec, vshape = row(hw), (B, sk, hw)
    return pl.pallas_call(
        functools.partial(_kvproj_kernel, v_transposed=v_transposed),
        grid=(B, sk // tm),
        in_specs=[row(KV_RANK), row(LANES), _const_spec(w_kvb.shape), _const_spec(knn.shape)],
        out_specs=[row(MLA_HEADS * NOPE_DIM), row(LANES), vspec],
        out_shape=[jax.ShapeDtypeStruct((B, sk, MLA_HEADS * NOPE_DIM), BF16),
                   jax.ShapeDtypeStruct((B, sk, LANES), BF16), jax.ShapeDtypeStruct(vshape, BF16)],
        compiler_params=_cparams(("parallel", "parallel")),
        name="kvproj",
    )(lat3, kpe3, w_kvb, knn)


def _last_kblock(qi, tq, tk, P, nk):
    last_key = ((P + qi * tq + tq - 1) // CHUNK) * CHUNK + CHUNK - 1
    return jnp.minimum(last_key // tk, nk - 1)


def _flash_kernel(q_ref, kn_ref, kr_ref, v_ref, o_ref, m_ref, l_ref, acc_ref, *, tq, tk, P, S, nk):
    qi = pl.program_id(1)
    ki = pl.program_id(2)

    @pl.when(ki == 0)
    def _():
        m_ref[...] = jnp.full(m_ref.shape, NEG_BIG, F32)
        l_ref[...] = jnp.zeros(l_ref.shape, F32)
        acc_ref[...] = jnp.zeros(acc_ref.shape, F32)

    @pl.when(ki <= _last_kblock(qi, tq, tk, P, nk))
    def _():
        qpos = P + qi * tq + lax.broadcasted_iota(jnp.int32, (tq, tk), 0)
        kpos = ki * tk + lax.broadcasted_iota(jnp.int32, (tq, tk), 1)
        mask = ((kpos // CHUNK) <= (qpos // CHUNK)) & (kpos < P + S)
        kr = kr_ref[...]
        for h in range(MLA_HEADS):
            qh = q_ref[:, h * Q_SLAB:(h + 1) * Q_SLAB]
            kh = jnp.concatenate([kn_ref[:, h * NOPE_DIM:(h + 1) * NOPE_DIM], kr], axis=1)
            s = jnp.where(mask, _dot_nt(qh, kh), NEG_BIG)
            m_prev = m_ref[h]
            m_new = jnp.maximum(m_prev, jnp.max(s, axis=-1, keepdims=True))
            alpha = jnp.exp(m_prev - m_new)
            p = jnp.exp(s - m_new)
            l_ref[h] = alpha * l_ref[h] + jnp.sum(p, axis=-1, keepdims=True)
            acc_ref[h] = alpha * acc_ref[h] + _dot(p.astype(BF16), v_ref[:, h * V_DIM:(h + 1) * V_DIM])
            m_ref[h] = m_new

    @pl.when(ki == nk - 1)
    def _():
        for h in range(MLA_HEADS):
            o_ref[:, h * V_DIM:(h + 1) * V_DIM] = (acc_ref[h] / l_ref[h]).astype(BF16)


def _flash(q3, kn3, kr3, v3, P, S, tq, tk):
    B = q3.shape[0]
    skp = kn3.shape[1]
    nk = skp // tk
    kmap = lambda b, qi, ki: (b, jnp.minimum(ki, _last_kblock(qi, tq, tk, P, nk)), 0)
    return pl.pallas_call(
        functools.partial(_flash_kernel, tq=tq, tk=tk, P=P, S=S, nk=nk),
        grid=(B, S // tq, nk),
        in_specs=[pl.BlockSpec((None, tq, MLA_HEADS * Q_SLAB), lambda b, qi, ki: (b, qi, 0)),
                  pl.BlockSpec((None, tk, MLA_HEADS * NOPE_DIM), kmap),
                  pl.BlockSpec((None, tk, LANES), kmap),
                  pl.BlockSpec((None, tk, MLA_HEADS * V_DIM), kmap)],
        out_specs=pl.BlockSpec((None, tq, MLA_HEADS * V_DIM), lambda b, qi, ki: (b, qi, 0)),
        out_shape=jax.ShapeDtypeStruct((B, S, MLA_HEADS * V_DIM), BF16),
        scratch_shapes=[pltpu.VMEM((MLA_HEADS, tq, 1), F32), pltpu.VMEM((MLA_HEADS, tq, 1), F32),
                        pltpu.VMEM((MLA_HEADS, tq, V_DIM), F32)],
        compiler_params=_cparams(("parallel", "parallel", "arbitrary")),
        name="mla_attn",
    )(q3, kn3, kr3, v3)


def _flash_t_kernel(q_ref, kn_ref, kr_ref, vt_ref, o_ref, m_ref, l_ref, acc_ref, *, tq, tk, P, S, nk):
    qi = pl.program_id(1)
    ki = pl.program_id(2)
    q0 = P + qi * tq
    k0 = ki * tk

    @pl.when(ki == 0)
    def _():
        m_ref[...] = jnp.full(m_ref.shape, NEG_BIG, F32)
        l_ref[...] = jnp.zeros(l_ref.shape, F32)
        acc_ref[...] = jnp.zeros(acc_ref.shape, F32)

    def step(masked):
        kr = kr_ref[...]
        if masked:
            kpos = k0 + lax.broadcasted_iota(jnp.int32, (tk, 1), 0)
            qpos = q0 + lax.broadcasted_iota(jnp.int32, (1, tq), 1)
            mask = ((kpos // CHUNK) <= (qpos // CHUNK)) & (kpos < P + S)
        for h in range(MLA_HEADS):
            kh = jnp.concatenate([kn_ref[:, h * NOPE_DIM:(h + 1) * NOPE_DIM], kr], axis=1)
            st = _dot_nt(kh, q_ref[:, h * Q_SLAB:(h + 1) * Q_SLAB])
            if masked:
                st = jnp.where(mask, st, NEG_BIG)
            m_prev = m_ref[h]
            m_new = jnp.maximum(m_prev, jnp.max(st, axis=0, keepdims=True))
            alpha = jnp.exp(m_prev - m_new)
            p = jnp.exp(st - m_new)
            l_ref[h] = alpha * l_ref[h] + jnp.sum(p, axis=0, keepdims=True)
            acc_ref[h] = alpha * acc_ref[h] + _dot(vt_ref[h * V_DIM:(h + 1) * V_DIM, :], p.astype(BF16))
            m_ref[h] = m_new

    needed = ki <= _last_kblock(qi, tq, tk, P, nk)
    full = ((k0 + tk - 1) // CHUNK <= q0 // CHUNK) & (k0 + tk <= P + S)
    pl.when(needed & full)(functools.partial(step, False))
    pl.when(needed & jnp.logical_not(full))(functools.partial(step, True))

    @pl.when(ki == nk - 1)
    def _():
        for h in range(MLA_HEADS):
            o_ref[:, h * V_DIM:(h + 1) * V_DIM] = (acc_ref[h] / l_ref[h]).T.astype(BF16)


def _flash_t(q3, kn3, kr3, vt3, P, S, tq, tk):
    B = q3.shape[0]
    skp = kn3.shape[1]
    nk = skp // tk
    kblk = lambda qi, ki: jnp.minimum(ki, _last_kblock(qi, tq, tk, P, nk))
    kmap = lambda b, qi, ki: (b, kblk(qi, ki), 0)
    return pl.pallas_call(
        functools.partial(_flash_t_kernel, tq=tq, tk=tk, P=P, S=S, nk=nk),
        grid=(B, S // tq, nk),
        in_specs=[pl.BlockSpec((None, tq, MLA_HEADS * Q_SLAB), lambda b, qi, ki: (b, qi, 0)),
                  pl.BlockSpec((None, tk, MLA_HEADS * NOPE_DIM), kmap),
                  pl.BlockSpec((None, tk, LANES), kmap),
                  pl.BlockSpec((None, MLA_HEADS * V_DIM, tk), lambda b, qi, ki: (b, 0, kblk(qi, ki)))],
        out_specs=pl.BlockSpec((None, tq, MLA_HEADS * V_DIM), lambda b, qi, ki: (b, qi, 0)),
        out_shape=jax.ShapeDtypeStruct((B, S, MLA_HEADS * V_DIM), BF16),
        scratch_shapes=[pltpu.VMEM((MLA_HEADS, 1, tq), F32), pltpu.VMEM((MLA_HEADS, 1, tq), F32),
                        pltpu.VMEM((MLA_HEADS, V_DIM, tq), F32)],
        compiler_params=_cparams(("parallel", "parallel", "arbitrary")),
        name="mla_attn_t",
    )(q3, kn3, kr3, vt3)


def _memkv_kernel(mem_ref, mnorm_ref, wmk_ref, wmv_ref, mkn_ref, k_ref, v_ref):
    mb = _rms(mem_ref[...], mnorm_ref[...]).astype(BF16)
    kf = _dot(mb, wmk_ref[...])
    for h in range(MEM_HEADS):
        sl = slice(h * MEM_HEAD_DIM, (h + 1) * MEM_HEAD_DIM)
        k_ref[:, sl] = _rms(kf[:, sl], mkn_ref[...])
    v_ref[...] = _dot(mb, wmv_ref[...])


def _memkv(mem2d, mnorm, w_mk, w_mv, mkn, tm):
    T = mem2d.shape[0]
    hw = MEM_HEADS * MEM_HEAD_DIM
    row = lambda n: pl.BlockSpec((tm, n), lambda i: (i, 0))
    return pl.pallas_call(
        _memkv_kernel,
        grid=(T // tm,),
        in_specs=[row(D_MODEL), _const_spec(mnorm.shape), _const_spec(w_mk.shape), _const_spec(w_mv.shape),
                  _const_spec(mkn.shape)],
        out_specs=[row(hw), row(hw)],
        out_shape=[jax.ShapeDtypeStruct((T, hw), F32), jax.ShapeDtypeStruct((T, hw), F32)],
        compiler_params=_cparams(("parallel",)),
        name="mem_kv",
    )(mem2d, mnorm, w_mk, w_mv, mkn)


def _mix_kernel(x_ref, og_ref, om_ref, mk_ref, mv_ref, wout_ref, nmem_ref, wmq_ref, mqn_ref, wmo_ref, nffn_ref,
                wrh_ref, wrl_ref, br_ref, x2_ref, h3_ref, idx_ref, gate_ref, counts_ref, cnt_ref):
    x1 = x_ref[...] + _dot(og_ref[...], wout_ref[0:GDN_VW, :]) + _dot(om_ref[...], wout_ref[GDN_VW:, :])
    hb = _rms(x1, nmem_ref[...]).astype(BF16)
    qm = _dot(hb, wmq_ref[...])
    heads = []
    for h in range(MEM_HEADS):
        sl = slice(h * MEM_HEAD_DIM, (h + 1) * MEM_HEAD_DIM)
        qh = (_rms(qm[:, sl], mqn_ref[...]) * (MEM_HEAD_DIM ** -0.5)).astype(BF16)
        s = _dot_nt(qh, mk_ref[:, sl].astype(BF16))
        p = jnp.exp(s - jnp.max(s, axis=-1, keepdims=True))
        p = p / jnp.sum(p, axis=-1, keepdims=True)
        heads.append(_dot(p.astype(BF16), mv_ref[:, sl].astype(BF16)).astype(BF16))
    x2 = x1 + _dot(jnp.concatenate(heads, axis=1), wmo_ref[...])
    x2_ref[...] = x2
    h3 = _rms(x2, nffn_ref[...])
    hi = h3.astype(BF16)
    packed = _pack_bf16_pairs(hi.astype(F32))
    for c in range(h3_ref.shape[0]):
        h3_ref[c] = packed[:, c * LANES:(c + 1) * LANES]
    lo = (h3 - hi.astype(F32)).astype(BF16)
    wrh = wrh_ref[...]
    logits = _dot(hi, wrh) + _dot(lo, wrh) + _dot(hi, wrl_ref[...]) + br_ref[...]

    lane = lax.broadcasted_iota(jnp.int32, logits.shape, 1).astype(F32)
    vals, idxs = [], []
    cur = logits
    for _ in range(TOP_K):
        mx = jnp.max(cur, axis=-1, keepdims=True)
        ix = jnp.min(jnp.where(cur == mx, lane, float(LANES)), axis=-1, keepdims=True)
        vals.append(mx)
        idxs.append(ix)
        cur = jnp.where(lane == ix, -3e38, cur)
    es = [jnp.exp(v - vals[0]) for v in vals]
    den = es[0] + es[1] + es[2] + es[3]

    first = (pl.program_id(0) == 0) & (pl.program_id(1) == 0)

    @pl.when(first)
    def _():
        cnt_ref[...] = jnp.zeros(cnt_ref.shape, F32)

    sel = jnp.zeros(logits.shape, F32)
    for k in range(TOP_K):
        sel = sel + jnp.where(lane == idxs[k], 1.0, 0.0)
    tm = logits.shape[0]
    ri = lax.broadcasted_iota(jnp.int32, (tm, tm), 0)
    ci = lax.broadcasted_iota(jnp.int32, (tm, tm), 1)
    before = jnp.where(ri > ci, 1.0, 0.0).astype(BF16)
    excl = _dot(before, sel.astype(BF16)) + cnt_ref[...]
    cnt_ref[...] = cnt_ref[...] + jnp.sum(sel, axis=0, keepdims=True)
    counts_ref[...] = cnt_ref[...].astype(jnp.int32)

    idx_out = jnp.zeros(logits.shape, F32)
    gate_out = jnp.zeros(logits.shape, F32)
    for k in range(TOP_K):
        rank = jnp.sum(jnp.where(lane == idxs[k], excl, 0.0), axis=-1, keepdims=True)
        idx_out = jnp.where(lane == float(k), idxs[k], idx_out)
        idx_out = jnp.where(lane == float(TOP_K + k), rank, idx_out)
        gate_out = jnp.where(lane == float(k), es[k] / den, gate_out)
    idx_ref[...] = idx_out.astype(jnp.int32)
    gate_ref[...] = gate_out


def _mix(x3, og3, om3, mk3, mv3, pw, tm):
    B, S, _ = x3.shape
    hw = MEM_HEADS * MEM_HEAD_DIM
    tile = lambda n: pl.BlockSpec((None, tm, n), lambda b, i: (b, i, 0))
    memspec = pl.BlockSpec((None, N_MEM, hw), lambda b, i: (b, 0, 0))
    consts = [pw['w_out'], pw['nmem'], pw['w_mq'], pw['mqn'], pw['w_mo'], pw['nffn'], pw['wr_hi'], pw['wr_lo'],
              pw['b_r']]
    return pl.pallas_call(
        _mix_kernel,
        grid=(B, S // tm),
        in_specs=[tile(D_MODEL), tile(GDN_VW), tile(MLA_HEADS * V_DIM), memspec, memspec]
                 + [_const_spec(c.shape) for c in consts],
        out_specs=[tile(D_MODEL), pl.BlockSpec((PK_CHUNKS, tm, LANES), lambda b, i: (0, b * (S // tm) + i, 0)),
                   tile(LANES), tile(LANES), _const_spec((1, LANES))],
        out_shape=[jax.ShapeDtypeStruct((B, S, D_MODEL), F32),
                   jax.ShapeDtypeStruct((PK_CHUNKS, B * S, LANES), jnp.uint32),
                   jax.ShapeDtypeStruct((B, S, LANES), jnp.int32), jax.ShapeDtypeStruct((B, S, LANES), F32),
                   jax.ShapeDtypeStruct((1, LANES), jnp.int32)],
        scratch_shapes=[pltpu.VMEM((1, LANES), F32)],
        compiler_params=_cparams(("arbitrary", "arbitrary")),
        name="mix_mem_router",
    )(x3, og3, om3, mk3, mv3, *consts)


def _expert_kernel(be_ref, nu_ref, rows_ref, wgu_ref, bgu_ref, wd_ref, bd_ref, y_ref, wgub_ref, wdb_ref):
    i = pl.program_id(0)
    used = i < nu_ref[0]
    new_expert = (i == 0) | (be_ref[i] != be_ref[jnp.maximum(i - 1, 0)])

    @pl.when(used & new_expert)
    def _():
        def cast(r, carry):
            rs = pl.ds(pl.multiple_of(r * LANES, LANES), LANES)
            wgub_ref[rs, :] = wgu_ref[rs, :].astype(BF16)
            wdb_ref[rs, :] = wd_ref[rs, :].astype(BF16)
            return carry
        lax.fori_loop(0, D_MODEL // LANES, cast, 0)

    @pl.when(used)
    def _():
        packed = jnp.concatenate([rows_ref[c] for c in range(PK_CHUNKS)], axis=1)
        x = _unpack_bf16_pairs(packed).astype(BF16)
        acc = None
        for c in range(D_FF // FF_CHUNK):
            gs_ = slice(c * FF_CHUNK, (c + 1) * FF_CHUNK)
            us_ = slice(D_FF + c * FF_CHUNK, D_FF + (c + 1) * FF_CHUNK)
            gt = jnp.minimum(_dot(x, wgub_ref[:, gs_]) + bgu_ref[:, gs_], SWIGLU_LIMIT)
            up = jnp.clip(_dot(x, wgub_ref[:, us_]) + bgu_ref[:, us_], -SWIGLU_LIMIT, SWIGLU_LIMIT)
            act = gt * _sigmoid(SWIGLU_ALPHA * gt) * (up + 1.0)
            part = _dot(act.astype(BF16), wdb_ref[gs_, :])
            acc = part if acc is None else acc + part
        y = acc + bd_ref[...]
        for c in range(Y_CHUNKS):
            y_ref[c] = y[:, c * LANES:(c + 1) * LANES]

    @pl.when(jnp.logical_not(used))
    def _():
        y_ref[...] = jnp.zeros(y_ref.shape, F32)


def _experts(block_e, n_used, rows, w_gu, b_gu, w_down, b_down):
    n_rows = rows.shape[1]
    nb = n_rows // MOE_ROWS
    gs = pltpu.PrefetchScalarGridSpec(
        num_scalar_prefetch=2,
        grid=(nb,),
        in_specs=[pl.BlockSpec((PK_CHUNKS, MOE_ROWS, LANES), lambda i, be, nu: (0, i, 0)),
                  pl.BlockSpec((None, D_MODEL, 2 * D_FF), lambda i, be, nu: (be[i], 0, 0)),
                  pl.BlockSpec((None, 1, 2 * D_FF), lambda i, be, nu: (be[i], 0, 0)),
                  pl.BlockSpec((None, D_FF, D_MODEL), lambda i, be, nu: (be[i], 0, 0)),
                  pl.BlockSpec((None, 1, D_MODEL), lambda i, be, nu: (be[i], 0, 0))],
        out_specs=pl.BlockSpec((Y_CHUNKS, MOE_ROWS, LANES), lambda i, be, nu: (0, i, 0)),
        scratch_shapes=[pltpu.VMEM((D_MODEL, 2 * D_FF), BF16), pltpu.VMEM((D_FF, D_MODEL), BF16)],
    )
    return pl.pallas_call(
        _expert_kernel,
        grid_spec=gs,
        out_shape=jax.ShapeDtypeStruct((Y_CHUNKS, n_rows, LANES), F32),
        compiler_params=_cparams(("arbitrary",)),
        name="moe_experts",
    )(block_e, n_used, rows, w_gu, b_gu, w_down, b_down)


def _sc_mesh():
    return plsc.VectorSubcoreMesh(core_axis_name="core", subcore_axis_name="subcore")


def _sc_scatter_rows(x3s, pos_ts, n_rows):
    C, _, L = x3s[0].shape
    K = pos_ts[0].shape[0]
    ns = len(x3s)

    @functools.partial(pl.kernel, out_type=jax.ShapeDtypeStruct((C, n_rows, L), x3s[0].dtype), mesh=_sc_mesh(),
                       scratch_types=[])
    def scatter(*refs):
        o_hbm = refs[2 * ns]
        for s in range(ns):
            x_hbm, i_hbm = refs[s], refs[ns + s]
            nwin = x3s[s].shape[1] // SC_WINDOW
            for c in range(C):
                def body(x_vmem, i_vmem, c=c):
                    for k in range(K):
                        pltpu.sync_copy(x_vmem, o_hbm.at[c].at[i_vmem.at[k]])

                pltpu.emit_pipeline(
                    body, grid=(nwin,),
                    in_specs=[pl.BlockSpec((SC_WINDOW, L), lambda i, c=c, nwin=nwin: (c * nwin + i, 0)),
                              pl.BlockSpec((K, SC_WINDOW), lambda i: (0, i))],
                    out_specs=[], core_axis_name=("core", "subcore"), dimension_semantics=(pltpu.PARALLEL,),
                )(x_hbm, i_hbm)

    return scatter(*[x.reshape(-1, L) for x in x3s], *pos_ts)


def _sc_gather_rows(table3, idxs):
    C, _, L = table3.shape
    ns = len(idxs)
    out_type = [jax.ShapeDtypeStruct((C * i.shape[0], L), table3.dtype) for i in idxs]

    @functools.partial(pl.kernel, out_type=out_type, mesh=_sc_mesh(), scratch_types=[])
    def gather(t_hbm, *refs):
        for s in range(ns):
            i_hbm, o_hbm = refs[s], refs[ns + s]
            nwin = idxs[s].shape[0] // SC_WINDOW
            for c in range(C):
                def body(i_vmem, o_vmem, c=c):
                    pltpu.sync_copy(t_hbm.at[c].at[i_vmem.at[0]], o_vmem)

                pltpu.emit_pipeline(
                    body, grid=(nwin,),
                    in_specs=[pl.BlockSpec((1, SC_WINDOW), lambda i: (0, i))],
                    out_specs=[pl.BlockSpec((SC_WINDOW, L), lambda i, c=c, nwin=nwin: (c * nwin + i, 0))],
                    core_axis_name=("core", "subcore"), dimension_semantics=(pltpu.PARALLEL,),
                )(i_hbm, o_hbm)

    outs = gather(table3, *[i.reshape(1, -1) for i in idxs])
    return [o.reshape(C, -1, L) for o in outs]


def _combine_kernel(x2_ref, g_ref, gate_ref, o_ref):
    gate = gate_ref[...]
    for c in range(Y_CHUNKS):
        cs = slice(c * LANES, (c + 1) * LANES)
        acc = x2_ref[:, cs]
        for k in range(TOP_K):
            acc = acc + g_ref[c, k] * gate[:, k:k + 1]
        o_ref[:, cs] = acc


def _combine(x2, g4, gate, tm):
    T = x2.shape[0]
    return pl.pallas_call(
        _combine_kernel,
        grid=(T // tm,),
        in_specs=[pl.BlockSpec((tm, D_MODEL), lambda i: (i, 0)),
                  pl.BlockSpec((Y_CHUNKS, TOP_K, tm, LANES), lambda i: (0, 0, i, 0)),
                  pl.BlockSpec((tm, LANES), lambda i: (i, 0))],
        out_specs=pl.BlockSpec((tm, D_MODEL), lambda i: (i, 0)),
        out_shape=jax.ShapeDtypeStruct((T, D_MODEL), F32),
        compiler_params=_cparams(("parallel",)),
        name="moe_combine",
    )(x2, g4, gate)


def _moe(streams, ew):
    cnts = [st[3][0, :N_EXPERTS] for st in streams]
    total = sum(cnts)
    padded = (total + MOE_ROWS - 1) // MOE_ROWS * MOE_ROWS
    pad_end = jnp.cumsum(padded)
    pad_start = pad_end - padded
    experts = jnp.arange(N_EXPERTS, dtype=jnp.int32)[None, None, :]
    pos_ts = []
    base = pad_start
    for (h3p, idxr, gate, counts, x2), cnt in zip(streams, cnts):
        onehot = idxr[:, :TOP_K, None] == experts
        start = jnp.sum(jnp.where(onehot, base[None, None, :], 0), axis=-1)
        pos_ts.append((start + idxr[:, TOP_K:2 * TOP_K]).T)
        base = base + cnt
    n_assign = sum(st[0].shape[1] for st in streams) * TOP_K
    nb = -(-n_assign // MOE_ROWS) + N_EXPERTS
    starts = jnp.arange(nb, dtype=jnp.int32) * MOE_ROWS
    block_e = jnp.minimum(jnp.sum((pad_end[None, :] <= starts[:, None]).astype(jnp.int32), axis=1), N_EXPERTS - 1)
    n_used = (pad_end[-1] // MOE_ROWS).astype(jnp.int32).reshape(1)
    rows = _sc_scatter_rows([st[0] for st in streams], pos_ts, nb * MOE_ROWS)
    y_rows = _experts(block_e, n_used, rows, ew['w_gu'], ew['b_gu'], ew['w_down'], ew['b_down'])
    gs = _sc_gather_rows(y_rows, [p.reshape(-1) for p in pos_ts])
    outs = []
    for (h3p, idxr, gate, counts, x2), g in zip(streams, gs):
        T = x2.shape[0]
        outs.append(_combine(x2, g.reshape(Y_CHUNKS, TOP_K, T, LANES), gate, _pick(T, (512, 256, 128, 64))))
    return outs


def _pad_lanes(v, n=LANES, fill=0.0):
    return jnp.pad(v, (0, n - v.shape[0]), constant_values=fill).reshape(1, n)


def _prep_weights(norm_mix, w_in, conv_w, a_log, dt_bias, gdn_norm, q_a_norm, w_qb, kv_a_norm, w_kvb, q_norm,
                  k_nope_norm, k_rope_norm, w_out, norm_mem, mem_norm, w_mq, w_mk, w_mv, mq_norm, mk_norm, w_mo,
                  norm_ffn, w_router, b_router, w_gu, b_gu, w_down, b_down):
    c = np.cumsum([CONV_DIM, GDN_VW, GDN_HEADS, GDN_HEADS, Q_RANK, KV_RANK])
    w_u, w_z, w_a, w_b, w_cq, w_ckv, w_kpe = [w_in[:, lo:hi] for lo, hi in
                                              zip([0, *c], [*c, w_in.shape[1]])]
    w_s = jnp.concatenate([w_kpe, w_a, w_b], axis=1)
    w_s = jnp.pad(w_s, ((0, 0), (0, LANES - w_s.shape[1])))
    wq = w_qb.reshape(Q_RANK, MLA_HEADS, QK_DIM)
    wq = jnp.pad(wq, ((0, 0), (0, 0), (0, Q_SLAB - QK_DIM))).reshape(Q_RANK, MLA_HEADS * Q_SLAB)
    wkv = w_kvb.reshape(KV_RANK, MLA_HEADS, NOPE_DIM + V_DIM)
    wkv = jnp.concatenate([wkv[:, :, :NOPE_DIM].reshape(KV_RANK, -1), wkv[:, :, NOPE_DIM:].reshape(KV_RANK, -1)], 1)
    wr = jnp.pad(w_router, ((0, 0), (0, LANES - N_EXPERTS)))
    wr_hi = wr.astype(BF16)
    wr_lo = (wr - wr_hi.astype(F32)).astype(BF16)
    row = lambda v: v.reshape(1, -1)
    gpad = ROPE_DIM
    pw = dict(
        nmix=row(norm_mix), w_u=w_u.astype(BF16), w_z=w_z.astype(BF16), w_cq=w_cq.astype(BF16),
        w_ckv=w_ckv.astype(BF16), w_s=w_s.astype(BF16), qan=row(q_a_norm), w_qb=wq.astype(BF16),
        qn=_pad_lanes(q_norm, Q_SLAB), kvan=row(kv_a_norm), krn=_pad_lanes(k_rope_norm),
        alog=jnp.pad(a_log, (gpad, LANES - gpad - GDN_HEADS)).reshape(1, LANES),
        dtb=jnp.pad(dt_bias, (gpad, LANES - gpad - GDN_HEADS)).reshape(1, LANES),
        conv_w=conv_w, gnorm=row(gdn_norm), w_kvb=wkv.astype(BF16), knn=row(k_nope_norm),
        w_out=w_out.astype(BF16), nmem=row(norm_mem), w_mq=w_mq.astype(BF16), mqn=row(mq_norm),
        w_mo=w_mo.astype(BF16), nffn=row(norm_ffn), wr_hi=wr_hi, wr_lo=wr_lo,
        b_r=_pad_lanes(b_router, LANES, NEG_BIG),
        mnorm=row(mem_norm), w_mk=w_mk.astype(BF16), w_mv=w_mv.astype(BF16), mkn=row(mk_norm),
    )
    ew = dict(w_gu=w_gu, b_gu=b_gu.reshape(N_EXPERTS, 1, 2 * D_FF), w_down=w_down,
              b_down=b_down.reshape(N_EXPERTS, 1, D_MODEL))
    return pw, ew


def _rope_tables(P, S):
    half = ROPE_DIM // 2
    inv = ROPE_THETA ** (-jnp.arange(half, dtype=F32) / half)
    ang = (P + jnp.arange(S, dtype=jnp.int32)).astype(F32)[:, None] * inv[None, :]
    cos, sin = jnp.cos(ang), jnp.sin(ang)
    zh = jnp.zeros((S, half), F32)
    zz = jnp.zeros((S, LANES - ROPE_DIM), F32)
    return (jnp.concatenate([cos, cos, zz], 1), jnp.concatenate([-sin, zh, zz], 1),
            jnp.concatenate([zh, sin, zz], 1))


def _pick(n, prefs):
    for t in prefs:
        if n % t == 0:
            return t
    return n


def _trunk_front(x, lat_past, kpe_past, s0, conv_past, mem_k, mem_v, pw):
    B, S, D = x.shape
    P = lat_past.shape[1]
    T = B * S
    tm = _pick(S, (512, 256, 128, 64))
    u, z, q, lat_new, small = _inproj(x.reshape(T, D), S, tm, pw, _rope_tables(P, S))

    LC = _pick(S, (256, 128, 64))
    NB = _pick(B, tuple(n for n in (8, 4, 2, 1) if n * (LC // CHUNK) <= GDN_UNITS))
    o_gdn, s_new, conv_new = _gdn(u.reshape(B, S, CONV_DIM), small.reshape(B, S, LANES), z.reshape(B, S, GDN_VW),
                                  pw['conv_w'], conv_past, s0, pw['gnorm'], NB, LC)

    sk = P + S
    tk = 512 if S >= 512 else -(-sk // LANES) * LANES
    skp = -(-sk // tk) * tk
    lat_all = jnp.concatenate([lat_past, lat_new.reshape(B, S, KV_RANK)], axis=1)
    kpe_new = small[:, :ROPE_DIM].reshape(B, S, ROPE_DIM)
    kpe_all = jnp.concatenate([kpe_past, kpe_new], axis=1)
    lat_all = jnp.pad(lat_all, ((0, 0), (0, skp - sk), (0, 0)))
    kpe_all = jnp.pad(kpe_all, ((0, 0), (0, skp - sk), (0, LANES - ROPE_DIM)))
    tq = _pick(S, (512, 256, 128, 64))
    key_major = tq >= LANES
    kn, kr, v = _kvproj(lat_all, kpe_all, pw['w_kvb'], pw['knn'], _pick(skp, (512,)), key_major)
    q3 = q.reshape(B, S, MLA_HEADS * Q_SLAB)
    o_mla = (_flash_t if key_major else _flash)(q3, kn, kr, v, P, S, tq, tk)

    x2, h3p, idxr, gate, counts = _mix(x, o_gdn, o_mla, mem_k, mem_v, pw, tm)
    stream = (h3p, idxr.reshape(T, LANES), gate.reshape(T, LANES), counts, x2.reshape(T, D))
    return stream, (lat_new.reshape(B, S, KV_RANK), kpe_new, s_new, conv_new)


def kernel(x_prompt, x_sample, cache_kv_latent, cache_k_rope, state_gdn, state_conv, cache_mem_k, cache_mem_v, mem_prompt, norm_mix, w_in, conv_w, a_log, dt_bias, gdn_norm, q_a_norm, w_qb, kv_a_norm, w_kvb, q_norm, k_nope_norm, k_rope_norm, w_out, norm_mem, mem_norm, w_mq, w_mk, w_mv, mq_norm, mk_norm, w_mo, norm_ffn, w_router, b_router, w_gu, b_gu, w_down, b_down):
    depth = norm_mix.shape[0]
    yp, ys = x_prompt, x_sample
    bp = x_prompt.shape[0]
    hw = MEM_HEADS * MEM_HEAD_DIM
    outs = [[] for _ in range(10)]
    for l in range(depth):
        pw, ew = _prep_weights(norm_mix[l], w_in[l], conv_w[l], a_log[l], dt_bias[l], gdn_norm[l], q_a_norm[l],
                               w_qb[l], kv_a_norm[l], w_kvb[l], q_norm[l], k_nope_norm[l], k_rope_norm[l], w_out[l],
                               norm_mem[l], mem_norm[l], w_mq[l], w_mk[l], w_mv[l], mq_norm[l], mk_norm[l], w_mo[l],
                               norm_ffn[l], w_router[l], b_router[l], w_gu[l], b_gu[l], w_down[l], b_down[l])
        nm = mem_prompt.shape[1]
        mk, mv = _memkv(mem_prompt.reshape(bp * nm, D_MODEL), pw['mnorm'], pw['w_mk'], pw['w_mv'], pw['mkn'],
                        _pick(bp * nm, (512, 256)))
        mk = mk.reshape(bp, nm, hw)
        mv = mv.reshape(bp, nm, hw)
        stream_p, (lat, kpe, s_fin, cv) = _trunk_front(
            yp, jnp.zeros((bp, 0, KV_RANK), F32), jnp.zeros((bp, 0, ROPE_DIM), F32),
            jnp.zeros((bp, GDN_HEADS, GDN_DK, GDN_DV), F32), jnp.zeros((bp, CONV_W - 1, CONV_DIM), F32), mk, mv, pw)
        bs = x_sample.shape[0]
        stream_s, (lat2, kpe2, s_fin2, cv2) = _trunk_front(
            ys, cache_kv_latent[l], cache_k_rope[l], state_gdn[l], state_conv[l],
            cache_mem_k[l].reshape(bs, nm, hw), cache_mem_v[l].reshape(bs, nm, hw), pw)
        yp2, ys2 = _moe([stream_p, stream_s], ew)
        yp, ys = yp2.reshape(yp.shape), ys2.reshape(ys.shape)
        for lst, val in zip(outs, (lat, kpe, s_fin, cv, mk.reshape(bp, nm, MEM_HEADS, MEM_HEAD_DIM),
                                   mv.reshape(bp, nm, MEM_HEADS, MEM_HEAD_DIM), lat2, kpe2, s_fin2, cv2)):
            lst.append(val)
    return (yp, ys) + tuple(jnp.stack(o) for o in outs)
```

```python
import functools
import math

import numpy as np
import jax
import jax.numpy as jnp
from jax import lax
from jax.experimental import pallas as pl
from jax.experimental.pallas import tpu as pltpu
from jax.experimental.pallas import tpu_sc as plsc

F32 = jnp.float32
BF16 = jnp.bfloat16

D_MODEL = 1024
CHUNK = 64
EPS = 1e-6
GDN_HEADS = 4
GDN_DK = 128
GDN_DV = 128
CONV_W = 4
GDN_QK = GDN_HEADS * GDN_DK
GDN_VW = GDN_HEADS * GDN_DV
CONV_DIM = 2 * GDN_QK + GDN_VW
MLA_HEADS = 4
Q_RANK = 384
KV_RANK = 256
NOPE_DIM = 128
ROPE_DIM = 64
V_DIM = 128
QK_DIM = NOPE_DIM + ROPE_DIM
ROPE_THETA = 10000.0
N_MEM = 256
MEM_HEADS = 4
MEM_HEAD_DIM = 128
N_EXPERTS = 32
TOP_K = 4
D_FF = D_MODEL
SWIGLU_ALPHA = 1.702
SWIGLU_LIMIT = 7.0

LANES = 128
Q_SLAB = 2 * LANES
NEG_BIG = -1e30
VMEM_LIMIT = 56 * 1024 * 1024
MOE_ROWS = 512
GDN_UNITS = 8
FF_CHUNK = 256
SC_WINDOW = 128
PK_CHUNKS = D_MODEL // 2 // LANES
SUBTILE_ROWS = 256


def _cparams(sem):
    return pltpu.CompilerParams(dimension_semantics=sem, vmem_limit_bytes=VMEM_LIMIT)


def _dot(a, b):
    return jnp.dot(a, b, preferred_element_type=F32)


def _dot_nt(a, b):
    return lax.dot_general(a, b, (((1,), (1,)), ((), ())), preferred_element_type=F32)


def _dot_tn(a, b):
    return lax.dot_general(a, b, (((0,), (0,)), ((), ())), preferred_element_type=F32)


def _rms(x, gain, n=None):
    n = x.shape[-1] if n is None else n
    ss = jnp.sum(x * x, axis=-1, keepdims=True) * (1.0 / n)
    return (x * lax.rsqrt(ss + EPS)) * gain


def _sigmoid(x):
    return 1.0 / (1.0 + jnp.exp(-x))


def _rope128(r, cos, sna, snb):
    return r * cos + pltpu.roll(r, 96, 1) * sna + pltpu.roll(r, 32, 1) * snb


def _pack_bf16_pairs(x):
    n = x.shape[1] // 2
    lo = pltpu.bitcast(x[:, :n], jnp.uint32) >> 16
    hi = pltpu.bitcast(x[:, n:], jnp.uint32) & jnp.uint32(0xFFFF0000)
    return lo | hi


def _unpack_bf16_pairs(p):
    lo = pltpu.bitcast(p << 16, F32)
    hi = pltpu.bitcast(p & jnp.uint32(0xFFFF0000), F32)
    return jnp.concatenate([lo, hi], axis=1)


def _subtiles(rows):
    n = rows // SUBTILE_ROWS if rows % SUBTILE_ROWS == 0 else 1
    step = rows // n
    return [slice(i * step, (i + 1) * step) for i in range(n)]


def _const_spec(shape):
    nd = len(shape)
    return pl.BlockSpec(shape, lambda *_: (0,) * nd)


def _inproj_kernel(x_ref, nmix_ref, wu_ref, wz_ref, wcq_ref, wckv_ref, ws_ref, qan_ref, wqb_ref, qn_ref,
                   kvan_ref, krn_ref, alog_ref, dtb_ref, cos_ref, sna_ref, snb_ref,
                   u_ref, z_ref, q_ref, lat_ref, small_ref):
    for rs in _subtiles(x_ref.shape[0]):
        x = x_ref[rs, :]
        hb = _rms(x, nmix_ref[...]).astype(BF16)
        u_ref[rs, :] = _dot(hb, wu_ref[...])
        z_ref[rs, :] = _dot(hb, wz_ref[...])
        cos, sna, snb = cos_ref[rs, :], sna_ref[rs, :], snb_ref[rs, :]

        cq = _rms(_dot(hb, wcq_ref[...]), qan_ref[...]).astype(BF16)
        qf = _dot(cq, wqb_ref[...])
        scale = QK_DIM ** -0.5
        for h in range(MLA_HEADS):
            slab = qf[:, h * Q_SLAB:(h + 1) * Q_SLAB]
            slab = _rms(slab, qn_ref[...], n=QK_DIM)
            nope = slab[:, :LANES]
            ropd = _rope128(slab[:, LANES:], cos, sna, snb)
            q_ref[rs, h * Q_SLAB:h * Q_SLAB + LANES] = (nope * scale).astype(BF16)
            q_ref[rs, h * Q_SLAB + LANES:(h + 1) * Q_SLAB] = (ropd * scale).astype(BF16)

        lat_ref[rs, :] = _rms(_dot(hb, wckv_ref[...]), kvan_ref[...])

        sm = _dot(hb, ws_ref[...])
        lane = lax.broadcasted_iota(jnp.int32, sm.shape, 1)
        kp = jnp.where(lane < ROPE_DIM, sm, 0.0)
        kpe = _rope128(_rms(kp, krn_ref[...], n=ROPE_DIM), cos, sna, snb)
        sp = sm + dtb_ref[...]
        softplus = jnp.maximum(sp, 0.0) + jnp.log1p(jnp.exp(-jnp.abs(sp)))
        g = -jnp.exp(alog_ref[...]) * softplus
        beta = _sigmoid(sm)
        small_ref[rs, :] = jnp.where(lane < ROPE_DIM, kpe,
                                     jnp.where(lane < ROPE_DIM + GDN_HEADS, g,
                                               jnp.where(lane < ROPE_DIM + 2 * GDN_HEADS, beta, 0.0)))


def _inproj(x2d, S, tm, pw, tabs):
    T = x2d.shape[0]
    nblk_s = S // tm
    row = lambda n: pl.BlockSpec((tm, n), lambda i: (i, 0))
    tab = pl.BlockSpec((tm, LANES), lambda i: (i % nblk_s, 0))
    consts = [pw['nmix'], pw['w_u'], pw['w_z'], pw['w_cq'], pw['w_ckv'], pw['w_s'], pw['qan'], pw['w_qb'],
              pw['qn'], pw['kvan'], pw['krn'], pw['alog'], pw['dtb']]
    return pl.pallas_call(
        _inproj_kernel,
        grid=(T // tm,),
        in_specs=[row(D_MODEL)] + [_const_spec(c.shape) for c in consts] + [tab, tab, tab],
        out_specs=[row(CONV_DIM), row(GDN_VW), row(MLA_HEADS * Q_SLAB), row(KV_RANK), row(LANES)],
        out_shape=[jax.ShapeDtypeStruct((T, CONV_DIM), F32), jax.ShapeDtypeStruct((T, GDN_VW), F32),
                   jax.ShapeDtypeStruct((T, MLA_HEADS * Q_SLAB), BF16), jax.ShapeDtypeStruct((T, KV_RANK), F32),
                   jax.ShapeDtypeStruct((T, LANES), F32)],
        compiler_params=_cparams(("parallel",)),
        name="inproj",
    )(x2d, *consts, *tabs)


def _split3(x):
    hi = x.astype(BF16)
    r = x - hi.astype(F32)
    mid = r.astype(BF16)
    lo = (r - mid.astype(F32)).astype(BF16)
    return hi, mid, lo


def _gdn_kernel(u_ref, small_ref, z_ref, convw_ref, cpast_ref, s0_ref, gnorm_ref,
                to_ref, trilm_ref, strictm_ref, same2_ref, lvl_ref,
                o_ref, sfin_ref, cnew_ref, ext_ref, uc_ref, state_ref,
                qf_ref, kf_ref, vb_ref, bt_ref, gg_ref, gcum_ref, glast_ref, kk_ref, qk_ref, mb_ref, x_ref, qkb_ref,
                kbe_ref, qg_ref, kdec_ref, egl_ref, t1_ref, uu_ref, ww_ref, vn_ref, qs_ref, *, NB, LC):
    j = pl.program_id(1)
    nj = pl.num_programs(1)
    PADR = 8
    C = LC // CHUNK
    U = NB * C
    HR = GDN_HEADS * CHUNK

    @pl.when(j == 0)
    def _():
        state_ref[...] = s0_ref[...]
        ext_ref[:, PADR - (CONV_W - 1):PADR, :] = cpast_ref[...]

    w = convw_ref[...]
    for nb in range(NB):
        ext_ref[nb, PADR:PADR + LC, :] = u_ref[nb]
        acc = ext_ref[nb, PADR:PADR + LC, :] * w[CONV_W - 1:CONV_W, :]
        for t in range(1, CONV_W):
            acc = acc + ext_ref[nb, PADR - t:PADR - t + LC, :] * w[CONV_W - 1 - t:CONV_W - t, :]
        uc_ref[nb] = acc * _sigmoid(acc)
        ext_ref[nb, 0:PADR, :] = ext_ref[nb, LC:LC + PADR, :]

    @pl.when(j == nj - 1)
    def _():
        cnew_ref[...] = ext_ref[:, PADR - (CONV_W - 1):PADR, :]

    units = [(nb, c) for nb in range(NB) for c in range(C)]
    g0 = ROPE_DIM
    b0 = ROPE_DIM + GDN_HEADS
    for u, (nb, c) in enumerate(units):
        rows = slice(c * CHUNK, (c + 1) * CHUNK)
        sm = small_ref[nb, rows, :]
        for h in range(GDN_HEADS):
            hr = slice(h * CHUNK, (h + 1) * CHUNK)
            q = uc_ref[nb, rows, h * GDN_DK:(h + 1) * GDN_DK]
            k = uc_ref[nb, rows, GDN_QK + h * GDN_DK:GDN_QK + (h + 1) * GDN_DK]
            v = uc_ref[nb, rows, 2 * GDN_QK + h * GDN_DV:2 * GDN_QK + (h + 1) * GDN_DV]
            beta = jnp.broadcast_to(sm[:, b0 + h:b0 + h + 1], (CHUNK, LANES))
            qf_ref[u, hr, :] = (q * lax.rsqrt(jnp.sum(q * q, -1, keepdims=True) + EPS)) * (GDN_DK ** -0.5)
            kf_ref[u, hr, :] = k * lax.rsqrt(jnp.sum(k * k, -1, keepdims=True) + EPS)
            vb_ref[u, hr, :] = (v * beta).astype(BF16)
            bt_ref[u, hr, :] = beta
            gg_ref[u, hr, :] = jnp.broadcast_to(sm[:, g0 + h:g0 + h + 1], (CHUNK, LANES))

    to = to_ref[...]
    for u in range(U):
        gl = sum(_dot(to, part) for part in _split3(gg_ref[u]))
        gcum_ref[u] = gl[:HR, :]
        glast_ref[u] = gl[HR:, :]
        k = kf_ref[u]
        kbf = k.astype(BF16)
        kk_ref[u] = _dot_nt((k * bt_ref[u]).astype(BF16), kbf)
        qk_ref[u] = _dot_nt(qf_ref[u].astype(BF16), kbf)

    tril = trilm_ref[...] > 0.0
    strict = strictm_ref[...] > 0.0
    same2 = same2_ref[...] > 0.0
    eye = trilm_ref[...] - strictm_ref[...]
    for u in range(U):
        gcum = gcum_ref[u]
        grow = gcum.T[0:1, :]
        gcol = jnp.concatenate([gcum, gcum], axis=1)
        decay = jnp.where(tril, jnp.exp(jnp.where(tril, gcol - grow, 0.0)), 0.0)
        m = jnp.where(strict, kk_ref[u] * decay, 0.0)
        mb_ref[u] = m.astype(BF16)
        x_ref[u] = eye - jnp.where(same2, m, 0.0)
        qkb_ref[u] = jnp.where(tril, qk_ref[u] * decay, 0.0).astype(BF16)
        egc = jnp.exp(gcum)
        k = kf_ref[u]
        kbe_ref[u] = (k * bt_ref[u] * egc).astype(BF16)
        qg_ref[u] = (qf_ref[u] * egc).astype(BF16)
        kdec_ref[u] = (k * jnp.exp(glast_ref[u] - gcum)).astype(BF16)
        egl_ref[u] = jnp.exp(glast_ref[u])

    for lvl in range(lvl_ref.shape[0]):
        lm = lvl_ref[lvl]
        for u in range(U):
            t1_ref[u] = _dot(mb_ref[u] * lm, x_ref[u].astype(BF16)).astype(BF16)
        for u in range(U):
            x = x_ref[u]
            x_ref[u] = x - _dot(x.astype(BF16), t1_ref[u])

    for u in range(U):
        xb = x_ref[u].astype(BF16)
        uu_ref[u] = _dot(xb, vb_ref[u])
        ww_ref[u] = _dot(xb, kbe_ref[u]).astype(BF16)

    gnorm = gnorm_ref[...]
    for c in range(C):
        for nb in range(NB):
            u = nb * C + c
            rows = slice(c * CHUNK, (c + 1) * CHUNK)
            for h in range(GDN_HEADS):
                hr = slice(h * CHUNK, (h + 1) * CHUNK)
                stb = state_ref[nb, h].astype(BF16)
                r = _dot(jnp.concatenate([ww_ref[u, hr, :], qg_ref[u, hr, :]], axis=0), stb)
                vn_ref[u, hr, :] = (uu_ref[u, hr, :] - r[:CHUNK]).astype(BF16)
                qs_ref[u, hr, :] = r[CHUNK:]
            out = qs_ref[u] + _dot(qkb_ref[u], vn_ref[u])
            for h in range(GDN_HEADS):
                hr = slice(h * CHUNK, (h + 1) * CHUNK)
                state_ref[nb, h] = (state_ref[nb, h] * egl_ref[u, h * CHUNK:h * CHUNK + 1, :]
                                    + _dot_tn(kdec_ref[u, hr, :], vn_ref[u, hr, :]))
                zz = z_ref[nb, rows, h * GDN_DV:(h + 1) * GDN_DV]
                og = _rms(out[hr, :], gnorm) * (zz * _sigmoid(zz))
                o_ref[nb, rows, h * GDN_DV:(h + 1) * GDN_DV] = og.astype(BF16)

    @pl.when(j == nj - 1)
    def _():
        sfin_ref[...] = state_ref[...]


def _gdn_masks():
    hr = GDN_HEADS * CHUNK
    i = np.arange(hr)[:, None]
    j = np.arange(hr)[None, :]
    same_head = (i // CHUNK) == (j // CHUNK)
    tril = same_head & (i >= j)
    strict = same_head & (i > j)
    same2 = strict & ((i // 2) == (j // 2))
    lvls = []
    blk = 2
    while blk < CHUNK:
        lvls.append(strict & ((i // (2 * blk)) == (j // (2 * blk))) & ((i // blk) != (j // blk)))
        blk *= 2
    to = np.concatenate([tril, same_head], axis=0)
    f = lambda a: jnp.asarray(a.astype(np.float32))
    return (jnp.asarray(to.astype(np.float32), dtype=BF16), f(tril), f(strict), f(same2),
            jnp.asarray(np.stack(lvls).astype(np.float32), dtype=BF16))


def _gdn(u3, small3, z3, conv_w, conv_past, s0, gnorm, NB, LC):
    B, S, _ = u3.shape
    C = LC // CHUNK
    U = NB * C
    HR = GDN_HEADS * CHUNK
    masks = _gdn_masks()
    tile = lambda n: pl.BlockSpec((NB, LC, n), lambda b, j: (b, j, 0))
    stspec = pl.BlockSpec((NB, GDN_HEADS, GDN_DK, GDN_DV), lambda b, j: (b, 0, 0, 0))
    cvspec = pl.BlockSpec((NB, CONV_W - 1, CONV_DIM), lambda b, j: (b, 0, 0))
    vm = lambda shape, dt: pltpu.VMEM(shape, dt)
    return pl.pallas_call(
        functools.partial(_gdn_kernel, NB=NB, LC=LC),
        grid=(B // NB, S // LC),
        in_specs=[tile(CONV_DIM), tile(LANES), tile(GDN_VW), _const_spec(conv_w.shape), cvspec, stspec,
                  _const_spec(gnorm.shape)] + [_const_spec(m.shape) for m in masks],
        out_specs=[tile(GDN_VW), stspec, cvspec],
        out_shape=[jax.ShapeDtypeStruct((B, S, GDN_VW), BF16),
                   jax.ShapeDtypeStruct((B, GDN_HEADS, GDN_DK, GDN_DV), F32),
                   jax.ShapeDtypeStruct((B, CONV_W - 1, CONV_DIM), F32)],
        scratch_shapes=[vm((NB, LC + 8, CONV_DIM), F32), vm((NB, LC, CONV_DIM), F32),
                        vm((NB, GDN_HEADS, GDN_DK, GDN_DV), F32),
                        vm((U, HR, LANES), F32), vm((U, HR, LANES), F32), vm((U, HR, LANES), BF16),
                        vm((U, HR, LANES), F32), vm((U, HR, LANES), F32),
                        vm((U, HR, LANES), F32), vm((U, HR, LANES), F32),
                        vm((U, HR, HR), F32), vm((U, HR, HR), F32),
                        vm((U, HR, HR), BF16), vm((U, HR, HR), F32), vm((U, HR, HR), BF16),
                        vm((U, HR, LANES), BF16), vm((U, HR, LANES), BF16), vm((U, HR, LANES), BF16),
                        vm((U, HR, LANES), F32), vm((U, HR, HR), BF16),
                        vm((U, HR, LANES), F32), vm((U, HR, LANES), BF16),
                        vm((U, HR, LANES), BF16), vm((U, HR, LANES), F32)],
        compiler_params=_cparams(("parallel", "arbitrary")),
        name="gdn",
    )(u3, small3, z3, conv_w, conv_past, s0, gnorm, *masks)


def _kvproj_kernel(lat_ref, kpe_ref, wkvb_ref, knn_ref, kn_ref, kr_ref, v_ref, *, v_transposed):
    kv = _dot(lat_ref[...].astype(BF16), wkvb_ref[...])
    hw = MLA_HEADS * NOPE_DIM
    for h in range(MLA_HEADS):
        kh = kv[:, h * NOPE_DIM:(h + 1) * NOPE_DIM]
        kn_ref[:, h * NOPE_DIM:(h + 1) * NOPE_DIM] = _rms(kh, knn_ref[...]).astype(BF16)
    v = kv[:, hw:]
    v_ref[...] = (v.T if v_transposed else v).astype(BF16)
    kp = kpe_ref[...]
    lane = lax.broadcasted_iota(jnp.int32, kp.shape, 1)
    kr_ref[...] = jnp.where(lane < ROPE_DIM, kp, 0.0).astype(BF16)


def _kvproj(lat3, kpe3, w_kvb, knn, tm, v_transposed):
    B, sk, _ = lat3.shape
    hw = MLA_HEADS * V_DIM
    row = lambda n: pl.BlockSpec((None, tm, n), lambda b, i: (b, i, 0))
    if v_transposed:
        vspec, vshape = pl.BlockSpec((None, hw, tm), lambda b, i: (b, 0, i)), (B, hw, sk)
    else:
        vspec, vshape = row(hw), (B, sk, hw)
    return pl.pallas_call(
        functools.partial(_kvproj_kernel, v_transposed=v_transposed),
        grid=(B, sk // tm),
        in_specs=[row(KV_RANK), row(LANES), _const_spec(w_kvb.shape), _const_spec(knn.shape)],
        out_specs=[row(MLA_HEADS * NOPE_DIM), row(LANES), vspec],
        out_shape=[jax.ShapeDtypeStruct((B, sk, MLA_HEADS * NOPE_DIM), BF16),
                   jax.ShapeDtypeStruct((B, sk, LANES), BF16), jax.ShapeDtypeStruct(vshape, BF16)],
        compiler_params=_cparams(("parallel", "parallel")),
        name="kvproj",
    )(lat3, kpe3, w_kvb, knn)


def _last_kblock(qi, tq, tk, P, nk):
    last_key = ((P + qi * tq + tq - 1) // CHUNK) * CHUNK + CHUNK - 1
    return jnp.minimum(last_key // tk, nk - 1)


def _flash_kernel(q_ref, kn_ref, kr_ref, v_ref, o_ref, m_ref, l_ref, acc_ref, *, tq, tk, P, S, nk):
    qi = pl.program_id(1)
    ki = pl.program_id(2)

    @pl.when(ki == 0)
    def _():
        m_ref[...] = jnp.full(m_ref.shape, NEG_BIG, F32)
        l_ref[...] = jnp.zeros(l_ref.shape, F32)
        acc_ref[...] = jnp.zeros(acc_ref.shape, F32)

    @pl.when(ki <= _last_kblock(qi, tq, tk, P, nk))
    def _():
        qpos = P + qi * tq + lax.broadcasted_iota(jnp.int32, (tq, tk), 0)
        kpos = ki * tk + lax.broadcasted_iota(jnp.int32, (tq, tk), 1)
        mask = ((kpos // CHUNK) <= (qpos // CHUNK)) & (kpos < P + S)
        kr = kr_ref[...]
        for h in range(MLA_HEADS):
            qh = q_ref[:, h * Q_SLAB:(h + 1) * Q_SLAB]
            kh = jnp.concatenate([kn_ref[:, h * NOPE_DIM:(h + 1) * NOPE_DIM], kr], axis=1)
            s = jnp.where(mask, _dot_nt(qh, kh), NEG_BIG)
            m_prev = m_ref[h]
            m_new = jnp.maximum(m_prev, jnp.max(s, axis=-1, keepdims=True))
            alpha = jnp.exp(m_prev - m_new)
            p = jnp.exp(s - m_new)
            l_ref[h] = alpha * l_ref[h] + jnp.sum(p, axis=-1, keepdims=True)
            acc_ref[h] = alpha * acc_ref[h] + _dot(p.astype(BF16), v_ref[:, h * V_DIM:(h + 1) * V_DIM])
            m_ref[h] = m_new

    @pl.when(ki == nk - 1)
    def _():
        for h in range(MLA_HEADS):
            o_ref[:, h * V_DIM:(h + 1) * V_DIM] = (acc_ref[h] / l_ref[h]).astype(BF16)


def _flash(q3, kn3, kr3, v3, P, S, tq, tk):
    B = q3.shape[0]
    skp = kn3.shape[1]
    nk = skp // tk
    kmap = lambda b, qi, ki: (b, jnp.minimum(ki, _last_kblock(qi, tq, tk, P, nk)), 0)
    return pl.pallas_call(
        functools.partial(_flash_kernel, tq=tq, tk=tk, P=P, S=S, nk=nk),
        grid=(B, S // tq, nk),
        in_specs=[pl.BlockSpec((None, tq, MLA_HEADS * Q_SLAB), lambda b, qi, ki: (b, qi, 0)),
                  pl.BlockSpec((None, tk, MLA_HEADS * NOPE_DIM), kmap),
                  pl.BlockSpec((None, tk, LANES), kmap),
                  pl.BlockSpec((None, tk, MLA_HEADS * V_DIM), kmap)],
        out_specs=pl.BlockSpec((None, tq, MLA_HEADS * V_DIM), lambda b, qi, ki: (b, qi, 0)),
        out_shape=jax.ShapeDtypeStruct((B, S, MLA_HEADS * V_DIM), BF16),
        scratch_shapes=[pltpu.VMEM((MLA_HEADS, tq, 1), F32), pltpu.VMEM((MLA_HEADS, tq, 1), F32),
                        pltpu.VMEM((MLA_HEADS, tq, V_DIM), F32)],
        compiler_params=_cparams(("parallel", "parallel", "arbitrary")),
        name="mla_attn",
    )(q3, kn3, kr3, v3)


def _flash_t_kernel(q_ref, kn_ref, kr_ref, vt_ref, o_ref, m_ref, l_ref, acc_ref, *, tq, tk, P, S, nk):
    qi = pl.program_id(1)
    ki = pl.program_id(2)
    q0 = P + qi * tq
    k0 = ki * tk

    @pl.when(ki == 0)
    def _():
        m_ref[...] = jnp.full(m_ref.shape, NEG_BIG, F32)
        l_ref[...] = jnp.zeros(l_ref.shape, F32)
        acc_ref[...] = jnp.zeros(acc_ref.shape, F32)

    def step(masked):
        kr = kr_ref[...]
        if masked:
            kpos = k0 + lax.broadcasted_iota(jnp.int32, (tk, 1), 0)
            qpos = q0 + lax.broadcasted_iota(jnp.int32, (1, tq), 1)
            mask = ((kpos // CHUNK) <= (qpos // CHUNK)) & (kpos < P + S)
        for h in range(MLA_HEADS):
            kh = jnp.concatenate([kn_ref[:, h * NOPE_DIM:(h + 1) * NOPE_DIM], kr], axis=1)
            st = _dot_nt(kh, q_ref[:, h * Q_SLAB:(h + 1) * Q_SLAB])
            if masked:
                st = jnp.where(mask, st, NEG_BIG)
            m_prev = m_ref[h]
            m_new = jnp.maximum(m_prev, jnp.max(st, axis=0, keepdims=True))
            alpha = jnp.exp(m_prev - m_new)
            p = jnp.exp(st - m_new)
            l_ref[h] = alpha * l_ref[h] + jnp.sum(p, axis=0, keepdims=True)
            acc_ref[h] = alpha * acc_ref[h] + _dot(vt_ref[h * V_DIM:(h + 1) * V_DIM, :], p.astype(BF16))
            m_ref[h] = m_new

    needed = ki <= _last_kblock(qi, tq, tk, P, nk)
    full = ((k0 + tk - 1) // CHUNK <= q0 // CHUNK) & (k0 + tk <= P + S)
    pl.when(needed & full)(functools.partial(step, False))
    pl.when(needed & jnp.logical_not(full))(functools.partial(step, True))

    @pl.when(ki == nk - 1)
    def _():
        for h in range(MLA_HEADS):
            o_ref[:, h * V_DIM:(h + 1) * V_DIM] = (acc_ref[h] / l_ref[h]).T.astype(BF16)


def _flash_t(q3, kn3, kr3, vt3, P, S, tq, tk):
    B = q3.shape[0]
    skp = kn3.shape[1]
    nk = skp // tk
    kblk = lambda qi, ki: jnp.minimum(ki, _last_kblock(qi, tq, tk, P, nk))
    kmap = lambda b, qi, ki: (b, kblk(qi, ki), 0)
    return pl.pallas_call(
        functools.partial(_flash_t_kernel, tq=tq, tk=tk, P=P, S=S, nk=nk),
        grid=(B, S // tq, nk),
        in_specs=[pl.BlockSpec((None, tq, MLA_HEADS * Q_SLAB), lambda b, qi, ki: (b, qi, 0)),
                  pl.BlockSpec((None, tk, MLA_HEADS * NOPE_DIM), kmap),
                  pl.BlockSpec((None, tk, LANES), kmap),
                  pl.BlockSpec((None, MLA_HEADS * V_DIM, tk), lambda b, qi, ki: (b, 0, kblk(qi, ki)))],
        out_specs=pl.BlockSpec((None, tq, MLA_HEADS * V_DIM), lambda b, qi, ki: (b, qi, 0)),
        out_shape=jax.ShapeDtypeStruct((B, S, MLA_HEADS * V_DIM), BF16),
        scratch_shapes=[pltpu.VMEM((MLA_HEADS, 1, tq), F32), pltpu.VMEM((MLA_HEADS, 1, tq), F32),
                        pltpu.VMEM((MLA_HEADS, V_DIM, tq), F32)],
        compiler_params=_cparams(("parallel", "parallel", "arbitrary")),
        name="mla_attn_t",
    )(q3, kn3, kr3, vt3)


def _memkv_kernel(mem_ref, mnorm_ref, wmk_ref, wmv_ref, mkn_ref, k_ref, v_ref):
    mb = _rms(mem_ref[...], mnorm_ref[...]).astype(BF16)
    kf = _dot(mb, wmk_ref[...])
    for h in range(MEM_HEADS):
        sl = slice(h * MEM_HEAD_DIM, (h + 1) * MEM_HEAD_DIM)
        k_ref[:, sl] = _rms(kf[:, sl], mkn_ref[...])
    v_ref[...] = _dot(mb, wmv_ref[...])


def _memkv(mem2d, mnorm, w_mk, w_mv, mkn, tm):
    T = mem2d.shape[0]
    hw = MEM_HEADS * MEM_HEAD_DIM
    row = lambda n: pl.BlockSpec((tm, n), lambda i: (i, 0))
    return pl.pallas_call(
        _memkv_kernel,
        grid=(T // tm,),
        in_specs=[row(D_MODEL), _const_spec(mnorm.shape), _const_spec(w_mk.shape), _const_spec(w_mv.shape),
                  _const_spec(mkn.shape)],
        out_specs=[row(hw), row(hw)],
        out_shape=[jax.ShapeDtypeStruct((T, hw), F32), jax.ShapeDtypeStruct((T, hw), F32)],
        compiler_params=_cparams(("parallel",)),
        name="mem_kv",
    )(mem2d, mnorm, w_mk, w_mv, mkn)


def _mix_kernel(x_ref, og_ref, om_ref, mk_ref, mv_ref, wout_ref, nmem_ref, wmq_ref, mqn_ref, wmo_ref, nffn_ref,
                wrh_ref, wrl_ref, br_ref, x2_ref, h3_ref, idx_ref, gate_ref, counts_ref, cnt_ref):
    first = (pl.program_id(0) == 0) & (pl.program_id(1) == 0)

    @pl.when(first)
    def _():
        cnt_ref[...] = jnp.zeros(cnt_ref.shape, F32)

    mkb = [mk_ref[:, h * MEM_HEAD_DIM:(h + 1) * MEM_HEAD_DIM].astype(BF16) for h in range(MEM_HEADS)]
    mvb = [mv_ref[:, h * MEM_HEAD_DIM:(h + 1) * MEM_HEAD_DIM].astype(BF16) for h in range(MEM_HEADS)]
    for rs in (slice(None),):
        x1 = x_ref[rs, :] + _dot(og_ref[rs, :], wout_ref[0:GDN_VW, :]) + _dot(om_ref[rs, :], wout_ref[GDN_VW:, :])
        hb = _rms(x1, nmem_ref[...]).astype(BF16)
        qm = _dot(hb, wmq_ref[...])
        heads = []
        for h in range(MEM_HEADS):
            sl = slice(h * MEM_HEAD_DIM, (h + 1) * MEM_HEAD_DIM)
            qh = (_rms(qm[:, sl], mqn_ref[...]) * (MEM_HEAD_DIM ** -0.5)).astype(BF16)
            s = _dot_nt(qh, mkb[h])
            p = jnp.exp(s - jnp.max(s, axis=-1, keepdims=True))
            p = p / jnp.sum(p, axis=-1, keepdims=True)
            heads.append(_dot(p.astype(BF16), mvb[h]).astype(BF16))
        x2 = x1 + _dot(jnp.concatenate(heads, axis=1), wmo_ref[...])
        x2_ref[rs, :] = x2
        h3 = _rms(x2, nffn_ref[...])
        hi = h3.astype(BF16)
        packed = _pack_bf16_pairs(hi.astype(F32))
        for c in range(h3_ref.shape[0]):
            h3_ref[c, rs, :] = packed[:, c * LANES:(c + 1) * LANES]
        lo = (h3 - hi.astype(F32)).astype(BF16)
        wrh = wrh_ref[...]
        logits = _dot(hi, wrh) + _dot(lo, wrh) + _dot(hi, wrl_ref[...]) + br_ref[...]

        lane = lax.broadcasted_iota(jnp.int32, logits.shape, 1).astype(F32)
        vals, idxs = [], []
        cur = logits
        for _ in range(TOP_K):
            mx = jnp.max(cur, axis=-1, keepdims=True)
            ix = jnp.min(jnp.where(cur == mx, lane, float(LANES)), axis=-1, keepdims=True)
            vals.append(mx)
            idxs.append(ix)
            cur = jnp.where(lane == ix, -3e38, cur)
        es = [jnp.exp(v - vals[0]) for v in vals]
        den = es[0] + es[1] + es[2] + es[3]

        sel = jnp.zeros(logits.shape, F32)
        for k in range(TOP_K):
            sel = sel + jnp.where(lane == idxs[k], 1.0, 0.0)
        tm = logits.shape[0]
        ri = lax.broadcasted_iota(jnp.int32, (tm, tm), 0)
        ci = lax.broadcasted_iota(jnp.int32, (tm, tm), 1)
        before = jnp.where(ri > ci, 1.0, 0.0).astype(BF16)
        excl = _dot(before, sel.astype(BF16)) + cnt_ref[...]
        cnt_ref[...] = cnt_ref[...] + jnp.sum(sel, axis=0, keepdims=True)
        counts_ref[...] = cnt_ref[...].astype(jnp.int32)

        idx_out = jnp.zeros(logits.shape, F32)
        gate_out = jnp.zeros(logits.shape, F32)
        for k in range(TOP_K):
            rank = jnp.sum(jnp.where(lane == idxs[k], excl, 0.0), axis=-1, keepdims=True)
            idx_out = jnp.where(lane == float(k), idxs[k], idx_out)
            idx_out = jnp.where(lane == float(TOP_K + k), rank, idx_out)
            gate_out = jnp.where(lane == float(k), es[k] / den, gate_out)
        idx_ref[rs, :] = idx_out.astype(jnp.int32)
        gate_ref[rs, :] = gate_out


def _mix(x3, og3, om3, mk3, mv3, pw, tm):
    B, S, _ = x3.shape
    hw = MEM_HEADS * MEM_HEAD_DIM
    tile = lambda n: pl.BlockSpec((None, tm, n), lambda b, i: (b, i, 0))
    memspec = pl.BlockSpec((None, N_MEM, hw), lambda b, i: (b, 0, 0))
    consts = [pw['w_out'], pw['nmem'], pw['w_mq'], pw['mqn'], pw['w_mo'], pw['nffn'], pw['wr_hi'], pw['wr_lo'],
              pw['b_r']]
    return pl.pallas_call(
        _mix_kernel,
        grid=(B, S // tm),
        in_specs=[tile(D_MODEL), tile(GDN_VW), tile(MLA_HEADS * V_DIM), memspec, memspec]
                 + [_const_spec(c.shape) for c in consts],
        out_specs=[tile(D_MODEL), pl.BlockSpec((PK_CHUNKS, tm, LANES), lambda b, i: (0, b * (S // tm) + i, 0)),
                   tile(LANES), tile(LANES), _const_spec((1, LANES))],
        out_shape=[jax.ShapeDtypeStruct((B, S, D_MODEL), F32),
                   jax.ShapeDtypeStruct((PK_CHUNKS, B * S, LANES), jnp.uint32),
                   jax.ShapeDtypeStruct((B, S, LANES), jnp.int32), jax.ShapeDtypeStruct((B, S, LANES), F32),
                   jax.ShapeDtypeStruct((1, LANES), jnp.int32)],
        scratch_shapes=[pltpu.VMEM((1, LANES), F32)],
        compiler_params=_cparams(("arbitrary", "arbitrary")),
        name="mix_mem_router",
    )(x3, og3, om3, mk3, mv3, *consts)


def _expert_kernel(be_ref, nu_ref, rows_ref, wgu_ref, bgu_ref, wd_ref, bd_ref, y_ref, wgub_ref, wdb_ref):
    i = pl.program_id(0)
    used = i < nu_ref[0]
    new_expert = (i == 0) | (be_ref[i] != be_ref[jnp.maximum(i - 1, 0)])

    @pl.when(used & new_expert)
    def _():
        def cast(r, carry):
            rs = pl.ds(pl.multiple_of(r * LANES, LANES), LANES)
            wgub_ref[rs, :] = wgu_ref[rs, :].astype(BF16)
            wdb_ref[rs, :] = wd_ref[rs, :].astype(BF16)
            return carry
        lax.fori_loop(0, D_MODEL // LANES, cast, 0)

    @pl.when(used)
    def _():
        packed = jnp.concatenate([rows_ref[c] for c in range(PK_CHUNKS)], axis=1)
        x = _unpack_bf16_pairs(packed).astype(BF16)
        acc = None
        for c in range(D_FF // FF_CHUNK):
            gs_ = slice(c * FF_CHUNK, (c + 1) * FF_CHUNK)
            us_ = slice(D_FF + c * FF_CHUNK, D_FF + (c + 1) * FF_CHUNK)
            gt = jnp.minimum(_dot(x, wgub_ref[:, gs_]) + bgu_ref[:, gs_], SWIGLU_LIMIT)
            up = jnp.clip(_dot(x, wgub_ref[:, us_]) + bgu_ref[:, us_], -SWIGLU_LIMIT, SWIGLU_LIMIT)
            act = gt * _sigmoid(SWIGLU_ALPHA * gt) * (up + 1.0)
            part = _dot(act.astype(BF16), wdb_ref[gs_, :])
            acc = part if acc is None else acc + part
        ypk = _pack_bf16_pairs((acc + bd_ref[...]).astype(BF16).astype(F32))
        for c in range(PK_CHUNKS):
            y_ref[c] = ypk[:, c * LANES:(c + 1) * LANES]

    @pl.when(jnp.logical_not(used))
    def _():
        y_ref[...] = jnp.zeros(y_ref.shape, y_ref.dtype)


def _experts(block_e, n_used, rows, w_gu, b_gu, w_down, b_down):
    n_rows = rows.shape[1]
    nb = n_rows // MOE_ROWS
    gs = pltpu.PrefetchScalarGridSpec(
        num_scalar_prefetch=2,
        grid=(nb,),
        in_specs=[pl.BlockSpec((PK_CHUNKS, MOE_ROWS, LANES), lambda i, be, nu: (0, i, 0)),
                  pl.BlockSpec((None, D_MODEL, 2 * D_FF), lambda i, be, nu: (be[i], 0, 0)),
                  pl.BlockSpec((None, 1, 2 * D_FF), lambda i, be, nu: (be[i], 0, 0)),
                  pl.BlockSpec((None, D_FF, D_MODEL), lambda i, be, nu: (be[i], 0, 0)),
                  pl.BlockSpec((None, 1, D_MODEL), lambda i, be, nu: (be[i], 0, 0))],
        out_specs=pl.BlockSpec((PK_CHUNKS, MOE_ROWS, LANES), lambda i, be, nu: (0, i, 0)),
        scratch_shapes=[pltpu.VMEM((D_MODEL, 2 * D_FF), BF16), pltpu.VMEM((D_FF, D_MODEL), BF16)],
    )
    return pl.pallas_call(
        _expert_kernel,
        grid_spec=gs,
        out_shape=jax.ShapeDtypeStruct((PK_CHUNKS, n_rows, LANES), jnp.uint32),
        compiler_params=_cparams(("arbitrary",)),
        name="moe_experts",
    )(block_e, n_used, rows, w_gu, b_gu, w_down, b_down)


def _sc_mesh():
    return plsc.VectorSubcoreMesh(core_axis_name="core", subcore_axis_name="subcore")


def _sc_scatter_rows(x3s, pos_ts, n_rows):
    C, _, L = x3s[0].shape
    K = pos_ts[0].shape[0]
    ns = len(x3s)

    @functools.partial(pl.kernel, out_type=jax.ShapeDtypeStruct((C, n_rows, L), x3s[0].dtype), mesh=_sc_mesh(),
                       scratch_types=[])
    def scatter(*refs):
        o_hbm = refs[2 * ns]
        for s in range(ns):
            x_hbm, i_hbm = refs[s], refs[ns + s]
            nwin = x3s[s].shape[1] // SC_WINDOW
            for c in range(C):
                def body(x_vmem, i_vmem, c=c):
                    for k in range(K):
                        pltpu.sync_copy(x_vmem, o_hbm.at[c].at[i_vmem.at[k]])

                pltpu.emit_pipeline(
                    body, grid=(nwin,),
                    in_specs=[pl.BlockSpec((SC_WINDOW, L), lambda i, c=c, nwin=nwin: (c * nwin + i, 0)),
                              pl.BlockSpec((K, SC_WINDOW), lambda i: (0, i))],
                    out_specs=[], core_axis_name=("core", "subcore"), dimension_semantics=(pltpu.PARALLEL,),
                )(x_hbm, i_hbm)

    return scatter(*[x.reshape(-1, L) for x in x3s], *pos_ts)


def _sc_gather_rows(table3, idxs):
    C, _, L = table3.shape
    ns = len(idxs)
    out_type = [jax.ShapeDtypeStruct((C * i.shape[0], L), table3.dtype) for i in idxs]

    @functools.partial(pl.kernel, out_type=out_type, mesh=_sc_mesh(), scratch_types=[])
    def gather(t_hbm, *refs):
        for s in range(ns):
            i_hbm, o_hbm = refs[s], refs[ns + s]
            nwin = idxs[s].shape[0] // SC_WINDOW
            for c in range(C):
                def body(i_vmem, o_vmem, c=c):
                    pltpu.sync_copy(t_hbm.at[c].at[i_vmem.at[0]], o_vmem)

                pltpu.emit_pipeline(
                    body, grid=(nwin,),
                    in_specs=[pl.BlockSpec((1, SC_WINDOW), lambda i: (0, i))],
                    out_specs=[pl.BlockSpec((SC_WINDOW, L), lambda i, c=c, nwin=nwin: (c * nwin + i, 0))],
                    core_axis_name=("core", "subcore"), dimension_semantics=(pltpu.PARALLEL,),
                )(i_hbm, o_hbm)

    outs = gather(table3, *[i.reshape(1, -1) for i in idxs])
    return [o.reshape(C, -1, L) for o in outs]


def _combine_kernel(x2_ref, g_ref, gate_ref, o_ref):
    gate = gate_ref[...]
    half = D_MODEL // 2
    for c in range(PK_CHUNKS):
        lo_s = slice(c * LANES, (c + 1) * LANES)
        hi_s = slice(half + c * LANES, half + (c + 1) * LANES)
        acc_lo = x2_ref[:, lo_s]
        acc_hi = x2_ref[:, hi_s]
        for k in range(TOP_K):
            w = g_ref[c, k]
            gk = gate[:, k:k + 1]
            acc_lo = acc_lo + pltpu.bitcast(w << 16, F32) * gk
            acc_hi = acc_hi + pltpu.bitcast(w & jnp.uint32(0xFFFF0000), F32) * gk
        o_ref[:, lo_s] = acc_lo
        o_ref[:, hi_s] = acc_hi


def _combine(x2, g4, gate, tm):
    T = x2.shape[0]
    return pl.pallas_call(
        _combine_kernel,
        grid=(T // tm,),
        in_specs=[pl.BlockSpec((tm, D_MODEL), lambda i: (i, 0)),
                  pl.BlockSpec((PK_CHUNKS, TOP_K, tm, LANES), lambda i: (0, 0, i, 0)),
                  pl.BlockSpec((tm, LANES), lambda i: (i, 0))],
        out_specs=pl.BlockSpec((tm, D_MODEL), lambda i: (i, 0)),
        out_shape=jax.ShapeDtypeStruct((T, D_MODEL), F32),
        compiler_params=_cparams(("parallel",)),
        name="moe_combine",
    )(x2, g4, gate)


def _moe(streams, ew):
    cnts = [st[3][0, :N_EXPERTS] for st in streams]
    total = sum(cnts)
    padded = (total + MOE_ROWS - 1) // MOE_ROWS * MOE_ROWS
    pad_end = jnp.cumsum(padded)
    pad_start = pad_end - padded
    experts = jnp.arange(N_EXPERTS, dtype=jnp.int32)[None, None, :]
    pos_ts = []
    base = pad_start
    for (h3p, idxr, gate, counts, x2), cnt in zip(streams, cnts):
        onehot = idxr[:, :TOP_K, None] == experts
        start = jnp.sum(jnp.where(onehot, base[None, None, :], 0), axis=-1)
        pos_ts.append((start + idxr[:, TOP_K:2 * TOP_K]).T)
        base = base + cnt
    n_assign = sum(st[0].shape[1] for st in streams) * TOP_K
    nb = -(-n_assign // MOE_ROWS) + N_EXPERTS
    starts = jnp.arange(nb, dtype=jnp.int32) * MOE_ROWS
    block_e = jnp.minimum(jnp.sum((pad_end[None, :] <= starts[:, None]).astype(jnp.int32), axis=1), N_EXPERTS - 1)
    n_used = (pad_end[-1] // MOE_ROWS).astype(jnp.int32).reshape(1)
    rows = _sc_scatter_rows([st[0] for st in streams], pos_ts, nb * MOE_ROWS)
    y_rows = _experts(block_e, n_used, rows, ew['w_gu'], ew['b_gu'], ew['w_down'], ew['b_down'])
    gs = _sc_gather_rows(y_rows, [p.reshape(-1) for p in pos_ts])
    outs = []
    for (h3p, idxr, gate, counts, x2), g in zip(streams, gs):
        T = x2.shape[0]
        outs.append(_combine(x2, g.reshape(PK_CHUNKS, TOP_K, T, LANES), gate, _pick(T, (512, 256, 128, 64))))
    return outs


def _pad_lanes(v, n=LANES, fill=0.0):
    return jnp.pad(v, (0, n - v.shape[0]), constant_values=fill).reshape(1, n)


def _prep_weights(norm_mix, w_in, conv_w, a_log, dt_bias, gdn_norm, q_a_norm, w_qb, kv_a_norm, w_kvb, q_norm,
                  k_nope_norm, k_rope_norm, w_out, norm_mem, mem_norm, w_mq, w_mk, w_mv, mq_norm, mk_norm, w_mo,
                  norm_ffn, w_router, b_router, w_gu, b_gu, w_down, b_down):
    c = np.cumsum([CONV_DIM, GDN_VW, GDN_HEADS, GDN_HEADS, Q_RANK, KV_RANK])
    w_u, w_z, w_a, w_b, w_cq, w_ckv, w_kpe = [w_in[:, lo:hi] for lo, hi in
                                              zip([0, *c], [*c, w_in.shape[1]])]
    w_s = jnp.concatenate([w_kpe, w_a, w_b], axis=1)
    w_s = jnp.pad(w_s, ((0, 0), (0, LANES - w_s.shape[1])))
    wq = w_qb.reshape(Q_RANK, MLA_HEADS, QK_DIM)
    wq = jnp.pad(wq, ((0, 0), (0, 0), (0, Q_SLAB - QK_DIM))).reshape(Q_RANK, MLA_HEADS * Q_SLAB)
    wkv = w_kvb.reshape(KV_RANK, MLA_HEADS, NOPE_DIM + V_DIM)
    wkv = jnp.concatenate([wkv[:, :, :NOPE_DIM].reshape(KV_RANK, -1), wkv[:, :, NOPE_DIM:].reshape(KV_RANK, -1)], 1)
    wr = jnp.pad(w_router, ((0, 0), (0, LANES - N_EXPERTS)))
    wr_hi = wr.astype(BF16)
    wr_lo = (wr - wr_hi.astype(F32)).astype(BF16)
    row = lambda v: v.reshape(1, -1)
    gpad = ROPE_DIM
    pw = dict(
        nmix=row(norm_mix), w_u=w_u.astype(BF16), w_z=w_z.astype(BF16), w_cq=w_cq.astype(BF16),
        w_ckv=w_ckv.astype(BF16), w_s=w_s.astype(BF16), qan=row(q_a_norm), w_qb=wq.astype(BF16),
        qn=_pad_lanes(q_norm, Q_SLAB), kvan=row(kv_a_norm), krn=_pad_lanes(k_rope_norm),
        alog=jnp.pad(a_log, (gpad, LANES - gpad - GDN_HEADS)).reshape(1, LANES),
        dtb=jnp.pad(dt_bias, (gpad, LANES - gpad - GDN_HEADS)).reshape(1, LANES),
        conv_w=conv_w, gnorm=row(gdn_norm), w_kvb=wkv.astype(BF16), knn=row(k_nope_norm),
        w_out=w_out.astype(BF16), nmem=row(norm_mem), w_mq=w_mq.astype(BF16), mqn=row(mq_norm),
        w_mo=w_mo.astype(BF16), nffn=row(norm_ffn), wr_hi=wr_hi, wr_lo=wr_lo,
        b_r=_pad_lanes(b_router, LANES, NEG_BIG),
        mnorm=row(mem_norm), w_mk=w_mk.astype(BF16), w_mv=w_mv.astype(BF16), mkn=row(mk_norm),
    )
    ew = dict(w_gu=w_gu, b_gu=b_gu.reshape(N_EXPERTS, 1, 2 * D_FF), w_down=w_down,
              b_down=b_down.reshape(N_EXPERTS, 1, D_MODEL))
    return pw, ew


def _rope_tables(P, S):
    half = ROPE_DIM // 2
    inv = ROPE_THETA ** (-jnp.arange(half, dtype=F32) / half)
    ang = (P + jnp.arange(S, dtype=jnp.int32)).astype(F32)[:, None] * inv[None, :]
    cos, sin = jnp.cos(ang), jnp.sin(ang)
    zh = jnp.zeros((S, half), F32)
    zz = jnp.zeros((S, LANES - ROPE_DIM), F32)
    return (jnp.concatenate([cos, cos, zz], 1), jnp.concatenate([-sin, zh, zz], 1),
            jnp.concatenate([zh, sin, zz], 1))


def _pick(n, prefs):
    for t in prefs:
        if n % t == 0:
            return t
    return n


def _trunk_front(x, lat_past, kpe_past, s0, conv_past, mem_k, mem_v, pw):
    B, S, D = x.shape
    P = lat_past.shape[1]
    T = B * S
    tm = _pick(S, (512, 256, 128, 64))
    u, z, q, lat_new, small = _inproj(x.reshape(T, D), S, tm, pw, _rope_tables(P, S))

    LC = _pick(S, (256, 128, 64))
    NB = _pick(B, tuple(n for n in (8, 4, 2, 1) if n * (LC // CHUNK) <= GDN_UNITS))
    o_gdn, s_new, conv_new = _gdn(u.reshape(B, S, CONV_DIM), small.reshape(B, S, LANES), z.reshape(B, S, GDN_VW),
                                  pw['conv_w'], conv_past, s0, pw['gnorm'], NB, LC)

    sk = P + S
    tk = 512 if S >= 512 else -(-sk // LANES) * LANES
    skp = -(-sk // tk) * tk
    lat_all = jnp.concatenate([lat_past, lat_new.reshape(B, S, KV_RANK)], axis=1)
    kpe_new = small[:, :ROPE_DIM].reshape(B, S, ROPE_DIM)
    kpe_all = jnp.concatenate([kpe_past, kpe_new], axis=1)
    lat_all = jnp.pad(lat_all, ((0, 0), (0, skp - sk), (0, 0)))
    kpe_all = jnp.pad(kpe_all, ((0, 0), (0, skp - sk), (0, LANES - ROPE_DIM)))
    tq = _pick(S, (512, 256, 128, 64))
    key_major = tq >= LANES
    kn, kr, v = _kvproj(lat_all, kpe_all, pw['w_kvb'], pw['knn'], _pick(skp, (512,)), key_major)
    q3 = q.reshape(B, S, MLA_HEADS * Q_SLAB)
    o_mla = (_flash_t if key_major else _flash)(q3, kn, kr, v, P, S, tq, tk)

    x2, h3p, idxr, gate, counts = _mix(x, o_gdn, o_mla, mem_k, mem_v, pw, tm)
    stream = (h3p, idxr.reshape(T, LANES), gate.reshape(T, LANES), counts, x2.reshape(T, D))
    return stream, (lat_new.reshape(B, S, KV_RANK), kpe_new, s_new, conv_new)


def kernel(x_prompt, x_sample, cache_kv_latent, cache_k_rope, state_gdn, state_conv, cache_mem_k, cache_mem_v, mem_prompt, norm_mix, w_in, conv_w, a_log, dt_bias, gdn_norm, q_a_norm, w_qb, kv_a_norm, w_kvb, q_norm, k_nope_norm, k_rope_norm, w_out, norm_mem, mem_norm, w_mq, w_mk, w_mv, mq_norm, mk_norm, w_mo, norm_ffn, w_router, b_router, w_gu, b_gu, w_down, b_down):
    depth = norm_mix.shape[0]
    yp, ys = x_prompt, x_sample
    bp = x_prompt.shape[0]
    hw = MEM_HEADS * MEM_HEAD_DIM
    outs = [[] for _ in range(10)]
    for l in range(depth):
        pw, ew = _prep_weights(norm_mix[l], w_in[l], conv_w[l], a_log[l], dt_bias[l], gdn_norm[l], q_a_norm[l],
                               w_qb[l], kv_a_norm[l], w_kvb[l], q_norm[l], k_nope_norm[l], k_rope_norm[l], w_out[l],
                               norm_mem[l], mem_norm[l], w_mq[l], w_mk[l], w_mv[l], mq_norm[l], mk_norm[l], w_mo[l],
                               norm_ffn[l], w_router[l], b_router[l], w_gu[l], b_gu[l], w_down[l], b_down[l])
        nm = mem_prompt.shape[1]
        mk, mv = _memkv(mem_prompt.reshape(bp * nm, D_MODEL), pw['mnorm'], pw['w_mk'], pw['w_mv'], pw['mkn'],
                        _pick(bp * nm, (512, 256)))
        mk = mk.reshape(bp, nm, hw)
        mv = mv.reshape(bp, nm, hw)
        stream_p, (lat, kpe, s_fin, cv) = _trunk_front(
            yp, jnp.zeros((bp, 0, KV_RANK), F32), jnp.zeros((bp, 0, ROPE_DIM), F32),
            jnp.zeros((bp, GDN_HEADS, GDN_DK, GDN_DV), F32), jnp.zeros((bp, CONV_W - 1, CONV_DIM), F32), mk, mv, pw)
        bs = x_sample.shape[0]
        stream_s, (lat2, kpe2, s_fin2, cv2) = _trunk_front(
            ys, cache_kv_latent[l], cache_k_rope[l], state_gdn[l], state_conv[l],
            cache_mem_k[l].reshape(bs, nm, hw), cache_mem_v[l].reshape(bs, nm, hw), pw)
        yp2, ys2 = _moe([stream_p, stream_s], ew)
        yp, ys = yp2.reshape(yp.shape), ys2.reshape(ys.shape)
        for lst, val in zip(outs, (lat, kpe, s_fin, cv, mk.reshape(bp, nm, MEM_HEADS, MEM_HEAD_DIM),
                                   mv.reshape(bp, nm, MEM_HEADS, MEM_HEAD_DIM), lat2, kpe2, s_fin2, cv2)):
            lst.append(val)
    return (yp, ys) + tuple(jnp.stack(o) for o in outs)
```

```python
import functools
import math

import numpy as np
import jax
import jax.numpy as jnp
from jax import lax
from jax.experimental import pallas as pl
from jax.experimental.pallas import tpu as pltpu
from jax.experimental.pallas import tpu_sc as plsc

F32 = jnp.float32
BF16 = jnp.bfloat16

D_MODEL = 1024
CHUNK = 64
EPS = 1e-6
GDN_HEADS = 4
GDN_DK = 128
GDN_DV = 128
CONV_W = 4
GDN_QK = GDN_HEADS * GDN_DK
GDN_VW = GDN_HEADS * GDN_DV
CONV_DIM = 2 * GDN_QK + GDN_VW
MLA_HEADS = 4
Q_RANK = 384
KV_RANK = 256
NOPE_DIM = 128
ROPE_DIM = 64
V_DIM = 128
QK_DIM = NOPE_DIM + ROPE_DIM
ROPE_THETA = 10000.0
N_MEM = 256
MEM_HEADS = 4
MEM_HEAD_DIM = 128
N_EXPERTS = 32
TOP_K = 4
D_FF = D_MODEL
SWIGLU_ALPHA = 1.702
SWIGLU_LIMIT = 7.0

LANES = 128
Q_SLAB = 2 * LANES
NEG_BIG = -1e30
VMEM_LIMIT = 56 * 1024 * 1024
MOE_ROWS = 512
GDN_UNITS = 8
FF_CHUNK = 256
SC_WINDOW = 128
PK_CHUNKS = D_MODEL // 2 // LANES
SUBTILE_ROWS = 256
ONES_ROWS = 16
MIX_ROWS = 512


def _cparams(sem):
    return pltpu.CompilerParams(dimension_semantics=sem, vmem_limit_bytes=VMEM_LIMIT)


def _dot(a, b):
    return jnp.dot(a, b, preferred_element_type=F32)


def _dot_nt(a, b):
    return lax.dot_general(a, b, (((1,), (1,)), ((), ())), preferred_element_type=F32)


def _dot_tn(a, b):
    return lax.dot_general(a, b, (((0,), (0,)), ((), ())), preferred_element_type=F32)


def _rms(x, gain, n=None):
    n = x.shape[-1] if n is None else n
    ss = jnp.sum(x * x, axis=-1, keepdims=True) * (1.0 / n)
    return (x * lax.rsqrt(ss + EPS)) * gain


def _sigmoid(x):
    return 1.0 / (1.0 + jnp.exp(-x))


def _rope128(r, cos, sna, snb):
    return r * cos + pltpu.roll(r, 96, 1) * sna + pltpu.roll(r, 32, 1) * snb


def _pack_bf16_pairs(x):
    n = x.shape[1] // 2
    lo = pltpu.bitcast(x[:, :n], jnp.uint32) >> 16
    hi = pltpu.bitcast(x[:, n:], jnp.uint32) & jnp.uint32(0xFFFF0000)
    return lo | hi


def _unpack_bf16_pairs(p):
    lo = pltpu.bitcast(p << 16, F32)
    hi = pltpu.bitcast(p & jnp.uint32(0xFFFF0000), F32)
    return jnp.concatenate([lo, hi], axis=1)


def _subtiles(rows):
    n = rows // SUBTILE_ROWS if rows % SUBTILE_ROWS == 0 else 1
    step = rows // n
    return [slice(i * step, (i + 1) * step) for i in range(n)]


def _const_spec(shape):
    nd = len(shape)
    return pl.BlockSpec(shape, lambda *_: (0,) * nd)


def _inproj_kernel(x_ref, nmix_ref, wu_ref, wz_ref, wcq_ref, wckv_ref, ws_ref, qan_ref, wqb_ref, qn_ref,
                   kvan_ref, krn_ref, alog_ref, dtb_ref, cos_ref, sna_ref, snb_ref,
                   u_ref, z_ref, q_ref, lat_ref, small_ref):
    for rs in _subtiles(x_ref.shape[0]):
        x = x_ref[rs, :]
        hb = _rms(x, nmix_ref[...]).astype(BF16)
        u_ref[rs, :] = _dot(hb, wu_ref[...])
        z_ref[rs, :] = _dot(hb, wz_ref[...])
        cos, sna, snb = cos_ref[rs, :], sna_ref[rs, :], snb_ref[rs, :]

        cq = _rms(_dot(hb, wcq_ref[...]), qan_ref[...]).astype(BF16)
        qf = _dot(cq, wqb_ref[...])
        scale = QK_DIM ** -0.5
        for h in range(MLA_HEADS):
            slab = qf[:, h * Q_SLAB:(h + 1) * Q_SLAB]
            slab = _rms(slab, qn_ref[...], n=QK_DIM)
            nope = slab[:, :LANES]
            ropd = _rope128(slab[:, LANES:], cos, sna, snb)
            q_ref[rs, h * Q_SLAB:h * Q_SLAB + LANES] = (nope * scale).astype(BF16)
            q_ref[rs, h * Q_SLAB + LANES:(h + 1) * Q_SLAB] = (ropd * scale).astype(BF16)

        lat_ref[rs, :] = _rms(_dot(hb, wckv_ref[...]), kvan_ref[...])

        sm = _dot(hb, ws_ref[...])
        lane = lax.broadcasted_iota(jnp.int32, sm.shape, 1)
        kp = jnp.where(lane < ROPE_DIM, sm, 0.0)
        kpe = _rope128(_rms(kp, krn_ref[...], n=ROPE_DIM), cos, sna, snb)
        sp = sm + dtb_ref[...]
        softplus = jnp.maximum(sp, 0.0) + jnp.log1p(jnp.exp(-jnp.abs(sp)))
        g = -jnp.exp(alog_ref[...]) * softplus
        beta = _sigmoid(sm)
        small_ref[rs, :] = jnp.where(lane < ROPE_DIM, kpe,
                                     jnp.where(lane < ROPE_DIM + GDN_HEADS, g,
                                               jnp.where(lane < ROPE_DIM + 2 * GDN_HEADS, beta, 0.0)))


def _inproj(x2d, S, tm, pw, tabs):
    T = x2d.shape[0]
    if tm > S:
        tabs = [jnp.tile(t, (tm // S, 1)) for t in tabs]
    nblk_s = max(S // tm, 1)
    row = lambda n: pl.BlockSpec((tm, n), lambda i: (i, 0))
    tab = pl.BlockSpec((tm, LANES), lambda i: (i % nblk_s, 0))
    consts = [pw['nmix'], pw['w_u'], pw['w_z'], pw['w_cq'], pw['w_ckv'], pw['w_s'], pw['qan'], pw['w_qb'],
              pw['qn'], pw['kvan'], pw['krn'], pw['alog'], pw['dtb']]
    return pl.pallas_call(
        _inproj_kernel,
        grid=(T // tm,),
        in_specs=[row(D_MODEL)] + [_const_spec(c.shape) for c in consts] + [tab, tab, tab],
        out_specs=[row(CONV_DIM), row(GDN_VW), row(MLA_HEADS * Q_SLAB), row(KV_RANK), row(LANES)],
        out_shape=[jax.ShapeDtypeStruct((T, CONV_DIM), F32), jax.ShapeDtypeStruct((T, GDN_VW), F32),
                   jax.ShapeDtypeStruct((T, MLA_HEADS * Q_SLAB), BF16), jax.ShapeDtypeStruct((T, KV_RANK), F32),
                   jax.ShapeDtypeStruct((T, LANES), F32)],
        compiler_params=_cparams(("parallel",)),
        name="inproj",
    )(x2d, *consts, *tabs)


def _split3(x):
    hi = x.astype(BF16)
    r = x - hi.astype(F32)
    mid = r.astype(BF16)
    lo = (r - mid.astype(F32)).astype(BF16)
    return hi, mid, lo


def _gdn_kernel(u_ref, small_ref, z_ref, convw_ref, cpast_ref, s0_ref, gnorm_ref,
                to_ref, trilm_ref, strictm_ref, same2_ref, lvl_ref,
                o_ref, sfin_ref, cnew_ref, ext_ref, uc_ref, state_ref,
                qf_ref, kf_ref, vb_ref, bt_ref, gcum_ref, glast_ref, kk_ref, qk_ref, mb_ref, x_ref, qkb_ref,
                kbe_ref, qg_ref, kdec_ref, egl_ref, t1_ref, uu_ref, ww_ref, vn_ref, qs_ref, *, NB, LC):
    j = pl.program_id(1)
    nj = pl.num_programs(1)
    PADR = 8
    C = LC // CHUNK
    U = NB * C
    HR = GDN_HEADS * CHUNK

    @pl.when(j == 0)
    def _():
        state_ref[...] = s0_ref[...]
        ext_ref[:, PADR - (CONV_W - 1):PADR, :] = cpast_ref[...]

    w = convw_ref[...]
    for nb in range(NB):
        ext_ref[nb, PADR:PADR + LC, :] = u_ref[nb]
        acc = ext_ref[nb, PADR:PADR + LC, :] * w[CONV_W - 1:CONV_W, :]
        for t in range(1, CONV_W):
            acc = acc + ext_ref[nb, PADR - t:PADR - t + LC, :] * w[CONV_W - 1 - t:CONV_W - t, :]
        uc_ref[nb] = acc * _sigmoid(acc)
        ext_ref[nb, 0:PADR, :] = ext_ref[nb, LC:LC + PADR, :]

    @pl.when(j == nj - 1)
    def _():
        cnew_ref[...] = ext_ref[:, PADR - (CONV_W - 1):PADR, :]

    units = [(nb, c) for nb in range(NB) for c in range(C)]
    g0 = ROPE_DIM
    b0 = ROPE_DIM + GDN_HEADS
    to = to_ref[...]
    for u, (nb, c) in enumerate(units):
        rows = slice(c * CHUNK, (c + 1) * CHUNK)
        sm = small_ref[nb, rows, :]
        gl = sum(_dot(to, part) for part in _split3(sm))
        for h in range(GDN_HEADS):
            hr = slice(h * CHUNK, (h + 1) * CHUNK)
            q = uc_ref[nb, rows, h * GDN_DK:(h + 1) * GDN_DK]
            k = uc_ref[nb, rows, GDN_QK + h * GDN_DK:GDN_QK + (h + 1) * GDN_DK]
            v = uc_ref[nb, rows, 2 * GDN_QK + h * GDN_DV:2 * GDN_QK + (h + 1) * GDN_DV]
            beta = jnp.broadcast_to(sm[:, b0 + h:b0 + h + 1], (CHUNK, LANES))
            qf_ref[u, hr, :] = (q * lax.rsqrt(jnp.sum(q * q, -1, keepdims=True) + EPS)) * (GDN_DK ** -0.5)
            kf_ref[u, hr, :] = k * lax.rsqrt(jnp.sum(k * k, -1, keepdims=True) + EPS)
            vb_ref[u, hr, :] = (v * beta).astype(BF16)
            bt_ref[u, hr, :] = beta
            gcum_ref[u, hr, :] = jnp.broadcast_to(gl[:CHUNK, g0 + h:g0 + h + 1], (CHUNK, LANES))
            glast_ref[u, hr, :] = jnp.broadcast_to(gl[CHUNK:, g0 + h:g0 + h + 1], (CHUNK, LANES))

    for u in range(U):
        k = kf_ref[u]
        kbf = k.astype(BF16)
        kk_ref[u] = _dot_nt((k * bt_ref[u]).astype(BF16), kbf)
        qk_ref[u] = _dot_nt(qf_ref[u].astype(BF16), kbf)

    trilm = trilm_ref[...]
    eye = trilm - strictm_ref[...]
    for u in range(U):
        gcum = gcum_ref[u]
        grow = gcum.T[0:1, :]
        gcol = jnp.concatenate([gcum, gcum], axis=1)
        decay = jnp.exp(jnp.minimum(gcol - grow, 0.0)) * trilm
        m = kk_ref[u] * (decay * strictm_ref[...])
        mb_ref[u] = m.astype(BF16)
        x_ref[u] = eye - m * same2_ref[...]
        qkb_ref[u] = (qk_ref[u] * decay).astype(BF16)
        egc = jnp.exp(gcum)
        k = kf_ref[u]
        kbe_ref[u] = (k * bt_ref[u] * egc).astype(BF16)
        qg_ref[u] = (qf_ref[u] * egc).astype(BF16)
        kdec_ref[u] = (k * jnp.exp(glast_ref[u] - gcum)).astype(BF16)
        egl_ref[u] = jnp.exp(glast_ref[u])

    for lvl in range(lvl_ref.shape[0]):
        lm = lvl_ref[lvl]
        for u in range(U):
            t1_ref[u] = _dot(mb_ref[u] * lm, x_ref[u].astype(BF16)).astype(BF16)
        for u in range(U):
            x = x_ref[u]
            x_ref[u] = x - _dot(x.astype(BF16), t1_ref[u])

    for u in range(U):
        xb = x_ref[u].astype(BF16)
        uu_ref[u] = _dot(xb, vb_ref[u])
        ww_ref[u] = _dot(xb, kbe_ref[u]).astype(BF16)

    gnorm = gnorm_ref[...]
    hrs = [slice(h * CHUNK, (h + 1) * CHUNK) for h in range(GDN_HEADS)]
    for c in range(C):
        rows = slice(c * CHUNK, (c + 1) * CHUNK)
        us = [nb * C + c for nb in range(NB)]
        for nb, u in enumerate(us):
            for h, hr in enumerate(hrs):
                stb = state_ref[nb, h].astype(BF16)
                r = _dot(jnp.concatenate([ww_ref[u, hr, :], qg_ref[u, hr, :]], axis=0), stb)
                vn_ref[u, hr, :] = (uu_ref[u, hr, :] - r[:CHUNK]).astype(BF16)
                qs_ref[u, hr, :] = r[CHUNK:]
        outs = [qs_ref[u] + _dot(qkb_ref[u], vn_ref[u]) for u in us]
        for nb, u in enumerate(us):
            for h, hr in enumerate(hrs):
                state_ref[nb, h] = (state_ref[nb, h] * egl_ref[u, h * CHUNK:h * CHUNK + 1, :]
                                    + _dot_tn(kdec_ref[u, hr, :], vn_ref[u, hr, :]))
        for nb, u in enumerate(us):
            for h, hr in enumerate(hrs):
                zz = z_ref[nb, rows, h * GDN_DV:(h + 1) * GDN_DV]
                og = _rms(outs[nb][hr, :], gnorm) * (zz * _sigmoid(zz))
                o_ref[nb, rows, h * GDN_DV:(h + 1) * GDN_DV] = og.astype(BF16)

    @pl.when(j == nj - 1)
    def _():
        sfin_ref[...] = state_ref[...]


def _gdn_masks():
    hr = GDN_HEADS * CHUNK
    i = np.arange(hr)[:, None]
    j = np.arange(hr)[None, :]
    same_head = (i // CHUNK) == (j // CHUNK)
    tril = same_head & (i >= j)
    strict = same_head & (i > j)
    same2 = strict & ((i // 2) == (j // 2))
    lvls = []
    blk = 2
    while blk < CHUNK:
        lvls.append(strict & ((i // (2 * blk)) == (j // (2 * blk))) & ((i // blk) != (j // blk)))
        blk *= 2
    fr = np.arange(CHUNK)
    to = np.concatenate([fr[:, None] >= fr[None, :], np.ones((CHUNK, CHUNK), bool)], axis=0)
    f = lambda a: jnp.asarray(a.astype(np.float32))
    return (jnp.asarray(to.astype(np.float32), dtype=BF16), f(tril), f(strict), f(same2),
            jnp.asarray(np.stack(lvls).astype(np.float32), dtype=BF16))


def _gdn(u3, small3, z3, conv_w, conv_past, s0, gnorm, NB, LC):
    B, S, _ = u3.shape
    C = LC // CHUNK
    U = NB * C
    HR = GDN_HEADS * CHUNK
    masks = _gdn_masks()
    tile = lambda n: pl.BlockSpec((NB, LC, n), lambda b, j: (b, j, 0))
    stspec = pl.BlockSpec((NB, GDN_HEADS, GDN_DK, GDN_DV), lambda b, j: (b, 0, 0, 0))
    cvspec = pl.BlockSpec((NB, CONV_W - 1, CONV_DIM), lambda b, j: (b, 0, 0))
    vm = lambda shape, dt: pltpu.VMEM(shape, dt)
    return pl.pallas_call(
        functools.partial(_gdn_kernel, NB=NB, LC=LC),
        grid=(B // NB, S // LC),
        in_specs=[tile(CONV_DIM), tile(LANES), tile(GDN_VW), _const_spec(conv_w.shape), cvspec, stspec,
                  _const_spec(gnorm.shape)] + [_const_spec(m.shape) for m in masks],
        out_specs=[tile(GDN_VW), stspec, cvspec],
        out_shape=[jax.ShapeDtypeStruct((B, S, GDN_VW), BF16),
                   jax.ShapeDtypeStruct((B, GDN_HEADS, GDN_DK, GDN_DV), F32),
                   jax.ShapeDtypeStruct((B, CONV_W - 1, CONV_DIM), F32)],
        scratch_shapes=[vm((NB, LC + 8, CONV_DIM), F32), vm((NB, LC, CONV_DIM), F32),
                        vm((NB, GDN_HEADS, GDN_DK, GDN_DV), F32),
                        vm((U, HR, LANES), F32), vm((U, HR, LANES), F32), vm((U, HR, LANES), BF16),
                        vm((U, HR, LANES), F32),
                        vm((U, HR, LANES), F32), vm((U, HR, LANES), F32),
                        vm((U, HR, HR), F32), vm((U, HR, HR), F32),
                        vm((U, HR, HR), BF16), vm((U, HR, HR), F32), vm((U, HR, HR), BF16),
                        vm((U, HR, LANES), BF16), vm((U, HR, LANES), BF16), vm((U, HR, LANES), BF16),
                        vm((U, HR, LANES), F32), vm((U, HR, HR), BF16),
                        vm((U, HR, LANES), F32), vm((U, HR, LANES), BF16),
                        vm((U, HR, LANES), BF16), vm((U, HR, LANES), F32)],
        compiler_params=_cparams(("parallel", "arbitrary")),
        name="gdn",
    )(u3, small3, z3, conv_w, conv_past, s0, gnorm, *masks)


def _kvproj_kernel(lat_ref, kpe_ref, wkvb_ref, knn_ref, kn_ref, kr_ref, v_ref, *, v_transposed):
    kv = _dot(lat_ref[...].astype(BF16), wkvb_ref[...])
    hw = MLA_HEADS * NOPE_DIM
    for h in range(MLA_HEADS):
        kh = kv[:, h * NOPE_DIM:(h + 1) * NOPE_DIM]
        kn_ref[:, h * NOPE_DIM:(h + 1) * NOPE_DIM] = _rms(kh, knn_ref[...]).astype(BF16)
    v = kv[:, hw:]
    v_ref[...] = (v.T if v_transposed else v).astype(BF16)
    kp = kpe_ref[...]
    lane = lax.broadcasted_iota(jnp.int32, kp.shape, 1)
    kr_ref[...] = jnp.where(lane < ROPE_DIM, kp, 0.0).astype(BF16)


def _kvproj(lat3, kpe3, w_kvb, knn, tm, v_transposed):
    B, sk, _ = lat3.shape
    hw = MLA_HEADS * V_DIM
    row = lambda n: pl.BlockSpec((None, tm, n), lambda b, i: (b, i, 0))
    if v_transposed:
        vspec, vshape = pl.BlockSpec((None, hw, tm), lambda b, i: (b, 0, i)), (B, hw, sk)
    else:
        vspec, vshape = row(hw), (B, sk, hw)
    return pl.pallas_call(
        functools.partial(_kvproj_kernel, v_transposed=v_transposed),
        grid=(B, sk // tm),
        in_specs=[row(KV_RANK), row(LANES), _const_spec(w_kvb.shape), _const_spec(knn.shape)],
        out_specs=[row(MLA_HEADS * NOPE_DIM), row(LANES), vspec],
        out_shape=[jax.ShapeDtypeStruct((B, sk, MLA_HEADS * NOPE_DIM), BF16),
                   jax.ShapeDtypeStruct((B, sk, LANES), BF16), jax.ShapeDtypeStruct(vshape, BF16)],
        compiler_params=_cparams(("parallel", "parallel")),
        name="kvproj",
    )(lat3, kpe3, w_kvb, knn)


def _last_kblock(qi, tq, tk, P, nk):
    last_key = ((P + qi * tq + tq - 1) // CHUNK) * CHUNK + CHUNK - 1
    return jnp.minimum(last_key // tk, nk - 1)


def _flash_kernel(q_ref, kn_ref, kr_ref, v_ref, o_ref, m_ref, l_ref, acc_ref, *, tq, tk, P, S, nk):
    qi = pl.program_id(1)
    ki = pl.program_id(2)

    @pl.when(ki == 0)
    def _():
        m_ref[...] = jnp.full(m_ref.shape, NEG_BIG, F32)
        l_ref[...] = jnp.zeros(l_ref.shape, F32)
        acc_ref[...] = jnp.zeros(acc_ref.shape, F32)

    @pl.when(ki <= _last_kblock(qi, tq, tk, P, nk))
    def _():
        qpos = P + qi * tq + lax.broadcasted_iota(jnp.int32, (tq, tk), 0)
        kpos = ki * tk + lax.broadcasted_iota(jnp.int32, (tq, tk), 1)
        mask = ((kpos // CHUNK) <= (qpos // CHUNK)) & (kpos < P + S)
        kr = kr_ref[...]
        for h in range(MLA_HEADS):
            qh = q_ref[:, h * Q_SLAB:(h + 1) * Q_SLAB]
            kh = jnp.concatenate([kn_ref[:, h * NOPE_DIM:(h + 1) * NOPE_DIM], kr], axis=1)
            s = jnp.where(mask, _dot_nt(qh, kh), NEG_BIG)
            m_prev = m_ref[h]
            m_new = jnp.maximum(m_prev, jnp.max(s, axis=-1, keepdims=True))
            alpha = jnp.exp(m_prev - m_new)
            p = jnp.exp(s - m_new)
            l_ref[h] = alpha * l_ref[h] + jnp.sum(p, axis=-1, keepdims=True)
            acc_ref[h] = alpha * acc_ref[h] + _dot(p.astype(BF16), v_ref[:, h * V_DIM:(h + 1) * V_DIM])
            m_ref[h] = m_new

    @pl.when(ki == nk - 1)
    def _():
        for h in range(MLA_HEADS):
            o_ref[:, h * V_DIM:(h + 1) * V_DIM] = (acc_ref[h] / l_ref[h]).astype(BF16)


def _flash(q3, kn3, kr3, v3, P, S, tq, tk):
    B = q3.shape[0]
    skp = kn3.shape[1]
    nk = skp // tk
    kmap = lambda b, qi, ki: (b, jnp.minimum(ki, _last_kblock(qi, tq, tk, P, nk)), 0)
    return pl.pallas_call(
        functools.partial(_flash_kernel, tq=tq, tk=tk, P=P, S=S, nk=nk),
        grid=(B, S // tq, nk),
        in_specs=[pl.BlockSpec((None, tq, MLA_HEADS * Q_SLAB), lambda b, qi, ki: (b, qi, 0)),
                  pl.BlockSpec((None, tk, MLA_HEADS * NOPE_DIM), kmap),
                  pl.BlockSpec((None, tk, LANES), kmap),
                  pl.BlockSpec((None, tk, MLA_HEADS * V_DIM), kmap)],
        out_specs=pl.BlockSpec((None, tq, MLA_HEADS * V_DIM), lambda b, qi, ki: (b, qi, 0)),
        out_shape=jax.ShapeDtypeStruct((B, S, MLA_HEADS * V_DIM), BF16),
        scratch_shapes=[pltpu.VMEM((MLA_HEADS, tq, 1), F32), pltpu.VMEM((MLA_HEADS, tq, 1), F32),
                        pltpu.VMEM((MLA_HEADS, tq, V_DIM), F32)],
        compiler_params=_cparams(("parallel", "parallel", "arbitrary")),
        name="mla_attn",
    )(q3, kn3, kr3, v3)


def _flash_t_kernel(q_ref, kn_ref, kr_ref, vt_ref, o_ref, m_ref, acc_ref, *, tq, tk, P, S, nk):
    qi = pl.program_id(1)
    ki = pl.program_id(2)
    q0 = P + qi * tq
    k0 = ki * tk

    @pl.when(ki == 0)
    def _():
        m_ref[...] = jnp.full(m_ref.shape, NEG_BIG, F32)
        acc_ref[...] = jnp.zeros(acc_ref.shape, F32)

    def step(masked):
        kr = kr_ref[...]
        ones = jnp.ones((ONES_ROWS, tk), BF16)
        if masked:
            kpos = k0 + lax.broadcasted_iota(jnp.int32, (tk, 1), 0)
            qpos = q0 + lax.broadcasted_iota(jnp.int32, (1, tq), 1)
            mask = ((kpos // CHUNK) <= (qpos // CHUNK)) & (kpos < P + S)
        sts = []
        for h in range(MLA_HEADS):
            kh = jnp.concatenate([kn_ref[:, h * NOPE_DIM:(h + 1) * NOPE_DIM], kr], axis=1)
            sts.append(_dot_nt(kh, q_ref[:, h * Q_SLAB:(h + 1) * Q_SLAB]))
        ps, alphas = [], []
        for h in range(MLA_HEADS):
            st = sts[h]
            if masked:
                st = jnp.where(mask, st, NEG_BIG)
            m_prev = m_ref[h]
            m_new = jnp.maximum(m_prev, jnp.max(st, axis=0, keepdims=True))
            alphas.append(jnp.exp(m_prev - m_new))
            ps.append(jnp.exp(st - m_new).astype(BF16))
            m_ref[h] = m_new
        for h in range(MLA_HEADS):
            vt1 = jnp.concatenate([vt_ref[h * V_DIM:(h + 1) * V_DIM, :], ones], axis=0)
            acc_ref[h] = alphas[h] * acc_ref[h] + _dot(vt1, ps[h])

    needed = ki <= _last_kblock(qi, tq, tk, P, nk)
    full = ((k0 + tk - 1) // CHUNK <= q0 // CHUNK) & (k0 + tk <= P + S)
    pl.when(needed & full)(functools.partial(step, False))
    pl.when(needed & jnp.logical_not(full))(functools.partial(step, True))

    @pl.when(ki == nk - 1)
    def _():
        for h in range(MLA_HEADS):
            acc = acc_ref[h]
            o_ref[:, h * V_DIM:(h + 1) * V_DIM] = (acc[:V_DIM] / acc[V_DIM:V_DIM + 1]).T.astype(BF16)


def _flash_t(q3, kn3, kr3, vt3, P, S, tq, tk):
    B = q3.shape[0]
    skp = kn3.shape[1]
    nk = skp // tk
    kblk = lambda qi, ki: jnp.minimum(ki, _last_kblock(qi, tq, tk, P, nk))
    kmap = lambda b, qi, ki: (b, kblk(qi, ki), 0)
    return pl.pallas_call(
        functools.partial(_flash_t_kernel, tq=tq, tk=tk, P=P, S=S, nk=nk),
        grid=(B, S // tq, nk),
        in_specs=[pl.BlockSpec((None, tq, MLA_HEADS * Q_SLAB), lambda b, qi, ki: (b, qi, 0)),
                  pl.BlockSpec((None, tk, MLA_HEADS * NOPE_DIM), kmap),
                  pl.BlockSpec((None, tk, LANES), kmap),
                  pl.BlockSpec((None, MLA_HEADS * V_DIM, tk), lambda b, qi, ki: (b, 0, kblk(qi, ki)))],
        out_specs=pl.BlockSpec((None, tq, MLA_HEADS * V_DIM), lambda b, qi, ki: (b, qi, 0)),
        out_shape=jax.ShapeDtypeStruct((B, S, MLA_HEADS * V_DIM), BF16),
        scratch_shapes=[pltpu.VMEM((MLA_HEADS, 1, tq), F32), pltpu.VMEM((MLA_HEADS, V_DIM + ONES_ROWS, tq), F32)],
        compiler_params=_cparams(("parallel", "parallel", "arbitrary")),
        name="mla_attn_t",
    )(q3, kn3, kr3, vt3)


def _memkv_kernel(mem_ref, mnorm_ref, wmk_ref, wmv_ref, mkn_ref, k_ref, v_ref):
    mb = _rms(mem_ref[...], mnorm_ref[...]).astype(BF16)
    kf = _dot(mb, wmk_ref[...])
    for h in range(MEM_HEADS):
        sl = slice(h * MEM_HEAD_DIM, (h + 1) * MEM_HEAD_DIM)
        k_ref[:, sl] = _rms(kf[:, sl], mkn_ref[...])
    v_ref[...] = _dot(mb, wmv_ref[...])


def _memkv(mem2d, mnorm, w_mk, w_mv, mkn, tm):
    T = mem2d.shape[0]
    hw = MEM_HEADS * MEM_HEAD_DIM
    row = lambda n: pl.BlockSpec((tm, n), lambda i: (i, 0))
    return pl.pallas_call(
        _memkv_kernel,
        grid=(T // tm,),
        in_specs=[row(D_MODEL), _const_spec(mnorm.shape), _const_spec(w_mk.shape), _const_spec(w_mv.shape),
                  _const_spec(mkn.shape)],
        out_specs=[row(hw), row(hw)],
        out_shape=[jax.ShapeDtypeStruct((T, hw), F32), jax.ShapeDtypeStruct((T, hw), F32)],
        compiler_params=_cparams(("parallel",)),
        name="mem_kv",
    )(mem2d, mnorm, w_mk, w_mv, mkn)


def _mix_kernel(x_ref, og_ref, om_ref, mk_ref, mv_ref, wout_ref, nmem_ref, wmq_ref, mqn_ref, wmo_ref, nffn_ref,
                wrh_ref, wrl_ref, br_ref, x2_ref, h3_ref, idx_ref, gate_ref, counts_ref, cnt_ref):
    first = (pl.program_id(0) == 0) & (pl.program_id(1) == 0)

    @pl.when(first)
    def _():
        cnt_ref[...] = jnp.zeros(cnt_ref.shape, F32)

    nbm, tm, _ = x_ref.shape
    rows = nbm * tm
    flat = lambda ref: ref[...].reshape(rows, ref.shape[-1])
    if True:
        x1 = flat(x_ref) + _dot(flat(og_ref), wout_ref[0:GDN_VW, :]) + _dot(flat(om_ref), wout_ref[GDN_VW:, :])
        hb = _rms(x1, nmem_ref[...]).astype(BF16)
        qm = _dot(hb, wmq_ref[...])
        per_batch = []
        hsl = [slice(h * MEM_HEAD_DIM, (h + 1) * MEM_HEAD_DIM) for h in range(MEM_HEADS)]
        for b in range(nbm):
            br = slice(b * tm, (b + 1) * tm)
            qhs = [(_rms(qm[br, sl], mqn_ref[...]) * (MEM_HEAD_DIM ** -0.5)).astype(BF16) for sl in hsl]
            ss = [_dot_nt(qh, mk_ref[b, :, sl].astype(BF16)) for qh, sl in zip(qhs, hsl)]
            ps = []
            for s in ss:
                p = jnp.exp(s - jnp.max(s, axis=-1, keepdims=True))
                ps.append((p / jnp.sum(p, axis=-1, keepdims=True)).astype(BF16))
            heads = [_dot(p, mv_ref[b, :, sl].astype(BF16)).astype(BF16) for p, sl in zip(ps, hsl)]
            per_batch.append(jnp.concatenate(heads, axis=1))
        om = per_batch[0] if nbm == 1 else jnp.concatenate(per_batch, axis=0)
        x2 = x1 + _dot(om, wmo_ref[...])
        x2_ref[...] = x2.reshape(x2_ref.shape)
        h3 = _rms(x2, nffn_ref[...])
        hi = h3.astype(BF16)
        packed = _pack_bf16_pairs(hi.astype(F32))
        for c in range(h3_ref.shape[0]):
            h3_ref[c] = packed[:, c * LANES:(c + 1) * LANES]
        lo = (h3 - hi.astype(F32)).astype(BF16)
        wrh = wrh_ref[...]
        logits = _dot(hi, wrh) + _dot(lo, wrh) + _dot(hi, wrl_ref[...]) + br_ref[...]

        lane = lax.broadcasted_iota(jnp.int32, logits.shape, 1).astype(F32)
        vals, idxs = [], []
        cur = logits
        for _ in range(TOP_K):
            mx = jnp.max(cur, axis=-1, keepdims=True)
            ix = jnp.min(jnp.where(cur == mx, lane, float(LANES)), axis=-1, keepdims=True)
            vals.append(mx)
            idxs.append(ix)
            cur = jnp.where(lane == ix, -3e38, cur)
        es = [jnp.exp(v - vals[0]) for v in vals]
        den = es[0] + es[1] + es[2] + es[3]

        sel = jnp.zeros(logits.shape, F32)
        for k in range(TOP_K):
            sel = sel + jnp.where(lane == idxs[k], 1.0, 0.0)
        ri = lax.broadcasted_iota(jnp.int32, (rows, rows), 0)
        ci = lax.broadcasted_iota(jnp.int32, (rows, rows), 1)
        before = jnp.where(ri > ci, 1.0, 0.0).astype(BF16)
        excl = _dot(before, sel.astype(BF16)) + cnt_ref[...]
        cnt_ref[...] = cnt_ref[...] + jnp.sum(sel, axis=0, keepdims=True)
        counts_ref[...] = cnt_ref[...].astype(jnp.int32)

        idx_out = jnp.zeros(logits.shape, F32)
        gate_out = jnp.zeros(logits.shape, F32)
        for k in range(TOP_K):
            rank = jnp.sum(jnp.where(lane == idxs[k], excl, 0.0), axis=-1, keepdims=True)
            idx_out = jnp.where(lane == float(k), idxs[k], idx_out)
            idx_out = jnp.where(lane == float(TOP_K + k), rank, idx_out)
            gate_out = jnp.where(lane == float(k), es[k] / den, gate_out)
        idx_ref[...] = idx_out.astype(jnp.int32).reshape(idx_ref.shape)
        gate_ref[...] = gate_out.reshape(gate_ref.shape)


def _mix(x3, og3, om3, mk3, mv3, pw, tm):
    B, S, _ = x3.shape
    hw = MEM_HEADS * MEM_HEAD_DIM
    nbm = _pick(B, tuple(n for n in (8, 4, 2, 1) if n * tm <= MIX_ROWS)) if tm == S else 1
    nsb = S // tm
    tile = lambda n: pl.BlockSpec((nbm, tm, n), lambda b, i: (b, i, 0))
    memspec = pl.BlockSpec((nbm, N_MEM, hw), lambda b, i: (b, 0, 0))
    consts = [pw['w_out'], pw['nmem'], pw['w_mq'], pw['mqn'], pw['w_mo'], pw['nffn'], pw['wr_hi'], pw['wr_lo'],
              pw['b_r']]
    return pl.pallas_call(
        _mix_kernel,
        grid=(B // nbm, nsb),
        in_specs=[tile(D_MODEL), tile(GDN_VW), tile(MLA_HEADS * V_DIM), memspec, memspec]
                 + [_const_spec(c.shape) for c in consts],
        out_specs=[tile(D_MODEL), pl.BlockSpec((PK_CHUNKS, nbm * tm, LANES), lambda b, i: (0, b * nsb + i, 0)),
                   tile(LANES), tile(LANES), _const_spec((1, LANES))],
        out_shape=[jax.ShapeDtypeStruct((B, S, D_MODEL), F32),
                   jax.ShapeDtypeStruct((PK_CHUNKS, B * S, LANES), jnp.uint32),
                   jax.ShapeDtypeStruct((B, S, LANES), jnp.int32), jax.ShapeDtypeStruct((B, S, LANES), F32),
                   jax.ShapeDtypeStruct((1, LANES), jnp.int32)],
        scratch_shapes=[pltpu.VMEM((1, LANES), F32)],
        compiler_params=_cparams(("arbitrary", "arbitrary")),
        name="mix_mem_router",
    )(x3, og3, om3, mk3, mv3, *consts)


def _expert_kernel(be_ref, nu_ref, rows_ref, wgu_ref, bgu_ref, wd_ref, bd_ref, y_ref, wgub_ref, wdb_ref):
    i = pl.program_id(0)
    used = i < nu_ref[0]
    new_expert = (i == 0) | (be_ref[i] != be_ref[jnp.maximum(i - 1, 0)])

    @pl.when(used & new_expert)
    def _():
        def cast(r, carry):
            rs = pl.ds(pl.multiple_of(r * LANES, LANES), LANES)
            wgub_ref[rs, :] = wgu_ref[rs, :].astype(BF16)
            wdb_ref[rs, :] = wd_ref[rs, :].astype(BF16)
            return carry
        lax.fori_loop(0, D_MODEL // LANES, cast, 0)

    @pl.when(used)
    def _():
        packed = jnp.concatenate([rows_ref[c] for c in range(PK_CHUNKS)], axis=1)
        x = _unpack_bf16_pairs(packed).astype(BF16)
        acc = None
        for c in range(D_FF // FF_CHUNK):
            gs_ = slice(c * FF_CHUNK, (c + 1) * FF_CHUNK)
            us_ = slice(D_FF + c * FF_CHUNK, D_FF + (c + 1) * FF_CHUNK)
            gt = jnp.minimum(_dot(x, wgub_ref[:, gs_]) + bgu_ref[:, gs_], SWIGLU_LIMIT)
            up = jnp.clip(_dot(x, wgub_ref[:, us_]) + bgu_ref[:, us_], -SWIGLU_LIMIT, SWIGLU_LIMIT)
            act = gt * _sigmoid(SWIGLU_ALPHA * gt) * (up + 1.0)
            part = _dot(act.astype(BF16), wdb_ref[gs_, :])
            acc = part if acc is None else acc + part
        ypk = _pack_bf16_pairs((acc + bd_ref[...]).astype(BF16).astype(F32))
        for c in range(PK_CHUNKS):
            y_ref[c] = ypk[:, c * LANES:(c + 1) * LANES]

    @pl.when(jnp.logical_not(used))
    def _():
        y_ref[...] = jnp.zeros(y_ref.shape, y_ref.dtype)


def _experts(block_e, n_used, rows, w_gu, b_gu, w_down, b_down):
    n_rows = rows.shape[1]
    nb = n_rows // MOE_ROWS
    gs = pltpu.PrefetchScalarGridSpec(
        num_scalar_prefetch=2,
        grid=(nb,),
        in_specs=[pl.BlockSpec((PK_CHUNKS, MOE_ROWS, LANES), lambda i, be, nu: (0, i, 0)),
                  pl.BlockSpec((None, D_MODEL, 2 * D_FF), lambda i, be, nu: (be[i], 0, 0)),
                  pl.BlockSpec((None, 1, 2 * D_FF), lambda i, be, nu: (be[i], 0, 0)),
                  pl.BlockSpec((None, D_FF, D_MODEL), lambda i, be, nu: (be[i], 0, 0)),
                  pl.BlockSpec((None, 1, D_MODEL), lambda i, be, nu: (be[i], 0, 0))],
        out_specs=pl.BlockSpec((PK_CHUNKS, MOE_ROWS, LANES), lambda i, be, nu: (0, i, 0)),
        scratch_shapes=[pltpu.VMEM((D_MODEL, 2 * D_FF), BF16), pltpu.VMEM((D_FF, D_MODEL), BF16)],
    )
    return pl.pallas_call(
        _expert_kernel,
        grid_spec=gs,
        out_shape=jax.ShapeDtypeStruct((PK_CHUNKS, n_rows, LANES), jnp.uint32),
        compiler_params=_cparams(("arbitrary",)),
        name="moe_experts",
    )(block_e, n_used, rows, w_gu, b_gu, w_down, b_down)


def _sc_mesh():
    return plsc.VectorSubcoreMesh(core_axis_name="core", subcore_axis_name="subcore")


def _sc_scatter_rows(x3s, pos_ts, n_rows):
    C, _, L = x3s[0].shape
    K = pos_ts[0].shape[0]
    ns = len(x3s)

    @functools.partial(pl.kernel, out_type=jax.ShapeDtypeStruct((C, n_rows, L), x3s[0].dtype), mesh=_sc_mesh(),
                       scratch_types=[])
    def scatter(*refs):
        o_hbm = refs[2 * ns]
        for s in range(ns):
            x_hbm, i_hbm = refs[s], refs[ns + s]
            nwin = x3s[s].shape[1] // SC_WINDOW
            for c in range(C):
                def body(x_vmem, i_vmem, c=c):
                    for k in range(K):
                        pltpu.sync_copy(x_vmem, o_hbm.at[c].at[i_vmem.at[k]])

                pltpu.emit_pipeline(
                    body, grid=(nwin,),
                    in_specs=[pl.BlockSpec((SC_WINDOW, L), lambda i, c=c, nwin=nwin: (c * nwin + i, 0)),
                              pl.BlockSpec((K, SC_WINDOW), lambda i: (0, i))],
                    out_specs=[], core_axis_name=("core", "subcore"), dimension_semantics=(pltpu.PARALLEL,),
                )(x_hbm, i_hbm)

    return scatter(*[x.reshape(-1, L) for x in x3s], *pos_ts)


def _sc_gather_rows(table3, idxs):
    C, _, L = table3.shape
    ns = len(idxs)
    out_type = [jax.ShapeDtypeStruct((C * i.shape[0], L), table3.dtype) for i in idxs]

    @functools.partial(pl.kernel, out_type=out_type, mesh=_sc_mesh(), scratch_types=[])
    def gather(t_hbm, *refs):
        for s in range(ns):
            i_hbm, o_hbm = refs[s], refs[ns + s]
            nwin = idxs[s].shape[0] // SC_WINDOW
            for c in range(C):
                def body(i_vmem, o_vmem, c=c):
                    pltpu.sync_copy(t_hbm.at[c].at[i_vmem.at[0]], o_vmem)

                pltpu.emit_pipeline(
                    body, grid=(nwin,),
                    in_specs=[pl.BlockSpec((1, SC_WINDOW), lambda i: (0, i))],
                    out_specs=[pl.BlockSpec((SC_WINDOW, L), lambda i, c=c, nwin=nwin: (c * nwin + i, 0))],
                    core_axis_name=("core", "subcore"), dimension_semantics=(pltpu.PARALLEL,),
                )(i_hbm, o_hbm)

    outs = gather(table3, *[i.reshape(1, -1) for i in idxs])
    return [o.reshape(C, -1, L) for o in outs]


def _combine_kernel(x2_ref, g_ref, gate_ref, o_ref):
    gate = gate_ref[...]
    half = D_MODEL // 2
    for c in range(PK_CHUNKS):
        lo_s = slice(c * LANES, (c + 1) * LANES)
        hi_s = slice(half + c * LANES, half + (c + 1) * LANES)
        acc_lo = x2_ref[:, lo_s]
        acc_hi = x2_ref[:, hi_s]
        for k in range(TOP_K):
            w = g_ref[c, k]
            gk = gate[:, k:k + 1]
            acc_lo = acc_lo + pltpu.bitcast(w << 16, F32) * gk
            acc_hi = acc_hi + pltpu.bitcast(w & jnp.uint32(0xFFFF0000), F32) * gk
        o_ref[:, lo_s] = acc_lo
        o_ref[:, hi_s] = acc_hi


def _combine(x2, g4, gate, tm):
    T = x2.shape[0]
    return pl.pallas_call(
        _combine_kernel,
        grid=(T // tm,),
        in_specs=[pl.BlockSpec((tm, D_MODEL), lambda i: (i, 0)),
                  pl.BlockSpec((PK_CHUNKS, TOP_K, tm, LANES), lambda i: (0, 0, i, 0)),
                  pl.BlockSpec((tm, LANES), lambda i: (i, 0))],
        out_specs=pl.BlockSpec((tm, D_MODEL), lambda i: (i, 0)),
        out_shape=jax.ShapeDtypeStruct((T, D_MODEL), F32),
        compiler_params=_cparams(("parallel",)),
        name="moe_combine",
    )(x2, g4, gate)


def _moe(streams, ew):
    cnts = [st[3][0, :N_EXPERTS] for st in streams]
    total = sum(cnts)
    padded = (total + MOE_ROWS - 1) // MOE_ROWS * MOE_ROWS
    pad_end = jnp.cumsum(padded)
    pad_start = pad_end - padded
    experts = jnp.arange(N_EXPERTS, dtype=jnp.int32)[None, None, :]
    pos_ts = []
    base = pad_start
    for (h3p, idxr, gate, counts, x2), cnt in zip(streams, cnts):
        onehot = idxr[:, :TOP_K, None] == experts
        start = jnp.sum(jnp.where(onehot, base[None, None, :], 0), axis=-1)
        pos_ts.append((start + idxr[:, TOP_K:2 * TOP_K]).T)
        base = base + cnt
    n_assign = sum(st[0].shape[1] for st in streams) * TOP_K
    nb = -(-n_assign // MOE_ROWS) + N_EXPERTS
    starts = jnp.arange(nb, dtype=jnp.int32) * MOE_ROWS
    block_e = jnp.minimum(jnp.sum((pad_end[None, :] <= starts[:, None]).astype(jnp.int32), axis=1), N_EXPERTS - 1)
    n_used = (pad_end[-1] // MOE_ROWS).astype(jnp.int32).reshape(1)
    rows = _sc_scatter_rows([st[0] for st in streams], pos_ts, nb * MOE_ROWS)
    y_rows = _experts(block_e, n_used, rows, ew['w_gu'], ew['b_gu'], ew['w_down'], ew['b_down'])
    gs = _sc_gather_rows(y_rows, [p.reshape(-1) for p in pos_ts])
    outs = []
    for (h3p, idxr, gate, counts, x2), g in zip(streams, gs):
        T = x2.shape[0]
        outs.append(_combine(x2, g.reshape(PK_CHUNKS, TOP_K, T, LANES), gate, _pick(T, (512, 256, 128, 64))))
    return outs


def _pad_lanes(v, n=LANES, fill=0.0):
    return jnp.pad(v, (0, n - v.shape[0]), constant_values=fill).reshape(1, n)


def _prep_weights(norm_mix, w_in, conv_w, a_log, dt_bias, gdn_norm, q_a_norm, w_qb, kv_a_norm, w_kvb, q_norm,
                  k_nope_norm, k_rope_norm, w_out, norm_mem, mem_norm, w_mq, w_mk, w_mv, mq_norm, mk_norm, w_mo,
                  norm_ffn, w_router, b_router, w_gu, b_gu, w_down, b_down):
    c = np.cumsum([CONV_DIM, GDN_VW, GDN_HEADS, GDN_HEADS, Q_RANK, KV_RANK])
    w_u, w_z, w_a, w_b, w_cq, w_ckv, w_kpe = [w_in[:, lo:hi] for lo, hi in
                                              zip([0, *c], [*c, w_in.shape[1]])]
    w_s = jnp.concatenate([w_kpe, w_a, w_b], axis=1)
    w_s = jnp.pad(w_s, ((0, 0), (0, LANES - w_s.shape[1])))
    wq = w_qb.reshape(Q_RANK, MLA_HEADS, QK_DIM)
    wq = jnp.pad(wq, ((0, 0), (0, 0), (0, Q_SLAB - QK_DIM))).reshape(Q_RANK, MLA_HEADS * Q_SLAB)
    wkv = w_kvb.reshape(KV_RANK, MLA_HEADS, NOPE_DIM + V_DIM)
    wkv = jnp.concatenate([wkv[:, :, :NOPE_DIM].reshape(KV_RANK, -1), wkv[:, :, NOPE_DIM:].reshape(KV_RANK, -1)], 1)
    wr = jnp.pad(w_router, ((0, 0), (0, LANES - N_EXPERTS)))
    wr_hi = wr.astype(BF16)
    wr_lo = (wr - wr_hi.astype(F32)).astype(BF16)
    row = lambda v: v.reshape(1, -1)
    gpad = ROPE_DIM
    pw = dict(
        nmix=row(norm_mix), w_u=w_u.astype(BF16), w_z=w_z.astype(BF16), w_cq=w_cq.astype(BF16),
        w_ckv=w_ckv.astype(BF16), w_s=w_s.astype(BF16), qan=row(q_a_norm), w_qb=wq.astype(BF16),
        qn=_pad_lanes(q_norm, Q_SLAB), kvan=row(kv_a_norm), krn=_pad_lanes(k_rope_norm),
        alog=jnp.pad(a_log, (gpad, LANES - gpad - GDN_HEADS)).reshape(1, LANES),
        dtb=jnp.pad(dt_bias, (gpad, LANES - gpad - GDN_HEADS)).reshape(1, LANES),
        conv_w=conv_w, gnorm=row(gdn_norm), w_kvb=wkv.astype(BF16), knn=row(k_nope_norm),
        w_out=w_out.astype(BF16), nmem=row(norm_mem), w_mq=w_mq.astype(BF16), mqn=row(mq_norm),
        w_mo=w_mo.astype(BF16), nffn=row(norm_ffn), wr_hi=wr_hi, wr_lo=wr_lo,
        b_r=_pad_lanes(b_router, LANES, NEG_BIG),
        mnorm=row(mem_norm), w_mk=w_mk.astype(BF16), w_mv=w_mv.astype(BF16), mkn=row(mk_norm),
    )
    ew = dict(w_gu=w_gu, b_gu=b_gu.reshape(N_EXPERTS, 1, 2 * D_FF), w_down=w_down,
              b_down=b_down.reshape(N_EXPERTS, 1, D_MODEL))
    return pw, ew


def _rope_tables(P, S):
    half = ROPE_DIM // 2
    inv = ROPE_THETA ** (-jnp.arange(half, dtype=F32) / half)
    ang = (P + jnp.arange(S, dtype=jnp.int32)).astype(F32)[:, None] * inv[None, :]
    cos, sin = jnp.cos(ang), jnp.sin(ang)
    zh = jnp.zeros((S, half), F32)
    zz = jnp.zeros((S, LANES - ROPE_DIM), F32)
    return (jnp.concatenate([cos, cos, zz], 1), jnp.concatenate([-sin, zh, zz], 1),
            jnp.concatenate([zh, sin, zz], 1))


def _pick(n, prefs):
    for t in prefs:
        if n % t == 0:
            return t
    return n


def _trunk_front(x, lat_past, kpe_past, s0, conv_past, mem_k, mem_v, pw):
    B, S, D = x.shape
    P = lat_past.shape[1]
    T = B * S
    tm = _pick(S, (512, 256, 128, 64))
    tm_in = 512 if (T % 512 == 0 and (512 % S == 0 or S % 512 == 0)) else tm
    u, z, q, lat_new, small = _inproj(x.reshape(T, D), S, tm_in, pw, _rope_tables(P, S))

    LC = _pick(S, (256, 128, 64))
    NB = _pick(B, tuple(n for n in (8, 4, 2, 1) if n * (LC // CHUNK) <= GDN_UNITS))
    o_gdn, s_new, conv_new = _gdn(u.reshape(B, S, CONV_DIM), small.reshape(B, S, LANES), z.reshape(B, S, GDN_VW),
                                  pw['conv_w'], conv_past, s0, pw['gnorm'], NB, LC)

    sk = P + S
    tk = 512 if S >= 512 else -(-sk // LANES) * LANES
    skp = -(-sk // tk) * tk
    lat_all = jnp.concatenate([lat_past, lat_new.reshape(B, S, KV_RANK)], axis=1)
    kpe_new = small[:, :ROPE_DIM].reshape(B, S, ROPE_DIM)
    kpe_all = jnp.concatenate([kpe_past, kpe_new], axis=1)
    lat_all = jnp.pad(lat_all, ((0, 0), (0, skp - sk), (0, 0)))
    kpe_all = jnp.pad(kpe_all, ((0, 0), (0, skp - sk), (0, LANES - ROPE_DIM)))
    tq = _pick(S, (512, 256, 128, 64))
    key_major = tq >= LANES
    kn, kr, v = _kvproj(lat_all, kpe_all, pw['w_kvb'], pw['knn'], _pick(skp, (512,)), key_major)
    q3 = q.reshape(B, S, MLA_HEADS * Q_SLAB)
    o_mla = (_flash_t if key_major else _flash)(q3, kn, kr, v, P, S, tq, tk)

    x2, h3p, idxr, gate, counts = _mix(x, o_gdn, o_mla, mem_k, mem_v, pw, tm)
    stream = (h3p, idxr.reshape(T, LANES), gate.reshape(T, LANES), counts, x2.reshape(T, D))
    return stream, (lat_new.reshape(B, S, KV_RANK), kpe_new, s_new, conv_new)


def kernel(x_prompt, x_sample, cache_kv_latent, cache_k_rope, state_gdn, state_conv, cache_mem_k, cache_mem_v, mem_prompt, norm_mix, w_in, conv_w, a_log, dt_bias, gdn_norm, q_a_norm, w_qb, kv_a_norm, w_kvb, q_norm, k_nope_norm, k_rope_norm, w_out, norm_mem, mem_norm, w_mq, w_mk, w_mv, mq_norm, mk_norm, w_mo, norm_ffn, w_router, b_router, w_gu, b_gu, w_down, b_down):
    depth = norm_mix.shape[0]
    yp, ys = x_prompt, x_sample
    bp = x_prompt.shape[0]
    hw = MEM_HEADS * MEM_HEAD_DIM
    outs = [[] for _ in range(10)]
    for l in range(depth):
        pw, ew = _prep_weights(norm_mix[l], w_in[l], conv_w[l], a_log[l], dt_bias[l], gdn_norm[l], q_a_norm[l],
                               w_qb[l], kv_a_norm[l], w_kvb[l], q_norm[l], k_nope_norm[l], k_rope_norm[l], w_out[l],
                               norm_mem[l], mem_norm[l], w_mq[l], w_mk[l], w_mv[l], mq_norm[l], mk_norm[l], w_mo[l],
                               norm_ffn[l], w_router[l], b_router[l], w_gu[l], b_gu[l], w_down[l], b_down[l])
        nm = mem_prompt.shape[1]
        mk, mv = _memkv(mem_prompt.reshape(bp * nm, D_MODEL), pw['mnorm'], pw['w_mk'], pw['w_mv'], pw['mkn'],
                        _pick(bp * nm, (512, 256)))
        mk = mk.reshape(bp, nm, hw)
        mv = mv.reshape(bp, nm, hw)
        stream_p, (lat, kpe, s_fin, cv) = _trunk_front(
            yp, jnp.zeros((bp, 0, KV_RANK), F32), jnp.zeros((bp, 0, ROPE_DIM), F32),
            jnp.zeros((bp, GDN_HEADS, GDN_DK, GDN_DV), F32), jnp.zeros((bp, CONV_W - 1, CONV_DIM), F32), mk, mv, pw)
        bs = x_sample.shape[0]
        stream_s, (lat2, kpe2, s_fin2, cv2) = _trunk_front(
            ys, cache_kv_latent[l], cache_k_rope[l], state_gdn[l], state_conv[l],
            cache_mem_k[l].reshape(bs, nm, hw), cache_mem_v[l].reshape(bs, nm, hw), pw)
        yp2, ys2 = _moe([stream_p, stream_s], ew)
        yp, ys = yp2.reshape(yp.shape), ys2.reshape(ys.shape)
        for lst, val in zip(outs, (lat, kpe, s_fin, cv, mk.reshape(bp, nm, MEM_HEADS, MEM_HEAD_DIM),
                                   mv.reshape(bp, nm, MEM_HEADS, MEM_HEAD_DIM), lat2, kpe2, s_fin2, cv2)):
            lst.append(val)
    return (yp, ys) + tuple(jnp.stack(o) for o in outs)
```

```python
import functools
import math

import numpy as np
import jax
import jax.numpy as jnp
from jax import lax
from jax.experimental import pallas as pl
from jax.experimental.pallas import tpu as pltpu
from jax.experimental.pallas import tpu_sc as plsc

F32 = jnp.float32
BF16 = jnp.bfloat16

D_MODEL = 1024
CHUNK = 64
EPS = 1e-6
GDN_HEADS = 4
GDN_DK = 128
GDN_DV = 128
CONV_W = 4
GDN_QK = GDN_HEADS * GDN_DK
GDN_VW = GDN_HEADS * GDN_DV
CONV_DIM = 2 * GDN_QK + GDN_VW
MLA_HEADS = 4
Q_RANK = 384
KV_RANK = 256
NOPE_DIM = 128
ROPE_DIM = 64
V_DIM = 128
QK_DIM = NOPE_DIM + ROPE_DIM
ROPE_THETA = 10000.0
N_MEM = 256
MEM_HEADS = 4
MEM_HEAD_DIM = 128
N_EXPERTS = 32
TOP_K = 4
D_FF = D_MODEL
SWIGLU_ALPHA = 1.702
SWIGLU_LIMIT = 7.0

LANES = 128
Q_SLAB = 2 * LANES
NEG_BIG = -1e30
VMEM_LIMIT = 56 * 1024 * 1024
MOE_ROWS = 512
GDN_UNITS = 8
FF_CHUNK = 256
SC_WINDOW = 128
PK_CHUNKS = D_MODEL // 2 // LANES
SUBTILE_ROWS = 256
ONES_ROWS = 16
MIX_ROWS = 512


def _cparams(sem):
    return pltpu.CompilerParams(dimension_semantics=sem, vmem_limit_bytes=VMEM_LIMIT)


def _dot(a, b):
    return jnp.dot(a, b, preferred_element_type=F32)


def _dot_nt(a, b):
    return lax.dot_general(a, b, (((1,), (1,)), ((), ())), preferred_element_type=F32)


def _dot_tn(a, b):
    return lax.dot_general(a, b, (((0,), (0,)), ((), ())), preferred_element_type=F32)


def _rms(x, gain, n=None):
    n = x.shape[-1] if n is None else n
    ss = jnp.sum(x * x, axis=-1, keepdims=True) * (1.0 / n)
    return (x * lax.rsqrt(ss + EPS)) * gain


def _sigmoid(x):
    return 1.0 / (1.0 + jnp.exp(-x))


def _rope128(r, cos, sna, snb):
    return r * cos + pltpu.roll(r, 96, 1) * sna + pltpu.roll(r, 32, 1) * snb


def _pack_bf16_pairs(x):
    n = x.shape[1] // 2
    lo = pltpu.bitcast(x[:, :n], jnp.uint32) >> 16
    hi = pltpu.bitcast(x[:, n:], jnp.uint32) & jnp.uint32(0xFFFF0000)
    return lo | hi


def _unpack_bf16_pairs(p):
    lo = pltpu.bitcast(p << 16, F32)
    hi = pltpu.bitcast(p & jnp.uint32(0xFFFF0000), F32)
    return jnp.concatenate([lo, hi], axis=1)


def _subtiles(rows):
    n = rows // SUBTILE_ROWS if rows % SUBTILE_ROWS == 0 else 1
    step = rows // n
    return [slice(i * step, (i + 1) * step) for i in range(n)]


def _const_spec(shape):
    nd = len(shape)
    return pl.BlockSpec(shape, lambda *_: (0,) * nd)


def _inproj_kernel(x_ref, nmix_ref, wu_ref, wz_ref, wcq_ref, wckv_ref, ws_ref, qan_ref, wqb_ref, qn_ref,
                   kvan_ref, krn_ref, alog_ref, dtb_ref, cos_ref, sna_ref, snb_ref,
                   u_ref, z_ref, q_ref, lat_ref, small_ref):
    for rs in _subtiles(x_ref.shape[0]):
        x = x_ref[rs, :]
        hb = _rms(x, nmix_ref[...]).astype(BF16)
        u_ref[rs, :] = _dot(hb, wu_ref[...])
        z_ref[rs, :] = _dot(hb, wz_ref[...])
        cos, sna, snb = cos_ref[rs, :], sna_ref[rs, :], snb_ref[rs, :]

        cq = _rms(_dot(hb, wcq_ref[...]), qan_ref[...]).astype(BF16)
        qf = _dot(cq, wqb_ref[...])
        scale = QK_DIM ** -0.5
        for h in range(MLA_HEADS):
            slab = qf[:, h * Q_SLAB:(h + 1) * Q_SLAB]
            slab = _rms(slab, qn_ref[...], n=QK_DIM)
            nope = slab[:, :LANES]
            ropd = _rope128(slab[:, LANES:], cos, sna, snb)
            q_ref[rs, h * Q_SLAB:h * Q_SLAB + LANES] = (nope * scale).astype(BF16)
            q_ref[rs, h * Q_SLAB + LANES:(h + 1) * Q_SLAB] = (ropd * scale).astype(BF16)

        lat_ref[rs, :] = _rms(_dot(hb, wckv_ref[...]), kvan_ref[...])

        sm = _dot(hb, ws_ref[...])
        lane = lax.broadcasted_iota(jnp.int32, sm.shape, 1)
        kp = jnp.where(lane < ROPE_DIM, sm, 0.0)
        kpe = _rope128(_rms(kp, krn_ref[...], n=ROPE_DIM), cos, sna, snb)
        sp = sm + dtb_ref[...]
        softplus = jnp.maximum(sp, 0.0) + jnp.log1p(jnp.exp(-jnp.abs(sp)))
        g = -jnp.exp(alog_ref[...]) * softplus
        beta = _sigmoid(sm)
        small_ref[rs, :] = jnp.where(lane < ROPE_DIM, kpe,
                                     jnp.where(lane < ROPE_DIM + GDN_HEADS, g,
                                               jnp.where(lane < ROPE_DIM + 2 * GDN_HEADS, beta, 0.0)))


def _inproj(x2d, S, tm, pw, tabs):
    T = x2d.shape[0]
    if tm > S:
        tabs = [jnp.tile(t, (tm // S, 1)) for t in tabs]
    nblk_s = max(S // tm, 1)
    row = lambda n: pl.BlockSpec((tm, n), lambda i: (i, 0))
    tab = pl.BlockSpec((tm, LANES), lambda i: (i % nblk_s, 0))
    consts = [pw['nmix'], pw['w_u'], pw['w_z'], pw['w_cq'], pw['w_ckv'], pw['w_s'], pw['qan'], pw['w_qb'],
              pw['qn'], pw['kvan'], pw['krn'], pw['alog'], pw['dtb']]
    return pl.pallas_call(
        _inproj_kernel,
        grid=(T // tm,),
        in_specs=[row(D_MODEL)] + [_const_spec(c.shape) for c in consts] + [tab, tab, tab],
        out_specs=[row(CONV_DIM), row(GDN_VW), row(MLA_HEADS * Q_SLAB), row(KV_RANK), row(LANES)],
        out_shape=[jax.ShapeDtypeStruct((T, CONV_DIM), F32), jax.ShapeDtypeStruct((T, GDN_VW), F32),
                   jax.ShapeDtypeStruct((T, MLA_HEADS * Q_SLAB), BF16), jax.ShapeDtypeStruct((T, KV_RANK), F32),
                   jax.ShapeDtypeStruct((T, LANES), F32)],
        compiler_params=_cparams(("parallel",)),
        name="inproj",
    )(x2d, *consts, *tabs)


def _split3(x):
    hi = x.astype(BF16)
    r = x - hi.astype(F32)
    mid = r.astype(BF16)
    lo = (r - mid.astype(F32)).astype(BF16)
    return hi, mid, lo


def _gdn_kernel(u_ref, small_ref, z_ref, convw_ref, cpast_ref, s0_ref, gnorm_ref,
                to_ref, trilm_ref, strictm_ref, same2_ref, lvl_ref,
                o_ref, sfin_ref, cnew_ref, ext_ref, uc_ref, state_ref,
                qf_ref, kf_ref, vb_ref, bt_ref, gcum_ref, glast_ref, kk_ref, qk_ref, mb_ref, x_ref, qkb_ref,
                kbe_ref, qg_ref, kdec_ref, egl_ref, t1_ref, uu_ref, ww_ref, vn_ref, qs_ref, *, NB, LC):
    j = pl.program_id(1)
    nj = pl.num_programs(1)
    PADR = 8
    C = LC // CHUNK
    U = NB * C
    HR = GDN_HEADS * CHUNK

    @pl.when(j == 0)
    def _():
        state_ref[...] = s0_ref[...]
        ext_ref[:, PADR - (CONV_W - 1):PADR, :] = cpast_ref[...]

    w = convw_ref[...]
    for nb in range(NB):
        ext_ref[nb, PADR:PADR + LC, :] = u_ref[nb]
        acc = ext_ref[nb, PADR:PADR + LC, :] * w[CONV_W - 1:CONV_W, :]
        for t in range(1, CONV_W):
            acc = acc + ext_ref[nb, PADR - t:PADR - t + LC, :] * w[CONV_W - 1 - t:CONV_W - t, :]
        uc_ref[nb] = acc * _sigmoid(acc)
        ext_ref[nb, 0:PADR, :] = ext_ref[nb, LC:LC + PADR, :]

    @pl.when(j == nj - 1)
    def _():
        cnew_ref[...] = ext_ref[:, PADR - (CONV_W - 1):PADR, :]

    units = [(nb, c) for nb in range(NB) for c in range(C)]
    g0 = ROPE_DIM
    b0 = ROPE_DIM + GDN_HEADS
    to = to_ref[...]
    for u, (nb, c) in enumerate(units):
        rows = slice(c * CHUNK, (c + 1) * CHUNK)
        sm = small_ref[nb, rows, :]
        gl = sum(_dot(to, part) for part in _split3(sm))
        for h in range(GDN_HEADS):
            hr = slice(h * CHUNK, (h + 1) * CHUNK)
            q = uc_ref[nb, rows, h * GDN_DK:(h + 1) * GDN_DK]
            k = uc_ref[nb, rows, GDN_QK + h * GDN_DK:GDN_QK + (h + 1) * GDN_DK]
            v = uc_ref[nb, rows, 2 * GDN_QK + h * GDN_DV:2 * GDN_QK + (h + 1) * GDN_DV]
            beta = jnp.broadcast_to(sm[:, b0 + h:b0 + h + 1], (CHUNK, LANES))
            qf_ref[u, hr, :] = (q * lax.rsqrt(jnp.sum(q * q, -1, keepdims=True) + EPS)) * (GDN_DK ** -0.5)
            kf_ref[u, hr, :] = k * lax.rsqrt(jnp.sum(k * k, -1, keepdims=True) + EPS)
            vb_ref[u, hr, :] = (v * beta).astype(BF16)
            bt_ref[u, hr, :] = beta
            gcum_ref[u, hr, :] = jnp.broadcast_to(gl[:CHUNK, g0 + h:g0 + h + 1], (CHUNK, LANES))
            glast_ref[u, hr, :] = jnp.broadcast_to(gl[CHUNK:, g0 + h:g0 + h + 1], (CHUNK, LANES))

    for u in range(U):
        k = kf_ref[u]
        kbf = k.astype(BF16)
        kk_ref[u] = _dot_nt((k * bt_ref[u]).astype(BF16), kbf)
        qk_ref[u] = _dot_nt(qf_ref[u].astype(BF16), kbf)

    trilm = trilm_ref[...]
    eye = trilm - strictm_ref[...]
    for u in range(U):
        gcum = gcum_ref[u]
        grow = gcum.T[0:1, :]
        gcol = jnp.concatenate([gcum, gcum], axis=1)
        decay = jnp.exp(jnp.minimum(gcol - grow, 0.0)) * trilm
        m = kk_ref[u] * (decay * strictm_ref[...])
        mb_ref[u] = m.astype(BF16)
        x_ref[u] = eye - m * same2_ref[...]
        qkb_ref[u] = (qk_ref[u] * decay).astype(BF16)
        egc = jnp.exp(gcum)
        k = kf_ref[u]
        kbe_ref[u] = (k * bt_ref[u] * egc).astype(BF16)
        qg_ref[u] = (qf_ref[u] * egc).astype(BF16)
        kdec_ref[u] = (k * jnp.exp(glast_ref[u] - gcum)).astype(BF16)
        egl_ref[u] = jnp.exp(glast_ref[u])

    for lvl in range(lvl_ref.shape[0]):
        lm = lvl_ref[lvl]
        for u in range(U):
            t1_ref[u] = _dot(mb_ref[u] * lm, x_ref[u].astype(BF16)).astype(BF16)
        for u in range(U):
            x = x_ref[u]
            x_ref[u] = x - _dot(x.astype(BF16), t1_ref[u])

    for u in range(U):
        xb = x_ref[u].astype(BF16)
        uu_ref[u] = _dot(xb, vb_ref[u])
        ww_ref[u] = _dot(xb, kbe_ref[u]).astype(BF16)

    gnorm = gnorm_ref[...]
    hrs = [slice(h * CHUNK, (h + 1) * CHUNK) for h in range(GDN_HEADS)]
    for c in range(C):
        rows = slice(c * CHUNK, (c + 1) * CHUNK)
        us = [nb * C + c for nb in range(NB)]
        for nb, u in enumerate(us):
            for h, hr in enumerate(hrs):
                stb = state_ref[nb, h].astype(BF16)
                r = _dot(jnp.concatenate([ww_ref[u, hr, :], qg_ref[u, hr, :]], axis=0), stb)
                vn_ref[u, hr, :] = (uu_ref[u, hr, :] - r[:CHUNK]).astype(BF16)
                qs_ref[u, hr, :] = r[CHUNK:]
        outs = [qs_ref[u] + _dot(qkb_ref[u], vn_ref[u]) for u in us]
        for nb, u in enumerate(us):
            for h, hr in enumerate(hrs):
                state_ref[nb, h] = (state_ref[nb, h] * egl_ref[u, h * CHUNK:h * CHUNK + 1, :]
                                    + _dot_tn(kdec_ref[u, hr, :], vn_ref[u, hr, :]))
        for nb, u in enumerate(us):
            for h, hr in enumerate(hrs):
                zz = z_ref[nb, rows, h * GDN_DV:(h + 1) * GDN_DV]
                og = _rms(outs[nb][hr, :], gnorm) * (zz * _sigmoid(zz))
                o_ref[nb, rows, h * GDN_DV:(h + 1) * GDN_DV] = og.astype(BF16)

    @pl.when(j == nj - 1)
    def _():
        sfin_ref[...] = state_ref[...]


def _gdn_masks():
    hr = GDN_HEADS * CHUNK
    i = np.arange(hr)[:, None]
    j = np.arange(hr)[None, :]
    same_head = (i // CHUNK) == (j // CHUNK)
    tril = same_head & (i >= j)
    strict = same_head & (i > j)
    same2 = strict & ((i // 2) == (j // 2))
    lvls = []
    blk = 2
    while blk < CHUNK:
        lvls.append(strict & ((i // (2 * blk)) == (j // (2 * blk))) & ((i // blk) != (j // blk)))
        blk *= 2
    fr = np.arange(CHUNK)
    to = np.concatenate([fr[:, None] >= fr[None, :], np.ones((CHUNK, CHUNK), bool)], axis=0)
    f = lambda a: jnp.asarray(a.astype(np.float32))
    return (jnp.asarray(to.astype(np.float32), dtype=BF16), f(tril), f(strict), f(same2),
            jnp.asarray(np.stack(lvls).astype(np.float32), dtype=BF16))


def _gdn(u3, small3, z3, conv_w, conv_past, s0, gnorm, NB, LC):
    B, S, _ = u3.shape
    C = LC // CHUNK
    U = NB * C
    HR = GDN_HEADS * CHUNK
    masks = _gdn_masks()
    tile = lambda n: pl.BlockSpec((NB, LC, n), lambda b, j: (b, j, 0))
    stspec = pl.BlockSpec((NB, GDN_HEADS, GDN_DK, GDN_DV), lambda b, j: (b, 0, 0, 0))
    cvspec = pl.BlockSpec((NB, CONV_W - 1, CONV_DIM), lambda b, j: (b, 0, 0))
    vm = lambda shape, dt: pltpu.VMEM(shape, dt)
    return pl.pallas_call(
        functools.partial(_gdn_kernel, NB=NB, LC=LC),
        grid=(B // NB, S // LC),
        in_specs=[tile(CONV_DIM), tile(LANES), tile(GDN_VW), _const_spec(conv_w.shape), cvspec, stspec,
                  _const_spec(gnorm.shape)] + [_const_spec(m.shape) for m in masks],
        out_specs=[tile(GDN_VW), stspec, cvspec],
        out_shape=[jax.ShapeDtypeStruct((B, S, GDN_VW), BF16),
                   jax.ShapeDtypeStruct((B, GDN_HEADS, GDN_DK, GDN_DV), F32),
                   jax.ShapeDtypeStruct((B, CONV_W - 1, CONV_DIM), F32)],
        scratch_shapes=[vm((NB, LC + 8, CONV_DIM), F32), vm((NB, LC, CONV_DIM), F32),
                        vm((NB, GDN_HEADS, GDN_DK, GDN_DV), F32),
                        vm((U, HR, LANES), F32), vm((U, HR, LANES), F32), vm((U, HR, LANES), BF16),
                        vm((U, HR, LANES), F32),
                        vm((U, HR, LANES), F32), vm((U, HR, LANES), F32),
                        vm((U, HR, HR), F32), vm((U, HR, HR), F32),
                        vm((U, HR, HR), BF16), vm((U, HR, HR), F32), vm((U, HR, HR), BF16),
                        vm((U, HR, LANES), BF16), vm((U, HR, LANES), BF16), vm((U, HR, LANES), BF16),
                        vm((U, HR, LANES), F32), vm((U, HR, HR), BF16),
                        vm((U, HR, LANES), F32), vm((U, HR, LANES), BF16),
                        vm((U, HR, LANES), BF16), vm((U, HR, LANES), F32)],
        compiler_params=_cparams(("parallel", "arbitrary")),
        name="gdn",
    )(u3, small3, z3, conv_w, conv_past, s0, gnorm, *masks)


def _kvproj_kernel(lat_ref, kpe_ref, wkvb_ref, knn_ref, kn_ref, kr_ref, v_ref, *, v_transposed):
    kv = _dot(lat_ref[...].astype(BF16), wkvb_ref[...])
    hw = MLA_HEADS * NOPE_DIM
    for h in range(MLA_HEADS):
        kh = kv[:, h * NOPE_DIM:(h + 1) * NOPE_DIM]
        kn_ref[:, h * NOPE_DIM:(h + 1) * NOPE_DIM] = _rms(kh, knn_ref[...]).astype(BF16)
    v = kv[:, hw:]
    v_ref[...] = (v.T if v_transposed else v).astype(BF16)
    kp = kpe_ref[...]
    lane = lax.broadcasted_iota(jnp.int32, kp.shape, 1)
    kr_ref[...] = jnp.where(lane < ROPE_DIM, kp, 0.0).astype(BF16)


def _kvproj(lat3, kpe3, w_kvb, knn, tm, v_transposed):
    B, sk, _ = lat3.shape
    hw = MLA_HEADS * V_DIM
    row = lambda n: pl.BlockSpec((None, tm, n), lambda b, i: (b, i, 0))
    if v_transposed:
        vspec, vshape = pl.BlockSpec((None, hw, tm), lambda b, i: (b, 0, i)), (B, hw, sk)
    else:
        vspec, vshape = row(hw), (B, sk, hw)
    return pl.pallas_call(
        functools.partial(_kvproj_kernel, v_transposed=v_transposed),
        grid=(B, sk // tm),
        in_specs=[row(KV_RANK), row(LANES), _const_spec(w_kvb.shape), _const_spec(knn.shape)],
        out_specs=[row(MLA_HEADS * NOPE_DIM), row(LANES), vspec],
        out_shape=[jax.ShapeDtypeStruct((B, sk, MLA_HEADS * NOPE_DIM), BF16),
                   jax.ShapeDtypeStruct((B, sk, LANES), BF16), jax.ShapeDtypeStruct(vshape, BF16)],
        compiler_params=_cparams(("parallel", "parallel")),
        name="kvproj",
    )(lat3, kpe3, w_kvb, knn)


def _last_kblock(qi, tq, tk, P, nk):
    last_key = ((P + qi * tq + tq - 1) // CHUNK) * CHUNK + CHUNK - 1
    return jnp.minimum(last_key // tk, nk - 1)


def _flash_kernel(q_ref, kn_ref, kr_ref, v_ref, o_ref, m_ref, l_ref, acc_ref, *, tq, tk, P, S, nk):
    qi = pl.program_id(1)
    ki = pl.program_id(2)

    @pl.when(ki == 0)
    def _():
        m_ref[...] = jnp.full(m_ref.shape, NEG_BIG, F32)
        l_ref[...] = jnp.zeros(l_ref.shape, F32)
        acc_ref[...] = jnp.zeros(acc_ref.shape, F32)

    @pl.when(ki <= _last_kblock(qi, tq, tk, P, nk))
    def _():
        qpos = P + qi * tq + lax.broadcasted_iota(jnp.int32, (tq, tk), 0)
        kpos = ki * tk + lax.broadcasted_iota(jnp.int32, (tq, tk), 1)
        mask = ((kpos // CHUNK) <= (qpos // CHUNK)) & (kpos < P + S)
        kr = kr_ref[...]
        for h in range(MLA_HEADS):
            qh = q_ref[:, h * Q_SLAB:(h + 1) * Q_SLAB]
            kh = jnp.concatenate([kn_ref[:, h * NOPE_DIM:(h + 1) * NOPE_DIM], kr], axis=1)
            s = jnp.where(mask, _dot_nt(qh, kh), NEG_BIG)
            m_prev = m_ref[h]
            m_new = jnp.maximum(m_prev, jnp.max(s, axis=-1, keepdims=True))
            alpha = jnp.exp(m_prev - m_new)
            p = jnp.exp(s - m_new)
            l_ref[h] = alpha * l_ref[h] + jnp.sum(p, axis=-1, keepdims=True)
            acc_ref[h] = alpha * acc_ref[h] + _dot(p.astype(BF16), v_ref[:, h * V_DIM:(h + 1) * V_DIM])
            m_ref[h] = m_new

    @pl.when(ki == nk - 1)
    def _():
        for h in range(MLA_HEADS):
            o_ref[:, h * V_DIM:(h + 1) * V_DIM] = (acc_ref[h] / l_ref[h]).astype(BF16)


def _flash(q3, kn3, kr3, v3, P, S, tq, tk):
    B = q3.shape[0]
    skp = kn3.shape[1]
    nk = skp // tk
    kmap = lambda b, qi, ki: (b, jnp.minimum(ki, _last_kblock(qi, tq, tk, P, nk)), 0)
    return pl.pallas_call(
        functools.partial(_flash_kernel, tq=tq, tk=tk, P=P, S=S, nk=nk),
        grid=(B, S // tq, nk),
        in_specs=[pl.BlockSpec((None, tq, MLA_HEADS * Q_SLAB), lambda b, qi, ki: (b, qi, 0)),
                  pl.BlockSpec((None, tk, MLA_HEADS * NOPE_DIM), kmap),
                  pl.BlockSpec((None, tk, LANES), kmap),
                  pl.BlockSpec((None, tk, MLA_HEADS * V_DIM), kmap)],
        out_specs=pl.BlockSpec((None, tq, MLA_HEADS * V_DIM), lambda b, qi, ki: (b, qi, 0)),
        out_shape=jax.ShapeDtypeStruct((B, S, MLA_HEADS * V_DIM), BF16),
        scratch_shapes=[pltpu.VMEM((MLA_HEADS, tq, 1), F32), pltpu.VMEM((MLA_HEADS, tq, 1), F32),
                        pltpu.VMEM((MLA_HEADS, tq, V_DIM), F32)],
        compiler_params=_cparams(("parallel", "parallel", "arbitrary")),
        name="mla_attn",
    )(q3, kn3, kr3, v3)


def _flash_t_kernel(q_ref, kn_ref, kr_ref, vt_ref, o_ref, m_ref, acc_ref, *, tq, tk, P, S, nk):
    qi = pl.program_id(1)
    ki = pl.program_id(2)
    q0 = P + qi * tq
    k0 = ki * tk

    @pl.when(ki == 0)
    def _():
        m_ref[...] = jnp.full(m_ref.shape, NEG_BIG, F32)
        acc_ref[...] = jnp.zeros(acc_ref.shape, F32)

    def step(masked):
        kr = kr_ref[...]
        ones = jnp.ones((ONES_ROWS, tk), BF16)
        if masked:
            kpos = k0 + lax.broadcasted_iota(jnp.int32, (tk, 1), 0)
            qpos = q0 + lax.broadcasted_iota(jnp.int32, (1, tq), 1)
            mask = ((kpos // CHUNK) <= (qpos // CHUNK)) & (kpos < P + S)
        sts = []
        for h in range(MLA_HEADS):
            kh = jnp.concatenate([kn_ref[:, h * NOPE_DIM:(h + 1) * NOPE_DIM], kr], axis=1)
            sts.append(_dot_nt(kh, q_ref[:, h * Q_SLAB:(h + 1) * Q_SLAB]))
        ps, alphas = [], []
        for h in range(MLA_HEADS):
            st = sts[h]
            if masked:
                st = jnp.where(mask, st, NEG_BIG)
            m_prev = m_ref[h]
            m_new = jnp.maximum(m_prev, jnp.max(st, axis=0, keepdims=True))
            alphas.append(jnp.exp(m_prev - m_new))
            ps.append(jnp.exp(st - m_new).astype(BF16))
            m_ref[h] = m_new
        for h in range(MLA_HEADS):
            vt1 = jnp.concatenate([vt_ref[h * V_DIM:(h + 1) * V_DIM, :], ones], axis=0)
            acc_ref[h] = alphas[h] * acc_ref[h] + _dot(vt1, ps[h])

    needed = ki <= _last_kblock(qi, tq, tk, P, nk)
    full = ((k0 + tk - 1) // CHUNK <= q0 // CHUNK) & (k0 + tk <= P + S)
    pl.when(needed & full)(functools.partial(step, False))
    pl.when(needed & jnp.logical_not(full))(functools.partial(step, True))

    @pl.when(ki == nk - 1)
    def _():
        for h in range(MLA_HEADS):
            acc = acc_ref[h]
            o_ref[:, h * V_DIM:(h + 1) * V_DIM] = (acc[:V_DIM] / acc[V_DIM:V_DIM + 1]).T.astype(BF16)


def _flash_t(q3, kn3, kr3, vt3, P, S, tq, tk):
    B = q3.shape[0]
    skp = kn3.shape[1]
    nk = skp // tk
    kblk = lambda qi, ki: jnp.minimum(ki, _last_kblock(qi, tq, tk, P, nk))
    kmap = lambda b, qi, ki: (b, kblk(qi, ki), 0)
    return pl.pallas_call(
        functools.partial(_flash_t_kernel, tq=tq, tk=tk, P=P, S=S, nk=nk),
        grid=(B, S // tq, nk),
        in_specs=[pl.BlockSpec((None, tq, MLA_HEADS * Q_SLAB), lambda b, qi, ki: (b, qi, 0)),
                  pl.BlockSpec((None, tk, MLA_HEADS * NOPE_DIM), kmap),
                  pl.BlockSpec((None, tk, LANES), kmap),
                  pl.BlockSpec((None, MLA_HEADS * V_DIM, tk), lambda b, qi, ki: (b, 0, kblk(qi, ki)))],
        out_specs=pl.BlockSpec((None, tq, MLA_HEADS * V_DIM), lambda b, qi, ki: (b, qi, 0)),
        out_shape=jax.ShapeDtypeStruct((B, S, MLA_HEADS * V_DIM), BF16),
        scratch_shapes=[pltpu.VMEM((MLA_HEADS, 1, tq), F32), pltpu.VMEM((MLA_HEADS, V_DIM + ONES_ROWS, tq), F32)],
        compiler_params=_cparams(("parallel", "parallel", "arbitrary")),
        name="mla_attn_t",
    )(q3, kn3, kr3, vt3)


def _memkv_kernel(mem_ref, mnorm_ref, wmk_ref, wmv_ref, mkn_ref, k_ref, v_ref):
    mb = _rms(mem_ref[...], mnorm_ref[...]).astype(BF16)
    kf = _dot(mb, wmk_ref[...])
    for h in range(MEM_HEADS):
        sl = slice(h * MEM_HEAD_DIM, (h + 1) * MEM_HEAD_DIM)
        k_ref[:, sl] = _rms(kf[:, sl], mkn_ref[...])
    v_ref[...] = _dot(mb, wmv_ref[...])


def _memkv(mem2d, mnorm, w_mk, w_mv, mkn, tm):
    T = mem2d.shape[0]
    hw = MEM_HEADS * MEM_HEAD_DIM
    row = lambda n: pl.BlockSpec((tm, n), lambda i: (i, 0))
    return pl.pallas_call(
        _memkv_kernel,
        grid=(T // tm,),
        in_specs=[row(D_MODEL), _const_spec(mnorm.shape), _const_spec(w_mk.shape), _const_spec(w_mv.shape),
                  _const_spec(mkn.shape)],
        out_specs=[row(hw), row(hw)],
        out_shape=[jax.ShapeDtypeStruct((T, hw), F32), jax.ShapeDtypeStruct((T, hw), F32)],
        compiler_params=_cparams(("parallel",)),
        name="mem_kv",
    )(mem2d, mnorm, w_mk, w_mv, mkn)


def _mix_kernel(x_ref, og_ref, om_ref, mk_ref, mv_ref, wout_ref, nmem_ref, wmq_ref, mqn_ref, wmo_ref, nffn_ref,
                wrh_ref, wrl_ref, br_ref, x2_ref, h3_ref, idx_ref, gate_ref, counts_ref, cnt_ref):
    first = (pl.program_id(0) == 0) & (pl.program_id(1) == 0)

    @pl.when(first)
    def _():
        cnt_ref[...] = jnp.zeros(cnt_ref.shape, F32)

    nbm, tm, _ = x_ref.shape
    rows = nbm * tm
    flat = lambda ref: ref[...].reshape(rows, ref.shape[-1])
    if True:
        x1 = flat(x_ref) + _dot(flat(og_ref), wout_ref[0:GDN_VW, :]) + _dot(flat(om_ref), wout_ref[GDN_VW:, :])
        hb = _rms(x1, nmem_ref[...]).astype(BF16)
        qm = _dot(hb, wmq_ref[...])
        per_batch = []
        hsl = [slice(h * MEM_HEAD_DIM, (h + 1) * MEM_HEAD_DIM) for h in range(MEM_HEADS)]
        for b in range(nbm):
            br = slice(b * tm, (b + 1) * tm)
            qhs = [(_rms(qm[br, sl], mqn_ref[...]) * (MEM_HEAD_DIM ** -0.5)).astype(BF16) for sl in hsl]
            ss = [_dot_nt(qh, mk_ref[b, :, sl].astype(BF16)) for qh, sl in zip(qhs, hsl)]
            ps = []
            for s in ss:
                p = jnp.exp(s - jnp.max(s, axis=-1, keepdims=True))
                ps.append((p / jnp.sum(p, axis=-1, keepdims=True)).astype(BF16))
            heads = [_dot(p, mv_ref[b, :, sl].astype(BF16)).astype(BF16) for p, sl in zip(ps, hsl)]
            per_batch.append(jnp.concatenate(heads, axis=1))
        om = per_batch[0] if nbm == 1 else jnp.concatenate(per_batch, axis=0)
        x2 = x1 + _dot(om, wmo_ref[...])
        x2_ref[...] = x2.reshape(x2_ref.shape)
        h3 = _rms(x2, nffn_ref[...])
        hi = h3.astype(BF16)
        packed = _pack_bf16_pairs(hi.astype(F32))
        for c in range(h3_ref.shape[0]):
            h3_ref[c] = packed[:, c * LANES:(c + 1) * LANES]
        lo = (h3 - hi.astype(F32)).astype(BF16)
        wrh = wrh_ref[...]
        logits = _dot(hi, wrh) + _dot(lo, wrh) + _dot(hi, wrl_ref[...]) + br_ref[...]

        lane = lax.broadcasted_iota(jnp.int32, logits.shape, 1).astype(F32)
        vals, idxs = [], []
        cur = logits
        for _ in range(TOP_K):
            mx = jnp.max(cur, axis=-1, keepdims=True)
            ix = jnp.min(jnp.where(cur == mx, lane, float(LANES)), axis=-1, keepdims=True)
            vals.append(mx)
            idxs.append(ix)
            cur = jnp.where(lane == ix, -3e38, cur)
        es = [jnp.exp(v - vals[0]) for v in vals]
        den = es[0] + es[1] + es[2] + es[3]

        sel = jnp.zeros(logits.shape, F32)
        for k in range(TOP_K):
            sel = sel + jnp.where(lane == idxs[k], 1.0, 0.0)
        ri = lax.broadcasted_iota(jnp.int32, (rows, rows), 0)
        ci = lax.broadcasted_iota(jnp.int32, (rows, rows), 1)
        before = jnp.where(ri > ci, 1.0, 0.0).astype(BF16)
        excl = _dot(before, sel.astype(BF16)) + cnt_ref[...]
        cnt_ref[...] = cnt_ref[...] + jnp.sum(sel, axis=0, keepdims=True)
        counts_ref[...] = cnt_ref[...].astype(jnp.int32)

        idx_out = jnp.zeros(logits.shape, F32)
        gate_out = jnp.zeros(logits.shape, F32)
        for k in range(TOP_K):
            rank = jnp.sum(jnp.where(lane == idxs[k], excl, 0.0), axis=-1, keepdims=True)
            idx_out = jnp.where(lane == float(k), idxs[k], idx_out)
            idx_out = jnp.where(lane == float(TOP_K + k), rank, idx_out)
            gate_out = jnp.where(lane == float(k), es[k] / den, gate_out)
        idx_ref[...] = idx_out.astype(jnp.int32).reshape(idx_ref.shape)
        gate_ref[...] = gate_out.reshape(gate_ref.shape)


def _mix(x3, og3, om3, mk3, mv3, pw, tm, b0, B):
    S = x3.shape[1]
    hw = MEM_HEADS * MEM_HEAD_DIM
    nbm = _pick(B, tuple(n for n in (8, 4, 2, 1) if n * tm <= MIX_ROWS and b0 % n == 0)) if tm == S else 1
    nsb = S // tm
    boff = b0 // nbm
    tile_in = lambda n: pl.BlockSpec((nbm, tm, n), lambda b, i: (b + boff, i, 0))
    tile = lambda n: pl.BlockSpec((nbm, tm, n), lambda b, i: (b, i, 0))
    memspec = pl.BlockSpec((nbm, N_MEM, hw), lambda b, i: (b + boff, 0, 0))
    consts = [pw['w_out'], pw['nmem'], pw['w_mq'], pw['mqn'], pw['w_mo'], pw['nffn'], pw['wr_hi'], pw['wr_lo'],
              pw['b_r']]
    return pl.pallas_call(
        _mix_kernel,
        grid=(B // nbm, nsb),
        in_specs=[tile_in(D_MODEL), tile_in(GDN_VW), tile_in(MLA_HEADS * V_DIM), memspec, memspec]
                 + [_const_spec(c.shape) for c in consts],
        out_specs=[tile(D_MODEL), pl.BlockSpec((PK_CHUNKS, nbm * tm, LANES), lambda b, i: (0, b * nsb + i, 0)),
                   tile(LANES), tile(LANES), _const_spec((1, LANES))],
        out_shape=[jax.ShapeDtypeStruct((B, S, D_MODEL), F32),
                   jax.ShapeDtypeStruct((PK_CHUNKS, B * S, LANES), jnp.uint32),
                   jax.ShapeDtypeStruct((B, S, LANES), jnp.int32), jax.ShapeDtypeStruct((B, S, LANES), F32),
                   jax.ShapeDtypeStruct((1, LANES), jnp.int32)],
        scratch_shapes=[pltpu.VMEM((1, LANES), F32)],
        compiler_params=_cparams(("arbitrary", "arbitrary")),
        name="mix_mem_router",
    )(x3, og3, om3, mk3, mv3, *consts)


def _expert_kernel(be_ref, nu_ref, rows_ref, wgu_ref, bgu_ref, wd_ref, bd_ref, y_ref, wgub_ref, wdb_ref):
    i = pl.program_id(0)
    used = i < nu_ref[0]
    new_expert = (i == 0) | (be_ref[i] != be_ref[jnp.maximum(i - 1, 0)])

    @pl.when(used & new_expert)
    def _():
        def cast(r, carry):
            rs = pl.ds(pl.multiple_of(r * LANES, LANES), LANES)
            wgub_ref[rs, :] = wgu_ref[rs, :].astype(BF16)
            wdb_ref[rs, :] = wd_ref[rs, :].astype(BF16)
            return carry
        lax.fori_loop(0, D_MODEL // LANES, cast, 0)

    @pl.when(used)
    def _():
        packed = jnp.concatenate([rows_ref[c] for c in range(PK_CHUNKS)], axis=1)
        x = _unpack_bf16_pairs(packed).astype(BF16)
        acc = None
        for c in range(D_FF // FF_CHUNK):
            gs_ = slice(c * FF_CHUNK, (c + 1) * FF_CHUNK)
            us_ = slice(D_FF + c * FF_CHUNK, D_FF + (c + 1) * FF_CHUNK)
            gt = jnp.minimum(_dot(x, wgub_ref[:, gs_]) + bgu_ref[:, gs_], SWIGLU_LIMIT)
            up = jnp.clip(_dot(x, wgub_ref[:, us_]) + bgu_ref[:, us_], -SWIGLU_LIMIT, SWIGLU_LIMIT)
            act = gt * _sigmoid(SWIGLU_ALPHA * gt) * (up + 1.0)
            part = _dot(act.astype(BF16), wdb_ref[gs_, :])
            acc = part if acc is None else acc + part
        ypk = _pack_bf16_pairs((acc + bd_ref[...]).astype(BF16).astype(F32))
        for c in range(PK_CHUNKS):
            y_ref[c] = ypk[:, c * LANES:(c + 1) * LANES]

    @pl.when(jnp.logical_not(used))
    def _():
        y_ref[...] = jnp.zeros(y_ref.shape, y_ref.dtype)


def _experts(block_e, n_used, rows, w_gu, b_gu, w_down, b_down):
    n_rows = rows.shape[1]
    nb = n_rows // MOE_ROWS
    gs = pltpu.PrefetchScalarGridSpec(
        num_scalar_prefetch=2,
        grid=(nb,),
        in_specs=[pl.BlockSpec((PK_CHUNKS, MOE_ROWS, LANES), lambda i, be, nu: (0, i, 0)),
                  pl.BlockSpec((None, D_MODEL, 2 * D_FF), lambda i, be, nu: (be[i], 0, 0)),
                  pl.BlockSpec((None, 1, 2 * D_FF), lambda i, be, nu: (be[i], 0, 0)),
                  pl.BlockSpec((None, D_FF, D_MODEL), lambda i, be, nu: (be[i], 0, 0)),
                  pl.BlockSpec((None, 1, D_MODEL), lambda i, be, nu: (be[i], 0, 0))],
        out_specs=pl.BlockSpec((PK_CHUNKS, MOE_ROWS, LANES), lambda i, be, nu: (0, i, 0)),
        scratch_shapes=[pltpu.VMEM((D_MODEL, 2 * D_FF), BF16), pltpu.VMEM((D_FF, D_MODEL), BF16)],
    )
    return pl.pallas_call(
        _expert_kernel,
        grid_spec=gs,
        out_shape=jax.ShapeDtypeStruct((PK_CHUNKS, n_rows, LANES), jnp.uint32),
        compiler_params=_cparams(("arbitrary",)),
        name="moe_experts",
    )(block_e, n_used, rows, w_gu, b_gu, w_down, b_down)


def _sc_mesh():
    return plsc.VectorSubcoreMesh(core_axis_name="core", subcore_axis_name="subcore")


def _sc_scatter_rows(x3s, pos_ts, n_rows):
    C, _, L = x3s[0].shape
    K = pos_ts[0].shape[0]
    ns = len(x3s)

    @functools.partial(pl.kernel, out_type=jax.ShapeDtypeStruct((C, n_rows, L), x3s[0].dtype), mesh=_sc_mesh(),
                       scratch_types=[])
    def scatter(*refs):
        o_hbm = refs[2 * ns]
        for s in range(ns):
            x_hbm, i_hbm = refs[s], refs[ns + s]
            nwin = x3s[s].shape[1] // SC_WINDOW
            for c in range(C):
                def body(x_vmem, i_vmem, c=c):
                    for k in range(K):
                        pltpu.sync_copy(x_vmem, o_hbm.at[c].at[i_vmem.at[k]])

                pltpu.emit_pipeline(
                    body, grid=(nwin,),
                    in_specs=[pl.BlockSpec((SC_WINDOW, L), lambda i, c=c, nwin=nwin: (c * nwin + i, 0)),
                              pl.BlockSpec((K, SC_WINDOW), lambda i: (0, i))],
                    out_specs=[], core_axis_name=("core", "subcore"), dimension_semantics=(pltpu.PARALLEL,),
                )(x_hbm, i_hbm)

    return scatter(*[x.reshape(-1, L) for x in x3s], *pos_ts)


def _sc_gather_rows(table3, idxs):
    C, _, L = table3.shape
    ns = len(idxs)
    out_type = [jax.ShapeDtypeStruct((C * i.shape[0], L), table3.dtype) for i in idxs]

    @functools.partial(pl.kernel, out_type=out_type, mesh=_sc_mesh(), scratch_types=[])
    def gather(t_hbm, *refs):
        for s in range(ns):
            i_hbm, o_hbm = refs[s], refs[ns + s]
            nwin = idxs[s].shape[0] // SC_WINDOW
            for c in range(C):
                def body(i_vmem, o_vmem, c=c):
                    pltpu.sync_copy(t_hbm.at[c].at[i_vmem.at[0]], o_vmem)

                pltpu.emit_pipeline(
                    body, grid=(nwin,),
                    in_specs=[pl.BlockSpec((1, SC_WINDOW), lambda i: (0, i))],
                    out_specs=[pl.BlockSpec((SC_WINDOW, L), lambda i, c=c, nwin=nwin: (c * nwin + i, 0))],
                    core_axis_name=("core", "subcore"), dimension_semantics=(pltpu.PARALLEL,),
                )(i_hbm, o_hbm)

    outs = gather(table3, *[i.reshape(1, -1) for i in idxs])
    return [o.reshape(C, -1, L) for o in outs]


def _combine_kernel(x2_ref, g_ref, gate_ref, *rest):
    o_ref = rest[-1]
    gate = gate_ref[...]
    half = D_MODEL // 2
    for c in range(PK_CHUNKS):
        lo_s = slice(c * LANES, (c + 1) * LANES)
        hi_s = slice(half + c * LANES, half + (c + 1) * LANES)
        acc_lo = x2_ref[:, lo_s]
        acc_hi = x2_ref[:, hi_s]
        for k in range(TOP_K):
            w = g_ref[c, k]
            gk = gate[:, k:k + 1]
            acc_lo = acc_lo + pltpu.bitcast(w << 16, F32) * gk
            acc_hi = acc_hi + pltpu.bitcast(w & jnp.uint32(0xFFFF0000), F32) * gk
        o_ref[:, lo_s] = acc_lo
        o_ref[:, hi_s] = acc_hi


def _combine(x2, g4, gate, tm, out_buf=None, row0=0, t_total=None):
    T = x2.shape[0]
    t_total = T if t_total is None else t_total
    blk0 = row0 // tm
    in_specs = [pl.BlockSpec((tm, D_MODEL), lambda i: (i, 0)),
                pl.BlockSpec((PK_CHUNKS, TOP_K, tm, LANES), lambda i: (0, 0, i, 0)),
                pl.BlockSpec((tm, LANES), lambda i: (i, 0))]
    args = [x2, g4, gate]
    aliases = {}
    if out_buf is not None:
        in_specs.append(pl.BlockSpec(memory_space=pl.ANY))
        args.append(out_buf)
        aliases = {3: 0}
    return pl.pallas_call(
        _combine_kernel,
        grid=(T // tm,),
        in_specs=in_specs,
        out_specs=pl.BlockSpec((tm, D_MODEL), lambda i: (i + blk0, 0)),
        out_shape=jax.ShapeDtypeStruct((t_total, D_MODEL), F32),
        input_output_aliases=aliases,
        compiler_params=_cparams(("parallel",)),
        name="moe_combine",
    )(*args)


def _moe(streams, places, ew):
    cnts = [st[3][0, :N_EXPERTS] for st in streams]
    total = sum(cnts)
    padded = (total + MOE_ROWS - 1) // MOE_ROWS * MOE_ROWS
    pad_end = jnp.cumsum(padded)
    pad_start = pad_end - padded
    experts = jnp.arange(N_EXPERTS, dtype=jnp.int32)[None, None, :]
    pos_ts = []
    base = pad_start
    for (h3p, idxr, gate, counts, x2), cnt in zip(streams, cnts):
        onehot = idxr[:, :TOP_K, None] == experts
        start = jnp.sum(jnp.where(onehot, base[None, None, :], 0), axis=-1)
        pos_ts.append((start + idxr[:, TOP_K:2 * TOP_K]).T)
        base = base + cnt
    n_assign = sum(st[0].shape[1] for st in streams) * TOP_K
    nb = -(-n_assign // MOE_ROWS) + N_EXPERTS
    starts = jnp.arange(nb, dtype=jnp.int32) * MOE_ROWS
    block_e = jnp.minimum(jnp.sum((pad_end[None, :] <= starts[:, None]).astype(jnp.int32), axis=1), N_EXPERTS - 1)
    n_used = (pad_end[-1] // MOE_ROWS).astype(jnp.int32).reshape(1)
    rows = _sc_scatter_rows([st[0] for st in streams], pos_ts, nb * MOE_ROWS)
    y_rows = _experts(block_e, n_used, rows, ew['w_gu'], ew['b_gu'], ew['w_down'], ew['b_down'])
    gs = _sc_gather_rows(y_rows, [p.reshape(-1) for p in pos_ts])
    outs = []
    for (h3p, idxr, gate, counts, x2), g, place in zip(streams, gs, places):
        T = x2.shape[0]
        outs.append(_combine(x2, g.reshape(PK_CHUNKS, TOP_K, T, LANES), gate, _pick(T, (512, 256, 128, 64)), *place))
    return outs


def _pad_lanes(v, n=LANES, fill=0.0):
    return jnp.pad(v, (0, n - v.shape[0]), constant_values=fill).reshape(1, n)


def _prep_weights(norm_mix, w_in, conv_w, a_log, dt_bias, gdn_norm, q_a_norm, w_qb, kv_a_norm, w_kvb, q_norm,
                  k_nope_norm, k_rope_norm, w_out, norm_mem, mem_norm, w_mq, w_mk, w_mv, mq_norm, mk_norm, w_mo,
                  norm_ffn, w_router, b_router, w_gu, b_gu, w_down, b_down):
    c = np.cumsum([CONV_DIM, GDN_VW, GDN_HEADS, GDN_HEADS, Q_RANK, KV_RANK])
    w_u, w_z, w_a, w_b, w_cq, w_ckv, w_kpe = [w_in[:, lo:hi] for lo, hi in
                                              zip([0, *c], [*c, w_in.shape[1]])]
    w_s = jnp.concatenate([w_kpe, w_a, w_b], axis=1)
    w_s = jnp.pad(w_s, ((0, 0), (0, LANES - w_s.shape[1])))
    wq = w_qb.reshape(Q_RANK, MLA_HEADS, QK_DIM)
    wq = jnp.pad(wq, ((0, 0), (0, 0), (0, Q_SLAB - QK_DIM))).reshape(Q_RANK, MLA_HEADS * Q_SLAB)
    wkv = w_kvb.reshape(KV_RANK, MLA_HEADS, NOPE_DIM + V_DIM)
    wkv = jnp.concatenate([wkv[:, :, :NOPE_DIM].reshape(KV_RANK, -1), wkv[:, :, NOPE_DIM:].reshape(KV_RANK, -1)], 1)
    wr = jnp.pad(w_router, ((0, 0), (0, LANES - N_EXPERTS)))
    wr_hi = wr.astype(BF16)
    wr_lo = (wr - wr_hi.astype(F32)).astype(BF16)
    row = lambda v: v.reshape(1, -1)
    gpad = ROPE_DIM
    pw = dict(
        nmix=row(norm_mix), w_u=w_u.astype(BF16), w_z=w_z.astype(BF16), w_cq=w_cq.astype(BF16),
        w_ckv=w_ckv.astype(BF16), w_s=w_s.astype(BF16), qan=row(q_a_norm), w_qb=wq.astype(BF16),
        qn=_pad_lanes(q_norm, Q_SLAB), kvan=row(kv_a_norm), krn=_pad_lanes(k_rope_norm),
        alog=jnp.pad(a_log, (gpad, LANES - gpad - GDN_HEADS)).reshape(1, LANES),
        dtb=jnp.pad(dt_bias, (gpad, LANES - gpad - GDN_HEADS)).reshape(1, LANES),
        conv_w=conv_w, gnorm=row(gdn_norm), w_kvb=wkv.astype(BF16), knn=row(k_nope_norm),
        w_out=w_out.astype(BF16), nmem=row(norm_mem), w_mq=w_mq.astype(BF16), mqn=row(mq_norm),
        w_mo=w_mo.astype(BF16), nffn=row(norm_ffn), wr_hi=wr_hi, wr_lo=wr_lo,
        b_r=_pad_lanes(b_router, LANES, NEG_BIG),
        mnorm=row(mem_norm), w_mk=w_mk.astype(BF16), w_mv=w_mv.astype(BF16), mkn=row(mk_norm),
    )
    ew = dict(w_gu=w_gu, b_gu=b_gu.reshape(N_EXPERTS, 1, 2 * D_FF), w_down=w_down,
              b_down=b_down.reshape(N_EXPERTS, 1, D_MODEL))
    return pw, ew


def _rope_tables(P, S):
    half = ROPE_DIM // 2
    inv = ROPE_THETA ** (-jnp.arange(half, dtype=F32) / half)
    ang = (P + jnp.arange(S, dtype=jnp.int32)).astype(F32)[:, None] * inv[None, :]
    cos, sin = jnp.cos(ang), jnp.sin(ang)
    zh = jnp.zeros((S, half), F32)
    zz = jnp.zeros((S, LANES - ROPE_DIM), F32)
    return (jnp.concatenate([cos, cos, zz], 1), jnp.concatenate([-sin, zh, zz], 1),
            jnp.concatenate([zh, sin, zz], 1))


def _pick(n, prefs):
    for t in prefs:
        if n % t == 0:
            return t
    return n


def _trunk_front(x, lat_past, kpe_past, s0, conv_past, mem_k, mem_v, pw, n_groups):
    B, S, D = x.shape
    P = lat_past.shape[1]
    T = B * S
    tm = _pick(S, (512, 256, 128, 64))
    tm_in = 512 if (T % 512 == 0 and (512 % S == 0 or S % 512 == 0)) else tm
    u, z, q, lat_new, small = _inproj(x.reshape(T, D), S, tm_in, pw, _rope_tables(P, S))

    LC = _pick(S, (256, 128, 64))
    NB = _pick(B, tuple(n for n in (8, 4, 2, 1) if n * (LC // CHUNK) <= GDN_UNITS))
    o_gdn, s_new, conv_new = _gdn(u.reshape(B, S, CONV_DIM), small.reshape(B, S, LANES), z.reshape(B, S, GDN_VW),
                                  pw['conv_w'], conv_past, s0, pw['gnorm'], NB, LC)

    sk = P + S
    tk = 512 if S >= 512 else -(-sk // LANES) * LANES
    skp = -(-sk // tk) * tk
    lat_all = jnp.concatenate([lat_past, lat_new.reshape(B, S, KV_RANK)], axis=1)
    kpe_new = small[:, :ROPE_DIM].reshape(B, S, ROPE_DIM)
    kpe_all = jnp.concatenate([kpe_past, kpe_new], axis=1)
    lat_all = jnp.pad(lat_all, ((0, 0), (0, skp - sk), (0, 0)))
    kpe_all = jnp.pad(kpe_all, ((0, 0), (0, skp - sk), (0, LANES - ROPE_DIM)))
    tq = _pick(S, (512, 256, 128, 64))
    key_major = tq >= LANES
    kn, kr, v = _kvproj(lat_all, kpe_all, pw['w_kvb'], pw['knn'], _pick(skp, (512,)), key_major)
    q3 = q.reshape(B, S, MLA_HEADS * Q_SLAB)
    o_mla = (_flash_t if key_major else _flash)(q3, kn, kr, v, P, S, tq, tk)

    streams = []
    gb = B // n_groups
    for gi in range(n_groups):
        x2, h3p, idxr, gate, counts = _mix(x, o_gdn, o_mla, mem_k, mem_v, pw, tm, gi * gb, gb)
        tg = gb * S
        streams.append((h3p, idxr.reshape(tg, LANES), gate.reshape(tg, LANES), counts, x2.reshape(tg, D)))
    return streams, (lat_new.reshape(B, S, KV_RANK), kpe_new, s_new, conv_new)


def kernel(x_prompt, x_sample, cache_kv_latent, cache_k_rope, state_gdn, state_conv, cache_mem_k, cache_mem_v, mem_prompt, norm_mix, w_in, conv_w, a_log, dt_bias, gdn_norm, q_a_norm, w_qb, kv_a_norm, w_kvb, q_norm, k_nope_norm, k_rope_norm, w_out, norm_mem, mem_norm, w_mq, w_mk, w_mv, mq_norm, mk_norm, w_mo, norm_ffn, w_router, b_router, w_gu, b_gu, w_down, b_down):
    depth = norm_mix.shape[0]
    yp, ys = x_prompt, x_sample
    bp = x_prompt.shape[0]
    hw = MEM_HEADS * MEM_HEAD_DIM
    outs = [[] for _ in range(10)]
    for l in range(depth):
        pw, ew = _prep_weights(norm_mix[l], w_in[l], conv_w[l], a_log[l], dt_bias[l], gdn_norm[l], q_a_norm[l],
                               w_qb[l], kv_a_norm[l], w_kvb[l], q_norm[l], k_nope_norm[l], k_rope_norm[l], w_out[l],
                               norm_mem[l], mem_norm[l], w_mq[l], w_mk[l], w_mv[l], mq_norm[l], mk_norm[l], w_mo[l],
                               norm_ffn[l], w_router[l], b_router[l], w_gu[l], b_gu[l], w_down[l], b_down[l])
        nm = mem_prompt.shape[1]
        mk, mv = _memkv(mem_prompt.reshape(bp * nm, D_MODEL), pw['mnorm'], pw['w_mk'], pw['w_mv'], pw['mkn'],
                        _pick(bp * nm, (512, 256)))
        mk = mk.reshape(bp, nm, hw)
        mv = mv.reshape(bp, nm, hw)
        n_groups = 2 if bp % 2 == 0 else 1
        streams_p, (lat, kpe, s_fin, cv) = _trunk_front(
            yp, jnp.zeros((bp, 0, KV_RANK), F32), jnp.zeros((bp, 0, ROPE_DIM), F32),
            jnp.zeros((bp, GDN_HEADS, GDN_DK, GDN_DV), F32), jnp.zeros((bp, CONV_W - 1, CONV_DIM), F32), mk, mv, pw,
            n_groups)
        bs = x_sample.shape[0]
        (stream_s,), (lat2, kpe2, s_fin2, cv2) = _trunk_front(
            ys, cache_kv_latent[l], cache_k_rope[l], state_gdn[l], state_conv[l],
            cache_mem_k[l].reshape(bs, nm, hw), cache_mem_v[l].reshape(bs, nm, hw), pw, 1)
        tp = yp.shape[0] * yp.shape[1]
        tg = tp // n_groups
        ybuf = None
        for gi, st in enumerate(streams_p):
            last = gi == n_groups - 1
            res = _moe([st] + ([stream_s] if last else []),
                       [(ybuf, gi * tg, tp)] + ([(None, 0, None)] if last else []), ew)
            ybuf = res[0]
        yp, ys = ybuf.reshape(yp.shape), res[1].reshape(ys.shape)
        for lst, val in zip(outs, (lat, kpe, s_fin, cv, mk.reshape(bp, nm, MEM_HEADS, MEM_HEAD_DIM),
                                   mv.reshape(bp, nm, MEM_HEADS, MEM_HEAD_DIM), lat2, kpe2, s_fin2, cv2)):
            lst.append(val)
    return (yp, ys) + tuple(jnp.stack(o) for o in outs)
```

```python
import functools
import math

import numpy as np
import jax
import jax.numpy as jnp
from jax import lax
from jax.experimental import pallas as pl
from jax.experimental.pallas import tpu as pltpu
from jax.experimental.pallas import tpu_sc as plsc

F32 = jnp.float32
BF16 = jnp.bfloat16

D_MODEL = 1024
CHUNK = 64
EPS = 1e-6
GDN_HEADS = 4
GDN_DK = 128
GDN_DV = 128
CONV_W = 4
GDN_QK = GDN_HEADS * GDN_DK
GDN_VW = GDN_HEADS * GDN_DV
CONV_DIM = 2 * GDN_QK + GDN_VW
MLA_HEADS = 4
Q_RANK = 384
KV_RANK = 256
NOPE_DIM = 128
ROPE_DIM = 64
V_DIM = 128
QK_DIM = NOPE_DIM + ROPE_DIM
ROPE_THETA = 10000.0
N_MEM = 256
MEM_HEADS = 4
MEM_HEAD_DIM = 128
N_EXPERTS = 32
TOP_K = 4
D_FF = D_MODEL
SWIGLU_ALPHA = 1.702
SWIGLU_LIMIT = 7.0

LANES = 128
Q_SLAB = 2 * LANES
NEG_BIG = -1e30
VMEM_LIMIT = 56 * 1024 * 1024
MOE_ROWS = 512
GDN_UNITS = 8
FF_CHUNK = 256
SC_WINDOW = 128
PK_CHUNKS = D_MODEL // 2 // LANES
SUBTILE_ROWS = 256
ONES_ROWS = 16
MIX_ROWS = 512


def _cparams(sem):
    return pltpu.CompilerParams(dimension_semantics=sem, vmem_limit_bytes=VMEM_LIMIT)


def _dot(a, b):
    return jnp.dot(a, b, preferred_element_type=F32)


def _dot_nt(a, b):
    return lax.dot_general(a, b, (((1,), (1,)), ((), ())), preferred_element_type=F32)


def _dot_tn(a, b):
    return lax.dot_general(a, b, (((0,), (0,)), ((), ())), preferred_element_type=F32)


def _rms(x, gain, n=None):
    n = x.shape[-1] if n is None else n
    ss = jnp.sum(x * x, axis=-1, keepdims=True) * (1.0 / n)
    return (x * lax.rsqrt(ss + EPS)) * gain


def _sigmoid(x):
    return 1.0 / (1.0 + jnp.exp(-x))


def _rope128(r, cos, sna, snb):
    return r * cos + pltpu.roll(r, 96, 1) * sna + pltpu.roll(r, 32, 1) * snb


def _pack_bf16_pairs(x):
    n = x.shape[1] // 2
    lo = pltpu.bitcast(x[:, :n], jnp.uint32) >> 16
    hi = pltpu.bitcast(x[:, n:], jnp.uint32) & jnp.uint32(0xFFFF0000)
    return lo | hi


def _unpack_bf16_pairs(p):
    lo = pltpu.bitcast(p << 16, F32)
    hi = pltpu.bitcast(p & jnp.uint32(0xFFFF0000), F32)
    return jnp.concatenate([lo, hi], axis=1)


def _subtiles(rows):
    n = rows // SUBTILE_ROWS if rows % SUBTILE_ROWS == 0 else 1
    step = rows // n
    return [slice(i * step, (i + 1) * step) for i in range(n)]


def _const_spec(shape):
    nd = len(shape)
    return pl.BlockSpec(shape, lambda *_: (0,) * nd)


def _inproj_kernel(x_ref, nmix_ref, wu_ref, wz_ref, wcq_ref, wckv_ref, ws_ref, qan_ref, wqb_ref, qn_ref,
                   kvan_ref, krn_ref, alog_ref, dtb_ref, cos_ref, sna_ref, snb_ref,
                   u_ref, z_ref, q_ref, lat_ref, small_ref):
    for rs in _subtiles(x_ref.shape[0]):
        x = x_ref[rs, :]
        hb = _rms(x, nmix_ref[...]).astype(BF16)
        u_ref[rs, :] = _dot(hb, wu_ref[...])
        z_ref[rs, :] = _dot(hb, wz_ref[...])
        cos, sna, snb = cos_ref[rs, :], sna_ref[rs, :], snb_ref[rs, :]

        cq = _rms(_dot(hb, wcq_ref[...]), qan_ref[...]).astype(BF16)
        qf = _dot(cq, wqb_ref[...])
        scale = QK_DIM ** -0.5
        for h in range(MLA_HEADS):
            slab = qf[:, h * Q_SLAB:(h + 1) * Q_SLAB]
            slab = _rms(slab, qn_ref[...], n=QK_DIM)
            nope = slab[:, :LANES]
            ropd = _rope128(slab[:, LANES:], cos, sna, snb)
            q_ref[rs, h * Q_SLAB:h * Q_SLAB + LANES] = (nope * scale).astype(BF16)
            q_ref[rs, h * Q_SLAB + LANES:(h + 1) * Q_SLAB] = (ropd * scale).astype(BF16)

        lat_ref[rs, :] = _rms(_dot(hb, wckv_ref[...]), kvan_ref[...])

        sm = _dot(hb, ws_ref[...])
        lane = lax.broadcasted_iota(jnp.int32, sm.shape, 1)
        kp = jnp.where(lane < ROPE_DIM, sm, 0.0)
        kpe = _rope128(_rms(kp, krn_ref[...], n=ROPE_DIM), cos, sna, snb)
        sp = sm + dtb_ref[...]
        softplus = jnp.maximum(sp, 0.0) + jnp.log1p(jnp.exp(-jnp.abs(sp)))
        g = -jnp.exp(alog_ref[...]) * softplus
        beta = _sigmoid(sm)
        small_ref[rs, :] = jnp.where(lane < ROPE_DIM, kpe,
                                     jnp.where(lane < ROPE_DIM + GDN_HEADS, g,
                                               jnp.where(lane < ROPE_DIM + 2 * GDN_HEADS, beta, 0.0)))


def _inproj(x2d, S, tm, pw, tabs):
    T = x2d.shape[0]
    if tm > S:
        tabs = [jnp.tile(t, (tm // S, 1)) for t in tabs]
    nblk_s = max(S // tm, 1)
    row = lambda n: pl.BlockSpec((tm, n), lambda i: (i, 0))
    tab = pl.BlockSpec((tm, LANES), lambda i: (i % nblk_s, 0))
    consts = [pw['nmix'], pw['w_u'], pw['w_z'], pw['w_cq'], pw['w_ckv'], pw['w_s'], pw['qan'], pw['w_qb'],
              pw['qn'], pw['kvan'], pw['krn'], pw['alog'], pw['dtb']]
    return pl.pallas_call(
        _inproj_kernel,
        grid=(T // tm,),
        in_specs=[row(D_MODEL)] + [_const_spec(c.shape) for c in consts] + [tab, tab, tab],
        out_specs=[row(CONV_DIM), row(GDN_VW), row(MLA_HEADS * Q_SLAB), row(KV_RANK), row(LANES)],
        out_shape=[jax.ShapeDtypeStruct((T, CONV_DIM), F32), jax.ShapeDtypeStruct((T, GDN_VW), F32),
                   jax.ShapeDtypeStruct((T, MLA_HEADS * Q_SLAB), BF16), jax.ShapeDtypeStruct((T, KV_RANK), F32),
                   jax.ShapeDtypeStruct((T, LANES), F32)],
        compiler_params=_cparams(("parallel",)),
        name="inproj",
    )(x2d, *consts, *tabs)


def _split3(x):
    hi = x.astype(BF16)
    r = x - hi.astype(F32)
    mid = r.astype(BF16)
    lo = (r - mid.astype(F32)).astype(BF16)
    return hi, mid, lo


def _gdn_kernel(u_ref, small_ref, z_ref, convw_ref, cpast_ref, s0_ref, gnorm_ref,
                to_ref, trilm_ref, strictm_ref, same2_ref, lvl_ref,
                o_ref, sfin_ref, cnew_ref, ext_ref, uc_ref, state_ref,
                qf_ref, kf_ref, vb_ref, bt_ref, gcum_ref, glast_ref, kk_ref, qk_ref, mb_ref, x_ref, qkb_ref,
                kbe_ref, qg_ref, kdec_ref, egl_ref, t1_ref, uu_ref, ww_ref, vn_ref, qs_ref, *, NB, LC):
    j = pl.program_id(1)
    nj = pl.num_programs(1)
    PADR = 8
    C = LC // CHUNK
    U = NB * C
    HR = GDN_HEADS * CHUNK

    @pl.when(j == 0)
    def _():
        state_ref[...] = s0_ref[...]
        ext_ref[:, PADR - (CONV_W - 1):PADR, :] = cpast_ref[...]

    w = convw_ref[...]
    for nb in range(NB):
        ext_ref[nb, PADR:PADR + LC, :] = u_ref[nb]
        acc = ext_ref[nb, PADR:PADR + LC, :] * w[CONV_W - 1:CONV_W, :]
        for t in range(1, CONV_W):
            acc = acc + ext_ref[nb, PADR - t:PADR - t + LC, :] * w[CONV_W - 1 - t:CONV_W - t, :]
        uc_ref[nb] = acc * _sigmoid(acc)
        ext_ref[nb, 0:PADR, :] = ext_ref[nb, LC:LC + PADR, :]

    @pl.when(j == nj - 1)
    def _():
        cnew_ref[...] = ext_ref[:, PADR - (CONV_W - 1):PADR, :]

    units = [(nb, c) for nb in range(NB) for c in range(C)]
    g0 = ROPE_DIM
    b0 = ROPE_DIM + GDN_HEADS
    to = to_ref[...]
    for u, (nb, c) in enumerate(units):
        rows = slice(c * CHUNK, (c + 1) * CHUNK)
        sm = small_ref[nb, rows, :]
        gl = sum(_dot(to, part) for part in _split3(sm))
        for h in range(GDN_HEADS):
            hr = slice(h * CHUNK, (h + 1) * CHUNK)
            q = uc_ref[nb, rows, h * GDN_DK:(h + 1) * GDN_DK]
            k = uc_ref[nb, rows, GDN_QK + h * GDN_DK:GDN_QK + (h + 1) * GDN_DK]
            v = uc_ref[nb, rows, 2 * GDN_QK + h * GDN_DV:2 * GDN_QK + (h + 1) * GDN_DV]
            beta = jnp.broadcast_to(sm[:, b0 + h:b0 + h + 1], (CHUNK, LANES))
            qf_ref[u, hr, :] = (q * lax.rsqrt(jnp.sum(q * q, -1, keepdims=True) + EPS)) * (GDN_DK ** -0.5)
            kf_ref[u, hr, :] = k * lax.rsqrt(jnp.sum(k * k, -1, keepdims=True) + EPS)
            vb_ref[u, hr, :] = (v * beta).astype(BF16)
            bt_ref[u, hr, :] = beta
            gcum_ref[u, hr, :] = jnp.broadcast_to(gl[:CHUNK, g0 + h:g0 + h + 1], (CHUNK, LANES))
            glast_ref[u, hr, :] = jnp.broadcast_to(gl[CHUNK:, g0 + h:g0 + h + 1], (CHUNK, LANES))

    for u in range(U):
        k = kf_ref[u]
        kbf = k.astype(BF16)
        kk_ref[u] = _dot_nt((k * bt_ref[u]).astype(BF16), kbf)
        qk_ref[u] = _dot_nt(qf_ref[u].astype(BF16), kbf)

    trilm = trilm_ref[...]
    eye = trilm - strictm_ref[...]
    for u in range(U):
        gcum = gcum_ref[u]
        grow = gcum.T[0:1, :]
        gcol = jnp.concatenate([gcum, gcum], axis=1)
        decay = jnp.exp(jnp.minimum(gcol - grow, 0.0)) * trilm
        m = kk_ref[u] * (decay * strictm_ref[...])
        mb_ref[u] = m.astype(BF16)
        x_ref[u] = eye - m * same2_ref[...]
        qkb_ref[u] = (qk_ref[u] * decay).astype(BF16)
        egc = jnp.exp(gcum)
        k = kf_ref[u]
        kbe_ref[u] = (k * bt_ref[u] * egc).astype(BF16)
        qg_ref[u] = (qf_ref[u] * egc).astype(BF16)
        kdec_ref[u] = (k * jnp.exp(glast_ref[u] - gcum)).astype(BF16)
        egl_ref[u] = jnp.exp(glast_ref[u])

    for lvl in range(lvl_ref.shape[0]):
        lm = lvl_ref[lvl]
        for u in range(U):
            t1_ref[u] = _dot(mb_ref[u] * lm, x_ref[u].astype(BF16)).astype(BF16)
        for u in range(U):
            x = x_ref[u]
            x_ref[u] = x - _dot(x.astype(BF16), t1_ref[u])

    for u in range(U):
        xb = x_ref[u].astype(BF16)
        uu_ref[u] = _dot(xb, vb_ref[u])
        ww_ref[u] = _dot(xb, kbe_ref[u]).astype(BF16)

    gnorm = gnorm_ref[...]
    hrs = [slice(h * CHUNK, (h + 1) * CHUNK) for h in range(GDN_HEADS)]
    for c in range(C):
        rows = slice(c * CHUNK, (c + 1) * CHUNK)
        us = [nb * C + c for nb in range(NB)]
        for nb, u in enumerate(us):
            for h, hr in enumerate(hrs):
                stb = state_ref[nb, h].astype(BF16)
                r = _dot(jnp.concatenate([ww_ref[u, hr, :], qg_ref[u, hr, :]], axis=0), stb)
                vn_ref[u, hr, :] = (uu_ref[u, hr, :] - r[:CHUNK]).astype(BF16)
                qs_ref[u, hr, :] = r[CHUNK:]
        outs = [qs_ref[u] + _dot(qkb_ref[u], vn_ref[u]) for u in us]
        for nb, u in enumerate(us):
            for h, hr in enumerate(hrs):
                state_ref[nb, h] = (state_ref[nb, h] * egl_ref[u, h * CHUNK:h * CHUNK + 1, :]
                                    + _dot_tn(kdec_ref[u, hr, :], vn_ref[u, hr, :]))
        for nb, u in enumerate(us):
            for h, hr in enumerate(hrs):
                zz = z_ref[nb, rows, h * GDN_DV:(h + 1) * GDN_DV]
                og = _rms(outs[nb][hr, :], gnorm) * (zz * _sigmoid(zz))
                o_ref[nb, rows, h * GDN_DV:(h + 1) * GDN_DV] = og.astype(BF16)

    @pl.when(j == nj - 1)
    def _():
        sfin_ref[...] = state_ref[...]


def _gdn_masks():
    hr = GDN_HEADS * CHUNK
    i = np.arange(hr)[:, None]
    j = np.arange(hr)[None, :]
    same_head = (i // CHUNK) == (j // CHUNK)
    tril = same_head & (i >= j)
    strict = same_head & (i > j)
    same2 = strict & ((i // 2) == (j // 2))
    lvls = []
    blk = 2
    while blk < CHUNK:
        lvls.append(strict & ((i // (2 * blk)) == (j // (2 * blk))) & ((i // blk) != (j // blk)))
        blk *= 2
    fr = np.arange(CHUNK)
    to = np.concatenate([fr[:, None] >= fr[None, :], np.ones((CHUNK, CHUNK), bool)], axis=0)
    f = lambda a: jnp.asarray(a.astype(np.float32))
    return (jnp.asarray(to.astype(np.float32), dtype=BF16), f(tril), f(strict), f(same2),
            jnp.asarray(np.stack(lvls).astype(np.float32), dtype=BF16))


def _gdn(u3, small3, z3, conv_w, conv_past, s0, gnorm, NB, LC):
    B, S, _ = u3.shape
    C = LC // CHUNK
    U = NB * C
    HR = GDN_HEADS * CHUNK
    masks = _gdn_masks()
    tile = lambda n: pl.BlockSpec((NB, LC, n), lambda b, j: (b, j, 0))
    stspec = pl.BlockSpec((NB, GDN_HEADS, GDN_DK, GDN_DV), lambda b, j: (b, 0, 0, 0))
    cvspec = pl.BlockSpec((NB, CONV_W - 1, CONV_DIM), lambda b, j: (b, 0, 0))
    vm = lambda shape, dt: pltpu.VMEM(shape, dt)
    return pl.pallas_call(
        functools.partial(_gdn_kernel, NB=NB, LC=LC),
        grid=(B // NB, S // LC),
        in_specs=[tile(CONV_DIM), tile(LANES), tile(GDN_VW), _const_spec(conv_w.shape), cvspec, stspec,
                  _const_spec(gnorm.shape)] + [_const_spec(m.shape) for m in masks],
        out_specs=[tile(GDN_VW), stspec, cvspec],
        out_shape=[jax.ShapeDtypeStruct((B, S, GDN_VW), BF16),
                   jax.ShapeDtypeStruct((B, GDN_HEADS, GDN_DK, GDN_DV), F32),
                   jax.ShapeDtypeStruct((B, CONV_W - 1, CONV_DIM), F32)],
        scratch_shapes=[vm((NB, LC + 8, CONV_DIM), F32), vm((NB, LC, CONV_DIM), F32),
                        vm((NB, GDN_HEADS, GDN_DK, GDN_DV), F32),
                        vm((U, HR, LANES), F32), vm((U, HR, LANES), F32), vm((U, HR, LANES), BF16),
                        vm((U, HR, LANES), F32),
                        vm((U, HR, LANES), F32), vm((U, HR, LANES), F32),
                        vm((U, HR, HR), F32), vm((U, HR, HR), F32),
                        vm((U, HR, HR), BF16), vm((U, HR, HR), F32), vm((U, HR, HR), BF16),
                        vm((U, HR, LANES), BF16), vm((U, HR, LANES), BF16), vm((U, HR, LANES), BF16),
                        vm((U, HR, LANES), F32), vm((U, HR, HR), BF16),
                        vm((U, HR, LANES), F32), vm((U, HR, LANES), BF16),
                        vm((U, HR, LANES), BF16), vm((U, HR, LANES), F32)],
        compiler_params=_cparams(("parallel", "arbitrary")),
        name="gdn",
    )(u3, small3, z3, conv_w, conv_past, s0, gnorm, *masks)


def _kvproj_kernel(lat_ref, kpe_ref, wk_ref, wv_ref, knn_ref, kn_ref, kr_ref, v_ref, *, v_transposed):
    lb = lat_ref[...].astype(BF16)
    kf = _dot(lb, wk_ref[...])
    for h in range(MLA_HEADS):
        sl = slice(h * NOPE_DIM, (h + 1) * NOPE_DIM)
        kn_ref[:, sl] = _rms(kf[:, sl], knn_ref[...]).astype(BF16)
    if v_transposed:
        v_ref[...] = _dot_nt(wv_ref[...], lb).astype(BF16)
    else:
        v_ref[...] = _dot(lb, wv_ref[...]).astype(BF16)
    kp = kpe_ref[...]
    if kp.shape[1] == ROPE_DIM:
        kr_ref[...] = jnp.concatenate([kp, jnp.zeros((kp.shape[0], LANES - ROPE_DIM), F32)], axis=1).astype(BF16)
    else:
        lane = lax.broadcasted_iota(jnp.int32, kp.shape, 1)
        kr_ref[...] = jnp.where(lane < ROPE_DIM, kp, 0.0).astype(BF16)


def _kvproj(lat3, kpe3, pw, tm, v_transposed):
    B, sk, _ = lat3.shape
    hw = MLA_HEADS * V_DIM
    row = lambda n: pl.BlockSpec((None, tm, n), lambda b, i: (b, i, 0))
    if v_transposed:
        vspec, vshape, w_v = pl.BlockSpec((None, hw, tm), lambda b, i: (b, 0, i)), (B, hw, sk), pw['w_vt']
    else:
        vspec, vshape, w_v = row(hw), (B, sk, hw), pw['w_v']
    return pl.pallas_call(
        functools.partial(_kvproj_kernel, v_transposed=v_transposed),
        grid=(B, sk // tm),
        in_specs=[row(KV_RANK), row(kpe3.shape[-1]), _const_spec(pw['w_k'].shape), _const_spec(w_v.shape),
                  _const_spec(pw['knn'].shape)],
        out_specs=[row(MLA_HEADS * NOPE_DIM), row(LANES), vspec],
        out_shape=[jax.ShapeDtypeStruct((B, sk, MLA_HEADS * NOPE_DIM), BF16),
                   jax.ShapeDtypeStruct((B, sk, LANES), BF16), jax.ShapeDtypeStruct(vshape, BF16)],
        compiler_params=_cparams(("parallel", "parallel")),
        name="kvproj",
    )(lat3, kpe3, pw['w_k'], w_v, pw['knn'])


def _last_kblock(qi, tq, tk, P, nk):
    last_key = ((P + qi * tq + tq - 1) // CHUNK) * CHUNK + CHUNK - 1
    return jnp.minimum(last_key // tk, nk - 1)


def _flash_kernel(q_ref, kn_ref, kr_ref, v_ref, o_ref, m_ref, l_ref, acc_ref, *, tq, tk, P, S, nk):
    qi = pl.program_id(1)
    ki = pl.program_id(2)

    @pl.when(ki == 0)
    def _():
        m_ref[...] = jnp.full(m_ref.shape, NEG_BIG, F32)
        l_ref[...] = jnp.zeros(l_ref.shape, F32)
        acc_ref[...] = jnp.zeros(acc_ref.shape, F32)

    @pl.when(ki <= _last_kblock(qi, tq, tk, P, nk))
    def _():
        qpos = P + qi * tq + lax.broadcasted_iota(jnp.int32, (tq, tk), 0)
        kpos = ki * tk + lax.broadcasted_iota(jnp.int32, (tq, tk), 1)
        mask = ((kpos // CHUNK) <= (qpos // CHUNK)) & (kpos < P + S)
        kr = kr_ref[...]
        for h in range(MLA_HEADS):
            qh = q_ref[:, h * Q_SLAB:(h + 1) * Q_SLAB]
            kh = jnp.concatenate([kn_ref[:, h * NOPE_DIM:(h + 1) * NOPE_DIM], kr], axis=1)
            s = jnp.where(mask, _dot_nt(qh, kh), NEG_BIG)
            m_prev = m_ref[h]
            m_new = jnp.maximum(m_prev, jnp.max(s, axis=-1, keepdims=True))
            alpha = jnp.exp(m_prev - m_new)
            p = jnp.exp(s - m_new)
            l_ref[h] = alpha * l_ref[h] + jnp.sum(p, axis=-1, keepdims=True)
            acc_ref[h] = alpha * acc_ref[h] + _dot(p.astype(BF16), v_ref[:, h * V_DIM:(h + 1) * V_DIM])
            m_ref[h] = m_new

    @pl.when(ki == nk - 1)
    def _():
        for h in range(MLA_HEADS):
            o_ref[:, h * V_DIM:(h + 1) * V_DIM] = (acc_ref[h] / l_ref[h]).astype(BF16)


def _flash(q3, kn3, kr3, v3, P, S, tq, tk):
    B = q3.shape[0]
    skp = kn3.shape[1]
    nk = skp // tk
    kmap = lambda b, qi, ki: (b, jnp.minimum(ki, _last_kblock(qi, tq, tk, P, nk)), 0)
    return pl.pallas_call(
        functools.partial(_flash_kernel, tq=tq, tk=tk, P=P, S=S, nk=nk),
        grid=(B, S // tq, nk),
        in_specs=[pl.BlockSpec((None, tq, MLA_HEADS * Q_SLAB), lambda b, qi, ki: (b, qi, 0)),
                  pl.BlockSpec((None, tk, MLA_HEADS * NOPE_DIM), kmap),
                  pl.BlockSpec((None, tk, LANES), kmap),
                  pl.BlockSpec((None, tk, MLA_HEADS * V_DIM), kmap)],
        out_specs=pl.BlockSpec((None, tq, MLA_HEADS * V_DIM), lambda b, qi, ki: (b, qi, 0)),
        out_shape=jax.ShapeDtypeStruct((B, S, MLA_HEADS * V_DIM), BF16),
        scratch_shapes=[pltpu.VMEM((MLA_HEADS, tq, 1), F32), pltpu.VMEM((MLA_HEADS, tq, 1), F32),
                        pltpu.VMEM((MLA_HEADS, tq, V_DIM), F32)],
        compiler_params=_cparams(("parallel", "parallel", "arbitrary")),
        name="mla_attn",
    )(q3, kn3, kr3, v3)


def _attn_dec_kernel(q_ref, knp_ref, krp_ref, vp_ref, knn_ref, krn_ref, vn_ref, o_ref):
    krp, krn = krp_ref[...], krn_ref[...]
    hsl = [slice(h * NOPE_DIM, (h + 1) * NOPE_DIM) for h in range(MLA_HEADS)]
    qs = [q_ref[:, h * Q_SLAB:(h + 1) * Q_SLAB] for h in range(MLA_HEADS)]
    sps = [_dot_nt(q, jnp.concatenate([knp_ref[:, sl], krp], axis=1)) for q, sl in zip(qs, hsl)]
    sns = [_dot_nt(q, jnp.concatenate([knn_ref[:, sl], krn], axis=1)) for q, sl in zip(qs, hsl)]
    for h, (sp, sn) in enumerate(zip(sps, sns)):
        vsl = slice(h * V_DIM, (h + 1) * V_DIM)
        m = jnp.maximum(jnp.max(sp, axis=-1, keepdims=True), jnp.max(sn, axis=-1, keepdims=True))
        pp = jnp.exp(sp - m)
        pn = jnp.exp(sn - m)
        l = jnp.sum(pp, axis=-1, keepdims=True) + jnp.sum(pn, axis=-1, keepdims=True)
        o = _dot(pp.astype(BF16), vp_ref[:, vsl]) + _dot(pn.astype(BF16), vn_ref[:, vsl])
        o_ref[:, vsl] = (o / l).astype(BF16)


def _attn_dec(q3, past, new):
    B, S, _ = q3.shape
    specs = [pl.BlockSpec((None, a.shape[1], a.shape[2]), lambda b: (b, 0, 0)) for a in (q3, *past, *new)]
    return pl.pallas_call(
        _attn_dec_kernel,
        grid=(B,),
        in_specs=specs,
        out_specs=pl.BlockSpec((None, S, MLA_HEADS * V_DIM), lambda b: (b, 0, 0)),
        out_shape=jax.ShapeDtypeStruct((B, S, MLA_HEADS * V_DIM), BF16),
        compiler_params=_cparams(("parallel",)),
        name="mla_attn_dec",
    )(q3, *past, *new)


def _flash_t_kernel(q_ref, kn_ref, kr_ref, vt_ref, o_ref, m_ref, acc_ref, *, tq, tk, P, S, nk):
    qi = pl.program_id(1)
    ki = pl.program_id(2)
    q0 = P + qi * tq
    k0 = ki * tk

    @pl.when(ki == 0)
    def _():
        m_ref[...] = jnp.full(m_ref.shape, NEG_BIG, F32)
        acc_ref[...] = jnp.zeros(acc_ref.shape, F32)

    def step(masked):
        kr = kr_ref[...]
        ones = jnp.ones((ONES_ROWS, tk), BF16)
        if masked:
            kpos = k0 + lax.broadcasted_iota(jnp.int32, (tk, 1), 0)
            qpos = q0 + lax.broadcasted_iota(jnp.int32, (1, tq), 1)
            mask = ((kpos // CHUNK) <= (qpos // CHUNK)) & (kpos < P + S)
        sts = []
        for h in range(MLA_HEADS):
            kh = jnp.concatenate([kn_ref[:, h * NOPE_DIM:(h + 1) * NOPE_DIM], kr], axis=1)
            sts.append(_dot_nt(kh, q_ref[:, h * Q_SLAB:(h + 1) * Q_SLAB]))
        ps, alphas = [], []
        for h in range(MLA_HEADS):
            st = sts[h]
            if masked:
                st = jnp.where(mask, st, NEG_BIG)
            m_prev = m_ref[h]
            m_new = jnp.maximum(m_prev, jnp.max(st, axis=0, keepdims=True))
            alphas.append(jnp.exp(m_prev - m_new))
            ps.append(jnp.exp(st - m_new).astype(BF16))
            m_ref[h] = m_new
        for h in range(MLA_HEADS):
            vt1 = jnp.concatenate([vt_ref[h * V_DIM:(h + 1) * V_DIM, :], ones], axis=0)
            acc_ref[h] = alphas[h] * acc_ref[h] + _dot(vt1, ps[h])

    needed = ki <= _last_kblock(qi, tq, tk, P, nk)
    full = ((k0 + tk - 1) // CHUNK <= q0 // CHUNK) & (k0 + tk <= P + S)
    pl.when(needed & full)(functools.partial(step, False))
    pl.when(needed & jnp.logical_not(full))(functools.partial(step, True))

    @pl.when(ki == nk - 1)
    def _():
        for h in range(MLA_HEADS):
            acc = acc_ref[h]
            o_ref[:, h * V_DIM:(h + 1) * V_DIM] = (acc[:V_DIM] / acc[V_DIM:V_DIM + 1]).T.astype(BF16)


def _flash_t(q3, kn3, kr3, vt3, P, S, tq, tk):
    B = q3.shape[0]
    skp = kn3.shape[1]
    nk = skp // tk
    kblk = lambda qi, ki: jnp.minimum(ki, _last_kblock(qi, tq, tk, P, nk))
    kmap = lambda b, qi, ki: (b, kblk(qi, ki), 0)
    return pl.pallas_call(
        functools.partial(_flash_t_kernel, tq=tq, tk=tk, P=P, S=S, nk=nk),
        grid=(B, S // tq, nk),
        in_specs=[pl.BlockSpec((None, tq, MLA_HEADS * Q_SLAB), lambda b, qi, ki: (b, qi, 0)),
                  pl.BlockSpec((None, tk, MLA_HEADS * NOPE_DIM), kmap),
                  pl.BlockSpec((None, tk, LANES), kmap),
                  pl.BlockSpec((None, MLA_HEADS * V_DIM, tk), lambda b, qi, ki: (b, 0, kblk(qi, ki)))],
        out_specs=pl.BlockSpec((None, tq, MLA_HEADS * V_DIM), lambda b, qi, ki: (b, qi, 0)),
        out_shape=jax.ShapeDtypeStruct((B, S, MLA_HEADS * V_DIM), BF16),
        scratch_shapes=[pltpu.VMEM((MLA_HEADS, 1, tq), F32), pltpu.VMEM((MLA_HEADS, V_DIM + ONES_ROWS, tq), F32)],
        compiler_params=_cparams(("parallel", "parallel", "arbitrary")),
        name="mla_attn_t",
    )(q3, kn3, kr3, vt3)


def _memkv_kernel(mem_ref, mnorm_ref, wmk_ref, wmv_ref, mkn_ref, k_ref, v_ref):
    mb = _rms(mem_ref[...], mnorm_ref[...]).astype(BF16)
    kf = _dot(mb, wmk_ref[...])
    for h in range(MEM_HEADS):
        sl = slice(h * MEM_HEAD_DIM, (h + 1) * MEM_HEAD_DIM)
        k_ref[:, sl] = _rms(kf[:, sl], mkn_ref[...])
    v_ref[...] = _dot(mb, wmv_ref[...])


def _memkv(mem2d, mnorm, w_mk, w_mv, mkn, tm):
    T = mem2d.shape[0]
    hw = MEM_HEADS * MEM_HEAD_DIM
    row = lambda n: pl.BlockSpec((tm, n), lambda i: (i, 0))
    return pl.pallas_call(
        _memkv_kernel,
        grid=(T // tm,),
        in_specs=[row(D_MODEL), _const_spec(mnorm.shape), _const_spec(w_mk.shape), _const_spec(w_mv.shape),
                  _const_spec(mkn.shape)],
        out_specs=[row(hw), row(hw)],
        out_shape=[jax.ShapeDtypeStruct((T, hw), F32), jax.ShapeDtypeStruct((T, hw), F32)],
        compiler_params=_cparams(("parallel",)),
        name="mem_kv",
    )(mem2d, mnorm, w_mk, w_mv, mkn)


def _mix_kernel(x_ref, og_ref, om_ref, mk_ref, mv_ref, wout_ref, nmem_ref, wmq_ref, mqn_ref, wmo_ref, nffn_ref,
                wrh_ref, wrl_ref, br_ref, x2_ref, h3_ref, idx_ref, gate_ref, counts_ref, cnt_ref):
    def mem_head(ref, b, h):
        if len(ref.shape) == 3:
            return ref[b, :, h * MEM_HEAD_DIM:(h + 1) * MEM_HEAD_DIM].astype(BF16)
        return ref[b, :, h, :].astype(BF16)

    first = (pl.program_id(0) == 0) & (pl.program_id(1) == 0)

    @pl.when(first)
    def _():
        cnt_ref[...] = jnp.zeros(cnt_ref.shape, F32)

    nbm, tm, _ = x_ref.shape
    rows = nbm * tm
    flat = lambda ref: ref[...].reshape(rows, ref.shape[-1])
    if True:
        x1 = flat(x_ref) + _dot(flat(og_ref), wout_ref[0:GDN_VW, :]) + _dot(flat(om_ref), wout_ref[GDN_VW:, :])
        hb = _rms(x1, nmem_ref[...]).astype(BF16)
        qm = _dot(hb, wmq_ref[...])
        per_batch = []
        hsl = [slice(h * MEM_HEAD_DIM, (h + 1) * MEM_HEAD_DIM) for h in range(MEM_HEADS)]
        for b in range(nbm):
            br = slice(b * tm, (b + 1) * tm)
            qhs = [(_rms(qm[br, sl], mqn_ref[...]) * (MEM_HEAD_DIM ** -0.5)).astype(BF16) for sl in hsl]
            ss = [_dot_nt(qh, mem_head(mk_ref, b, h)) for h, qh in enumerate(qhs)]
            ps = []
            for s in ss:
                p = jnp.exp(s - jnp.max(s, axis=-1, keepdims=True))
                ps.append((p / jnp.sum(p, axis=-1, keepdims=True)).astype(BF16))
            heads = [_dot(p, mem_head(mv_ref, b, h)).astype(BF16) for h, p in enumerate(ps)]
            per_batch.append(jnp.concatenate(heads, axis=1))
        om = per_batch[0] if nbm == 1 else jnp.concatenate(per_batch, axis=0)
        x2 = x1 + _dot(om, wmo_ref[...])
        x2_ref[...] = x2.reshape(x2_ref.shape)
        h3 = _rms(x2, nffn_ref[...])
        hi = h3.astype(BF16)
        packed = _pack_bf16_pairs(hi.astype(F32))
        for c in range(h3_ref.shape[0]):
            h3_ref[c] = packed[:, c * LANES:(c + 1) * LANES]
        lo = (h3 - hi.astype(F32)).astype(BF16)
        wrh = wrh_ref[...]
        logits = _dot(hi, wrh) + _dot(lo, wrh) + _dot(hi, wrl_ref[...]) + br_ref[...]

        lane = lax.broadcasted_iota(jnp.int32, logits.shape, 1).astype(F32)
        vals, idxs = [], []
        cur = logits
        for _ in range(TOP_K):
            mx = jnp.max(cur, axis=-1, keepdims=True)
            ix = jnp.min(jnp.where(cur == mx, lane, float(LANES)), axis=-1, keepdims=True)
            vals.append(mx)
            idxs.append(ix)
            cur = jnp.where(lane == ix, -3e38, cur)
        es = [jnp.exp(v - vals[0]) for v in vals]
        den = es[0] + es[1] + es[2] + es[3]

        sel = jnp.zeros(logits.shape, F32)
        for k in range(TOP_K):
            sel = sel + jnp.where(lane == idxs[k], 1.0, 0.0)
        ri = lax.broadcasted_iota(jnp.int32, (rows, rows), 0)
        ci = lax.broadcasted_iota(jnp.int32, (rows, rows), 1)
        before = jnp.where(ri > ci, 1.0, 0.0).astype(BF16)
        excl = _dot(before, sel.astype(BF16)) + cnt_ref[...]
        cnt_ref[...] = cnt_ref[...] + jnp.sum(sel, axis=0, keepdims=True)
        counts_ref[...] = cnt_ref[...].astype(jnp.int32)

        idx_out = jnp.zeros(logits.shape, F32)
        gate_out = jnp.zeros(logits.shape, F32)
        for k in range(TOP_K):
            rank = jnp.sum(jnp.where(lane == idxs[k], excl, 0.0), axis=-1, keepdims=True)
            idx_out = jnp.where(lane == float(k), idxs[k], idx_out)
            idx_out = jnp.where(lane == float(TOP_K + k), rank, idx_out)
            gate_out = jnp.where(lane == float(k), es[k] / den, gate_out)
        idx_ref[...] = idx_out.astype(jnp.int32).reshape(idx_ref.shape)
        gate_ref[...] = gate_out.reshape(gate_ref.shape)


def _mix(x3, og3, om3, mk3, mv3, pw, tm, b0, B):
    S = x3.shape[1]
    hw = MEM_HEADS * MEM_HEAD_DIM
    nbm = _pick(B, tuple(n for n in (8, 4, 2, 1) if n * tm <= MIX_ROWS and b0 % n == 0)) if tm == S else 1
    nsb = S // tm
    boff = b0 // nbm
    tile_in = lambda n: pl.BlockSpec((nbm, tm, n), lambda b, i: (b + boff, i, 0))
    tile = lambda n: pl.BlockSpec((nbm, tm, n), lambda b, i: (b, i, 0))
    if mk3.ndim == 3:
        memspec = pl.BlockSpec((nbm, N_MEM, hw), lambda b, i: (b + boff, 0, 0))
    else:
        memspec = pl.BlockSpec((nbm, N_MEM, MEM_HEADS, MEM_HEAD_DIM), lambda b, i: (b + boff, 0, 0, 0))
    consts = [pw['w_out'], pw['nmem'], pw['w_mq'], pw['mqn'], pw['w_mo'], pw['nffn'], pw['wr_hi'], pw['wr_lo'],
              pw['b_r']]
    return pl.pallas_call(
        _mix_kernel,
        grid=(B // nbm, nsb),
        in_specs=[tile_in(D_MODEL), tile_in(GDN_VW), tile_in(MLA_HEADS * V_DIM), memspec, memspec]
                 + [_const_spec(c.shape) for c in consts],
        out_specs=[tile(D_MODEL), pl.BlockSpec((PK_CHUNKS, nbm * tm, LANES), lambda b, i: (0, b * nsb + i, 0)),
                   tile(LANES), tile(LANES), _const_spec((1, LANES))],
        out_shape=[jax.ShapeDtypeStruct((B, S, D_MODEL), F32),
                   jax.ShapeDtypeStruct((PK_CHUNKS, B * S, LANES), jnp.uint32),
                   jax.ShapeDtypeStruct((B, S, LANES), jnp.int32), jax.ShapeDtypeStruct((B, S, LANES), F32),
                   jax.ShapeDtypeStruct((1, LANES), jnp.int32)],
        scratch_shapes=[pltpu.VMEM((1, LANES), F32)],
        compiler_params=_cparams(("arbitrary", "arbitrary")),
        name="mix_mem_router",
    )(x3, og3, om3, mk3, mv3, *consts)


def _expert_kernel(be_ref, nu_ref, rows_ref, wgu_ref, bgu_ref, wd_ref, bd_ref, y_ref, wgub_ref, wdb_ref):
    i = pl.program_id(0)
    used = i < nu_ref[0]
    new_expert = (i == 0) | (be_ref[i] != be_ref[jnp.maximum(i - 1, 0)])

    @pl.when(used & new_expert)
    def _():
        def cast(r, carry):
            rs = pl.ds(pl.multiple_of(r * LANES, LANES), LANES)
            wgub_ref[rs, :] = wgu_ref[rs, :].astype(BF16)
            wdb_ref[rs, :] = wd_ref[rs, :].astype(BF16)
            return carry
        lax.fori_loop(0, D_MODEL // LANES, cast, 0)

    @pl.when(used)
    def _():
        packed = jnp.concatenate([rows_ref[c] for c in range(PK_CHUNKS)], axis=1)
        x = _unpack_bf16_pairs(packed).astype(BF16)
        acc = None
        for c in range(D_FF // FF_CHUNK):
            gs_ = slice(c * FF_CHUNK, (c + 1) * FF_CHUNK)
            us_ = slice(D_FF + c * FF_CHUNK, D_FF + (c + 1) * FF_CHUNK)
            gt = jnp.minimum(_dot(x, wgub_ref[:, gs_]) + bgu_ref[:, gs_], SWIGLU_LIMIT)
            up = jnp.clip(_dot(x, wgub_ref[:, us_]) + bgu_ref[:, us_], -SWIGLU_LIMIT, SWIGLU_LIMIT)
            act = gt * _sigmoid(SWIGLU_ALPHA * gt) * (up + 1.0)
            part = _dot(act.astype(BF16), wdb_ref[gs_, :])
            acc = part if acc is None else acc + part
        ypk = _pack_bf16_pairs((acc + bd_ref[...]).astype(BF16).astype(F32))
        for c in range(PK_CHUNKS):
            y_ref[c] = ypk[:, c * LANES:(c + 1) * LANES]

    @pl.when(jnp.logical_not(used))
    def _():
        y_ref[...] = jnp.zeros(y_ref.shape, y_ref.dtype)


def _experts(block_e, n_used, rows, w_gu, b_gu, w_down, b_down):
    n_rows = rows.shape[1]
    nb = n_rows // MOE_ROWS
    gs = pltpu.PrefetchScalarGridSpec(
        num_scalar_prefetch=2,
        grid=(nb,),
        in_specs=[pl.BlockSpec((PK_CHUNKS, MOE_ROWS, LANES), lambda i, be, nu: (0, i, 0)),
                  pl.BlockSpec((None, D_MODEL, 2 * D_FF), lambda i, be, nu: (be[i], 0, 0)),
                  pl.BlockSpec((None, 1, 2 * D_FF), lambda i, be, nu: (be[i], 0, 0)),
                  pl.BlockSpec((None, D_FF, D_MODEL), lambda i, be, nu: (be[i], 0, 0)),
                  pl.BlockSpec((None, 1, D_MODEL), lambda i, be, nu: (be[i], 0, 0))],
        out_specs=pl.BlockSpec((PK_CHUNKS, MOE_ROWS, LANES), lambda i, be, nu: (0, i, 0)),
        scratch_shapes=[pltpu.VMEM((D_MODEL, 2 * D_FF), BF16), pltpu.VMEM((D_FF, D_MODEL), BF16)],
    )
    return pl.pallas_call(
        _expert_kernel,
        grid_spec=gs,
        out_shape=jax.ShapeDtypeStruct((PK_CHUNKS, n_rows, LANES), jnp.uint32),
        compiler_params=_cparams(("arbitrary",)),
        name="moe_experts",
    )(block_e, n_used, rows, w_gu, b_gu, w_down, b_down)


def _sc_mesh():
    return plsc.VectorSubcoreMesh(core_axis_name="core", subcore_axis_name="subcore")


def _sc_scatter_rows(x3s, pos_ts, n_rows):
    C, _, L = x3s[0].shape
    K = pos_ts[0].shape[0]
    ns = len(x3s)

    @functools.partial(pl.kernel, out_type=jax.ShapeDtypeStruct((C, n_rows, L), x3s[0].dtype), mesh=_sc_mesh(),
                       scratch_types=[])
    def scatter(*refs):
        o_hbm = refs[2 * ns]
        for s in range(ns):
            x_hbm, i_hbm = refs[s], refs[ns + s]
            nwin = x3s[s].shape[1] // SC_WINDOW
            for c in range(C):
                def body(x_vmem, i_vmem, c=c):
                    for k in range(K):
                        pltpu.sync_copy(x_vmem, o_hbm.at[c].at[i_vmem.at[k]])

                pltpu.emit_pipeline(
                    body, grid=(nwin,),
                    in_specs=[pl.BlockSpec((SC_WINDOW, L), lambda i, c=c, nwin=nwin: (c * nwin + i, 0)),
                              pl.BlockSpec((K, SC_WINDOW), lambda i: (0, i))],
                    out_specs=[], core_axis_name=("core", "subcore"), dimension_semantics=(pltpu.PARALLEL,),
                )(x_hbm, i_hbm)

    return scatter(*[x.reshape(-1, L) for x in x3s], *pos_ts)


def _sc_gather_rows(table3, idxs):
    C, _, L = table3.shape
    ns = len(idxs)
    out_type = [jax.ShapeDtypeStruct((C * i.shape[0], L), table3.dtype) for i in idxs]

    @functools.partial(pl.kernel, out_type=out_type, mesh=_sc_mesh(), scratch_types=[])
    def gather(t_hbm, *refs):
        for s in range(ns):
            i_hbm, o_hbm = refs[s], refs[ns + s]
            nwin = idxs[s].shape[0] // SC_WINDOW
            for c in range(C):
                def body(i_vmem, o_vmem, c=c):
                    pltpu.sync_copy(t_hbm.at[c].at[i_vmem.at[0]], o_vmem)

                pltpu.emit_pipeline(
                    body, grid=(nwin,),
                    in_specs=[pl.BlockSpec((1, SC_WINDOW), lambda i: (0, i))],
                    out_specs=[pl.BlockSpec((SC_WINDOW, L), lambda i, c=c, nwin=nwin: (c * nwin + i, 0))],
                    core_axis_name=("core", "subcore"), dimension_semantics=(pltpu.PARALLEL,),
                )(i_hbm, o_hbm)

    outs = gather(table3, *[i.reshape(1, -1) for i in idxs])
    return [o.reshape(C, -1, L) for o in outs]


def _combine_kernel(x2_ref, g_ref, gate_ref, *rest):
    o_ref = rest[-1]
    gate = gate_ref[...]
    half = D_MODEL // 2
    for c in range(PK_CHUNKS):
        lo_s = slice(c * LANES, (c + 1) * LANES)
        hi_s = slice(half + c * LANES, half + (c + 1) * LANES)
        acc_lo = x2_ref[:, lo_s]
        acc_hi = x2_ref[:, hi_s]
        for k in range(TOP_K):
            w = g_ref[c, k]
            gk = gate[:, k:k + 1]
            acc_lo = acc_lo + pltpu.bitcast(w << 16, F32) * gk
            acc_hi = acc_hi + pltpu.bitcast(w & jnp.uint32(0xFFFF0000), F32) * gk
        o_ref[:, lo_s] = acc_lo
        o_ref[:, hi_s] = acc_hi


def _combine(x2, g4, gate, tm, out_buf=None, row0=0, t_total=None):
    T = x2.shape[0]
    t_total = T if t_total is None else t_total
    blk0 = row0 // tm
    in_specs = [pl.BlockSpec((tm, D_MODEL), lambda i: (i, 0)),
                pl.BlockSpec((PK_CHUNKS, TOP_K, tm, LANES), lambda i: (0, 0, i, 0)),
                pl.BlockSpec((tm, LANES), lambda i: (i, 0))]
    args = [x2, g4, gate]
    aliases = {}
    if out_buf is not None:
        in_specs.append(pl.BlockSpec(memory_space=pl.ANY))
        args.append(out_buf)
        aliases = {3: 0}
    return pl.pallas_call(
        _combine_kernel,
        grid=(T // tm,),
        in_specs=in_specs,
        out_specs=pl.BlockSpec((tm, D_MODEL), lambda i: (i + blk0, 0)),
        out_shape=jax.ShapeDtypeStruct((t_total, D_MODEL), F32),
        input_output_aliases=aliases,
        compiler_params=_cparams(("parallel",)),
        name="moe_combine",
    )(*args)


def _moe(streams, places, ew):
    cnts = [st[3][0, :N_EXPERTS] for st in streams]
    total = sum(cnts)
    padded = (total + MOE_ROWS - 1) // MOE_ROWS * MOE_ROWS
    pad_end = jnp.cumsum(padded)
    pad_start = pad_end - padded
    experts = jnp.arange(N_EXPERTS, dtype=jnp.int32)[None, None, :]
    pos_ts = []
    base = pad_start
    for (h3p, idxr, gate, counts, x2), cnt in zip(streams, cnts):
        onehot = idxr[:, :TOP_K, None] == experts
        start = jnp.sum(jnp.where(onehot, base[None, None, :], 0), axis=-1)
        pos_ts.append((start + idxr[:, TOP_K:2 * TOP_K]).T)
        base = base + cnt
    n_assign = sum(st[0].shape[1] for st in streams) * TOP_K
    nb = -(-n_assign // MOE_ROWS) + N_EXPERTS
    starts = jnp.arange(nb, dtype=jnp.int32) * MOE_ROWS
    block_e = jnp.minimum(jnp.sum((pad_end[None, :] <= starts[:, None]).astype(jnp.int32), axis=1), N_EXPERTS - 1)
    n_used = (pad_end[-1] // MOE_ROWS).astype(jnp.int32).reshape(1)
    rows = _sc_scatter_rows([st[0] for st in streams], pos_ts, nb * MOE_ROWS)
    y_rows = _experts(block_e, n_used, rows, ew['w_gu'], ew['b_gu'], ew['w_down'], ew['b_down'])
    gs = _sc_gather_rows(y_rows, [p.reshape(-1) for p in pos_ts])
    outs = []
    for (h3p, idxr, gate, counts, x2), g, place in zip(streams, gs, places):
        T = x2.shape[0]
        outs.append(_combine(x2, g.reshape(PK_CHUNKS, TOP_K, T, LANES), gate, _pick(T, (512, 256, 128, 64)), *place))
    return outs


def _pad_lanes(v, n=LANES, fill=0.0):
    return jnp.pad(v, (0, n - v.shape[0]), constant_values=fill).reshape(1, n)


def _prep_weights(norm_mix, w_in, conv_w, a_log, dt_bias, gdn_norm, q_a_norm, w_qb, kv_a_norm, w_kvb, q_norm,
                  k_nope_norm, k_rope_norm, w_out, norm_mem, mem_norm, w_mq, w_mk, w_mv, mq_norm, mk_norm, w_mo,
                  norm_ffn, w_router, b_router, w_gu, b_gu, w_down, b_down):
    c = np.cumsum([CONV_DIM, GDN_VW, GDN_HEADS, GDN_HEADS, Q_RANK, KV_RANK])
    w_u, w_z, w_a, w_b, w_cq, w_ckv, w_kpe = [w_in[:, lo:hi] for lo, hi in
                                              zip([0, *c], [*c, w_in.shape[1]])]
    w_s = jnp.concatenate([w_kpe, w_a, w_b], axis=1)
    w_s = jnp.pad(w_s, ((0, 0), (0, LANES - w_s.shape[1])))
    wq = w_qb.reshape(Q_RANK, MLA_HEADS, QK_DIM)
    wq = jnp.pad(wq, ((0, 0), (0, 0), (0, Q_SLAB - QK_DIM))).reshape(Q_RANK, MLA_HEADS * Q_SLAB)
    wkv = w_kvb.reshape(KV_RANK, MLA_HEADS, NOPE_DIM + V_DIM)
    w_k = wkv[:, :, :NOPE_DIM].reshape(KV_RANK, -1).astype(BF16)
    w_v = wkv[:, :, NOPE_DIM:].reshape(KV_RANK, -1).astype(BF16)
    wr = jnp.pad(w_router, ((0, 0), (0, LANES - N_EXPERTS)))
    wr_hi = wr.astype(BF16)
    wr_lo = (wr - wr_hi.astype(F32)).astype(BF16)
    row = lambda v: v.reshape(1, -1)
    gpad = ROPE_DIM
    pw = dict(
        nmix=row(norm_mix), w_u=w_u.astype(BF16), w_z=w_z.astype(BF16), w_cq=w_cq.astype(BF16),
        w_ckv=w_ckv.astype(BF16), w_s=w_s.astype(BF16), qan=row(q_a_norm), w_qb=wq.astype(BF16),
        qn=_pad_lanes(q_norm, Q_SLAB), kvan=row(kv_a_norm), krn=_pad_lanes(k_rope_norm),
        alog=jnp.pad(a_log, (gpad, LANES - gpad - GDN_HEADS)).reshape(1, LANES),
        dtb=jnp.pad(dt_bias, (gpad, LANES - gpad - GDN_HEADS)).reshape(1, LANES),
        conv_w=conv_w, gnorm=row(gdn_norm), w_k=w_k, w_v=w_v, w_vt=w_v.T, knn=row(k_nope_norm),
        w_out=w_out.astype(BF16), nmem=row(norm_mem), w_mq=w_mq.astype(BF16), mqn=row(mq_norm),
        w_mo=w_mo.astype(BF16), nffn=row(norm_ffn), wr_hi=wr_hi, wr_lo=wr_lo,
        b_r=_pad_lanes(b_router, LANES, NEG_BIG),
        mnorm=row(mem_norm), w_mk=w_mk.astype(BF16), w_mv=w_mv.astype(BF16), mkn=row(mk_norm),
    )
    ew = dict(w_gu=w_gu, b_gu=b_gu.reshape(N_EXPERTS, 1, 2 * D_FF), w_down=w_down,
              b_down=b_down.reshape(N_EXPERTS, 1, D_MODEL))
    return pw, ew


def _rope_tables(P, S):
    half = ROPE_DIM // 2
    inv = ROPE_THETA ** (-jnp.arange(half, dtype=F32) / half)
    ang = (P + jnp.arange(S, dtype=jnp.int32)).astype(F32)[:, None] * inv[None, :]
    cos, sin = jnp.cos(ang), jnp.sin(ang)
    zh = jnp.zeros((S, half), F32)
    zz = jnp.zeros((S, LANES - ROPE_DIM), F32)
    return (jnp.concatenate([cos, cos, zz], 1), jnp.concatenate([-sin, zh, zz], 1),
            jnp.concatenate([zh, sin, zz], 1))


def _pick(n, prefs):
    for t in prefs:
        if n % t == 0:
            return t
    return n


def _trunk_front(x, lat_past, kpe_past, s0, conv_past, mem_k, mem_v, pw, n_groups):
    B, S, D = x.shape
    P = lat_past.shape[1]
    T = B * S
    tm = _pick(S, (512, 256, 128, 64))
    tm_in = 512 if (T % 512 == 0 and (512 % S == 0 or S % 512 == 0)) else tm
    u, z, q, lat_new, small = _inproj(x.reshape(T, D), S, tm_in, pw, _rope_tables(P, S))

    LC = _pick(S, (256, 128, 64))
    NB = _pick(B, tuple(n for n in (8, 4, 2, 1) if n * (LC // CHUNK) <= GDN_UNITS))
    o_gdn, s_new, conv_new = _gdn(u.reshape(B, S, CONV_DIM), small.reshape(B, S, LANES), z.reshape(B, S, GDN_VW),
                                  pw['conv_w'], conv_past, s0, pw['gnorm'], NB, LC)

    kpe_new = small[:, :ROPE_DIM].reshape(B, S, ROPE_DIM)
    lat3 = lat_new.reshape(B, S, KV_RANK)
    small3 = small.reshape(B, S, LANES)
    q3 = q.reshape(B, S, MLA_HEADS * Q_SLAB)
    tq = _pick(S, (512, 256, 128, 64))
    if P == 0 and S % 512 == 0:
        kn, kr, vt = _kvproj(lat3, small3, pw, 512, True)
        o_mla = _flash_t(q3, kn, kr, vt, P, S, tq, 512)
    elif P > 0 and P % CHUNK == 0 and S == CHUNK:
        past = _kvproj(lat_past, kpe_past, pw, _pick(P, (512,)), False)
        new = _kvproj(lat3, small3, pw, S, False)
        o_mla = _attn_dec(q3, past, new)
    else:
        sk = P + S
        tk = 512 if S >= 512 else -(-sk // LANES) * LANES
        skp = -(-sk // tk) * tk
        lat_all = jnp.pad(jnp.concatenate([lat_past, lat3], axis=1), ((0, 0), (0, skp - sk), (0, 0)))
        kpe_all = jnp.pad(jnp.concatenate([kpe_past, kpe_new], axis=1), ((0, 0), (0, skp - sk), (0, 0)))
        key_major = tq >= LANES
        kn, kr, v = _kvproj(lat_all, kpe_all, pw, _pick(skp, (512,)), key_major)
        o_mla = (_flash_t if key_major else _flash)(q3, kn, kr, v, P, S, tq, tk)

    streams = []
    gb = B // n_groups
    for gi in range(n_groups):
        x2, h3p, idxr, gate, counts = _mix(x, o_gdn, o_mla, mem_k, mem_v, pw, tm, gi * gb, gb)
        tg = gb * S
        streams.append((h3p, idxr.reshape(tg, LANES), gate.reshape(tg, LANES), counts, x2.reshape(tg, D)))
    return streams, (lat_new.reshape(B, S, KV_RANK), kpe_new, s_new, conv_new)


def kernel(x_prompt, x_sample, cache_kv_latent, cache_k_rope, state_gdn, state_conv, cache_mem_k, cache_mem_v, mem_prompt, norm_mix, w_in, conv_w, a_log, dt_bias, gdn_norm, q_a_norm, w_qb, kv_a_norm, w_kvb, q_norm, k_nope_norm, k_rope_norm, w_out, norm_mem, mem_norm, w_mq, w_mk, w_mv, mq_norm, mk_norm, w_mo, norm_ffn, w_router, b_router, w_gu, b_gu, w_down, b_down):
    depth = norm_mix.shape[0]
    yp, ys = x_prompt, x_sample
    bp = x_prompt.shape[0]
    hw = MEM_HEADS * MEM_HEAD_DIM
    outs = [[] for _ in range(10)]
    for l in range(depth):
        pw, ew = _prep_weights(norm_mix[l], w_in[l], conv_w[l], a_log[l], dt_bias[l], gdn_norm[l], q_a_norm[l],
                               w_qb[l], kv_a_norm[l], w_kvb[l], q_norm[l], k_nope_norm[l], k_rope_norm[l], w_out[l],
                               norm_mem[l], mem_norm[l], w_mq[l], w_mk[l], w_mv[l], mq_norm[l], mk_norm[l], w_mo[l],
                               norm_ffn[l], w_router[l], b_router[l], w_gu[l], b_gu[l], w_down[l], b_down[l])
        nm = mem_prompt.shape[1]
        mk, mv = _memkv(mem_prompt.reshape(bp * nm, D_MODEL), pw['mnorm'], pw['w_mk'], pw['w_mv'], pw['mkn'],
                        _pick(bp * nm, (512, 256)))
        mk = mk.reshape(bp, nm, hw)
        mv = mv.reshape(bp, nm, hw)
        n_groups = 2 if bp % 2 == 0 else 1
        streams_p, (lat, kpe, s_fin, cv) = _trunk_front(
            yp, jnp.zeros((bp, 0, KV_RANK), F32), jnp.zeros((bp, 0, ROPE_DIM), F32),
            jnp.zeros((bp, GDN_HEADS, GDN_DK, GDN_DV), F32), jnp.zeros((bp, CONV_W - 1, CONV_DIM), F32), mk, mv, pw,
            n_groups)
        bs = x_sample.shape[0]
        (stream_s,), (lat2, kpe2, s_fin2, cv2) = _trunk_front(
            ys, cache_kv_latent[l], cache_k_rope[l], state_gdn[l], state_conv[l],
            cache_mem_k[l], cache_mem_v[l], pw, 1)
        tp = yp.shape[0] * yp.shape[1]
        tg = tp // n_groups
        ybuf = None
        for gi, st in enumerate(streams_p):
            last = gi == n_groups - 1
            res = _moe([st] + ([stream_s] if last else []),
                       [(ybuf, gi * tg, tp)] + ([(None, 0, None)] if last else []), ew)
            ybuf = res[0]
        yp, ys = ybuf.reshape(yp.shape), res[1].reshape(ys.shape)
        for lst, val in zip(outs, (lat, kpe, s_fin, cv, mk.reshape(bp, nm, MEM_HEADS, MEM_HEAD_DIM),
                                   mv.reshape(bp, nm, MEM_HEADS, MEM_HEAD_DIM), lat2, kpe2, s_fin2, cv2)):
            lst.append(val)
    return (yp, ys) + tuple(jnp.stack(o) for o in outs)
```

```python
import functools
import math

import numpy as np
import jax
import jax.numpy as jnp
from jax import lax
from jax.experimental import pallas as pl
from jax.experimental.pallas import tpu as pltpu
from jax.experimental.pallas import tpu_sc as plsc

F32 = jnp.float32
BF16 = jnp.bfloat16

D_MODEL = 1024
CHUNK = 64
EPS = 1e-6
GDN_HEADS = 4
GDN_DK = 128
GDN_DV = 128
CONV_W = 4
GDN_QK = GDN_HEADS * GDN_DK
GDN_VW = GDN_HEADS * GDN_DV
CONV_DIM = 2 * GDN_QK + GDN_VW
MLA_HEADS = 4
Q_RANK = 384
KV_RANK = 256
NOPE_DIM = 128
ROPE_DIM = 64
V_DIM = 128
QK_DIM = NOPE_DIM + ROPE_DIM
ROPE_THETA = 10000.0
N_MEM = 256
MEM_HEADS = 4
MEM_HEAD_DIM = 128
N_EXPERTS = 32
TOP_K = 4
D_FF = D_MODEL
SWIGLU_ALPHA = 1.702
SWIGLU_LIMIT = 7.0

LANES = 128
Q_SLAB = 2 * LANES
NEG_BIG = -1e30
VMEM_LIMIT = 56 * 1024 * 1024
MOE_ROWS = 512
GDN_UNITS = 8
FF_CHUNK = 256
SC_WINDOW = 128
PK_CHUNKS = D_MODEL // 2 // LANES
SUBTILE_ROWS = 256
ONES_ROWS = 16
MIX_ROWS = 512


def _cparams(sem):
    return pltpu.CompilerParams(dimension_semantics=sem, vmem_limit_bytes=VMEM_LIMIT)


def _dot(a, b):
    return jnp.dot(a, b, preferred_element_type=F32)


def _dot_nt(a, b):
    return lax.dot_general(a, b, (((1,), (1,)), ((), ())), preferred_element_type=F32)


def _dot_tn(a, b):
    return lax.dot_general(a, b, (((0,), (0,)), ((), ())), preferred_element_type=F32)


def _rms(x, gain, n=None):
    n = x.shape[-1] if n is None else n
    ss = jnp.sum(x * x, axis=-1, keepdims=True) * (1.0 / n)
    return (x * lax.rsqrt(ss + EPS)) * gain


def _sigmoid(x):
    return 1.0 / (1.0 + jnp.exp(-x))


def _rope128(r, cos, sna, snb):
    return r * cos + pltpu.roll(r, 96, 1) * sna + pltpu.roll(r, 32, 1) * snb


def _pack_bf16_pairs(x):
    n = x.shape[1] // 2
    lo = pltpu.bitcast(x[:, :n], jnp.uint32) >> 16
    hi = pltpu.bitcast(x[:, n:], jnp.uint32) & jnp.uint32(0xFFFF0000)
    return lo | hi


def _unpack_bf16_pairs(p):
    lo = pltpu.bitcast(p << 16, F32)
    hi = pltpu.bitcast(p & jnp.uint32(0xFFFF0000), F32)
    return jnp.concatenate([lo, hi], axis=1)


def _subtiles(rows):
    n = rows // SUBTILE_ROWS if rows % SUBTILE_ROWS == 0 else 1
    step = rows // n
    return [slice(i * step, (i + 1) * step) for i in range(n)]


def _const_spec(shape):
    nd = len(shape)
    return pl.BlockSpec(shape, lambda *_: (0,) * nd)


def _inproj_kernel(x_ref, nmix_ref, wu_ref, wz_ref, wcq_ref, wckv_ref, ws_ref, qan_ref, wqb_ref, qn_ref,
                   kvan_ref, krn_ref, alog_ref, dtb_ref, cos_ref, sna_ref, snb_ref,
                   u_ref, z_ref, q_ref, lat_ref, small_ref):
    for rs in _subtiles(x_ref.shape[0]):
        x = x_ref[rs, :]
        hb = _rms(x, nmix_ref[...]).astype(BF16)
        u_ref[rs, :] = _dot(hb, wu_ref[...])
        z_ref[rs, :] = _dot(hb, wz_ref[...])
        cos, sna, snb = cos_ref[rs, :], sna_ref[rs, :], snb_ref[rs, :]

        cq = _rms(_dot(hb, wcq_ref[...]), qan_ref[...]).astype(BF16)
        qf = _dot(cq, wqb_ref[...])
        scale = QK_DIM ** -0.5
        for h in range(MLA_HEADS):
            slab = qf[:, h * Q_SLAB:(h + 1) * Q_SLAB]
            slab = _rms(slab, qn_ref[...], n=QK_DIM)
            nope = slab[:, :LANES]
            ropd = _rope128(slab[:, LANES:], cos, sna, snb)
            q_ref[rs, h * Q_SLAB:h * Q_SLAB + LANES] = (nope * scale).astype(BF16)
            q_ref[rs, h * Q_SLAB + LANES:(h + 1) * Q_SLAB] = (ropd * scale).astype(BF16)

        lat_ref[rs, :] = _rms(_dot(hb, wckv_ref[...]), kvan_ref[...])

        sm = _dot(hb, ws_ref[...])
        lane = lax.broadcasted_iota(jnp.int32, sm.shape, 1)
        kp = jnp.where(lane < ROPE_DIM, sm, 0.0)
        kpe = _rope128(_rms(kp, krn_ref[...], n=ROPE_DIM), cos, sna, snb)
        sp = sm + dtb_ref[...]
        softplus = jnp.maximum(sp, 0.0) + jnp.log1p(jnp.exp(-jnp.abs(sp)))
        g = -jnp.exp(alog_ref[...]) * softplus
        beta = _sigmoid(sm)
        small_ref[rs, :] = jnp.where(lane < ROPE_DIM, kpe,
                                     jnp.where(lane < ROPE_DIM + GDN_HEADS, g,
                                               jnp.where(lane < ROPE_DIM + 2 * GDN_HEADS, beta, 0.0)))


def _inproj(x2d, S, tm, pw, tabs):
    T = x2d.shape[0]
    if tm > S:
        tabs = [jnp.tile(t, (tm // S, 1)) for t in tabs]
    nblk_s = max(S // tm, 1)
    row = lambda n: pl.BlockSpec((tm, n), lambda i: (i, 0))
    tab = pl.BlockSpec((tm, LANES), lambda i: (i % nblk_s, 0))
    consts = [pw['nmix'], pw['w_u'], pw['w_z'], pw['w_cq'], pw['w_ckv'], pw['w_s'], pw['qan'], pw['w_qb'],
              pw['qn'], pw['kvan'], pw['krn'], pw['alog'], pw['dtb']]
    return pl.pallas_call(
        _inproj_kernel,
        grid=(T // tm,),
        in_specs=[row(D_MODEL)] + [_const_spec(c.shape) for c in consts] + [tab, tab, tab],
        out_specs=[row(CONV_DIM), row(GDN_VW), row(MLA_HEADS * Q_SLAB), row(KV_RANK), row(LANES)],
        out_shape=[jax.ShapeDtypeStruct((T, CONV_DIM), F32), jax.ShapeDtypeStruct((T, GDN_VW), F32),
                   jax.ShapeDtypeStruct((T, MLA_HEADS * Q_SLAB), BF16), jax.ShapeDtypeStruct((T, KV_RANK), F32),
                   jax.ShapeDtypeStruct((T, LANES), F32)],
        compiler_params=_cparams(("parallel",)),
        name="inproj",
    )(x2d, *consts, *tabs)


def _split3(x):
    hi = x.astype(BF16)
    r = x - hi.astype(F32)
    mid = r.astype(BF16)
    lo = (r - mid.astype(F32)).astype(BF16)
    return hi, mid, lo


def _gdn_kernel(u_ref, small_ref, z_ref, convw_ref, cpast_ref, s0_ref, gnorm_ref,
                to_ref, trilm_ref, strictm_ref, same2_ref, lvl_ref,
                o_ref, sfin_ref, cnew_ref, ext_ref, uc_ref, state_ref,
                qf_ref, kf_ref, vb_ref, bt_ref, gcum_ref, glast_ref, kk_ref, qk_ref, mb_ref, x_ref, qkb_ref,
                kbe_ref, qg_ref, kdec_ref, egl_ref, t1_ref, uu_ref, ww_ref, vn_ref, qs_ref, *, NB, LC):
    j = pl.program_id(1)
    nj = pl.num_programs(1)
    PADR = 8
    C = LC // CHUNK
    U = NB * C
    HR = GDN_HEADS * CHUNK

    @pl.when(j == 0)
    def _():
        state_ref[...] = s0_ref[...]
        ext_ref[:, PADR - (CONV_W - 1):PADR, :] = cpast_ref[...]

    w = convw_ref[...]
    for nb in range(NB):
        ext_ref[nb, PADR:PADR + LC, :] = u_ref[nb]
        acc = ext_ref[nb, PADR:PADR + LC, :] * w[CONV_W - 1:CONV_W, :]
        for t in range(1, CONV_W):
            acc = acc + ext_ref[nb, PADR - t:PADR - t + LC, :] * w[CONV_W - 1 - t:CONV_W - t, :]
        uc_ref[nb] = acc * _sigmoid(acc)
        ext_ref[nb, 0:PADR, :] = ext_ref[nb, LC:LC + PADR, :]

    @pl.when(j == nj - 1)
    def _():
        cnew_ref[...] = ext_ref[:, PADR - (CONV_W - 1):PADR, :]

    units = [(nb, c) for nb in range(NB) for c in range(C)]
    g0 = ROPE_DIM
    b0 = ROPE_DIM + GDN_HEADS
    to = to_ref[...]
    for u, (nb, c) in enumerate(units):
        rows = slice(c * CHUNK, (c + 1) * CHUNK)
        sm = small_ref[nb, rows, :]
        gl = sum(_dot(to, part) for part in _split3(sm))
        for h in range(GDN_HEADS):
            hr = slice(h * CHUNK, (h + 1) * CHUNK)
            q = uc_ref[nb, rows, h * GDN_DK:(h + 1) * GDN_DK]
            k = uc_ref[nb, rows, GDN_QK + h * GDN_DK:GDN_QK + (h + 1) * GDN_DK]
            v = uc_ref[nb, rows, 2 * GDN_QK + h * GDN_DV:2 * GDN_QK + (h + 1) * GDN_DV]
            beta = jnp.broadcast_to(sm[:, b0 + h:b0 + h + 1], (CHUNK, LANES))
            qf_ref[u, hr, :] = (q * lax.rsqrt(jnp.sum(q * q, -1, keepdims=True) + EPS)) * (GDN_DK ** -0.5)
            kf_ref[u, hr, :] = k * lax.rsqrt(jnp.sum(k * k, -1, keepdims=True) + EPS)
            vb_ref[u, hr, :] = (v * beta).astype(BF16)
            bt_ref[u, hr, :] = beta
            gcum_ref[u, hr, :] = jnp.broadcast_to(gl[:CHUNK, g0 + h:g0 + h + 1], (CHUNK, LANES))
            glast_ref[u, hr, :] = jnp.broadcast_to(gl[CHUNK:, g0 + h:g0 + h + 1], (CHUNK, LANES))

    for u in range(U):
        k = kf_ref[u]
        kbf = k.astype(BF16)
        kk_ref[u] = _dot_nt((k * bt_ref[u]).astype(BF16), kbf)
        qk_ref[u] = _dot_nt(qf_ref[u].astype(BF16), kbf)

    trilm = trilm_ref[...]
    eye = trilm - strictm_ref[...]
    for u in range(U):
        gcum = gcum_ref[u]
        grow = gcum.T[0:1, :]
        gcol = jnp.concatenate([gcum, gcum], axis=1)
        decay = jnp.exp(jnp.minimum(gcol - grow, 0.0)) * trilm
        m = kk_ref[u] * (decay * strictm_ref[...])
        mb_ref[u] = m.astype(BF16)
        x_ref[u] = eye - m * same2_ref[...]
        qkb_ref[u] = (qk_ref[u] * decay).astype(BF16)
        egc = jnp.exp(gcum)
        k = kf_ref[u]
        kbe_ref[u] = (k * bt_ref[u] * egc).astype(BF16)
        qg_ref[u] = (qf_ref[u] * egc).astype(BF16)
        kdec_ref[u] = (k * jnp.exp(glast_ref[u] - gcum)).astype(BF16)
        egl_ref[u] = jnp.exp(glast_ref[u])

    for lvl in range(lvl_ref.shape[0]):
        lm = lvl_ref[lvl]
        for u in range(U):
            t1_ref[u] = _dot(mb_ref[u] * lm, x_ref[u].astype(BF16)).astype(BF16)
        for u in range(U):
            x = x_ref[u]
            x_ref[u] = x - _dot(x.astype(BF16), t1_ref[u])

    for u in range(U):
        xb = x_ref[u].astype(BF16)
        uu_ref[u] = _dot(xb, vb_ref[u])
        ww_ref[u] = _dot(xb, kbe_ref[u]).astype(BF16)

    gnorm = gnorm_ref[...]
    hrs = [slice(h * CHUNK, (h + 1) * CHUNK) for h in range(GDN_HEADS)]
    for c in range(C):
        rows = slice(c * CHUNK, (c + 1) * CHUNK)
        us = [nb * C + c for nb in range(NB)]
        for nb, u in enumerate(us):
            for h, hr in enumerate(hrs):
                stb = state_ref[nb, h].astype(BF16)
                r = _dot(jnp.concatenate([ww_ref[u, hr, :], qg_ref[u, hr, :]], axis=0), stb)
                vn_ref[u, hr, :] = (uu_ref[u, hr, :] - r[:CHUNK]).astype(BF16)
                qs_ref[u, hr, :] = r[CHUNK:]
        outs = [qs_ref[u] + _dot(qkb_ref[u], vn_ref[u]) for u in us]
        for nb, u in enumerate(us):
            for h, hr in enumerate(hrs):
                state_ref[nb, h] = (state_ref[nb, h] * egl_ref[u, h * CHUNK:h * CHUNK + 1, :]
                                    + _dot_tn(kdec_ref[u, hr, :], vn_ref[u, hr, :]))
        for nb, u in enumerate(us):
            for h, hr in enumerate(hrs):
                zz = z_ref[nb, rows, h * GDN_DV:(h + 1) * GDN_DV]
                og = _rms(outs[nb][hr, :], gnorm) * (zz * _sigmoid(zz))
                o_ref[nb, rows, h * GDN_DV:(h + 1) * GDN_DV] = og.astype(BF16)

    @pl.when(j == nj - 1)
    def _():
        sfin_ref[...] = state_ref[...]


def _gdn_masks():
    hr = GDN_HEADS * CHUNK
    i = np.arange(hr)[:, None]
    j = np.arange(hr)[None, :]
    same_head = (i // CHUNK) == (j // CHUNK)
    tril = same_head & (i >= j)
    strict = same_head & (i > j)
    same2 = strict & ((i // 2) == (j // 2))
    lvls = []
    blk = 2
    while blk < CHUNK:
        lvls.append(strict & ((i // (2 * blk)) == (j // (2 * blk))) & ((i // blk) != (j // blk)))
        blk *= 2
    fr = np.arange(CHUNK)
    to = np.concatenate([fr[:, None] >= fr[None, :], np.ones((CHUNK, CHUNK), bool)], axis=0)
    f = lambda a: jnp.asarray(a.astype(np.float32))
    return (jnp.asarray(to.astype(np.float32), dtype=BF16), f(tril), f(strict), f(same2),
            jnp.asarray(np.stack(lvls).astype(np.float32), dtype=BF16))


def _gdn(u3, small3, z3, conv_w, conv_past, s0, gnorm, NB, LC):
    B, S, _ = u3.shape
    C = LC // CHUNK
    U = NB * C
    HR = GDN_HEADS * CHUNK
    masks = _gdn_masks()
    tile = lambda n: pl.BlockSpec((NB, LC, n), lambda b, j: (b, j, 0))
    stspec = pl.BlockSpec((NB, GDN_HEADS, GDN_DK, GDN_DV), lambda b, j: (b, 0, 0, 0))
    cvspec = pl.BlockSpec((NB, CONV_W - 1, CONV_DIM), lambda b, j: (b, 0, 0))
    vm = lambda shape, dt: pltpu.VMEM(shape, dt)
    return pl.pallas_call(
        functools.partial(_gdn_kernel, NB=NB, LC=LC),
        grid=(B // NB, S // LC),
        in_specs=[tile(CONV_DIM), tile(LANES), tile(GDN_VW), _const_spec(conv_w.shape), cvspec, stspec,
                  _const_spec(gnorm.shape)] + [_const_spec(m.shape) for m in masks],
        out_specs=[tile(GDN_VW), stspec, cvspec],
        out_shape=[jax.ShapeDtypeStruct((B, S, GDN_VW), BF16),
                   jax.ShapeDtypeStruct((B, GDN_HEADS, GDN_DK, GDN_DV), F32),
                   jax.ShapeDtypeStruct((B, CONV_W - 1, CONV_DIM), F32)],
        scratch_shapes=[vm((NB, LC + 8, CONV_DIM), F32), vm((NB, LC, CONV_DIM), F32),
                        vm((NB, GDN_HEADS, GDN_DK, GDN_DV), F32),
                        vm((U, HR, LANES), F32), vm((U, HR, LANES), F32), vm((U, HR, LANES), BF16),
                        vm((U, HR, LANES), F32),
                        vm((U, HR, LANES), F32), vm((U, HR, LANES), F32),
                        vm((U, HR, HR), F32), vm((U, HR, HR), F32),
                        vm((U, HR, HR), BF16), vm((U, HR, HR), F32), vm((U, HR, HR), BF16),
                        vm((U, HR, LANES), BF16), vm((U, HR, LANES), BF16), vm((U, HR, LANES), BF16),
                        vm((U, HR, LANES), F32), vm((U, HR, HR), BF16),
                        vm((U, HR, LANES), F32), vm((U, HR, LANES), BF16),
                        vm((U, HR, LANES), BF16), vm((U, HR, LANES), F32)],
        compiler_params=_cparams(("parallel", "arbitrary")),
        name="gdn",
    )(u3, small3, z3, conv_w, conv_past, s0, gnorm, *masks)


def _kvproj_kernel(lat_ref, kpe_ref, wk_ref, wv_ref, knn_ref, kn_ref, kr_ref, v_ref, *, v_transposed):
    lb = lat_ref[...].astype(BF16)
    kf = _dot(lb, wk_ref[...])
    for h in range(MLA_HEADS):
        sl = slice(h * NOPE_DIM, (h + 1) * NOPE_DIM)
        kn_ref[:, sl] = _rms(kf[:, sl], knn_ref[...]).astype(BF16)
    if v_transposed:
        v_ref[...] = _dot_nt(wv_ref[...], lb).astype(BF16)
    else:
        v_ref[...] = _dot(lb, wv_ref[...]).astype(BF16)
    kp = kpe_ref[...]
    if kp.shape[1] == ROPE_DIM:
        kr_ref[...] = jnp.concatenate([kp, jnp.zeros((kp.shape[0], LANES - ROPE_DIM), F32)], axis=1).astype(BF16)
    else:
        lane = lax.broadcasted_iota(jnp.int32, kp.shape, 1)
        kr_ref[...] = jnp.where(lane < ROPE_DIM, kp, 0.0).astype(BF16)


def _kvproj(lat3, kpe3, pw, tm, v_transposed):
    B, sk, _ = lat3.shape
    hw = MLA_HEADS * V_DIM
    row = lambda n: pl.BlockSpec((None, tm, n), lambda b, i: (b, i, 0))
    if v_transposed:
        vspec, vshape, w_v = pl.BlockSpec((None, hw, tm), lambda b, i: (b, 0, i)), (B, hw, sk), pw['w_vt']
    else:
        vspec, vshape, w_v = row(hw), (B, sk, hw), pw['w_v']
    return pl.pallas_call(
        functools.partial(_kvproj_kernel, v_transposed=v_transposed),
        grid=(B, sk // tm),
        in_specs=[row(KV_RANK), row(kpe3.shape[-1]), _const_spec(pw['w_k'].shape), _const_spec(w_v.shape),
                  _const_spec(pw['knn'].shape)],
        out_specs=[row(MLA_HEADS * NOPE_DIM), row(LANES), vspec],
        out_shape=[jax.ShapeDtypeStruct((B, sk, MLA_HEADS * NOPE_DIM), BF16),
                   jax.ShapeDtypeStruct((B, sk, LANES), BF16), jax.ShapeDtypeStruct(vshape, BF16)],
        compiler_params=_cparams(("parallel", "parallel")),
        name="kvproj",
    )(lat3, kpe3, pw['w_k'], w_v, pw['knn'])


def _last_kblock(qi, tq, tk, P, nk):
    last_key = ((P + qi * tq + tq - 1) // CHUNK) * CHUNK + CHUNK - 1
    return jnp.minimum(last_key // tk, nk - 1)


def _flash_kernel(q_ref, kn_ref, kr_ref, v_ref, o_ref, m_ref, l_ref, acc_ref, *, tq, tk, P, S, nk):
    qi = pl.program_id(1)
    ki = pl.program_id(2)

    @pl.when(ki == 0)
    def _():
        m_ref[...] = jnp.full(m_ref.shape, NEG_BIG, F32)
        l_ref[...] = jnp.zeros(l_ref.shape, F32)
        acc_ref[...] = jnp.zeros(acc_ref.shape, F32)

    @pl.when(ki <= _last_kblock(qi, tq, tk, P, nk))
    def _():
        qpos = P + qi * tq + lax.broadcasted_iota(jnp.int32, (tq, tk), 0)
        kpos = ki * tk + lax.broadcasted_iota(jnp.int32, (tq, tk), 1)
        mask = ((kpos // CHUNK) <= (qpos // CHUNK)) & (kpos < P + S)
        kr = kr_ref[...]
        for h in range(MLA_HEADS):
            qh = q_ref[:, h * Q_SLAB:(h + 1) * Q_SLAB]
            kh = jnp.concatenate([kn_ref[:, h * NOPE_DIM:(h + 1) * NOPE_DIM], kr], axis=1)
            s = jnp.where(mask, _dot_nt(qh, kh), NEG_BIG)
            m_prev = m_ref[h]
            m_new = jnp.maximum(m_prev, jnp.max(s, axis=-1, keepdims=True))
            alpha = jnp.exp(m_prev - m_new)
            p = jnp.exp(s - m_new)
            l_ref[h] = alpha * l_ref[h] + jnp.sum(p, axis=-1, keepdims=True)
            acc_ref[h] = alpha * acc_ref[h] + _dot(p.astype(BF16), v_ref[:, h * V_DIM:(h + 1) * V_DIM])
            m_ref[h] = m_new

    @pl.when(ki == nk - 1)
    def _():
        for h in range(MLA_HEADS):
            o_ref[:, h * V_DIM:(h + 1) * V_DIM] = (acc_ref[h] / l_ref[h]).astype(BF16)


def _flash(q3, kn3, kr3, v3, P, S, tq, tk):
    B = q3.shape[0]
    skp = kn3.shape[1]
    nk = skp // tk
    kmap = lambda b, qi, ki: (b, jnp.minimum(ki, _last_kblock(qi, tq, tk, P, nk)), 0)
    return pl.pallas_call(
        functools.partial(_flash_kernel, tq=tq, tk=tk, P=P, S=S, nk=nk),
        grid=(B, S // tq, nk),
        in_specs=[pl.BlockSpec((None, tq, MLA_HEADS * Q_SLAB), lambda b, qi, ki: (b, qi, 0)),
                  pl.BlockSpec((None, tk, MLA_HEADS * NOPE_DIM), kmap),
                  pl.BlockSpec((None, tk, LANES), kmap),
                  pl.BlockSpec((None, tk, MLA_HEADS * V_DIM), kmap)],
        out_specs=pl.BlockSpec((None, tq, MLA_HEADS * V_DIM), lambda b, qi, ki: (b, qi, 0)),
        out_shape=jax.ShapeDtypeStruct((B, S, MLA_HEADS * V_DIM), BF16),
        scratch_shapes=[pltpu.VMEM((MLA_HEADS, tq, 1), F32), pltpu.VMEM((MLA_HEADS, tq, 1), F32),
                        pltpu.VMEM((MLA_HEADS, tq, V_DIM), F32)],
        compiler_params=_cparams(("parallel", "parallel", "arbitrary")),
        name="mla_attn",
    )(q3, kn3, kr3, v3)


def _attn_dec_kernel(q_ref, knp_ref, krp_ref, vp_ref, knn_ref, krn_ref, vn_ref, o_ref):
    krp, krn = krp_ref[...], krn_ref[...]
    hsl = [slice(h * NOPE_DIM, (h + 1) * NOPE_DIM) for h in range(MLA_HEADS)]
    qs = [q_ref[:, h * Q_SLAB:(h + 1) * Q_SLAB] for h in range(MLA_HEADS)]
    sps = [_dot_nt(q, jnp.concatenate([knp_ref[:, sl], krp], axis=1)) for q, sl in zip(qs, hsl)]
    sns = [_dot_nt(q, jnp.concatenate([knn_ref[:, sl], krn], axis=1)) for q, sl in zip(qs, hsl)]
    for h, (sp, sn) in enumerate(zip(sps, sns)):
        vsl = slice(h * V_DIM, (h + 1) * V_DIM)
        m = jnp.maximum(jnp.max(sp, axis=-1, keepdims=True), jnp.max(sn, axis=-1, keepdims=True))
        pp = jnp.exp(sp - m)
        pn = jnp.exp(sn - m)
        l = jnp.sum(pp, axis=-1, keepdims=True) + jnp.sum(pn, axis=-1, keepdims=True)
        o = _dot(pp.astype(BF16), vp_ref[:, vsl]) + _dot(pn.astype(BF16), vn_ref[:, vsl])
        o_ref[:, vsl] = (o / l).astype(BF16)


def _attn_dec(q3, past, new):
    B, S, _ = q3.shape
    specs = [pl.BlockSpec((None, a.shape[1], a.shape[2]), lambda b: (b, 0, 0)) for a in (q3, *past, *new)]
    return pl.pallas_call(
        _attn_dec_kernel,
        grid=(B,),
        in_specs=specs,
        out_specs=pl.BlockSpec((None, S, MLA_HEADS * V_DIM), lambda b: (b, 0, 0)),
        out_shape=jax.ShapeDtypeStruct((B, S, MLA_HEADS * V_DIM), BF16),
        compiler_params=_cparams(("parallel",)),
        name="mla_attn_dec",
    )(q3, *past, *new)


def _flash_t_kernel(q_ref, kn_ref, kr_ref, vt_ref, o_ref, m_ref, acc_ref, *, tq, tk, P, S, nk):
    qi = pl.program_id(1)
    ki = pl.program_id(2)
    q0 = P + qi * tq
    k0 = ki * tk

    @pl.when(ki == 0)
    def _():
        m_ref[...] = jnp.full(m_ref.shape, NEG_BIG, F32)
        acc_ref[...] = jnp.zeros(acc_ref.shape, F32)

    def step(masked):
        kr = kr_ref[...]
        ones = jnp.ones((ONES_ROWS, tk), BF16)
        if masked:
            kpos = k0 + lax.broadcasted_iota(jnp.int32, (tk, 1), 0)
            qpos = q0 + lax.broadcasted_iota(jnp.int32, (1, tq), 1)
            mask = ((kpos // CHUNK) <= (qpos // CHUNK)) & (kpos < P + S)
        sts = []
        for h in range(MLA_HEADS):
            kh = jnp.concatenate([kn_ref[:, h * NOPE_DIM:(h + 1) * NOPE_DIM], kr], axis=1)
            sts.append(_dot_nt(kh, q_ref[:, h * Q_SLAB:(h + 1) * Q_SLAB]))
        ps, alphas = [], []
        for h in range(MLA_HEADS):
            st = sts[h]
            if masked:
                st = jnp.where(mask, st, NEG_BIG)
            m_prev = m_ref[h]
            m_new = jnp.maximum(m_prev, jnp.max(st, axis=0, keepdims=True))
            alphas.append(jnp.exp(m_prev - m_new))
            ps.append(jnp.exp(st - m_new).astype(BF16))
            m_ref[h] = m_new
        for h in range(MLA_HEADS):
            vt1 = jnp.concatenate([vt_ref[h * V_DIM:(h + 1) * V_DIM, :], ones], axis=0)
            acc_ref[h] = alphas[h] * acc_ref[h] + _dot(vt1, ps[h])

    needed = ki <= _last_kblock(qi, tq, tk, P, nk)
    full = ((k0 + tk - 1) // CHUNK <= q0 // CHUNK) & (k0 + tk <= P + S)
    pl.when(needed & full)(functools.partial(step, False))
    pl.when(needed & jnp.logical_not(full))(functools.partial(step, True))

    @pl.when(ki == nk - 1)
    def _():
        for h in range(MLA_HEADS):
            acc = acc_ref[h]
            o_ref[:, h * V_DIM:(h + 1) * V_DIM] = (acc[:V_DIM] / acc[V_DIM:V_DIM + 1]).T.astype(BF16)


def _flash_t(q3, kn3, kr3, vt3, P, S, tq, tk):
    B = q3.shape[0]
    skp = kn3.shape[1]
    nk = skp // tk
    kblk = lambda qi, ki: jnp.minimum(ki, _last_kblock(qi, tq, tk, P, nk))
    kmap = lambda b, qi, ki: (b, kblk(qi, ki), 0)
    return pl.pallas_call(
        functools.partial(_flash_t_kernel, tq=tq, tk=tk, P=P, S=S, nk=nk),
        grid=(B, S // tq, nk),
        in_specs=[pl.BlockSpec((None, tq, MLA_HEADS * Q_SLAB), lambda b, qi, ki: (b, qi, 0)),
                  pl.BlockSpec((None, tk, MLA_HEADS * NOPE_DIM), kmap),
                  pl.BlockSpec((None, tk, LANES), kmap),
                  pl.BlockSpec((None, MLA_HEADS * V_DIM, tk), lambda b, qi, ki: (b, 0, kblk(qi, ki)))],
        out_specs=pl.BlockSpec((None, tq, MLA_HEADS * V_DIM), lambda b, qi, ki: (b, qi, 0)),
        out_shape=jax.ShapeDtypeStruct((B, S, MLA_HEADS * V_DIM), BF16),
        scratch_shapes=[pltpu.VMEM((MLA_HEADS, 1, tq), F32), pltpu.VMEM((MLA_HEADS, V_DIM + ONES_ROWS, tq), F32)],
        compiler_params=_cparams(("parallel", "parallel", "arbitrary")),
        name="mla_attn_t",
    )(q3, kn3, kr3, vt3)


def _memkv_kernel(mem_ref, mnorm_ref, wmk_ref, wmv_ref, mkn_ref, k_ref, v_ref):
    mb = _rms(mem_ref[...], mnorm_ref[...]).astype(BF16)
    kf = _dot(mb, wmk_ref[...])
    for h in range(MEM_HEADS):
        sl = slice(h * MEM_HEAD_DIM, (h + 1) * MEM_HEAD_DIM)
        k_ref[:, sl] = _rms(kf[:, sl], mkn_ref[...])
    v_ref[...] = _dot(mb, wmv_ref[...])


def _memkv(mem2d, mnorm, w_mk, w_mv, mkn, tm):
    T = mem2d.shape[0]
    hw = MEM_HEADS * MEM_HEAD_DIM
    row = lambda n: pl.BlockSpec((tm, n), lambda i: (i, 0))
    return pl.pallas_call(
        _memkv_kernel,
        grid=(T // tm,),
        in_specs=[row(D_MODEL), _const_spec(mnorm.shape), _const_spec(w_mk.shape), _const_spec(w_mv.shape),
                  _const_spec(mkn.shape)],
        out_specs=[row(hw), row(hw)],
        out_shape=[jax.ShapeDtypeStruct((T, hw), F32), jax.ShapeDtypeStruct((T, hw), F32)],
        compiler_params=_cparams(("parallel",)),
        name="mem_kv",
    )(mem2d, mnorm, w_mk, w_mv, mkn)


def _mix_kernel(x_ref, og_ref, om_ref, mk_ref, mv_ref, wout_ref, nmem_ref, wmq_ref, mqn_ref, wmo_ref, nffn_ref,
                wrh_ref, wrl_ref, br_ref, x2_ref, h3_ref, idx_ref, gate_ref, counts_ref, cnt_ref):
    def mem_head(ref, b, h):
        if len(ref.shape) == 3:
            return ref[b, :, h * MEM_HEAD_DIM:(h + 1) * MEM_HEAD_DIM].astype(BF16)
        return ref[b, :, h, :].astype(BF16)

    first = (pl.program_id(0) == 0) & (pl.program_id(1) == 0)

    @pl.when(first)
    def _():
        cnt_ref[...] = jnp.zeros(cnt_ref.shape, F32)

    nbm, tm, _ = x_ref.shape
    rows = nbm * tm
    flat = lambda ref: ref[...].reshape(rows, ref.shape[-1])
    if True:
        x1 = flat(x_ref) + _dot(flat(og_ref), wout_ref[0:GDN_VW, :]) + _dot(flat(om_ref), wout_ref[GDN_VW:, :])
        hb = _rms(x1, nmem_ref[...]).astype(BF16)
        qm = _dot(hb, wmq_ref[...])
        per_batch = []
        hsl = [slice(h * MEM_HEAD_DIM, (h + 1) * MEM_HEAD_DIM) for h in range(MEM_HEADS)]
        for b in range(nbm):
            br = slice(b * tm, (b + 1) * tm)
            qhs = [(_rms(qm[br, sl], mqn_ref[...]) * (MEM_HEAD_DIM ** -0.5)).astype(BF16) for sl in hsl]
            ss = [_dot_nt(qh, mem_head(mk_ref, b, h)) for h, qh in enumerate(qhs)]
            ps = []
            for s in ss:
                p = jnp.exp(s - jnp.max(s, axis=-1, keepdims=True))
                ps.append((p / jnp.sum(p, axis=-1, keepdims=True)).astype(BF16))
            heads = [_dot(p, mem_head(mv_ref, b, h)).astype(BF16) for h, p in enumerate(ps)]
            per_batch.append(jnp.concatenate(heads, axis=1))
        om = per_batch[0] if nbm == 1 else jnp.concatenate(per_batch, axis=0)
        x2 = x1 + _dot(om, wmo_ref[...])
        x2_ref[...] = x2.reshape(x2_ref.shape)
        h3 = _rms(x2, nffn_ref[...])
        hi = h3.astype(BF16)
        packed = _pack_bf16_pairs(hi.astype(F32))
        for c in range(h3_ref.shape[0]):
            h3_ref[c] = packed[:, c * LANES:(c + 1) * LANES]
        lo = (h3 - hi.astype(F32)).astype(BF16)
        wrh = wrh_ref[...]
        logits = _dot(hi, wrh) + _dot(lo, wrh) + _dot(hi, wrl_ref[...]) + br_ref[...]

        lane = lax.broadcasted_iota(jnp.int32, logits.shape, 1).astype(F32)
        vals, idxs = [], []
        cur = logits
        for _ in range(TOP_K):
            mx = jnp.max(cur, axis=-1, keepdims=True)
            ix = jnp.min(jnp.where(cur == mx, lane, float(LANES)), axis=-1, keepdims=True)
            vals.append(mx)
            idxs.append(ix)
            cur = jnp.where(lane == ix, -3e38, cur)
        es = [jnp.exp(v - vals[0]) for v in vals]
        den = es[0] + es[1] + es[2] + es[3]

        sel = jnp.zeros(logits.shape, F32)
        for k in range(TOP_K):
            sel = sel + jnp.where(lane == idxs[k], 1.0, 0.0)
        ri = lax.broadcasted_iota(jnp.int32, (rows, rows), 0)
        ci = lax.broadcasted_iota(jnp.int32, (rows, rows), 1)
        before = jnp.where(ri > ci, 1.0, 0.0).astype(BF16)
        excl = _dot(before, sel.astype(BF16)) + cnt_ref[...]
        cnt_ref[...] = cnt_ref[...] + jnp.sum(sel, axis=0, keepdims=True)
        counts_ref[...] = cnt_ref[...].astype(jnp.int32)

        idx_out = jnp.zeros(logits.shape, F32)
        gate_out = jnp.zeros(logits.shape, F32)
        for k in range(TOP_K):
            rank = jnp.sum(jnp.where(lane == idxs[k], excl, 0.0), axis=-1, keepdims=True)
            idx_out = jnp.where(lane == float(k), idxs[k], idx_out)
            idx_out = jnp.where(lane == float(TOP_K + k), rank, idx_out)
            gate_out = jnp.where(lane == float(k), es[k] / den, gate_out)
        idx_ref[...] = idx_out.astype(jnp.int32).reshape(idx_ref.shape)
        gate_ref[...] = gate_out.reshape(gate_ref.shape)


def _mix(x3, og3, om3, mk3, mv3, pw, tm, b0, B):
    S = x3.shape[1]
    hw = MEM_HEADS * MEM_HEAD_DIM
    nbm = _pick(B, tuple(n for n in (8, 4, 2, 1) if n * tm <= MIX_ROWS and b0 % n == 0)) if tm == S else 1
    nsb = S // tm
    boff = b0 // nbm
    tile_in = lambda n: pl.BlockSpec((nbm, tm, n), lambda b, i: (b + boff, i, 0))
    tile = lambda n: pl.BlockSpec((nbm, tm, n), lambda b, i: (b, i, 0))
    if mk3.ndim == 3:
        memspec = pl.BlockSpec((nbm, N_MEM, hw), lambda b, i: (b + boff, 0, 0))
    else:
        memspec = pl.BlockSpec((nbm, N_MEM, MEM_HEADS, MEM_HEAD_DIM), lambda b, i: (b + boff, 0, 0, 0))
    consts = [pw['w_out'], pw['nmem'], pw['w_mq'], pw['mqn'], pw['w_mo'], pw['nffn'], pw['wr_hi'], pw['wr_lo'],
              pw['b_r']]
    return pl.pallas_call(
        _mix_kernel,
        grid=(B // nbm, nsb),
        in_specs=[tile_in(D_MODEL), tile_in(GDN_VW), tile_in(MLA_HEADS * V_DIM), memspec, memspec]
                 + [_const_spec(c.shape) for c in consts],
        out_specs=[tile(D_MODEL), pl.BlockSpec((PK_CHUNKS, nbm * tm, LANES), lambda b, i: (0, b * nsb + i, 0)),
                   tile(LANES), tile(LANES), _const_spec((1, LANES))],
        out_shape=[jax.ShapeDtypeStruct((B, S, D_MODEL), F32),
                   jax.ShapeDtypeStruct((PK_CHUNKS, B * S, LANES), jnp.uint32),
                   jax.ShapeDtypeStruct((B, S, LANES), jnp.int32), jax.ShapeDtypeStruct((B, S, LANES), F32),
                   jax.ShapeDtypeStruct((1, LANES), jnp.int32)],
        scratch_shapes=[pltpu.VMEM((1, LANES), F32)],
        compiler_params=_cparams(("arbitrary", "arbitrary")),
        name="mix_mem_router",
    )(x3, og3, om3, mk3, mv3, *consts)


def _expert_kernel(be_ref, nx_ref, nu_ref, rows_ref, wgu_hbm, bgu_ref, wd_hbm, bd_ref, y_ref,
                   wgus_ref, wds_ref, wgub_ref, wdb_ref, sem_ref):
    i = pl.program_id(0)
    used = i < nu_ref[0]
    new_expert = (i == 0) | (be_ref[i] != be_ref[jnp.maximum(i - 1, 0)])

    def weight_copies(e):
        return (pltpu.make_async_copy(wgu_hbm.at[e], wgus_ref, sem_ref.at[0]),
                pltpu.make_async_copy(wd_hbm.at[e], wds_ref, sem_ref.at[1]))

    @pl.when(i == 0)
    def _():
        for cp in weight_copies(be_ref[0]):
            cp.start()

    @pl.when(used & new_expert)
    def _():
        for cp in weight_copies(be_ref[i]):
            cp.wait()

        def cast(r, carry):
            rs = pl.ds(pl.multiple_of(r * LANES, LANES), LANES)
            wgub_ref[rs, :] = wgus_ref[rs, :].astype(BF16)
            wdb_ref[rs, :] = wds_ref[rs, :].astype(BF16)
            return carry
        lax.fori_loop(0, D_MODEL // LANES, cast, 0)

        @pl.when(nx_ref[i] >= 0)
        def _():
            for cp in weight_copies(nx_ref[i]):
                cp.start()

    @pl.when(used)
    def _():
        packed = jnp.concatenate([rows_ref[c] for c in range(PK_CHUNKS)], axis=1)
        x = _unpack_bf16_pairs(packed).astype(BF16)
        acc = None
        for c in range(D_FF // FF_CHUNK):
            gs_ = slice(c * FF_CHUNK, (c + 1) * FF_CHUNK)
            us_ = slice(D_FF + c * FF_CHUNK, D_FF + (c + 1) * FF_CHUNK)
            gt = jnp.minimum(_dot(x, wgub_ref[:, gs_]) + bgu_ref[:, gs_], SWIGLU_LIMIT)
            up = jnp.clip(_dot(x, wgub_ref[:, us_]) + bgu_ref[:, us_], -SWIGLU_LIMIT, SWIGLU_LIMIT)
            act = gt * _sigmoid(SWIGLU_ALPHA * gt) * (up + 1.0)
            part = _dot(act.astype(BF16), wdb_ref[gs_, :])
            acc = part if acc is None else acc + part
        ypk = _pack_bf16_pairs((acc + bd_ref[...]).astype(BF16).astype(F32))
        for c in range(PK_CHUNKS):
            y_ref[c] = ypk[:, c * LANES:(c + 1) * LANES]

    @pl.when(jnp.logical_not(used))
    def _():
        y_ref[...] = jnp.zeros(y_ref.shape, y_ref.dtype)


def _experts(block_e, next_e, n_used, rows, w_gu, b_gu, w_down, b_down):
    n_rows = rows.shape[1]
    nb = n_rows // MOE_ROWS
    gs = pltpu.PrefetchScalarGridSpec(
        num_scalar_prefetch=3,
        grid=(nb,),
        in_specs=[pl.BlockSpec((PK_CHUNKS, MOE_ROWS, LANES), lambda i, be, nx, nu: (0, i, 0)),
                  pl.BlockSpec(memory_space=pl.ANY),
                  pl.BlockSpec((None, 1, 2 * D_FF), lambda i, be, nx, nu: (be[i], 0, 0)),
                  pl.BlockSpec(memory_space=pl.ANY),
                  pl.BlockSpec((None, 1, D_MODEL), lambda i, be, nx, nu: (be[i], 0, 0))],
        out_specs=pl.BlockSpec((PK_CHUNKS, MOE_ROWS, LANES), lambda i, be, nx, nu: (0, i, 0)),
        scratch_shapes=[pltpu.VMEM((D_MODEL, 2 * D_FF), F32), pltpu.VMEM((D_FF, D_MODEL), F32),
                        pltpu.VMEM((D_MODEL, 2 * D_FF), BF16), pltpu.VMEM((D_FF, D_MODEL), BF16),
                        pltpu.SemaphoreType.DMA((2,))],
    )
    return pl.pallas_call(
        _expert_kernel,
        grid_spec=gs,
        out_shape=jax.ShapeDtypeStruct((PK_CHUNKS, n_rows, LANES), jnp.uint32),
        compiler_params=_cparams(("arbitrary",)),
        name="moe_experts",
    )(block_e, next_e, n_used, rows, w_gu, b_gu, w_down, b_down)


def _sc_mesh():
    return plsc.VectorSubcoreMesh(core_axis_name="core", subcore_axis_name="subcore")


def _sc_scatter_rows(x3s, pos_ts, n_rows):
    C, _, L = x3s[0].shape
    K = pos_ts[0].shape[0]
    ns = len(x3s)

    @functools.partial(pl.kernel, out_type=jax.ShapeDtypeStruct((C, n_rows, L), x3s[0].dtype), mesh=_sc_mesh(),
                       scratch_types=[])
    def scatter(*refs):
        o_hbm = refs[2 * ns]
        for s in range(ns):
            x_hbm, i_hbm = refs[s], refs[ns + s]
            nwin = x3s[s].shape[1] // SC_WINDOW
            for c in range(C):
                def body(x_vmem, i_vmem, c=c):
                    for k in range(K):
                        pltpu.sync_copy(x_vmem, o_hbm.at[c].at[i_vmem.at[k]])

                pltpu.emit_pipeline(
                    body, grid=(nwin,),
                    in_specs=[pl.BlockSpec((SC_WINDOW, L), lambda i, c=c, nwin=nwin: (c * nwin + i, 0)),
                              pl.BlockSpec((K, SC_WINDOW), lambda i: (0, i))],
                    out_specs=[], core_axis_name=("core", "subcore"), dimension_semantics=(pltpu.PARALLEL,),
                )(x_hbm, i_hbm)

    return scatter(*[x.reshape(-1, L) for x in x3s], *pos_ts)


def _sc_gather_rows(table3, idxs):
    C, _, L = table3.shape
    ns = len(idxs)
    out_type = [jax.ShapeDtypeStruct((C * i.shape[0], L), table3.dtype) for i in idxs]

    @functools.partial(pl.kernel, out_type=out_type, mesh=_sc_mesh(), scratch_types=[])
    def gather(t_hbm, *refs):
        for s in range(ns):
            i_hbm, o_hbm = refs[s], refs[ns + s]
            nwin = idxs[s].shape[0] // SC_WINDOW
            for c in range(C):
                def body(i_vmem, o_vmem, c=c):
                    pltpu.sync_copy(t_hbm.at[c].at[i_vmem.at[0]], o_vmem)

                pltpu.emit_pipeline(
                    body, grid=(nwin,),
                    in_specs=[pl.BlockSpec((1, SC_WINDOW), lambda i: (0, i))],
                    out_specs=[pl.BlockSpec((SC_WINDOW, L), lambda i, c=c, nwin=nwin: (c * nwin + i, 0))],
                    core_axis_name=("core", "subcore"), dimension_semantics=(pltpu.PARALLEL,),
                )(i_hbm, o_hbm)

    outs = gather(table3, *[i.reshape(1, -1) for i in idxs])
    return [o.reshape(C, -1, L) for o in outs]


def _combine_kernel(x2_ref, g_ref, gate_ref, *rest):
    o_ref = rest[-1]
    gate = gate_ref[...]
    half = D_MODEL // 2
    for c in range(PK_CHUNKS):
        lo_s = slice(c * LANES, (c + 1) * LANES)
        hi_s = slice(half + c * LANES, half + (c + 1) * LANES)
        acc_lo = x2_ref[:, lo_s]
        acc_hi = x2_ref[:, hi_s]
        for k in range(TOP_K):
            w = g_ref[c, k]
            gk = gate[:, k:k + 1]
            acc_lo = acc_lo + pltpu.bitcast(w << 16, F32) * gk
            acc_hi = acc_hi + pltpu.bitcast(w & jnp.uint32(0xFFFF0000), F32) * gk
        o_ref[:, lo_s] = acc_lo
        o_ref[:, hi_s] = acc_hi


def _combine(x2, g4, gate, tm, out_buf=None, row0=0, t_total=None):
    T = x2.shape[0]
    t_total = T if t_total is None else t_total
    blk0 = row0 // tm
    in_specs = [pl.BlockSpec((tm, D_MODEL), lambda i: (i, 0)),
                pl.BlockSpec((PK_CHUNKS, TOP_K, tm, LANES), lambda i: (0, 0, i, 0)),
                pl.BlockSpec((tm, LANES), lambda i: (i, 0))]
    args = [x2, g4, gate]
    aliases = {}
    if out_buf is not None:
        in_specs.append(pl.BlockSpec(memory_space=pl.ANY))
        args.append(out_buf)
        aliases = {3: 0}
    return pl.pallas_call(
        _combine_kernel,
        grid=(T // tm,),
        in_specs=in_specs,
        out_specs=pl.BlockSpec((tm, D_MODEL), lambda i: (i + blk0, 0)),
        out_shape=jax.ShapeDtypeStruct((t_total, D_MODEL), F32),
        input_output_aliases=aliases,
        compiler_params=_cparams(("parallel",)),
        name="moe_combine",
    )(*args)


def _moe(streams, places, ew):
    cnts = [st[3][0, :N_EXPERTS] for st in streams]
    total = sum(cnts)
    padded = (total + MOE_ROWS - 1) // MOE_ROWS * MOE_ROWS
    pad_end = jnp.cumsum(padded)
    pad_start = pad_end - padded
    experts = jnp.arange(N_EXPERTS, dtype=jnp.int32)[None, None, :]
    pos_ts = []
    base = pad_start
    for (h3p, idxr, gate, counts, x2), cnt in zip(streams, cnts):
        onehot = idxr[:, :TOP_K, None] == experts
        start = jnp.sum(jnp.where(onehot, base[None, None, :], 0), axis=-1)
        pos_ts.append((start + idxr[:, TOP_K:2 * TOP_K]).T)
        base = base + cnt
    n_assign = sum(st[0].shape[1] for st in streams) * TOP_K
    nb = -(-n_assign // MOE_ROWS) + N_EXPERTS
    starts = jnp.arange(nb, dtype=jnp.int32) * MOE_ROWS
    block_e = jnp.minimum(jnp.sum((pad_end[None, :] <= starts[:, None]).astype(jnp.int32), axis=1), N_EXPERTS - 1)
    n_used = (pad_end[-1] // MOE_ROWS).astype(jnp.int32).reshape(1)
    ar = jnp.arange(N_EXPERTS, dtype=jnp.int32)
    later = (padded > 0)[None, :] & (ar[None, :] > ar[:, None])
    nxt = jnp.min(jnp.where(later, ar[None, :], N_EXPERTS), axis=1)
    nxt = jnp.where(nxt >= N_EXPERTS, -1, nxt)
    next_e = jnp.sum(jnp.where(block_e[:, None] == ar[None, :], nxt[None, :], 0), axis=1).astype(jnp.int32)
    rows = _sc_scatter_rows([st[0] for st in streams], pos_ts, nb * MOE_ROWS)
    y_rows = _experts(block_e, next_e, n_used, rows, ew['w_gu'], ew['b_gu'], ew['w_down'], ew['b_down'])
    gs = _sc_gather_rows(y_rows, [p.reshape(-1) for p in pos_ts])
    outs = []
    for (h3p, idxr, gate, counts, x2), g, place in zip(streams, gs, places):
        T = x2.shape[0]
        outs.append(_combine(x2, g.reshape(PK_CHUNKS, TOP_K, T, LANES), gate, _pick(T, (512, 256, 128, 64)), *place))
    return outs


def _pad_lanes(v, n=LANES, fill=0.0):
    return jnp.pad(v, (0, n - v.shape[0]), constant_values=fill).reshape(1, n)


def _prep_weights(norm_mix, w_in, conv_w, a_log, dt_bias, gdn_norm, q_a_norm, w_qb, kv_a_norm, w_kvb, q_norm,
                  k_nope_norm, k_rope_norm, w_out, norm_mem, mem_norm, w_mq, w_mk, w_mv, mq_norm, mk_norm, w_mo,
                  norm_ffn, w_router, b_router, w_gu, b_gu, w_down, b_down):
    c = np.cumsum([CONV_DIM, GDN_VW, GDN_HEADS, GDN_HEADS, Q_RANK, KV_RANK])
    w_u, w_z, w_a, w_b, w_cq, w_ckv, w_kpe = [w_in[:, lo:hi] for lo, hi in
                                              zip([0, *c], [*c, w_in.shape[1]])]
    w_s = jnp.concatenate([w_kpe, w_a, w_b], axis=1)
    w_s = jnp.pad(w_s, ((0, 0), (0, LANES - w_s.shape[1])))
    wq = w_qb.reshape(Q_RANK, MLA_HEADS, QK_DIM)
    wq = jnp.pad(wq, ((0, 0), (0, 0), (0, Q_SLAB - QK_DIM))).reshape(Q_RANK, MLA_HEADS * Q_SLAB)
    wkv = w_kvb.reshape(KV_RANK, MLA_HEADS, NOPE_DIM + V_DIM)
    w_k = wkv[:, :, :NOPE_DIM].reshape(KV_RANK, -1).astype(BF16)
    w_v = wkv[:, :, NOPE_DIM:].reshape(KV_RANK, -1).astype(BF16)
    wr = jnp.pad(w_router, ((0, 0), (0, LANES - N_EXPERTS)))
    wr_hi = wr.astype(BF16)
    wr_lo = (wr - wr_hi.astype(F32)).astype(BF16)
    row = lambda v: v.reshape(1, -1)
    gpad = ROPE_DIM
    pw = dict(
        nmix=row(norm_mix), w_u=w_u.astype(BF16), w_z=w_z.astype(BF16), w_cq=w_cq.astype(BF16),
        w_ckv=w_ckv.astype(BF16), w_s=w_s.astype(BF16), qan=row(q_a_norm), w_qb=wq.astype(BF16),
        qn=_pad_lanes(q_norm, Q_SLAB), kvan=row(kv_a_norm), krn=_pad_lanes(k_rope_norm),
        alog=jnp.pad(a_log, (gpad, LANES - gpad - GDN_HEADS)).reshape(1, LANES),
        dtb=jnp.pad(dt_bias, (gpad, LANES - gpad - GDN_HEADS)).reshape(1, LANES),
        conv_w=conv_w, gnorm=row(gdn_norm), w_k=w_k, w_v=w_v, w_vt=w_v.T, knn=row(k_nope_norm),
        w_out=w_out.astype(BF16), nmem=row(norm_mem), w_mq=w_mq.astype(BF16), mqn=row(mq_norm),
        w_mo=w_mo.astype(BF16), nffn=row(norm_ffn), wr_hi=wr_hi, wr_lo=wr_lo,
        b_r=_pad_lanes(b_router, LANES, NEG_BIG),
        mnorm=row(mem_norm), w_mk=w_mk.astype(BF16), w_mv=w_mv.astype(BF16), mkn=row(mk_norm),
    )
    ew = dict(w_gu=w_gu, b_gu=b_gu.reshape(N_EXPERTS, 1, 2 * D_FF), w_down=w_down,
              b_down=b_down.reshape(N_EXPERTS, 1, D_MODEL))
    return pw, ew


def _rope_tables(P, S):
    half = ROPE_DIM // 2
    inv = ROPE_THETA ** (-jnp.arange(half, dtype=F32) / half)
    ang = (P + jnp.arange(S, dtype=jnp.int32)).astype(F32)[:, None] * inv[None, :]
    cos, sin = jnp.cos(ang), jnp.sin(ang)
    zh = jnp.zeros((S, half), F32)
    zz = jnp.zeros((S, LANES - ROPE_DIM), F32)
    return (jnp.concatenate([cos, cos, zz], 1), jnp.concatenate([-sin, zh, zz], 1),
            jnp.concatenate([zh, sin, zz], 1))


def _pick(n, prefs):
    for t in prefs:
        if n % t == 0:
            return t
    return n


def _trunk_front(x, lat_past, kpe_past, s0, conv_past, mem_k, mem_v, pw, n_groups):
    B, S, D = x.shape
    P = lat_past.shape[1]
    T = B * S
    tm = _pick(S, (512, 256, 128, 64))
    tm_in = 512 if (T % 512 == 0 and (512 % S == 0 or S % 512 == 0)) else tm
    u, z, q, lat_new, small = _inproj(x.reshape(T, D), S, tm_in, pw, _rope_tables(P, S))

    LC = _pick(S, (256, 128, 64))
    NB = _pick(B, tuple(n for n in (8, 4, 2, 1) if n * (LC // CHUNK) <= GDN_UNITS))
    o_gdn, s_new, conv_new = _gdn(u.reshape(B, S, CONV_DIM), small.reshape(B, S, LANES), z.reshape(B, S, GDN_VW),
                                  pw['conv_w'], conv_past, s0, pw['gnorm'], NB, LC)

    kpe_new = small[:, :ROPE_DIM].reshape(B, S, ROPE_DIM)
    lat3 = lat_new.reshape(B, S, KV_RANK)
    small3 = small.reshape(B, S, LANES)
    q3 = q.reshape(B, S, MLA_HEADS * Q_SLAB)
    tq = _pick(S, (512, 256, 128, 64))
    if P == 0 and S % 512 == 0:
        kn, kr, vt = _kvproj(lat3, small3, pw, 512, True)
        o_mla = _flash_t(q3, kn, kr, vt, P, S, tq, 512)
    elif P > 0 and P % CHUNK == 0 and S == CHUNK:
        past = _kvproj(lat_past, kpe_past, pw, _pick(P, (2048, 1024, 512)), False)
        new = _kvproj(lat3, small3, pw, S, False)
        o_mla = _attn_dec(q3, past, new)
    else:
        sk = P + S
        tk = 512 if S >= 512 else -(-sk // LANES) * LANES
        skp = -(-sk // tk) * tk
        lat_all = jnp.pad(jnp.concatenate([lat_past, lat3], axis=1), ((0, 0), (0, skp - sk), (0, 0)))
        kpe_all = jnp.pad(jnp.concatenate([kpe_past, kpe_new], axis=1), ((0, 0), (0, skp - sk), (0, 0)))
        key_major = tq >= LANES
        kn, kr, v = _kvproj(lat_all, kpe_all, pw, _pick(skp, (512,)), key_major)
        o_mla = (_flash_t if key_major else _flash)(q3, kn, kr, v, P, S, tq, tk)

    streams = []
    gb = B // n_groups
    for gi in range(n_groups):
        x2, h3p, idxr, gate, counts = _mix(x, o_gdn, o_mla, mem_k, mem_v, pw, tm, gi * gb, gb)
        tg = gb * S
        streams.append((h3p, idxr.reshape(tg, LANES), gate.reshape(tg, LANES), counts, x2.reshape(tg, D)))
    return streams, (lat_new.reshape(B, S, KV_RANK), kpe_new, s_new, conv_new)


def kernel(x_prompt, x_sample, cache_kv_latent, cache_k_rope, state_gdn, state_conv, cache_mem_k, cache_mem_v, mem_prompt, norm_mix, w_in, conv_w, a_log, dt_bias, gdn_norm, q_a_norm, w_qb, kv_a_norm, w_kvb, q_norm, k_nope_norm, k_rope_norm, w_out, norm_mem, mem_norm, w_mq, w_mk, w_mv, mq_norm, mk_norm, w_mo, norm_ffn, w_router, b_router, w_gu, b_gu, w_down, b_down):
    depth = norm_mix.shape[0]
    yp, ys = x_prompt, x_sample
    bp = x_prompt.shape[0]
    hw = MEM_HEADS * MEM_HEAD_DIM
    outs = [[] for _ in range(10)]
    for l in range(depth):
        pw, ew = _prep_weights(norm_mix[l], w_in[l], conv_w[l], a_log[l], dt_bias[l], gdn_norm[l], q_a_norm[l],
                               w_qb[l], kv_a_norm[l], w_kvb[l], q_norm[l], k_nope_norm[l], k_rope_norm[l], w_out[l],
                               norm_mem[l], mem_norm[l], w_mq[l], w_mk[l], w_mv[l], mq_norm[l], mk_norm[l], w_mo[l],
                               norm_ffn[l], w_router[l], b_router[l], w_gu[l], b_gu[l], w_down[l], b_down[l])
        nm = mem_prompt.shape[1]
        mk, mv = _memkv(mem_prompt.reshape(bp * nm, D_MODEL), pw['mnorm'], pw['w_mk'], pw['w_mv'], pw['mkn'],
                        _pick(bp * nm, (512, 256)))
        mk = mk.reshape(bp, nm, hw)
        mv = mv.reshape(bp, nm, hw)
        n_groups = 2 if bp % 2 == 0 else 1
        streams_p, (lat, kpe, s_fin, cv) = _trunk_front(
            yp, jnp.zeros((bp, 0, KV_RANK), F32), jnp.zeros((bp, 0, ROPE_DIM), F32),
            jnp.zeros((bp, GDN_HEADS, GDN_DK, GDN_DV), F32), jnp.zeros((bp, CONV_W - 1, CONV_DIM), F32), mk, mv, pw,
            n_groups)
        bs = x_sample.shape[0]
        (stream_s,), (lat2, kpe2, s_fin2, cv2) = _trunk_front(
            ys, cache_kv_latent[l], cache_k_rope[l], state_gdn[l], state_conv[l],
            cache_mem_k[l], cache_mem_v[l], pw, 1)
        tp = yp.shape[0] * yp.shape[1]
        tg = tp // n_groups
        ybuf = None
        for gi, st in enumerate(streams_p):
            last = gi == n_groups - 1
            res = _moe([st] + ([stream_s] if last else []),
                       [(ybuf, gi * tg, tp)] + ([(None, 0, None)] if last else []), ew)
            ybuf = res[0]
        yp, ys = ybuf.reshape(yp.shape), res[1].reshape(ys.shape)
        for lst, val in zip(outs, (lat, kpe, s_fin, cv, mk.reshape(bp, nm, MEM_HEADS, MEM_HEAD_DIM),
                                   mv.reshape(bp, nm, MEM_HEADS, MEM_HEAD_DIM), lat2, kpe2, s_fin2, cv2)):
            lst.append(val)
    return (yp, ys) + tuple(jnp.stack(o) for o in outs)
```

```python
import functools
import math

import numpy as np
import jax
import jax.numpy as jnp
from jax import lax
from jax.experimental import pallas as pl
from jax.experimental.pallas import tpu as pltpu
from jax.experimental.pallas import tpu_sc as plsc

F32 = jnp.float32
BF16 = jnp.bfloat16

D_MODEL = 1024
CHUNK = 64
EPS = 1e-6
GDN_HEADS = 4
GDN_DK = 128
GDN_DV = 128
CONV_W = 4
GDN_QK = GDN_HEADS * GDN_DK
GDN_VW = GDN_HEADS * GDN_DV
CONV_DIM = 2 * GDN_QK + GDN_VW
MLA_HEADS = 4
Q_RANK = 384
KV_RANK = 256
NOPE_DIM = 128
ROPE_DIM = 64
V_DIM = 128
QK_DIM = NOPE_DIM + ROPE_DIM
ROPE_THETA = 10000.0
N_MEM = 256
MEM_HEADS = 4
MEM_HEAD_DIM = 128
N_EXPERTS = 32
TOP_K = 4
D_FF = D_MODEL
SWIGLU_ALPHA = 1.702
SWIGLU_LIMIT = 7.0

LANES = 128
Q_SLAB = 2 * LANES
NEG_BIG = -1e30
VMEM_LIMIT = 56 * 1024 * 1024
MOE_ROWS = 512
GDN_UNITS = 8
FF_CHUNK = 256
SC_WINDOW = 128
PK_CHUNKS = D_MODEL // 2 // LANES
SUBTILE_ROWS = 256
ONES_ROWS = 16
MIX_ROWS = 512


def _cparams(sem):
    return pltpu.CompilerParams(dimension_semantics=sem, vmem_limit_bytes=VMEM_LIMIT)


def _dot(a, b):
    return jnp.dot(a, b, preferred_element_type=F32)


def _dot_nt(a, b):
    return lax.dot_general(a, b, (((1,), (1,)), ((), ())), preferred_element_type=F32)


def _dot_tn(a, b):
    return lax.dot_general(a, b, (((0,), (0,)), ((), ())), preferred_element_type=F32)


def _rms(x, gain, n=None):
    n = x.shape[-1] if n is None else n
    ss = jnp.sum(x * x, axis=-1, keepdims=True) * (1.0 / n)
    return (x * lax.rsqrt(ss + EPS)) * gain


def _sigmoid(x):
    return 1.0 / (1.0 + jnp.exp(-x))


def _rope128(r, cos, sna, snb):
    return r * cos + pltpu.roll(r, 96, 1) * sna + pltpu.roll(r, 32, 1) * snb


def _pack_bf16_pairs(x):
    n = x.shape[1] // 2
    lo = pltpu.bitcast(x[:, :n], jnp.uint32) >> 16
    hi = pltpu.bitcast(x[:, n:], jnp.uint32) & jnp.uint32(0xFFFF0000)
    return lo | hi


def _unpack_bf16_pairs(p):
    lo = pltpu.bitcast(p << 16, F32)
    hi = pltpu.bitcast(p & jnp.uint32(0xFFFF0000), F32)
    return jnp.concatenate([lo, hi], axis=1)


def _subtiles(rows):
    n = rows // SUBTILE_ROWS if rows % SUBTILE_ROWS == 0 else 1
    step = rows // n
    return [slice(i * step, (i + 1) * step) for i in range(n)]


def _const_spec(shape):
    nd = len(shape)
    return pl.BlockSpec(shape, lambda *_: (0,) * nd)


def _inproj_kernel(x_ref, nmix_ref, wu_ref, wz_ref, wcq_ref, wckv_ref, ws_ref, qan_ref, wqb_ref, qn_ref,
                   kvan_ref, krn_ref, alog_ref, dtb_ref, cos_ref, sna_ref, snb_ref,
                   u_ref, z_ref, q_ref, lat_ref, small_ref):
    for rs in _subtiles(x_ref.shape[0]):
        x = x_ref[rs, :]
        hb = _rms(x, nmix_ref[...]).astype(BF16)
        u_ref[rs, :] = _dot(hb, wu_ref[...])
        z_ref[rs, :] = _dot(hb, wz_ref[...])
        cq_raw = _dot(hb, wcq_ref[...])
        ckv_raw = _dot(hb, wckv_ref[...])
        sm = _dot(hb, ws_ref[...])
        cos, sna, snb = cos_ref[rs, :], sna_ref[rs, :], snb_ref[rs, :]

        cq = _rms(cq_raw, qan_ref[...]).astype(BF16)
        qf = _dot(cq, wqb_ref[...])
        scale = QK_DIM ** -0.5
        for h in range(MLA_HEADS):
            slab = qf[:, h * Q_SLAB:(h + 1) * Q_SLAB]
            slab = _rms(slab, qn_ref[...], n=QK_DIM)
            nope = slab[:, :LANES]
            ropd = _rope128(slab[:, LANES:], cos, sna, snb)
            q_ref[rs, h * Q_SLAB:h * Q_SLAB + LANES] = (nope * scale).astype(BF16)
            q_ref[rs, h * Q_SLAB + LANES:(h + 1) * Q_SLAB] = (ropd * scale).astype(BF16)

        lat_ref[rs, :] = _rms(ckv_raw, kvan_ref[...])

        lane = lax.broadcasted_iota(jnp.int32, sm.shape, 1)
        kp = jnp.where(lane < ROPE_DIM, sm, 0.0)
        kpe = _rope128(_rms(kp, krn_ref[...], n=ROPE_DIM), cos, sna, snb)
        sp = sm + dtb_ref[...]
        softplus = jnp.maximum(sp, 0.0) + jnp.log1p(jnp.exp(-jnp.abs(sp)))
        g = -jnp.exp(alog_ref[...]) * softplus
        beta = _sigmoid(sm)
        small_ref[rs, :] = jnp.where(lane < ROPE_DIM, kpe,
                                     jnp.where(lane < ROPE_DIM + GDN_HEADS, g,
                                               jnp.where(lane < ROPE_DIM + 2 * GDN_HEADS, beta, 0.0)))


def _inproj(x2d, S, tm, pw, tabs):
    T = x2d.shape[0]
    if tm > S:
        tabs = [jnp.tile(t, (tm // S, 1)) for t in tabs]
    nblk_s = max(S // tm, 1)
    row = lambda n: pl.BlockSpec((tm, n), lambda i: (i, 0))
    tab = pl.BlockSpec((tm, LANES), lambda i: (i % nblk_s, 0))
    consts = [pw['nmix'], pw['w_u'], pw['w_z'], pw['w_cq'], pw['w_ckv'], pw['w_s'], pw['qan'], pw['w_qb'],
              pw['qn'], pw['kvan'], pw['krn'], pw['alog'], pw['dtb']]
    return pl.pallas_call(
        _inproj_kernel,
        grid=(T // tm,),
        in_specs=[row(D_MODEL)] + [_const_spec(c.shape) for c in consts] + [tab, tab, tab],
        out_specs=[row(CONV_DIM), row(GDN_VW), row(MLA_HEADS * Q_SLAB), row(KV_RANK), row(LANES)],
        out_shape=[jax.ShapeDtypeStruct((T, CONV_DIM), F32), jax.ShapeDtypeStruct((T, GDN_VW), F32),
                   jax.ShapeDtypeStruct((T, MLA_HEADS * Q_SLAB), BF16), jax.ShapeDtypeStruct((T, KV_RANK), F32),
                   jax.ShapeDtypeStruct((T, LANES), F32)],
        compiler_params=_cparams(("parallel",)),
        name="inproj",
    )(x2d, *consts, *tabs)


def _split3(x):
    hi = x.astype(BF16)
    r = x - hi.astype(F32)
    mid = r.astype(BF16)
    lo = (r - mid.astype(F32)).astype(BF16)
    return hi, mid, lo


def _gdn_kernel(u_ref, small_ref, z_ref, convw_ref, cpast_ref, s0_ref, gnorm_ref,
                to_ref, trilm_ref, strictm_ref, same2_ref, lvl_ref,
                o_ref, sfin_ref, cnew_ref, ext_ref, uc_ref, state_ref,
                qf_ref, kf_ref, vb_ref, bt_ref, gcum_ref, glast_ref, kk_ref, qk_ref, mb_ref, x_ref, qkb_ref,
                kbe_ref, qg_ref, kdec_ref, egl_ref, t1_ref, uu_ref, ww_ref, vn_ref, qs_ref, *, NB, LC):
    j = pl.program_id(1)
    nj = pl.num_programs(1)
    PADR = 8
    C = LC // CHUNK
    U = NB * C
    HR = GDN_HEADS * CHUNK

    @pl.when(j == 0)
    def _():
        state_ref[...] = s0_ref[...]
        ext_ref[:, PADR - (CONV_W - 1):PADR, :] = cpast_ref[...]

    w = convw_ref[...]
    for nb in range(NB):
        ext_ref[nb, PADR:PADR + LC, :] = u_ref[nb]
        acc = ext_ref[nb, PADR:PADR + LC, :] * w[CONV_W - 1:CONV_W, :]
        for t in range(1, CONV_W):
            acc = acc + ext_ref[nb, PADR - t:PADR - t + LC, :] * w[CONV_W - 1 - t:CONV_W - t, :]
        uc_ref[nb] = acc * _sigmoid(acc)
        ext_ref[nb, 0:PADR, :] = ext_ref[nb, LC:LC + PADR, :]

    @pl.when(j == nj - 1)
    def _():
        cnew_ref[...] = ext_ref[:, PADR - (CONV_W - 1):PADR, :]

    units = [(nb, c) for nb in range(NB) for c in range(C)]
    g0 = ROPE_DIM
    b0 = ROPE_DIM + GDN_HEADS
    to = to_ref[...]
    for u, (nb, c) in enumerate(units):
        rows = slice(c * CHUNK, (c + 1) * CHUNK)
        sm = small_ref[nb, rows, :]
        gl = sum(_dot(to, part) for part in _split3(sm))
        for h in range(GDN_HEADS):
            hr = slice(h * CHUNK, (h + 1) * CHUNK)
            q = uc_ref[nb, rows, h * GDN_DK:(h + 1) * GDN_DK]
            k = uc_ref[nb, rows, GDN_QK + h * GDN_DK:GDN_QK + (h + 1) * GDN_DK]
            v = uc_ref[nb, rows, 2 * GDN_QK + h * GDN_DV:2 * GDN_QK + (h + 1) * GDN_DV]
            beta = jnp.broadcast_to(sm[:, b0 + h:b0 + h + 1], (CHUNK, LANES))
            qf_ref[u, hr, :] = (q * lax.rsqrt(jnp.sum(q * q, -1, keepdims=True) + EPS)) * (GDN_DK ** -0.5)
            kf_ref[u, hr, :] = k * lax.rsqrt(jnp.sum(k * k, -1, keepdims=True) + EPS)
            vb_ref[u, hr, :] = (v * beta).astype(BF16)
            bt_ref[u, hr, :] = beta
            gcum_ref[u, hr, :] = jnp.broadcast_to(gl[:CHUNK, g0 + h:g0 + h + 1], (CHUNK, LANES))
            glast_ref[u, hr, :] = jnp.broadcast_to(gl[CHUNK:, g0 + h:g0 + h + 1], (CHUNK, LANES))

    for u in range(U):
        k = kf_ref[u]
        kbf = k.astype(BF16)
        kk_ref[u] = _dot_nt((k * bt_ref[u]).astype(BF16), kbf)
        qk_ref[u] = _dot_nt(qf_ref[u].astype(BF16), kbf)

    trilm = trilm_ref[...]
    eye = trilm - strictm_ref[...]
    for u in range(U):
        gcum = gcum_ref[u]
        grow = gcum.T[0:1, :]
        gcol = jnp.concatenate([gcum, gcum], axis=1)
        decay = jnp.exp(jnp.minimum(gcol - grow, 0.0)) * trilm
        m = kk_ref[u] * (decay * strictm_ref[...])
        mb_ref[u] = m.astype(BF16)
        x_ref[u] = eye - m * same2_ref[...]
        qkb_ref[u] = (qk_ref[u] * decay).astype(BF16)
        egc = jnp.exp(gcum)
        k = kf_ref[u]
        kbe_ref[u] = (k * bt_ref[u] * egc).astype(BF16)
        qg_ref[u] = (qf_ref[u] * egc).astype(BF16)
        kdec_ref[u] = (k * jnp.exp(glast_ref[u] - gcum)).astype(BF16)
        egl_ref[u] = jnp.exp(glast_ref[u])

    for lvl in range(lvl_ref.shape[0]):
        lm = lvl_ref[lvl]
        for u in range(U):
            t1_ref[u] = _dot(mb_ref[u] * lm, x_ref[u].astype(BF16)).astype(BF16)
        for u in range(U):
            x = x_ref[u]
            x_ref[u] = x - _dot(x.astype(BF16), t1_ref[u])

    for u in range(U):
        xb = x_ref[u].astype(BF16)
        uu_ref[u] = _dot(xb, vb_ref[u])
        ww_ref[u] = _dot(xb, kbe_ref[u]).astype(BF16)

    gnorm = gnorm_ref[...]
    hrs = [slice(h * CHUNK, (h + 1) * CHUNK) for h in range(GDN_HEADS)]
    for c in range(C):
        rows = slice(c * CHUNK, (c + 1) * CHUNK)
        us = [nb * C + c for nb in range(NB)]
        for nb, u in enumerate(us):
            for h, hr in enumerate(hrs):
                stb = state_ref[nb, h].astype(BF16)
                r = _dot(jnp.concatenate([ww_ref[u, hr, :], qg_ref[u, hr, :]], axis=0), stb)
                vn_ref[u, hr, :] = (uu_ref[u, hr, :] - r[:CHUNK]).astype(BF16)
                qs_ref[u, hr, :] = r[CHUNK:]
        outs = [qs_ref[u] + _dot(qkb_ref[u], vn_ref[u]) for u in us]
        for nb, u in enumerate(us):
            for h, hr in enumerate(hrs):
                state_ref[nb, h] = (state_ref[nb, h] * egl_ref[u, h * CHUNK:h * CHUNK + 1, :]
                                    + _dot_tn(kdec_ref[u, hr, :], vn_ref[u, hr, :]))
        for nb, u in enumerate(us):
            for h, hr in enumerate(hrs):
                zz = z_ref[nb, rows, h * GDN_DV:(h + 1) * GDN_DV]
                og = _rms(outs[nb][hr, :], gnorm) * (zz * _sigmoid(zz))
                o_ref[nb, rows, h * GDN_DV:(h + 1) * GDN_DV] = og.astype(BF16)

    @pl.when(j == nj - 1)
    def _():
        sfin_ref[...] = state_ref[...]


def _gdn_masks():
    hr = GDN_HEADS * CHUNK
    i = np.arange(hr)[:, None]
    j = np.arange(hr)[None, :]
    same_head = (i // CHUNK) == (j // CHUNK)
    tril = same_head & (i >= j)
    strict = same_head & (i > j)
    same2 = strict & ((i // 2) == (j // 2))
    lvls = []
    blk = 2
    while blk < CHUNK:
        lvls.append(strict & ((i // (2 * blk)) == (j // (2 * blk))) & ((i // blk) != (j // blk)))
        blk *= 2
    fr = np.arange(CHUNK)
    to = np.concatenate([fr[:, None] >= fr[None, :], np.ones((CHUNK, CHUNK), bool)], axis=0)
    f = lambda a: jnp.asarray(a.astype(np.float32))
    return (jnp.asarray(to.astype(np.float32), dtype=BF16), f(tril), f(strict), f(same2),
            jnp.asarray(np.stack(lvls).astype(np.float32), dtype=BF16))


def _gdn(u3, small3, z3, conv_w, conv_past, s0, gnorm, NB, LC):
    B, S, _ = u3.shape
    C = LC // CHUNK
    U = NB * C
    HR = GDN_HEADS * CHUNK
    masks = _gdn_masks()
    tile = lambda n: pl.BlockSpec((NB, LC, n), lambda b, j: (b, j, 0))
    stspec = pl.BlockSpec((NB, GDN_HEADS, GDN_DK, GDN_DV), lambda b, j: (b, 0, 0, 0))
    cvspec = pl.BlockSpec((NB, CONV_W - 1, CONV_DIM), lambda b, j: (b, 0, 0))
    vm = lambda shape, dt: pltpu.VMEM(shape, dt)
    return pl.pallas_call(
        functools.partial(_gdn_kernel, NB=NB, LC=LC),
        grid=(B // NB, S // LC),
        in_specs=[tile(CONV_DIM), tile(LANES), tile(GDN_VW), _const_spec(conv_w.shape), cvspec, stspec,
                  _const_spec(gnorm.shape)] + [_const_spec(m.shape) for m in masks],
        out_specs=[tile(GDN_VW), stspec, cvspec],
        out_shape=[jax.ShapeDtypeStruct((B, S, GDN_VW), BF16),
                   jax.ShapeDtypeStruct((B, GDN_HEADS, GDN_DK, GDN_DV), F32),
                   jax.ShapeDtypeStruct((B, CONV_W - 1, CONV_DIM), F32)],
        scratch_shapes=[vm((NB, LC + 8, CONV_DIM), F32), vm((NB, LC, CONV_DIM), F32),
                        vm((NB, GDN_HEADS, GDN_DK, GDN_DV), F32),
                        vm((U, HR, LANES), F32), vm((U, HR, LANES), F32), vm((U, HR, LANES), BF16),
                        vm((U, HR, LANES), F32),
                        vm((U, HR, LANES), F32), vm((U, HR, LANES), F32),
                        vm((U, HR, HR), F32), vm((U, HR, HR), F32),
                        vm((U, HR, HR), BF16), vm((U, HR, HR), F32), vm((U, HR, HR), BF16),
                        vm((U, HR, LANES), BF16), vm((U, HR, LANES), BF16), vm((U, HR, LANES), BF16),
                        vm((U, HR, LANES), F32), vm((U, HR, HR), BF16),
                        vm((U, HR, LANES), F32), vm((U, HR, LANES), BF16),
                        vm((U, HR, LANES), BF16), vm((U, HR, LANES), F32)],
        compiler_params=_cparams(("parallel", "arbitrary")),
        name="gdn",
    )(u3, small3, z3, conv_w, conv_past, s0, gnorm, *masks)


def _kvproj_kernel(lat_ref, kpe_ref, wk_ref, wv_ref, knn_ref, kn_ref, kr_ref, v_ref, *, v_transposed):
    lb = lat_ref[...].astype(BF16)
    kf = _dot(lb, wk_ref[...])
    for h in range(MLA_HEADS):
        sl = slice(h * NOPE_DIM, (h + 1) * NOPE_DIM)
        kn_ref[:, sl] = _rms(kf[:, sl], knn_ref[...]).astype(BF16)
    if v_transposed:
        v_ref[...] = _dot_nt(wv_ref[...], lb).astype(BF16)
    else:
        v_ref[...] = _dot(lb, wv_ref[...]).astype(BF16)
    kp = kpe_ref[...]
    if kp.shape[1] == ROPE_DIM:
        kr_ref[...] = jnp.concatenate([kp, jnp.zeros((kp.shape[0], LANES - ROPE_DIM), F32)], axis=1).astype(BF16)
    else:
        lane = lax.broadcasted_iota(jnp.int32, kp.shape, 1)
        kr_ref[...] = jnp.where(lane < ROPE_DIM, kp, 0.0).astype(BF16)


def _kvproj(lat3, kpe3, pw, tm, v_transposed):
    B, sk, _ = lat3.shape
    hw = MLA_HEADS * V_DIM
    row = lambda n: pl.BlockSpec((None, tm, n), lambda b, i: (b, i, 0))
    if v_transposed:
        vspec, vshape, w_v = pl.BlockSpec((None, hw, tm), lambda b, i: (b, 0, i)), (B, hw, sk), pw['w_vt']
    else:
        vspec, vshape, w_v = row(hw), (B, sk, hw), pw['w_v']
    return pl.pallas_call(
        functools.partial(_kvproj_kernel, v_transposed=v_transposed),
        grid=(B, sk // tm),
        in_specs=[row(KV_RANK), row(kpe3.shape[-1]), _const_spec(pw['w_k'].shape), _const_spec(w_v.shape),
                  _const_spec(pw['knn'].shape)],
        out_specs=[row(MLA_HEADS * NOPE_DIM), row(LANES), vspec],
        out_shape=[jax.ShapeDtypeStruct((B, sk, MLA_HEADS * NOPE_DIM), BF16),
                   jax.ShapeDtypeStruct((B, sk, LANES), BF16), jax.ShapeDtypeStruct(vshape, BF16)],
        compiler_params=_cparams(("parallel", "parallel")),
        name="kvproj",
    )(lat3, kpe3, pw['w_k'], w_v, pw['knn'])


def _last_kblock(qi, tq, tk, P, nk):
    last_key = ((P + qi * tq + tq - 1) // CHUNK) * CHUNK + CHUNK - 1
    return jnp.minimum(last_key // tk, nk - 1)


def _flash_kernel(q_ref, kn_ref, kr_ref, v_ref, o_ref, m_ref, l_ref, acc_ref, *, tq, tk, P, S, nk):
    qi = pl.program_id(1)
    ki = pl.program_id(2)

    @pl.when(ki == 0)
    def _():
        m_ref[...] = jnp.full(m_ref.shape, NEG_BIG, F32)
        l_ref[...] = jnp.zeros(l_ref.shape, F32)
        acc_ref[...] = jnp.zeros(acc_ref.shape, F32)

    @pl.when(ki <= _last_kblock(qi, tq, tk, P, nk))
    def _():
        qpos = P + qi * tq + lax.broadcasted_iota(jnp.int32, (tq, tk), 0)
        kpos = ki * tk + lax.broadcasted_iota(jnp.int32, (tq, tk), 1)
        mask = ((kpos // CHUNK) <= (qpos // CHUNK)) & (kpos < P + S)
        kr = kr_ref[...]
        for h in range(MLA_HEADS):
            qh = q_ref[:, h * Q_SLAB:(h + 1) * Q_SLAB]
            kh = jnp.concatenate([kn_ref[:, h * NOPE_DIM:(h + 1) * NOPE_DIM], kr], axis=1)
            s = jnp.where(mask, _dot_nt(qh, kh), NEG_BIG)
            m_prev = m_ref[h]
            m_new = jnp.maximum(m_prev, jnp.max(s, axis=-1, keepdims=True))
            alpha = jnp.exp(m_prev - m_new)
            p = jnp.exp(s - m_new)
            l_ref[h] = alpha * l_ref[h] + jnp.sum(p, axis=-1, keepdims=True)
            acc_ref[h] = alpha * acc_ref[h] + _dot(p.astype(BF16), v_ref[:, h * V_DIM:(h + 1) * V_DIM])
            m_ref[h] = m_new

    @pl.when(ki == nk - 1)
    def _():
        for h in range(MLA_HEADS):
            o_ref[:, h * V_DIM:(h + 1) * V_DIM] = (acc_ref[h] / l_ref[h]).astype(BF16)


def _flash(q3, kn3, kr3, v3, P, S, tq, tk):
    B = q3.shape[0]
    skp = kn3.shape[1]
    nk = skp // tk
    kmap = lambda b, qi, ki: (b, jnp.minimum(ki, _last_kblock(qi, tq, tk, P, nk)), 0)
    return pl.pallas_call(
        functools.partial(_flash_kernel, tq=tq, tk=tk, P=P, S=S, nk=nk),
        grid=(B, S // tq, nk),
        in_specs=[pl.BlockSpec((None, tq, MLA_HEADS * Q_SLAB), lambda b, qi, ki: (b, qi, 0)),
                  pl.BlockSpec((None, tk, MLA_HEADS * NOPE_DIM), kmap),
                  pl.BlockSpec((None, tk, LANES), kmap),
                  pl.BlockSpec((None, tk, MLA_HEADS * V_DIM), kmap)],
        out_specs=pl.BlockSpec((None, tq, MLA_HEADS * V_DIM), lambda b, qi, ki: (b, qi, 0)),
        out_shape=jax.ShapeDtypeStruct((B, S, MLA_HEADS * V_DIM), BF16),
        scratch_shapes=[pltpu.VMEM((MLA_HEADS, tq, 1), F32), pltpu.VMEM((MLA_HEADS, tq, 1), F32),
                        pltpu.VMEM((MLA_HEADS, tq, V_DIM), F32)],
        compiler_params=_cparams(("parallel", "parallel", "arbitrary")),
        name="mla_attn",
    )(q3, kn3, kr3, v3)


def _attn_dec_kernel(q_ref, knp_ref, krp_ref, vp_ref, knn_ref, krn_ref, vn_ref, o_ref):
    krp, krn = krp_ref[...], krn_ref[...]
    hsl = [slice(h * NOPE_DIM, (h + 1) * NOPE_DIM) for h in range(MLA_HEADS)]
    qs = [q_ref[:, h * Q_SLAB:(h + 1) * Q_SLAB] for h in range(MLA_HEADS)]
    sps = [_dot_nt(q, jnp.concatenate([knp_ref[:, sl], krp], axis=1)) for q, sl in zip(qs, hsl)]
    sns = [_dot_nt(q, jnp.concatenate([knn_ref[:, sl], krn], axis=1)) for q, sl in zip(qs, hsl)]
    for h, (sp, sn) in enumerate(zip(sps, sns)):
        vsl = slice(h * V_DIM, (h + 1) * V_DIM)
        m = jnp.maximum(jnp.max(sp, axis=-1, keepdims=True), jnp.max(sn, axis=-1, keepdims=True))
        pp = jnp.exp(sp - m)
        pn = jnp.exp(sn - m)
        l = jnp.sum(pp, axis=-1, keepdims=True) + jnp.sum(pn, axis=-1, keepdims=True)
        o = _dot(pp.astype(BF16), vp_ref[:, vsl]) + _dot(pn.astype(BF16), vn_ref[:, vsl])
        o_ref[:, vsl] = (o / l).astype(BF16)


def _attn_dec(q3, past, new):
    B, S, _ = q3.shape
    specs = [pl.BlockSpec((None, a.shape[1], a.shape[2]), lambda b: (b, 0, 0)) for a in (q3, *past, *new)]
    return pl.pallas_call(
        _attn_dec_kernel,
        grid=(B,),
        in_specs=specs,
        out_specs=pl.BlockSpec((None, S, MLA_HEADS * V_DIM), lambda b: (b, 0, 0)),
        out_shape=jax.ShapeDtypeStruct((B, S, MLA_HEADS * V_DIM), BF16),
        compiler_params=_cparams(("parallel",)),
        name="mla_attn_dec",
    )(q3, *past, *new)


def _flash_t_kernel(qt_ref, kt_ref, lt_ref, q_ref, kn_ref, kr_ref, vt_ref, o_ref, m_ref, acc_ref, *, tq, tk, P, S):
    j = pl.program_id(1)
    qi = qt_ref[j]
    ki = kt_ref[j]
    q0 = P + qi * tq
    k0 = ki * tk

    @pl.when(ki == 0)
    def _():
        m_ref[...] = jnp.full(m_ref.shape, NEG_BIG, F32)
        acc_ref[...] = jnp.zeros(acc_ref.shape, F32)

    def step(masked):
        kr = kr_ref[...]
        ones = jnp.ones((ONES_ROWS, tk), BF16)
        if masked:
            kpos = k0 + lax.broadcasted_iota(jnp.int32, (tk, 1), 0)
            qpos = q0 + lax.broadcasted_iota(jnp.int32, (1, tq), 1)
            mask = ((kpos // CHUNK) <= (qpos // CHUNK)) & (kpos < P + S)
        sts = []
        for h in range(MLA_HEADS):
            kh = jnp.concatenate([kn_ref[:, h * NOPE_DIM:(h + 1) * NOPE_DIM], kr], axis=1)
            sts.append(_dot_nt(kh, q_ref[:, h * Q_SLAB:(h + 1) * Q_SLAB]))
        ps, alphas = [], []
        for h in range(MLA_HEADS):
            st = sts[h]
            if masked:
                st = jnp.where(mask, st, NEG_BIG)
            m_prev = m_ref[h]
            m_new = jnp.maximum(m_prev, jnp.max(st, axis=0, keepdims=True))
            alphas.append(jnp.exp(m_prev - m_new))
            ps.append(jnp.exp(st - m_new).astype(BF16))
            m_ref[h] = m_new
        for h in range(MLA_HEADS):
            vt1 = jnp.concatenate([vt_ref[h * V_DIM:(h + 1) * V_DIM, :], ones], axis=0)
            acc_ref[h] = alphas[h] * acc_ref[h] + _dot(vt1, ps[h])

    full = ((k0 + tk - 1) // CHUNK <= q0 // CHUNK) & (k0 + tk <= P + S)
    pl.when(full)(functools.partial(step, False))
    pl.when(jnp.logical_not(full))(functools.partial(step, True))

    @pl.when(lt_ref[j] == 1)
    def _():
        for h in range(MLA_HEADS):
            acc = acc_ref[h]
            o_ref[:, h * V_DIM:(h + 1) * V_DIM] = (acc[:V_DIM] / acc[V_DIM:V_DIM + 1]).T.astype(BF16)


def _flash_t(q3, kn3, kr3, vt3, P, S, tq, tk):
    B = q3.shape[0]
    nk = kn3.shape[1] // tk
    pairs = []
    for qi in range(S // tq):
        last = min((((P + qi * tq + tq - 1) // CHUNK) * CHUNK + CHUNK - 1) // tk, nk - 1)
        pairs += [(qi, ki, int(ki == last)) for ki in range(last + 1)]
    qt, kt, lt = (jnp.asarray(np.array(col, np.int32)) for col in zip(*pairs))
    kmap = lambda b, j, qt, kt, lt: (b, kt[j], 0)
    gs = pltpu.PrefetchScalarGridSpec(
        num_scalar_prefetch=3,
        grid=(B, len(pairs)),
        in_specs=[pl.BlockSpec((None, tq, MLA_HEADS * Q_SLAB), lambda b, j, qt, kt, lt: (b, qt[j], 0)),
                  pl.BlockSpec((None, tk, MLA_HEADS * NOPE_DIM), kmap),
                  pl.BlockSpec((None, tk, LANES), kmap),
                  pl.BlockSpec((None, MLA_HEADS * V_DIM, tk), lambda b, j, qt, kt, lt: (b, 0, kt[j]))],
        out_specs=pl.BlockSpec((None, tq, MLA_HEADS * V_DIM), lambda b, j, qt, kt, lt: (b, qt[j], 0)),
        scratch_shapes=[pltpu.VMEM((MLA_HEADS, 1, tq), F32), pltpu.VMEM((MLA_HEADS, V_DIM + ONES_ROWS, tq), F32)],
    )
    return pl.pallas_call(
        functools.partial(_flash_t_kernel, tq=tq, tk=tk, P=P, S=S),
        grid_spec=gs,
        out_shape=jax.ShapeDtypeStruct((B, S, MLA_HEADS * V_DIM), BF16),
        compiler_params=_cparams(("parallel", "arbitrary")),
        name="mla_attn_t",
    )(qt, kt, lt, q3, kn3, kr3, vt3)


def _memkv_kernel(mem_ref, mnorm_ref, wmk_ref, wmv_ref, mkn_ref, k_ref, v_ref):
    mb = _rms(mem_ref[...], mnorm_ref[...]).astype(BF16)
    kf = _dot(mb, wmk_ref[...])
    for h in range(MEM_HEADS):
        sl = slice(h * MEM_HEAD_DIM, (h + 1) * MEM_HEAD_DIM)
        k_ref[:, sl] = _rms(kf[:, sl], mkn_ref[...])
    v_ref[...] = _dot(mb, wmv_ref[...])


def _memkv(mem2d, mnorm, w_mk, w_mv, mkn, tm):
    T = mem2d.shape[0]
    hw = MEM_HEADS * MEM_HEAD_DIM
    row = lambda n: pl.BlockSpec((tm, n), lambda i: (i, 0))
    return pl.pallas_call(
        _memkv_kernel,
        grid=(T // tm,),
        in_specs=[row(D_MODEL), _const_spec(mnorm.shape), _const_spec(w_mk.shape), _const_spec(w_mv.shape),
                  _const_spec(mkn.shape)],
        out_specs=[row(hw), row(hw)],
        out_shape=[jax.ShapeDtypeStruct((T, hw), F32), jax.ShapeDtypeStruct((T, hw), F32)],
        compiler_params=_cparams(("parallel",)),
        name="mem_kv",
    )(mem2d, mnorm, w_mk, w_mv, mkn)


def _mix_kernel(x_ref, og_ref, om_ref, mk_ref, mv_ref, wout_ref, nmem_ref, wmq_ref, mqn_ref, wmo_ref, nffn_ref,
                wrh_ref, wrl_ref, br_ref, x2_ref, h3_ref, idx_ref, gate_ref, counts_ref, cnt_ref):
    def mem_head(ref, b, h):
        if len(ref.shape) == 3:
            return ref[b, :, h * MEM_HEAD_DIM:(h + 1) * MEM_HEAD_DIM].astype(BF16)
        return ref[b, :, h, :].astype(BF16)

    first = (pl.program_id(0) == 0) & (pl.program_id(1) == 0)

    @pl.when(first)
    def _():
        cnt_ref[...] = jnp.zeros(cnt_ref.shape, F32)

    nbm, tm, _ = x_ref.shape
    rows = nbm * tm
    flat = lambda ref: ref[...].reshape(rows, ref.shape[-1])
    if True:
        x1 = flat(x_ref) + _dot(flat(og_ref), wout_ref[0:GDN_VW, :]) + _dot(flat(om_ref), wout_ref[GDN_VW:, :])
        hb = _rms(x1, nmem_ref[...]).astype(BF16)
        qm = _dot(hb, wmq_ref[...])
        per_batch = []
        hsl = [slice(h * MEM_HEAD_DIM, (h + 1) * MEM_HEAD_DIM) for h in range(MEM_HEADS)]
        for b in range(nbm):
            br = slice(b * tm, (b + 1) * tm)
            qhs = [(_rms(qm[br, sl], mqn_ref[...]) * (MEM_HEAD_DIM ** -0.5)).astype(BF16) for sl in hsl]
            ss = [_dot_nt(qh, mem_head(mk_ref, b, h)) for h, qh in enumerate(qhs)]
            ps = []
            for s in ss:
                p = jnp.exp(s - jnp.max(s, axis=-1, keepdims=True))
                ps.append((p / jnp.sum(p, axis=-1, keepdims=True)).astype(BF16))
            heads = [_dot(p, mem_head(mv_ref, b, h)).astype(BF16) for h, p in enumerate(ps)]
            per_batch.append(jnp.concatenate(heads, axis=1))
        om = per_batch[0] if nbm == 1 else jnp.concatenate(per_batch, axis=0)
        x2 = x1 + _dot(om, wmo_ref[...])
        x2_ref[...] = x2.reshape(x2_ref.shape)
        h3 = _rms(x2, nffn_ref[...])
        hi = h3.astype(BF16)
        packed = _pack_bf16_pairs(hi.astype(F32))
        for c in range(h3_ref.shape[0]):
            h3_ref[c] = packed[:, c * LANES:(c + 1) * LANES]
        lo = (h3 - hi.astype(F32)).astype(BF16)
        wrh = wrh_ref[...]
        logits = _dot(hi, wrh) + _dot(lo, wrh) + _dot(hi, wrl_ref[...]) + br_ref[...]

        lane = lax.broadcasted_iota(jnp.int32, logits.shape, 1).astype(F32)
        vals, idxs = [], []
        cur = logits
        for _ in range(TOP_K):
            mx = jnp.max(cur, axis=-1, keepdims=True)
            ix = jnp.min(jnp.where(cur == mx, lane, float(LANES)), axis=-1, keepdims=True)
            vals.append(mx)
            idxs.append(ix)
            cur = jnp.where(lane == ix, -3e38, cur)
        es = [jnp.exp(v - vals[0]) for v in vals]
        den = es[0] + es[1] + es[2] + es[3]

        sel = jnp.zeros(logits.shape, F32)
        for k in range(TOP_K):
            sel = sel + jnp.where(lane == idxs[k], 1.0, 0.0)
        ri = lax.broadcasted_iota(jnp.int32, (rows, rows), 0)
        ci = lax.broadcasted_iota(jnp.int32, (rows, rows), 1)
        before = jnp.where(ri > ci, 1.0, 0.0).astype(BF16)
        excl = _dot(before, sel.astype(BF16)) + cnt_ref[...]
        cnt_ref[...] = cnt_ref[...] + jnp.sum(sel, axis=0, keepdims=True)
        counts_ref[...] = cnt_ref[...].astype(jnp.int32)

        idx_out = jnp.zeros(logits.shape, F32)
        gate_out = jnp.zeros(logits.shape, F32)
        for k in range(TOP_K):
            rank = jnp.sum(jnp.where(lane == idxs[k], excl, 0.0), axis=-1, keepdims=True)
            idx_out = jnp.where(lane == float(k), idxs[k], idx_out)
            idx_out = jnp.where(lane == float(TOP_K + k), rank, idx_out)
            gate_out = jnp.where(lane == float(k), es[k] / den, gate_out)
        idx_ref[...] = idx_out.astype(jnp.int32).reshape(idx_ref.shape)
        gate_ref[...] = gate_out.reshape(gate_ref.shape)


def _mix(x3, og3, om3, mk3, mv3, pw, tm, b0, B):
    S = x3.shape[1]
    hw = MEM_HEADS * MEM_HEAD_DIM
    nbm = _pick(B, tuple(n for n in (8, 4, 2, 1) if n * tm <= MIX_ROWS and b0 % n == 0)) if tm == S else 1
    nsb = S // tm
    boff = b0 // nbm
    tile_in = lambda n: pl.BlockSpec((nbm, tm, n), lambda b, i: (b + boff, i, 0))
    tile = lambda n: pl.BlockSpec((nbm, tm, n), lambda b, i: (b, i, 0))
    if mk3.ndim == 3:
        memspec = pl.BlockSpec((nbm, N_MEM, hw), lambda b, i: (b + boff, 0, 0))
    else:
        memspec = pl.BlockSpec((nbm, N_MEM, MEM_HEADS, MEM_HEAD_DIM), lambda b, i: (b + boff, 0, 0, 0))
    consts = [pw['w_out'], pw['nmem'], pw['w_mq'], pw['mqn'], pw['w_mo'], pw['nffn'], pw['wr_hi'], pw['wr_lo'],
              pw['b_r']]
    return pl.pallas_call(
        _mix_kernel,
        grid=(B // nbm, nsb),
        in_specs=[tile_in(D_MODEL), tile_in(GDN_VW), tile_in(MLA_HEADS * V_DIM), memspec, memspec]
                 + [_const_spec(c.shape) for c in consts],
        out_specs=[tile(D_MODEL), pl.BlockSpec((PK_CHUNKS, nbm * tm, LANES), lambda b, i: (0, b * nsb + i, 0)),
                   tile(LANES), tile(LANES), _const_spec((1, LANES))],
        out_shape=[jax.ShapeDtypeStruct((B, S, D_MODEL), F32),
                   jax.ShapeDtypeStruct((PK_CHUNKS, B * S, LANES), jnp.uint32),
                   jax.ShapeDtypeStruct((B, S, LANES), jnp.int32), jax.ShapeDtypeStruct((B, S, LANES), F32),
                   jax.ShapeDtypeStruct((1, LANES), jnp.int32)],
        scratch_shapes=[pltpu.VMEM((1, LANES), F32)],
        compiler_params=_cparams(("arbitrary", "arbitrary")),
        name="mix_mem_router",
    )(x3, og3, om3, mk3, mv3, *consts)


def _expert_kernel(be_ref, nx_ref, nv_ref, nu_ref, rows_ref, wgu_hbm, bgu_ref, wd_hbm, bd_ref, y_ref,
                   wgus_ref, wds_ref, wgub_ref, wdb_ref, sem_ref):
    i = pl.program_id(0)
    used = i < nu_ref[0]
    new_expert = (i == 0) | (be_ref[i] != be_ref[jnp.maximum(i - 1, 0)])

    def weight_copies(e):
        return (pltpu.make_async_copy(wgu_hbm.at[e], wgus_ref, sem_ref.at[0]),
                pltpu.make_async_copy(wd_hbm.at[e], wds_ref, sem_ref.at[1]))

    @pl.when(i == 0)
    def _():
        for cp in weight_copies(be_ref[0]):
            cp.start()

    @pl.when(used & new_expert)
    def _():
        for cp in weight_copies(be_ref[i]):
            cp.wait()

        def cast(r, carry):
            rs = pl.ds(pl.multiple_of(r * LANES, LANES), LANES)
            wgub_ref[rs, :] = wgus_ref[rs, :].astype(BF16)
            wdb_ref[rs, :] = wds_ref[rs, :].astype(BF16)
            return carry
        lax.fori_loop(0, D_MODEL // LANES, cast, 0)

        @pl.when(nx_ref[i] >= 0)
        def _():
            for cp in weight_copies(nx_ref[i]):
                cp.start()

    def ffn(nrows):
        packed = jnp.concatenate([rows_ref[c, :nrows, :] for c in range(PK_CHUNKS)], axis=1)
        x = _unpack_bf16_pairs(packed).astype(BF16)
        acc = None
        for c in range(D_FF // FF_CHUNK):
            gs_ = slice(c * FF_CHUNK, (c + 1) * FF_CHUNK)
            us_ = slice(D_FF + c * FF_CHUNK, D_FF + (c + 1) * FF_CHUNK)
            gt = jnp.minimum(_dot(x, wgub_ref[:, gs_]) + bgu_ref[:, gs_], SWIGLU_LIMIT)
            up = jnp.clip(_dot(x, wgub_ref[:, us_]) + bgu_ref[:, us_], -SWIGLU_LIMIT, SWIGLU_LIMIT)
            act = gt * _sigmoid(SWIGLU_ALPHA * gt) * (up + 1.0)
            part = _dot(act.astype(BF16), wdb_ref[gs_, :])
            acc = part if acc is None else acc + part
        ypk = _pack_bf16_pairs((acc + bd_ref[...]).astype(BF16).astype(F32))
        for c in range(PK_CHUNKS):
            y_ref[c, :nrows, :] = ypk[:, c * LANES:(c + 1) * LANES]
        if nrows < MOE_ROWS:
            y_ref[:, nrows:, :] = jnp.zeros((PK_CHUNKS, MOE_ROWS - nrows, LANES), y_ref.dtype)

    half_full = nv_ref[i] <= MOE_ROWS // 2
    pl.when(used & jnp.logical_not(half_full))(functools.partial(ffn, MOE_ROWS))
    pl.when(used & half_full)(functools.partial(ffn, MOE_ROWS // 2))

    @pl.when(jnp.logical_not(used))
    def _():
        y_ref[...] = jnp.zeros(y_ref.shape, y_ref.dtype)


def _experts(block_e, next_e, block_rows, n_used, rows, w_gu, b_gu, w_down, b_down):
    n_rows = rows.shape[1]
    nb = n_rows // MOE_ROWS
    gs = pltpu.PrefetchScalarGridSpec(
        num_scalar_prefetch=4,
        grid=(nb,),
        in_specs=[pl.BlockSpec((PK_CHUNKS, MOE_ROWS, LANES), lambda i, be, nx, nv, nu: (0, i, 0)),
                  pl.BlockSpec(memory_space=pl.ANY),
                  pl.BlockSpec((None, 1, 2 * D_FF), lambda i, be, nx, nv, nu: (be[i], 0, 0)),
                  pl.BlockSpec(memory_space=pl.ANY),
                  pl.BlockSpec((None, 1, D_MODEL), lambda i, be, nx, nv, nu: (be[i], 0, 0))],
        out_specs=pl.BlockSpec((PK_CHUNKS, MOE_ROWS, LANES), lambda i, be, nx, nv, nu: (0, i, 0)),
        scratch_shapes=[pltpu.VMEM((D_MODEL, 2 * D_FF), F32), pltpu.VMEM((D_FF, D_MODEL), F32),
                        pltpu.VMEM((D_MODEL, 2 * D_FF), BF16), pltpu.VMEM((D_FF, D_MODEL), BF16),
                        pltpu.SemaphoreType.DMA((2,))],
    )
    return pl.pallas_call(
        _expert_kernel,
        grid_spec=gs,
        out_shape=jax.ShapeDtypeStruct((PK_CHUNKS, n_rows, LANES), jnp.uint32),
        compiler_params=_cparams(("arbitrary",)),
        name="moe_experts",
    )(block_e, next_e, block_rows, n_used, rows, w_gu, b_gu, w_down, b_down)


def _sc_mesh():
    return plsc.VectorSubcoreMesh(core_axis_name="core", subcore_axis_name="subcore")


def _sc_scatter_rows(x3s, pos_ts, n_rows):
    C, _, L = x3s[0].shape
    K = pos_ts[0].shape[0]
    ns = len(x3s)

    @functools.partial(pl.kernel, out_type=jax.ShapeDtypeStruct((C, n_rows, L), x3s[0].dtype), mesh=_sc_mesh(),
                       scratch_types=[])
    def scatter(*refs):
        o_hbm = refs[2 * ns]
        for s in range(ns):
            x_hbm, i_hbm = refs[s], refs[ns + s]
            nwin = x3s[s].shape[1] // SC_WINDOW
            for c in range(C):
                def body(x_vmem, i_vmem, c=c):
                    for k in range(K):
                        pltpu.sync_copy(x_vmem, o_hbm.at[c].at[i_vmem.at[k]])

                pltpu.emit_pipeline(
                    body, grid=(nwin,),
                    in_specs=[pl.BlockSpec((SC_WINDOW, L), lambda i, c=c, nwin=nwin: (c * nwin + i, 0)),
                              pl.BlockSpec((K, SC_WINDOW), lambda i: (0, i))],
                    out_specs=[], core_axis_name=("core", "subcore"), dimension_semantics=(pltpu.PARALLEL,),
                )(x_hbm, i_hbm)

    return scatter(*[x.reshape(-1, L) for x in x3s], *pos_ts)


def _sc_gather_rows(table3, idxs):
    C, _, L = table3.shape
    ns = len(idxs)
    out_type = [jax.ShapeDtypeStruct((C * i.shape[0], L), table3.dtype) for i in idxs]

    @functools.partial(pl.kernel, out_type=out_type, mesh=_sc_mesh(), scratch_types=[])
    def gather(t_hbm, *refs):
        for s in range(ns):
            i_hbm, o_hbm = refs[s], refs[ns + s]
            nwin = idxs[s].shape[0] // SC_WINDOW
            for c in range(C):
                def body(i_vmem, o_vmem, c=c):
                    pltpu.sync_copy(t_hbm.at[c].at[i_vmem.at[0]], o_vmem)

                pltpu.emit_pipeline(
                    body, grid=(nwin,),
                    in_specs=[pl.BlockSpec((1, SC_WINDOW), lambda i: (0, i))],
                    out_specs=[pl.BlockSpec((SC_WINDOW, L), lambda i, c=c, nwin=nwin: (c * nwin + i, 0))],
                    core_axis_name=("core", "subcore"), dimension_semantics=(pltpu.PARALLEL,),
                )(i_hbm, o_hbm)

    outs = gather(table3, *[i.reshape(1, -1) for i in idxs])
    return [o.reshape(C, -1, L) for o in outs]


def _combine_kernel(x2_ref, g_ref, gate_ref, *rest):
    o_ref = rest[-1]
    gate = gate_ref[...]
    half = D_MODEL // 2
    for c in range(PK_CHUNKS):
        lo_s = slice(c * LANES, (c + 1) * LANES)
        hi_s = slice(half + c * LANES, half + (c + 1) * LANES)
        acc_lo = x2_ref[:, lo_s]
        acc_hi = x2_ref[:, hi_s]
        for k in range(TOP_K):
            w = g_ref[c, k]
            gk = gate[:, k:k + 1]
            acc_lo = acc_lo + pltpu.bitcast(w << 16, F32) * gk
            acc_hi = acc_hi + pltpu.bitcast(w & jnp.uint32(0xFFFF0000), F32) * gk
        o_ref[:, lo_s] = acc_lo
        o_ref[:, hi_s] = acc_hi


def _combine(x2, g4, gate, tm, out_buf=None, row0=0, t_total=None):
    T = x2.shape[0]
    t_total = T if t_total is None else t_total
    blk0 = row0 // tm
    in_specs = [pl.BlockSpec((tm, D_MODEL), lambda i: (i, 0)),
                pl.BlockSpec((PK_CHUNKS, TOP_K, tm, LANES), lambda i: (0, 0, i, 0)),
                pl.BlockSpec((tm, LANES), lambda i: (i, 0))]
    args = [x2, g4, gate]
    aliases = {}
    if out_buf is not None:
        in_specs.append(pl.BlockSpec(memory_space=pl.ANY))
        args.append(out_buf)
        aliases = {3: 0}
    return pl.pallas_call(
        _combine_kernel,
        grid=(T // tm,),
        in_specs=in_specs,
        out_specs=pl.BlockSpec((tm, D_MODEL), lambda i: (i + blk0, 0)),
        out_shape=jax.ShapeDtypeStruct((t_total, D_MODEL), F32),
        input_output_aliases=aliases,
        compiler_params=_cparams(("parallel",)),
        name="moe_combine",
    )(*args)


def _moe(streams, places, ew):
    cnts = [st[3][0, :N_EXPERTS] for st in streams]
    total = sum(cnts)
    padded = (total + MOE_ROWS - 1) // MOE_ROWS * MOE_ROWS
    pad_end = jnp.cumsum(padded)
    pad_start = pad_end - padded
    experts = jnp.arange(N_EXPERTS, dtype=jnp.int32)[None, None, :]
    pos_ts = []
    base = pad_start
    for (h3p, idxr, gate, counts, x2), cnt in zip(streams, cnts):
        onehot = idxr[:, :TOP_K, None] == experts
        start = jnp.sum(jnp.where(onehot, base[None, None, :], 0), axis=-1)
        pos_ts.append((start + idxr[:, TOP_K:2 * TOP_K]).T)
        base = base + cnt
    n_assign = sum(st[0].shape[1] for st in streams) * TOP_K
    nb = -(-n_assign // MOE_ROWS) + N_EXPERTS
    starts = jnp.arange(nb, dtype=jnp.int32) * MOE_ROWS
    block_e = jnp.minimum(jnp.sum((pad_end[None, :] <= starts[:, None]).astype(jnp.int32), axis=1), N_EXPERTS - 1)
    n_used = (pad_end[-1] // MOE_ROWS).astype(jnp.int32).reshape(1)
    ar = jnp.arange(N_EXPERTS, dtype=jnp.int32)
    later = (padded > 0)[None, :] & (ar[None, :] > ar[:, None])
    nxt = jnp.min(jnp.where(later, ar[None, :], N_EXPERTS), axis=1)
    nxt = jnp.where(nxt >= N_EXPERTS, -1, nxt)
    mine = block_e[:, None] == ar[None, :]
    next_e = jnp.sum(jnp.where(mine, nxt[None, :], 0), axis=1).astype(jnp.int32)
    seg_end = jnp.sum(jnp.where(mine, (pad_start + total)[None, :], 0), axis=1)
    block_rows = jnp.clip(seg_end - starts, 0, MOE_ROWS).astype(jnp.int32)
    rows = _sc_scatter_rows([st[0] for st in streams], pos_ts, nb * MOE_ROWS)
    y_rows = _experts(block_e, next_e, block_rows, n_used, rows, ew['w_gu'], ew['b_gu'], ew['w_down'], ew['b_down'])
    gs = _sc_gather_rows(y_rows, [p.reshape(-1) for p in pos_ts])
    outs = []
    for (h3p, idxr, gate, counts, x2), g, place in zip(streams, gs, places):
        T = x2.shape[0]
        outs.append(_combine(x2, g.reshape(PK_CHUNKS, TOP_K, T, LANES), gate, _pick(T, (512, 256, 128, 64)), *place))
    return outs


def _pad_lanes(v, n=LANES, fill=0.0):
    return jnp.pad(v, (0, n - v.shape[0]), constant_values=fill).reshape(1, n)


def _prep_weights(norm_mix, w_in, conv_w, a_log, dt_bias, gdn_norm, q_a_norm, w_qb, kv_a_norm, w_kvb, q_norm,
                  k_nope_norm, k_rope_norm, w_out, norm_mem, mem_norm, w_mq, w_mk, w_mv, mq_norm, mk_norm, w_mo,
                  norm_ffn, w_router, b_router, w_gu, b_gu, w_down, b_down):
    c = np.cumsum([CONV_DIM, GDN_VW, GDN_HEADS, GDN_HEADS, Q_RANK, KV_RANK])
    w_u, w_z, w_a, w_b, w_cq, w_ckv, w_kpe = [w_in[:, lo:hi] for lo, hi in
                                              zip([0, *c], [*c, w_in.shape[1]])]
    w_s = jnp.concatenate([w_kpe, w_a, w_b], axis=1)
    w_s = jnp.pad(w_s, ((0, 0), (0, LANES - w_s.shape[1])))
    wq = w_qb.reshape(Q_RANK, MLA_HEADS, QK_DIM)
    wq = jnp.pad(wq, ((0, 0), (0, 0), (0, Q_SLAB - QK_DIM))).reshape(Q_RANK, MLA_HEADS * Q_SLAB)
    wkv = w_kvb.reshape(KV_RANK, MLA_HEADS, NOPE_DIM + V_DIM)
    w_k = wkv[:, :, :NOPE_DIM].reshape(KV_RANK, -1).astype(BF16)
    w_v = wkv[:, :, NOPE_DIM:].reshape(KV_RANK, -1).astype(BF16)
    wr = jnp.pad(w_router, ((0, 0), (0, LANES - N_EXPERTS)))
    wr_hi = wr.astype(BF16)
    wr_lo = (wr - wr_hi.astype(F32)).astype(BF16)
    row = lambda v: v.reshape(1, -1)
    gpad = ROPE_DIM
    pw = dict(
        nmix=row(norm_mix), w_u=w_u.astype(BF16), w_z=w_z.astype(BF16), w_cq=w_cq.astype(BF16),
        w_ckv=w_ckv.astype(BF16), w_s=w_s.astype(BF16), qan=row(q_a_norm), w_qb=wq.astype(BF16),
        qn=_pad_lanes(q_norm, Q_SLAB), kvan=row(kv_a_norm), krn=_pad_lanes(k_rope_norm),
        alog=jnp.pad(a_log, (gpad, LANES - gpad - GDN_HEADS)).reshape(1, LANES),
        dtb=jnp.pad(dt_bias, (gpad, LANES - gpad - GDN_HEADS)).reshape(1, LANES),
        conv_w=conv_w, gnorm=row(gdn_norm), w_k=w_k, w_v=w_v, w_vt=w_v.T, knn=row(k_nope_norm),
        w_out=w_out.astype(BF16), nmem=row(norm_mem), w_mq=w_mq.astype(BF16), mqn=row(mq_norm),
        w_mo=w_mo.astype(BF16), nffn=row(norm_ffn), wr_hi=wr_hi, wr_lo=wr_lo,
        b_r=_pad_lanes(b_router, LANES, NEG_BIG),
        mnorm=row(mem_norm), w_mk=w_mk.astype(BF16), w_mv=w_mv.astype(BF16), mkn=row(mk_norm),
    )
    ew = dict(w_gu=w_gu, b_gu=b_gu.reshape(N_EXPERTS, 1, 2 * D_FF), w_down=w_down,
              b_down=b_down.reshape(N_EXPERTS, 1, D_MODEL))
    return pw, ew


def _rope_tables(P, S):
    half = ROPE_DIM // 2
    inv = ROPE_THETA ** (-jnp.arange(half, dtype=F32) / half)
    ang = (P + jnp.arange(S, dtype=jnp.int32)).astype(F32)[:, None] * inv[None, :]
    cos, sin = jnp.cos(ang), jnp.sin(ang)
    zh = jnp.zeros((S, half), F32)
    zz = jnp.zeros((S, LANES - ROPE_DIM), F32)
    return (jnp.concatenate([cos, cos, zz], 1), jnp.concatenate([-sin, zh, zz], 1),
            jnp.concatenate([zh, sin, zz], 1))


def _pick(n, prefs):
    for t in prefs:
        if n % t == 0:
            return t
    return n


def _trunk_front(x, lat_past, kpe_past, s0, conv_past, mem_k, mem_v, pw, n_groups):
    B, S, D = x.shape
    P = lat_past.shape[1]
    T = B * S
    tm = _pick(S, (512, 256, 128, 64))
    tm_in = 512 if (T % 512 == 0 and (512 % S == 0 or S % 512 == 0)) else tm
    u, z, q, lat_new, small = _inproj(x.reshape(T, D), S, tm_in, pw, _rope_tables(P, S))

    LC = _pick(S, (256, 128, 64))
    NB = _pick(B, tuple(n for n in (8, 4, 2, 1) if n * (LC // CHUNK) <= GDN_UNITS))
    o_gdn, s_new, conv_new = _gdn(u.reshape(B, S, CONV_DIM), small.reshape(B, S, LANES), z.reshape(B, S, GDN_VW),
                                  pw['conv_w'], conv_past, s0, pw['gnorm'], NB, LC)

    kpe_new = small[:, :ROPE_DIM].reshape(B, S, ROPE_DIM)
    lat3 = lat_new.reshape(B, S, KV_RANK)
    small3 = small.reshape(B, S, LANES)
    q3 = q.reshape(B, S, MLA_HEADS * Q_SLAB)
    tq = _pick(S, (512, 256, 128, 64))
    if P == 0 and S % 512 == 0:
        kn, kr, vt = _kvproj(lat3, small3, pw, 512, True)
        o_mla = _flash_t(q3, kn, kr, vt, P, S, tq, 512)
    elif P > 0 and P % CHUNK == 0 and S == CHUNK:
        past = _kvproj(lat_past, kpe_past, pw, _pick(P, (2048, 1024, 512)), False)
        new = _kvproj(lat3, small3, pw, S, False)
        o_mla = _attn_dec(q3, past, new)
    else:
        sk = P + S
        tk = 512 if S >= 512 else -(-sk // LANES) * LANES
        skp = -(-sk // tk) * tk
        lat_all = jnp.pad(jnp.concatenate([lat_past, lat3], axis=1), ((0, 0), (0, skp - sk), (0, 0)))
        kpe_all = jnp.pad(jnp.concatenate([kpe_past, kpe_new], axis=1), ((0, 0), (0, skp - sk), (0, 0)))
        key_major = tq >= LANES
        kn, kr, v = _kvproj(lat_all, kpe_all, pw, _pick(skp, (512,)), key_major)
        o_mla = (_flash_t if key_major else _flash)(q3, kn, kr, v, P, S, tq, tk)

    streams = []
    gb = B // n_groups
    for gi in range(n_groups):
        x2, h3p, idxr, gate, counts = _mix(x, o_gdn, o_mla, mem_k, mem_v, pw, tm, gi * gb, gb)
        tg = gb * S
        streams.append((h3p, idxr.reshape(tg, LANES), gate.reshape(tg, LANES), counts, x2.reshape(tg, D)))
    return streams, (lat_new.reshape(B, S, KV_RANK), kpe_new, s_new, conv_new)


def kernel(x_prompt, x_sample, cache_kv_latent, cache_k_rope, state_gdn, state_conv, cache_mem_k, cache_mem_v, mem_prompt, norm_mix, w_in, conv_w, a_log, dt_bias, gdn_norm, q_a_norm, w_qb, kv_a_norm, w_kvb, q_norm, k_nope_norm, k_rope_norm, w_out, norm_mem, mem_norm, w_mq, w_mk, w_mv, mq_norm, mk_norm, w_mo, norm_ffn, w_router, b_router, w_gu, b_gu, w_down, b_down):
    depth = norm_mix.shape[0]
    yp, ys = x_prompt, x_sample
    bp = x_prompt.shape[0]
    hw = MEM_HEADS * MEM_HEAD_DIM
    outs = [[] for _ in range(10)]
    for l in range(depth):
        pw, ew = _prep_weights(norm_mix[l], w_in[l], conv_w[l], a_log[l], dt_bias[l], gdn_norm[l], q_a_norm[l],
                               w_qb[l], kv_a_norm[l], w_kvb[l], q_norm[l], k_nope_norm[l], k_rope_norm[l], w_out[l],
                               norm_mem[l], mem_norm[l], w_mq[l], w_mk[l], w_mv[l], mq_norm[l], mk_norm[l], w_mo[l],
                               norm_ffn[l], w_router[l], b_router[l], w_gu[l], b_gu[l], w_down[l], b_down[l])
        nm = mem_prompt.shape[1]
        mk, mv = _memkv(mem_prompt.reshape(bp * nm, D_MODEL), pw['mnorm'], pw['w_mk'], pw['w_mv'], pw['mkn'],
                        _pick(bp * nm, (512, 256)))
        mk = mk.reshape(bp, nm, hw)
        mv = mv.reshape(bp, nm, hw)
        n_groups = 2 if bp % 2 == 0 else 1
        streams_p, (lat, kpe, s_fin, cv) = _trunk_front(
            yp, jnp.zeros((bp, 0, KV_RANK), F32), jnp.zeros((bp, 0, ROPE_DIM), F32),
            jnp.zeros((bp, GDN_HEADS, GDN_DK, GDN_DV), F32), jnp.zeros((bp, CONV_W - 1, CONV_DIM), F32), mk, mv, pw,
            n_groups)
        bs = x_sample.shape[0]
        (stream_s,), (lat2, kpe2, s_fin2, cv2) = _trunk_front(
            ys, cache_kv_latent[l], cache_k_rope[l], state_gdn[l], state_conv[l],
            cache_mem_k[l], cache_mem_v[l], pw, 1)
        tp = yp.shape[0] * yp.shape[1]
        tg = tp // n_groups
        ybuf = None
        for gi, st in enumerate(streams_p):
            last = gi == n_groups - 1
            res = _moe([st] + ([stream_s] if last else []),
                       [(ybuf, gi * tg, tp)] + ([(None, 0, None)] if last else []), ew)
            ybuf = res[0]
        yp, ys = ybuf.reshape(yp.shape), res[1].reshape(ys.shape)
        for lst, val in zip(outs, (lat, kpe, s_fin, cv, mk.reshape(bp, nm, MEM_HEADS, MEM_HEAD_DIM),
                                   mv.reshape(bp, nm, MEM_HEADS, MEM_HEAD_DIM), lat2, kpe2, s_fin2, cv2)):
            lst.append(val)
    return (yp, ys) + tuple(jnp.stack(o) for o in outs)
```

```python
import functools
import math

import numpy as np
import jax
import jax.numpy as jnp
from jax import lax
from jax.experimental import pallas as pl
from jax.experimental.pallas import tpu as pltpu
from jax.experimental.pallas import tpu_sc as plsc

F32 = jnp.float32
BF16 = jnp.bfloat16

D_MODEL = 1024
CHUNK = 64
EPS = 1e-6
GDN_HEADS = 4
GDN_DK = 128
GDN_DV = 128
CONV_W = 4
GDN_QK = GDN_HEADS * GDN_DK
GDN_VW = GDN_HEADS * GDN_DV
CONV_DIM = 2 * GDN_QK + GDN_VW
MLA_HEADS = 4
Q_RANK = 384
KV_RANK = 256
NOPE_DIM = 128
ROPE_DIM = 64
V_DIM = 128
QK_DIM = NOPE_DIM + ROPE_DIM
ROPE_THETA = 10000.0
N_MEM = 256
MEM_HEADS = 4
MEM_HEAD_DIM = 128
N_EXPERTS = 32
TOP_K = 4
D_FF = D_MODEL
SWIGLU_ALPHA = 1.702
SWIGLU_LIMIT = 7.0

LANES = 128
Q_SLAB = 2 * LANES
NEG_BIG = -1e30
VMEM_LIMIT = 56 * 1024 * 1024
MOE_ROWS = 512
GDN_UNITS = 8
GDN_GROUP = 2
FF_CHUNK = 256
SC_WINDOW = 128
PK_CHUNKS = D_MODEL // 2 // LANES
SUBTILE_ROWS = 256
ONES_ROWS = 16
MIX_ROWS = 512


def _cparams(sem):
    return pltpu.CompilerParams(dimension_semantics=sem, vmem_limit_bytes=VMEM_LIMIT)


def _dot(a, b):
    return jnp.dot(a, b, preferred_element_type=F32)


def _dot_nt(a, b):
    return lax.dot_general(a, b, (((1,), (1,)), ((), ())), preferred_element_type=F32)


def _dot_tn(a, b):
    return lax.dot_general(a, b, (((0,), (0,)), ((), ())), preferred_element_type=F32)


def _rms(x, gain, n=None):
    n = x.shape[-1] if n is None else n
    ss = jnp.sum(x * x, axis=-1, keepdims=True) * (1.0 / n)
    return (x * lax.rsqrt(ss + EPS)) * gain


def _sigmoid(x):
    return 1.0 / (1.0 + jnp.exp(-x))


def _rope128(r, cos, sna, snb):
    return r * cos + pltpu.roll(r, 96, 1) * sna + pltpu.roll(r, 32, 1) * snb


def _pack_bf16_pairs(x):
    n = x.shape[1] // 2
    lo = pltpu.bitcast(x[:, :n], jnp.uint32) >> 16
    hi = pltpu.bitcast(x[:, n:], jnp.uint32) & jnp.uint32(0xFFFF0000)
    return lo | hi


def _unpack_bf16_pairs(p):
    lo = pltpu.bitcast(p << 16, F32)
    hi = pltpu.bitcast(p & jnp.uint32(0xFFFF0000), F32)
    return jnp.concatenate([lo, hi], axis=1)


def _subtiles(rows):
    n = rows // SUBTILE_ROWS if rows % SUBTILE_ROWS == 0 else 1
    step = rows // n
    return [slice(i * step, (i + 1) * step) for i in range(n)]


def _const_spec(shape):
    nd = len(shape)
    return pl.BlockSpec(shape, lambda *_: (0,) * nd)


def _inproj_kernel(x_ref, nmix_ref, wu_ref, wz_ref, wcq_ref, wckv_ref, ws_ref, qan_ref, wqb_ref, qn_ref,
                   kvan_ref, krn_ref, alog_ref, dtb_ref, cos_ref, sna_ref, snb_ref,
                   u_ref, z_ref, q_ref, lat_ref, small_ref):
    for rs in _subtiles(x_ref.shape[0]):
        x = x_ref[rs, :]
        hb = _rms(x, nmix_ref[...]).astype(BF16)
        u_ref[rs, :] = _dot(hb, wu_ref[...])
        z_ref[rs, :] = _dot(hb, wz_ref[...])
        cq_raw = _dot(hb, wcq_ref[...])
        ckv_raw = _dot(hb, wckv_ref[...])
        sm = _dot(hb, ws_ref[...])
        cos, sna, snb = cos_ref[rs, :], sna_ref[rs, :], snb_ref[rs, :]

        cq = _rms(cq_raw, qan_ref[...]).astype(BF16)
        qf = _dot(cq, wqb_ref[...])
        scale = QK_DIM ** -0.5
        for h in range(MLA_HEADS):
            slab = qf[:, h * Q_SLAB:(h + 1) * Q_SLAB]
            slab = _rms(slab, qn_ref[...], n=QK_DIM)
            nope = slab[:, :LANES]
            ropd = _rope128(slab[:, LANES:], cos, sna, snb)
            q_ref[rs, h * Q_SLAB:h * Q_SLAB + LANES] = (nope * scale).astype(BF16)
            q_ref[rs, h * Q_SLAB + LANES:(h + 1) * Q_SLAB] = (ropd * scale).astype(BF16)

        lat_ref[rs, :] = _rms(ckv_raw, kvan_ref[...])

        lane = lax.broadcasted_iota(jnp.int32, sm.shape, 1)
        kp = jnp.where(lane < ROPE_DIM, sm, 0.0)
        kpe = _rope128(_rms(kp, krn_ref[...], n=ROPE_DIM), cos, sna, snb)
        sp = sm + dtb_ref[...]
        softplus = jnp.maximum(sp, 0.0) + jnp.log1p(jnp.exp(-jnp.abs(sp)))
        g = -jnp.exp(alog_ref[...]) * softplus
        beta = _sigmoid(sm)
        small_ref[rs, :] = jnp.where(lane < ROPE_DIM, kpe,
                                     jnp.where(lane < ROPE_DIM + GDN_HEADS, g,
                                               jnp.where(lane < ROPE_DIM + 2 * GDN_HEADS, beta, 0.0)))


def _inproj(x2d, S, tm, pw, tabs):
    T = x2d.shape[0]
    if tm > S:
        tabs = [jnp.tile(t, (tm // S, 1)) for t in tabs]
    nblk_s = max(S // tm, 1)
    row = lambda n: pl.BlockSpec((tm, n), lambda i: (i, 0))
    tab = pl.BlockSpec((tm, LANES), lambda i: (i % nblk_s, 0))
    consts = [pw['nmix'], pw['w_u'], pw['w_z'], pw['w_cq'], pw['w_ckv'], pw['w_s'], pw['qan'], pw['w_qb'],
              pw['qn'], pw['kvan'], pw['krn'], pw['alog'], pw['dtb']]
    return pl.pallas_call(
        _inproj_kernel,
        grid=(T // tm,),
        in_specs=[row(D_MODEL)] + [_const_spec(c.shape) for c in consts] + [tab, tab, tab],
        out_specs=[row(CONV_DIM), row(GDN_VW), row(MLA_HEADS * Q_SLAB), row(KV_RANK), row(LANES)],
        out_shape=[jax.ShapeDtypeStruct((T, CONV_DIM), F32), jax.ShapeDtypeStruct((T, GDN_VW), F32),
                   jax.ShapeDtypeStruct((T, MLA_HEADS * Q_SLAB), BF16), jax.ShapeDtypeStruct((T, KV_RANK), F32),
                   jax.ShapeDtypeStruct((T, LANES), F32)],
        compiler_params=_cparams(("parallel",)),
        name="inproj",
    )(x2d, *consts, *tabs)


def _split3(x):
    hi = x.astype(BF16)
    r = x - hi.astype(F32)
    mid = r.astype(BF16)
    lo = (r - mid.astype(F32)).astype(BF16)
    return hi, mid, lo


def _gdn_kernel(u_ref, small_ref, z_ref, convw_ref, cpast_ref, s0_ref, gnorm_ref,
                to_ref, trilm_ref, strictm_ref, same2_ref, lvl_ref,
                o_ref, sfin_ref, cnew_ref, ext_ref, uc_ref, state_ref,
                qf_ref, kf_ref, vb_ref, bt_ref, gcum_ref, glast_ref, kk_ref, qk_ref, mb_ref, x_ref, qkb_ref,
                kbe_ref, qg_ref, kdec_ref, egl_ref, t1_ref, uu_ref, ww_ref, vn_ref, qs_ref, *, NB, LC):
    j = pl.program_id(1)
    nj = pl.num_programs(1)
    PADR = 8
    C = LC // CHUNK
    NG = GDN_HEADS // GDN_GROUP
    U = NB * C * NG
    HR = GDN_GROUP * CHUNK

    def unit(nb, c, g):
        return (nb * C + c) * NG + g

    def head_rows(h):
        return slice((h % GDN_GROUP) * CHUNK, (h % GDN_GROUP + 1) * CHUNK)

    @pl.when(j == 0)
    def _():
        state_ref[...] = s0_ref[...]
        ext_ref[:, PADR - (CONV_W - 1):PADR, :] = cpast_ref[...]

    w = convw_ref[...]
    for nb in range(NB):
        ext_ref[nb, PADR:PADR + LC, :] = u_ref[nb]
        acc = ext_ref[nb, PADR:PADR + LC, :] * w[CONV_W - 1:CONV_W, :]
        for t in range(1, CONV_W):
            acc = acc + ext_ref[nb, PADR - t:PADR - t + LC, :] * w[CONV_W - 1 - t:CONV_W - t, :]
        uc_ref[nb] = acc * _sigmoid(acc)
        ext_ref[nb, 0:PADR, :] = ext_ref[nb, LC:LC + PADR, :]

    @pl.when(j == nj - 1)
    def _():
        cnew_ref[...] = ext_ref[:, PADR - (CONV_W - 1):PADR, :]

    g0 = ROPE_DIM
    b0 = ROPE_DIM + GDN_HEADS
    to = to_ref[...]
    for nb, c in [(nb, c) for nb in range(NB) for c in range(C)]:
        rows = slice(c * CHUNK, (c + 1) * CHUNK)
        sm = small_ref[nb, rows, :]
        gl = sum(_dot(to, part) for part in _split3(sm))
        for h in range(GDN_HEADS):
            u = unit(nb, c, h // GDN_GROUP)
            hr = head_rows(h)
            q = uc_ref[nb, rows, h * GDN_DK:(h + 1) * GDN_DK]
            k = uc_ref[nb, rows, GDN_QK + h * GDN_DK:GDN_QK + (h + 1) * GDN_DK]
            v = uc_ref[nb, rows, 2 * GDN_QK + h * GDN_DV:2 * GDN_QK + (h + 1) * GDN_DV]
            beta = jnp.broadcast_to(sm[:, b0 + h:b0 + h + 1], (CHUNK, LANES))
            qf_ref[u, hr, :] = (q * lax.rsqrt(jnp.sum(q * q, -1, keepdims=True) + EPS)) * (GDN_DK ** -0.5)
            kf_ref[u, hr, :] = k * lax.rsqrt(jnp.sum(k * k, -1, keepdims=True) + EPS)
            vb_ref[u, hr, :] = (v * beta).astype(BF16)
            bt_ref[u, hr, :] = beta
            gcum_ref[u, hr, :] = jnp.broadcast_to(gl[:CHUNK, g0 + h:g0 + h + 1], (CHUNK, LANES))
            glast_ref[u, hr, :] = jnp.broadcast_to(gl[CHUNK:, g0 + h:g0 + h + 1], (CHUNK, LANES))

    for u in range(U):
        k = kf_ref[u]
        kbf = k.astype(BF16)
        kk_ref[u] = _dot_nt((k * bt_ref[u]).astype(BF16), kbf)
        qk_ref[u] = _dot_nt(qf_ref[u].astype(BF16), kbf)

    trilm = trilm_ref[...]
    eye = trilm - strictm_ref[...]
    for u in range(U):
        gcum = gcum_ref[u]
        grow = gcum.T[0:1, :]
        gcol = gcum if HR == LANES else jnp.concatenate([gcum] * (HR // LANES), axis=1)
        decay = jnp.exp(jnp.minimum(gcol - grow, 0.0)) * trilm
        m = kk_ref[u] * (decay * strictm_ref[...])
        mb_ref[u] = m.astype(BF16)
        x_ref[u] = eye - m * same2_ref[...]
        qkb_ref[u] = (qk_ref[u] * decay).astype(BF16)
        egc = jnp.exp(gcum)
        k = kf_ref[u]
        kbe_ref[u] = (k * bt_ref[u] * egc).astype(BF16)
        qg_ref[u] = (qf_ref[u] * egc).astype(BF16)
        kdec_ref[u] = (k * jnp.exp(glast_ref[u] - gcum)).astype(BF16)
        egl_ref[u] = jnp.exp(glast_ref[u])

    for lvl in range(lvl_ref.shape[0]):
        lm = lvl_ref[lvl]
        for u in range(U):
            t1_ref[u] = _dot(mb_ref[u] * lm, x_ref[u].astype(BF16)).astype(BF16)
        for u in range(U):
            x = x_ref[u]
            x_ref[u] = x - _dot(x.astype(BF16), t1_ref[u])

    for u in range(U):
        xb = x_ref[u].astype(BF16)
        uu_ref[u] = _dot(xb, vb_ref[u])
        ww_ref[u] = _dot(xb, kbe_ref[u]).astype(BF16)

    gnorm = gnorm_ref[...]
    for c in range(C):
        rows = slice(c * CHUNK, (c + 1) * CHUNK)
        heads = [(nb, h, unit(nb, c, h // GDN_GROUP), head_rows(h)) for nb in range(NB) for h in range(GDN_HEADS)]
        groups = [(nb, g, unit(nb, c, g)) for nb in range(NB) for g in range(NG)]
        for nb, h, u, hr in heads:
            stb = state_ref[nb, h].astype(BF16)
            r = _dot(jnp.concatenate([ww_ref[u, hr, :], qg_ref[u, hr, :]], axis=0), stb)
            vn_ref[u, hr, :] = (uu_ref[u, hr, :] - r[:CHUNK]).astype(BF16)
            qs_ref[u, hr, :] = r[CHUNK:]
        outs = {(nb, g): qs_ref[u] + _dot(qkb_ref[u], vn_ref[u]) for nb, g, u in groups}
        for nb, h, u, hr in heads:
            state_ref[nb, h] = (state_ref[nb, h] * egl_ref[u, hr.start:hr.start + 1, :]
                                + _dot_tn(kdec_ref[u, hr, :], vn_ref[u, hr, :]))
        for nb, h, u, hr in heads:
            zz = z_ref[nb, rows, h * GDN_DV:(h + 1) * GDN_DV]
            og = _rms(outs[(nb, h // GDN_GROUP)][hr, :], gnorm) * (zz * _sigmoid(zz))
            o_ref[nb, rows, h * GDN_DV:(h + 1) * GDN_DV] = og.astype(BF16)

    @pl.when(j == nj - 1)
    def _():
        sfin_ref[...] = state_ref[...]


def _gdn_masks():
    hr = GDN_GROUP * CHUNK
    i = np.arange(hr)[:, None]
    j = np.arange(hr)[None, :]
    same_head = (i // CHUNK) == (j // CHUNK)
    tril = same_head & (i >= j)
    strict = same_head & (i > j)
    same2 = strict & ((i // 2) == (j // 2))
    lvls = []
    blk = 2
    while blk < CHUNK:
        lvls.append(strict & ((i // (2 * blk)) == (j // (2 * blk))) & ((i // blk) != (j // blk)))
        blk *= 2
    fr = np.arange(CHUNK)
    to = np.concatenate([fr[:, None] >= fr[None, :], np.ones((CHUNK, CHUNK), bool)], axis=0)
    f = lambda a: jnp.asarray(a.astype(np.float32))
    return (jnp.asarray(to.astype(np.float32), dtype=BF16), f(tril), f(strict), f(same2),
            jnp.asarray(np.stack(lvls).astype(np.float32), dtype=BF16))


def _gdn(u3, small3, z3, conv_w, conv_past, s0, gnorm, NB, LC):
    B, S, _ = u3.shape
    C = LC // CHUNK
    U = NB * C * (GDN_HEADS // GDN_GROUP)
    HR = GDN_GROUP * CHUNK
    masks = _gdn_masks()
    tile = lambda n: pl.BlockSpec((NB, LC, n), lambda b, j: (b, j, 0))
    stspec = pl.BlockSpec((NB, GDN_HEADS, GDN_DK, GDN_DV), lambda b, j: (b, 0, 0, 0))
    cvspec = pl.BlockSpec((NB, CONV_W - 1, CONV_DIM), lambda b, j: (b, 0, 0))
    vm = lambda shape, dt: pltpu.VMEM(shape, dt)
    return pl.pallas_call(
        functools.partial(_gdn_kernel, NB=NB, LC=LC),
        grid=(B // NB, S // LC),
        in_specs=[tile(CONV_DIM), tile(LANES), tile(GDN_VW), _const_spec(conv_w.shape), cvspec, stspec,
                  _const_spec(gnorm.shape)] + [_const_spec(m.shape) for m in masks],
        out_specs=[tile(GDN_VW), stspec, cvspec],
        out_shape=[jax.ShapeDtypeStruct((B, S, GDN_VW), BF16),
                   jax.ShapeDtypeStruct((B, GDN_HEADS, GDN_DK, GDN_DV), F32),
                   jax.ShapeDtypeStruct((B, CONV_W - 1, CONV_DIM), F32)],
        scratch_shapes=[vm((NB, LC + 8, CONV_DIM), F32), vm((NB, LC, CONV_DIM), F32),
                        vm((NB, GDN_HEADS, GDN_DK, GDN_DV), F32),
                        vm((U, HR, LANES), F32), vm((U, HR, LANES), F32), vm((U, HR, LANES), BF16),
                        vm((U, HR, LANES), F32),
                        vm((U, HR, LANES), F32), vm((U, HR, LANES), F32),
                        vm((U, HR, HR), F32), vm((U, HR, HR), F32),
                        vm((U, HR, HR), BF16), vm((U, HR, HR), F32), vm((U, HR, HR), BF16),
                        vm((U, HR, LANES), BF16), vm((U, HR, LANES), BF16), vm((U, HR, LANES), BF16),
                        vm((U, HR, LANES), F32), vm((U, HR, HR), BF16),
                        vm((U, HR, LANES), F32), vm((U, HR, LANES), BF16),
                        vm((U, HR, LANES), BF16), vm((U, HR, LANES), F32)],
        compiler_params=_cparams(("parallel", "arbitrary")),
        name="gdn",
    )(u3, small3, z3, conv_w, conv_past, s0, gnorm, *masks)


def _kvproj_kernel(lat_ref, kpe_ref, wk_ref, wv_ref, knn_ref, kn_ref, kr_ref, v_ref, *, v_transposed):
    lb = lat_ref[...].astype(BF16)
    kf = _dot(lb, wk_ref[...])
    for h in range(MLA_HEADS):
        sl = slice(h * NOPE_DIM, (h + 1) * NOPE_DIM)
        kn_ref[:, sl] = _rms(kf[:, sl], knn_ref[...]).astype(BF16)
    if v_transposed:
        v_ref[...] = _dot_nt(wv_ref[...], lb).astype(BF16)
    else:
        v_ref[...] = _dot(lb, wv_ref[...]).astype(BF16)
    kp = kpe_ref[...]
    if kp.shape[1] == ROPE_DIM:
        kr_ref[...] = jnp.concatenate([kp, jnp.zeros((kp.shape[0], LANES - ROPE_DIM), F32)], axis=1).astype(BF16)
    else:
        lane = lax.broadcasted_iota(jnp.int32, kp.shape, 1)
        kr_ref[...] = jnp.where(lane < ROPE_DIM, kp, 0.0).astype(BF16)


def _kvproj(lat3, kpe3, pw, tm, v_transposed):
    B, sk, _ = lat3.shape
    hw = MLA_HEADS * V_DIM
    row = lambda n: pl.BlockSpec((None, tm, n), lambda b, i: (b, i, 0))
    if v_transposed:
        vspec, vshape, w_v = pl.BlockSpec((None, hw, tm), lambda b, i: (b, 0, i)), (B, hw, sk), pw['w_vt']
    else:
        vspec, vshape, w_v = row(hw), (B, sk, hw), pw['w_v']
    return pl.pallas_call(
        functools.partial(_kvproj_kernel, v_transposed=v_transposed),
        grid=(B, sk // tm),
        in_specs=[row(KV_RANK), row(kpe3.shape[-1]), _const_spec(pw['w_k'].shape), _const_spec(w_v.shape),
                  _const_spec(pw['knn'].shape)],
        out_specs=[row(MLA_HEADS * NOPE_DIM), row(LANES), vspec],
        out_shape=[jax.ShapeDtypeStruct((B, sk, MLA_HEADS * NOPE_DIM), BF16),
                   jax.ShapeDtypeStruct((B, sk, LANES), BF16), jax.ShapeDtypeStruct(vshape, BF16)],
        compiler_params=_cparams(("parallel", "parallel")),
        name="kvproj",
    )(lat3, kpe3, pw['w_k'], w_v, pw['knn'])


def _last_kblock(qi, tq, tk, P, nk):
    last_key = ((P + qi * tq + tq - 1) // CHUNK) * CHUNK + CHUNK - 1
    return jnp.minimum(last_key // tk, nk - 1)


def _flash_kernel(q_ref, kn_ref, kr_ref, v_ref, o_ref, m_ref, l_ref, acc_ref, *, tq, tk, P, S, nk):
    qi = pl.program_id(1)
    ki = pl.program_id(2)

    @pl.when(ki == 0)
    def _():
        m_ref[...] = jnp.full(m_ref.shape, NEG_BIG, F32)
        l_ref[...] = jnp.zeros(l_ref.shape, F32)
        acc_ref[...] = jnp.zeros(acc_ref.shape, F32)

    @pl.when(ki <= _last_kblock(qi, tq, tk, P, nk))
    def _():
        qpos = P + qi * tq + lax.broadcasted_iota(jnp.int32, (tq, tk), 0)
        kpos = ki * tk + lax.broadcasted_iota(jnp.int32, (tq, tk), 1)
        mask = ((kpos // CHUNK) <= (qpos // CHUNK)) & (kpos < P + S)
        kr = kr_ref[...]
        for h in range(MLA_HEADS):
            qh = q_ref[:, h * Q_SLAB:(h + 1) * Q_SLAB]
            kh = jnp.concatenate([kn_ref[:, h * NOPE_DIM:(h + 1) * NOPE_DIM], kr], axis=1)
            s = jnp.where(mask, _dot_nt(qh, kh), NEG_BIG)
            m_prev = m_ref[h]
            m_new = jnp.maximum(m_prev, jnp.max(s, axis=-1, keepdims=True))
            alpha = jnp.exp(m_prev - m_new)
            p = jnp.exp(s - m_new)
            l_ref[h] = alpha * l_ref[h] + jnp.sum(p, axis=-1, keepdims=True)
            acc_ref[h] = alpha * acc_ref[h] + _dot(p.astype(BF16), v_ref[:, h * V_DIM:(h + 1) * V_DIM])
            m_ref[h] = m_new

    @pl.when(ki == nk - 1)
    def _():
        for h in range(MLA_HEADS):
            o_ref[:, h * V_DIM:(h + 1) * V_DIM] = (acc_ref[h] / l_ref[h]).astype(BF16)


def _flash(q3, kn3, kr3, v3, P, S, tq, tk):
    B = q3.shape[0]
    skp = kn3.shape[1]
    nk = skp // tk
    kmap = lambda b, qi, ki: (b, jnp.minimum(ki, _last_kblock(qi, tq, tk, P, nk)), 0)
    return pl.pallas_call(
        functools.partial(_flash_kernel, tq=tq, tk=tk, P=P, S=S, nk=nk),
        grid=(B, S // tq, nk),
        in_specs=[pl.BlockSpec((None, tq, MLA_HEADS * Q_SLAB), lambda b, qi, ki: (b, qi, 0)),
                  pl.BlockSpec((None, tk, MLA_HEADS * NOPE_DIM), kmap),
                  pl.BlockSpec((None, tk, LANES), kmap),
                  pl.BlockSpec((None, tk, MLA_HEADS * V_DIM), kmap)],
        out_specs=pl.BlockSpec((None, tq, MLA_HEADS * V_DIM), lambda b, qi, ki: (b, qi, 0)),
        out_shape=jax.ShapeDtypeStruct((B, S, MLA_HEADS * V_DIM), BF16),
        scratch_shapes=[pltpu.VMEM((MLA_HEADS, tq, 1), F32), pltpu.VMEM((MLA_HEADS, tq, 1), F32),
                        pltpu.VMEM((MLA_HEADS, tq, V_DIM), F32)],
        compiler_params=_cparams(("parallel", "parallel", "arbitrary")),
        name="mla_attn",
    )(q3, kn3, kr3, v3)


def _attn_dec_kernel(q_ref, knp_ref, krp_ref, vp_ref, knn_ref, krn_ref, vn_ref, o_ref):
    krp, krn = krp_ref[...], krn_ref[...]
    hsl = [slice(h * NOPE_DIM, (h + 1) * NOPE_DIM) for h in range(MLA_HEADS)]
    qs = [q_ref[:, h * Q_SLAB:(h + 1) * Q_SLAB] for h in range(MLA_HEADS)]
    sps = [_dot_nt(q, jnp.concatenate([knp_ref[:, sl], krp], axis=1)) for q, sl in zip(qs, hsl)]
    sns = [_dot_nt(q, jnp.concatenate([knn_ref[:, sl], krn], axis=1)) for q, sl in zip(qs, hsl)]
    for h, (sp, sn) in enumerate(zip(sps, sns)):
        vsl = slice(h * V_DIM, (h + 1) * V_DIM)
        m = jnp.maximum(jnp.max(sp, axis=-1, keepdims=True), jnp.max(sn, axis=-1, keepdims=True))
        pp = jnp.exp(sp - m)
        pn = jnp.exp(sn - m)
        l = jnp.sum(pp, axis=-1, keepdims=True) + jnp.sum(pn, axis=-1, keepdims=True)
        o = _dot(pp.astype(BF16), vp_ref[:, vsl]) + _dot(pn.astype(BF16), vn_ref[:, vsl])
        o_ref[:, vsl] = (o / l).astype(BF16)


def _attn_dec(q3, past, new):
    B, S, _ = q3.shape
    specs = [pl.BlockSpec((None, a.shape[1], a.shape[2]), lambda b: (b, 0, 0)) for a in (q3, *past, *new)]
    return pl.pallas_call(
        _attn_dec_kernel,
        grid=(B,),
        in_specs=specs,
        out_specs=pl.BlockSpec((None, S, MLA_HEADS * V_DIM), lambda b: (b, 0, 0)),
        out_shape=jax.ShapeDtypeStruct((B, S, MLA_HEADS * V_DIM), BF16),
        compiler_params=_cparams(("parallel",)),
        name="mla_attn_dec",
    )(q3, *past, *new)


def _flash_t_kernel(qt_ref, kt_ref, lt_ref, q_ref, kn_ref, kr_ref, vt_ref, o_ref, m_ref, acc_ref, *, tq, tk, P, S):
    j = pl.program_id(1)
    qi = qt_ref[j]
    ki = kt_ref[j]
    q0 = P + qi * tq
    k0 = ki * tk

    @pl.when(ki == 0)
    def _():
        m_ref[...] = jnp.full(m_ref.shape, NEG_BIG, F32)
        acc_ref[...] = jnp.zeros(acc_ref.shape, F32)

    def step(masked):
        kr = kr_ref[...]
        ones = jnp.ones((ONES_ROWS, tk), BF16)
        if masked:
            kpos = k0 + lax.broadcasted_iota(jnp.int32, (tk, 1), 0)
            qpos = q0 + lax.broadcasted_iota(jnp.int32, (1, tq), 1)
            mask = ((kpos // CHUNK) <= (qpos // CHUNK)) & (kpos < P + S)
        sts = []
        for h in range(MLA_HEADS):
            kh = jnp.concatenate([kn_ref[:, h * NOPE_DIM:(h + 1) * NOPE_DIM], kr], axis=1)
            sts.append(_dot_nt(kh, q_ref[:, h * Q_SLAB:(h + 1) * Q_SLAB]))
        ps, alphas = [], []
        for h in range(MLA_HEADS):
            st = sts[h]
            if masked:
                st = jnp.where(mask, st, NEG_BIG)
            m_prev = m_ref[h]
            m_new = jnp.maximum(m_prev, jnp.max(st, axis=0, keepdims=True))
            alphas.append(jnp.exp(m_prev - m_new))
            ps.append(jnp.exp(st - m_new).astype(BF16))
            m_ref[h] = m_new
        for h in range(MLA_HEADS):
            vt1 = jnp.concatenate([vt_ref[h * V_DIM:(h + 1) * V_DIM, :], ones], axis=0)
            acc_ref[h] = alphas[h] * acc_ref[h] + _dot(vt1, ps[h])

    full = ((k0 + tk - 1) // CHUNK <= q0 // CHUNK) & (k0 + tk <= P + S)
    pl.when(full)(functools.partial(step, False))
    pl.when(jnp.logical_not(full))(functools.partial(step, True))

    @pl.when(lt_ref[j] == 1)
    def _():
        for h in range(MLA_HEADS):
            acc = acc_ref[h]
            o_ref[:, h * V_DIM:(h + 1) * V_DIM] = (acc[:V_DIM] / acc[V_DIM:V_DIM + 1]).T.astype(BF16)


def _flash_t(q3, kn3, kr3, vt3, P, S, tq, tk):
    B = q3.shape[0]
    nk = kn3.shape[1] // tk
    pairs = []
    for qi in range(S // tq):
        last = min((((P + qi * tq + tq - 1) // CHUNK) * CHUNK + CHUNK - 1) // tk, nk - 1)
        pairs += [(qi, ki, int(ki == last)) for ki in range(last + 1)]
    qt, kt, lt = (jnp.asarray(np.array(col, np.int32)) for col in zip(*pairs))
    kmap = lambda b, j, qt, kt, lt: (b, kt[j], 0)
    gs = pltpu.PrefetchScalarGridSpec(
        num_scalar_prefetch=3,
        grid=(B, len(pairs)),
        in_specs=[pl.BlockSpec((None, tq, MLA_HEADS * Q_SLAB), lambda b, j, qt, kt, lt: (b, qt[j], 0)),
                  pl.BlockSpec((None, tk, MLA_HEADS * NOPE_DIM), kmap),
                  pl.BlockSpec((None, tk, LANES), kmap),
                  pl.BlockSpec((None, MLA_HEADS * V_DIM, tk), lambda b, j, qt, kt, lt: (b, 0, kt[j]))],
        out_specs=pl.BlockSpec((None, tq, MLA_HEADS * V_DIM), lambda b, j, qt, kt, lt: (b, qt[j], 0)),
        scratch_shapes=[pltpu.VMEM((MLA_HEADS, 1, tq), F32), pltpu.VMEM((MLA_HEADS, V_DIM + ONES_ROWS, tq), F32)],
    )
    return pl.pallas_call(
        functools.partial(_flash_t_kernel, tq=tq, tk=tk, P=P, S=S),
        grid_spec=gs,
        out_shape=jax.ShapeDtypeStruct((B, S, MLA_HEADS * V_DIM), BF16),
        compiler_params=_cparams(("parallel", "arbitrary")),
        name="mla_attn_t",
    )(qt, kt, lt, q3, kn3, kr3, vt3)


def _memkv_kernel(mem_ref, mnorm_ref, wmk_ref, wmv_ref, mkn_ref, k_ref, v_ref):
    mb = _rms(mem_ref[...], mnorm_ref[...]).astype(BF16)
    kf = _dot(mb, wmk_ref[...])
    for h in range(MEM_HEADS):
        sl = slice(h * MEM_HEAD_DIM, (h + 1) * MEM_HEAD_DIM)
        k_ref[:, sl] = _rms(kf[:, sl], mkn_ref[...])
    v_ref[...] = _dot(mb, wmv_ref[...])


def _memkv(mem2d, mnorm, w_mk, w_mv, mkn, tm):
    T = mem2d.shape[0]
    hw = MEM_HEADS * MEM_HEAD_DIM
    row = lambda n: pl.BlockSpec((tm, n), lambda i: (i, 0))
    return pl.pallas_call(
        _memkv_kernel,
        grid=(T // tm,),
        in_specs=[row(D_MODEL), _const_spec(mnorm.shape), _const_spec(w_mk.shape), _const_spec(w_mv.shape),
                  _const_spec(mkn.shape)],
        out_specs=[row(hw), row(hw)],
        out_shape=[jax.ShapeDtypeStruct((T, hw), F32), jax.ShapeDtypeStruct((T, hw), F32)],
        compiler_params=_cparams(("parallel",)),
        name="mem_kv",
    )(mem2d, mnorm, w_mk, w_mv, mkn)


def _mix_kernel(x_ref, og_ref, om_ref, mk_ref, mv_ref, wout_ref, nmem_ref, wmq_ref, mqn_ref, wmo_ref, nffn_ref,
                wrh_ref, wrl_ref, br_ref, x2_ref, h3_ref, idx_ref, gate_ref, counts_ref, cnt_ref):
    def mem_head(ref, b, h):
        if len(ref.shape) == 3:
            return ref[b, :, h * MEM_HEAD_DIM:(h + 1) * MEM_HEAD_DIM].astype(BF16)
        return ref[b, :, h, :].astype(BF16)

    first = (pl.program_id(0) == 0) & (pl.program_id(1) == 0)

    @pl.when(first)
    def _():
        cnt_ref[...] = jnp.zeros(cnt_ref.shape, F32)

    nbm, tm, _ = x_ref.shape
    rows = nbm * tm
    flat = lambda ref: ref[...].reshape(rows, ref.shape[-1])
    if True:
        x1 = flat(x_ref) + _dot(flat(og_ref), wout_ref[0:GDN_VW, :]) + _dot(flat(om_ref), wout_ref[GDN_VW:, :])
        hb = _rms(x1, nmem_ref[...]).astype(BF16)
        qm = _dot(hb, wmq_ref[...])
        per_batch = []
        hsl = [slice(h * MEM_HEAD_DIM, (h + 1) * MEM_HEAD_DIM) for h in range(MEM_HEADS)]
        for b in range(nbm):
            br = slice(b * tm, (b + 1) * tm)
            qhs = [(_rms(qm[br, sl], mqn_ref[...]) * (MEM_HEAD_DIM ** -0.5)).astype(BF16) for sl in hsl]
            ss = [_dot_nt(qh, mem_head(mk_ref, b, h)) for h, qh in enumerate(qhs)]
            ps = []
            for s in ss:
                p = jnp.exp(s - jnp.max(s, axis=-1, keepdims=True))
                ps.append((p / jnp.sum(p, axis=-1, keepdims=True)).astype(BF16))
            heads = [_dot(p, mem_head(mv_ref, b, h)).astype(BF16) for h, p in enumerate(ps)]
            per_batch.append(jnp.concatenate(heads, axis=1))
        om = per_batch[0] if nbm == 1 else jnp.concatenate(per_batch, axis=0)
        x2 = x1 + _dot(om, wmo_ref[...])
        x2_ref[...] = x2.reshape(x2_ref.shape)
        h3 = _rms(x2, nffn_ref[...])
        hi = h3.astype(BF16)
        packed = _pack_bf16_pairs(hi.astype(F32))
        for c in range(h3_ref.shape[0]):
            h3_ref[c] = packed[:, c * LANES:(c + 1) * LANES]
        lo = (h3 - hi.astype(F32)).astype(BF16)
        wrh = wrh_ref[...]
        logits = _dot(hi, wrh) + _dot(lo, wrh) + _dot(hi, wrl_ref[...]) + br_ref[...]

        lane = lax.broadcasted_iota(jnp.int32, logits.shape, 1).astype(F32)
        vals, idxs = [], []
        cur = logits
        for _ in range(TOP_K):
            mx = jnp.max(cur, axis=-1, keepdims=True)
            ix = jnp.min(jnp.where(cur == mx, lane, float(LANES)), axis=-1, keepdims=True)
            vals.append(mx)
            idxs.append(ix)
            cur = jnp.where(lane == ix, -3e38, cur)
        es = [jnp.exp(v - vals[0]) for v in vals]
        den = es[0] + es[1] + es[2] + es[3]

        sel = jnp.zeros(logits.shape, F32)
        for k in range(TOP_K):
            sel = sel + jnp.where(lane == idxs[k], 1.0, 0.0)
        ri = lax.broadcasted_iota(jnp.int32, (rows, rows), 0)
        ci = lax.broadcasted_iota(jnp.int32, (rows, rows), 1)
        before = jnp.where(ri > ci, 1.0, 0.0).astype(BF16)
        excl = _dot(before, sel.astype(BF16)) + cnt_ref[...]
        cnt_ref[...] = cnt_ref[...] + jnp.sum(sel, axis=0, keepdims=True)
        counts_ref[...] = cnt_ref[...].astype(jnp.int32)

        idx_out = jnp.zeros(logits.shape, F32)
        gate_out = jnp.zeros(logits.shape, F32)
        for k in range(TOP_K):
            rank = jnp.sum(jnp.where(lane == idxs[k], excl, 0.0), axis=-1, keepdims=True)
            idx_out = jnp.where(lane == float(k), idxs[k], idx_out)
            idx_out = jnp.where(lane == float(TOP_K + k), rank, idx_out)
            gate_out = jnp.where(lane == float(k), es[k] / den, gate_out)
        idx_ref[...] = idx_out.astype(jnp.int32).reshape(idx_ref.shape)
        gate_ref[...] = gate_out.reshape(gate_ref.shape)


def _mix(x3, og3, om3, mk3, mv3, pw, tm, b0, B):
    S = x3.shape[1]
    hw = MEM_HEADS * MEM_HEAD_DIM
    nbm = _pick(B, tuple(n for n in (8, 4, 2, 1) if n * tm <= MIX_ROWS and b0 % n == 0)) if tm == S else 1
    nsb = S // tm
    boff = b0 // nbm
    tile_in = lambda n: pl.BlockSpec((nbm, tm, n), lambda b, i: (b + boff, i, 0))
    tile = lambda n: pl.BlockSpec((nbm, tm, n), lambda b, i: (b, i, 0))
    if mk3.ndim == 3:
        memspec = pl.BlockSpec((nbm, N_MEM, hw), lambda b, i: (b + boff, 0, 0))
    else:
        memspec = pl.BlockSpec((nbm, N_MEM, MEM_HEADS, MEM_HEAD_DIM), lambda b, i: (b + boff, 0, 0, 0))
    consts = [pw['w_out'], pw['nmem'], pw['w_mq'], pw['mqn'], pw['w_mo'], pw['nffn'], pw['wr_hi'], pw['wr_lo'],
              pw['b_r']]
    return pl.pallas_call(
        _mix_kernel,
        grid=(B // nbm, nsb),
        in_specs=[tile_in(D_MODEL), tile_in(GDN_VW), tile_in(MLA_HEADS * V_DIM), memspec, memspec]
                 + [_const_spec(c.shape) for c in consts],
        out_specs=[tile(D_MODEL), pl.BlockSpec((PK_CHUNKS, nbm * tm, LANES), lambda b, i: (0, b * nsb + i, 0)),
                   tile(LANES), tile(LANES), _const_spec((1, LANES))],
        out_shape=[jax.ShapeDtypeStruct((B, S, D_MODEL), F32),
                   jax.ShapeDtypeStruct((PK_CHUNKS, B * S, LANES), jnp.uint32),
                   jax.ShapeDtypeStruct((B, S, LANES), jnp.int32), jax.ShapeDtypeStruct((B, S, LANES), F32),
                   jax.ShapeDtypeStruct((1, LANES), jnp.int32)],
        scratch_shapes=[pltpu.VMEM((1, LANES), F32)],
        compiler_params=_cparams(("arbitrary", "arbitrary")),
        name="mix_mem_router",
    )(x3, og3, om3, mk3, mv3, *consts)


def _expert_kernel(be_ref, nx_ref, nv_ref, nu_ref, rows_ref, wgu_hbm, bgu_ref, wd_hbm, bd_ref, y_ref,
                   wgus_ref, wds_ref, wgub_ref, wdb_ref, sem_ref):
    i = pl.program_id(0)
    used = i < nu_ref[0]
    new_expert = (i == 0) | (be_ref[i] != be_ref[jnp.maximum(i - 1, 0)])

    def weight_copies(e):
        return (pltpu.make_async_copy(wgu_hbm.at[e], wgus_ref, sem_ref.at[0]),
                pltpu.make_async_copy(wd_hbm.at[e], wds_ref, sem_ref.at[1]))

    @pl.when(i == 0)
    def _():
        for cp in weight_copies(be_ref[0]):
            cp.start()

    @pl.when(used & new_expert)
    def _():
        for cp in weight_copies(be_ref[i]):
            cp.wait()

        def cast(r, carry):
            rs = pl.ds(pl.multiple_of(r * LANES, LANES), LANES)
            wgub_ref[rs, :] = wgus_ref[rs, :].astype(BF16)
            wdb_ref[rs, :] = wds_ref[rs, :].astype(BF16)
            return carry
        lax.fori_loop(0, D_MODEL // LANES, cast, 0)

        @pl.when(nx_ref[i] >= 0)
        def _():
            for cp in weight_copies(nx_ref[i]):
                cp.start()

    def ffn(nrows):
        packed = jnp.concatenate([rows_ref[c, :nrows, :] for c in range(PK_CHUNKS)], axis=1)
        x = _unpack_bf16_pairs(packed).astype(BF16)
        acc = None
        for c in range(D_FF // FF_CHUNK):
            gs_ = slice(c * FF_CHUNK, (c + 1) * FF_CHUNK)
            us_ = slice(D_FF + c * FF_CHUNK, D_FF + (c + 1) * FF_CHUNK)
            gt = jnp.minimum(_dot(x, wgub_ref[:, gs_]) + bgu_ref[:, gs_], SWIGLU_LIMIT)
            up = jnp.clip(_dot(x, wgub_ref[:, us_]) + bgu_ref[:, us_], -SWIGLU_LIMIT, SWIGLU_LIMIT)
            act = gt * _sigmoid(SWIGLU_ALPHA * gt) * (up + 1.0)
            part = _dot(act.astype(BF16), wdb_ref[gs_, :])
            acc = part if acc is None else acc + part
        ypk = _pack_bf16_pairs((acc + bd_ref[...]).astype(BF16).astype(F32))
        for c in range(PK_CHUNKS):
            y_ref[c, :nrows, :] = ypk[:, c * LANES:(c + 1) * LANES]
        if nrows < MOE_ROWS:
            y_ref[:, nrows:, :] = jnp.zeros((PK_CHUNKS, MOE_ROWS - nrows, LANES), y_ref.dtype)

    half_full = nv_ref[i] <= MOE_ROWS // 2
    pl.when(used & jnp.logical_not(half_full))(functools.partial(ffn, MOE_ROWS))
    pl.when(used & half_full)(functools.partial(ffn, MOE_ROWS // 2))

    @pl.when(jnp.logical_not(used))
    def _():
        y_ref[...] = jnp.zeros(y_ref.shape, y_ref.dtype)


def _experts(block_e, next_e, block_rows, n_used, rows, w_gu, b_gu, w_down, b_down):
    n_rows = rows.shape[1]
    nb = n_rows // MOE_ROWS
    gs = pltpu.PrefetchScalarGridSpec(
        num_scalar_prefetch=4,
        grid=(nb,),
        in_specs=[pl.BlockSpec((PK_CHUNKS, MOE_ROWS, LANES), lambda i, be, nx, nv, nu: (0, i, 0)),
                  pl.BlockSpec(memory_space=pl.ANY),
                  pl.BlockSpec((None, 1, 2 * D_FF), lambda i, be, nx, nv, nu: (be[i], 0, 0)),
                  pl.BlockSpec(memory_space=pl.ANY),
                  pl.BlockSpec((None, 1, D_MODEL), lambda i, be, nx, nv, nu: (be[i], 0, 0))],
        out_specs=pl.BlockSpec((PK_CHUNKS, MOE_ROWS, LANES), lambda i, be, nx, nv, nu: (0, i, 0)),
        scratch_shapes=[pltpu.VMEM((D_MODEL, 2 * D_FF), F32), pltpu.VMEM((D_FF, D_MODEL), F32),
                        pltpu.VMEM((D_MODEL, 2 * D_FF), BF16), pltpu.VMEM((D_FF, D_MODEL), BF16),
                        pltpu.SemaphoreType.DMA((2,))],
    )
    return pl.pallas_call(
        _expert_kernel,
        grid_spec=gs,
        out_shape=jax.ShapeDtypeStruct((PK_CHUNKS, n_rows, LANES), jnp.uint32),
        compiler_params=_cparams(("arbitrary",)),
        name="moe_experts",
    )(block_e, next_e, block_rows, n_used, rows, w_gu, b_gu, w_down, b_down)


def _sc_mesh():
    return plsc.VectorSubcoreMesh(core_axis_name="core", subcore_axis_name="subcore")


def _sc_scatter_rows(x3s, pos_ts, n_rows):
    C, _, L = x3s[0].shape
    K = pos_ts[0].shape[0]
    ns = len(x3s)

    @functools.partial(pl.kernel, out_type=jax.ShapeDtypeStruct((C, n_rows, L), x3s[0].dtype), mesh=_sc_mesh(),
                       scratch_types=[])
    def scatter(*refs):
        o_hbm = refs[2 * ns]
        for s in range(ns):
            x_hbm, i_hbm = refs[s], refs[ns + s]
            nwin = x3s[s].shape[1] // SC_WINDOW
            for c in range(C):
                def body(x_vmem, i_vmem, c=c):
                    for k in range(K):
                        pltpu.sync_copy(x_vmem, o_hbm.at[c].at[i_vmem.at[k]])

                pltpu.emit_pipeline(
                    body, grid=(nwin,),
                    in_specs=[pl.BlockSpec((SC_WINDOW, L), lambda i, c=c, nwin=nwin: (c * nwin + i, 0)),
                              pl.BlockSpec((K, SC_WINDOW), lambda i: (0, i))],
                    out_specs=[], core_axis_name=("core", "subcore"), dimension_semantics=(pltpu.PARALLEL,),
                )(x_hbm, i_hbm)

    return scatter(*[x.reshape(-1, L) for x in x3s], *pos_ts)


def _sc_gather_rows(table3, idxs):
    C, _, L = table3.shape
    ns = len(idxs)
    out_type = [jax.ShapeDtypeStruct((C * i.shape[0], L), table3.dtype) for i in idxs]

    @functools.partial(pl.kernel, out_type=out_type, mesh=_sc_mesh(), scratch_types=[])
    def gather(t_hbm, *refs):
        for s in range(ns):
            i_hbm, o_hbm = refs[s], refs[ns + s]
            nwin = idxs[s].shape[0] // SC_WINDOW
            for c in range(C):
                def body(i_vmem, o_vmem, c=c):
                    pltpu.sync_copy(t_hbm.at[c].at[i_vmem.at[0]], o_vmem)

                pltpu.emit_pipeline(
                    body, grid=(nwin,),
                    in_specs=[pl.BlockSpec((1, SC_WINDOW), lambda i: (0, i))],
                    out_specs=[pl.BlockSpec((SC_WINDOW, L), lambda i, c=c, nwin=nwin: (c * nwin + i, 0))],
                    core_axis_name=("core", "subcore"), dimension_semantics=(pltpu.PARALLEL,),
                )(i_hbm, o_hbm)

    outs = gather(table3, *[i.reshape(1, -1) for i in idxs])
    return [o.reshape(C, -1, L) for o in outs]


def _combine_kernel(x2_ref, g_ref, gate_ref, *rest):
    o_ref = rest[-1]
    gate = gate_ref[...]
    half = D_MODEL // 2
    for c in range(PK_CHUNKS):
        lo_s = slice(c * LANES, (c + 1) * LANES)
        hi_s = slice(half + c * LANES, half + (c + 1) * LANES)
        acc_lo = x2_ref[:, lo_s]
        acc_hi = x2_ref[:, hi_s]
        for k in range(TOP_K):
            w = g_ref[c, k]
            gk = gate[:, k:k + 1]
            acc_lo = acc_lo + pltpu.bitcast(w << 16, F32) * gk
            acc_hi = acc_hi + pltpu.bitcast(w & jnp.uint32(0xFFFF0000), F32) * gk
        o_ref[:, lo_s] = acc_lo
        o_ref[:, hi_s] = acc_hi


def _combine(x2, g4, gate, tm, out_buf=None, row0=0, t_total=None):
    T = x2.shape[0]
    t_total = T if t_total is None else t_total
    blk0 = row0 // tm
    in_specs = [pl.BlockSpec((tm, D_MODEL), lambda i: (i, 0)),
                pl.BlockSpec((PK_CHUNKS, TOP_K, tm, LANES), lambda i: (0, 0, i, 0)),
                pl.BlockSpec((tm, LANES), lambda i: (i, 0))]
    args = [x2, g4, gate]
    aliases = {}
    if out_buf is not None:
        in_specs.append(pl.BlockSpec(memory_space=pl.ANY))
        args.append(out_buf)
        aliases = {3: 0}
    return pl.pallas_call(
        _combine_kernel,
        grid=(T // tm,),
        in_specs=in_specs,
        out_specs=pl.BlockSpec((tm, D_MODEL), lambda i: (i + blk0, 0)),
        out_shape=jax.ShapeDtypeStruct((t_total, D_MODEL), F32),
        input_output_aliases=aliases,
        compiler_params=_cparams(("parallel",)),
        name="moe_combine",
    )(*args)


def _moe(streams, places, ew):
    cnts = [st[3][0, :N_EXPERTS] for st in streams]
    total = sum(cnts)
    padded = (total + MOE_ROWS - 1) // MOE_ROWS * MOE_ROWS
    pad_end = jnp.cumsum(padded)
    pad_start = pad_end - padded
    experts = jnp.arange(N_EXPERTS, dtype=jnp.int32)[None, None, :]
    pos_ts = []
    base = pad_start
    for (h3p, idxr, gate, counts, x2), cnt in zip(streams, cnts):
        onehot = idxr[:, :TOP_K, None] == experts
        start = jnp.sum(jnp.where(onehot, base[None, None, :], 0), axis=-1)
        pos_ts.append((start + idxr[:, TOP_K:2 * TOP_K]).T)
        base = base + cnt
    n_assign = sum(st[0].shape[1] for st in streams) * TOP_K
    nb = -(-n_assign // MOE_ROWS) + N_EXPERTS
    starts = jnp.arange(nb, dtype=jnp.int32) * MOE_ROWS
    block_e = jnp.minimum(jnp.sum((pad_end[None, :] <= starts[:, None]).astype(jnp.int32), axis=1), N_EXPERTS - 1)
    n_used = (pad_end[-1] // MOE_ROWS).astype(jnp.int32).reshape(1)
    ar = jnp.arange(N_EXPERTS, dtype=jnp.int32)
    later = (padded > 0)[None, :] & (ar[None, :] > ar[:, None])
    nxt = jnp.min(jnp.where(later, ar[None, :], N_EXPERTS), axis=1)
    nxt = jnp.where(nxt >= N_EXPERTS, -1, nxt)
    mine = block_e[:, None] == ar[None, :]
    next_e = jnp.sum(jnp.where(mine, nxt[None, :], 0), axis=1).astype(jnp.int32)
    seg_end = jnp.sum(jnp.where(mine, (pad_start + total)[None, :], 0), axis=1)
    block_rows = jnp.clip(seg_end - starts, 0, MOE_ROWS).astype(jnp.int32)
    rows = _sc_scatter_rows([st[0] for st in streams], pos_ts, nb * MOE_ROWS)
    y_rows = _experts(block_e, next_e, block_rows, n_used, rows, ew['w_gu'], ew['b_gu'], ew['w_down'], ew['b_down'])
    gs = _sc_gather_rows(y_rows, [p.reshape(-1) for p in pos_ts])
    outs = []
    for (h3p, idxr, gate, counts, x2), g, place in zip(streams, gs, places):
        T = x2.shape[0]
        outs.append(_combine(x2, g.reshape(PK_CHUNKS, TOP_K, T, LANES), gate, _pick(T, (512, 256, 128, 64)), *place))
    return outs


def _pad_lanes(v, n=LANES, fill=0.0):
    return jnp.pad(v, (0, n - v.shape[0]), constant_values=fill).reshape(1, n)


def _prep_weights(norm_mix, w_in, conv_w, a_log, dt_bias, gdn_norm, q_a_norm, w_qb, kv_a_norm, w_kvb, q_norm,
                  k_nope_norm, k_rope_norm, w_out, norm_mem, mem_norm, w_mq, w_mk, w_mv, mq_norm, mk_norm, w_mo,
                  norm_ffn, w_router, b_router, w_gu, b_gu, w_down, b_down):
    c = np.cumsum([CONV_DIM, GDN_VW, GDN_HEADS, GDN_HEADS, Q_RANK, KV_RANK])
    w_u, w_z, w_a, w_b, w_cq, w_ckv, w_kpe = [w_in[:, lo:hi] for lo, hi in
                                              zip([0, *c], [*c, w_in.shape[1]])]
    w_s = jnp.concatenate([w_kpe, w_a, w_b], axis=1)
    w_s = jnp.pad(w_s, ((0, 0), (0, LANES - w_s.shape[1])))
    wq = w_qb.reshape(Q_RANK, MLA_HEADS, QK_DIM)
    wq = jnp.pad(wq, ((0, 0), (0, 0), (0, Q_SLAB - QK_DIM))).reshape(Q_RANK, MLA_HEADS * Q_SLAB)
    wkv = w_kvb.reshape(KV_RANK, MLA_HEADS, NOPE_DIM + V_DIM)
    w_k = wkv[:, :, :NOPE_DIM].reshape(KV_RANK, -1).astype(BF16)
    w_v = wkv[:, :, NOPE_DIM:].reshape(KV_RANK, -1).astype(BF16)
    wr = jnp.pad(w_router, ((0, 0), (0, LANES - N_EXPERTS)))
    wr_hi = wr.astype(BF16)
    wr_lo = (wr - wr_hi.astype(F32)).astype(BF16)
    row = lambda v: v.reshape(1, -1)
    gpad = ROPE_DIM
    pw = dict(
        nmix=row(norm_mix), w_u=w_u.astype(BF16), w_z=w_z.astype(BF16), w_cq=w_cq.astype(BF16),
        w_ckv=w_ckv.astype(BF16), w_s=w_s.astype(BF16), qan=row(q_a_norm), w_qb=wq.astype(BF16),
        qn=_pad_lanes(q_norm, Q_SLAB), kvan=row(kv_a_norm), krn=_pad_lanes(k_rope_norm),
        alog=jnp.pad(a_log, (gpad, LANES - gpad - GDN_HEADS)).reshape(1, LANES),
        dtb=jnp.pad(dt_bias, (gpad, LANES - gpad - GDN_HEADS)).reshape(1, LANES),
        conv_w=conv_w, gnorm=row(gdn_norm), w_k=w_k, w_v=w_v, w_vt=w_v.T, knn=row(k_nope_norm),
        w_out=w_out.astype(BF16), nmem=row(norm_mem), w_mq=w_mq.astype(BF16), mqn=row(mq_norm),
        w_mo=w_mo.astype(BF16), nffn=row(norm_ffn), wr_hi=wr_hi, wr_lo=wr_lo,
        b_r=_pad_lanes(b_router, LANES, NEG_BIG),
        mnorm=row(mem_norm), w_mk=w_mk.astype(BF16), w_mv=w_mv.astype(BF16), mkn=row(mk_norm),
    )
    ew = dict(w_gu=w_gu, b_gu=b_gu.reshape(N_EXPERTS, 1, 2 * D_FF), w_down=w_down,
              b_down=b_down.reshape(N_EXPERTS, 1, D_MODEL))
    return pw, ew


def _rope_tables(P, S):
    half = ROPE_DIM // 2
    inv = ROPE_THETA ** (-jnp.arange(half, dtype=F32) / half)
    ang = (P + jnp.arange(S, dtype=jnp.int32)).astype(F32)[:, None] * inv[None, :]
    cos, sin = jnp.cos(ang), jnp.sin(ang)
    zh = jnp.zeros((S, half), F32)
    zz = jnp.zeros((S, LANES - ROPE_DIM), F32)
    return (jnp.concatenate([cos, cos, zz], 1), jnp.concatenate([-sin, zh, zz], 1),
            jnp.concatenate([zh, sin, zz], 1))


def _pick(n, prefs):
    for t in prefs:
        if n % t == 0:
            return t
    return n


def _trunk_front(x, lat_past, kpe_past, s0, conv_past, mem_k, mem_v, pw, n_groups):
    B, S, D = x.shape
    P = lat_past.shape[1]
    T = B * S
    tm = _pick(S, (512, 256, 128, 64))
    tm_in = 512 if (T % 512 == 0 and (512 % S == 0 or S % 512 == 0)) else tm
    u, z, q, lat_new, small = _inproj(x.reshape(T, D), S, tm_in, pw, _rope_tables(P, S))

    LC = _pick(S, (256, 128, 64))
    NB = _pick(B, tuple(n for n in (8, 4, 2, 1) if n * (LC // CHUNK) <= GDN_UNITS))
    o_gdn, s_new, conv_new = _gdn(u.reshape(B, S, CONV_DIM), small.reshape(B, S, LANES), z.reshape(B, S, GDN_VW),
                                  pw['conv_w'], conv_past, s0, pw['gnorm'], NB, LC)

    kpe_new = small[:, :ROPE_DIM].reshape(B, S, ROPE_DIM)
    lat3 = lat_new.reshape(B, S, KV_RANK)
    small3 = small.reshape(B, S, LANES)
    q3 = q.reshape(B, S, MLA_HEADS * Q_SLAB)
    tq = _pick(S, (512, 256, 128, 64))
    if P == 0 and S % 512 == 0:
        kn, kr, vt = _kvproj(lat3, small3, pw, 512, True)
        o_mla = _flash_t(q3, kn, kr, vt, P, S, tq, 512)
    elif P > 0 and P % CHUNK == 0 and S == CHUNK:
        past = _kvproj(lat_past, kpe_past, pw, _pick(P, (2048, 1024, 512)), False)
        new = _kvproj(lat3, small3, pw, S, False)
        o_mla = _attn_dec(q3, past, new)
    else:
        sk = P + S
        tk = 512 if S >= 512 else -(-sk // LANES) * LANES
        skp = -(-sk // tk) * tk
        lat_all = jnp.pad(jnp.concatenate([lat_past, lat3], axis=1), ((0, 0), (0, skp - sk), (0, 0)))
        kpe_all = jnp.pad(jnp.concatenate([kpe_past, kpe_new], axis=1), ((0, 0), (0, skp - sk), (0, 0)))
        key_major = tq >= LANES
        kn, kr, v = _kvproj(lat_all, kpe_all, pw, _pick(skp, (512,)), key_major)
        o_mla = (_flash_t if key_major else _flash)(q3, kn, kr, v, P, S, tq, tk)

    streams = []
    gb = B // n_groups
    for gi in range(n_groups):
        x2, h3p, idxr, gate, counts = _mix(x, o_gdn, o_mla, mem_k, mem_v, pw, tm, gi * gb, gb)
        tg = gb * S
        streams.append((h3p, idxr.reshape(tg, LANES), gate.reshape(tg, LANES), counts, x2.reshape(tg, D)))
    return streams, (lat_new.reshape(B, S, KV_RANK), kpe_new, s_new, conv_new)


def kernel(x_prompt, x_sample, cache_kv_latent, cache_k_rope, state_gdn, state_conv, cache_mem_k, cache_mem_v, mem_prompt, norm_mix, w_in, conv_w, a_log, dt_bias, gdn_norm, q_a_norm, w_qb, kv_a_norm, w_kvb, q_norm, k_nope_norm, k_rope_norm, w_out, norm_mem, mem_norm, w_mq, w_mk, w_mv, mq_norm, mk_norm, w_mo, norm_ffn, w_router, b_router, w_gu, b_gu, w_down, b_down):
    depth = norm_mix.shape[0]
    yp, ys = x_prompt, x_sample
    bp = x_prompt.shape[0]
    hw = MEM_HEADS * MEM_HEAD_DIM
    outs = [[] for _ in range(10)]
    for l in range(depth):
        pw, ew = _prep_weights(norm_mix[l], w_in[l], conv_w[l], a_log[l], dt_bias[l], gdn_norm[l], q_a_norm[l],
                               w_qb[l], kv_a_norm[l], w_kvb[l], q_norm[l], k_nope_norm[l], k_rope_norm[l], w_out[l],
                               norm_mem[l], mem_norm[l], w_mq[l], w_mk[l], w_mv[l], mq_norm[l], mk_norm[l], w_mo[l],
                               norm_ffn[l], w_router[l], b_router[l], w_gu[l], b_gu[l], w_down[l], b_down[l])
        nm = mem_prompt.shape[1]
        mk, mv = _memkv(mem_prompt.reshape(bp * nm, D_MODEL), pw['mnorm'], pw['w_mk'], pw['w_mv'], pw['mkn'],
                        _pick(bp * nm, (512, 256)))
        mk = mk.reshape(bp, nm, hw)
        mv = mv.reshape(bp, nm, hw)
        n_groups = 2 if bp % 2 == 0 else 1
        streams_p, (lat, kpe, s_fin, cv) = _trunk_front(
            yp, jnp.zeros((bp, 0, KV_RANK), F32), jnp.zeros((bp, 0, ROPE_DIM), F32),
            jnp.zeros((bp, GDN_HEADS, GDN_DK, GDN_DV), F32), jnp.zeros((bp, CONV_W - 1, CONV_DIM), F32), mk, mv, pw,
            n_groups)
        bs = x_sample.shape[0]
        (stream_s,), (lat2, kpe2, s_fin2, cv2) = _trunk_front(
            ys, cache_kv_latent[l], cache_k_rope[l], state_gdn[l], state_conv[l],
            cache_mem_k[l], cache_mem_v[l], pw, 1)
        tp = yp.shape[0] * yp.shape[1]
        tg = tp // n_groups
        ybuf = None
        for gi, st in enumerate(streams_p):
            last = gi == n_groups - 1
            res = _moe([st] + ([stream_s] if last else []),
                       [(ybuf, gi * tg, tp)] + ([(None, 0, None)] if last else []), ew)
            ybuf = res[0]
        yp, ys = ybuf.reshape(yp.shape), res[1].reshape(ys.shape)
        for lst, val in zip(outs, (lat, kpe, s_fin, cv, mk.reshape(bp, nm, MEM_HEADS, MEM_HEAD_DIM),
                                   mv.reshape(bp, nm, MEM_HEADS, MEM_HEAD_DIM), lat2, kpe2, s_fin2, cv2)):
            lst.append(val)
    return (yp, ys) + tuple(jnp.stack(o) for o in outs)
```

```python
import functools
import math

import numpy as np
import jax
import jax.numpy as jnp
from jax import lax
from jax.experimental import pallas as pl
from jax.experimental.pallas import tpu as pltpu
from jax.experimental.pallas import tpu_sc as plsc

F32 = jnp.float32
BF16 = jnp.bfloat16

D_MODEL = 1024
CHUNK = 64
EPS = 1e-6
GDN_HEADS = 4
GDN_DK = 128
GDN_DV = 128
CONV_W = 4
GDN_QK = GDN_HEADS * GDN_DK
GDN_VW = GDN_HEADS * GDN_DV
CONV_DIM = 2 * GDN_QK + GDN_VW
MLA_HEADS = 4
Q_RANK = 384
KV_RANK = 256
NOPE_DIM = 128
ROPE_DIM = 64
V_DIM = 128
QK_DIM = NOPE_DIM + ROPE_DIM
ROPE_THETA = 10000.0
N_MEM = 256
MEM_HEADS = 4
MEM_HEAD_DIM = 128
N_EXPERTS = 32
TOP_K = 4
D_FF = D_MODEL
SWIGLU_ALPHA = 1.702
SWIGLU_LIMIT = 7.0

LANES = 128
Q_SLAB = 2 * LANES
NEG_BIG = -1e30
VMEM_LIMIT = 56 * 1024 * 1024
MOE_ROWS = 512
GDN_UNITS = 8
GDN_GROUP = 2
FF_CHUNK = 512
SC_WINDOW = 128
PK_CHUNKS = D_MODEL // 2 // LANES
SUBTILE_ROWS = 256
ONES_ROWS = 16
MIX_ROWS = 512


def _cparams(sem):
    return pltpu.CompilerParams(dimension_semantics=sem, vmem_limit_bytes=VMEM_LIMIT)


def _dot(a, b):
    return jnp.dot(a, b, preferred_element_type=F32)


def _dot_nt(a, b):
    return lax.dot_general(a, b, (((1,), (1,)), ((), ())), preferred_element_type=F32)


def _dot_tn(a, b):
    return lax.dot_general(a, b, (((0,), (0,)), ((), ())), preferred_element_type=F32)


def _rms(x, gain, n=None):
    n = x.shape[-1] if n is None else n
    ss = jnp.sum(x * x, axis=-1, keepdims=True) * (1.0 / n)
    return (x * lax.rsqrt(ss + EPS)) * gain


def _sigmoid(x):
    return 1.0 / (1.0 + jnp.exp(-x))


def _rope128(r, cos, sna, snb):
    return r * cos + pltpu.roll(r, 96, 1) * sna + pltpu.roll(r, 32, 1) * snb


def _pack_bf16_pairs(x):
    n = x.shape[1] // 2
    lo = pltpu.bitcast(x[:, :n], jnp.uint32) >> 16
    hi = pltpu.bitcast(x[:, n:], jnp.uint32) & jnp.uint32(0xFFFF0000)
    return lo | hi


def _unpack_bf16_pairs(p):
    lo = pltpu.bitcast(p << 16, F32)
    hi = pltpu.bitcast(p & jnp.uint32(0xFFFF0000), F32)
    return jnp.concatenate([lo, hi], axis=1)


def _subtiles(rows):
    n = rows // SUBTILE_ROWS if rows % SUBTILE_ROWS == 0 else 1
    step = rows // n
    return [slice(i * step, (i + 1) * step) for i in range(n)]


def _const_spec(shape):
    nd = len(shape)
    return pl.BlockSpec(shape, lambda *_: (0,) * nd)


def _inproj_kernel(x_ref, nmix_ref, wu_ref, wz_ref, wcq_ref, wckv_ref, ws_ref, qan_ref, wqb_ref, qn_ref,
                   kvan_ref, krn_ref, alog_ref, dtb_ref, cos_ref, sna_ref, snb_ref,
                   u_ref, z_ref, q_ref, lat_ref, small_ref):
    for rs in _subtiles(x_ref.shape[0]):
        x = x_ref[rs, :]
        hb = _rms(x, nmix_ref[...]).astype(BF16)
        u_ref[rs, :] = _dot(hb, wu_ref[...])
        z_ref[rs, :] = _dot(hb, wz_ref[...])
        cq_raw = _dot(hb, wcq_ref[...])
        ckv_raw = _dot(hb, wckv_ref[...])
        sm = _dot(hb, ws_ref[...])
        cos, sna, snb = cos_ref[rs, :], sna_ref[rs, :], snb_ref[rs, :]

        cq = _rms(cq_raw, qan_ref[...]).astype(BF16)
        qf = _dot(cq, wqb_ref[...])
        scale = QK_DIM ** -0.5
        for h in range(MLA_HEADS):
            slab = qf[:, h * Q_SLAB:(h + 1) * Q_SLAB]
            slab = _rms(slab, qn_ref[...], n=QK_DIM)
            nope = slab[:, :LANES]
            ropd = _rope128(slab[:, LANES:], cos, sna, snb)
            q_ref[rs, h * Q_SLAB:h * Q_SLAB + LANES] = (nope * scale).astype(BF16)
            q_ref[rs, h * Q_SLAB + LANES:(h + 1) * Q_SLAB] = (ropd * scale).astype(BF16)

        lat_ref[rs, :] = _rms(ckv_raw, kvan_ref[...])

        lane = lax.broadcasted_iota(jnp.int32, sm.shape, 1)
        kp = jnp.where(lane < ROPE_DIM, sm, 0.0)
        kpe = _rope128(_rms(kp, krn_ref[...], n=ROPE_DIM), cos, sna, snb)
        sp = sm + dtb_ref[...]
        softplus = jnp.maximum(sp, 0.0) + jnp.log1p(jnp.exp(-jnp.abs(sp)))
        g = -jnp.exp(alog_ref[...]) * softplus
        beta = _sigmoid(sm)
        small_ref[rs, :] = jnp.where(lane < ROPE_DIM, kpe,
                                     jnp.where(lane < ROPE_DIM + GDN_HEADS, g,
                                               jnp.where(lane < ROPE_DIM + 2 * GDN_HEADS, beta, 0.0)))


def _inproj(x2d, S, tm, pw, tabs):
    T = x2d.shape[0]
    if tm > S:
        tabs = [jnp.tile(t, (tm // S, 1)) for t in tabs]
    nblk_s = max(S // tm, 1)
    row = lambda n: pl.BlockSpec((tm, n), lambda i: (i, 0))
    tab = pl.BlockSpec((tm, LANES), lambda i: (i % nblk_s, 0))
    consts = [pw['nmix'], pw['w_u'], pw['w_z'], pw['w_cq'], pw['w_ckv'], pw['w_s'], pw['qan'], pw['w_qb'],
              pw['qn'], pw['kvan'], pw['krn'], pw['alog'], pw['dtb']]
    return pl.pallas_call(
        _inproj_kernel,
        grid=(T // tm,),
        in_specs=[row(D_MODEL)] + [_const_spec(c.shape) for c in consts] + [tab, tab, tab],
        out_specs=[row(CONV_DIM), row(GDN_VW), row(MLA_HEADS * Q_SLAB), row(KV_RANK), row(LANES)],
        out_shape=[jax.ShapeDtypeStruct((T, CONV_DIM), F32), jax.ShapeDtypeStruct((T, GDN_VW), F32),
                   jax.ShapeDtypeStruct((T, MLA_HEADS * Q_SLAB), BF16), jax.ShapeDtypeStruct((T, KV_RANK), F32),
                   jax.ShapeDtypeStruct((T, LANES), F32)],
        compiler_params=_cparams(("parallel",)),
        name="inproj",
    )(x2d, *consts, *tabs)


def _split3(x):
    hi = x.astype(BF16)
    r = x - hi.astype(F32)
    mid = r.astype(BF16)
    lo = (r - mid.astype(F32)).astype(BF16)
    return hi, mid, lo


def _gdn_kernel(u_ref, small_ref, z_ref, convw_ref, cpast_ref, s0_ref, gnorm_ref,
                to_ref, trilm_ref, strictm_ref, same2_ref, lvl_ref,
                o_ref, sfin_ref, cnew_ref, ext_ref, uc_ref, state_ref,
                qf_ref, kf_ref, vb_ref, bt_ref, gcum_ref, glast_ref, kk_ref, qk_ref, mb_ref, x_ref, qkb_ref,
                kbe_ref, qg_ref, kdec_ref, egl_ref, t1_ref, uu_ref, ww_ref, vn_ref, qs_ref, *, NB, LC):
    j = pl.program_id(1)
    nj = pl.num_programs(1)
    PADR = 8
    C = LC // CHUNK
    NG = GDN_HEADS // GDN_GROUP
    U = NB * C * NG
    HR = GDN_GROUP * CHUNK

    def unit(nb, c, g):
        return (nb * C + c) * NG + g

    def head_rows(h):
        return slice((h % GDN_GROUP) * CHUNK, (h % GDN_GROUP + 1) * CHUNK)

    @pl.when(j == 0)
    def _():
        state_ref[...] = s0_ref[...]
        ext_ref[:, PADR - (CONV_W - 1):PADR, :] = cpast_ref[...]

    w = convw_ref[...]
    for nb in range(NB):
        ext_ref[nb, PADR:PADR + LC, :] = u_ref[nb]
        acc = ext_ref[nb, PADR:PADR + LC, :] * w[CONV_W - 1:CONV_W, :]
        for t in range(1, CONV_W):
            acc = acc + ext_ref[nb, PADR - t:PADR - t + LC, :] * w[CONV_W - 1 - t:CONV_W - t, :]
        uc_ref[nb] = acc * _sigmoid(acc)
        ext_ref[nb, 0:PADR, :] = ext_ref[nb, LC:LC + PADR, :]

    @pl.when(j == nj - 1)
    def _():
        cnew_ref[...] = ext_ref[:, PADR - (CONV_W - 1):PADR, :]

    g0 = ROPE_DIM
    b0 = ROPE_DIM + GDN_HEADS
    to = to_ref[...]
    for nb, c in [(nb, c) for nb in range(NB) for c in range(C)]:
        rows = slice(c * CHUNK, (c + 1) * CHUNK)
        sm = small_ref[nb, rows, :]
        gl = sum(_dot(to, part) for part in _split3(sm))
        for h in range(GDN_HEADS):
            u = unit(nb, c, h // GDN_GROUP)
            hr = head_rows(h)
            q = uc_ref[nb, rows, h * GDN_DK:(h + 1) * GDN_DK]
            k = uc_ref[nb, rows, GDN_QK + h * GDN_DK:GDN_QK + (h + 1) * GDN_DK]
            v = uc_ref[nb, rows, 2 * GDN_QK + h * GDN_DV:2 * GDN_QK + (h + 1) * GDN_DV]
            beta = jnp.broadcast_to(sm[:, b0 + h:b0 + h + 1], (CHUNK, LANES))
            qf_ref[u, hr, :] = (q * lax.rsqrt(jnp.sum(q * q, -1, keepdims=True) + EPS)) * (GDN_DK ** -0.5)
            kf_ref[u, hr, :] = k * lax.rsqrt(jnp.sum(k * k, -1, keepdims=True) + EPS)
            vb_ref[u, hr, :] = (v * beta).astype(BF16)
            bt_ref[u, hr, :] = beta
            gcum_ref[u, hr, :] = jnp.broadcast_to(gl[:CHUNK, g0 + h:g0 + h + 1], (CHUNK, LANES))
            glast_ref[u, hr, :] = jnp.broadcast_to(gl[CHUNK:, g0 + h:g0 + h + 1], (CHUNK, LANES))

    for u in range(U):
        k = kf_ref[u]
        kbf = k.astype(BF16)
        kk_ref[u] = _dot_nt((k * bt_ref[u]).astype(BF16), kbf)
        qk_ref[u] = _dot_nt(qf_ref[u].astype(BF16), kbf)

    trilm = trilm_ref[...]
    eye = trilm - strictm_ref[...]
    for u in range(U):
        gcum = gcum_ref[u]
        grow = gcum.T[0:1, :]
        gcol = gcum if HR == LANES else jnp.concatenate([gcum] * (HR // LANES), axis=1)
        decay = jnp.exp(jnp.minimum(gcol - grow, 0.0)) * trilm
        m = kk_ref[u] * (decay * strictm_ref[...])
        mb_ref[u] = m.astype(BF16)
        x_ref[u] = eye - m * same2_ref[...]
        qkb_ref[u] = (qk_ref[u] * decay).astype(BF16)
        egc = jnp.exp(gcum)
        k = kf_ref[u]
        kbe_ref[u] = (k * bt_ref[u] * egc).astype(BF16)
        qg_ref[u] = (qf_ref[u] * egc).astype(BF16)
        kdec_ref[u] = (k * jnp.exp(glast_ref[u] - gcum)).astype(BF16)
        egl_ref[u] = jnp.exp(glast_ref[u])

    for lvl in range(lvl_ref.shape[0]):
        lm = lvl_ref[lvl]
        for u in range(U):
            t1_ref[u] = _dot(mb_ref[u] * lm, x_ref[u].astype(BF16)).astype(BF16)
        for u in range(U):
            x = x_ref[u]
            x_ref[u] = x - _dot(x.astype(BF16), t1_ref[u])

    for u in range(U):
        xb = x_ref[u].astype(BF16)
        uu_ref[u] = _dot(xb, vb_ref[u])
        ww_ref[u] = _dot(xb, kbe_ref[u]).astype(BF16)

    gnorm = gnorm_ref[...]
    for c in range(C):
        rows = slice(c * CHUNK, (c + 1) * CHUNK)
        heads = [(nb, h, unit(nb, c, h // GDN_GROUP), head_rows(h)) for nb in range(NB) for h in range(GDN_HEADS)]
        groups = [(nb, g, unit(nb, c, g)) for nb in range(NB) for g in range(NG)]
        for nb, h, u, hr in heads:
            stb = state_ref[nb, h].astype(BF16)
            r = _dot(jnp.concatenate([ww_ref[u, hr, :], qg_ref[u, hr, :]], axis=0), stb)
            vn_ref[u, hr, :] = (uu_ref[u, hr, :] - r[:CHUNK]).astype(BF16)
            qs_ref[u, hr, :] = r[CHUNK:]
        outs = {(nb, g): qs_ref[u] + _dot(qkb_ref[u], vn_ref[u]) for nb, g, u in groups}
        for nb, h, u, hr in heads:
            state_ref[nb, h] = (state_ref[nb, h] * egl_ref[u, hr.start:hr.start + 1, :]
                                + _dot_tn(kdec_ref[u, hr, :], vn_ref[u, hr, :]))
        for nb, h, u, hr in heads:
            zz = z_ref[nb, rows, h * GDN_DV:(h + 1) * GDN_DV]
            og = _rms(outs[(nb, h // GDN_GROUP)][hr, :], gnorm) * (zz * _sigmoid(zz))
            o_ref[nb, rows, h * GDN_DV:(h + 1) * GDN_DV] = og.astype(BF16)

    @pl.when(j == nj - 1)
    def _():
        sfin_ref[...] = state_ref[...]


def _gdn_masks():
    hr = GDN_GROUP * CHUNK
    i = np.arange(hr)[:, None]
    j = np.arange(hr)[None, :]
    same_head = (i // CHUNK) == (j // CHUNK)
    tril = same_head & (i >= j)
    strict = same_head & (i > j)
    same2 = strict & ((i // 2) == (j // 2))
    lvls = []
    blk = 2
    while blk < CHUNK:
        lvls.append(strict & ((i // (2 * blk)) == (j // (2 * blk))) & ((i // blk) != (j // blk)))
        blk *= 2
    fr = np.arange(CHUNK)
    to = np.concatenate([fr[:, None] >= fr[None, :], np.ones((CHUNK, CHUNK), bool)], axis=0)
    f = lambda a: jnp.asarray(a.astype(np.float32))
    return (jnp.asarray(to.astype(np.float32), dtype=BF16), f(tril), f(strict), f(same2),
            jnp.asarray(np.stack(lvls).astype(np.float32), dtype=BF16))


def _gdn(u3, small3, z3, conv_w, conv_past, s0, gnorm, NB, LC):
    B, S, _ = u3.shape
    C = LC // CHUNK
    U = NB * C * (GDN_HEADS // GDN_GROUP)
    HR = GDN_GROUP * CHUNK
    masks = _gdn_masks()
    tile = lambda n: pl.BlockSpec((NB, LC, n), lambda b, j: (b, j, 0))
    stspec = pl.BlockSpec((NB, GDN_HEADS, GDN_DK, GDN_DV), lambda b, j: (b, 0, 0, 0))
    cvspec = pl.BlockSpec((NB, CONV_W - 1, CONV_DIM), lambda b, j: (b, 0, 0))
    vm = lambda shape, dt: pltpu.VMEM(shape, dt)
    return pl.pallas_call(
        functools.partial(_gdn_kernel, NB=NB, LC=LC),
        grid=(B // NB, S // LC),
        in_specs=[tile(CONV_DIM), tile(LANES), tile(GDN_VW), _const_spec(conv_w.shape), cvspec, stspec,
                  _const_spec(gnorm.shape)] + [_const_spec(m.shape) for m in masks],
        out_specs=[tile(GDN_VW), stspec, cvspec],
        out_shape=[jax.ShapeDtypeStruct((B, S, GDN_VW), BF16),
                   jax.ShapeDtypeStruct((B, GDN_HEADS, GDN_DK, GDN_DV), F32),
                   jax.ShapeDtypeStruct((B, CONV_W - 1, CONV_DIM), F32)],
        scratch_shapes=[vm((NB, LC + 8, CONV_DIM), F32), vm((NB, LC, CONV_DIM), F32),
                        vm((NB, GDN_HEADS, GDN_DK, GDN_DV), F32),
                        vm((U, HR, LANES), F32), vm((U, HR, LANES), F32), vm((U, HR, LANES), BF16),
                        vm((U, HR, LANES), F32),
                        vm((U, HR, LANES), F32), vm((U, HR, LANES), F32),
                        vm((U, HR, HR), F32), vm((U, HR, HR), F32),
                        vm((U, HR, HR), BF16), vm((U, HR, HR), F32), vm((U, HR, HR), BF16),
                        vm((U, HR, LANES), BF16), vm((U, HR, LANES), BF16), vm((U, HR, LANES), BF16),
                        vm((U, HR, LANES), F32), vm((U, HR, HR), BF16),
                        vm((U, HR, LANES), F32), vm((U, HR, LANES), BF16),
                        vm((U, HR, LANES), BF16), vm((U, HR, LANES), F32)],
        compiler_params=_cparams(("parallel", "arbitrary")),
        name="gdn",
    )(u3, small3, z3, conv_w, conv_past, s0, gnorm, *masks)


def _kvproj_kernel(lat_ref, kpe_ref, wk_ref, wv_ref, knn_ref, kn_ref, kr_ref, v_ref, *, v_transposed):
    lb = lat_ref[...].astype(BF16)
    kf = _dot(lb, wk_ref[...])
    for h in range(MLA_HEADS):
        sl = slice(h * NOPE_DIM, (h + 1) * NOPE_DIM)
        kn_ref[:, sl] = _rms(kf[:, sl], knn_ref[...]).astype(BF16)
    if v_transposed:
        v_ref[...] = _dot_nt(wv_ref[...], lb).astype(BF16)
    else:
        v_ref[...] = _dot(lb, wv_ref[...]).astype(BF16)
    kp = kpe_ref[...]
    if kp.shape[1] == ROPE_DIM:
        kr_ref[...] = jnp.concatenate([kp, jnp.zeros((kp.shape[0], LANES - ROPE_DIM), F32)], axis=1).astype(BF16)
    else:
        lane = lax.broadcasted_iota(jnp.int32, kp.shape, 1)
        kr_ref[...] = jnp.where(lane < ROPE_DIM, kp, 0.0).astype(BF16)


def _kvproj(lat3, kpe3, pw, tm, v_transposed):
    B, sk, _ = lat3.shape
    hw = MLA_HEADS * V_DIM
    row = lambda n: pl.BlockSpec((None, tm, n), lambda b, i: (b, i, 0))
    if v_transposed:
        vspec, vshape, w_v = pl.BlockSpec((None, hw, tm), lambda b, i: (b, 0, i)), (B, hw, sk), pw['w_vt']
    else:
        vspec, vshape, w_v = row(hw), (B, sk, hw), pw['w_v']
    return pl.pallas_call(
        functools.partial(_kvproj_kernel, v_transposed=v_transposed),
        grid=(B, sk // tm),
        in_specs=[row(KV_RANK), row(kpe3.shape[-1]), _const_spec(pw['w_k'].shape), _const_spec(w_v.shape),
                  _const_spec(pw['knn'].shape)],
        out_specs=[row(MLA_HEADS * NOPE_DIM), row(LANES), vspec],
        out_shape=[jax.ShapeDtypeStruct((B, sk, MLA_HEADS * NOPE_DIM), BF16),
                   jax.ShapeDtypeStruct((B, sk, LANES), BF16), jax.ShapeDtypeStruct(vshape, BF16)],
        compiler_params=_cparams(("parallel", "parallel")),
        name="kvproj",
    )(lat3, kpe3, pw['w_k'], w_v, pw['knn'])


def _last_kblock(qi, tq, tk, P, nk):
    last_key = ((P + qi * tq + tq - 1) // CHUNK) * CHUNK + CHUNK - 1
    return jnp.minimum(last_key // tk, nk - 1)


def _flash_kernel(q_ref, kn_ref, kr_ref, v_ref, o_ref, m_ref, l_ref, acc_ref, *, tq, tk, P, S, nk):
    qi = pl.program_id(1)
    ki = pl.program_id(2)

    @pl.when(ki == 0)
    def _():
        m_ref[...] = jnp.full(m_ref.shape, NEG_BIG, F32)
        l_ref[...] = jnp.zeros(l_ref.shape, F32)
        acc_ref[...] = jnp.zeros(acc_ref.shape, F32)

    @pl.when(ki <= _last_kblock(qi, tq, tk, P, nk))
    def _():
        qpos = P + qi * tq + lax.broadcasted_iota(jnp.int32, (tq, tk), 0)
        kpos = ki * tk + lax.broadcasted_iota(jnp.int32, (tq, tk), 1)
        mask = ((kpos // CHUNK) <= (qpos // CHUNK)) & (kpos < P + S)
        kr = kr_ref[...]
        for h in range(MLA_HEADS):
            qh = q_ref[:, h * Q_SLAB:(h + 1) * Q_SLAB]
            kh = jnp.concatenate([kn_ref[:, h * NOPE_DIM:(h + 1) * NOPE_DIM], kr], axis=1)
            s = jnp.where(mask, _dot_nt(qh, kh), NEG_BIG)
            m_prev = m_ref[h]
            m_new = jnp.maximum(m_prev, jnp.max(s, axis=-1, keepdims=True))
            alpha = jnp.exp(m_prev - m_new)
            p = jnp.exp(s - m_new)
            l_ref[h] = alpha * l_ref[h] + jnp.sum(p, axis=-1, keepdims=True)
            acc_ref[h] = alpha * acc_ref[h] + _dot(p.astype(BF16), v_ref[:, h * V_DIM:(h + 1) * V_DIM])
            m_ref[h] = m_new

    @pl.when(ki == nk - 1)
    def _():
        for h in range(MLA_HEADS):
            o_ref[:, h * V_DIM:(h + 1) * V_DIM] = (acc_ref[h] / l_ref[h]).astype(BF16)


def _flash(q3, kn3, kr3, v3, P, S, tq, tk):
    B = q3.shape[0]
    skp = kn3.shape[1]
    nk = skp // tk
    kmap = lambda b, qi, ki: (b, jnp.minimum(ki, _last_kblock(qi, tq, tk, P, nk)), 0)
    return pl.pallas_call(
        functools.partial(_flash_kernel, tq=tq, tk=tk, P=P, S=S, nk=nk),
        grid=(B, S // tq, nk),
        in_specs=[pl.BlockSpec((None, tq, MLA_HEADS * Q_SLAB), lambda b, qi, ki: (b, qi, 0)),
                  pl.BlockSpec((None, tk, MLA_HEADS * NOPE_DIM), kmap),
                  pl.BlockSpec((None, tk, LANES), kmap),
                  pl.BlockSpec((None, tk, MLA_HEADS * V_DIM), kmap)],
        out_specs=pl.BlockSpec((None, tq, MLA_HEADS * V_DIM), lambda b, qi, ki: (b, qi, 0)),
        out_shape=jax.ShapeDtypeStruct((B, S, MLA_HEADS * V_DIM), BF16),
        scratch_shapes=[pltpu.VMEM((MLA_HEADS, tq, 1), F32), pltpu.VMEM((MLA_HEADS, tq, 1), F32),
                        pltpu.VMEM((MLA_HEADS, tq, V_DIM), F32)],
        compiler_params=_cparams(("parallel", "parallel", "arbitrary")),
        name="mla_attn",
    )(q3, kn3, kr3, v3)


def _attn_dec_kernel(q_ref, knp_ref, krp_ref, vp_ref, knn_ref, krn_ref, vn_ref, o_ref):
    krp, krn = krp_ref[...], krn_ref[...]
    hsl = [slice(h * NOPE_DIM, (h + 1) * NOPE_DIM) for h in range(MLA_HEADS)]
    qs = [q_ref[:, h * Q_SLAB:(h + 1) * Q_SLAB] for h in range(MLA_HEADS)]
    sps = [_dot_nt(q, jnp.concatenate([knp_ref[:, sl], krp], axis=1)) for q, sl in zip(qs, hsl)]
    sns = [_dot_nt(q, jnp.concatenate([knn_ref[:, sl], krn], axis=1)) for q, sl in zip(qs, hsl)]
    for h, (sp, sn) in enumerate(zip(sps, sns)):
        vsl = slice(h * V_DIM, (h + 1) * V_DIM)
        m = jnp.maximum(jnp.max(sp, axis=-1, keepdims=True), jnp.max(sn, axis=-1, keepdims=True))
        pp = jnp.exp(sp - m)
        pn = jnp.exp(sn - m)
        l = jnp.sum(pp, axis=-1, keepdims=True) + jnp.sum(pn, axis=-1, keepdims=True)
        o = _dot(pp.astype(BF16), vp_ref[:, vsl]) + _dot(pn.astype(BF16), vn_ref[:, vsl])
        o_ref[:, vsl] = (o / l).astype(BF16)


def _attn_dec(q3, past, new):
    B, S, _ = q3.shape
    specs = [pl.BlockSpec((None, a.shape[1], a.shape[2]), lambda b: (b, 0, 0)) for a in (q3, *past, *new)]
    return pl.pallas_call(
        _attn_dec_kernel,
        grid=(B,),
        in_specs=specs,
        out_specs=pl.BlockSpec((None, S, MLA_HEADS * V_DIM), lambda b: (b, 0, 0)),
        out_shape=jax.ShapeDtypeStruct((B, S, MLA_HEADS * V_DIM), BF16),
        compiler_params=_cparams(("parallel",)),
        name="mla_attn_dec",
    )(q3, *past, *new)


def _flash_t_kernel(qt_ref, kt_ref, lt_ref, q_ref, kn_ref, kr_ref, vt_ref, o_ref, m_ref, acc_ref, *, tq, tk, P, S):
    j = pl.program_id(1)
    qi = qt_ref[j]
    ki = kt_ref[j]
    q0 = P + qi * tq
    k0 = ki * tk

    @pl.when(ki == 0)
    def _():
        m_ref[...] = jnp.full(m_ref.shape, NEG_BIG, F32)
        acc_ref[...] = jnp.zeros(acc_ref.shape, F32)

    def step(masked):
        kr = kr_ref[...]
        ones = jnp.ones((ONES_ROWS, tk), BF16)
        if masked:
            kpos = k0 + lax.broadcasted_iota(jnp.int32, (tk, 1), 0)
            qpos = q0 + lax.broadcasted_iota(jnp.int32, (1, tq), 1)
            mask = ((kpos // CHUNK) <= (qpos // CHUNK)) & (kpos < P + S)
        sts = []
        for h in range(MLA_HEADS):
            kh = jnp.concatenate([kn_ref[:, h * NOPE_DIM:(h + 1) * NOPE_DIM], kr], axis=1)
            sts.append(_dot_nt(kh, q_ref[:, h * Q_SLAB:(h + 1) * Q_SLAB]))
        ps, alphas = [], []
        for h in range(MLA_HEADS):
            st = sts[h]
            if masked:
                st = jnp.where(mask, st, NEG_BIG)
            m_prev = m_ref[h]
            m_new = jnp.maximum(m_prev, jnp.max(st, axis=0, keepdims=True))
            alphas.append(jnp.exp(m_prev - m_new))
            ps.append(jnp.exp(st - m_new).astype(BF16))
            m_ref[h] = m_new
        for h in range(MLA_HEADS):
            vt1 = jnp.concatenate([vt_ref[h * V_DIM:(h + 1) * V_DIM, :], ones], axis=0)
            acc_ref[h] = alphas[h] * acc_ref[h] + _dot(vt1, ps[h])

    full = ((k0 + tk - 1) // CHUNK <= q0 // CHUNK) & (k0 + tk <= P + S)
    pl.when(full)(functools.partial(step, False))
    pl.when(jnp.logical_not(full))(functools.partial(step, True))

    @pl.when(lt_ref[j] == 1)
    def _():
        for h in range(MLA_HEADS):
            acc = acc_ref[h]
            o_ref[:, h * V_DIM:(h + 1) * V_DIM] = (acc[:V_DIM] / acc[V_DIM:V_DIM + 1]).T.astype(BF16)


def _flash_t(q3, kn3, kr3, vt3, P, S, tq, tk):
    B = q3.shape[0]
    nk = kn3.shape[1] // tk
    pairs = []
    for qi in range(S // tq):
        last = min((((P + qi * tq + tq - 1) // CHUNK) * CHUNK + CHUNK - 1) // tk, nk - 1)
        pairs += [(qi, ki, int(ki == last)) for ki in range(last + 1)]
    qt, kt, lt = (jnp.asarray(np.array(col, np.int32)) for col in zip(*pairs))
    kmap = lambda b, j, qt, kt, lt: (b, kt[j], 0)
    gs = pltpu.PrefetchScalarGridSpec(
        num_scalar_prefetch=3,
        grid=(B, len(pairs)),
        in_specs=[pl.BlockSpec((None, tq, MLA_HEADS * Q_SLAB), lambda b, j, qt, kt, lt: (b, qt[j], 0)),
                  pl.BlockSpec((None, tk, MLA_HEADS * NOPE_DIM), kmap),
                  pl.BlockSpec((None, tk, LANES), kmap),
                  pl.BlockSpec((None, MLA_HEADS * V_DIM, tk), lambda b, j, qt, kt, lt: (b, 0, kt[j]))],
        out_specs=pl.BlockSpec((None, tq, MLA_HEADS * V_DIM), lambda b, j, qt, kt, lt: (b, qt[j], 0)),
        scratch_shapes=[pltpu.VMEM((MLA_HEADS, 1, tq), F32), pltpu.VMEM((MLA_HEADS, V_DIM + ONES_ROWS, tq), F32)],
    )
    return pl.pallas_call(
        functools.partial(_flash_t_kernel, tq=tq, tk=tk, P=P, S=S),
        grid_spec=gs,
        out_shape=jax.ShapeDtypeStruct((B, S, MLA_HEADS * V_DIM), BF16),
        compiler_params=_cparams(("parallel", "arbitrary")),
        name="mla_attn_t",
    )(qt, kt, lt, q3, kn3, kr3, vt3)


def _memkv_kernel(mem_ref, mnorm_ref, wmk_ref, wmv_ref, mkn_ref, k_ref, v_ref):
    mb = _rms(mem_ref[...], mnorm_ref[...]).astype(BF16)
    kf = _dot(mb, wmk_ref[...])
    for h in range(MEM_HEADS):
        sl = slice(h * MEM_HEAD_DIM, (h + 1) * MEM_HEAD_DIM)
        k_ref[:, sl] = _rms(kf[:, sl], mkn_ref[...])
    v_ref[...] = _dot(mb, wmv_ref[...])


def _memkv(mem2d, mnorm, w_mk, w_mv, mkn, tm):
    T = mem2d.shape[0]
    hw = MEM_HEADS * MEM_HEAD_DIM
    row = lambda n: pl.BlockSpec((tm, n), lambda i: (i, 0))
    return pl.pallas_call(
        _memkv_kernel,
        grid=(T // tm,),
        in_specs=[row(D_MODEL), _const_spec(mnorm.shape), _const_spec(w_mk.shape), _const_spec(w_mv.shape),
                  _const_spec(mkn.shape)],
        out_specs=[row(hw), row(hw)],
        out_shape=[jax.ShapeDtypeStruct((T, hw), F32), jax.ShapeDtypeStruct((T, hw), F32)],
        compiler_params=_cparams(("parallel",)),
        name="mem_kv",
    )(mem2d, mnorm, w_mk, w_mv, mkn)


def _mix_kernel(x_ref, og_ref, om_ref, mk_ref, mv_ref, wout_ref, nmem_ref, wmq_ref, mqn_ref, wmo_ref, nffn_ref,
                wrh_ref, wrl_ref, br_ref, x2_ref, h3_ref, idx_ref, gate_ref, counts_ref, cnt_ref):
    def mem_head(ref, b, h):
        if len(ref.shape) == 3:
            return ref[b, :, h * MEM_HEAD_DIM:(h + 1) * MEM_HEAD_DIM].astype(BF16)
        return ref[b, :, h, :].astype(BF16)

    first = (pl.program_id(0) == 0) & (pl.program_id(1) == 0)

    @pl.when(first)
    def _():
        cnt_ref[...] = jnp.zeros(cnt_ref.shape, F32)

    nbm, tm, _ = x_ref.shape
    rows = nbm * tm
    tiles = [slice(0, rows)]
    take = (lambda ref, rs: ref[0, rs, :]) if nbm == 1 else (lambda ref, rs: ref[...].reshape(rows, ref.shape[-1]))
    logit_parts = []
    for rs in tiles:
        x1 = take(x_ref, rs) + _dot(take(og_ref, rs), wout_ref[0:GDN_VW, :]) + _dot(take(om_ref, rs),
                                                                                    wout_ref[GDN_VW:, :])
        hb = _rms(x1, nmem_ref[...]).astype(BF16)
        qm = _dot(hb, wmq_ref[...])
        per_batch = []
        hsl = [slice(h * MEM_HEAD_DIM, (h + 1) * MEM_HEAD_DIM) for h in range(MEM_HEADS)]
        for b in range(nbm):
            br = slice(b * tm, (b + 1) * tm) if nbm > 1 else slice(None)
            qhs = [(_rms(qm[br, sl], mqn_ref[...]) * (MEM_HEAD_DIM ** -0.5)).astype(BF16) for sl in hsl]
            ss = [_dot_nt(qh, mem_head(mk_ref, b, h)) for h, qh in enumerate(qhs)]
            ps = []
            for s in ss:
                p = jnp.exp(s - jnp.max(s, axis=-1, keepdims=True))
                ps.append((p / jnp.sum(p, axis=-1, keepdims=True)).astype(BF16))
            heads = [_dot(p, mem_head(mv_ref, b, h)).astype(BF16) for h, p in enumerate(ps)]
            per_batch.append(jnp.concatenate(heads, axis=1))
        om = per_batch[0] if nbm == 1 else jnp.concatenate(per_batch, axis=0)
        x2 = x1 + _dot(om, wmo_ref[...])
        if nbm == 1:
            x2_ref[0, rs, :] = x2
        else:
            x2_ref[...] = x2.reshape(x2_ref.shape)
        h3 = _rms(x2, nffn_ref[...])
        hi = h3.astype(BF16)
        packed = _pack_bf16_pairs(hi.astype(F32))
        for c in range(h3_ref.shape[0]):
            h3_ref[c, rs, :] = packed[:, c * LANES:(c + 1) * LANES]
        lo = (h3 - hi.astype(F32)).astype(BF16)
        wrh = wrh_ref[...]
        logit_parts.append(_dot(hi, wrh) + _dot(lo, wrh) + _dot(hi, wrl_ref[...]) + br_ref[...])
    if True:
        logits = logit_parts[0] if len(logit_parts) == 1 else jnp.concatenate(logit_parts, axis=0)

        lane = lax.broadcasted_iota(jnp.int32, logits.shape, 1).astype(F32)
        vals, idxs = [], []
        cur = logits
        for _ in range(TOP_K):
            mx = jnp.max(cur, axis=-1, keepdims=True)
            ix = jnp.min(jnp.where(cur == mx, lane, float(LANES)), axis=-1, keepdims=True)
            vals.append(mx)
            idxs.append(ix)
            cur = jnp.where(lane == ix, -3e38, cur)
        es = [jnp.exp(v - vals[0]) for v in vals]
        den = es[0] + es[1] + es[2] + es[3]

        sel = jnp.zeros(logits.shape, F32)
        for k in range(TOP_K):
            sel = sel + jnp.where(lane == idxs[k], 1.0, 0.0)
        ri = lax.broadcasted_iota(jnp.int32, (rows, rows), 0)
        ci = lax.broadcasted_iota(jnp.int32, (rows, rows), 1)
        before = jnp.where(ri > ci, 1.0, 0.0).astype(BF16)
        excl = _dot(before, sel.astype(BF16)) + cnt_ref[...]
        cnt_ref[...] = cnt_ref[...] + jnp.sum(sel, axis=0, keepdims=True)
        counts_ref[...] = cnt_ref[...].astype(jnp.int32)

        idx_out = jnp.zeros(logits.shape, F32)
        gate_out = jnp.zeros(logits.shape, F32)
        for k in range(TOP_K):
            rank = jnp.sum(jnp.where(lane == idxs[k], excl, 0.0), axis=-1, keepdims=True)
            idx_out = jnp.where(lane == float(k), idxs[k], idx_out)
            idx_out = jnp.where(lane == float(TOP_K + k), rank, idx_out)
            gate_out = jnp.where(lane == float(k), es[k] / den, gate_out)
        idx_ref[...] = idx_out.astype(jnp.int32).reshape(idx_ref.shape)
        gate_ref[...] = gate_out.reshape(gate_ref.shape)


def _mix(x3, og3, om3, mk3, mv3, pw, tm, b0, B):
    S = x3.shape[1]
    hw = MEM_HEADS * MEM_HEAD_DIM
    nbm = _pick(B, tuple(n for n in (8, 4, 2, 1) if n * tm <= MIX_ROWS and b0 % n == 0)) if tm == S else 1
    nsb = S // tm
    boff = b0 // nbm
    tile_in = lambda n: pl.BlockSpec((nbm, tm, n), lambda b, i: (b + boff, i, 0))
    tile = lambda n: pl.BlockSpec((nbm, tm, n), lambda b, i: (b, i, 0))
    if mk3.ndim == 3:
        memspec = pl.BlockSpec((nbm, N_MEM, hw), lambda b, i: (b + boff, 0, 0))
    else:
        memspec = pl.BlockSpec((nbm, N_MEM, MEM_HEADS, MEM_HEAD_DIM), lambda b, i: (b + boff, 0, 0, 0))
    consts = [pw['w_out'], pw['nmem'], pw['w_mq'], pw['mqn'], pw['w_mo'], pw['nffn'], pw['wr_hi'], pw['wr_lo'],
              pw['b_r']]
    return pl.pallas_call(
        _mix_kernel,
        grid=(B // nbm, nsb),
        in_specs=[tile_in(D_MODEL), tile_in(GDN_VW), tile_in(MLA_HEADS * V_DIM), memspec, memspec]
                 + [_const_spec(c.shape) for c in consts],
        out_specs=[tile(D_MODEL), pl.BlockSpec((PK_CHUNKS, nbm * tm, LANES), lambda b, i: (0, b * nsb + i, 0)),
                   tile(LANES), tile(LANES), _const_spec((1, LANES))],
        out_shape=[jax.ShapeDtypeStruct((B, S, D_MODEL), F32),
                   jax.ShapeDtypeStruct((PK_CHUNKS, B * S, LANES), jnp.uint32),
                   jax.ShapeDtypeStruct((B, S, LANES), jnp.int32), jax.ShapeDtypeStruct((B, S, LANES), F32),
                   jax.ShapeDtypeStruct((1, LANES), jnp.int32)],
        scratch_shapes=[pltpu.VMEM((1, LANES), F32)],
        compiler_params=_cparams(("arbitrary", "arbitrary")),
        name="mix_mem_router",
    )(x3, og3, om3, mk3, mv3, *consts)


def _expert_kernel(be_ref, nx_ref, nv_ref, nu_ref, rows_ref, wgu_hbm, bgu_ref, wd_hbm, bd_ref, y_ref,
                   wgus_ref, wds_ref, wgub_ref, wdb_ref, sem_ref):
    i = pl.program_id(0)
    used = i < nu_ref[0]
    new_expert = (i == 0) | (be_ref[i] != be_ref[jnp.maximum(i - 1, 0)])

    def weight_copies(e):
        return (pltpu.make_async_copy(wgu_hbm.at[e], wgus_ref, sem_ref.at[0]),
                pltpu.make_async_copy(wd_hbm.at[e], wds_ref, sem_ref.at[1]))

    @pl.when(i == 0)
    def _():
        for cp in weight_copies(be_ref[0]):
            cp.start()

    @pl.when(used & new_expert)
    def _():
        for cp in weight_copies(be_ref[i]):
            cp.wait()

        def cast(r, carry):
            rs = pl.ds(pl.multiple_of(r * LANES, LANES), LANES)
            wgub_ref[rs, :] = wgus_ref[rs, :].astype(BF16)
            wdb_ref[rs, :] = wds_ref[rs, :].astype(BF16)
            return carry
        lax.fori_loop(0, D_MODEL // LANES, cast, 0)

        @pl.when(nx_ref[i] >= 0)
        def _():
            for cp in weight_copies(nx_ref[i]):
                cp.start()

    def ffn(nrows):
        packed = jnp.concatenate([rows_ref[c, :nrows, :] for c in range(PK_CHUNKS)], axis=1)
        x = _unpack_bf16_pairs(packed).astype(BF16)
        acc = None
        for c in range(D_FF // FF_CHUNK):
            gs_ = slice(c * FF_CHUNK, (c + 1) * FF_CHUNK)
            us_ = slice(D_FF + c * FF_CHUNK, D_FF + (c + 1) * FF_CHUNK)
            gt = jnp.minimum(_dot(x, wgub_ref[:, gs_]) + bgu_ref[:, gs_], SWIGLU_LIMIT)
            up = jnp.clip(_dot(x, wgub_ref[:, us_]) + bgu_ref[:, us_], -SWIGLU_LIMIT, SWIGLU_LIMIT)
            act = gt * _sigmoid(SWIGLU_ALPHA * gt) * (up + 1.0)
            part = _dot(act.astype(BF16), wdb_ref[gs_, :])
            acc = part if acc is None else acc + part
        ypk = _pack_bf16_pairs((acc + bd_ref[...]).astype(BF16).astype(F32))
        for c in range(PK_CHUNKS):
            y_ref[c, :nrows, :] = ypk[:, c * LANES:(c + 1) * LANES]
        if nrows < MOE_ROWS:
            y_ref[:, nrows:, :] = jnp.zeros((PK_CHUNKS, MOE_ROWS - nrows, LANES), y_ref.dtype)

    half_full = nv_ref[i] <= MOE_ROWS // 2
    pl.when(used & jnp.logical_not(half_full))(functools.partial(ffn, MOE_ROWS))
    pl.when(used & half_full)(functools.partial(ffn, MOE_ROWS // 2))

    @pl.when(jnp.logical_not(used))
    def _():
        y_ref[...] = jnp.zeros(y_ref.shape, y_ref.dtype)


def _experts(block_e, next_e, block_rows, n_used, rows, w_gu, b_gu, w_down, b_down):
    n_rows = rows.shape[1]
    nb = n_rows // MOE_ROWS
    gs = pltpu.PrefetchScalarGridSpec(
        num_scalar_prefetch=4,
        grid=(nb,),
        in_specs=[pl.BlockSpec((PK_CHUNKS, MOE_ROWS, LANES), lambda i, be, nx, nv, nu: (0, i, 0)),
                  pl.BlockSpec(memory_space=pl.ANY),
                  pl.BlockSpec((None, 1, 2 * D_FF), lambda i, be, nx, nv, nu: (be[i], 0, 0)),
                  pl.BlockSpec(memory_space=pl.ANY),
                  pl.BlockSpec((None, 1, D_MODEL), lambda i, be, nx, nv, nu: (be[i], 0, 0))],
        out_specs=pl.BlockSpec((PK_CHUNKS, MOE_ROWS, LANES), lambda i, be, nx, nv, nu: (0, i, 0)),
        scratch_shapes=[pltpu.VMEM((D_MODEL, 2 * D_FF), F32), pltpu.VMEM((D_FF, D_MODEL), F32),
                        pltpu.VMEM((D_MODEL, 2 * D_FF), BF16), pltpu.VMEM((D_FF, D_MODEL), BF16),
                        pltpu.SemaphoreType.DMA((2,))],
    )
    return pl.pallas_call(
        _expert_kernel,
        grid_spec=gs,
        out_shape=jax.ShapeDtypeStruct((PK_CHUNKS, n_rows, LANES), jnp.uint32),
        compiler_params=_cparams(("arbitrary",)),
        name="moe_experts",
    )(block_e, next_e, block_rows, n_used, rows, w_gu, b_gu, w_down, b_down)


def _sc_mesh():
    return plsc.VectorSubcoreMesh(core_axis_name="core", subcore_axis_name="subcore")


def _sc_scatter_rows(x3s, pos_ts, n_rows):
    C, _, L = x3s[0].shape
    K = pos_ts[0].shape[0]
    ns = len(x3s)

    @functools.partial(pl.kernel, out_type=jax.ShapeDtypeStruct((C, n_rows, L), x3s[0].dtype), mesh=_sc_mesh(),
                       scratch_types=[])
    def scatter(*refs):
        o_hbm = refs[2 * ns]
        for s in range(ns):
            x_hbm, i_hbm = refs[s], refs[ns + s]
            nwin = x3s[s].shape[1] // SC_WINDOW
            for c in range(C):
                def body(x_vmem, i_vmem, c=c):
                    for k in range(K):
                        pltpu.sync_copy(x_vmem, o_hbm.at[c].at[i_vmem.at[k]])

                pltpu.emit_pipeline(
                    body, grid=(nwin,),
                    in_specs=[pl.BlockSpec((SC_WINDOW, L), lambda i, c=c, nwin=nwin: (c * nwin + i, 0)),
                              pl.BlockSpec((K, SC_WINDOW), lambda i: (0, i))],
                    out_specs=[], core_axis_name=("core", "subcore"), dimension_semantics=(pltpu.PARALLEL,),
                )(x_hbm, i_hbm)

    return scatter(*[x.reshape(-1, L) for x in x3s], *pos_ts)


def _sc_gather_rows(table3, idxs):
    C, _, L = table3.shape
    ns = len(idxs)
    out_type = [jax.ShapeDtypeStruct((C * i.shape[0], L), table3.dtype) for i in idxs]

    @functools.partial(pl.kernel, out_type=out_type, mesh=_sc_mesh(), scratch_types=[])
    def gather(t_hbm, *refs):
        for s in range(ns):
            i_hbm, o_hbm = refs[s], refs[ns + s]
            nwin = idxs[s].shape[0] // SC_WINDOW
            for c in range(C):
                def body(i_vmem, o_vmem, c=c):
                    pltpu.sync_copy(t_hbm.at[c].at[i_vmem.at[0]], o_vmem)

                pltpu.emit_pipeline(
                    body, grid=(nwin,),
                    in_specs=[pl.BlockSpec((1, SC_WINDOW), lambda i: (0, i))],
                    out_specs=[pl.BlockSpec((SC_WINDOW, L), lambda i, c=c, nwin=nwin: (c * nwin + i, 0))],
                    core_axis_name=("core", "subcore"), dimension_semantics=(pltpu.PARALLEL,),
                )(i_hbm, o_hbm)

    outs = gather(table3, *[i.reshape(1, -1) for i in idxs])
    return [o.reshape(C, -1, L) for o in outs]


def _combine_kernel(x2_ref, g_ref, gate_ref, *rest):
    o_ref = rest[-1]
    gate = gate_ref[...]
    half = D_MODEL // 2
    for c in range(PK_CHUNKS):
        lo_s = slice(c * LANES, (c + 1) * LANES)
        hi_s = slice(half + c * LANES, half + (c + 1) * LANES)
        acc_lo = x2_ref[:, lo_s]
        acc_hi = x2_ref[:, hi_s]
        for k in range(TOP_K):
            w = g_ref[c, k]
            gk = gate[:, k:k + 1]
            acc_lo = acc_lo + pltpu.bitcast(w << 16, F32) * gk
            acc_hi = acc_hi + pltpu.bitcast(w & jnp.uint32(0xFFFF0000), F32) * gk
        o_ref[:, lo_s] = acc_lo
        o_ref[:, hi_s] = acc_hi


def _combine(x2, g4, gate, tm, out_buf=None, row0=0, t_total=None):
    T = x2.shape[0]
    t_total = T if t_total is None else t_total
    blk0 = row0 // tm
    in_specs = [pl.BlockSpec((tm, D_MODEL), lambda i: (i, 0)),
                pl.BlockSpec((PK_CHUNKS, TOP_K, tm, LANES), lambda i: (0, 0, i, 0)),
                pl.BlockSpec((tm, LANES), lambda i: (i, 0))]
    args = [x2, g4, gate]
    aliases = {}
    if out_buf is not None:
        in_specs.append(pl.BlockSpec(memory_space=pl.ANY))
        args.append(out_buf)
        aliases = {3: 0}
    return pl.pallas_call(
        _combine_kernel,
        grid=(T // tm,),
        in_specs=in_specs,
        out_specs=pl.BlockSpec((tm, D_MODEL), lambda i: (i + blk0, 0)),
        out_shape=jax.ShapeDtypeStruct((t_total, D_MODEL), F32),
        input_output_aliases=aliases,
        compiler_params=_cparams(("parallel",)),
        name="moe_combine",
    )(*args)


def _plan_kernel(idxr_ref, base_ref, pos_ref):
    idxr = idxr_ref[...].astype(F32)
    base = base_ref[...].astype(F32)
    lane = lax.broadcasted_iota(jnp.int32, idxr.shape, 1).astype(F32)
    out = jnp.zeros(idxr.shape, F32)
    for k in range(TOP_K):
        start = jnp.sum(jnp.where(lane == idxr[:, k:k + 1], base, 0.0), axis=-1, keepdims=True)
        out = jnp.where(lane == float(k), start + idxr[:, TOP_K + k:TOP_K + k + 1], out)
    pos_ref[...] = out.T[:TOP_K, :].astype(jnp.int32)


def _plan(idxr, base):
    T = idxr.shape[0]
    tm = _pick(T, (512, 256, 128))
    return pl.pallas_call(
        _plan_kernel,
        grid=(T // tm,),
        in_specs=[pl.BlockSpec((tm, LANES), lambda i: (i, 0)), _const_spec((1, LANES))],
        out_specs=pl.BlockSpec((TOP_K, tm), lambda i: (0, i)),
        out_shape=jax.ShapeDtypeStruct((TOP_K, T), jnp.int32),
        compiler_params=_cparams(("parallel",)),
        name="moe_plan",
    )(idxr, base)


def _moe(streams, places, ew):
    cnts = [st[3][0, :N_EXPERTS] for st in streams]
    total = sum(cnts)
    padded = (total + MOE_ROWS - 1) // MOE_ROWS * MOE_ROWS
    pad_end = jnp.cumsum(padded)
    pad_start = pad_end - padded
    pos_ts = []
    base = pad_start
    for (h3p, idxr, gate, counts, x2), cnt in zip(streams, cnts):
        pos_ts.append(_plan(idxr, _pad_lanes(base)))
        base = base + cnt
    n_assign = sum(st[0].shape[1] for st in streams) * TOP_K
    nb = -(-n_assign // MOE_ROWS) + N_EXPERTS
    starts = jnp.arange(nb, dtype=jnp.int32) * MOE_ROWS
    block_e = jnp.minimum(jnp.sum((pad_end[None, :] <= starts[:, None]).astype(jnp.int32), axis=1), N_EXPERTS - 1)
    n_used = (pad_end[-1] // MOE_ROWS).astype(jnp.int32).reshape(1)
    ar = jnp.arange(N_EXPERTS, dtype=jnp.int32)
    later = (padded > 0)[None, :] & (ar[None, :] > ar[:, None])
    nxt = jnp.min(jnp.where(later, ar[None, :], N_EXPERTS), axis=1)
    nxt = jnp.where(nxt >= N_EXPERTS, -1, nxt)
    mine = block_e[:, None] == ar[None, :]
    next_e = jnp.sum(jnp.where(mine, nxt[None, :], 0), axis=1).astype(jnp.int32)
    seg_end = jnp.sum(jnp.where(mine, (pad_start + total)[None, :], 0), axis=1)
    block_rows = jnp.clip(seg_end - starts, 0, MOE_ROWS).astype(jnp.int32)
    rows = _sc_scatter_rows([st[0] for st in streams], pos_ts, nb * MOE_ROWS)
    y_rows = _experts(block_e, next_e, block_rows, n_used, rows, ew['w_gu'], ew['b_gu'], ew['w_down'], ew['b_down'])
    gs = _sc_gather_rows(y_rows, [p.reshape(-1) for p in pos_ts])
    outs = []
    for (h3p, idxr, gate, counts, x2), g, place in zip(streams, gs, places):
        T = x2.shape[0]
        outs.append(_combine(x2, g.reshape(PK_CHUNKS, TOP_K, T, LANES), gate, _pick(T, (512, 256, 128, 64)), *place))
    return outs


def _pad_lanes(v, n=LANES, fill=0.0):
    return jnp.pad(v, (0, n - v.shape[0]), constant_values=fill).reshape(1, n)


def _prep_weights(norm_mix, w_in, conv_w, a_log, dt_bias, gdn_norm, q_a_norm, w_qb, kv_a_norm, w_kvb, q_norm,
                  k_nope_norm, k_rope_norm, w_out, norm_mem, mem_norm, w_mq, w_mk, w_mv, mq_norm, mk_norm, w_mo,
                  norm_ffn, w_router, b_router, w_gu, b_gu, w_down, b_down):
    c = np.cumsum([CONV_DIM, GDN_VW, GDN_HEADS, GDN_HEADS, Q_RANK, KV_RANK])
    w_u, w_z, w_a, w_b, w_cq, w_ckv, w_kpe = [w_in[:, lo:hi] for lo, hi in
                                              zip([0, *c], [*c, w_in.shape[1]])]
    w_s = jnp.concatenate([w_kpe, w_a, w_b], axis=1)
    w_s = jnp.pad(w_s, ((0, 0), (0, LANES - w_s.shape[1])))
    wq = w_qb.reshape(Q_RANK, MLA_HEADS, QK_DIM)
    wq = jnp.pad(wq, ((0, 0), (0, 0), (0, Q_SLAB - QK_DIM))).reshape(Q_RANK, MLA_HEADS * Q_SLAB)
    wkv = w_kvb.reshape(KV_RANK, MLA_HEADS, NOPE_DIM + V_DIM)
    w_k = wkv[:, :, :NOPE_DIM].reshape(KV_RANK, -1).astype(BF16)
    w_v = wkv[:, :, NOPE_DIM:].reshape(KV_RANK, -1).astype(BF16)
    wr = jnp.pad(w_router, ((0, 0), (0, LANES - N_EXPERTS)))
    wr_hi = wr.astype(BF16)
    wr_lo = (wr - wr_hi.astype(F32)).astype(BF16)
    row = lambda v: v.reshape(1, -1)
    gpad = ROPE_DIM
    pw = dict(
        nmix=row(norm_mix), w_u=w_u.astype(BF16), w_z=w_z.astype(BF16), w_cq=w_cq.astype(BF16),
        w_ckv=w_ckv.astype(BF16), w_s=w_s.astype(BF16), qan=row(q_a_norm), w_qb=wq.astype(BF16),
        qn=_pad_lanes(q_norm, Q_SLAB), kvan=row(kv_a_norm), krn=_pad_lanes(k_rope_norm),
        alog=jnp.pad(a_log, (gpad, LANES - gpad - GDN_HEADS)).reshape(1, LANES),
        dtb=jnp.pad(dt_bias, (gpad, LANES - gpad - GDN_HEADS)).reshape(1, LANES),
        conv_w=conv_w, gnorm=row(gdn_norm), w_k=w_k, w_v=w_v, w_vt=w_v.T, knn=row(k_nope_norm),
        w_out=w_out.astype(BF16), nmem=row(norm_mem), w_mq=w_mq.astype(BF16), mqn=row(mq_norm),
        w_mo=w_mo.astype(BF16), nffn=row(norm_ffn), wr_hi=wr_hi, wr_lo=wr_lo,
        b_r=_pad_lanes(b_router, LANES, NEG_BIG),
        mnorm=row(mem_norm), w_mk=w_mk.astype(BF16), w_mv=w_mv.astype(BF16), mkn=row(mk_norm),
    )
    ew = dict(w_gu=w_gu, b_gu=b_gu.reshape(N_EXPERTS, 1, 2 * D_FF), w_down=w_down,
              b_down=b_down.reshape(N_EXPERTS, 1, D_MODEL))
    return pw, ew


def _rope_tables(P, S):
    half = ROPE_DIM // 2
    inv = ROPE_THETA ** (-jnp.arange(half, dtype=F32) / half)
    ang = (P + jnp.arange(S, dtype=jnp.int32)).astype(F32)[:, None] * inv[None, :]
    cos, sin = jnp.cos(ang), jnp.sin(ang)
    zh = jnp.zeros((S, half), F32)
    zz = jnp.zeros((S, LANES - ROPE_DIM), F32)
    return (jnp.concatenate([cos, cos, zz], 1), jnp.concatenate([-sin, zh, zz], 1),
            jnp.concatenate([zh, sin, zz], 1))


def _pick(n, prefs):
    for t in prefs:
        if n % t == 0:
            return t
    return n


def _trunk_front(x, lat_past, kpe_past, s0, conv_past, mem_k, mem_v, pw, n_groups):
    B, S, D = x.shape
    P = lat_past.shape[1]
    T = B * S
    tm = _pick(S, (512, 256, 128, 64))
    tm_in = 512 if (T % 512 == 0 and (512 % S == 0 or S % 512 == 0)) else tm
    u, z, q, lat_new, small = _inproj(x.reshape(T, D), S, tm_in, pw, _rope_tables(P, S))

    LC = _pick(S, (256, 128, 64))
    NB = _pick(B, tuple(n for n in (8, 4, 2, 1) if n * (LC // CHUNK) <= GDN_UNITS))
    o_gdn, s_new, conv_new = _gdn(u.reshape(B, S, CONV_DIM), small.reshape(B, S, LANES), z.reshape(B, S, GDN_VW),
                                  pw['conv_w'], conv_past, s0, pw['gnorm'], NB, LC)

    kpe_new = small[:, :ROPE_DIM].reshape(B, S, ROPE_DIM)
    lat3 = lat_new.reshape(B, S, KV_RANK)
    small3 = small.reshape(B, S, LANES)
    q3 = q.reshape(B, S, MLA_HEADS * Q_SLAB)
    tq = _pick(S, (512, 256, 128, 64))
    if P == 0 and S % 512 == 0:
        kn, kr, vt = _kvproj(lat3, small3, pw, 512, True)
        o_mla = _flash_t(q3, kn, kr, vt, P, S, tq, 512)
    elif P > 0 and P % CHUNK == 0 and S == CHUNK:
        past = _kvproj(lat_past, kpe_past, pw, _pick(P, (2048, 1024, 512)), False)
        new = _kvproj(lat3, small3, pw, S, False)
        o_mla = _attn_dec(q3, past, new)
    else:
        sk = P + S
        tk = 512 if S >= 512 else -(-sk // LANES) * LANES
        skp = -(-sk // tk) * tk
        lat_all = jnp.pad(jnp.concatenate([lat_past, lat3], axis=1), ((0, 0), (0, skp - sk), (0, 0)))
        kpe_all = jnp.pad(jnp.concatenate([kpe_past, kpe_new], axis=1), ((0, 0), (0, skp - sk), (0, 0)))
        key_major = tq >= LANES
        kn, kr, v = _kvproj(lat_all, kpe_all, pw, _pick(skp, (512,)), key_major)
        o_mla = (_flash_t if key_major else _flash)(q3, kn, kr, v, P, S, tq, tk)

    streams = []
    gb = B // n_groups
    for gi in range(n_groups):
        x2, h3p, idxr, gate, counts = _mix(x, o_gdn, o_mla, mem_k, mem_v, pw, tm, gi * gb, gb)
        tg = gb * S
        streams.append((h3p, idxr.reshape(tg, LANES), gate.reshape(tg, LANES), counts, x2.reshape(tg, D)))
    return streams, (lat_new.reshape(B, S, KV_RANK), kpe_new, s_new, conv_new)


def kernel(x_prompt, x_sample, cache_kv_latent, cache_k_rope, state_gdn, state_conv, cache_mem_k, cache_mem_v, mem_prompt, norm_mix, w_in, conv_w, a_log, dt_bias, gdn_norm, q_a_norm, w_qb, kv_a_norm, w_kvb, q_norm, k_nope_norm, k_rope_norm, w_out, norm_mem, mem_norm, w_mq, w_mk, w_mv, mq_norm, mk_norm, w_mo, norm_ffn, w_router, b_router, w_gu, b_gu, w_down, b_down):
    depth = norm_mix.shape[0]
    yp, ys = x_prompt, x_sample
    bp = x_prompt.shape[0]
    hw = MEM_HEADS * MEM_HEAD_DIM
    outs = [[] for _ in range(10)]
    for l in range(depth):
        pw, ew = _prep_weights(norm_mix[l], w_in[l], conv_w[l], a_log[l], dt_bias[l], gdn_norm[l], q_a_norm[l],
                               w_qb[l], kv_a_norm[l], w_kvb[l], q_norm[l], k_nope_norm[l], k_rope_norm[l], w_out[l],
                               norm_mem[l], mem_norm[l], w_mq[l], w_mk[l], w_mv[l], mq_norm[l], mk_norm[l], w_mo[l],
                               norm_ffn[l], w_router[l], b_router[l], w_gu[l], b_gu[l], w_down[l], b_down[l])
        nm = mem_prompt.shape[1]
        mk, mv = _memkv(mem_prompt.reshape(bp * nm, D_MODEL), pw['mnorm'], pw['w_mk'], pw['w_mv'], pw['mkn'],
                        _pick(bp * nm, (512, 256)))
        mk = mk.reshape(bp, nm, hw)
        mv = mv.reshape(bp, nm, hw)
        n_groups = 2 if bp % 2 == 0 else 1
        streams_p, (lat, kpe, s_fin, cv) = _trunk_front(
            yp, jnp.zeros((bp, 0, KV_RANK), F32), jnp.zeros((bp, 0, ROPE_DIM), F32),
            jnp.zeros((bp, GDN_HEADS, GDN_DK, GDN_DV), F32), jnp.zeros((bp, CONV_W - 1, CONV_DIM), F32), mk, mv, pw,
            n_groups)
        bs = x_sample.shape[0]
        (stream_s,), (lat2, kpe2, s_fin2, cv2) = _trunk_front(
            ys, cache_kv_latent[l], cache_k_rope[l], state_gdn[l], state_conv[l],
            cache_mem_k[l], cache_mem_v[l], pw, 1)
        tp = yp.shape[0] * yp.shape[1]
        tg = tp // n_groups
        ybuf = None
        for gi, st in enumerate(streams_p):
            last = gi == n_groups - 1
            res = _moe([st] + ([stream_s] if last else []),
                       [(ybuf, gi * tg, tp)] + ([(None, 0, None)] if last else []), ew)
            ybuf = res[0]
        yp, ys = ybuf.reshape(yp.shape), res[1].reshape(ys.shape)
        for lst, val in zip(outs, (lat, kpe, s_fin, cv, mk.reshape(bp, nm, MEM_HEADS, MEM_HEAD_DIM),
                                   mv.reshape(bp, nm, MEM_HEADS, MEM_HEAD_DIM), lat2, kpe2, s_fin2, cv2)):
            lst.append(val)
    return (yp, ys) + tuple(jnp.stack(o) for o in outs)
```

```python
import functools
import math

import numpy as np
import jax
import jax.numpy as jnp
from jax import lax
from jax.experimental import pallas as pl
from jax.experimental.pallas import tpu as pltpu
from jax.experimental.pallas import tpu_sc as plsc

F32 = jnp.float32
BF16 = jnp.bfloat16

D_MODEL = 1024
CHUNK = 64
EPS = 1e-6
GDN_HEADS = 4
GDN_DK = 128
GDN_DV = 128
CONV_W = 4
GDN_QK = GDN_HEADS * GDN_DK
GDN_VW = GDN_HEADS * GDN_DV
CONV_DIM = 2 * GDN_QK + GDN_VW
MLA_HEADS = 4
Q_RANK = 384
KV_RANK = 256
NOPE_DIM = 128
ROPE_DIM = 64
V_DIM = 128
QK_DIM = NOPE_DIM + ROPE_DIM
ROPE_THETA = 10000.0
N_MEM = 256
MEM_HEADS = 4
MEM_HEAD_DIM = 128
N_EXPERTS = 32
TOP_K = 4
D_FF = D_MODEL
SWIGLU_ALPHA = 1.702
SWIGLU_LIMIT = 7.0

LANES = 128
Q_SLAB = 2 * LANES
NEG_BIG = -1e30
VMEM_LIMIT = 56 * 1024 * 1024
MOE_ROWS = 512
GDN_UNITS = 8
GDN_GROUP = 2
FF_CHUNK = 512
SC_WINDOW = 128
PK_CHUNKS = D_MODEL // 2 // LANES
SUBTILE_ROWS = 256
ONES_ROWS = 16
MIX_ROWS = 512


def _cparams(sem):
    return pltpu.CompilerParams(dimension_semantics=sem, vmem_limit_bytes=VMEM_LIMIT)


def _dot(a, b):
    return jnp.dot(a, b, preferred_element_type=F32)


def _dot_nt(a, b):
    return lax.dot_general(a, b, (((1,), (1,)), ((), ())), preferred_element_type=F32)


def _dot_tn(a, b):
    return lax.dot_general(a, b, (((0,), (0,)), ((), ())), preferred_element_type=F32)


def _rms(x, gain, n=None):
    n = x.shape[-1] if n is None else n
    ss = jnp.sum(x * x, axis=-1, keepdims=True) * (1.0 / n)
    return (x * lax.rsqrt(ss + EPS)) * gain


def _sigmoid(x):
    return 1.0 / (1.0 + jnp.exp(-x))


def _rope128(r, cos, sna, snb):
    return r * cos + pltpu.roll(r, 96, 1) * sna + pltpu.roll(r, 32, 1) * snb


def _pack_bf16_pairs(x):
    n = x.shape[1] // 2
    lo = pltpu.bitcast(x[:, :n], jnp.uint32) >> 16
    hi = pltpu.bitcast(x[:, n:], jnp.uint32) & jnp.uint32(0xFFFF0000)
    return lo | hi


def _unpack_bf16_pairs(p):
    lo = pltpu.bitcast(p << 16, F32)
    hi = pltpu.bitcast(p & jnp.uint32(0xFFFF0000), F32)
    return jnp.concatenate([lo, hi], axis=1)


def _subtiles(rows):
    n = rows // SUBTILE_ROWS if rows % SUBTILE_ROWS == 0 else 1
    step = rows // n
    return [slice(i * step, (i + 1) * step) for i in range(n)]


def _const_spec(shape):
    nd = len(shape)
    return pl.BlockSpec(shape, lambda *_: (0,) * nd)


def _inproj_kernel(x_ref, nmix_ref, wu_ref, wz_ref, wcq_ref, wckv_ref, ws_ref, qan_ref, wqb_ref, qn_ref,
                   kvan_ref, krn_ref, alog_ref, dtb_ref, cos_ref, sna_ref, snb_ref,
                   u_ref, z_ref, q_ref, lat_ref, small_ref):
    for rs in _subtiles(x_ref.shape[0]):
        x = x_ref[rs, :]
        hb = _rms(x, nmix_ref[...]).astype(BF16)
        u_ref[rs, :] = _dot(hb, wu_ref[...])
        z_ref[rs, :] = _dot(hb, wz_ref[...])
        cq_raw = _dot(hb, wcq_ref[...])
        ckv_raw = _dot(hb, wckv_ref[...])
        sm = _dot(hb, ws_ref[...])
        cos, sna, snb = cos_ref[rs, :], sna_ref[rs, :], snb_ref[rs, :]

        cq = _rms(cq_raw, qan_ref[...]).astype(BF16)
        qf = _dot(cq, wqb_ref[...])
        scale = QK_DIM ** -0.5
        for h in range(MLA_HEADS):
            slab = qf[:, h * Q_SLAB:(h + 1) * Q_SLAB]
            slab = _rms(slab, qn_ref[...], n=QK_DIM)
            nope = slab[:, :LANES]
            ropd = _rope128(slab[:, LANES:], cos, sna, snb)
            q_ref[rs, h * Q_SLAB:h * Q_SLAB + LANES] = (nope * scale).astype(BF16)
            q_ref[rs, h * Q_SLAB + LANES:(h + 1) * Q_SLAB] = (ropd * scale).astype(BF16)

        lat_ref[rs, :] = _rms(ckv_raw, kvan_ref[...])

        lane = lax.broadcasted_iota(jnp.int32, sm.shape, 1)
        kp = jnp.where(lane < ROPE_DIM, sm, 0.0)
        kpe = _rope128(_rms(kp, krn_ref[...], n=ROPE_DIM), cos, sna, snb)
        sp = sm + dtb_ref[...]
        softplus = jnp.maximum(sp, 0.0) + jnp.log1p(jnp.exp(-jnp.abs(sp)))
        g = -jnp.exp(alog_ref[...]) * softplus
        beta = _sigmoid(sm)
        small_ref[rs, :] = jnp.where(lane < ROPE_DIM, kpe,
                                     jnp.where(lane < ROPE_DIM + GDN_HEADS, g,
                                               jnp.where(lane < ROPE_DIM + 2 * GDN_HEADS, beta, 0.0)))


def _inproj(x2d, S, tm, pw, tabs):
    T = x2d.shape[0]
    if tm > S:
        tabs = [jnp.tile(t, (tm // S, 1)) for t in tabs]
    nblk_s = max(S // tm, 1)
    row = lambda n: pl.BlockSpec((tm, n), lambda i: (i, 0))
    tab = pl.BlockSpec((tm, LANES), lambda i: (i % nblk_s, 0))
    consts = [pw['nmix'], pw['w_u'], pw['w_z'], pw['w_cq'], pw['w_ckv'], pw['w_s'], pw['qan'], pw['w_qb'],
              pw['qn'], pw['kvan'], pw['krn'], pw['alog'], pw['dtb']]
    return pl.pallas_call(
        _inproj_kernel,
        grid=(T // tm,),
        in_specs=[row(D_MODEL)] + [_const_spec(c.shape) for c in consts] + [tab, tab, tab],
        out_specs=[row(CONV_DIM), row(GDN_VW), row(MLA_HEADS * Q_SLAB), row(KV_RANK), row(LANES)],
        out_shape=[jax.ShapeDtypeStruct((T, CONV_DIM), F32), jax.ShapeDtypeStruct((T, GDN_VW), F32),
                   jax.ShapeDtypeStruct((T, MLA_HEADS * Q_SLAB), BF16), jax.ShapeDtypeStruct((T, KV_RANK), F32),
                   jax.ShapeDtypeStruct((T, LANES), F32)],
        compiler_params=_cparams(("parallel",)),
        name="inproj",
    )(x2d, *consts, *tabs)


def _split3(x):
    hi = x.astype(BF16)
    r = x - hi.astype(F32)
    mid = r.astype(BF16)
    lo = (r - mid.astype(F32)).astype(BF16)
    return hi, mid, lo


def _gdn_kernel(u_ref, small_ref, z_ref, convw_ref, cpast_ref, s0_ref, gnorm_ref,
                to_ref, trilm_ref, strictm_ref, same2_ref, lvl_ref,
                o_ref, sfin_ref, cnew_ref, ext_ref, uc_ref, state_ref,
                qf_ref, kf_ref, vb_ref, bt_ref, gcum_ref, glast_ref, kk_ref, qk_ref, mb_ref, x_ref, qkb_ref,
                kbe_ref, qg_ref, kdec_ref, egl_ref, t1_ref, uu_ref, ww_ref, vn_ref, qs_ref, *, NB, LC):
    j = pl.program_id(1)
    nj = pl.num_programs(1)
    PADR = 8
    C = LC // CHUNK
    NG = GDN_HEADS // GDN_GROUP
    U = NB * C * NG
    HR = GDN_GROUP * CHUNK

    def unit(nb, c, g):
        return (nb * C + c) * NG + g

    def head_rows(h):
        return slice((h % GDN_GROUP) * CHUNK, (h % GDN_GROUP + 1) * CHUNK)

    @pl.when(j == 0)
    def _():
        state_ref[...] = s0_ref[...]
        ext_ref[:, PADR - (CONV_W - 1):PADR, :] = cpast_ref[...]

    w = convw_ref[...]
    for nb in range(NB):
        ext_ref[nb, PADR:PADR + LC, :] = u_ref[nb]
        acc = ext_ref[nb, PADR:PADR + LC, :] * w[CONV_W - 1:CONV_W, :]
        for t in range(1, CONV_W):
            acc = acc + ext_ref[nb, PADR - t:PADR - t + LC, :] * w[CONV_W - 1 - t:CONV_W - t, :]
        uc_ref[nb] = acc * _sigmoid(acc)
        ext_ref[nb, 0:PADR, :] = ext_ref[nb, LC:LC + PADR, :]

    @pl.when(j == nj - 1)
    def _():
        cnew_ref[...] = ext_ref[:, PADR - (CONV_W - 1):PADR, :]

    g0 = ROPE_DIM
    b0 = ROPE_DIM + GDN_HEADS
    to = to_ref[...]
    for nb, c in [(nb, c) for nb in range(NB) for c in range(C)]:
        rows = slice(c * CHUNK, (c + 1) * CHUNK)
        sm = small_ref[nb, rows, :]
        gl = sum(_dot(to, part) for part in _split3(sm))
        for h in range(GDN_HEADS):
            u = unit(nb, c, h // GDN_GROUP)
            hr = head_rows(h)
            q = uc_ref[nb, rows, h * GDN_DK:(h + 1) * GDN_DK]
            k = uc_ref[nb, rows, GDN_QK + h * GDN_DK:GDN_QK + (h + 1) * GDN_DK]
            v = uc_ref[nb, rows, 2 * GDN_QK + h * GDN_DV:2 * GDN_QK + (h + 1) * GDN_DV]
            beta = jnp.broadcast_to(sm[:, b0 + h:b0 + h + 1], (CHUNK, LANES))
            qf_ref[u, hr, :] = (q * lax.rsqrt(jnp.sum(q * q, -1, keepdims=True) + EPS)) * (GDN_DK ** -0.5)
            kf_ref[u, hr, :] = k * lax.rsqrt(jnp.sum(k * k, -1, keepdims=True) + EPS)
            vb_ref[u, hr, :] = (v * beta).astype(BF16)
            bt_ref[u, hr, :] = beta
            gcum_ref[u, hr, :] = jnp.broadcast_to(gl[:CHUNK, g0 + h:g0 + h + 1], (CHUNK, LANES))
            glast_ref[u, hr, :] = jnp.broadcast_to(gl[CHUNK:, g0 + h:g0 + h + 1], (CHUNK, LANES))

    for u in range(U):
        k = kf_ref[u]
        kbf = k.astype(BF16)
        kk_ref[u] = _dot_nt((k * bt_ref[u]).astype(BF16), kbf)
        qk_ref[u] = _dot_nt(qf_ref[u].astype(BF16), kbf)

    trilm = trilm_ref[...]
    eye = trilm - strictm_ref[...]
    for u in range(U):
        gcum = gcum_ref[u]
        grow = gcum.T[0:1, :]
        gcol = gcum if HR == LANES else jnp.concatenate([gcum] * (HR // LANES), axis=1)
        decay = jnp.exp(jnp.minimum(gcol - grow, 0.0)) * trilm
        m = kk_ref[u] * (decay * strictm_ref[...])
        mb_ref[u] = m.astype(BF16)
        x_ref[u] = eye - m * same2_ref[...]
        qkb_ref[u] = (qk_ref[u] * decay).astype(BF16)
        egc = jnp.exp(gcum)
        k = kf_ref[u]
        kbe_ref[u] = (k * bt_ref[u] * egc).astype(BF16)
        qg_ref[u] = (qf_ref[u] * egc).astype(BF16)
        kdec_ref[u] = (k * jnp.exp(glast_ref[u] - gcum)).astype(BF16)
        egl_ref[u] = jnp.exp(glast_ref[u])

    for lvl in range(lvl_ref.shape[0]):
        lm = lvl_ref[lvl]
        for u in range(U):
            t1_ref[u] = _dot(mb_ref[u] * lm, x_ref[u].astype(BF16)).astype(BF16)
        for u in range(U):
            x = x_ref[u]
            x_ref[u] = x - _dot(x.astype(BF16), t1_ref[u])

    for u in range(U):
        xb = x_ref[u].astype(BF16)
        uu_ref[u] = _dot(xb, vb_ref[u])
        ww_ref[u] = _dot(xb, kbe_ref[u]).astype(BF16)

    gnorm = gnorm_ref[...]
    for c in range(C):
        rows = slice(c * CHUNK, (c + 1) * CHUNK)
        heads = [(nb, h, unit(nb, c, h // GDN_GROUP), head_rows(h)) for nb in range(NB) for h in range(GDN_HEADS)]
        groups = [(nb, g, unit(nb, c, g)) for nb in range(NB) for g in range(NG)]
        for nb, h, u, hr in heads:
            stb = state_ref[nb, h].astype(BF16)
            r = _dot(jnp.concatenate([ww_ref[u, hr, :], qg_ref[u, hr, :]], axis=0), stb)
            vn_ref[u, hr, :] = (uu_ref[u, hr, :] - r[:CHUNK]).astype(BF16)
            qs_ref[u, hr, :] = r[CHUNK:]
        outs = {(nb, g): qs_ref[u] + _dot(qkb_ref[u], vn_ref[u]) for nb, g, u in groups}
        for nb, h, u, hr in heads:
            state_ref[nb, h] = (state_ref[nb, h] * egl_ref[u, hr.start:hr.start + 1, :]
                                + _dot_tn(kdec_ref[u, hr, :], vn_ref[u, hr, :]))
        for nb, h, u, hr in heads:
            zz = z_ref[nb, rows, h * GDN_DV:(h + 1) * GDN_DV]
            og = _rms(outs[(nb, h // GDN_GROUP)][hr, :], gnorm) * (zz * _sigmoid(zz))
            o_ref[nb, rows, h * GDN_DV:(h + 1) * GDN_DV] = og.astype(BF16)

    @pl.when(j == nj - 1)
    def _():
        sfin_ref[...] = state_ref[...]


def _gdn_masks():
    hr = GDN_GROUP * CHUNK
    i = np.arange(hr)[:, None]
    j = np.arange(hr)[None, :]
    same_head = (i // CHUNK) == (j // CHUNK)
    tril = same_head & (i >= j)
    strict = same_head & (i > j)
    same2 = strict & ((i // 2) == (j // 2))
    lvls = []
    blk = 2
    while blk < CHUNK:
        lvls.append(strict & ((i // (2 * blk)) == (j // (2 * blk))) & ((i // blk) != (j // blk)))
        blk *= 2
    fr = np.arange(CHUNK)
    to = np.concatenate([fr[:, None] >= fr[None, :], np.ones((CHUNK, CHUNK), bool)], axis=0)
    f = lambda a: jnp.asarray(a.astype(np.float32))
    return (jnp.asarray(to.astype(np.float32), dtype=BF16), f(tril), f(strict), f(same2),
            jnp.asarray(np.stack(lvls).astype(np.float32), dtype=BF16))


def _gdn(u3, small3, z3, conv_w, conv_past, s0, gnorm, NB, LC):
    B, S, _ = u3.shape
    C = LC // CHUNK
    U = NB * C * (GDN_HEADS // GDN_GROUP)
    HR = GDN_GROUP * CHUNK
    masks = _gdn_masks()
    tile = lambda n: pl.BlockSpec((NB, LC, n), lambda b, j: (b, j, 0))
    stspec = pl.BlockSpec((NB, GDN_HEADS, GDN_DK, GDN_DV), lambda b, j: (b, 0, 0, 0))
    cvspec = pl.BlockSpec((NB, CONV_W - 1, CONV_DIM), lambda b, j: (b, 0, 0))
    vm = lambda shape, dt: pltpu.VMEM(shape, dt)
    return pl.pallas_call(
        functools.partial(_gdn_kernel, NB=NB, LC=LC),
        grid=(B // NB, S // LC),
        in_specs=[tile(CONV_DIM), tile(LANES), tile(GDN_VW), _const_spec(conv_w.shape), cvspec, stspec,
                  _const_spec(gnorm.shape)] + [_const_spec(m.shape) for m in masks],
        out_specs=[tile(GDN_VW), stspec, cvspec],
        out_shape=[jax.ShapeDtypeStruct((B, S, GDN_VW), BF16),
                   jax.ShapeDtypeStruct((B, GDN_HEADS, GDN_DK, GDN_DV), F32),
                   jax.ShapeDtypeStruct((B, CONV_W - 1, CONV_DIM), F32)],
        scratch_shapes=[vm((NB, LC + 8, CONV_DIM), F32), vm((NB, LC, CONV_DIM), F32),
                        vm((NB, GDN_HEADS, GDN_DK, GDN_DV), F32),
                        vm((U, HR, LANES), F32), vm((U, HR, LANES), F32), vm((U, HR, LANES), BF16),
                        vm((U, HR, LANES), F32),
                        vm((U, HR, LANES), F32), vm((U, HR, LANES), F32),
                        vm((U, HR, HR), F32), vm((U, HR, HR), F32),
                        vm((U, HR, HR), BF16), vm((U, HR, HR), F32), vm((U, HR, HR), BF16),
                        vm((U, HR, LANES), BF16), vm((U, HR, LANES), BF16), vm((U, HR, LANES), BF16),
                        vm((U, HR, LANES), F32), vm((U, HR, HR), BF16),
                        vm((U, HR, LANES), F32), vm((U, HR, LANES), BF16),
                        vm((U, HR, LANES), BF16), vm((U, HR, LANES), F32)],
        compiler_params=_cparams(("parallel", "arbitrary")),
        name="gdn",
    )(u3, small3, z3, conv_w, conv_past, s0, gnorm, *masks)


def _kvproj_kernel(lat_ref, kpe_ref, wk_ref, wv_ref, knn_ref, kn_ref, kr_ref, v_ref, *, v_transposed):
    lb = lat_ref[...].astype(BF16)
    kf = _dot(lb, wk_ref[...])
    for h in range(MLA_HEADS):
        sl = slice(h * NOPE_DIM, (h + 1) * NOPE_DIM)
        kn_ref[:, sl] = _rms(kf[:, sl], knn_ref[...]).astype(BF16)
    if v_transposed:
        v_ref[...] = _dot_nt(wv_ref[...], lb).astype(BF16)
    else:
        v_ref[...] = _dot(lb, wv_ref[...]).astype(BF16)
    kp = kpe_ref[...]
    if kp.shape[1] == ROPE_DIM:
        kr_ref[...] = jnp.concatenate([kp, jnp.zeros((kp.shape[0], LANES - ROPE_DIM), F32)], axis=1).astype(BF16)
    else:
        lane = lax.broadcasted_iota(jnp.int32, kp.shape, 1)
        kr_ref[...] = jnp.where(lane < ROPE_DIM, kp, 0.0).astype(BF16)


def _kvproj(lat3, kpe3, pw, tm, v_transposed):
    B, sk, _ = lat3.shape
    hw = MLA_HEADS * V_DIM
    row = lambda n: pl.BlockSpec((None, tm, n), lambda b, i: (b, i, 0))
    if v_transposed:
        vspec, vshape, w_v = pl.BlockSpec((None, hw, tm), lambda b, i: (b, 0, i)), (B, hw, sk), pw['w_vt']
    else:
        vspec, vshape, w_v = row(hw), (B, sk, hw), pw['w_v']
    return pl.pallas_call(
        functools.partial(_kvproj_kernel, v_transposed=v_transposed),
        grid=(B, sk // tm),
        in_specs=[row(KV_RANK), row(kpe3.shape[-1]), _const_spec(pw['w_k'].shape), _const_spec(w_v.shape),
                  _const_spec(pw['knn'].shape)],
        out_specs=[row(MLA_HEADS * NOPE_DIM), row(LANES), vspec],
        out_shape=[jax.ShapeDtypeStruct((B, sk, MLA_HEADS * NOPE_DIM), BF16),
                   jax.ShapeDtypeStruct((B, sk, LANES), BF16), jax.ShapeDtypeStruct(vshape, BF16)],
        compiler_params=_cparams(("parallel", "parallel")),
        name="kvproj",
    )(lat3, kpe3, pw['w_k'], w_v, pw['knn'])


def _last_kblock(qi, tq, tk, P, nk):
    last_key = ((P + qi * tq + tq - 1) // CHUNK) * CHUNK + CHUNK - 1
    return jnp.minimum(last_key // tk, nk - 1)


def _flash_kernel(q_ref, kn_ref, kr_ref, v_ref, o_ref, m_ref, l_ref, acc_ref, *, tq, tk, P, S, nk):
    qi = pl.program_id(1)
    ki = pl.program_id(2)

    @pl.when(ki == 0)
    def _():
        m_ref[...] = jnp.full(m_ref.shape, NEG_BIG, F32)
        l_ref[...] = jnp.zeros(l_ref.shape, F32)
        acc_ref[...] = jnp.zeros(acc_ref.shape, F32)

    @pl.when(ki <= _last_kblock(qi, tq, tk, P, nk))
    def _():
        qpos = P + qi * tq + lax.broadcasted_iota(jnp.int32, (tq, tk), 0)
        kpos = ki * tk + lax.broadcasted_iota(jnp.int32, (tq, tk), 1)
        mask = ((kpos // CHUNK) <= (qpos // CHUNK)) & (kpos < P + S)
        kr = kr_ref[...]
        for h in range(MLA_HEADS):
            qh = q_ref[:, h * Q_SLAB:(h + 1) * Q_SLAB]
            kh = jnp.concatenate([kn_ref[:, h * NOPE_DIM:(h + 1) * NOPE_DIM], kr], axis=1)
            s = jnp.where(mask, _dot_nt(qh, kh), NEG_BIG)
            m_prev = m_ref[h]
            m_new = jnp.maximum(m_prev, jnp.max(s, axis=-1, keepdims=True))
            alpha = jnp.exp(m_prev - m_new)
            p = jnp.exp(s - m_new)
            l_ref[h] = alpha * l_ref[h] + jnp.sum(p, axis=-1, keepdims=True)
            acc_ref[h] = alpha * acc_ref[h] + _dot(p.astype(BF16), v_ref[:, h * V_DIM:(h + 1) * V_DIM])
            m_ref[h] = m_new

    @pl.when(ki == nk - 1)
    def _():
        for h in range(MLA_HEADS):
            o_ref[:, h * V_DIM:(h + 1) * V_DIM] = (acc_ref[h] / l_ref[h]).astype(BF16)


def _flash(q3, kn3, kr3, v3, P, S, tq, tk):
    B = q3.shape[0]
    skp = kn3.shape[1]
    nk = skp // tk
    kmap = lambda b, qi, ki: (b, jnp.minimum(ki, _last_kblock(qi, tq, tk, P, nk)), 0)
    return pl.pallas_call(
        functools.partial(_flash_kernel, tq=tq, tk=tk, P=P, S=S, nk=nk),
        grid=(B, S // tq, nk),
        in_specs=[pl.BlockSpec((None, tq, MLA_HEADS * Q_SLAB), lambda b, qi, ki: (b, qi, 0)),
                  pl.BlockSpec((None, tk, MLA_HEADS * NOPE_DIM), kmap),
                  pl.BlockSpec((None, tk, LANES), kmap),
                  pl.BlockSpec((None, tk, MLA_HEADS * V_DIM), kmap)],
        out_specs=pl.BlockSpec((None, tq, MLA_HEADS * V_DIM), lambda b, qi, ki: (b, qi, 0)),
        out_shape=jax.ShapeDtypeStruct((B, S, MLA_HEADS * V_DIM), BF16),
        scratch_shapes=[pltpu.VMEM((MLA_HEADS, tq, 1), F32), pltpu.VMEM((MLA_HEADS, tq, 1), F32),
                        pltpu.VMEM((MLA_HEADS, tq, V_DIM), F32)],
        compiler_params=_cparams(("parallel", "parallel", "arbitrary")),
        name="mla_attn",
    )(q3, kn3, kr3, v3)


def _attn_dec_kernel(q_ref, lat_ref, kpe_ref, knn_ref, krn_ref, vn_ref, wk_ref, wv_ref, gain_ref, o_ref):
    lb = lat_ref[...].astype(BF16)
    kf = _dot(lb, wk_ref[...])
    vp = _dot(lb, wv_ref[...]).astype(BF16)
    kp = kpe_ref[...]
    krp = jnp.concatenate([kp, jnp.zeros((kp.shape[0], LANES - ROPE_DIM), F32)], axis=1).astype(BF16)
    krn = krn_ref[...]
    hsl = [slice(h * NOPE_DIM, (h + 1) * NOPE_DIM) for h in range(MLA_HEADS)]
    qs = [q_ref[:, h * Q_SLAB:(h + 1) * Q_SLAB] for h in range(MLA_HEADS)]
    knp = [_rms(kf[:, sl], gain_ref[...]).astype(BF16) for sl in hsl]
    sps = [_dot_nt(q, jnp.concatenate([k, krp], axis=1)) for q, k in zip(qs, knp)]
    sns = [_dot_nt(q, jnp.concatenate([knn_ref[:, sl], krn], axis=1)) for q, sl in zip(qs, hsl)]
    for h, (sp, sn) in enumerate(zip(sps, sns)):
        vsl = slice(h * V_DIM, (h + 1) * V_DIM)
        m = jnp.maximum(jnp.max(sp, axis=-1, keepdims=True), jnp.max(sn, axis=-1, keepdims=True))
        pp = jnp.exp(sp - m)
        pn = jnp.exp(sn - m)
        l = jnp.sum(pp, axis=-1, keepdims=True) + jnp.sum(pn, axis=-1, keepdims=True)
        o = _dot(pp.astype(BF16), vp[:, vsl]) + _dot(pn.astype(BF16), vn_ref[:, vsl])
        o_ref[:, vsl] = (o / l).astype(BF16)


def _attn_dec(q3, lat_past, kpe_past, new, pw):
    B, S, _ = q3.shape
    consts = [pw['w_k'], pw['w_v'], pw['knn']]
    specs = [pl.BlockSpec((None, a.shape[1], a.shape[2]), lambda b: (b, 0, 0)) for a in (q3, lat_past, kpe_past, *new)]
    return pl.pallas_call(
        _attn_dec_kernel,
        grid=(B,),
        in_specs=specs + [_const_spec(c.shape) for c in consts],
        out_specs=pl.BlockSpec((None, S, MLA_HEADS * V_DIM), lambda b: (b, 0, 0)),
        out_shape=jax.ShapeDtypeStruct((B, S, MLA_HEADS * V_DIM), BF16),
        compiler_params=_cparams(("parallel",)),
        name="mla_attn_dec",
    )(q3, lat_past, kpe_past, *new, *consts)


def _flash_t_kernel(qt_ref, kt_ref, lt_ref, q_ref, kn_ref, kr_ref, vt_ref, o_ref, m_ref, acc_ref, *, tq, tk, P, S):
    j = pl.program_id(1)
    qi = qt_ref[j]
    ki = kt_ref[j]
    q0 = P + qi * tq
    k0 = ki * tk

    @pl.when(ki == 0)
    def _():
        m_ref[...] = jnp.full(m_ref.shape, NEG_BIG, F32)
        acc_ref[...] = jnp.zeros(acc_ref.shape, F32)

    def step(masked):
        kr = kr_ref[...]
        ones = jnp.ones((ONES_ROWS, tk), BF16)
        if masked:
            kpos = k0 + lax.broadcasted_iota(jnp.int32, (tk, 1), 0)
            qpos = q0 + lax.broadcasted_iota(jnp.int32, (1, tq), 1)
            mask = ((kpos // CHUNK) <= (qpos // CHUNK)) & (kpos < P + S)
        sts = []
        for h in range(MLA_HEADS):
            kh = jnp.concatenate([kn_ref[:, h * NOPE_DIM:(h + 1) * NOPE_DIM], kr], axis=1)
            sts.append(_dot_nt(kh, q_ref[:, h * Q_SLAB:(h + 1) * Q_SLAB]))
        ps, alphas = [], []
        for h in range(MLA_HEADS):
            st = sts[h]
            if masked:
                st = jnp.where(mask, st, NEG_BIG)
            m_prev = m_ref[h]
            m_new = jnp.maximum(m_prev, jnp.max(st, axis=0, keepdims=True))
            alphas.append(jnp.exp(m_prev - m_new))
            ps.append(jnp.exp(st - m_new).astype(BF16))
            m_ref[h] = m_new
        for h in range(MLA_HEADS):
            vt1 = jnp.concatenate([vt_ref[h * V_DIM:(h + 1) * V_DIM, :], ones], axis=0)
            acc_ref[h] = alphas[h] * acc_ref[h] + _dot(vt1, ps[h])

    full = ((k0 + tk - 1) // CHUNK <= q0 // CHUNK) & (k0 + tk <= P + S)
    pl.when(full)(functools.partial(step, False))
    pl.when(jnp.logical_not(full))(functools.partial(step, True))

    @pl.when(lt_ref[j] == 1)
    def _():
        for h in range(MLA_HEADS):
            acc = acc_ref[h]
            o_ref[:, h * V_DIM:(h + 1) * V_DIM] = (acc[:V_DIM] / acc[V_DIM:V_DIM + 1]).T.astype(BF16)


def _flash_t(q3, kn3, kr3, vt3, P, S, tq, tk):
    B = q3.shape[0]
    nk = kn3.shape[1] // tk
    pairs = []
    for qi in range(S // tq):
        last = min((((P + qi * tq + tq - 1) // CHUNK) * CHUNK + CHUNK - 1) // tk, nk - 1)
        pairs += [(qi, ki, int(ki == last)) for ki in range(last + 1)]
    qt, kt, lt = (jnp.asarray(np.array(col, np.int32)) for col in zip(*pairs))
    kmap = lambda b, j, qt, kt, lt: (b, kt[j], 0)
    gs = pltpu.PrefetchScalarGridSpec(
        num_scalar_prefetch=3,
        grid=(B, len(pairs)),
        in_specs=[pl.BlockSpec((None, tq, MLA_HEADS * Q_SLAB), lambda b, j, qt, kt, lt: (b, qt[j], 0)),
                  pl.BlockSpec((None, tk, MLA_HEADS * NOPE_DIM), kmap),
                  pl.BlockSpec((None, tk, LANES), kmap),
                  pl.BlockSpec((None, MLA_HEADS * V_DIM, tk), lambda b, j, qt, kt, lt: (b, 0, kt[j]))],
        out_specs=pl.BlockSpec((None, tq, MLA_HEADS * V_DIM), lambda b, j, qt, kt, lt: (b, qt[j], 0)),
        scratch_shapes=[pltpu.VMEM((MLA_HEADS, 1, tq), F32), pltpu.VMEM((MLA_HEADS, V_DIM + ONES_ROWS, tq), F32)],
    )
    return pl.pallas_call(
        functools.partial(_flash_t_kernel, tq=tq, tk=tk, P=P, S=S),
        grid_spec=gs,
        out_shape=jax.ShapeDtypeStruct((B, S, MLA_HEADS * V_DIM), BF16),
        compiler_params=_cparams(("parallel", "arbitrary")),
        name="mla_attn_t",
    )(qt, kt, lt, q3, kn3, kr3, vt3)


def _memkv_kernel(mem_ref, mnorm_ref, wmk_ref, wmv_ref, mkn_ref, k_ref, v_ref):
    mb = _rms(mem_ref[...], mnorm_ref[...]).astype(BF16)
    kf = _dot(mb, wmk_ref[...])
    for h in range(MEM_HEADS):
        sl = slice(h * MEM_HEAD_DIM, (h + 1) * MEM_HEAD_DIM)
        k_ref[:, sl] = _rms(kf[:, sl], mkn_ref[...])
    v_ref[...] = _dot(mb, wmv_ref[...])


def _memkv(mem2d, mnorm, w_mk, w_mv, mkn, tm):
    T = mem2d.shape[0]
    hw = MEM_HEADS * MEM_HEAD_DIM
    row = lambda n: pl.BlockSpec((tm, n), lambda i: (i, 0))
    return pl.pallas_call(
        _memkv_kernel,
        grid=(T // tm,),
        in_specs=[row(D_MODEL), _const_spec(mnorm.shape), _const_spec(w_mk.shape), _const_spec(w_mv.shape),
                  _const_spec(mkn.shape)],
        out_specs=[row(hw), row(hw)],
        out_shape=[jax.ShapeDtypeStruct((T, hw), F32), jax.ShapeDtypeStruct((T, hw), F32)],
        compiler_params=_cparams(("parallel",)),
        name="mem_kv",
    )(mem2d, mnorm, w_mk, w_mv, mkn)


def _mix_kernel(x_ref, og_ref, om_ref, mk_ref, mv_ref, wout_ref, nmem_ref, wmq_ref, mqn_ref, wmo_ref, nffn_ref,
                wrh_ref, wrl_ref, br_ref, x2_ref, h3_ref, idx_ref, gate_ref, counts_ref, cnt_ref):
    def mem_head(ref, b, h):
        if len(ref.shape) == 3:
            return ref[b, :, h * MEM_HEAD_DIM:(h + 1) * MEM_HEAD_DIM].astype(BF16)
        return ref[b, :, h, :].astype(BF16)

    first = (pl.program_id(0) == 0) & (pl.program_id(1) == 0)

    @pl.when(first)
    def _():
        cnt_ref[...] = jnp.zeros(cnt_ref.shape, F32)

    nbm, tm, _ = x_ref.shape
    rows = nbm * tm
    flat = lambda ref: ref[...].reshape(rows, ref.shape[-1])

    x1 = flat(x_ref) + _dot(flat(og_ref), wout_ref[0:GDN_VW, :]) + _dot(flat(om_ref), wout_ref[GDN_VW:, :])
    hb = _rms(x1, nmem_ref[...]).astype(BF16)
    qm = _dot(hb, wmq_ref[...])
    per_batch = []
    hsl = [slice(h * MEM_HEAD_DIM, (h + 1) * MEM_HEAD_DIM) for h in range(MEM_HEADS)]
    for b in range(nbm):
        br = slice(b * tm, (b + 1) * tm)
        qhs = [(_rms(qm[br, sl], mqn_ref[...]) * (MEM_HEAD_DIM ** -0.5)).astype(BF16) for sl in hsl]
        ss = [_dot_nt(qh, mem_head(mk_ref, b, h)) for h, qh in enumerate(qhs)]
        ps = []
        for s in ss:
            p = jnp.exp(s - jnp.max(s, axis=-1, keepdims=True))
            ps.append((p / jnp.sum(p, axis=-1, keepdims=True)).astype(BF16))
        heads = [_dot(p, mem_head(mv_ref, b, h)).astype(BF16) for h, p in enumerate(ps)]
        per_batch.append(jnp.concatenate(heads, axis=1))
    om = per_batch[0] if nbm == 1 else jnp.concatenate(per_batch, axis=0)
    x2 = x1 + _dot(om, wmo_ref[...])
    x2_ref[...] = x2.reshape(x2_ref.shape)
    h3 = _rms(x2, nffn_ref[...])
    hi = h3.astype(BF16)
    packed = _pack_bf16_pairs(hi.astype(F32))
    for c in range(h3_ref.shape[0]):
        h3_ref[c] = packed[:, c * LANES:(c + 1) * LANES]
    lo = (h3 - hi.astype(F32)).astype(BF16)
    wrh = wrh_ref[...]
    logits = _dot(hi, wrh) + _dot(lo, wrh) + _dot(hi, wrl_ref[...]) + br_ref[...]

    lane = lax.broadcasted_iota(jnp.int32, logits.shape, 1).astype(F32)
    vals, idxs = [], []
    cur = logits
    for _ in range(TOP_K):
        mx = jnp.max(cur, axis=-1, keepdims=True)
        ix = jnp.min(jnp.where(cur == mx, lane, float(LANES)), axis=-1, keepdims=True)
        vals.append(mx)
        idxs.append(ix)
        cur = jnp.where(lane == ix, -3e38, cur)
    es = [jnp.exp(v - vals[0]) for v in vals]
    den = es[0] + es[1] + es[2] + es[3]

    sel = jnp.zeros(logits.shape, F32)
    for k in range(TOP_K):
        sel = sel + jnp.where(lane == idxs[k], 1.0, 0.0)
    ri = lax.broadcasted_iota(jnp.int32, (rows, rows), 0)
    ci = lax.broadcasted_iota(jnp.int32, (rows, rows), 1)
    before = jnp.where(ri > ci, 1.0, 0.0).astype(BF16)
    excl = _dot(before, sel.astype(BF16)) + cnt_ref[...]
    cnt_ref[...] = cnt_ref[...] + jnp.sum(sel, axis=0, keepdims=True)
    counts_ref[...] = cnt_ref[...].astype(jnp.int32)

    idx_out = jnp.zeros(logits.shape, F32)
    gate_out = jnp.zeros(logits.shape, F32)
    for k in range(TOP_K):
        rank = jnp.sum(jnp.where(lane == idxs[k], excl, 0.0), axis=-1, keepdims=True)
        idx_out = jnp.where(lane == float(k), idxs[k], idx_out)
        idx_out = jnp.where(lane == float(TOP_K + k), rank, idx_out)
        gate_out = jnp.where(lane == float(k), es[k] / den, gate_out)
    idx_ref[...] = idx_out.astype(jnp.int32).reshape(idx_ref.shape)
    gate_ref[...] = gate_out.reshape(gate_ref.shape)


def _mix(x3, og3, om3, mk3, mv3, pw, tm, b0, B):
    S = x3.shape[1]
    hw = MEM_HEADS * MEM_HEAD_DIM
    nbm = _pick(B, tuple(n for n in (8, 4, 2, 1) if n * tm <= MIX_ROWS and b0 % n == 0)) if tm == S else 1
    nsb = S // tm
    boff = b0 // nbm
    tile_in = lambda n: pl.BlockSpec((nbm, tm, n), lambda b, i: (b + boff, i, 0))
    tile = lambda n: pl.BlockSpec((nbm, tm, n), lambda b, i: (b, i, 0))
    if mk3.ndim == 3:
        memspec = pl.BlockSpec((nbm, N_MEM, hw), lambda b, i: (b + boff, 0, 0))
    else:
        memspec = pl.BlockSpec((nbm, N_MEM, MEM_HEADS, MEM_HEAD_DIM), lambda b, i: (b + boff, 0, 0, 0))
    consts = [pw['w_out'], pw['nmem'], pw['w_mq'], pw['mqn'], pw['w_mo'], pw['nffn'], pw['wr_hi'], pw['wr_lo'],
              pw['b_r']]
    return pl.pallas_call(
        _mix_kernel,
        grid=(B // nbm, nsb),
        in_specs=[tile_in(D_MODEL), tile_in(GDN_VW), tile_in(MLA_HEADS * V_DIM), memspec, memspec]
                 + [_const_spec(c.shape) for c in consts],
        out_specs=[tile(D_MODEL), pl.BlockSpec((PK_CHUNKS, nbm * tm, LANES), lambda b, i: (0, b * nsb + i, 0)),
                   tile(LANES), tile(LANES), _const_spec((1, LANES))],
        out_shape=[jax.ShapeDtypeStruct((B, S, D_MODEL), F32),
                   jax.ShapeDtypeStruct((PK_CHUNKS, B * S, LANES), jnp.uint32),
                   jax.ShapeDtypeStruct((B, S, LANES), jnp.int32), jax.ShapeDtypeStruct((B, S, LANES), F32),
                   jax.ShapeDtypeStruct((1, LANES), jnp.int32)],
        scratch_shapes=[pltpu.VMEM((1, LANES), F32)],
        compiler_params=_cparams(("arbitrary", "arbitrary")),
        name="mix_mem_router",
    )(x3, og3, om3, mk3, mv3, *consts)


def _expert_kernel(be_ref, nx_ref, nv_ref, nu_ref, rows_ref, wgu_hbm, bgu_ref, wd_hbm, bd_ref, y_ref,
                   wgus_ref, wds_ref, wgub_ref, wdb_ref, sem_ref):
    i = pl.program_id(0)
    used = i < nu_ref[0]
    new_expert = (i == 0) | (be_ref[i] != be_ref[jnp.maximum(i - 1, 0)])

    def weight_copies(e):
        return (pltpu.make_async_copy(wgu_hbm.at[e], wgus_ref, sem_ref.at[0]),
                pltpu.make_async_copy(wd_hbm.at[e], wds_ref, sem_ref.at[1]))

    @pl.when(i == 0)
    def _():
        for cp in weight_copies(be_ref[0]):
            cp.start()

    @pl.when(used & new_expert)
    def _():
        for cp in weight_copies(be_ref[i]):
            cp.wait()

        def cast(r, carry):
            rs = pl.ds(pl.multiple_of(r * LANES, LANES), LANES)
            wgub_ref[rs, :] = wgus_ref[rs, :].astype(BF16)
            wdb_ref[rs, :] = wds_ref[rs, :].astype(BF16)
            return carry
        lax.fori_loop(0, D_MODEL // LANES, cast, 0)

        @pl.when(nx_ref[i] >= 0)
        def _():
            for cp in weight_copies(nx_ref[i]):
                cp.start()

    def ffn(nrows):
        packed = jnp.concatenate([rows_ref[c, :nrows, :] for c in range(PK_CHUNKS)], axis=1)
        x = _unpack_bf16_pairs(packed).astype(BF16)
        acc = None
        for c in range(D_FF // FF_CHUNK):
            gs_ = slice(c * FF_CHUNK, (c + 1) * FF_CHUNK)
            us_ = slice(D_FF + c * FF_CHUNK, D_FF + (c + 1) * FF_CHUNK)
            gt = jnp.minimum(_dot(x, wgub_ref[:, gs_]) + bgu_ref[:, gs_], SWIGLU_LIMIT)
            up = jnp.clip(_dot(x, wgub_ref[:, us_]) + bgu_ref[:, us_], -SWIGLU_LIMIT, SWIGLU_LIMIT)
            act = gt * _sigmoid(SWIGLU_ALPHA * gt) * (up + 1.0)
            part = _dot(act.astype(BF16), wdb_ref[gs_, :])
            acc = part if acc is None else acc + part
        ypk = _pack_bf16_pairs((acc + bd_ref[...]).astype(BF16).astype(F32))
        for c in range(PK_CHUNKS):
            y_ref[c, :nrows, :] = ypk[:, c * LANES:(c + 1) * LANES]
        if nrows < MOE_ROWS:
            y_ref[:, nrows:, :] = jnp.zeros((PK_CHUNKS, MOE_ROWS - nrows, LANES), y_ref.dtype)

    half_full = nv_ref[i] <= MOE_ROWS // 2
    pl.when(used & jnp.logical_not(half_full))(functools.partial(ffn, MOE_ROWS))
    pl.when(used & half_full)(functools.partial(ffn, MOE_ROWS // 2))

    @pl.when(jnp.logical_not(used))
    def _():
        y_ref[...] = jnp.zeros(y_ref.shape, y_ref.dtype)


def _experts(block_e, next_e, block_rows, n_used, rows, w_gu, b_gu, w_down, b_down):
    n_rows = rows.shape[1]
    nb = n_rows // MOE_ROWS
    gs = pltpu.PrefetchScalarGridSpec(
        num_scalar_prefetch=4,
        grid=(nb,),
        in_specs=[pl.BlockSpec((PK_CHUNKS, MOE_ROWS, LANES), lambda i, be, nx, nv, nu: (0, i, 0)),
                  pl.BlockSpec(memory_space=pl.ANY),
                  pl.BlockSpec((None, 1, 2 * D_FF), lambda i, be, nx, nv, nu: (be[i], 0, 0)),
                  pl.BlockSpec(memory_space=pl.ANY),
                  pl.BlockSpec((None, 1, D_MODEL), lambda i, be, nx, nv, nu: (be[i], 0, 0))],
        out_specs=pl.BlockSpec((PK_CHUNKS, MOE_ROWS, LANES), lambda i, be, nx, nv, nu: (0, i, 0)),
        scratch_shapes=[pltpu.VMEM((D_MODEL, 2 * D_FF), F32), pltpu.VMEM((D_FF, D_MODEL), F32),
                        pltpu.VMEM((D_MODEL, 2 * D_FF), BF16), pltpu.VMEM((D_FF, D_MODEL), BF16),
                        pltpu.SemaphoreType.DMA((2,))],
    )
    return pl.pallas_call(
        _expert_kernel,
        grid_spec=gs,
        out_shape=jax.ShapeDtypeStruct((PK_CHUNKS, n_rows, LANES), jnp.uint32),
        compiler_params=_cparams(("arbitrary",)),
        name="moe_experts",
    )(block_e, next_e, block_rows, n_used, rows, w_gu, b_gu, w_down, b_down)


def _sc_mesh():
    return plsc.VectorSubcoreMesh(core_axis_name="core", subcore_axis_name="subcore")


def _sc_scatter_rows(x3s, pos_ts, n_rows):
    C, _, L = x3s[0].shape
    K = pos_ts[0].shape[0]
    ns = len(x3s)

    @functools.partial(pl.kernel, out_type=jax.ShapeDtypeStruct((C, n_rows, L), x3s[0].dtype), mesh=_sc_mesh(),
                       scratch_types=[])
    def scatter(*refs):
        o_hbm = refs[2 * ns]
        for s in range(ns):
            x_hbm, i_hbm = refs[s], refs[ns + s]
            nwin = x3s[s].shape[1] // SC_WINDOW
            for c in range(C):
                def body(x_vmem, i_vmem, c=c):
                    for k in range(K):
                        pltpu.sync_copy(x_vmem, o_hbm.at[c].at[i_vmem.at[k]])

                pltpu.emit_pipeline(
                    body, grid=(nwin,),
                    in_specs=[pl.BlockSpec((SC_WINDOW, L), lambda i, c=c, nwin=nwin: (c * nwin + i, 0)),
                              pl.BlockSpec((K, SC_WINDOW), lambda i: (0, i))],
                    out_specs=[], core_axis_name=("core", "subcore"), dimension_semantics=(pltpu.PARALLEL,),
                )(x_hbm, i_hbm)

    return scatter(*[x.reshape(-1, L) for x in x3s], *pos_ts)


def _sc_gather_rows(table3, idxs):
    C, _, L = table3.shape
    ns = len(idxs)
    out_type = [jax.ShapeDtypeStruct((C * i.shape[0], L), table3.dtype) for i in idxs]

    @functools.partial(pl.kernel, out_type=out_type, mesh=_sc_mesh(), scratch_types=[])
    def gather(t_hbm, *refs):
        for s in range(ns):
            i_hbm, o_hbm = refs[s], refs[ns + s]
            nwin = idxs[s].shape[0] // SC_WINDOW
            for c in range(C):
                def body(i_vmem, o_vmem, c=c):
                    pltpu.sync_copy(t_hbm.at[c].at[i_vmem.at[0]], o_vmem)

                pltpu.emit_pipeline(
                    body, grid=(nwin,),
                    in_specs=[pl.BlockSpec((1, SC_WINDOW), lambda i: (0, i))],
                    out_specs=[pl.BlockSpec((SC_WINDOW, L), lambda i, c=c, nwin=nwin: (c * nwin + i, 0))],
                    core_axis_name=("core", "subcore"), dimension_semantics=(pltpu.PARALLEL,),
                )(i_hbm, o_hbm)

    outs = gather(table3, *[i.reshape(1, -1) for i in idxs])
    return [o.reshape(C, -1, L) for o in outs]


def _combine_kernel(x2_ref, g_ref, gate_ref, *rest):
    o_ref = rest[-1]
    gate = gate_ref[...]
    half = D_MODEL // 2
    for c in range(PK_CHUNKS):
        lo_s = slice(c * LANES, (c + 1) * LANES)
        hi_s = slice(half + c * LANES, half + (c + 1) * LANES)
        acc_lo = x2_ref[:, lo_s]
        acc_hi = x2_ref[:, hi_s]
        for k in range(TOP_K):
            w = g_ref[c, k]
            gk = gate[:, k:k + 1]
            acc_lo = acc_lo + pltpu.bitcast(w << 16, F32) * gk
            acc_hi = acc_hi + pltpu.bitcast(w & jnp.uint32(0xFFFF0000), F32) * gk
        o_ref[:, lo_s] = acc_lo
        o_ref[:, hi_s] = acc_hi


def _combine(x2, g4, gate, tm, out_buf=None, row0=0, t_total=None):
    T = x2.shape[0]
    t_total = T if t_total is None else t_total
    blk0 = row0 // tm
    in_specs = [pl.BlockSpec((tm, D_MODEL), lambda i: (i, 0)),
                pl.BlockSpec((PK_CHUNKS, TOP_K, tm, LANES), lambda i: (0, 0, i, 0)),
                pl.BlockSpec((tm, LANES), lambda i: (i, 0))]
    args = [x2, g4, gate]
    aliases = {}
    if out_buf is not None:
        in_specs.append(pl.BlockSpec(memory_space=pl.ANY))
        args.append(out_buf)
        aliases = {3: 0}
    return pl.pallas_call(
        _combine_kernel,
        grid=(T // tm,),
        in_specs=in_specs,
        out_specs=pl.BlockSpec((tm, D_MODEL), lambda i: (i + blk0, 0)),
        out_shape=jax.ShapeDtypeStruct((t_total, D_MODEL), F32),
        input_output_aliases=aliases,
        compiler_params=_cparams(("parallel",)),
        name="moe_combine",
    )(*args)


def _plan_kernel(idxr_ref, base_ref, pos_ref):
    idxr = idxr_ref[...].astype(F32)
    base = base_ref[...].astype(F32)
    lane = lax.broadcasted_iota(jnp.int32, idxr.shape, 1).astype(F32)
    out = jnp.zeros(idxr.shape, F32)
    for k in range(TOP_K):
        start = jnp.sum(jnp.where(lane == idxr[:, k:k + 1], base, 0.0), axis=-1, keepdims=True)
        out = jnp.where(lane == float(k), start + idxr[:, TOP_K + k:TOP_K + k + 1], out)
    pos_ref[...] = out.T[:TOP_K, :].astype(jnp.int32)


def _plan(idxr, base):
    T = idxr.shape[0]
    tm = _pick(T, (512, 256, 128))
    return pl.pallas_call(
        _plan_kernel,
        grid=(T // tm,),
        in_specs=[pl.BlockSpec((tm, LANES), lambda i: (i, 0)), _const_spec((1, LANES))],
        out_specs=pl.BlockSpec((TOP_K, tm), lambda i: (0, i)),
        out_shape=jax.ShapeDtypeStruct((TOP_K, T), jnp.int32),
        compiler_params=_cparams(("parallel",)),
        name="moe_plan",
    )(idxr, base)


def _moe(streams, places, ew):
    cnts = [st[3][0, :N_EXPERTS] for st in streams]
    total = sum(cnts)
    padded = (total + MOE_ROWS - 1) // MOE_ROWS * MOE_ROWS
    pad_end = jnp.cumsum(padded)
    pad_start = pad_end - padded
    pos_ts = []
    base = pad_start
    for (h3p, idxr, gate, counts, x2), cnt in zip(streams, cnts):
        pos_ts.append(_plan(idxr, _pad_lanes(base)))
        base = base + cnt
    n_assign = sum(st[0].shape[1] for st in streams) * TOP_K
    nb = -(-n_assign // MOE_ROWS) + N_EXPERTS
    starts = jnp.arange(nb, dtype=jnp.int32) * MOE_ROWS
    block_e = jnp.minimum(jnp.sum((pad_end[None, :] <= starts[:, None]).astype(jnp.int32), axis=1), N_EXPERTS - 1)
    n_used = (pad_end[-1] // MOE_ROWS).astype(jnp.int32).reshape(1)
    ar = jnp.arange(N_EXPERTS, dtype=jnp.int32)
    later = (padded > 0)[None, :] & (ar[None, :] > ar[:, None])
    nxt = jnp.min(jnp.where(later, ar[None, :], N_EXPERTS), axis=1)
    nxt = jnp.where(nxt >= N_EXPERTS, -1, nxt)
    mine = block_e[:, None] == ar[None, :]
    next_e = jnp.sum(jnp.where(mine, nxt[None, :], 0), axis=1).astype(jnp.int32)
    seg_end = jnp.sum(jnp.where(mine, (pad_start + total)[None, :], 0), axis=1)
    block_rows = jnp.clip(seg_end - starts, 0, MOE_ROWS).astype(jnp.int32)
    rows = _sc_scatter_rows([st[0] for st in streams], pos_ts, nb * MOE_ROWS)
    y_rows = _experts(block_e, next_e, block_rows, n_used, rows, ew['w_gu'], ew['b_gu'], ew['w_down'], ew['b_down'])
    gs = _sc_gather_rows(y_rows, [p.reshape(-1) for p in pos_ts])
    outs = []
    for (h3p, idxr, gate, counts, x2), g, place in zip(streams, gs, places):
        T = x2.shape[0]
        outs.append(_combine(x2, g.reshape(PK_CHUNKS, TOP_K, T, LANES), gate, _pick(T, (512, 256, 128, 64)), *place))
    return outs


def _pad_lanes(v, n=LANES, fill=0.0):
    return jnp.pad(v, (0, n - v.shape[0]), constant_values=fill).reshape(1, n)


def _prep_weights(norm_mix, w_in, conv_w, a_log, dt_bias, gdn_norm, q_a_norm, w_qb, kv_a_norm, w_kvb, q_norm,
                  k_nope_norm, k_rope_norm, w_out, norm_mem, mem_norm, w_mq, w_mk, w_mv, mq_norm, mk_norm, w_mo,
                  norm_ffn, w_router, b_router, w_gu, b_gu, w_down, b_down):
    c = np.cumsum([CONV_DIM, GDN_VW, GDN_HEADS, GDN_HEADS, Q_RANK, KV_RANK])
    w_u, w_z, w_a, w_b, w_cq, w_ckv, w_kpe = [w_in[:, lo:hi] for lo, hi in
                                              zip([0, *c], [*c, w_in.shape[1]])]
    w_s = jnp.concatenate([w_kpe, w_a, w_b], axis=1)
    w_s = jnp.pad(w_s, ((0, 0), (0, LANES - w_s.shape[1])))
    wq = w_qb.reshape(Q_RANK, MLA_HEADS, QK_DIM)
    wq = jnp.pad(wq, ((0, 0), (0, 0), (0, Q_SLAB - QK_DIM))).reshape(Q_RANK, MLA_HEADS * Q_SLAB)
    wkv = w_kvb.reshape(KV_RANK, MLA_HEADS, NOPE_DIM + V_DIM)
    w_k = wkv[:, :, :NOPE_DIM].reshape(KV_RANK, -1).astype(BF16)
    w_v = wkv[:, :, NOPE_DIM:].reshape(KV_RANK, -1).astype(BF16)
    wr = jnp.pad(w_router, ((0, 0), (0, LANES - N_EXPERTS)))
    wr_hi = wr.astype(BF16)
    wr_lo = (wr - wr_hi.astype(F32)).astype(BF16)
    row = lambda v: v.reshape(1, -1)
    gpad = ROPE_DIM
    pw = dict(
        nmix=row(norm_mix), w_u=w_u.astype(BF16), w_z=w_z.astype(BF16), w_cq=w_cq.astype(BF16),
        w_ckv=w_ckv.astype(BF16), w_s=w_s.astype(BF16), qan=row(q_a_norm), w_qb=wq.astype(BF16),
        qn=_pad_lanes(q_norm, Q_SLAB), kvan=row(kv_a_norm), krn=_pad_lanes(k_rope_norm),
        alog=jnp.pad(a_log, (gpad, LANES - gpad - GDN_HEADS)).reshape(1, LANES),
        dtb=jnp.pad(dt_bias, (gpad, LANES - gpad - GDN_HEADS)).reshape(1, LANES),
        conv_w=conv_w, gnorm=row(gdn_norm), w_k=w_k, w_v=w_v, w_vt=w_v.T, knn=row(k_nope_norm),
        w_out=w_out.astype(BF16), nmem=row(norm_mem), w_mq=w_mq.astype(BF16), mqn=row(mq_norm),
        w_mo=w_mo.astype(BF16), nffn=row(norm_ffn), wr_hi=wr_hi, wr_lo=wr_lo,
        b_r=_pad_lanes(b_router, LANES, NEG_BIG),
        mnorm=row(mem_norm), w_mk=w_mk.astype(BF16), w_mv=w_mv.astype(BF16), mkn=row(mk_norm),
    )
    ew = dict(w_gu=w_gu, b_gu=b_gu.reshape(N_EXPERTS, 1, 2 * D_FF), w_down=w_down,
              b_down=b_down.reshape(N_EXPERTS, 1, D_MODEL))
    return pw, ew


def _rope_tables(P, S):
    half = ROPE_DIM // 2
    inv = ROPE_THETA ** (-jnp.arange(half, dtype=F32) / half)
    ang = (P + jnp.arange(S, dtype=jnp.int32)).astype(F32)[:, None] * inv[None, :]
    cos, sin = jnp.cos(ang), jnp.sin(ang)
    zh = jnp.zeros((S, half), F32)
    zz = jnp.zeros((S, LANES - ROPE_DIM), F32)
    return (jnp.concatenate([cos, cos, zz], 1), jnp.concatenate([-sin, zh, zz], 1),
            jnp.concatenate([zh, sin, zz], 1))


def _pick(n, prefs):
    for t in prefs:
        if n % t == 0:
            return t
    return n


def _trunk_front(x, lat_past, kpe_past, s0, conv_past, mem_k, mem_v, pw, n_groups):
    B, S, D = x.shape
    P = lat_past.shape[1]
    T = B * S
    tm = _pick(S, (512, 256, 128, 64))
    tm_in = 512 if (T % 512 == 0 and (512 % S == 0 or S % 512 == 0)) else tm
    u, z, q, lat_new, small = _inproj(x.reshape(T, D), S, tm_in, pw, _rope_tables(P, S))

    LC = _pick(S, (256, 128, 64))
    NB = _pick(B, tuple(n for n in (8, 4, 2, 1) if n * (LC // CHUNK) <= GDN_UNITS))
    o_gdn, s_new, conv_new = _gdn(u.reshape(B, S, CONV_DIM), small.reshape(B, S, LANES), z.reshape(B, S, GDN_VW),
                                  pw['conv_w'], conv_past, s0, pw['gnorm'], NB, LC)

    kpe_new = small[:, :ROPE_DIM].reshape(B, S, ROPE_DIM)
    lat3 = lat_new.reshape(B, S, KV_RANK)
    small3 = small.reshape(B, S, LANES)
    q3 = q.reshape(B, S, MLA_HEADS * Q_SLAB)
    tq = _pick(S, (512, 256, 128, 64))
    if P == 0 and S % 512 == 0:
        kn, kr, vt = _kvproj(lat3, small3, pw, 512, True)
        o_mla = _flash_t(q3, kn, kr, vt, P, S, tq, 512)
    elif P > 0 and P % CHUNK == 0 and S == CHUNK:
        new = _kvproj(lat3, small3, pw, S, False)
        o_mla = _attn_dec(q3, lat_past, kpe_past, new, pw)
    else:
        sk = P + S
        tk = 512 if S >= 512 else -(-sk // LANES) * LANES
        skp = -(-sk // tk) * tk
        lat_all = jnp.pad(jnp.concatenate([lat_past, lat3], axis=1), ((0, 0), (0, skp - sk), (0, 0)))
        kpe_all = jnp.pad(jnp.concatenate([kpe_past, kpe_new], axis=1), ((0, 0), (0, skp - sk), (0, 0)))
        key_major = tq >= LANES
        kn, kr, v = _kvproj(lat_all, kpe_all, pw, _pick(skp, (512,)), key_major)
        o_mla = (_flash_t if key_major else _flash)(q3, kn, kr, v, P, S, tq, tk)

    streams = []
    gb = B // n_groups
    for gi in range(n_groups):
        x2, h3p, idxr, gate, counts = _mix(x, o_gdn, o_mla, mem_k, mem_v, pw, tm, gi * gb, gb)
        tg = gb * S
        streams.append((h3p, idxr.reshape(tg, LANES), gate.reshape(tg, LANES), counts, x2.reshape(tg, D)))
    return streams, (lat_new.reshape(B, S, KV_RANK), kpe_new, s_new, conv_new)


def kernel(x_prompt, x_sample, cache_kv_latent, cache_k_rope, state_gdn, state_conv, cache_mem_k, cache_mem_v, mem_prompt, norm_mix, w_in, conv_w, a_log, dt_bias, gdn_norm, q_a_norm, w_qb, kv_a_norm, w_kvb, q_norm, k_nope_norm, k_rope_norm, w_out, norm_mem, mem_norm, w_mq, w_mk, w_mv, mq_norm, mk_norm, w_mo, norm_ffn, w_router, b_router, w_gu, b_gu, w_down, b_down):
    depth = norm_mix.shape[0]
    yp, ys = x_prompt, x_sample
    bp = x_prompt.shape[0]
    hw = MEM_HEADS * MEM_HEAD_DIM
    outs = [[] for _ in range(10)]
    for l in range(depth):
        pw, ew = _prep_weights(norm_mix[l], w_in[l], conv_w[l], a_log[l], dt_bias[l], gdn_norm[l], q_a_norm[l],
                               w_qb[l], kv_a_norm[l], w_kvb[l], q_norm[l], k_nope_norm[l], k_rope_norm[l], w_out[l],
                               norm_mem[l], mem_norm[l], w_mq[l], w_mk[l], w_mv[l], mq_norm[l], mk_norm[l], w_mo[l],
                               norm_ffn[l], w_router[l], b_router[l], w_gu[l], b_gu[l], w_down[l], b_down[l])
        nm = mem_prompt.shape[1]
        mk, mv = _memkv(mem_prompt.reshape(bp * nm, D_MODEL), pw['mnorm'], pw['w_mk'], pw['w_mv'], pw['mkn'],
                        _pick(bp * nm, (512, 256)))
        mk = mk.reshape(bp, nm, hw)
        mv = mv.reshape(bp, nm, hw)
        n_groups = 2 if bp % 2 == 0 else 1
        streams_p, (lat, kpe, s_fin, cv) = _trunk_front(
            yp, jnp.zeros((bp, 0, KV_RANK), F32), jnp.zeros((bp, 0, ROPE_DIM), F32),
            jnp.zeros((bp, GDN_HEADS, GDN_DK, GDN_DV), F32), jnp.zeros((bp, CONV_W - 1, CONV_DIM), F32), mk, mv, pw,
            n_groups)
        bs = x_sample.shape[0]
        (stream_s,), (lat2, kpe2, s_fin2, cv2) = _trunk_front(
            ys, cache_kv_latent[l], cache_k_rope[l], state_gdn[l], state_conv[l],
            cache_mem_k[l], cache_mem_v[l], pw, 1)
        tp = yp.shape[0] * yp.shape[1]
        tg = tp // n_groups
        ybuf = None
        for gi, st in enumerate(streams_p):
            last = gi == n_groups - 1
            res = _moe([st] + ([stream_s] if last else []),
                       [(ybuf, gi * tg, tp)] + ([(None, 0, None)] if last else []), ew)
            ybuf = res[0]
        yp, ys = ybuf.reshape(yp.shape), res[1].reshape(ys.shape)
        for lst, val in zip(outs, (lat, kpe, s_fin, cv, mk.reshape(bp, nm, MEM_HEADS, MEM_HEAD_DIM),
                                   mv.reshape(bp, nm, MEM_HEADS, MEM_HEAD_DIM), lat2, kpe2, s_fin2, cv2)):
            lst.append(val)
    return (yp, ys) + tuple(jnp.stack(o) for o in outs)
```

```python
import functools
import math

import numpy as np
import jax
import jax.numpy as jnp
from jax import lax
from jax.experimental import pallas as pl
from jax.experimental.pallas import tpu as pltpu
from jax.experimental.pallas import tpu_sc as plsc

F32 = jnp.float32
BF16 = jnp.bfloat16

D_MODEL = 1024
CHUNK = 64
EPS = 1e-6
GDN_HEADS = 4
GDN_DK = 128
GDN_DV = 128
CONV_W = 4
GDN_QK = GDN_HEADS * GDN_DK
GDN_VW = GDN_HEADS * GDN_DV
CONV_DIM = 2 * GDN_QK + GDN_VW
MLA_HEADS = 4
Q_RANK = 384
KV_RANK = 256
NOPE_DIM = 128
ROPE_DIM = 64
V_DIM = 128
QK_DIM = NOPE_DIM + ROPE_DIM
ROPE_THETA = 10000.0
N_MEM = 256
MEM_HEADS = 4
MEM_HEAD_DIM = 128
N_EXPERTS = 32
TOP_K = 4
D_FF = D_MODEL
SWIGLU_ALPHA = 1.702
SWIGLU_LIMIT = 7.0

LANES = 128
Q_SLAB = 2 * LANES
NEG_BIG = -1e30
VMEM_LIMIT = 56 * 1024 * 1024
MOE_ROWS = 512
GDN_UNITS = 8
GDN_GROUP = 2
FF_CHUNK = 512
SC_WINDOW = 128
PK_CHUNKS = D_MODEL // 2 // LANES
SUBTILE_ROWS = 256
ONES_ROWS = 16
MIX_ROWS = 512


def _cparams(sem):
    return pltpu.CompilerParams(dimension_semantics=sem, vmem_limit_bytes=VMEM_LIMIT)


def _dot(a, b):
    return jnp.dot(a, b, preferred_element_type=F32)


def _dot_nt(a, b):
    return lax.dot_general(a, b, (((1,), (1,)), ((), ())), preferred_element_type=F32)


def _dot_tn(a, b):
    return lax.dot_general(a, b, (((0,), (0,)), ((), ())), preferred_element_type=F32)


def _rms(x, gain, n=None):
    n = x.shape[-1] if n is None else n
    ss = jnp.sum(x * x, axis=-1, keepdims=True) * (1.0 / n)
    return (x * lax.rsqrt(ss + EPS)) * gain


def _sigmoid(x):
    return 1.0 / (1.0 + jnp.exp(-x))


def _rope128(r, cos, sna, snb):
    return r * cos + pltpu.roll(r, 96, 1) * sna + pltpu.roll(r, 32, 1) * snb


def _pack_bf16_pairs(x):
    n = x.shape[1] // 2
    lo = pltpu.bitcast(x[:, :n], jnp.uint32) >> 16
    hi = pltpu.bitcast(x[:, n:], jnp.uint32) & jnp.uint32(0xFFFF0000)
    return lo | hi


def _unpack_bf16_pairs(p):
    lo = pltpu.bitcast(p << 16, F32)
    hi = pltpu.bitcast(p & jnp.uint32(0xFFFF0000), F32)
    return jnp.concatenate([lo, hi], axis=1)


def _subtiles(rows):
    n = rows // SUBTILE_ROWS if rows % SUBTILE_ROWS == 0 else 1
    step = rows // n
    return [slice(i * step, (i + 1) * step) for i in range(n)]


def _const_spec(shape):
    nd = len(shape)
    return pl.BlockSpec(shape, lambda *_: (0,) * nd)


def _inproj_kernel(x_ref, nmix_ref, wu_ref, wz_ref, wcq_ref, wckv_ref, ws_ref, qan_ref, wqb_ref, qn_ref,
                   kvan_ref, krn_ref, alog_ref, dtb_ref, cos_ref, sna_ref, snb_ref,
                   u_ref, z_ref, q_ref, lat_ref, small_ref):
    for rs in _subtiles(x_ref.shape[0]):
        x = x_ref[rs, :]
        hb = _rms(x, nmix_ref[...]).astype(BF16)
        u_ref[rs, :] = _dot(hb, wu_ref[...])
        z_ref[rs, :] = _dot(hb, wz_ref[...])
        cq_raw = _dot(hb, wcq_ref[...])
        ckv_raw = _dot(hb, wckv_ref[...])
        sm = _dot(hb, ws_ref[...])
        cos, sna, snb = cos_ref[rs, :], sna_ref[rs, :], snb_ref[rs, :]

        cq = _rms(cq_raw, qan_ref[...]).astype(BF16)
        qf = _dot(cq, wqb_ref[...])
        scale = QK_DIM ** -0.5
        for h in range(MLA_HEADS):
            slab = qf[:, h * Q_SLAB:(h + 1) * Q_SLAB]
            slab = _rms(slab, qn_ref[...], n=QK_DIM)
            nope = slab[:, :LANES]
            ropd = _rope128(slab[:, LANES:], cos, sna, snb)
            q_ref[rs, h * Q_SLAB:h * Q_SLAB + LANES] = (nope * scale).astype(BF16)
            q_ref[rs, h * Q_SLAB + LANES:(h + 1) * Q_SLAB] = (ropd * scale).astype(BF16)

        lat_ref[rs, :] = _rms(ckv_raw, kvan_ref[...])

        lane = lax.broadcasted_iota(jnp.int32, sm.shape, 1)
        kp = jnp.where(lane < ROPE_DIM, sm, 0.0)
        kpe = _rope128(_rms(kp, krn_ref[...], n=ROPE_DIM), cos, sna, snb)
        sp = sm + dtb_ref[...]
        softplus = jnp.maximum(sp, 0.0) + jnp.log1p(jnp.exp(-jnp.abs(sp)))
        g = -jnp.exp(alog_ref[...]) * softplus
        beta = _sigmoid(sm)
        small_ref[rs, :] = jnp.where(lane < ROPE_DIM, kpe,
                                     jnp.where(lane < ROPE_DIM + GDN_HEADS, g,
                                               jnp.where(lane < ROPE_DIM + 2 * GDN_HEADS, beta, 0.0)))


def _inproj(x2d, S, tm, pw, tabs):
    T = x2d.shape[0]
    if tm > S:
        tabs = [jnp.tile(t, (tm // S, 1)) for t in tabs]
    nblk_s = max(S // tm, 1)
    row = lambda n: pl.BlockSpec((tm, n), lambda i: (i, 0))
    tab = pl.BlockSpec((tm, LANES), lambda i: (i % nblk_s, 0))
    consts = [pw['nmix'], pw['w_u'], pw['w_z'], pw['w_cq'], pw['w_ckv'], pw['w_s'], pw['qan'], pw['w_qb'],
              pw['qn'], pw['kvan'], pw['krn'], pw['alog'], pw['dtb']]
    return pl.pallas_call(
        _inproj_kernel,
        grid=(T // tm,),
        in_specs=[row(D_MODEL)] + [_const_spec(c.shape) for c in consts] + [tab, tab, tab],
        out_specs=[row(CONV_DIM), row(GDN_VW), row(MLA_HEADS * Q_SLAB), row(KV_RANK), row(LANES)],
        out_shape=[jax.ShapeDtypeStruct((T, CONV_DIM), F32), jax.ShapeDtypeStruct((T, GDN_VW), F32),
                   jax.ShapeDtypeStruct((T, MLA_HEADS * Q_SLAB), BF16), jax.ShapeDtypeStruct((T, KV_RANK), F32),
                   jax.ShapeDtypeStruct((T, LANES), F32)],
        compiler_params=_cparams(("parallel",)),
        name="inproj",
    )(x2d, *consts, *tabs)


def _split3(x):
    hi = x.astype(BF16)
    r = x - hi.astype(F32)
    mid = r.astype(BF16)
    lo = (r - mid.astype(F32)).astype(BF16)
    return hi, mid, lo


def _gdn_kernel(u_ref, small_ref, z_ref, convw_ref, cpast_ref, s0_ref, gnorm_ref,
                to_ref, trilm_ref, strictm_ref, same2_ref, lvl_ref,
                o_ref, sfin_ref, cnew_ref, ext_ref, uc_ref, state_ref,
                qf_ref, kf_ref, vb_ref, bt_ref, gcum_ref, glast_ref, kk_ref, qk_ref, mb_ref, x_ref, qkb_ref,
                kbe_ref, qg_ref, kdec_ref, egl_ref, t1_ref, uu_ref, ww_ref, vn_ref, qs_ref, *, NB, LC):
    j = pl.program_id(1)
    nj = pl.num_programs(1)
    PADR = 8
    C = LC // CHUNK
    NG = GDN_HEADS // GDN_GROUP
    U = NB * C * NG
    HR = GDN_GROUP * CHUNK

    def unit(nb, c, g):
        return (nb * C + c) * NG + g

    def head_rows(h):
        return slice((h % GDN_GROUP) * CHUNK, (h % GDN_GROUP + 1) * CHUNK)

    @pl.when(j == 0)
    def _():
        state_ref[...] = s0_ref[...]
        ext_ref[:, PADR - (CONV_W - 1):PADR, :] = cpast_ref[...]

    w = convw_ref[...]
    for nb in range(NB):
        ext_ref[nb, PADR:PADR + LC, :] = u_ref[nb]
        acc = ext_ref[nb, PADR:PADR + LC, :] * w[CONV_W - 1:CONV_W, :]
        for t in range(1, CONV_W):
            acc = acc + ext_ref[nb, PADR - t:PADR - t + LC, :] * w[CONV_W - 1 - t:CONV_W - t, :]
        uc_ref[nb] = acc * _sigmoid(acc)
        ext_ref[nb, 0:PADR, :] = ext_ref[nb, LC:LC + PADR, :]

    @pl.when(j == nj - 1)
    def _():
        cnew_ref[...] = ext_ref[:, PADR - (CONV_W - 1):PADR, :]

    g0 = ROPE_DIM
    b0 = ROPE_DIM + GDN_HEADS
    to = to_ref[...]
    for nb, c in [(nb, c) for nb in range(NB) for c in range(C)]:
        rows = slice(c * CHUNK, (c + 1) * CHUNK)
        sm = small_ref[nb, rows, :]
        gl = sum(_dot(to, part) for part in _split3(sm))
        for h in range(GDN_HEADS):
            u = unit(nb, c, h // GDN_GROUP)
            hr = head_rows(h)
            q = uc_ref[nb, rows, h * GDN_DK:(h + 1) * GDN_DK]
            k = uc_ref[nb, rows, GDN_QK + h * GDN_DK:GDN_QK + (h + 1) * GDN_DK]
            v = uc_ref[nb, rows, 2 * GDN_QK + h * GDN_DV:2 * GDN_QK + (h + 1) * GDN_DV]
            beta = jnp.broadcast_to(sm[:, b0 + h:b0 + h + 1], (CHUNK, LANES))
            qf_ref[u, hr, :] = (q * lax.rsqrt(jnp.sum(q * q, -1, keepdims=True) + EPS)) * (GDN_DK ** -0.5)
            kf_ref[u, hr, :] = k * lax.rsqrt(jnp.sum(k * k, -1, keepdims=True) + EPS)
            vb_ref[u, hr, :] = (v * beta).astype(BF16)
            bt_ref[u, hr, :] = beta
            gcum_ref[u, hr, :] = jnp.broadcast_to(gl[:CHUNK, g0 + h:g0 + h + 1], (CHUNK, LANES))
            glast_ref[u, hr, :] = jnp.broadcast_to(gl[CHUNK:, g0 + h:g0 + h + 1], (CHUNK, LANES))

    for u in range(U):
        k = kf_ref[u]
        kbf = k.astype(BF16)
        kk_ref[u] = _dot_nt((k * bt_ref[u]).astype(BF16), kbf)
        qk_ref[u] = _dot_nt(qf_ref[u].astype(BF16), kbf)

    trilm = trilm_ref[...]
    eye = trilm - strictm_ref[...]
    for u in range(U):
        gcum = gcum_ref[u]
        grow = gcum.T[0:1, :]
        gcol = gcum if HR == LANES else jnp.concatenate([gcum] * (HR // LANES), axis=1)
        decay = jnp.exp(jnp.minimum(gcol - grow, 0.0)) * trilm
        m = kk_ref[u] * (decay * strictm_ref[...])
        mb_ref[u] = m.astype(BF16)
        x_ref[u] = eye - m * same2_ref[...]
        qkb_ref[u] = (qk_ref[u] * decay).astype(BF16)
        egc = jnp.exp(gcum)
        k = kf_ref[u]
        kbe_ref[u] = (k * bt_ref[u] * egc).astype(BF16)
        qg_ref[u] = (qf_ref[u] * egc).astype(BF16)
        kdec_ref[u] = (k * jnp.exp(glast_ref[u] - gcum)).astype(BF16)
        egl_ref[u] = jnp.exp(glast_ref[u])

    for lvl in range(lvl_ref.shape[0]):
        lm = lvl_ref[lvl]
        for u in range(U):
            t1_ref[u] = _dot(mb_ref[u] * lm, x_ref[u].astype(BF16)).astype(BF16)
        for u in range(U):
            x = x_ref[u]
            x_ref[u] = x - _dot(x.astype(BF16), t1_ref[u])

    for u in range(U):
        xb = x_ref[u].astype(BF16)
        uu_ref[u] = _dot(xb, vb_ref[u])
        ww_ref[u] = _dot(xb, kbe_ref[u]).astype(BF16)

    gnorm = gnorm_ref[...]
    for c in range(C):
        rows = slice(c * CHUNK, (c + 1) * CHUNK)
        heads = [(nb, h, unit(nb, c, h // GDN_GROUP), head_rows(h)) for nb in range(NB) for h in range(GDN_HEADS)]
        groups = [(nb, g, unit(nb, c, g)) for nb in range(NB) for g in range(NG)]
        for nb, h, u, hr in heads:
            stb = state_ref[nb, h].astype(BF16)
            r = _dot(jnp.concatenate([ww_ref[u, hr, :], qg_ref[u, hr, :]], axis=0), stb)
            vn_ref[u, hr, :] = (uu_ref[u, hr, :] - r[:CHUNK]).astype(BF16)
            qs_ref[u, hr, :] = r[CHUNK:]
        outs = {(nb, g): qs_ref[u] + _dot(qkb_ref[u], vn_ref[u]) for nb, g, u in groups}
        for nb, h, u, hr in heads:
            state_ref[nb, h] = (state_ref[nb, h] * egl_ref[u, hr.start:hr.start + 1, :]
                                + _dot_tn(kdec_ref[u, hr, :], vn_ref[u, hr, :]))
        for nb, h, u, hr in heads:
            zz = z_ref[nb, rows, h * GDN_DV:(h + 1) * GDN_DV]
            og = _rms(outs[(nb, h // GDN_GROUP)][hr, :], gnorm) * (zz * _sigmoid(zz))
            o_ref[nb, rows, h * GDN_DV:(h + 1) * GDN_DV] = og.astype(BF16)

    @pl.when(j == nj - 1)
    def _():
        sfin_ref[...] = state_ref[...]


def _gdn_masks():
    hr = GDN_GROUP * CHUNK
    i = np.arange(hr)[:, None]
    j = np.arange(hr)[None, :]
    same_head = (i // CHUNK) == (j // CHUNK)
    tril = same_head & (i >= j)
    strict = same_head & (i > j)
    same2 = strict & ((i // 2) == (j // 2))
    lvls = []
    blk = 2
    while blk < CHUNK:
        lvls.append(strict & ((i // (2 * blk)) == (j // (2 * blk))) & ((i // blk) != (j // blk)))
        blk *= 2
    fr = np.arange(CHUNK)
    to = np.concatenate([fr[:, None] >= fr[None, :], np.ones((CHUNK, CHUNK), bool)], axis=0)
    f = lambda a: jnp.asarray(a.astype(np.float32))
    return (jnp.asarray(to.astype(np.float32), dtype=BF16), f(tril), f(strict), f(same2),
            jnp.asarray(np.stack(lvls).astype(np.float32), dtype=BF16))


def _gdn(u3, small3, z3, conv_w, conv_past, s0, gnorm, NB, LC):
    B, S, _ = u3.shape
    C = LC // CHUNK
    U = NB * C * (GDN_HEADS // GDN_GROUP)
    HR = GDN_GROUP * CHUNK
    masks = _gdn_masks()
    tile = lambda n: pl.BlockSpec((NB, LC, n), lambda b, j: (b, j, 0))
    stspec = pl.BlockSpec((NB, GDN_HEADS, GDN_DK, GDN_DV), lambda b, j: (b, 0, 0, 0))
    cvspec = pl.BlockSpec((NB, CONV_W - 1, CONV_DIM), lambda b, j: (b, 0, 0))
    vm = lambda shape, dt: pltpu.VMEM(shape, dt)
    return pl.pallas_call(
        functools.partial(_gdn_kernel, NB=NB, LC=LC),
        grid=(B // NB, S // LC),
        in_specs=[tile(CONV_DIM), tile(LANES), tile(GDN_VW), _const_spec(conv_w.shape), cvspec, stspec,
                  _const_spec(gnorm.shape)] + [_const_spec(m.shape) for m in masks],
        out_specs=[tile(GDN_VW), stspec, cvspec],
        out_shape=[jax.ShapeDtypeStruct((B, S, GDN_VW), BF16),
                   jax.ShapeDtypeStruct((B, GDN_HEADS, GDN_DK, GDN_DV), F32),
                   jax.ShapeDtypeStruct((B, CONV_W - 1, CONV_DIM), F32)],
        scratch_shapes=[vm((NB, LC + 8, CONV_DIM), F32), vm((NB, LC, CONV_DIM), F32),
                        vm((NB, GDN_HEADS, GDN_DK, GDN_DV), F32),
                        vm((U, HR, LANES), F32), vm((U, HR, LANES), F32), vm((U, HR, LANES), BF16),
                        vm((U, HR, LANES), F32),
                        vm((U, HR, LANES), F32), vm((U, HR, LANES), F32),
                        vm((U, HR, HR), F32), vm((U, HR, HR), F32),
                        vm((U, HR, HR), BF16), vm((U, HR, HR), F32), vm((U, HR, HR), BF16),
                        vm((U, HR, LANES), BF16), vm((U, HR, LANES), BF16), vm((U, HR, LANES), BF16),
                        vm((U, HR, LANES), F32), vm((U, HR, HR), BF16),
                        vm((U, HR, LANES), F32), vm((U, HR, LANES), BF16),
                        vm((U, HR, LANES), BF16), vm((U, HR, LANES), F32)],
        compiler_params=_cparams(("parallel", "arbitrary")),
        name="gdn",
    )(u3, small3, z3, conv_w, conv_past, s0, gnorm, *masks)


def _kvproj_kernel(lat_ref, kpe_ref, wk_ref, wv_ref, knn_ref, kn_ref, kr_ref, v_ref, *, v_transposed):
    lb = lat_ref[...].astype(BF16)
    kf = _dot(lb, wk_ref[...])
    for h in range(MLA_HEADS):
        sl = slice(h * NOPE_DIM, (h + 1) * NOPE_DIM)
        kn_ref[:, sl] = _rms(kf[:, sl], knn_ref[...]).astype(BF16)
    if v_transposed:
        v_ref[...] = _dot_nt(wv_ref[...], lb).astype(BF16)
    else:
        v_ref[...] = _dot(lb, wv_ref[...]).astype(BF16)
    kp = kpe_ref[...]
    if kp.shape[1] == ROPE_DIM:
        kr_ref[...] = jnp.concatenate([kp, jnp.zeros((kp.shape[0], LANES - ROPE_DIM), F32)], axis=1).astype(BF16)
    else:
        lane = lax.broadcasted_iota(jnp.int32, kp.shape, 1)
        kr_ref[...] = jnp.where(lane < ROPE_DIM, kp, 0.0).astype(BF16)


def _kvproj(lat3, kpe3, pw, tm, v_transposed):
    B, sk, _ = lat3.shape
    hw = MLA_HEADS * V_DIM
    row = lambda n: pl.BlockSpec((None, tm, n), lambda b, i: (b, i, 0))
    if v_transposed:
        vspec, vshape, w_v = pl.BlockSpec((None, hw, tm), lambda b, i: (b, 0, i)), (B, hw, sk), pw['w_vt']
    else:
        vspec, vshape, w_v = row(hw), (B, sk, hw), pw['w_v']
    return pl.pallas_call(
        functools.partial(_kvproj_kernel, v_transposed=v_transposed),
        grid=(B, sk // tm),
        in_specs=[row(KV_RANK), row(kpe3.shape[-1]), _const_spec(pw['w_k'].shape), _const_spec(w_v.shape),
                  _const_spec(pw['knn'].shape)],
        out_specs=[row(MLA_HEADS * NOPE_DIM), row(LANES), vspec],
        out_shape=[jax.ShapeDtypeStruct((B, sk, MLA_HEADS * NOPE_DIM), BF16),
                   jax.ShapeDtypeStruct((B, sk, LANES), BF16), jax.ShapeDtypeStruct(vshape, BF16)],
        compiler_params=_cparams(("parallel", "parallel")),
        name="kvproj",
    )(lat3, kpe3, pw['w_k'], w_v, pw['knn'])


def _last_kblock(qi, tq, tk, P, nk):
    last_key = ((P + qi * tq + tq - 1) // CHUNK) * CHUNK + CHUNK - 1
    return jnp.minimum(last_key // tk, nk - 1)


def _flash_kernel(q_ref, kn_ref, kr_ref, v_ref, o_ref, m_ref, l_ref, acc_ref, *, tq, tk, P, S, nk):
    qi = pl.program_id(1)
    ki = pl.program_id(2)

    @pl.when(ki == 0)
    def _():
        m_ref[...] = jnp.full(m_ref.shape, NEG_BIG, F32)
        l_ref[...] = jnp.zeros(l_ref.shape, F32)
        acc_ref[...] = jnp.zeros(acc_ref.shape, F32)

    @pl.when(ki <= _last_kblock(qi, tq, tk, P, nk))
    def _():
        qpos = P + qi * tq + lax.broadcasted_iota(jnp.int32, (tq, tk), 0)
        kpos = ki * tk + lax.broadcasted_iota(jnp.int32, (tq, tk), 1)
        mask = ((kpos // CHUNK) <= (qpos // CHUNK)) & (kpos < P + S)
        kr = kr_ref[...]
        for h in range(MLA_HEADS):
            qh = q_ref[:, h * Q_SLAB:(h + 1) * Q_SLAB]
            kh = jnp.concatenate([kn_ref[:, h * NOPE_DIM:(h + 1) * NOPE_DIM], kr], axis=1)
            s = jnp.where(mask, _dot_nt(qh, kh), NEG_BIG)
            m_prev = m_ref[h]
            m_new = jnp.maximum(m_prev, jnp.max(s, axis=-1, keepdims=True))
            alpha = jnp.exp(m_prev - m_new)
            p = jnp.exp(s - m_new)
            l_ref[h] = alpha * l_ref[h] + jnp.sum(p, axis=-1, keepdims=True)
            acc_ref[h] = alpha * acc_ref[h] + _dot(p.astype(BF16), v_ref[:, h * V_DIM:(h + 1) * V_DIM])
            m_ref[h] = m_new

    @pl.when(ki == nk - 1)
    def _():
        for h in range(MLA_HEADS):
            o_ref[:, h * V_DIM:(h + 1) * V_DIM] = (acc_ref[h] / l_ref[h]).astype(BF16)


def _flash(q3, kn3, kr3, v3, P, S, tq, tk):
    B = q3.shape[0]
    skp = kn3.shape[1]
    nk = skp // tk
    kmap = lambda b, qi, ki: (b, jnp.minimum(ki, _last_kblock(qi, tq, tk, P, nk)), 0)
    return pl.pallas_call(
        functools.partial(_flash_kernel, tq=tq, tk=tk, P=P, S=S, nk=nk),
        grid=(B, S // tq, nk),
        in_specs=[pl.BlockSpec((None, tq, MLA_HEADS * Q_SLAB), lambda b, qi, ki: (b, qi, 0)),
                  pl.BlockSpec((None, tk, MLA_HEADS * NOPE_DIM), kmap),
                  pl.BlockSpec((None, tk, LANES), kmap),
                  pl.BlockSpec((None, tk, MLA_HEADS * V_DIM), kmap)],
        out_specs=pl.BlockSpec((None, tq, MLA_HEADS * V_DIM), lambda b, qi, ki: (b, qi, 0)),
        out_shape=jax.ShapeDtypeStruct((B, S, MLA_HEADS * V_DIM), BF16),
        scratch_shapes=[pltpu.VMEM((MLA_HEADS, tq, 1), F32), pltpu.VMEM((MLA_HEADS, tq, 1), F32),
                        pltpu.VMEM((MLA_HEADS, tq, V_DIM), F32)],
        compiler_params=_cparams(("parallel", "parallel", "arbitrary")),
        name="mla_attn",
    )(q3, kn3, kr3, v3)


def _attn_dec_kernel(q_ref, lat_ref, kpe_ref, knn_ref, krn_ref, vn_ref, wk_ref, wv_ref, gain_ref, o_ref):
    lb = lat_ref[...].astype(BF16)
    kf = _dot(lb, wk_ref[...])
    vp = _dot(lb, wv_ref[...]).astype(BF16)
    kp = kpe_ref[...]
    krp = jnp.concatenate([kp, jnp.zeros((kp.shape[0], LANES - ROPE_DIM), F32)], axis=1).astype(BF16)
    krn = krn_ref[...]
    hsl = [slice(h * NOPE_DIM, (h + 1) * NOPE_DIM) for h in range(MLA_HEADS)]
    qs = [q_ref[:, h * Q_SLAB:(h + 1) * Q_SLAB] for h in range(MLA_HEADS)]
    knp = [_rms(kf[:, sl], gain_ref[...]).astype(BF16) for sl in hsl]
    sps = [_dot_nt(q, jnp.concatenate([k, krp], axis=1)) for q, k in zip(qs, knp)]
    sns = [_dot_nt(q, jnp.concatenate([knn_ref[:, sl], krn], axis=1)) for q, sl in zip(qs, hsl)]
    for h, (sp, sn) in enumerate(zip(sps, sns)):
        vsl = slice(h * V_DIM, (h + 1) * V_DIM)
        m = jnp.maximum(jnp.max(sp, axis=-1, keepdims=True), jnp.max(sn, axis=-1, keepdims=True))
        pp = jnp.exp(sp - m)
        pn = jnp.exp(sn - m)
        l = jnp.sum(pp, axis=-1, keepdims=True) + jnp.sum(pn, axis=-1, keepdims=True)
        o = _dot(pp.astype(BF16), vp[:, vsl]) + _dot(pn.astype(BF16), vn_ref[:, vsl])
        o_ref[:, vsl] = (o / l).astype(BF16)


def _attn_dec(q3, lat_past, kpe_past, new, pw):
    B, S, _ = q3.shape
    consts = [pw['w_k'], pw['w_v'], pw['knn']]
    specs = [pl.BlockSpec((None, a.shape[1], a.shape[2]), lambda b: (b, 0, 0)) for a in (q3, lat_past, kpe_past, *new)]
    return pl.pallas_call(
        _attn_dec_kernel,
        grid=(B,),
        in_specs=specs + [_const_spec(c.shape) for c in consts],
        out_specs=pl.BlockSpec((None, S, MLA_HEADS * V_DIM), lambda b: (b, 0, 0)),
        out_shape=jax.ShapeDtypeStruct((B, S, MLA_HEADS * V_DIM), BF16),
        compiler_params=_cparams(("parallel",)),
        name="mla_attn_dec",
    )(q3, lat_past, kpe_past, *new, *consts)


def _flash_t_kernel(qt_ref, kt_ref, lt_ref, q_ref, kn_ref, kr_ref, vt_ref, o_ref, m_ref, acc_ref, *, tq, tk, P, S):
    j = pl.program_id(1)
    qi = qt_ref[j]
    ki = kt_ref[j]
    q0 = P + qi * tq
    k0 = ki * tk

    @pl.when(ki == 0)
    def _():
        m_ref[...] = jnp.full(m_ref.shape, NEG_BIG, F32)
        acc_ref[...] = jnp.zeros(acc_ref.shape, F32)

    def step(masked):
        kr = kr_ref[...]
        ones = jnp.ones((ONES_ROWS, tk), BF16)
        if masked:
            kpos = k0 + lax.broadcasted_iota(jnp.int32, (tk, 1), 0)
            qpos = q0 + lax.broadcasted_iota(jnp.int32, (1, tq), 1)
            mask = ((kpos // CHUNK) <= (qpos // CHUNK)) & (kpos < P + S)
        sts = []
        for h in range(MLA_HEADS):
            kh = jnp.concatenate([kn_ref[:, h * NOPE_DIM:(h + 1) * NOPE_DIM], kr], axis=1)
            sts.append(_dot_nt(kh, q_ref[:, h * Q_SLAB:(h + 1) * Q_SLAB]))
        ps, alphas = [], []
        for h in range(MLA_HEADS):
            st = sts[h]
            if masked:
                st = jnp.where(mask, st, NEG_BIG)
            m_prev = m_ref[h]
            m_new = jnp.maximum(m_prev, jnp.max(st, axis=0, keepdims=True))
            alphas.append(jnp.exp(m_prev - m_new))
            ps.append(jnp.exp(st - m_new).astype(BF16))
            m_ref[h] = m_new
        for h in range(MLA_HEADS):
            vt1 = jnp.concatenate([vt_ref[h * V_DIM:(h + 1) * V_DIM, :], ones], axis=0)
            acc_ref[h] = alphas[h] * acc_ref[h] + _dot(vt1, ps[h])

    full = ((k0 + tk - 1) // CHUNK <= q0 // CHUNK) & (k0 + tk <= P + S)
    pl.when(full)(functools.partial(step, False))
    pl.when(jnp.logical_not(full))(functools.partial(step, True))

    @pl.when(lt_ref[j] == 1)
    def _():
        for h in range(MLA_HEADS):
            acc = acc_ref[h]
            o_ref[:, h * V_DIM:(h + 1) * V_DIM] = (acc[:V_DIM] / acc[V_DIM:V_DIM + 1]).T.astype(BF16)


def _flash_t(q3, kn3, kr3, vt3, P, S, tq, tk):
    B = q3.shape[0]
    nk = kn3.shape[1] // tk
    pairs = []
    for qi in range(S // tq):
        last = min((((P + qi * tq + tq - 1) // CHUNK) * CHUNK + CHUNK - 1) // tk, nk - 1)
        pairs += [(qi, ki, int(ki == last)) for ki in range(last + 1)]
    qt, kt, lt = (jnp.asarray(np.array(col, np.int32)) for col in zip(*pairs))
    kmap = lambda b, j, qt, kt, lt: (b, kt[j], 0)
    gs = pltpu.PrefetchScalarGridSpec(
        num_scalar_prefetch=3,
        grid=(B, len(pairs)),
        in_specs=[pl.BlockSpec((None, tq, MLA_HEADS * Q_SLAB), lambda b, j, qt, kt, lt: (b, qt[j], 0)),
                  pl.BlockSpec((None, tk, MLA_HEADS * NOPE_DIM), kmap),
                  pl.BlockSpec((None, tk, LANES), kmap),
                  pl.BlockSpec((None, MLA_HEADS * V_DIM, tk), lambda b, j, qt, kt, lt: (b, 0, kt[j]))],
        out_specs=pl.BlockSpec((None, tq, MLA_HEADS * V_DIM), lambda b, j, qt, kt, lt: (b, qt[j], 0)),
        scratch_shapes=[pltpu.VMEM((MLA_HEADS, 1, tq), F32), pltpu.VMEM((MLA_HEADS, V_DIM + ONES_ROWS, tq), F32)],
    )
    return pl.pallas_call(
        functools.partial(_flash_t_kernel, tq=tq, tk=tk, P=P, S=S),
        grid_spec=gs,
        out_shape=jax.ShapeDtypeStruct((B, S, MLA_HEADS * V_DIM), BF16),
        compiler_params=_cparams(("parallel", "arbitrary")),
        name="mla_attn_t",
    )(qt, kt, lt, q3, kn3, kr3, vt3)


def _memkv_kernel(mem_ref, mnorm_ref, wmk_ref, wmv_ref, mkn_ref, k_ref, v_ref):
    mb = _rms(mem_ref[...], mnorm_ref[...]).astype(BF16)
    kf = _dot(mb, wmk_ref[...])
    for h in range(MEM_HEADS):
        sl = slice(h * MEM_HEAD_DIM, (h + 1) * MEM_HEAD_DIM)
        k_ref[:, sl] = _rms(kf[:, sl], mkn_ref[...])
    v_ref[...] = _dot(mb, wmv_ref[...])


def _memkv(mem2d, mnorm, w_mk, w_mv, mkn, tm):
    T = mem2d.shape[0]
    hw = MEM_HEADS * MEM_HEAD_DIM
    row = lambda n: pl.BlockSpec((tm, n), lambda i: (i, 0))
    return pl.pallas_call(
        _memkv_kernel,
        grid=(T // tm,),
        in_specs=[row(D_MODEL), _const_spec(mnorm.shape), _const_spec(w_mk.shape), _const_spec(w_mv.shape),
                  _const_spec(mkn.shape)],
        out_specs=[row(hw), row(hw)],
        out_shape=[jax.ShapeDtypeStruct((T, hw), F32), jax.ShapeDtypeStruct((T, hw), F32)],
        compiler_params=_cparams(("parallel",)),
        name="mem_kv",
    )(mem2d, mnorm, w_mk, w_mv, mkn)


def _mix_kernel(x_ref, og_ref, om_ref, mk_ref, mv_ref, wout_ref, nmem_ref, wmq_ref, mqn_ref, wmo_ref, nffn_ref,
                wrh_ref, wrl_ref, br_ref, x2_ref, h3_ref, idx_ref, gate_ref, counts_ref, cnt_ref):
    def mem_head(ref, b, h):
        if len(ref.shape) == 3:
            return ref[b, :, h * MEM_HEAD_DIM:(h + 1) * MEM_HEAD_DIM].astype(BF16)
        return ref[b, :, h, :].astype(BF16)

    first = (pl.program_id(0) == 0) & (pl.program_id(1) == 0)

    @pl.when(first)
    def _():
        cnt_ref[...] = jnp.zeros(cnt_ref.shape, F32)

    nbm, tm, _ = x_ref.shape
    rows = nbm * tm
    flat = lambda ref: ref[...].reshape(rows, ref.shape[-1])

    x1 = flat(x_ref) + _dot(flat(og_ref), wout_ref[0:GDN_VW, :]) + _dot(flat(om_ref), wout_ref[GDN_VW:, :])
    hb = _rms(x1, nmem_ref[...]).astype(BF16)
    qm = _dot(hb, wmq_ref[...])
    per_batch = []
    hsl = [slice(h * MEM_HEAD_DIM, (h + 1) * MEM_HEAD_DIM) for h in range(MEM_HEADS)]
    for b in range(nbm):
        br = slice(b * tm, (b + 1) * tm)
        qhs = [(_rms(qm[br, sl], mqn_ref[...]) * (MEM_HEAD_DIM ** -0.5)).astype(BF16) for sl in hsl]
        ss = [_dot_nt(qh, mem_head(mk_ref, b, h)) for h, qh in enumerate(qhs)]
        ps = []
        for s in ss:
            p = jnp.exp(s - jnp.max(s, axis=-1, keepdims=True))
            ps.append((p / jnp.sum(p, axis=-1, keepdims=True)).astype(BF16))
        heads = [_dot(p, mem_head(mv_ref, b, h)).astype(BF16) for h, p in enumerate(ps)]
        per_batch.append(jnp.concatenate(heads, axis=1))
    om = per_batch[0] if nbm == 1 else jnp.concatenate(per_batch, axis=0)
    x2 = x1 + _dot(om, wmo_ref[...])
    x2_ref[...] = x2.reshape(x2_ref.shape)
    h3 = _rms(x2, nffn_ref[...])
    hi = h3.astype(BF16)
    packed = _pack_bf16_pairs(hi.astype(F32))
    for c in range(h3_ref.shape[0]):
        h3_ref[c] = packed[:, c * LANES:(c + 1) * LANES]
    lo = (h3 - hi.astype(F32)).astype(BF16)
    wrh = wrh_ref[...]
    logits = _dot(hi, wrh) + _dot(lo, wrh) + _dot(hi, wrl_ref[...]) + br_ref[...]

    lane = lax.broadcasted_iota(jnp.int32, logits.shape, 1).astype(F32)
    vals, idxs = [], []
    cur = logits
    for _ in range(TOP_K):
        mx = jnp.max(cur, axis=-1, keepdims=True)
        ix = jnp.min(jnp.where(cur == mx, lane, float(LANES)), axis=-1, keepdims=True)
        vals.append(mx)
        idxs.append(ix)
        cur = jnp.where(lane == ix, -3e38, cur)
    es = [jnp.exp(v - vals[0]) for v in vals]
    den = es[0] + es[1] + es[2] + es[3]

    sel = jnp.zeros(logits.shape, F32)
    for k in range(TOP_K):
        sel = sel + jnp.where(lane == idxs[k], 1.0, 0.0)
    ri = lax.broadcasted_iota(jnp.int32, (rows, rows), 0)
    ci = lax.broadcasted_iota(jnp.int32, (rows, rows), 1)
    before = jnp.where(ri > ci, 1.0, 0.0).astype(BF16)
    excl = _dot(before, sel.astype(BF16)) + cnt_ref[...]
    cnt_ref[...] = cnt_ref[...] + jnp.sum(sel, axis=0, keepdims=True)
    counts_ref[...] = cnt_ref[...].astype(jnp.int32)

    idx_out = jnp.zeros(logits.shape, F32)
    gate_out = jnp.zeros(logits.shape, F32)
    for k in range(TOP_K):
        rank = jnp.sum(jnp.where(lane == idxs[k], excl, 0.0), axis=-1, keepdims=True)
        idx_out = jnp.where(lane == float(k), idxs[k], idx_out)
        idx_out = jnp.where(lane == float(TOP_K + k), rank, idx_out)
        gate_out = jnp.where(lane == float(k), es[k] / den, gate_out)
    idx_ref[...] = idx_out.T[:2 * TOP_K, :].astype(jnp.int32)
    gate_ref[...] = gate_out.reshape(gate_ref.shape)


def _mix(x3, og3, om3, mk3, mv3, pw, tm, b0, B):
    S = x3.shape[1]
    hw = MEM_HEADS * MEM_HEAD_DIM
    nbm = _pick(B, tuple(n for n in (8, 4, 2, 1) if n * tm <= MIX_ROWS and b0 % n == 0)) if tm == S else 1
    nsb = S // tm
    boff = b0 // nbm
    tile_in = lambda n: pl.BlockSpec((nbm, tm, n), lambda b, i: (b + boff, i, 0))
    tile = lambda n: pl.BlockSpec((nbm, tm, n), lambda b, i: (b, i, 0))
    if mk3.ndim == 3:
        memspec = pl.BlockSpec((nbm, N_MEM, hw), lambda b, i: (b + boff, 0, 0))
    else:
        memspec = pl.BlockSpec((nbm, N_MEM, MEM_HEADS, MEM_HEAD_DIM), lambda b, i: (b + boff, 0, 0, 0))
    consts = [pw['w_out'], pw['nmem'], pw['w_mq'], pw['mqn'], pw['w_mo'], pw['nffn'], pw['wr_hi'], pw['wr_lo'],
              pw['b_r']]
    return pl.pallas_call(
        _mix_kernel,
        grid=(B // nbm, nsb),
        in_specs=[tile_in(D_MODEL), tile_in(GDN_VW), tile_in(MLA_HEADS * V_DIM), memspec, memspec]
                 + [_const_spec(c.shape) for c in consts],
        out_specs=[tile(D_MODEL), pl.BlockSpec((PK_CHUNKS, nbm * tm, LANES), lambda b, i: (0, b * nsb + i, 0)),
                   pl.BlockSpec((2 * TOP_K, nbm * tm), lambda b, i: (0, b * nsb + i)),
                   tile(LANES), _const_spec((1, LANES))],
        out_shape=[jax.ShapeDtypeStruct((B, S, D_MODEL), F32),
                   jax.ShapeDtypeStruct((PK_CHUNKS, B * S, LANES), jnp.uint32),
                   jax.ShapeDtypeStruct((2 * TOP_K, B * S), jnp.int32), jax.ShapeDtypeStruct((B, S, LANES), F32),
                   jax.ShapeDtypeStruct((1, LANES), jnp.int32)],
        scratch_shapes=[pltpu.VMEM((1, LANES), F32)],
        compiler_params=_cparams(("arbitrary", "arbitrary")),
        name="mix_mem_router",
    )(x3, og3, om3, mk3, mv3, *consts)


def _expert_kernel(be_ref, nx_ref, nv_ref, nu_ref, rows_ref, wgu_hbm, bgu_ref, wd_hbm, bd_ref, y_ref,
                   wgus_ref, wds_ref, wgub_ref, wdb_ref, sem_ref):
    i = pl.program_id(0)
    used = i < nu_ref[0]
    new_expert = (i == 0) | (be_ref[i] != be_ref[jnp.maximum(i - 1, 0)])

    def weight_copies(e):
        return (pltpu.make_async_copy(wgu_hbm.at[e], wgus_ref, sem_ref.at[0]),
                pltpu.make_async_copy(wd_hbm.at[e], wds_ref, sem_ref.at[1]))

    @pl.when(i == 0)
    def _():
        for cp in weight_copies(be_ref[0]):
            cp.start()

    @pl.when(used & new_expert)
    def _():
        for cp in weight_copies(be_ref[i]):
            cp.wait()

        def cast(r, carry):
            rs = pl.ds(pl.multiple_of(r * LANES, LANES), LANES)
            wgub_ref[rs, :] = wgus_ref[rs, :].astype(BF16)
            wdb_ref[rs, :] = wds_ref[rs, :].astype(BF16)
            return carry
        lax.fori_loop(0, D_MODEL // LANES, cast, 0)

        @pl.when(nx_ref[i] >= 0)
        def _():
            for cp in weight_copies(nx_ref[i]):
                cp.start()

    def ffn(nrows):
        packed = jnp.concatenate([rows_ref[c, :nrows, :] for c in range(PK_CHUNKS)], axis=1)
        x = _unpack_bf16_pairs(packed).astype(BF16)
        acc = None
        for c in range(D_FF // FF_CHUNK):
            gs_ = slice(c * FF_CHUNK, (c + 1) * FF_CHUNK)
            us_ = slice(D_FF + c * FF_CHUNK, D_FF + (c + 1) * FF_CHUNK)
            gt = jnp.minimum(_dot(x, wgub_ref[:, gs_]) + bgu_ref[:, gs_], SWIGLU_LIMIT)
            up = jnp.clip(_dot(x, wgub_ref[:, us_]) + bgu_ref[:, us_], -SWIGLU_LIMIT, SWIGLU_LIMIT)
            act = gt * _sigmoid(SWIGLU_ALPHA * gt) * (up + 1.0)
            part = _dot(act.astype(BF16), wdb_ref[gs_, :])
            acc = part if acc is None else acc + part
        ypk = _pack_bf16_pairs((acc + bd_ref[...]).astype(BF16).astype(F32))
        for c in range(PK_CHUNKS):
            y_ref[c, :nrows, :] = ypk[:, c * LANES:(c + 1) * LANES]
        if nrows < MOE_ROWS:
            y_ref[:, nrows:, :] = jnp.zeros((PK_CHUNKS, MOE_ROWS - nrows, LANES), y_ref.dtype)

    half_full = nv_ref[i] <= MOE_ROWS // 2
    pl.when(used & jnp.logical_not(half_full))(functools.partial(ffn, MOE_ROWS))
    pl.when(used & half_full)(functools.partial(ffn, MOE_ROWS // 2))

    @pl.when(jnp.logical_not(used))
    def _():
        y_ref[...] = jnp.zeros(y_ref.shape, y_ref.dtype)


def _experts(block_e, next_e, block_rows, n_used, rows, w_gu, b_gu, w_down, b_down):
    n_rows = rows.shape[1]
    nb = n_rows // MOE_ROWS
    gs = pltpu.PrefetchScalarGridSpec(
        num_scalar_prefetch=4,
        grid=(nb,),
        in_specs=[pl.BlockSpec((PK_CHUNKS, MOE_ROWS, LANES), lambda i, be, nx, nv, nu: (0, i, 0)),
                  pl.BlockSpec(memory_space=pl.ANY),
                  pl.BlockSpec((None, 1, 2 * D_FF), lambda i, be, nx, nv, nu: (be[i], 0, 0)),
                  pl.BlockSpec(memory_space=pl.ANY),
                  pl.BlockSpec((None, 1, D_MODEL), lambda i, be, nx, nv, nu: (be[i], 0, 0))],
        out_specs=pl.BlockSpec((PK_CHUNKS, MOE_ROWS, LANES), lambda i, be, nx, nv, nu: (0, i, 0)),
        scratch_shapes=[pltpu.VMEM((D_MODEL, 2 * D_FF), F32), pltpu.VMEM((D_FF, D_MODEL), F32),
                        pltpu.VMEM((D_MODEL, 2 * D_FF), BF16), pltpu.VMEM((D_FF, D_MODEL), BF16),
                        pltpu.SemaphoreType.DMA((2,))],
    )
    return pl.pallas_call(
        _expert_kernel,
        grid_spec=gs,
        out_shape=jax.ShapeDtypeStruct((PK_CHUNKS, n_rows, LANES), jnp.uint32),
        compiler_params=_cparams(("arbitrary",)),
        name="moe_experts",
    )(block_e, next_e, block_rows, n_used, rows, w_gu, b_gu, w_down, b_down)


def _sc_mesh():
    return plsc.VectorSubcoreMesh(core_axis_name="core", subcore_axis_name="subcore")


def _sc_scatter_rows(x3s, pos_ts, n_rows):
    C, _, L = x3s[0].shape
    K = pos_ts[0].shape[0]
    ns = len(x3s)

    @functools.partial(pl.kernel, out_type=jax.ShapeDtypeStruct((C, n_rows, L), x3s[0].dtype), mesh=_sc_mesh(),
                       scratch_types=[])
    def scatter(*refs):
        o_hbm = refs[2 * ns]
        for s in range(ns):
            x_hbm, i_hbm = refs[s], refs[ns + s]
            nwin = x3s[s].shape[1] // SC_WINDOW
            for c in range(C):
                def body(x_vmem, i_vmem, c=c):
                    for k in range(K):
                        pltpu.sync_copy(x_vmem, o_hbm.at[c].at[i_vmem.at[k]])

                pltpu.emit_pipeline(
                    body, grid=(nwin,),
                    in_specs=[pl.BlockSpec((SC_WINDOW, L), lambda i, c=c, nwin=nwin: (c * nwin + i, 0)),
                              pl.BlockSpec((K, SC_WINDOW), lambda i: (0, i))],
                    out_specs=[], core_axis_name=("core", "subcore"), dimension_semantics=(pltpu.PARALLEL,),
                )(x_hbm, i_hbm)

    return scatter(*[x.reshape(-1, L) for x in x3s], *pos_ts)


def _sc_gather_rows(table3, idxs):
    C, _, L = table3.shape
    ns = len(idxs)
    out_type = [jax.ShapeDtypeStruct((C * i.shape[0], L), table3.dtype) for i in idxs]

    @functools.partial(pl.kernel, out_type=out_type, mesh=_sc_mesh(), scratch_types=[])
    def gather(t_hbm, *refs):
        for s in range(ns):
            i_hbm, o_hbm = refs[s], refs[ns + s]
            nwin = idxs[s].shape[0] // SC_WINDOW
            for c in range(C):
                def body(i_vmem, o_vmem, c=c):
                    pltpu.sync_copy(t_hbm.at[c].at[i_vmem.at[0]], o_vmem)

                pltpu.emit_pipeline(
                    body, grid=(nwin,),
                    in_specs=[pl.BlockSpec((1, SC_WINDOW), lambda i: (0, i))],
                    out_specs=[pl.BlockSpec((SC_WINDOW, L), lambda i, c=c, nwin=nwin: (c * nwin + i, 0))],
                    core_axis_name=("core", "subcore"), dimension_semantics=(pltpu.PARALLEL,),
                )(i_hbm, o_hbm)

    outs = gather(table3, *[i.reshape(1, -1) for i in idxs])
    return [o.reshape(C, -1, L) for o in outs]


def _combine_kernel(x2_ref, g_ref, gate_ref, *rest):
    o_ref = rest[-1]
    gate = gate_ref[...]
    half = D_MODEL // 2
    for c in range(PK_CHUNKS):
        lo_s = slice(c * LANES, (c + 1) * LANES)
        hi_s = slice(half + c * LANES, half + (c + 1) * LANES)
        acc_lo = x2_ref[:, lo_s]
        acc_hi = x2_ref[:, hi_s]
        for k in range(TOP_K):
            w = g_ref[c, k]
            gk = gate[:, k:k + 1]
            acc_lo = acc_lo + pltpu.bitcast(w << 16, F32) * gk
            acc_hi = acc_hi + pltpu.bitcast(w & jnp.uint32(0xFFFF0000), F32) * gk
        o_ref[:, lo_s] = acc_lo
        o_ref[:, hi_s] = acc_hi


def _combine(x2, g4, gate, tm, out_buf=None, row0=0, t_total=None):
    T = x2.shape[0]
    t_total = T if t_total is None else t_total
    blk0 = row0 // tm
    in_specs = [pl.BlockSpec((tm, D_MODEL), lambda i: (i, 0)),
                pl.BlockSpec((PK_CHUNKS, TOP_K, tm, LANES), lambda i: (0, 0, i, 0)),
                pl.BlockSpec((tm, LANES), lambda i: (i, 0))]
    args = [x2, g4, gate]
    aliases = {}
    if out_buf is not None:
        in_specs.append(pl.BlockSpec(memory_space=pl.ANY))
        args.append(out_buf)
        aliases = {3: 0}
    return pl.pallas_call(
        _combine_kernel,
        grid=(T // tm,),
        in_specs=in_specs,
        out_specs=pl.BlockSpec((tm, D_MODEL), lambda i: (i + blk0, 0)),
        out_shape=jax.ShapeDtypeStruct((t_total, D_MODEL), F32),
        input_output_aliases=aliases,
        compiler_params=_cparams(("parallel",)),
        name="moe_combine",
    )(*args)


def _plan_kernel(idxr_ref, base_ref, pos_ref):
    rows = 2 * TOP_K
    base = jnp.broadcast_to(base_ref[...], (rows, LANES))
    is_id = lax.broadcasted_iota(jnp.int32, (rows, LANES), 0) < TOP_K
    for j in range(idxr_ref.shape[1] // LANES):
        cols = slice(j * LANES, (j + 1) * LANES)
        blk = idxr_ref[:, cols]
        start = jnp.take_along_axis(base, jnp.where(is_id, blk, 0), axis=1)
        pos_ref[:, cols] = (start + pltpu.roll(blk, TOP_K, 0))[:TOP_K, :]


def _plan(idxr, base):
    T = idxr.shape[1]
    tm = _pick(T, (2048, 1024, 512, 256, 128))
    return pl.pallas_call(
        _plan_kernel,
        grid=(T // tm,),
        in_specs=[pl.BlockSpec((2 * TOP_K, tm), lambda i: (0, i)), _const_spec((1, LANES))],
        out_specs=pl.BlockSpec((TOP_K, tm), lambda i: (0, i)),
        out_shape=jax.ShapeDtypeStruct((TOP_K, T), jnp.int32),
        compiler_params=_cparams(("parallel",)),
        name="moe_plan",
    )(idxr, base)


def _moe(streams, places, ew):
    cnts = [st[3][0, :N_EXPERTS] for st in streams]
    total = sum(cnts)
    padded = (total + MOE_ROWS - 1) // MOE_ROWS * MOE_ROWS
    pad_end = jnp.cumsum(padded)
    pad_start = pad_end - padded
    pos_ts = []
    base = pad_start
    for (h3p, idxr, gate, counts, x2), cnt in zip(streams, cnts):
        pos_ts.append(_plan(idxr, _pad_lanes(base)))
        base = base + cnt
    n_assign = sum(st[0].shape[1] for st in streams) * TOP_K
    nb = -(-n_assign // MOE_ROWS) + N_EXPERTS
    starts = jnp.arange(nb, dtype=jnp.int32) * MOE_ROWS
    block_e = jnp.minimum(jnp.sum((pad_end[None, :] <= starts[:, None]).astype(jnp.int32), axis=1), N_EXPERTS - 1)
    n_used = (pad_end[-1] // MOE_ROWS).astype(jnp.int32).reshape(1)
    ar = jnp.arange(N_EXPERTS, dtype=jnp.int32)
    later = (padded > 0)[None, :] & (ar[None, :] > ar[:, None])
    nxt = jnp.min(jnp.where(later, ar[None, :], N_EXPERTS), axis=1)
    nxt = jnp.where(nxt >= N_EXPERTS, -1, nxt)
    mine = block_e[:, None] == ar[None, :]
    next_e = jnp.sum(jnp.where(mine, nxt[None, :], 0), axis=1).astype(jnp.int32)
    seg_end = jnp.sum(jnp.where(mine, (pad_start + total)[None, :], 0), axis=1)
    block_rows = jnp.clip(seg_end - starts, 0, MOE_ROWS).astype(jnp.int32)
    rows = _sc_scatter_rows([st[0] for st in streams], pos_ts, nb * MOE_ROWS)
    y_rows = _experts(block_e, next_e, block_rows, n_used, rows, ew['w_gu'], ew['b_gu'], ew['w_down'], ew['b_down'])
    gs = _sc_gather_rows(y_rows, [p.reshape(-1) for p in pos_ts])
    outs = []
    for (h3p, idxr, gate, counts, x2), g, place in zip(streams, gs, places):
        T = x2.shape[0]
        outs.append(_combine(x2, g.reshape(PK_CHUNKS, TOP_K, T, LANES), gate, _pick(T, (512, 256, 128, 64)), *place))
    return outs


def _pad_lanes(v, n=LANES, fill=0.0):
    return jnp.pad(v, (0, n - v.shape[0]), constant_values=fill).reshape(1, n)


def _prep_weights(norm_mix, w_in, conv_w, a_log, dt_bias, gdn_norm, q_a_norm, w_qb, kv_a_norm, w_kvb, q_norm,
                  k_nope_norm, k_rope_norm, w_out, norm_mem, mem_norm, w_mq, w_mk, w_mv, mq_norm, mk_norm, w_mo,
                  norm_ffn, w_router, b_router, w_gu, b_gu, w_down, b_down):
    c = np.cumsum([CONV_DIM, GDN_VW, GDN_HEADS, GDN_HEADS, Q_RANK, KV_RANK])
    w_u, w_z, w_a, w_b, w_cq, w_ckv, w_kpe = [w_in[:, lo:hi] for lo, hi in
                                              zip([0, *c], [*c, w_in.shape[1]])]
    w_s = jnp.concatenate([w_kpe, w_a, w_b], axis=1)
    w_s = jnp.pad(w_s, ((0, 0), (0, LANES - w_s.shape[1])))
    wq = w_qb.reshape(Q_RANK, MLA_HEADS, QK_DIM)
    wq = jnp.pad(wq, ((0, 0), (0, 0), (0, Q_SLAB - QK_DIM))).reshape(Q_RANK, MLA_HEADS * Q_SLAB)
    wkv = w_kvb.reshape(KV_RANK, MLA_HEADS, NOPE_DIM + V_DIM)
    w_k = wkv[:, :, :NOPE_DIM].reshape(KV_RANK, -1).astype(BF16)
    w_v = wkv[:, :, NOPE_DIM:].reshape(KV_RANK, -1).astype(BF16)
    wr = jnp.pad(w_router, ((0, 0), (0, LANES - N_EXPERTS)))
    wr_hi = wr.astype(BF16)
    wr_lo = (wr - wr_hi.astype(F32)).astype(BF16)
    row = lambda v: v.reshape(1, -1)
    gpad = ROPE_DIM
    pw = dict(
        nmix=row(norm_mix), w_u=w_u.astype(BF16), w_z=w_z.astype(BF16), w_cq=w_cq.astype(BF16),
        w_ckv=w_ckv.astype(BF16), w_s=w_s.astype(BF16), qan=row(q_a_norm), w_qb=wq.astype(BF16),
        qn=_pad_lanes(q_norm, Q_SLAB), kvan=row(kv_a_norm), krn=_pad_lanes(k_rope_norm),
        alog=jnp.pad(a_log, (gpad, LANES - gpad - GDN_HEADS)).reshape(1, LANES),
        dtb=jnp.pad(dt_bias, (gpad, LANES - gpad - GDN_HEADS)).reshape(1, LANES),
        conv_w=conv_w, gnorm=row(gdn_norm), w_k=w_k, w_v=w_v, w_vt=w_v.T, knn=row(k_nope_norm),
        w_out=w_out.astype(BF16), nmem=row(norm_mem), w_mq=w_mq.astype(BF16), mqn=row(mq_norm),
        w_mo=w_mo.astype(BF16), nffn=row(norm_ffn), wr_hi=wr_hi, wr_lo=wr_lo,
        b_r=_pad_lanes(b_router, LANES, NEG_BIG),
        mnorm=row(mem_norm), w_mk=w_mk.astype(BF16), w_mv=w_mv.astype(BF16), mkn=row(mk_norm),
    )
    ew = dict(w_gu=w_gu, b_gu=b_gu.reshape(N_EXPERTS, 1, 2 * D_FF), w_down=w_down,
              b_down=b_down.reshape(N_EXPERTS, 1, D_MODEL))
    return pw, ew


def _rope_tables(P, S):
    half = ROPE_DIM // 2
    inv = ROPE_THETA ** (-jnp.arange(half, dtype=F32) / half)
    ang = (P + jnp.arange(S, dtype=jnp.int32)).astype(F32)[:, None] * inv[None, :]
    cos, sin = jnp.cos(ang), jnp.sin(ang)
    zh = jnp.zeros((S, half), F32)
    zz = jnp.zeros((S, LANES - ROPE_DIM), F32)
    return (jnp.concatenate([cos, cos, zz], 1), jnp.concatenate([-sin, zh, zz], 1),
            jnp.concatenate([zh, sin, zz], 1))


def _pick(n, prefs):
    for t in prefs:
        if n % t == 0:
            return t
    return n


def _trunk_front(x, lat_past, kpe_past, s0, conv_past, mem_k, mem_v, pw, n_groups):
    B, S, D = x.shape
    P = lat_past.shape[1]
    T = B * S
    tm = _pick(S, (512, 256, 128, 64))
    tm_in = 512 if (T % 512 == 0 and (512 % S == 0 or S % 512 == 0)) else tm
    u, z, q, lat_new, small = _inproj(x.reshape(T, D), S, tm_in, pw, _rope_tables(P, S))

    LC = _pick(S, (256, 128, 64))
    NB = _pick(B, tuple(n for n in (8, 4, 2, 1) if n * (LC // CHUNK) <= GDN_UNITS))
    o_gdn, s_new, conv_new = _gdn(u.reshape(B, S, CONV_DIM), small.reshape(B, S, LANES), z.reshape(B, S, GDN_VW),
                                  pw['conv_w'], conv_past, s0, pw['gnorm'], NB, LC)

    kpe_new = small[:, :ROPE_DIM].reshape(B, S, ROPE_DIM)
    lat3 = lat_new.reshape(B, S, KV_RANK)
    small3 = small.reshape(B, S, LANES)
    q3 = q.reshape(B, S, MLA_HEADS * Q_SLAB)
    tq = _pick(S, (512, 256, 128, 64))
    if P == 0 and S % 512 == 0:
        kn, kr, vt = _kvproj(lat3, small3, pw, 512, True)
        o_mla = _flash_t(q3, kn, kr, vt, P, S, tq, 512)
    elif P > 0 and P % CHUNK == 0 and S == CHUNK:
        new = _kvproj(lat3, small3, pw, S, False)
        o_mla = _attn_dec(q3, lat_past, kpe_past, new, pw)
    else:
        sk = P + S
        tk = 512 if S >= 512 else -(-sk // LANES) * LANES
        skp = -(-sk // tk) * tk
        lat_all = jnp.pad(jnp.concatenate([lat_past, lat3], axis=1), ((0, 0), (0, skp - sk), (0, 0)))
        kpe_all = jnp.pad(jnp.concatenate([kpe_past, kpe_new], axis=1), ((0, 0), (0, skp - sk), (0, 0)))
        key_major = tq >= LANES
        kn, kr, v = _kvproj(lat_all, kpe_all, pw, _pick(skp, (512,)), key_major)
        o_mla = (_flash_t if key_major else _flash)(q3, kn, kr, v, P, S, tq, tk)

    streams = []
    gb = B // n_groups
    for gi in range(n_groups):
        x2, h3p, idxr, gate, counts = _mix(x, o_gdn, o_mla, mem_k, mem_v, pw, tm, gi * gb, gb)
        tg = gb * S
        streams.append((h3p, idxr, gate.reshape(tg, LANES), counts, x2.reshape(tg, D)))
    return streams, (lat_new.reshape(B, S, KV_RANK), kpe_new, s_new, conv_new)


def kernel(x_prompt, x_sample, cache_kv_latent, cache_k_rope, state_gdn, state_conv, cache_mem_k, cache_mem_v, mem_prompt, norm_mix, w_in, conv_w, a_log, dt_bias, gdn_norm, q_a_norm, w_qb, kv_a_norm, w_kvb, q_norm, k_nope_norm, k_rope_norm, w_out, norm_mem, mem_norm, w_mq, w_mk, w_mv, mq_norm, mk_norm, w_mo, norm_ffn, w_router, b_router, w_gu, b_gu, w_down, b_down):
    depth = norm_mix.shape[0]
    yp, ys = x_prompt, x_sample
    bp = x_prompt.shape[0]
    hw = MEM_HEADS * MEM_HEAD_DIM
    outs = [[] for _ in range(10)]
    for l in range(depth):
        pw, ew = _prep_weights(norm_mix[l], w_in[l], conv_w[l], a_log[l], dt_bias[l], gdn_norm[l], q_a_norm[l],
                               w_qb[l], kv_a_norm[l], w_kvb[l], q_norm[l], k_nope_norm[l], k_rope_norm[l], w_out[l],
                               norm_mem[l], mem_norm[l], w_mq[l], w_mk[l], w_mv[l], mq_norm[l], mk_norm[l], w_mo[l],
                               norm_ffn[l], w_router[l], b_router[l], w_gu[l], b_gu[l], w_down[l], b_down[l])
        nm = mem_prompt.shape[1]
        mk, mv = _memkv(mem_prompt.reshape(bp * nm, D_MODEL), pw['mnorm'], pw['w_mk'], pw['w_mv'], pw['mkn'],
                        _pick(bp * nm, (512, 256)))
        mk = mk.reshape(bp, nm, hw)
        mv = mv.reshape(bp, nm, hw)
        n_groups = 2 if bp % 2 == 0 else 1
        streams_p, (lat, kpe, s_fin, cv) = _trunk_front(
            yp, jnp.zeros((bp, 0, KV_RANK), F32), jnp.zeros((bp, 0, ROPE_DIM), F32),
            jnp.zeros((bp, GDN_HEADS, GDN_DK, GDN_DV), F32), jnp.zeros((bp, CONV_W - 1, CONV_DIM), F32), mk, mv, pw,
            n_groups)
        bs = x_sample.shape[0]
        (stream_s,), (lat2, kpe2, s_fin2, cv2) = _trunk_front(
            ys, cache_kv_latent[l], cache_k_rope[l], state_gdn[l], state_conv[l],
            cache_mem_k[l], cache_mem_v[l], pw, 1)
        tp = yp.shape[0] * yp.shape[1]
        tg = tp // n_groups
        ybuf = None
        for gi, st in enumerate(streams_p):
            last = gi == n_groups - 1
            res = _moe([st] + ([stream_s] if last else []),
                       [(ybuf, gi * tg, tp)] + ([(None, 0, None)] if last else []), ew)
            ybuf = res[0]
        yp, ys = ybuf.reshape(yp.shape), res[1].reshape(ys.shape)
        for lst, val in zip(outs, (lat, kpe, s_fin, cv, mk.reshape(bp, nm, MEM_HEADS, MEM_HEAD_DIM),
                                   mv.reshape(bp, nm, MEM_HEADS, MEM_HEAD_DIM), lat2, kpe2, s_fin2, cv2)):
            lst.append(val)
    return (yp, ys) + tuple(jnp.stack(o) for o in outs)
```

```python
import functools
import math

import numpy as np
import jax
import jax.numpy as jnp
from jax import lax
from jax.experimental import pallas as pl
from jax.experimental.pallas import tpu as pltpu
from jax.experimental.pallas import tpu_sc as plsc

F32 = jnp.float32
BF16 = jnp.bfloat16

D_MODEL = 1024
CHUNK = 64
EPS = 1e-6
GDN_HEADS = 4
GDN_DK = 128
GDN_DV = 128
CONV_W = 4
GDN_QK = GDN_HEADS * GDN_DK
GDN_VW = GDN_HEADS * GDN_DV
CONV_DIM = 2 * GDN_QK + GDN_VW
MLA_HEADS = 4
Q_RANK = 384
KV_RANK = 256
NOPE_DIM = 128
ROPE_DIM = 64
V_DIM = 128
QK_DIM = NOPE_DIM + ROPE_DIM
ROPE_THETA = 10000.0
N_MEM = 256
MEM_HEADS = 4
MEM_HEAD_DIM = 128
N_EXPERTS = 32
TOP_K = 4
D_FF = D_MODEL
SWIGLU_ALPHA = 1.702
SWIGLU_LIMIT = 7.0

LANES = 128
Q_SLAB = 2 * LANES
NEG_BIG = -1e30
VMEM_LIMIT = 56 * 1024 * 1024
MOE_ROWS = 512
GDN_UNITS = 8
GDN_GROUP = 2
FF_CHUNK = 512
SC_WINDOW = 128
PK_CHUNKS = D_MODEL // 2 // LANES
SUBTILE_ROWS = 256
ONES_ROWS = 16
MIX_ROWS = 512


def _cparams(sem):
    return pltpu.CompilerParams(dimension_semantics=sem, vmem_limit_bytes=VMEM_LIMIT)


def _dot(a, b):
    return jnp.dot(a, b, preferred_element_type=F32)


def _dot_nt(a, b):
    return lax.dot_general(a, b, (((1,), (1,)), ((), ())), preferred_element_type=F32)


def _dot_tn(a, b):
    return lax.dot_general(a, b, (((0,), (0,)), ((), ())), preferred_element_type=F32)


def _rms(x, gain, n=None):
    n = x.shape[-1] if n is None else n
    ss = jnp.sum(x * x, axis=-1, keepdims=True) * (1.0 / n)
    return (x * lax.rsqrt(ss + EPS)) * gain


def _sigmoid(x):
    return 1.0 / (1.0 + jnp.exp(-x))


def _rope128(r, cos, sna, snb):
    return r * cos + pltpu.roll(r, 96, 1) * sna + pltpu.roll(r, 32, 1) * snb


def _pack_bf16_pairs(x):
    n = x.shape[1] // 2
    lo = pltpu.bitcast(x[:, :n], jnp.uint32) >> 16
    hi = pltpu.bitcast(x[:, n:], jnp.uint32) & jnp.uint32(0xFFFF0000)
    return lo | hi


def _unpack_bf16_pairs(p):
    lo = pltpu.bitcast(p << 16, F32)
    hi = pltpu.bitcast(p & jnp.uint32(0xFFFF0000), F32)
    return jnp.concatenate([lo, hi], axis=1)


def _subtiles(rows):
    n = rows // SUBTILE_ROWS if rows % SUBTILE_ROWS == 0 else 1
    step = rows // n
    return [slice(i * step, (i + 1) * step) for i in range(n)]


def _const_spec(shape):
    nd = len(shape)
    return pl.BlockSpec(shape, lambda *_: (0,) * nd)


def _inproj_kernel(x_ref, nmix_ref, wu_ref, wz_ref, wcq_ref, wckv_ref, ws_ref, qan_ref, wqb_ref, qn_ref,
                   kvan_ref, krn_ref, alog_ref, dtb_ref, cos_ref, sna_ref, snb_ref,
                   u_ref, z_ref, q_ref, lat_ref, small_ref):
    for rs in _subtiles(x_ref.shape[0]):
        x = x_ref[rs, :]
        hb = _rms(x, nmix_ref[...]).astype(BF16)
        u_ref[rs, :] = _dot(hb, wu_ref[...])
        z_ref[rs, :] = _dot(hb, wz_ref[...])
        cq_raw = _dot(hb, wcq_ref[...])
        ckv_raw = _dot(hb, wckv_ref[...])
        sm = _dot(hb, ws_ref[...])
        cos, sna, snb = cos_ref[rs, :], sna_ref[rs, :], snb_ref[rs, :]

        cq = _rms(cq_raw, qan_ref[...]).astype(BF16)
        qf = _dot(cq, wqb_ref[...])
        scale = QK_DIM ** -0.5
        for h in range(MLA_HEADS):
            slab = qf[:, h * Q_SLAB:(h + 1) * Q_SLAB]
            slab = _rms(slab, qn_ref[...], n=QK_DIM)
            nope = slab[:, :LANES]
            ropd = _rope128(slab[:, LANES:], cos, sna, snb)
            q_ref[rs, h * Q_SLAB:h * Q_SLAB + LANES] = (nope * scale).astype(BF16)
            q_ref[rs, h * Q_SLAB + LANES:(h + 1) * Q_SLAB] = (ropd * scale).astype(BF16)

        lat_ref[rs, :] = _rms(ckv_raw, kvan_ref[...])

        lane = lax.broadcasted_iota(jnp.int32, sm.shape, 1)
        kp = jnp.where(lane < ROPE_DIM, sm, 0.0)
        kpe = _rope128(_rms(kp, krn_ref[...], n=ROPE_DIM), cos, sna, snb)
        sp = sm + dtb_ref[...]
        softplus = jnp.maximum(sp, 0.0) + jnp.log1p(jnp.exp(-jnp.abs(sp)))
        g = -jnp.exp(alog_ref[...]) * softplus
        beta = _sigmoid(sm)
        small_ref[rs, :] = jnp.where(lane < ROPE_DIM, kpe,
                                     jnp.where(lane < ROPE_DIM + GDN_HEADS, g,
                                               jnp.where(lane < ROPE_DIM + 2 * GDN_HEADS, beta, 0.0)))


def _inproj(x2d, S, tm, pw, tabs):
    T = x2d.shape[0]
    if tm > S:
        tabs = [jnp.tile(t, (tm // S, 1)) for t in tabs]
    nblk_s = max(S // tm, 1)
    row = lambda n: pl.BlockSpec((tm, n), lambda i: (i, 0))
    tab = pl.BlockSpec((tm, LANES), lambda i: (i % nblk_s, 0))
    consts = [pw['nmix'], pw['w_u'], pw['w_z'], pw['w_cq'], pw['w_ckv'], pw['w_s'], pw['qan'], pw['w_qb'],
              pw['qn'], pw['kvan'], pw['krn'], pw['alog'], pw['dtb']]
    return pl.pallas_call(
        _inproj_kernel,
        grid=(T // tm,),
        in_specs=[row(D_MODEL)] + [_const_spec(c.shape) for c in consts] + [tab, tab, tab],
        out_specs=[row(CONV_DIM), row(GDN_VW), row(MLA_HEADS * Q_SLAB), row(KV_RANK), row(LANES)],
        out_shape=[jax.ShapeDtypeStruct((T, CONV_DIM), F32), jax.ShapeDtypeStruct((T, GDN_VW), F32),
                   jax.ShapeDtypeStruct((T, MLA_HEADS * Q_SLAB), BF16), jax.ShapeDtypeStruct((T, KV_RANK), F32),
                   jax.ShapeDtypeStruct((T, LANES), F32)],
        compiler_params=_cparams(("parallel",)),
        name="inproj",
    )(x2d, *consts, *tabs)


def _split3(x):
    hi = x.astype(BF16)
    r = x - hi.astype(F32)
    mid = r.astype(BF16)
    lo = (r - mid.astype(F32)).astype(BF16)
    return hi, mid, lo


def _gdn_kernel(u_ref, small_ref, z_ref, convw_ref, cpast_ref, s0_ref, gnorm_ref,
                to_ref, trilm_ref, strictm_ref, same2_ref, lvl_ref,
                o_ref, sfin_ref, cnew_ref, ext_ref, uc_ref, state_ref,
                qf_ref, kf_ref, vb_ref, bt_ref, gcum_ref, glast_ref, kk_ref, qk_ref, mb_ref, x_ref, qkb_ref,
                kbe_ref, qg_ref, kdec_ref, egl_ref, t1_ref, uu_ref, ww_ref, vn_ref, qs_ref, *, NB, LC):
    j = pl.program_id(1)
    nj = pl.num_programs(1)
    PADR = 8
    C = LC // CHUNK
    NG = GDN_HEADS // GDN_GROUP
    U = NB * C * NG
    HR = GDN_GROUP * CHUNK

    def unit(nb, c, g):
        return (nb * C + c) * NG + g

    def head_rows(h):
        return slice((h % GDN_GROUP) * CHUNK, (h % GDN_GROUP + 1) * CHUNK)

    @pl.when(j == 0)
    def _():
        state_ref[...] = s0_ref[...]
        ext_ref[:, PADR - (CONV_W - 1):PADR, :] = cpast_ref[...]

    w = convw_ref[...]
    for nb in range(NB):
        ext_ref[nb, PADR:PADR + LC, :] = u_ref[nb]
        acc = ext_ref[nb, PADR:PADR + LC, :] * w[CONV_W - 1:CONV_W, :]
        for t in range(1, CONV_W):
            acc = acc + ext_ref[nb, PADR - t:PADR - t + LC, :] * w[CONV_W - 1 - t:CONV_W - t, :]
        uc_ref[nb] = acc * _sigmoid(acc)
        ext_ref[nb, 0:PADR, :] = ext_ref[nb, LC:LC + PADR, :]

    @pl.when(j == nj - 1)
    def _():
        cnew_ref[...] = ext_ref[:, PADR - (CONV_W - 1):PADR, :]

    g0 = ROPE_DIM
    b0 = ROPE_DIM + GDN_HEADS
    to = to_ref[...]
    for nb, c in [(nb, c) for nb in range(NB) for c in range(C)]:
        rows = slice(c * CHUNK, (c + 1) * CHUNK)
        sm = small_ref[nb, rows, :]
        gl = sum(_dot(to, part) for part in _split3(sm))
        for h in range(GDN_HEADS):
            u = unit(nb, c, h // GDN_GROUP)
            hr = head_rows(h)
            q = uc_ref[nb, rows, h * GDN_DK:(h + 1) * GDN_DK]
            k = uc_ref[nb, rows, GDN_QK + h * GDN_DK:GDN_QK + (h + 1) * GDN_DK]
            v = uc_ref[nb, rows, 2 * GDN_QK + h * GDN_DV:2 * GDN_QK + (h + 1) * GDN_DV]
            beta = jnp.broadcast_to(sm[:, b0 + h:b0 + h + 1], (CHUNK, LANES))
            qf_ref[u, hr, :] = (q * lax.rsqrt(jnp.sum(q * q, -1, keepdims=True) + EPS)) * (GDN_DK ** -0.5)
            kf_ref[u, hr, :] = k * lax.rsqrt(jnp.sum(k * k, -1, keepdims=True) + EPS)
            vb_ref[u, hr, :] = (v * beta).astype(BF16)
            bt_ref[u, hr, :] = beta
            gcum_ref[u, hr, :] = jnp.broadcast_to(gl[:CHUNK, g0 + h:g0 + h + 1], (CHUNK, LANES))
            glast_ref[u, hr, :] = jnp.broadcast_to(gl[CHUNK:, g0 + h:g0 + h + 1], (CHUNK, LANES))

    for u in range(U):
        k = kf_ref[u]
        kbf = k.astype(BF16)
        kk_ref[u] = _dot_nt((k * bt_ref[u]).astype(BF16), kbf)
        qk_ref[u] = _dot_nt(qf_ref[u].astype(BF16), kbf)

    trilm = trilm_ref[...]
    eye = trilm - strictm_ref[...]
    for u in range(U):
        gcum = gcum_ref[u]
        grow = gcum.T[0:1, :]
        gcol = gcum if HR == LANES else jnp.concatenate([gcum] * (HR // LANES), axis=1)
        decay = jnp.exp(jnp.minimum(gcol - grow, 0.0)) * trilm
        m = kk_ref[u] * (decay * strictm_ref[...])
        mb_ref[u] = m.astype(BF16)
        x_ref[u] = eye - m * same2_ref[...]
        qkb_ref[u] = (qk_ref[u] * decay).astype(BF16)
        egc = jnp.exp(gcum)
        k = kf_ref[u]
        kbe_ref[u] = (k * bt_ref[u] * egc).astype(BF16)
        qg_ref[u] = (qf_ref[u] * egc).astype(BF16)
        kdec_ref[u] = (k * jnp.exp(glast_ref[u] - gcum)).astype(BF16)
        egl_ref[u] = jnp.exp(glast_ref[u])

    for lvl in range(lvl_ref.shape[0]):
        lm = lvl_ref[lvl]
        for u in range(U):
            t1_ref[u] = _dot(mb_ref[u] * lm, x_ref[u].astype(BF16)).astype(BF16)
        for u in range(U):
            x = x_ref[u]
            x_ref[u] = x - _dot(x.astype(BF16), t1_ref[u])

    for u in range(U):
        xb = x_ref[u].astype(BF16)
        uu_ref[u] = _dot(xb, vb_ref[u])
        ww_ref[u] = _dot(xb, kbe_ref[u]).astype(BF16)

    gnorm = gnorm_ref[...]
    for c in range(C):
        rows = slice(c * CHUNK, (c + 1) * CHUNK)
        heads = [(nb, h, unit(nb, c, h // GDN_GROUP), head_rows(h)) for nb in range(NB) for h in range(GDN_HEADS)]
        groups = [(nb, g, unit(nb, c, g)) for nb in range(NB) for g in range(NG)]
        for nb, h, u, hr in heads:
            stb = state_ref[nb, h].astype(BF16)
            r = _dot(jnp.concatenate([ww_ref[u, hr, :], qg_ref[u, hr, :]], axis=0), stb)
            vn_ref[u, hr, :] = (uu_ref[u, hr, :] - r[:CHUNK]).astype(BF16)
            qs_ref[u, hr, :] = r[CHUNK:]
        outs = {(nb, g): qs_ref[u] + _dot(qkb_ref[u], vn_ref[u]) for nb, g, u in groups}
        for nb, h, u, hr in heads:
            state_ref[nb, h] = (state_ref[nb, h] * egl_ref[u, hr.start:hr.start + 1, :]
                                + _dot_tn(kdec_ref[u, hr, :], vn_ref[u, hr, :]))
        for nb, h, u, hr in heads:
            zz = z_ref[nb, rows, h * GDN_DV:(h + 1) * GDN_DV]
            og = _rms(outs[(nb, h // GDN_GROUP)][hr, :], gnorm) * (zz * _sigmoid(zz))
            o_ref[nb, rows, h * GDN_DV:(h + 1) * GDN_DV] = og.astype(BF16)

    @pl.when(j == nj - 1)
    def _():
        sfin_ref[...] = state_ref[...]


def _gdn_masks():
    hr = GDN_GROUP * CHUNK
    i = np.arange(hr)[:, None]
    j = np.arange(hr)[None, :]
    same_head = (i // CHUNK) == (j // CHUNK)
    tril = same_head & (i >= j)
    strict = same_head & (i > j)
    same2 = strict & ((i // 2) == (j // 2))
    lvls = []
    blk = 2
    while blk < CHUNK:
        lvls.append(strict & ((i // (2 * blk)) == (j // (2 * blk))) & ((i // blk) != (j // blk)))
        blk *= 2
    fr = np.arange(CHUNK)
    to = np.concatenate([fr[:, None] >= fr[None, :], np.ones((CHUNK, CHUNK), bool)], axis=0)
    f = lambda a: jnp.asarray(a.astype(np.float32))
    return (jnp.asarray(to.astype(np.float32), dtype=BF16), f(tril), f(strict), f(same2),
            jnp.asarray(np.stack(lvls).astype(np.float32), dtype=BF16))


def _gdn(u3, small3, z3, conv_w, conv_past, s0, gnorm, NB, LC):
    B, S, _ = u3.shape
    C = LC // CHUNK
    U = NB * C * (GDN_HEADS // GDN_GROUP)
    HR = GDN_GROUP * CHUNK
    masks = _gdn_masks()
    tile = lambda n: pl.BlockSpec((NB, LC, n), lambda b, j: (b, j, 0))
    stspec = pl.BlockSpec((NB, GDN_HEADS, GDN_DK, GDN_DV), lambda b, j: (b, 0, 0, 0))
    cvspec = pl.BlockSpec((NB, CONV_W - 1, CONV_DIM), lambda b, j: (b, 0, 0))
    vm = lambda shape, dt: pltpu.VMEM(shape, dt)
    return pl.pallas_call(
        functools.partial(_gdn_kernel, NB=NB, LC=LC),
        grid=(B // NB, S // LC),
        in_specs=[tile(CONV_DIM), tile(LANES), tile(GDN_VW), _const_spec(conv_w.shape), cvspec, stspec,
                  _const_spec(gnorm.shape)] + [_const_spec(m.shape) for m in masks],
        out_specs=[tile(GDN_VW), stspec, cvspec],
        out_shape=[jax.ShapeDtypeStruct((B, S, GDN_VW), BF16),
                   jax.ShapeDtypeStruct((B, GDN_HEADS, GDN_DK, GDN_DV), F32),
                   jax.ShapeDtypeStruct((B, CONV_W - 1, CONV_DIM), F32)],
        scratch_shapes=[vm((NB, LC + 8, CONV_DIM), F32), vm((NB, LC, CONV_DIM), F32),
                        vm((NB, GDN_HEADS, GDN_DK, GDN_DV), F32),
                        vm((U, HR, LANES), F32), vm((U, HR, LANES), F32), vm((U, HR, LANES), BF16),
                        vm((U, HR, LANES), F32),
                        vm((U, HR, LANES), F32), vm((U, HR, LANES), F32),
                        vm((U, HR, HR), F32), vm((U, HR, HR), F32),
                        vm((U, HR, HR), BF16), vm((U, HR, HR), F32), vm((U, HR, HR), BF16),
                        vm((U, HR, LANES), BF16), vm((U, HR, LANES), BF16), vm((U, HR, LANES), BF16),
                        vm((U, HR, LANES), F32), vm((U, HR, HR), BF16),
                        vm((U, HR, LANES), F32), vm((U, HR, LANES), BF16),
                        vm((U, HR, LANES), BF16), vm((U, HR, LANES), F32)],
        compiler_params=_cparams(("parallel", "arbitrary")),
        name="gdn",
    )(u3, small3, z3, conv_w, conv_past, s0, gnorm, *masks)


def _kvproj_kernel(lat_ref, kpe_ref, wk_ref, wv_ref, knn_ref, kn_ref, kr_ref, v_ref, *, v_transposed):
    lb = lat_ref[...].astype(BF16)
    kf = _dot(lb, wk_ref[...])
    for h in range(MLA_HEADS):
        sl = slice(h * NOPE_DIM, (h + 1) * NOPE_DIM)
        kn_ref[:, sl] = _rms(kf[:, sl], knn_ref[...]).astype(BF16)
    if v_transposed:
        v_ref[...] = _dot_nt(wv_ref[...], lb).astype(BF16)
    else:
        v_ref[...] = _dot(lb, wv_ref[...]).astype(BF16)
    kp = kpe_ref[...]
    if kp.shape[1] == ROPE_DIM:
        kr_ref[...] = jnp.concatenate([kp, jnp.zeros((kp.shape[0], LANES - ROPE_DIM), F32)], axis=1).astype(BF16)
    else:
        lane = lax.broadcasted_iota(jnp.int32, kp.shape, 1)
        kr_ref[...] = jnp.where(lane < ROPE_DIM, kp, 0.0).astype(BF16)


def _kvproj(lat3, kpe3, pw, tm, v_transposed):
    B, sk, _ = lat3.shape
    hw = MLA_HEADS * V_DIM
    row = lambda n: pl.BlockSpec((None, tm, n), lambda b, i: (b, i, 0))
    if v_transposed:
        vspec, vshape, w_v = pl.BlockSpec((None, hw, tm), lambda b, i: (b, 0, i)), (B, hw, sk), pw['w_vt']
    else:
        vspec, vshape, w_v = row(hw), (B, sk, hw), pw['w_v']
    return pl.pallas_call(
        functools.partial(_kvproj_kernel, v_transposed=v_transposed),
        grid=(B, sk // tm),
        in_specs=[row(KV_RANK), row(kpe3.shape[-1]), _const_spec(pw['w_k'].shape), _const_spec(w_v.shape),
                  _const_spec(pw['knn'].shape)],
        out_specs=[row(MLA_HEADS * NOPE_DIM), row(LANES), vspec],
        out_shape=[jax.ShapeDtypeStruct((B, sk, MLA_HEADS * NOPE_DIM), BF16),
                   jax.ShapeDtypeStruct((B, sk, LANES), BF16), jax.ShapeDtypeStruct(vshape, BF16)],
        compiler_params=_cparams(("parallel", "parallel")),
        name="kvproj",
    )(lat3, kpe3, pw['w_k'], w_v, pw['knn'])


def _last_kblock(qi, tq, tk, P, nk):
    last_key = ((P + qi * tq + tq - 1) // CHUNK) * CHUNK + CHUNK - 1
    return jnp.minimum(last_key // tk, nk - 1)


def _flash_kernel(q_ref, kn_ref, kr_ref, v_ref, o_ref, m_ref, l_ref, acc_ref, *, tq, tk, P, S, nk):
    qi = pl.program_id(1)
    ki = pl.program_id(2)

    @pl.when(ki == 0)
    def _():
        m_ref[...] = jnp.full(m_ref.shape, NEG_BIG, F32)
        l_ref[...] = jnp.zeros(l_ref.shape, F32)
        acc_ref[...] = jnp.zeros(acc_ref.shape, F32)

    @pl.when(ki <= _last_kblock(qi, tq, tk, P, nk))
    def _():
        qpos = P + qi * tq + lax.broadcasted_iota(jnp.int32, (tq, tk), 0)
        kpos = ki * tk + lax.broadcasted_iota(jnp.int32, (tq, tk), 1)
        mask = ((kpos // CHUNK) <= (qpos // CHUNK)) & (kpos < P + S)
        kr = kr_ref[...]
        for h in range(MLA_HEADS):
            qh = q_ref[:, h * Q_SLAB:(h + 1) * Q_SLAB]
            kh = jnp.concatenate([kn_ref[:, h * NOPE_DIM:(h + 1) * NOPE_DIM], kr], axis=1)
            s = jnp.where(mask, _dot_nt(qh, kh), NEG_BIG)
            m_prev = m_ref[h]
            m_new = jnp.maximum(m_prev, jnp.max(s, axis=-1, keepdims=True))
            alpha = jnp.exp(m_prev - m_new)
            p = jnp.exp(s - m_new)
            l_ref[h] = alpha * l_ref[h] + jnp.sum(p, axis=-1, keepdims=True)
            acc_ref[h] = alpha * acc_ref[h] + _dot(p.astype(BF16), v_ref[:, h * V_DIM:(h + 1) * V_DIM])
            m_ref[h] = m_new

    @pl.when(ki == nk - 1)
    def _():
        for h in range(MLA_HEADS):
            o_ref[:, h * V_DIM:(h + 1) * V_DIM] = (acc_ref[h] / l_ref[h]).astype(BF16)


def _flash(q3, kn3, kr3, v3, P, S, tq, tk):
    B = q3.shape[0]
    skp = kn3.shape[1]
    nk = skp // tk
    kmap = lambda b, qi, ki: (b, jnp.minimum(ki, _last_kblock(qi, tq, tk, P, nk)), 0)
    return pl.pallas_call(
        functools.partial(_flash_kernel, tq=tq, tk=tk, P=P, S=S, nk=nk),
        grid=(B, S // tq, nk),
        in_specs=[pl.BlockSpec((None, tq, MLA_HEADS * Q_SLAB), lambda b, qi, ki: (b, qi, 0)),
                  pl.BlockSpec((None, tk, MLA_HEADS * NOPE_DIM), kmap),
                  pl.BlockSpec((None, tk, LANES), kmap),
                  pl.BlockSpec((None, tk, MLA_HEADS * V_DIM), kmap)],
        out_specs=pl.BlockSpec((None, tq, MLA_HEADS * V_DIM), lambda b, qi, ki: (b, qi, 0)),
        out_shape=jax.ShapeDtypeStruct((B, S, MLA_HEADS * V_DIM), BF16),
        scratch_shapes=[pltpu.VMEM((MLA_HEADS, tq, 1), F32), pltpu.VMEM((MLA_HEADS, tq, 1), F32),
                        pltpu.VMEM((MLA_HEADS, tq, V_DIM), F32)],
        compiler_params=_cparams(("parallel", "parallel", "arbitrary")),
        name="mla_attn",
    )(q3, kn3, kr3, v3)


def _attn_dec_kernel(q_ref, lat_ref, kpe_ref, knn_ref, krn_ref, vn_ref, wk_ref, wv_ref, gain_ref, o_ref):
    lb = lat_ref[...].astype(BF16)
    kf = _dot(lb, wk_ref[...])
    vp = _dot(lb, wv_ref[...]).astype(BF16)
    kp = kpe_ref[...]
    krp = jnp.concatenate([kp, jnp.zeros((kp.shape[0], LANES - ROPE_DIM), F32)], axis=1).astype(BF16)
    krn = krn_ref[...]
    hsl = [slice(h * NOPE_DIM, (h + 1) * NOPE_DIM) for h in range(MLA_HEADS)]
    qs = [q_ref[:, h * Q_SLAB:(h + 1) * Q_SLAB] for h in range(MLA_HEADS)]
    knp = [_rms(kf[:, sl], gain_ref[...]).astype(BF16) for sl in hsl]
    sps = [_dot_nt(q, jnp.concatenate([k, krp], axis=1)) for q, k in zip(qs, knp)]
    sns = [_dot_nt(q, jnp.concatenate([knn_ref[:, sl], krn], axis=1)) for q, sl in zip(qs, hsl)]
    for h, (sp, sn) in enumerate(zip(sps, sns)):
        vsl = slice(h * V_DIM, (h + 1) * V_DIM)
        m = jnp.maximum(jnp.max(sp, axis=-1, keepdims=True), jnp.max(sn, axis=-1, keepdims=True))
        pp = jnp.exp(sp - m)
        pn = jnp.exp(sn - m)
        l = jnp.sum(pp, axis=-1, keepdims=True) + jnp.sum(pn, axis=-1, keepdims=True)
        o = _dot(pp.astype(BF16), vp[:, vsl]) + _dot(pn.astype(BF16), vn_ref[:, vsl])
        o_ref[:, vsl] = (o / l).astype(BF16)


def _attn_dec(q3, lat_past, kpe_past, new, pw):
    B, S, _ = q3.shape
    consts = [pw['w_k'], pw['w_v'], pw['knn']]
    specs = [pl.BlockSpec((None, a.shape[1], a.shape[2]), lambda b: (b, 0, 0)) for a in (q3, lat_past, kpe_past, *new)]
    return pl.pallas_call(
        _attn_dec_kernel,
        grid=(B,),
        in_specs=specs + [_const_spec(c.shape) for c in consts],
        out_specs=pl.BlockSpec((None, S, MLA_HEADS * V_DIM), lambda b: (b, 0, 0)),
        out_shape=jax.ShapeDtypeStruct((B, S, MLA_HEADS * V_DIM), BF16),
        compiler_params=_cparams(("parallel",)),
        name="mla_attn_dec",
    )(q3, lat_past, kpe_past, *new, *consts)


def _flash_t_kernel(qt_ref, kt_ref, lt_ref, q_ref, kn_ref, kr_ref, vt_ref, o_ref, m_ref, acc_ref, *, tq, tk, P, S):
    j = pl.program_id(1)
    qi = qt_ref[j]
    ki = kt_ref[j]
    q0 = P + qi * tq
    k0 = ki * tk

    @pl.when(ki == 0)
    def _():
        m_ref[...] = jnp.full(m_ref.shape, NEG_BIG, F32)
        acc_ref[...] = jnp.zeros(acc_ref.shape, F32)

    def step(masked):
        kr = kr_ref[...]
        ones = jnp.ones((ONES_ROWS, tk), BF16)
        if masked:
            kpos = k0 + lax.broadcasted_iota(jnp.int32, (tk, 1), 0)
            qpos = q0 + lax.broadcasted_iota(jnp.int32, (1, tq), 1)
            mask = ((kpos // CHUNK) <= (qpos // CHUNK)) & (kpos < P + S)
        sts = []
        for h in range(MLA_HEADS):
            kh = jnp.concatenate([kn_ref[:, h * NOPE_DIM:(h + 1) * NOPE_DIM], kr], axis=1)
            sts.append(_dot_nt(kh, q_ref[:, h * Q_SLAB:(h + 1) * Q_SLAB]))
        ps, alphas = [], []
        for h in range(MLA_HEADS):
            st = sts[h]
            if masked:
                st = jnp.where(mask, st, NEG_BIG)
            m_prev = m_ref[h]
            m_new = jnp.maximum(m_prev, jnp.max(st, axis=0, keepdims=True))
            alphas.append(jnp.exp(m_prev - m_new))
            ps.append(jnp.exp(st - m_new).astype(BF16))
            m_ref[h] = m_new
        for h in range(MLA_HEADS):
            vt1 = jnp.concatenate([vt_ref[h * V_DIM:(h + 1) * V_DIM, :], ones], axis=0)
            acc_ref[h] = alphas[h] * acc_ref[h] + _dot(vt1, ps[h])

    full = ((k0 + tk - 1) // CHUNK <= q0 // CHUNK) & (k0 + tk <= P + S)
    pl.when(full)(functools.partial(step, False))
    pl.when(jnp.logical_not(full))(functools.partial(step, True))

    @pl.when(lt_ref[j] == 1)
    def _():
        for h in range(MLA_HEADS):
            acc = acc_ref[h]
            o_ref[:, h * V_DIM:(h + 1) * V_DIM] = (acc[:V_DIM] / acc[V_DIM:V_DIM + 1]).T.astype(BF16)


def _flash_t(q3, kn3, kr3, vt3, P, S, tq, tk):
    B = q3.shape[0]
    nk = kn3.shape[1] // tk
    pairs = []
    for qi in range(S // tq):
        last = min((((P + qi * tq + tq - 1) // CHUNK) * CHUNK + CHUNK - 1) // tk, nk - 1)
        pairs += [(qi, ki, int(ki == last)) for ki in range(last + 1)]
    qt, kt, lt = (jnp.asarray(np.array(col, np.int32)) for col in zip(*pairs))
    kmap = lambda b, j, qt, kt, lt: (b, kt[j], 0)
    gs = pltpu.PrefetchScalarGridSpec(
        num_scalar_prefetch=3,
        grid=(B, len(pairs)),
        in_specs=[pl.BlockSpec((None, tq, MLA_HEADS * Q_SLAB), lambda b, j, qt, kt, lt: (b, qt[j], 0)),
                  pl.BlockSpec((None, tk, MLA_HEADS * NOPE_DIM), kmap),
                  pl.BlockSpec((None, tk, LANES), kmap),
                  pl.BlockSpec((None, MLA_HEADS * V_DIM, tk), lambda b, j, qt, kt, lt: (b, 0, kt[j]))],
        out_specs=pl.BlockSpec((None, tq, MLA_HEADS * V_DIM), lambda b, j, qt, kt, lt: (b, qt[j], 0)),
        scratch_shapes=[pltpu.VMEM((MLA_HEADS, 1, tq), F32), pltpu.VMEM((MLA_HEADS, V_DIM + ONES_ROWS, tq), F32)],
    )
    return pl.pallas_call(
        functools.partial(_flash_t_kernel, tq=tq, tk=tk, P=P, S=S),
        grid_spec=gs,
        out_shape=jax.ShapeDtypeStruct((B, S, MLA_HEADS * V_DIM), BF16),
        compiler_params=_cparams(("parallel", "arbitrary")),
        name="mla_attn_t",
    )(qt, kt, lt, q3, kn3, kr3, vt3)


def _memkv_kernel(mem_ref, mnorm_ref, wmk_ref, wmv_ref, mkn_ref, k_ref, v_ref):
    mb = _rms(mem_ref[...], mnorm_ref[...]).astype(BF16)
    kf = _dot(mb, wmk_ref[...])
    for h in range(MEM_HEADS):
        sl = slice(h * MEM_HEAD_DIM, (h + 1) * MEM_HEAD_DIM)
        k_ref[:, sl] = _rms(kf[:, sl], mkn_ref[...])
    v_ref[...] = _dot(mb, wmv_ref[...])


def _memkv(mem2d, mnorm, w_mk, w_mv, mkn, tm):
    T = mem2d.shape[0]
    hw = MEM_HEADS * MEM_HEAD_DIM
    row = lambda n: pl.BlockSpec((tm, n), lambda i: (i, 0))
    return pl.pallas_call(
        _memkv_kernel,
        grid=(T // tm,),
        in_specs=[row(D_MODEL), _const_spec(mnorm.shape), _const_spec(w_mk.shape), _const_spec(w_mv.shape),
                  _const_spec(mkn.shape)],
        out_specs=[row(hw), row(hw)],
        out_shape=[jax.ShapeDtypeStruct((T, hw), F32), jax.ShapeDtypeStruct((T, hw), F32)],
        compiler_params=_cparams(("parallel",)),
        name="mem_kv",
    )(mem2d, mnorm, w_mk, w_mv, mkn)


def _mix_kernel(x_ref, og_ref, om_ref, mk_ref, mv_ref, wout_ref, nmem_ref, wmq_ref, mqn_ref, wmo_ref, nffn_ref,
                wrh_ref, wrl_ref, br_ref, x2_ref, h3_ref, idx_ref, gate_ref, counts_ref, cnt_ref):
    def mem_head(ref, b, h):
        if len(ref.shape) == 3:
            return ref[b, :, h * MEM_HEAD_DIM:(h + 1) * MEM_HEAD_DIM].astype(BF16)
        return ref[b, :, h, :].astype(BF16)

    first = (pl.program_id(0) == 0) & (pl.program_id(1) == 0)

    @pl.when(first)
    def _():
        cnt_ref[...] = jnp.zeros(cnt_ref.shape, F32)

    nbm, tm, _ = x_ref.shape
    rows = nbm * tm
    flat = lambda ref: ref[...].reshape(rows, ref.shape[-1])

    x1 = flat(x_ref) + _dot(flat(og_ref), wout_ref[0:GDN_VW, :]) + _dot(flat(om_ref), wout_ref[GDN_VW:, :])
    hb = _rms(x1, nmem_ref[...]).astype(BF16)
    qm = _dot(hb, wmq_ref[...])
    per_batch = []
    hsl = [slice(h * MEM_HEAD_DIM, (h + 1) * MEM_HEAD_DIM) for h in range(MEM_HEADS)]
    for b in range(nbm):
        br = slice(b * tm, (b + 1) * tm)
        qhs = [(_rms(qm[br, sl], mqn_ref[...]) * (MEM_HEAD_DIM ** -0.5)).astype(BF16) for sl in hsl]
        ss = [_dot_nt(qh, mem_head(mk_ref, b, h)) for h, qh in enumerate(qhs)]
        ps = []
        for s in ss:
            p = jnp.exp(s - jnp.max(s, axis=-1, keepdims=True))
            ps.append((p / jnp.sum(p, axis=-1, keepdims=True)).astype(BF16))
        heads = [_dot(p, mem_head(mv_ref, b, h)).astype(BF16) for h, p in enumerate(ps)]
        per_batch.append(jnp.concatenate(heads, axis=1))
    om = per_batch[0] if nbm == 1 else jnp.concatenate(per_batch, axis=0)
    x2 = x1 + _dot(om, wmo_ref[...])
    x2_ref[...] = x2.reshape(x2_ref.shape)
    h3 = _rms(x2, nffn_ref[...])
    hi = h3.astype(BF16)
    packed = _pack_bf16_pairs(hi.astype(F32))
    for c in range(h3_ref.shape[0]):
        h3_ref[c] = packed[:, c * LANES:(c + 1) * LANES]
    lo = (h3 - hi.astype(F32)).astype(BF16)
    wrh = wrh_ref[...]
    logits = _dot(hi, wrh) + _dot(lo, wrh) + _dot(hi, wrl_ref[...]) + br_ref[...]

    lane = lax.broadcasted_iota(jnp.int32, logits.shape, 1).astype(F32)
    vals, idxs = [], []
    cur = logits
    for _ in range(TOP_K):
        mx = jnp.max(cur, axis=-1, keepdims=True)
        ix = jnp.min(jnp.where(cur == mx, lane, float(LANES)), axis=-1, keepdims=True)
        vals.append(mx)
        idxs.append(ix)
        cur = jnp.where(lane == ix, -3e38, cur)
    es = [jnp.exp(v - vals[0]) for v in vals]
    den = es[0] + es[1] + es[2] + es[3]

    sel = jnp.zeros(logits.shape, F32)
    for k in range(TOP_K):
        sel = sel + jnp.where(lane == idxs[k], 1.0, 0.0)
    ri = lax.broadcasted_iota(jnp.int32, (rows, rows), 0)
    ci = lax.broadcasted_iota(jnp.int32, (rows, rows), 1)
    before = jnp.where(ri > ci, 1.0, 0.0).astype(BF16)
    excl = _dot(before, sel.astype(BF16)) + cnt_ref[...]
    cnt_ref[...] = cnt_ref[...] + jnp.sum(sel, axis=0, keepdims=True)
    counts_ref[...] = cnt_ref[...].astype(jnp.int32)

    idx_out = jnp.zeros(logits.shape, F32)
    gate_out = jnp.zeros(logits.shape, F32)
    for k in range(TOP_K):
        rank = jnp.sum(jnp.where(lane == idxs[k], excl, 0.0), axis=-1, keepdims=True)
        idx_out = jnp.where(lane == float(k), idxs[k], idx_out)
        idx_out = jnp.where(lane == float(TOP_K + k), rank, idx_out)
        gate_out = jnp.where(lane == float(k), es[k] / den, gate_out)
    idx_ref[...] = idx_out.T[:2 * TOP_K, :].astype(jnp.int32)
    gate_ref[...] = gate_out.reshape(gate_ref.shape)


def _mix(x3, og3, om3, mk3, mv3, pw, tm, b0, B):
    S = x3.shape[1]
    hw = MEM_HEADS * MEM_HEAD_DIM
    nbm = _pick(B, tuple(n for n in (8, 4, 2, 1) if n * tm <= MIX_ROWS and b0 % n == 0)) if tm == S else 1
    nsb = S // tm
    boff = b0 // nbm
    tile_in = lambda n: pl.BlockSpec((nbm, tm, n), lambda b, i: (b + boff, i, 0))
    tile = lambda n: pl.BlockSpec((nbm, tm, n), lambda b, i: (b, i, 0))
    if mk3.ndim == 3:
        memspec = pl.BlockSpec((nbm, N_MEM, hw), lambda b, i: (b + boff, 0, 0))
    else:
        memspec = pl.BlockSpec((nbm, N_MEM, MEM_HEADS, MEM_HEAD_DIM), lambda b, i: (b + boff, 0, 0, 0))
    consts = [pw['w_out'], pw['nmem'], pw['w_mq'], pw['mqn'], pw['w_mo'], pw['nffn'], pw['wr_hi'], pw['wr_lo'],
              pw['b_r']]
    return pl.pallas_call(
        _mix_kernel,
        grid=(B // nbm, nsb),
        in_specs=[tile_in(D_MODEL), tile_in(GDN_VW), tile_in(MLA_HEADS * V_DIM), memspec, memspec]
                 + [_const_spec(c.shape) for c in consts],
        out_specs=[tile(D_MODEL), pl.BlockSpec((PK_CHUNKS, nbm * tm, LANES), lambda b, i: (0, b * nsb + i, 0)),
                   pl.BlockSpec((2 * TOP_K, nbm * tm), lambda b, i: (0, b * nsb + i)),
                   tile(LANES), _const_spec((1, LANES))],
        out_shape=[jax.ShapeDtypeStruct((B, S, D_MODEL), F32),
                   jax.ShapeDtypeStruct((PK_CHUNKS, B * S, LANES), jnp.uint32),
                   jax.ShapeDtypeStruct((2 * TOP_K, B * S), jnp.int32), jax.ShapeDtypeStruct((B, S, LANES), F32),
                   jax.ShapeDtypeStruct((1, LANES), jnp.int32)],
        scratch_shapes=[pltpu.VMEM((1, LANES), F32)],
        compiler_params=_cparams(("arbitrary", "arbitrary")),
        name="mix_mem_router",
    )(x3, og3, om3, mk3, mv3, *consts)


def _expert_kernel(be_ref, nx_ref, nv_ref, nu_ref, rows_ref, wgu_hbm, bgu_ref, wd_hbm, bd_ref, y_ref,
                   wgus_ref, wds_ref, wgub_ref, wdb_ref, sem_ref):
    i = pl.program_id(0)
    used = i < nu_ref[0]
    new_expert = (i == 0) | (be_ref[i] != be_ref[jnp.maximum(i - 1, 0)])

    def weight_copies(e):
        return (pltpu.make_async_copy(wgu_hbm.at[e], wgus_ref, sem_ref.at[0]),
                pltpu.make_async_copy(wd_hbm.at[e], wds_ref, sem_ref.at[1]))

    @pl.when(i == 0)
    def _():
        for cp in weight_copies(be_ref[0]):
            cp.start()

    @pl.when(used & new_expert)
    def _():
        for cp in weight_copies(be_ref[i]):
            cp.wait()

        def cast(r, carry):
            rs = pl.ds(pl.multiple_of(r * LANES, LANES), LANES)
            wgub_ref[rs, :] = wgus_ref[rs, :].astype(BF16)
            wdb_ref[rs, :] = wds_ref[rs, :].astype(BF16)
            return carry
        lax.fori_loop(0, D_MODEL // LANES, cast, 0)

        @pl.when(nx_ref[i] >= 0)
        def _():
            for cp in weight_copies(nx_ref[i]):
                cp.start()

    def ffn(nrows):
        packed = jnp.concatenate([rows_ref[c, :nrows, :] for c in range(PK_CHUNKS)], axis=1)
        x = _unpack_bf16_pairs(packed).astype(BF16)
        acc = None
        for c in range(D_FF // FF_CHUNK):
            gs_ = slice(c * FF_CHUNK, (c + 1) * FF_CHUNK)
            us_ = slice(D_FF + c * FF_CHUNK, D_FF + (c + 1) * FF_CHUNK)
            gt = jnp.minimum(_dot(x, wgub_ref[:, gs_]) + bgu_ref[:, gs_], SWIGLU_LIMIT)
            up = jnp.clip(_dot(x, wgub_ref[:, us_]) + bgu_ref[:, us_], -SWIGLU_LIMIT, SWIGLU_LIMIT)
            act = gt * _sigmoid(SWIGLU_ALPHA * gt) * (up + 1.0)
            part = _dot(act.astype(BF16), wdb_ref[gs_, :])
            acc = part if acc is None else acc + part
        ypk = _pack_bf16_pairs((acc + bd_ref[...]).astype(BF16).astype(F32))
        for c in range(PK_CHUNKS):
            y_ref[c, :nrows, :] = ypk[:, c * LANES:(c + 1) * LANES]
        if nrows < MOE_ROWS:
            y_ref[:, nrows:, :] = jnp.zeros((PK_CHUNKS, MOE_ROWS - nrows, LANES), y_ref.dtype)

    half_full = nv_ref[i] <= MOE_ROWS // 2
    pl.when(used & jnp.logical_not(half_full))(functools.partial(ffn, MOE_ROWS))
    pl.when(used & half_full)(functools.partial(ffn, MOE_ROWS // 2))

    @pl.when(jnp.logical_not(used))
    def _():
        y_ref[...] = jnp.zeros(y_ref.shape, y_ref.dtype)


def _experts(block_e, next_e, block_rows, n_used, rows, w_gu, b_gu, w_down, b_down):
    n_rows = rows.shape[1]
    nb = n_rows // MOE_ROWS
    gs = pltpu.PrefetchScalarGridSpec(
        num_scalar_prefetch=4,
        grid=(nb,),
        in_specs=[pl.BlockSpec((PK_CHUNKS, MOE_ROWS, LANES), lambda i, be, nx, nv, nu: (0, i, 0)),
                  pl.BlockSpec(memory_space=pl.ANY),
                  pl.BlockSpec((None, 1, 2 * D_FF), lambda i, be, nx, nv, nu: (be[i], 0, 0)),
                  pl.BlockSpec(memory_space=pl.ANY),
                  pl.BlockSpec((None, 1, D_MODEL), lambda i, be, nx, nv, nu: (be[i], 0, 0))],
        out_specs=pl.BlockSpec((PK_CHUNKS, MOE_ROWS, LANES), lambda i, be, nx, nv, nu: (0, i, 0)),
        scratch_shapes=[pltpu.VMEM((D_MODEL, 2 * D_FF), F32), pltpu.VMEM((D_FF, D_MODEL), F32),
                        pltpu.VMEM((D_MODEL, 2 * D_FF), BF16), pltpu.VMEM((D_FF, D_MODEL), BF16),
                        pltpu.SemaphoreType.DMA((2,))],
    )
    return pl.pallas_call(
        _expert_kernel,
        grid_spec=gs,
        out_shape=jax.ShapeDtypeStruct((PK_CHUNKS, n_rows, LANES), jnp.uint32),
        compiler_params=_cparams(("arbitrary",)),
        name="moe_experts",
    )(block_e, next_e, block_rows, n_used, rows, w_gu, b_gu, w_down, b_down)


def _sc_mesh():
    return plsc.VectorSubcoreMesh(core_axis_name="core", subcore_axis_name="subcore")


def _sc_scatter_rows(x3s, pos_ts, n_rows):
    C, _, L = x3s[0].shape
    K = pos_ts[0].shape[0]
    ns = len(x3s)

    @functools.partial(pl.kernel, out_type=jax.ShapeDtypeStruct((C, n_rows, L), x3s[0].dtype), mesh=_sc_mesh(),
                       scratch_types=[])
    def scatter(*refs):
        o_hbm = refs[2 * ns]
        for s in range(ns):
            x_hbm, i_hbm = refs[s], refs[ns + s]
            nwin = x3s[s].shape[1] // SC_WINDOW
            for c in range(C):
                def body(x_vmem, i_vmem, c=c):
                    for k in range(K):
                        pltpu.sync_copy(x_vmem, o_hbm.at[c].at[i_vmem.at[k]])

                pltpu.emit_pipeline(
                    body, grid=(nwin,),
                    in_specs=[pl.BlockSpec((SC_WINDOW, L), lambda i, c=c, nwin=nwin: (c * nwin + i, 0)),
                              pl.BlockSpec((K, SC_WINDOW), lambda i: (0, i))],
                    out_specs=[], core_axis_name=("core", "subcore"), dimension_semantics=(pltpu.PARALLEL,),
                )(x_hbm, i_hbm)

    return scatter(*[x.reshape(-1, L) for x in x3s], *pos_ts)


def _sc_gather_rows(table3, idxs):
    C, _, L = table3.shape
    ns = len(idxs)
    out_type = [jax.ShapeDtypeStruct((C * i.shape[0], L), table3.dtype) for i in idxs]

    @functools.partial(pl.kernel, out_type=out_type, mesh=_sc_mesh(), scratch_types=[])
    def gather(t_hbm, *refs):
        for s in range(ns):
            i_hbm, o_hbm = refs[s], refs[ns + s]
            nwin = idxs[s].shape[0] // SC_WINDOW
            for c in range(C):
                def body(i_vmem, o_vmem, c=c):
                    pltpu.sync_copy(t_hbm.at[c].at[i_vmem.at[0]], o_vmem)

                pltpu.emit_pipeline(
                    body, grid=(nwin,),
                    in_specs=[pl.BlockSpec((1, SC_WINDOW), lambda i: (0, i))],
                    out_specs=[pl.BlockSpec((SC_WINDOW, L), lambda i, c=c, nwin=nwin: (c * nwin + i, 0))],
                    core_axis_name=("core", "subcore"), dimension_semantics=(pltpu.PARALLEL,),
                )(i_hbm, o_hbm)

    outs = gather(table3, *[i.reshape(1, -1) for i in idxs])
    return [o.reshape(C, -1, L) for o in outs]


def _combine_kernel(x2_ref, g_ref, gate_ref, *rest):
    o_ref = rest[-1]
    gate = gate_ref[...]
    half = D_MODEL // 2
    for c in range(PK_CHUNKS):
        lo_s = slice(c * LANES, (c + 1) * LANES)
        hi_s = slice(half + c * LANES, half + (c + 1) * LANES)
        acc_lo = x2_ref[:, lo_s]
        acc_hi = x2_ref[:, hi_s]
        for k in range(TOP_K):
            w = g_ref[c, k]
            gk = gate[:, k:k + 1]
            acc_lo = acc_lo + pltpu.bitcast(w << 16, F32) * gk
            acc_hi = acc_hi + pltpu.bitcast(w & jnp.uint32(0xFFFF0000), F32) * gk
        o_ref[:, lo_s] = acc_lo
        o_ref[:, hi_s] = acc_hi


def _combine(x2, g4, gate, tm, out_buf=None, row0=0, t_total=None):
    T = x2.shape[0]
    t_total = T if t_total is None else t_total
    blk0 = row0 // tm
    in_specs = [pl.BlockSpec((tm, D_MODEL), lambda i: (i, 0)),
                pl.BlockSpec((PK_CHUNKS, TOP_K, tm, LANES), lambda i: (0, 0, i, 0)),
                pl.BlockSpec((tm, LANES), lambda i: (i, 0))]
    args = [x2, g4, gate]
    aliases = {}
    if out_buf is not None:
        in_specs.append(pl.BlockSpec(memory_space=pl.ANY))
        args.append(out_buf)
        aliases = {3: 0}
    return pl.pallas_call(
        _combine_kernel,
        grid=(T // tm,),
        in_specs=in_specs,
        out_specs=pl.BlockSpec((tm, D_MODEL), lambda i: (i + blk0, 0)),
        out_shape=jax.ShapeDtypeStruct((t_total, D_MODEL), F32),
        input_output_aliases=aliases,
        compiler_params=_cparams(("parallel",)),
        name="moe_combine",
    )(*args)


def _plan_kernel(idxr_ref, base_ref, pos_ref):
    rows = 2 * TOP_K
    base = jnp.broadcast_to(base_ref[...], (rows, LANES))
    is_id = lax.broadcasted_iota(jnp.int32, (rows, LANES), 0) < TOP_K
    for j in range(idxr_ref.shape[1] // LANES):
        cols = slice(j * LANES, (j + 1) * LANES)
        blk = idxr_ref[:, cols]
        start = jnp.take_along_axis(base, jnp.where(is_id, blk, 0), axis=1)
        pos_ref[:, cols] = (start + pltpu.roll(blk, TOP_K, 0))[:TOP_K, :]


def _plan(idxr, base):
    T = idxr.shape[1]
    tm = _pick(T, (2048, 1024, 512, 256, 128))
    return pl.pallas_call(
        _plan_kernel,
        grid=(T // tm,),
        in_specs=[pl.BlockSpec((2 * TOP_K, tm), lambda i: (0, i)), _const_spec((1, LANES))],
        out_specs=pl.BlockSpec((TOP_K, tm), lambda i: (0, i)),
        out_shape=jax.ShapeDtypeStruct((TOP_K, T), jnp.int32),
        compiler_params=_cparams(("parallel",)),
        name="moe_plan",
    )(idxr, base)


def _moe(streams, places, ew):
    cnts = [st[3][0, :N_EXPERTS] for st in streams]
    total = sum(cnts)
    padded = (total + MOE_ROWS - 1) // MOE_ROWS * MOE_ROWS
    pad_end = jnp.cumsum(padded)
    pad_start = pad_end - padded
    pos_ts = []
    base = pad_start
    for (h3p, idxr, gate, counts, x2), cnt in zip(streams, cnts):
        pos_ts.append(_plan(idxr, _pad_lanes(base)))
        base = base + cnt
    n_assign = sum(st[0].shape[1] for st in streams) * TOP_K
    nb = -(-n_assign // MOE_ROWS) + N_EXPERTS
    starts = jnp.arange(nb, dtype=jnp.int32) * MOE_ROWS
    block_e = jnp.minimum(jnp.sum((pad_end[None, :] <= starts[:, None]).astype(jnp.int32), axis=1), N_EXPERTS - 1)
    n_used = (pad_end[-1] // MOE_ROWS).astype(jnp.int32).reshape(1)
    ar = jnp.arange(N_EXPERTS, dtype=jnp.int32)
    later = (padded > 0)[None, :] & (ar[None, :] > ar[:, None])
    nxt = jnp.min(jnp.where(later, ar[None, :], N_EXPERTS), axis=1)
    nxt = jnp.where(nxt >= N_EXPERTS, -1, nxt)
    mine = block_e[:, None] == ar[None, :]
    next_e = jnp.sum(jnp.where(mine, nxt[None, :], 0), axis=1).astype(jnp.int32)
    seg_end = jnp.sum(jnp.where(mine, (pad_start + total)[None, :], 0), axis=1)
    block_rows = jnp.clip(seg_end - starts, 0, MOE_ROWS).astype(jnp.int32)
    rows = _sc_scatter_rows([st[0] for st in streams], pos_ts, nb * MOE_ROWS)
    y_rows = _experts(block_e, next_e, block_rows, n_used, rows, ew['w_gu'], ew['b_gu'], ew['w_down'], ew['b_down'])
    gs = _sc_gather_rows(y_rows, [p.reshape(-1) for p in pos_ts])
    outs = []
    for (h3p, idxr, gate, counts, x2), g, place in zip(streams, gs, places):
        T = x2.shape[0]
        outs.append(_combine(x2, g.reshape(PK_CHUNKS, TOP_K, T, LANES), gate, _pick(T, (512, 256, 128, 64)), *place))
    return outs


def _pad_lanes(v, n=LANES, fill=0.0):
    return jnp.pad(v, (0, n - v.shape[0]), constant_values=fill).reshape(1, n)


def _prep_weights(norm_mix, w_in, conv_w, a_log, dt_bias, gdn_norm, q_a_norm, w_qb, kv_a_norm, w_kvb, q_norm,
                  k_nope_norm, k_rope_norm, w_out, norm_mem, mem_norm, w_mq, w_mk, w_mv, mq_norm, mk_norm, w_mo,
                  norm_ffn, w_router, b_router, w_gu, b_gu, w_down, b_down):
    c = np.cumsum([CONV_DIM, GDN_VW, GDN_HEADS, GDN_HEADS, Q_RANK, KV_RANK])
    w_u, w_z, w_a, w_b, w_cq, w_ckv, w_kpe = [w_in[:, lo:hi] for lo, hi in
                                              zip([0, *c], [*c, w_in.shape[1]])]
    w_s = jnp.concatenate([w_kpe, w_a, w_b], axis=1)
    w_s = jnp.pad(w_s, ((0, 0), (0, LANES - w_s.shape[1])))
    wq = w_qb.reshape(Q_RANK, MLA_HEADS, QK_DIM)
    wq = jnp.pad(wq, ((0, 0), (0, 0), (0, Q_SLAB - QK_DIM))).reshape(Q_RANK, MLA_HEADS * Q_SLAB)
    wkv = w_kvb.reshape(KV_RANK, MLA_HEADS, NOPE_DIM + V_DIM)
    w_k = wkv[:, :, :NOPE_DIM].reshape(KV_RANK, -1).astype(BF16)
    w_v = wkv[:, :, NOPE_DIM:].reshape(KV_RANK, -1).astype(BF16)
    wr = jnp.pad(w_router, ((0, 0), (0, LANES - N_EXPERTS)))
    wr_hi = wr.astype(BF16)
    wr_lo = (wr - wr_hi.astype(F32)).astype(BF16)
    row = lambda v: v.reshape(1, -1)
    gpad = ROPE_DIM
    pw = dict(
        nmix=row(norm_mix), w_u=w_u.astype(BF16), w_z=w_z.astype(BF16), w_cq=w_cq.astype(BF16),
        w_ckv=w_ckv.astype(BF16), w_s=w_s.astype(BF16), qan=row(q_a_norm), w_qb=wq.astype(BF16),
        qn=_pad_lanes(q_norm, Q_SLAB), kvan=row(kv_a_norm), krn=_pad_lanes(k_rope_norm),
        alog=jnp.pad(a_log, (gpad, LANES - gpad - GDN_HEADS)).reshape(1, LANES),
        dtb=jnp.pad(dt_bias, (gpad, LANES - gpad - GDN_HEADS)).reshape(1, LANES),
        conv_w=conv_w, gnorm=row(gdn_norm), w_k=w_k, w_v=w_v, w_vt=w_v.T, knn=row(k_nope_norm),
        w_out=w_out.astype(BF16), nmem=row(norm_mem), w_mq=w_mq.astype(BF16), mqn=row(mq_norm),
        w_mo=w_mo.astype(BF16), nffn=row(norm_ffn), wr_hi=wr_hi, wr_lo=wr_lo,
        b_r=_pad_lanes(b_router, LANES, NEG_BIG),
        mnorm=row(mem_norm), w_mk=w_mk.astype(BF16), w_mv=w_mv.astype(BF16), mkn=row(mk_norm),
    )
    ew = dict(w_gu=w_gu, b_gu=b_gu.reshape(N_EXPERTS, 1, 2 * D_FF), w_down=w_down,
              b_down=b_down.reshape(N_EXPERTS, 1, D_MODEL))
    return pw, ew


def _rope_tables(P, S):
    half = ROPE_DIM // 2
    inv = ROPE_THETA ** (-jnp.arange(half, dtype=F32) / half)
    ang = (P + jnp.arange(S, dtype=jnp.int32)).astype(F32)[:, None] * inv[None, :]
    cos, sin = jnp.cos(ang), jnp.sin(ang)
    zh = jnp.zeros((S, half), F32)
    zz = jnp.zeros((S, LANES - ROPE_DIM), F32)
    return (jnp.concatenate([cos, cos, zz], 1), jnp.concatenate([-sin, zh, zz], 1),
            jnp.concatenate([zh, sin, zz], 1))


def _pick(n, prefs):
    for t in prefs:
        if n % t == 0:
            return t
    return n


def _trunk_front(x, lat_past, kpe_past, s0, conv_past, mem_k, mem_v, pw, n_groups):
    B, S, D = x.shape
    P = lat_past.shape[1]
    T = B * S
    tm = _pick(S, (512, 256, 128, 64))
    tm_in = 512 if (T % 512 == 0 and (512 % S == 0 or S % 512 == 0)) else tm
    u, z, q, lat_new, small = _inproj(x.reshape(T, D), S, tm_in, pw, _rope_tables(P, S))

    LC = _pick(S, (256, 128, 64))
    NB = _pick(B, tuple(n for n in (8, 4, 2, 1) if n * (LC // CHUNK) <= GDN_UNITS))
    o_gdn, s_new, conv_new = _gdn(u.reshape(B, S, CONV_DIM), small.reshape(B, S, LANES), z.reshape(B, S, GDN_VW),
                                  pw['conv_w'], conv_past, s0, pw['gnorm'], NB, LC)

    kpe_new = small[:, :ROPE_DIM].reshape(B, S, ROPE_DIM)
    lat3 = lat_new.reshape(B, S, KV_RANK)
    small3 = small.reshape(B, S, LANES)
    q3 = q.reshape(B, S, MLA_HEADS * Q_SLAB)
    tq = _pick(S, (512, 256, 128, 64))
    if P == 0 and S % 512 == 0:
        kn, kr, vt = _kvproj(lat3, small3, pw, _pick(S, (2048, 1024, 512)), True)
        o_mla = _flash_t(q3, kn, kr, vt, P, S, tq, 512)
    elif P > 0 and P % CHUNK == 0 and S == CHUNK:
        new = _kvproj(lat3, small3, pw, S, False)
        o_mla = _attn_dec(q3, lat_past, kpe_past, new, pw)
    else:
        sk = P + S
        tk = 512 if S >= 512 else -(-sk // LANES) * LANES
        skp = -(-sk // tk) * tk
        lat_all = jnp.pad(jnp.concatenate([lat_past, lat3], axis=1), ((0, 0), (0, skp - sk), (0, 0)))
        kpe_all = jnp.pad(jnp.concatenate([kpe_past, kpe_new], axis=1), ((0, 0), (0, skp - sk), (0, 0)))
        key_major = tq >= LANES
        kn, kr, v = _kvproj(lat_all, kpe_all, pw, _pick(skp, (512,)), key_major)
        o_mla = (_flash_t if key_major else _flash)(q3, kn, kr, v, P, S, tq, tk)

    streams = []
    gb = B // n_groups
    for gi in range(n_groups):
        x2, h3p, idxr, gate, counts = _mix(x, o_gdn, o_mla, mem_k, mem_v, pw, tm, gi * gb, gb)
        tg = gb * S
        streams.append((h3p, idxr, gate.reshape(tg, LANES), counts, x2.reshape(tg, D)))
    return streams, (lat_new.reshape(B, S, KV_RANK), kpe_new, s_new, conv_new)


def kernel(x_prompt, x_sample, cache_kv_latent, cache_k_rope, state_gdn, state_conv, cache_mem_k, cache_mem_v, mem_prompt, norm_mix, w_in, conv_w, a_log, dt_bias, gdn_norm, q_a_norm, w_qb, kv_a_norm, w_kvb, q_norm, k_nope_norm, k_rope_norm, w_out, norm_mem, mem_norm, w_mq, w_mk, w_mv, mq_norm, mk_norm, w_mo, norm_ffn, w_router, b_router, w_gu, b_gu, w_down, b_down):
    depth = norm_mix.shape[0]
    yp, ys = x_prompt, x_sample
    bp = x_prompt.shape[0]
    hw = MEM_HEADS * MEM_HEAD_DIM
    outs = [[] for _ in range(10)]
    for l in range(depth):
        pw, ew = _prep_weights(norm_mix[l], w_in[l], conv_w[l], a_log[l], dt_bias[l], gdn_norm[l], q_a_norm[l],
                               w_qb[l], kv_a_norm[l], w_kvb[l], q_norm[l], k_nope_norm[l], k_rope_norm[l], w_out[l],
                               norm_mem[l], mem_norm[l], w_mq[l], w_mk[l], w_mv[l], mq_norm[l], mk_norm[l], w_mo[l],
                               norm_ffn[l], w_router[l], b_router[l], w_gu[l], b_gu[l], w_down[l], b_down[l])
        nm = mem_prompt.shape[1]
        mk, mv = _memkv(mem_prompt.reshape(bp * nm, D_MODEL), pw['mnorm'], pw['w_mk'], pw['w_mv'], pw['mkn'],
                        _pick(bp * nm, (512, 256)))
        mk = mk.reshape(bp, nm, hw)
        mv = mv.reshape(bp, nm, hw)
        n_groups = 2 if bp % 2 == 0 else 1
        streams_p, (lat, kpe, s_fin, cv) = _trunk_front(
            yp, jnp.zeros((bp, 0, KV_RANK), F32), jnp.zeros((bp, 0, ROPE_DIM), F32),
            jnp.zeros((bp, GDN_HEADS, GDN_DK, GDN_DV), F32), jnp.zeros((bp, CONV_W - 1, CONV_DIM), F32), mk, mv, pw,
            n_groups)
        bs = x_sample.shape[0]
        (stream_s,), (lat2, kpe2, s_fin2, cv2) = _trunk_front(
            ys, cache_kv_latent[l], cache_k_rope[l], state_gdn[l], state_conv[l],
            cache_mem_k[l], cache_mem_v[l], pw, 1)
        tp = yp.shape[0] * yp.shape[1]
        tg = tp // n_groups
        ybuf = None
        for gi, st in enumerate(streams_p):
            last = gi == n_groups - 1
            res = _moe([st] + ([stream_s] if last else []),
                       [(ybuf, gi * tg, tp)] + ([(None, 0, None)] if last else []), ew)
            ybuf = res[0]
        yp, ys = ybuf.reshape(yp.shape), res[1].reshape(ys.shape)
        for lst, val in zip(outs, (lat, kpe, s_fin, cv, mk.reshape(bp, nm, MEM_HEADS, MEM_HEAD_DIM),
                                   mv.reshape(bp, nm, MEM_HEADS, MEM_HEAD_DIM), lat2, kpe2, s_fin2, cv2)):
            lst.append(val)
    return (yp, ys) + tuple(jnp.stack(o) for o in outs)
```

```python
import functools
import math

import numpy as np
import jax
import jax.numpy as jnp
from jax import lax
from jax.experimental import pallas as pl
from jax.experimental.pallas import tpu as pltpu
from jax.experimental.pallas import tpu_sc as plsc

F32 = jnp.float32
BF16 = jnp.bfloat16

D_MODEL = 1024
CHUNK = 64
EPS = 1e-6
GDN_HEADS = 4
GDN_DK = 128
GDN_DV = 128
CONV_W = 4
GDN_QK = GDN_HEADS * GDN_DK
GDN_VW = GDN_HEADS * GDN_DV
CONV_DIM = 2 * GDN_QK + GDN_VW
MLA_HEADS = 4
Q_RANK = 384
KV_RANK = 256
NOPE_DIM = 128
ROPE_DIM = 64
V_DIM = 128
QK_DIM = NOPE_DIM + ROPE_DIM
ROPE_THETA = 10000.0
N_MEM = 256
MEM_HEADS = 4
MEM_HEAD_DIM = 128
N_EXPERTS = 32
TOP_K = 4
D_FF = D_MODEL
SWIGLU_ALPHA = 1.702
SWIGLU_LIMIT = 7.0

LANES = 128
Q_SLAB = 2 * LANES
NEG_BIG = -1e30
VMEM_LIMIT = 56 * 1024 * 1024
MOE_ROWS = 512
GDN_UNITS = 8
GDN_GROUP = 2
FF_CHUNK = 512
SC_WINDOW = 128
PK_CHUNKS = D_MODEL // 2 // LANES
SUBTILE_ROWS = 256
ONES_ROWS = 16
MIX_ROWS = 512


def _cparams(sem):
    return pltpu.CompilerParams(dimension_semantics=sem, vmem_limit_bytes=VMEM_LIMIT)


def _dot(a, b):
    return jnp.dot(a, b, preferred_element_type=F32)


def _dot_nt(a, b):
    return lax.dot_general(a, b, (((1,), (1,)), ((), ())), preferred_element_type=F32)


def _dot_tn(a, b):
    return lax.dot_general(a, b, (((0,), (0,)), ((), ())), preferred_element_type=F32)


def _rms(x, gain, n=None):
    n = x.shape[-1] if n is None else n
    ss = jnp.sum(x * x, axis=-1, keepdims=True) * (1.0 / n)
    return (x * lax.rsqrt(ss + EPS)) * gain


def _sigmoid(x):
    return 1.0 / (1.0 + jnp.exp(-x))


def _rope128(r, cos, sna, snb):
    return r * cos + pltpu.roll(r, 96, 1) * sna + pltpu.roll(r, 32, 1) * snb


def _pack_bf16_pairs(x):
    n = x.shape[1] // 2
    lo = pltpu.bitcast(x[:, :n], jnp.uint32) >> 16
    hi = pltpu.bitcast(x[:, n:], jnp.uint32) & jnp.uint32(0xFFFF0000)
    return lo | hi


def _unpack_bf16_pairs(p):
    lo = pltpu.bitcast(p << 16, F32)
    hi = pltpu.bitcast(p & jnp.uint32(0xFFFF0000), F32)
    return jnp.concatenate([lo, hi], axis=1)


def _subtiles(rows):
    n = rows // SUBTILE_ROWS if rows % SUBTILE_ROWS == 0 else 1
    step = rows // n
    return [slice(i * step, (i + 1) * step) for i in range(n)]


def _const_spec(shape):
    nd = len(shape)
    return pl.BlockSpec(shape, lambda *_: (0,) * nd)


def _inproj_kernel(x_ref, nmix_ref, wu_ref, wz_ref, wcq_ref, wckv_ref, ws_ref, qan_ref, wqb_ref, qn_ref,
                   kvan_ref, krn_ref, alog_ref, dtb_ref, cos_ref, sna_ref, snb_ref,
                   u_ref, z_ref, q_ref, lat_ref, small_ref):
    for rs in _subtiles(x_ref.shape[0]):
        x = x_ref[rs, :]
        hb = _rms(x, nmix_ref[...]).astype(BF16)
        u_ref[rs, :] = _dot(hb, wu_ref[...])
        z_ref[rs, :] = _dot(hb, wz_ref[...])
        cq_raw = _dot(hb, wcq_ref[...])
        ckv_raw = _dot(hb, wckv_ref[...])
        sm = _dot(hb, ws_ref[...])
        cos, sna, snb = cos_ref[rs, :], sna_ref[rs, :], snb_ref[rs, :]

        cq = _rms(cq_raw, qan_ref[...]).astype(BF16)
        qf = _dot(cq, wqb_ref[...])
        scale = QK_DIM ** -0.5
        for h in range(MLA_HEADS):
            slab = qf[:, h * Q_SLAB:(h + 1) * Q_SLAB]
            slab = _rms(slab, qn_ref[...], n=QK_DIM)
            nope = slab[:, :LANES]
            ropd = _rope128(slab[:, LANES:], cos, sna, snb)
            q_ref[rs, h * Q_SLAB:h * Q_SLAB + LANES] = (nope * scale).astype(BF16)
            q_ref[rs, h * Q_SLAB + LANES:(h + 1) * Q_SLAB] = (ropd * scale).astype(BF16)

        lat_ref[rs, :] = _rms(ckv_raw, kvan_ref[...])

        lane = lax.broadcasted_iota(jnp.int32, sm.shape, 1)
        kp = jnp.where(lane < ROPE_DIM, sm, 0.0)
        kpe = _rope128(_rms(kp, krn_ref[...], n=ROPE_DIM), cos, sna, snb)
        sp = sm + dtb_ref[...]
        softplus = jnp.maximum(sp, 0.0) + jnp.log1p(jnp.exp(-jnp.abs(sp)))
        g = -jnp.exp(alog_ref[...]) * softplus
        beta = _sigmoid(sm)
        small_ref[rs, :] = jnp.where(lane < ROPE_DIM, kpe,
                                     jnp.where(lane < ROPE_DIM + GDN_HEADS, g,
                                               jnp.where(lane < ROPE_DIM + 2 * GDN_HEADS, beta, 0.0)))


def _inproj(x2d, S, tm, pw, tabs):
    T = x2d.shape[0]
    if tm > S:
        tabs = [jnp.tile(t, (tm // S, 1)) for t in tabs]
    nblk_s = max(S // tm, 1)
    row = lambda n: pl.BlockSpec((tm, n), lambda i: (i, 0))
    tab = pl.BlockSpec((tm, LANES), lambda i: (i % nblk_s, 0))
    consts = [pw['nmix'], pw['w_u'], pw['w_z'], pw['w_cq'], pw['w_ckv'], pw['w_s'], pw['qan'], pw['w_qb'],
              pw['qn'], pw['kvan'], pw['krn'], pw['alog'], pw['dtb']]
    return pl.pallas_call(
        _inproj_kernel,
        grid=(T // tm,),
        in_specs=[row(D_MODEL)] + [_const_spec(c.shape) for c in consts] + [tab, tab, tab],
        out_specs=[row(CONV_DIM), row(GDN_VW), row(MLA_HEADS * Q_SLAB), row(KV_RANK), row(LANES)],
        out_shape=[jax.ShapeDtypeStruct((T, CONV_DIM), F32), jax.ShapeDtypeStruct((T, GDN_VW), F32),
                   jax.ShapeDtypeStruct((T, MLA_HEADS * Q_SLAB), BF16), jax.ShapeDtypeStruct((T, KV_RANK), F32),
                   jax.ShapeDtypeStruct((T, LANES), F32)],
        compiler_params=_cparams(("parallel",)),
        name="inproj",
    )(x2d, *consts, *tabs)


def _split3(x):
    hi = x.astype(BF16)
    r = x - hi.astype(F32)
    mid = r.astype(BF16)
    lo = (r - mid.astype(F32)).astype(BF16)
    return hi, mid, lo


def _gdn_kernel(u_ref, small_ref, z_ref, convw_ref, cpast_ref, s0_ref, gnorm_ref,
                to_ref, trilm_ref, strictm_ref, same2_ref, lvl_ref,
                o_ref, sfin_ref, cnew_ref, ext_ref, uc_ref, state_ref,
                qf_ref, kf_ref, vb_ref, bt_ref, gcum_ref, glast_ref, kk_ref, qk_ref, mb_ref, x_ref, qkb_ref,
                kbe_ref, qg_ref, kdec_ref, egl_ref, t1_ref, uu_ref, ww_ref, vn_ref, qs_ref, *, NB, LC):
    j = pl.program_id(1)
    nj = pl.num_programs(1)
    PADR = 8
    C = LC // CHUNK
    NG = GDN_HEADS // GDN_GROUP
    U = NB * C * NG
    HR = GDN_GROUP * CHUNK

    def unit(nb, c, g):
        return (nb * C + c) * NG + g

    def head_rows(h):
        return slice((h % GDN_GROUP) * CHUNK, (h % GDN_GROUP + 1) * CHUNK)

    @pl.when(j == 0)
    def _():
        state_ref[...] = s0_ref[...]
        ext_ref[:, PADR - (CONV_W - 1):PADR, :] = cpast_ref[...]

    w = convw_ref[...]
    for nb in range(NB):
        ext_ref[nb, PADR:PADR + LC, :] = u_ref[nb]
        acc = ext_ref[nb, PADR:PADR + LC, :] * w[CONV_W - 1:CONV_W, :]
        for t in range(1, CONV_W):
            acc = acc + ext_ref[nb, PADR - t:PADR - t + LC, :] * w[CONV_W - 1 - t:CONV_W - t, :]
        uc_ref[nb] = acc * _sigmoid(acc)
        ext_ref[nb, 0:PADR, :] = ext_ref[nb, LC:LC + PADR, :]

    @pl.when(j == nj - 1)
    def _():
        cnew_ref[...] = ext_ref[:, PADR - (CONV_W - 1):PADR, :]

    g0 = ROPE_DIM
    b0 = ROPE_DIM + GDN_HEADS
    to = to_ref[...]
    for nb, c in [(nb, c) for nb in range(NB) for c in range(C)]:
        rows = slice(c * CHUNK, (c + 1) * CHUNK)
        sm = small_ref[nb, rows, :]
        gl = sum(_dot(to, part) for part in _split3(sm))
        for h in range(GDN_HEADS):
            u = unit(nb, c, h // GDN_GROUP)
            hr = head_rows(h)
            q = uc_ref[nb, rows, h * GDN_DK:(h + 1) * GDN_DK]
            k = uc_ref[nb, rows, GDN_QK + h * GDN_DK:GDN_QK + (h + 1) * GDN_DK]
            v = uc_ref[nb, rows, 2 * GDN_QK + h * GDN_DV:2 * GDN_QK + (h + 1) * GDN_DV]
            beta = jnp.broadcast_to(sm[:, b0 + h:b0 + h + 1], (CHUNK, LANES))
            qf_ref[u, hr, :] = (q * lax.rsqrt(jnp.sum(q * q, -1, keepdims=True) + EPS)) * (GDN_DK ** -0.5)
            kf_ref[u, hr, :] = k * lax.rsqrt(jnp.sum(k * k, -1, keepdims=True) + EPS)
            vb_ref[u, hr, :] = (v * beta).astype(BF16)
            bt_ref[u, hr, :] = beta
            gcum_ref[u, hr, :] = jnp.broadcast_to(gl[:CHUNK, g0 + h:g0 + h + 1], (CHUNK, LANES))
            glast_ref[u, hr, :] = jnp.broadcast_to(gl[CHUNK:, g0 + h:g0 + h + 1], (CHUNK, LANES))

    for u in range(U):
        k = kf_ref[u]
        kbf = k.astype(BF16)
        kk_ref[u] = _dot_nt((k * bt_ref[u]).astype(BF16), kbf)
        qk_ref[u] = _dot_nt(qf_ref[u].astype(BF16), kbf)

    trilm = trilm_ref[...]
    eye = trilm - strictm_ref[...]
    for u in range(U):
        gcum = gcum_ref[u]
        grow = gcum.T[0:1, :]
        gcol = gcum if HR == LANES else jnp.concatenate([gcum] * (HR // LANES), axis=1)
        decay = jnp.exp(jnp.minimum(gcol - grow, 0.0)) * trilm
        m = kk_ref[u] * (decay * strictm_ref[...])
        mb_ref[u] = m.astype(BF16)
        x_ref[u] = eye - m * same2_ref[...]
        qkb_ref[u] = (qk_ref[u] * decay).astype(BF16)
        egc = jnp.exp(gcum)
        k = kf_ref[u]
        kbe_ref[u] = (k * bt_ref[u] * egc).astype(BF16)
        qg_ref[u] = (qf_ref[u] * egc).astype(BF16)
        kdec_ref[u] = (k * jnp.exp(glast_ref[u] - gcum)).astype(BF16)
        egl_ref[u] = jnp.exp(glast_ref[u])

    for lvl in range(lvl_ref.shape[0]):
        lm = lvl_ref[lvl]
        for u in range(U):
            t1_ref[u] = _dot(mb_ref[u] * lm, x_ref[u].astype(BF16)).astype(BF16)
        for u in range(U):
            x = x_ref[u]
            x_ref[u] = x - _dot(x.astype(BF16), t1_ref[u])

    for u in range(U):
        xb = x_ref[u].astype(BF16)
        uu_ref[u] = _dot(xb, vb_ref[u])
        ww_ref[u] = _dot(xb, kbe_ref[u]).astype(BF16)

    gnorm = gnorm_ref[...]
    for c in range(C):
        rows = slice(c * CHUNK, (c + 1) * CHUNK)
        heads = [(nb, h, unit(nb, c, h // GDN_GROUP), head_rows(h)) for nb in range(NB) for h in range(GDN_HEADS)]
        groups = [(nb, g, unit(nb, c, g)) for nb in range(NB) for g in range(NG)]
        for nb, h, u, hr in heads:
            stb = state_ref[nb, h].astype(BF16)
            r = _dot(jnp.concatenate([ww_ref[u, hr, :], qg_ref[u, hr, :]], axis=0), stb)
            vn_ref[u, hr, :] = (uu_ref[u, hr, :] - r[:CHUNK]).astype(BF16)
            qs_ref[u, hr, :] = r[CHUNK:]
        outs = {(nb, g): qs_ref[u] + _dot(qkb_ref[u], vn_ref[u]) for nb, g, u in groups}
        for nb, h, u, hr in heads:
            state_ref[nb, h] = (state_ref[nb, h] * egl_ref[u, hr.start:hr.start + 1, :]
                                + _dot_tn(kdec_ref[u, hr, :], vn_ref[u, hr, :]))
        for nb, h, u, hr in heads:
            zz = z_ref[nb, rows, h * GDN_DV:(h + 1) * GDN_DV]
            og = _rms(outs[(nb, h // GDN_GROUP)][hr, :], gnorm) * (zz * _sigmoid(zz))
            o_ref[nb, rows, h * GDN_DV:(h + 1) * GDN_DV] = og.astype(BF16)

    @pl.when(j == nj - 1)
    def _():
        sfin_ref[...] = state_ref[...]


def _gdn_masks():
    hr = GDN_GROUP * CHUNK
    i = np.arange(hr)[:, None]
    j = np.arange(hr)[None, :]
    same_head = (i // CHUNK) == (j // CHUNK)
    tril = same_head & (i >= j)
    strict = same_head & (i > j)
    same2 = strict & ((i // 2) == (j // 2))
    lvls = []
    blk = 2
    while blk < CHUNK:
        lvls.append(strict & ((i // (2 * blk)) == (j // (2 * blk))) & ((i // blk) != (j // blk)))
        blk *= 2
    fr = np.arange(CHUNK)
    to = np.concatenate([fr[:, None] >= fr[None, :], np.ones((CHUNK, CHUNK), bool)], axis=0)
    f = lambda a: jnp.asarray(a.astype(np.float32))
    return (jnp.asarray(to.astype(np.float32), dtype=BF16), f(tril), f(strict), f(same2),
            jnp.asarray(np.stack(lvls).astype(np.float32), dtype=BF16))


def _gdn(u3, small3, z3, conv_w, conv_past, s0, gnorm, NB, LC):
    B, S, _ = u3.shape
    C = LC // CHUNK
    U = NB * C * (GDN_HEADS // GDN_GROUP)
    HR = GDN_GROUP * CHUNK
    masks = _gdn_masks()
    tile = lambda n: pl.BlockSpec((NB, LC, n), lambda b, j: (b, j, 0))
    stspec = pl.BlockSpec((NB, GDN_HEADS, GDN_DK, GDN_DV), lambda b, j: (b, 0, 0, 0))
    cvspec = pl.BlockSpec((NB, CONV_W - 1, CONV_DIM), lambda b, j: (b, 0, 0))
    vm = lambda shape, dt: pltpu.VMEM(shape, dt)
    return pl.pallas_call(
        functools.partial(_gdn_kernel, NB=NB, LC=LC),
        grid=(B // NB, S // LC),
        in_specs=[tile(CONV_DIM), tile(LANES), tile(GDN_VW), _const_spec(conv_w.shape), cvspec, stspec,
                  _const_spec(gnorm.shape)] + [_const_spec(m.shape) for m in masks],
        out_specs=[tile(GDN_VW), stspec, cvspec],
        out_shape=[jax.ShapeDtypeStruct((B, S, GDN_VW), BF16),
                   jax.ShapeDtypeStruct((B, GDN_HEADS, GDN_DK, GDN_DV), F32),
                   jax.ShapeDtypeStruct((B, CONV_W - 1, CONV_DIM), F32)],
        scratch_shapes=[vm((NB, LC + 8, CONV_DIM), F32), vm((NB, LC, CONV_DIM), F32),
                        vm((NB, GDN_HEADS, GDN_DK, GDN_DV), F32),
                        vm((U, HR, LANES), F32), vm((U, HR, LANES), F32), vm((U, HR, LANES), BF16),
                        vm((U, HR, LANES), F32),
                        vm((U, HR, LANES), F32), vm((U, HR, LANES), F32),
                        vm((U, HR, HR), F32), vm((U, HR, HR), F32),
                        vm((U, HR, HR), BF16), vm((U, HR, HR), F32), vm((U, HR, HR), BF16),
                        vm((U, HR, LANES), BF16), vm((U, HR, LANES), BF16), vm((U, HR, LANES), BF16),
                        vm((U, HR, LANES), F32), vm((U, HR, HR), BF16),
                        vm((U, HR, LANES), F32), vm((U, HR, LANES), BF16),
                        vm((U, HR, LANES), BF16), vm((U, HR, LANES), F32)],
        compiler_params=_cparams(("parallel", "arbitrary")),
        name="gdn",
    )(u3, small3, z3, conv_w, conv_past, s0, gnorm, *masks)


def _kvproj_kernel(lat_ref, kpe_ref, wk_ref, wv_ref, knn_ref, kn_ref, kr_ref, v_ref, *, v_transposed):
    lb = lat_ref[...].astype(BF16)
    kf = _dot(lb, wk_ref[...])
    for h in range(MLA_HEADS):
        sl = slice(h * NOPE_DIM, (h + 1) * NOPE_DIM)
        kn_ref[:, sl] = _rms(kf[:, sl], knn_ref[...]).astype(BF16)
    if v_transposed:
        v_ref[...] = _dot_nt(wv_ref[...], lb).astype(BF16)
    else:
        v_ref[...] = _dot(lb, wv_ref[...]).astype(BF16)
    kp = kpe_ref[...]
    if kp.shape[1] == ROPE_DIM:
        kr_ref[...] = jnp.concatenate([kp, jnp.zeros((kp.shape[0], LANES - ROPE_DIM), F32)], axis=1).astype(BF16)
    else:
        lane = lax.broadcasted_iota(jnp.int32, kp.shape, 1)
        kr_ref[...] = jnp.where(lane < ROPE_DIM, kp, 0.0).astype(BF16)


def _kvproj(lat3, kpe3, pw, tm, v_transposed):
    B, sk, _ = lat3.shape
    hw = MLA_HEADS * V_DIM
    row = lambda n: pl.BlockSpec((None, tm, n), lambda b, i: (b, i, 0))
    if v_transposed:
        vspec, vshape, w_v = pl.BlockSpec((None, hw, tm), lambda b, i: (b, 0, i)), (B, hw, sk), pw['w_vt']
    else:
        vspec, vshape, w_v = row(hw), (B, sk, hw), pw['w_v']
    return pl.pallas_call(
        functools.partial(_kvproj_kernel, v_transposed=v_transposed),
        grid=(B, sk // tm),
        in_specs=[row(KV_RANK), row(kpe3.shape[-1]), _const_spec(pw['w_k'].shape), _const_spec(w_v.shape),
                  _const_spec(pw['knn'].shape)],
        out_specs=[row(MLA_HEADS * NOPE_DIM), row(LANES), vspec],
        out_shape=[jax.ShapeDtypeStruct((B, sk, MLA_HEADS * NOPE_DIM), BF16),
                   jax.ShapeDtypeStruct((B, sk, LANES), BF16), jax.ShapeDtypeStruct(vshape, BF16)],
        compiler_params=_cparams(("parallel", "parallel")),
        name="kvproj",
    )(lat3, kpe3, pw['w_k'], w_v, pw['knn'])


def _last_kblock(qi, tq, tk, P, nk):
    last_key = ((P + qi * tq + tq - 1) // CHUNK) * CHUNK + CHUNK - 1
    return jnp.minimum(last_key // tk, nk - 1)


def _flash_kernel(q_ref, kn_ref, kr_ref, v_ref, o_ref, m_ref, l_ref, acc_ref, *, tq, tk, P, S, nk):
    qi = pl.program_id(1)
    ki = pl.program_id(2)

    @pl.when(ki == 0)
    def _():
        m_ref[...] = jnp.full(m_ref.shape, NEG_BIG, F32)
        l_ref[...] = jnp.zeros(l_ref.shape, F32)
        acc_ref[...] = jnp.zeros(acc_ref.shape, F32)

    @pl.when(ki <= _last_kblock(qi, tq, tk, P, nk))
    def _():
        qpos = P + qi * tq + lax.broadcasted_iota(jnp.int32, (tq, tk), 0)
        kpos = ki * tk + lax.broadcasted_iota(jnp.int32, (tq, tk), 1)
        mask = ((kpos // CHUNK) <= (qpos // CHUNK)) & (kpos < P + S)
        kr = kr_ref[...]
        for h in range(MLA_HEADS):
            qh = q_ref[:, h * Q_SLAB:(h + 1) * Q_SLAB]
            kh = jnp.concatenate([kn_ref[:, h * NOPE_DIM:(h + 1) * NOPE_DIM], kr], axis=1)
            s = jnp.where(mask, _dot_nt(qh, kh), NEG_BIG)
            m_prev = m_ref[h]
            m_new = jnp.maximum(m_prev, jnp.max(s, axis=-1, keepdims=True))
            alpha = jnp.exp(m_prev - m_new)
            p = jnp.exp(s - m_new)
            l_ref[h] = alpha * l_ref[h] + jnp.sum(p, axis=-1, keepdims=True)
            acc_ref[h] = alpha * acc_ref[h] + _dot(p.astype(BF16), v_ref[:, h * V_DIM:(h + 1) * V_DIM])
            m_ref[h] = m_new

    @pl.when(ki == nk - 1)
    def _():
        for h in range(MLA_HEADS):
            o_ref[:, h * V_DIM:(h + 1) * V_DIM] = (acc_ref[h] / l_ref[h]).astype(BF16)


def _flash(q3, kn3, kr3, v3, P, S, tq, tk):
    B = q3.shape[0]
    skp = kn3.shape[1]
    nk = skp // tk
    kmap = lambda b, qi, ki: (b, jnp.minimum(ki, _last_kblock(qi, tq, tk, P, nk)), 0)
    return pl.pallas_call(
        functools.partial(_flash_kernel, tq=tq, tk=tk, P=P, S=S, nk=nk),
        grid=(B, S // tq, nk),
        in_specs=[pl.BlockSpec((None, tq, MLA_HEADS * Q_SLAB), lambda b, qi, ki: (b, qi, 0)),
                  pl.BlockSpec((None, tk, MLA_HEADS * NOPE_DIM), kmap),
                  pl.BlockSpec((None, tk, LANES), kmap),
                  pl.BlockSpec((None, tk, MLA_HEADS * V_DIM), kmap)],
        out_specs=pl.BlockSpec((None, tq, MLA_HEADS * V_DIM), lambda b, qi, ki: (b, qi, 0)),
        out_shape=jax.ShapeDtypeStruct((B, S, MLA_HEADS * V_DIM), BF16),
        scratch_shapes=[pltpu.VMEM((MLA_HEADS, tq, 1), F32), pltpu.VMEM((MLA_HEADS, tq, 1), F32),
                        pltpu.VMEM((MLA_HEADS, tq, V_DIM), F32)],
        compiler_params=_cparams(("parallel", "parallel", "arbitrary")),
        name="mla_attn",
    )(q3, kn3, kr3, v3)


def _attn_dec_kernel(q_ref, lat_ref, kpe_ref, knn_ref, krn_ref, vn_ref, wk_ref, wv_ref, gain_ref, o_ref):
    lb = lat_ref[...].astype(BF16)
    kf = _dot(lb, wk_ref[...])
    vp = _dot(lb, wv_ref[...]).astype(BF16)
    kp = kpe_ref[...]
    krp = jnp.concatenate([kp, jnp.zeros((kp.shape[0], LANES - ROPE_DIM), F32)], axis=1).astype(BF16)
    krn = krn_ref[...]
    hsl = [slice(h * NOPE_DIM, (h + 1) * NOPE_DIM) for h in range(MLA_HEADS)]
    qs = [q_ref[:, h * Q_SLAB:(h + 1) * Q_SLAB] for h in range(MLA_HEADS)]
    knp = [_rms(kf[:, sl], gain_ref[...]).astype(BF16) for sl in hsl]
    sps = [_dot_nt(q, jnp.concatenate([k, krp], axis=1)) for q, k in zip(qs, knp)]
    sns = [_dot_nt(q, jnp.concatenate([knn_ref[:, sl], krn], axis=1)) for q, sl in zip(qs, hsl)]
    for h, (sp, sn) in enumerate(zip(sps, sns)):
        vsl = slice(h * V_DIM, (h + 1) * V_DIM)
        m = jnp.maximum(jnp.max(sp, axis=-1, keepdims=True), jnp.max(sn, axis=-1, keepdims=True))
        pp = jnp.exp(sp - m)
        pn = jnp.exp(sn - m)
        l = jnp.sum(pp, axis=-1, keepdims=True) + jnp.sum(pn, axis=-1, keepdims=True)
        o = _dot(pp.astype(BF16), vp[:, vsl]) + _dot(pn.astype(BF16), vn_ref[:, vsl])
        o_ref[:, vsl] = (o / l).astype(BF16)


def _attn_dec(q3, lat_past, kpe_past, new, pw):
    B, S, _ = q3.shape
    consts = [pw['w_k'], pw['w_v'], pw['knn']]
    specs = [pl.BlockSpec((None, a.shape[1], a.shape[2]), lambda b: (b, 0, 0)) for a in (q3, lat_past, kpe_past, *new)]
    return pl.pallas_call(
        _attn_dec_kernel,
        grid=(B,),
        in_specs=specs + [_const_spec(c.shape) for c in consts],
        out_specs=pl.BlockSpec((None, S, MLA_HEADS * V_DIM), lambda b: (b, 0, 0)),
        out_shape=jax.ShapeDtypeStruct((B, S, MLA_HEADS * V_DIM), BF16),
        compiler_params=_cparams(("parallel",)),
        name="mla_attn_dec",
    )(q3, lat_past, kpe_past, *new, *consts)


def _flash_t_kernel(qt_ref, kt_ref, lt_ref, q_ref, kn_ref, kr_ref, vt_ref, o_ref, m_ref, acc_ref, *, tq, tk, P, S):
    j = pl.program_id(1)
    qi = qt_ref[j]
    ki = kt_ref[j]
    q0 = P + qi * tq
    k0 = ki * tk

    @pl.when(ki == 0)
    def _():
        m_ref[...] = jnp.full(m_ref.shape, NEG_BIG, F32)
        acc_ref[...] = jnp.zeros(acc_ref.shape, F32)

    def step(masked):
        kr = kr_ref[...]
        ones = jnp.ones((ONES_ROWS, tk), BF16)
        if masked:
            kpos = k0 + lax.broadcasted_iota(jnp.int32, (tk, 1), 0)
            qpos = q0 + lax.broadcasted_iota(jnp.int32, (1, tq), 1)
            mask = ((kpos // CHUNK) <= (qpos // CHUNK)) & (kpos < P + S)
        sts = []
        for h in range(MLA_HEADS):
            kh = jnp.concatenate([kn_ref[:, h * NOPE_DIM:(h + 1) * NOPE_DIM], kr], axis=1)
            sts.append(_dot_nt(kh, q_ref[:, h * Q_SLAB:(h + 1) * Q_SLAB]))
        ps, alphas = [], []
        for h in range(MLA_HEADS):
            st = sts[h]
            if masked:
                st = jnp.where(mask, st, NEG_BIG)
            m_prev = m_ref[h]
            m_new = jnp.maximum(m_prev, jnp.max(st, axis=0, keepdims=True))
            alphas.append(jnp.exp(m_prev - m_new))
            ps.append(jnp.exp(st - m_new).astype(BF16))
            m_ref[h] = m_new
        for h in range(MLA_HEADS):
            vt1 = jnp.concatenate([vt_ref[h * V_DIM:(h + 1) * V_DIM, :], ones], axis=0)
            acc_ref[h] = alphas[h] * acc_ref[h] + _dot(vt1, ps[h])

    full = ((k0 + tk - 1) // CHUNK <= q0 // CHUNK) & (k0 + tk <= P + S)
    pl.when(full)(functools.partial(step, False))
    pl.when(jnp.logical_not(full))(functools.partial(step, True))

    @pl.when(lt_ref[j] == 1)
    def _():
        for h in range(MLA_HEADS):
            acc = acc_ref[h]
            o_ref[:, h * V_DIM:(h + 1) * V_DIM] = (acc[:V_DIM] / acc[V_DIM:V_DIM + 1]).T.astype(BF16)


def _flash_t(q3, kn3, kr3, vt3, P, S, tq, tk):
    B = q3.shape[0]
    nk = kn3.shape[1] // tk
    pairs = []
    for qi in range(S // tq):
        last = min((((P + qi * tq + tq - 1) // CHUNK) * CHUNK + CHUNK - 1) // tk, nk - 1)
        pairs += [(qi, ki, int(ki == last)) for ki in range(last + 1)]
    qt, kt, lt = (jnp.asarray(np.array(col, np.int32)) for col in zip(*pairs))
    kmap = lambda b, j, qt, kt, lt: (b, kt[j], 0)
    gs = pltpu.PrefetchScalarGridSpec(
        num_scalar_prefetch=3,
        grid=(B, len(pairs)),
        in_specs=[pl.BlockSpec((None, tq, MLA_HEADS * Q_SLAB), lambda b, j, qt, kt, lt: (b, qt[j], 0)),
                  pl.BlockSpec((None, tk, MLA_HEADS * NOPE_DIM), kmap),
                  pl.BlockSpec((None, tk, LANES), kmap),
                  pl.BlockSpec((None, MLA_HEADS * V_DIM, tk), lambda b, j, qt, kt, lt: (b, 0, kt[j]))],
        out_specs=pl.BlockSpec((None, tq, MLA_HEADS * V_DIM), lambda b, j, qt, kt, lt: (b, qt[j], 0)),
        scratch_shapes=[pltpu.VMEM((MLA_HEADS, 1, tq), F32), pltpu.VMEM((MLA_HEADS, V_DIM + ONES_ROWS, tq), F32)],
    )
    return pl.pallas_call(
        functools.partial(_flash_t_kernel, tq=tq, tk=tk, P=P, S=S),
        grid_spec=gs,
        out_shape=jax.ShapeDtypeStruct((B, S, MLA_HEADS * V_DIM), BF16),
        compiler_params=_cparams(("parallel", "arbitrary")),
        name="mla_attn_t",
    )(qt, kt, lt, q3, kn3, kr3, vt3)


def _memkv_kernel(mem_ref, mnorm_ref, wmk_ref, wmv_ref, mkn_ref, k_ref, v_ref):
    mb = _rms(mem_ref[...], mnorm_ref[...]).astype(BF16)
    kf = _dot(mb, wmk_ref[...])
    for h in range(MEM_HEADS):
        sl = slice(h * MEM_HEAD_DIM, (h + 1) * MEM_HEAD_DIM)
        k_ref[:, sl] = _rms(kf[:, sl], mkn_ref[...])
    v_ref[...] = _dot(mb, wmv_ref[...])


def _memkv(mem2d, mnorm, w_mk, w_mv, mkn, tm):
    T = mem2d.shape[0]
    hw = MEM_HEADS * MEM_HEAD_DIM
    row = lambda n: pl.BlockSpec((tm, n), lambda i: (i, 0))
    return pl.pallas_call(
        _memkv_kernel,
        grid=(T // tm,),
        in_specs=[row(D_MODEL), _const_spec(mnorm.shape), _const_spec(w_mk.shape), _const_spec(w_mv.shape),
                  _const_spec(mkn.shape)],
        out_specs=[row(hw), row(hw)],
        out_shape=[jax.ShapeDtypeStruct((T, hw), F32), jax.ShapeDtypeStruct((T, hw), F32)],
        compiler_params=_cparams(("parallel",)),
        name="mem_kv",
    )(mem2d, mnorm, w_mk, w_mv, mkn)


def _mix_kernel(x_ref, og_ref, om_ref, mk_ref, mv_ref, wout_ref, nmem_ref, wmq_ref, mqn_ref, wmo_ref, nffn_ref,
                wrh_ref, wrl_ref, br_ref, x2_ref, h3_ref, idx_ref, gate_ref, counts_ref, cnt_ref):
    def mem_head(ref, b, h):
        if len(ref.shape) == 3:
            return ref[b, :, h * MEM_HEAD_DIM:(h + 1) * MEM_HEAD_DIM].astype(BF16)
        return ref[b, :, h, :].astype(BF16)

    first = (pl.program_id(0) == 0) & (pl.program_id(1) == 0)

    @pl.when(first)
    def _():
        cnt_ref[...] = jnp.zeros(cnt_ref.shape, F32)

    nbm, tm, _ = x_ref.shape
    rows = nbm * tm
    flat = lambda ref: ref[...].reshape(rows, ref.shape[-1])

    x1 = flat(x_ref) + _dot(flat(og_ref), wout_ref[0:GDN_VW, :]) + _dot(flat(om_ref), wout_ref[GDN_VW:, :])
    hb = _rms(x1, nmem_ref[...]).astype(BF16)
    qm = _dot(hb, wmq_ref[...])
    per_batch = []
    hsl = [slice(h * MEM_HEAD_DIM, (h + 1) * MEM_HEAD_DIM) for h in range(MEM_HEADS)]
    for b in range(nbm):
        br = slice(b * tm, (b + 1) * tm)
        qhs = [(_rms(qm[br, sl], mqn_ref[...]) * (MEM_HEAD_DIM ** -0.5)).astype(BF16) for sl in hsl]
        ss = [_dot_nt(qh, mem_head(mk_ref, b, h)) for h, qh in enumerate(qhs)]
        ps = []
        for s in ss:
            p = jnp.exp(s - jnp.max(s, axis=-1, keepdims=True))
            ps.append((p / jnp.sum(p, axis=-1, keepdims=True)).astype(BF16))
        heads = [_dot(p, mem_head(mv_ref, b, h)).astype(BF16) for h, p in enumerate(ps)]
        per_batch.append(jnp.concatenate(heads, axis=1))
    om = per_batch[0] if nbm == 1 else jnp.concatenate(per_batch, axis=0)
    x2 = x1 + _dot(om, wmo_ref[...])
    x2_ref[...] = x2.reshape(x2_ref.shape)
    h3 = _rms(x2, nffn_ref[...])
    hi = h3.astype(BF16)
    packed = _pack_bf16_pairs(hi.astype(F32))
    for c in range(h3_ref.shape[0]):
        h3_ref[c] = packed[:, c * LANES:(c + 1) * LANES]
    lo = (h3 - hi.astype(F32)).astype(BF16)
    wrh = wrh_ref[...]
    logits = _dot(hi, wrh) + _dot(lo, wrh) + _dot(hi, wrl_ref[...]) + br_ref[...]

    lane = lax.broadcasted_iota(jnp.int32, logits.shape, 1).astype(F32)
    vals, idxs = [], []
    cur = logits
    for _ in range(TOP_K):
        mx = jnp.max(cur, axis=-1, keepdims=True)
        ix = jnp.min(jnp.where(cur == mx, lane, float(LANES)), axis=-1, keepdims=True)
        vals.append(mx)
        idxs.append(ix)
        cur = jnp.where(lane == ix, -3e38, cur)
    es = [jnp.exp(v - vals[0]) for v in vals]
    den = es[0] + es[1] + es[2] + es[3]

    sel = jnp.zeros(logits.shape, F32)
    for k in range(TOP_K):
        sel = sel + jnp.where(lane == idxs[k], 1.0, 0.0)
    ri = lax.broadcasted_iota(jnp.int32, (rows, rows), 0)
    ci = lax.broadcasted_iota(jnp.int32, (rows, rows), 1)
    before = jnp.where(ri > ci, 1.0, 0.0).astype(BF16)
    excl = _dot(before, sel.astype(BF16)) + cnt_ref[...]
    cnt_ref[...] = cnt_ref[...] + jnp.sum(sel, axis=0, keepdims=True)
    counts_ref[...] = cnt_ref[...].astype(jnp.int32)

    idx_out = jnp.zeros(logits.shape, F32)
    gate_out = jnp.zeros(logits.shape, F32)
    for k in range(TOP_K):
        rank = jnp.sum(jnp.where(lane == idxs[k], excl, 0.0), axis=-1, keepdims=True)
        idx_out = jnp.where(lane == float(k), idxs[k], idx_out)
        idx_out = jnp.where(lane == float(TOP_K + k), rank, idx_out)
        gate_out = jnp.where(lane == float(k), es[k] / den, gate_out)
    idx_ref[...] = idx_out.T[:2 * TOP_K, :].astype(jnp.int32)
    gate_ref[...] = gate_out.reshape(gate_ref.shape)


def _mix(x3, og3, om3, mk3, mv3, pw, tm, b0, B):
    S = x3.shape[1]
    hw = MEM_HEADS * MEM_HEAD_DIM
    nbm = _pick(B, tuple(n for n in (8, 4, 2, 1) if n * tm <= MIX_ROWS and b0 % n == 0)) if tm == S else 1
    nsb = S // tm
    boff = b0 // nbm
    tile_in = lambda n: pl.BlockSpec((nbm, tm, n), lambda b, i: (b + boff, i, 0))
    tile = lambda n: pl.BlockSpec((nbm, tm, n), lambda b, i: (b, i, 0))
    if mk3.ndim == 3:
        memspec = pl.BlockSpec((nbm, N_MEM, hw), lambda b, i: (b + boff, 0, 0))
    else:
        memspec = pl.BlockSpec((nbm, N_MEM, MEM_HEADS, MEM_HEAD_DIM), lambda b, i: (b + boff, 0, 0, 0))
    consts = [pw['w_out'], pw['nmem'], pw['w_mq'], pw['mqn'], pw['w_mo'], pw['nffn'], pw['wr_hi'], pw['wr_lo'],
              pw['b_r']]
    return pl.pallas_call(
        _mix_kernel,
        grid=(B // nbm, nsb),
        in_specs=[tile_in(D_MODEL), tile_in(GDN_VW), tile_in(MLA_HEADS * V_DIM), memspec, memspec]
                 + [_const_spec(c.shape) for c in consts],
        out_specs=[tile(D_MODEL), pl.BlockSpec((PK_CHUNKS, nbm * tm, LANES), lambda b, i: (0, b * nsb + i, 0)),
                   pl.BlockSpec((2 * TOP_K, nbm * tm), lambda b, i: (0, b * nsb + i)),
                   tile(LANES), _const_spec((1, LANES))],
        out_shape=[jax.ShapeDtypeStruct((B, S, D_MODEL), F32),
                   jax.ShapeDtypeStruct((PK_CHUNKS, B * S, LANES), jnp.uint32),
                   jax.ShapeDtypeStruct((2 * TOP_K, B * S), jnp.int32), jax.ShapeDtypeStruct((B, S, LANES), F32),
                   jax.ShapeDtypeStruct((1, LANES), jnp.int32)],
        scratch_shapes=[pltpu.VMEM((1, LANES), F32)],
        compiler_params=_cparams(("arbitrary", "arbitrary")),
        name="mix_mem_router",
    )(x3, og3, om3, mk3, mv3, *consts)


def _expert_kernel(be_ref, nx_ref, nv_ref, nu_ref, rows_ref, wgu_hbm, bgu_ref, wd_hbm, bd_ref, y_ref,
                   wgus_ref, wds_ref, wgub_ref, wdb_ref, sem_ref):
    i = pl.program_id(0)
    used = i < nu_ref[0]
    new_expert = (i == 0) | (be_ref[i] != be_ref[jnp.maximum(i - 1, 0)])

    def weight_copies(e):
        return (pltpu.make_async_copy(wgu_hbm.at[e], wgus_ref, sem_ref.at[0]),
                pltpu.make_async_copy(wd_hbm.at[e], wds_ref, sem_ref.at[1]))

    @pl.when(i == 0)
    def _():
        for cp in weight_copies(be_ref[0]):
            cp.start()

    @pl.when(used & new_expert)
    def _():
        for cp in weight_copies(be_ref[i]):
            cp.wait()

        def cast(r, carry):
            rs = pl.ds(pl.multiple_of(r * LANES, LANES), LANES)
            wgub_ref[rs, :] = wgus_ref[rs, :].astype(BF16)
            wdb_ref[rs, :] = wds_ref[rs, :].astype(BF16)
            return carry
        lax.fori_loop(0, D_MODEL // LANES, cast, 0)

        @pl.when(nx_ref[i] >= 0)
        def _():
            for cp in weight_copies(nx_ref[i]):
                cp.start()

    def ffn(nrows):
        packed = jnp.concatenate([rows_ref[c, :nrows, :] for c in range(PK_CHUNKS)], axis=1)
        real = lax.broadcasted_iota(jnp.int32, packed.shape, 0) < nv_ref[i]
        packed = jnp.where(real, packed, jnp.uint32(0))
        x = _unpack_bf16_pairs(packed).astype(BF16)
        acc = None
        for c in range(D_FF // FF_CHUNK):
            gs_ = slice(c * FF_CHUNK, (c + 1) * FF_CHUNK)
            us_ = slice(D_FF + c * FF_CHUNK, D_FF + (c + 1) * FF_CHUNK)
            gt = jnp.minimum(_dot(x, wgub_ref[:, gs_]) + bgu_ref[:, gs_], SWIGLU_LIMIT)
            up = jnp.clip(_dot(x, wgub_ref[:, us_]) + bgu_ref[:, us_], -SWIGLU_LIMIT, SWIGLU_LIMIT)
            act = gt * _sigmoid(SWIGLU_ALPHA * gt) * (up + 1.0)
            part = _dot(act.astype(BF16), wdb_ref[gs_, :])
            acc = part if acc is None else acc + part
        ypk = _pack_bf16_pairs((acc + bd_ref[...]).astype(BF16).astype(F32))
        for c in range(PK_CHUNKS):
            y_ref[c, :nrows, :] = ypk[:, c * LANES:(c + 1) * LANES]
        if nrows < MOE_ROWS:
            y_ref[:, nrows:, :] = jnp.zeros((PK_CHUNKS, MOE_ROWS - nrows, LANES), y_ref.dtype)

    half_full = nv_ref[i] <= MOE_ROWS // 2
    pl.when(used & jnp.logical_not(half_full))(functools.partial(ffn, MOE_ROWS))
    pl.when(used & half_full)(functools.partial(ffn, MOE_ROWS // 2))

    @pl.when(jnp.logical_not(used))
    def _():
        y_ref[...] = jnp.zeros(y_ref.shape, y_ref.dtype)


def _experts(block_e, next_e, block_rows, n_used, rows, w_gu, b_gu, w_down, b_down):
    n_rows = rows.shape[1]
    nb = n_rows // MOE_ROWS
    gs = pltpu.PrefetchScalarGridSpec(
        num_scalar_prefetch=4,
        grid=(nb,),
        in_specs=[pl.BlockSpec((PK_CHUNKS, MOE_ROWS, LANES), lambda i, be, nx, nv, nu: (0, i, 0)),
                  pl.BlockSpec(memory_space=pl.ANY),
                  pl.BlockSpec((None, 1, 2 * D_FF), lambda i, be, nx, nv, nu: (be[i], 0, 0)),
                  pl.BlockSpec(memory_space=pl.ANY),
                  pl.BlockSpec((None, 1, D_MODEL), lambda i, be, nx, nv, nu: (be[i], 0, 0))],
        out_specs=pl.BlockSpec((PK_CHUNKS, MOE_ROWS, LANES), lambda i, be, nx, nv, nu: (0, i, 0)),
        scratch_shapes=[pltpu.VMEM((D_MODEL, 2 * D_FF), F32), pltpu.VMEM((D_FF, D_MODEL), F32),
                        pltpu.VMEM((D_MODEL, 2 * D_FF), BF16), pltpu.VMEM((D_FF, D_MODEL), BF16),
                        pltpu.SemaphoreType.DMA((2,))],
    )
    return pl.pallas_call(
        _expert_kernel,
        grid_spec=gs,
        out_shape=jax.ShapeDtypeStruct((PK_CHUNKS, n_rows, LANES), jnp.uint32),
        compiler_params=_cparams(("arbitrary",)),
        name="moe_experts",
    )(block_e, next_e, block_rows, n_used, rows, w_gu, b_gu, w_down, b_down)


def _sc_mesh():
    return plsc.VectorSubcoreMesh(core_axis_name="core", subcore_axis_name="subcore")


def _sc_scatter_rows(x3s, pos_ts, n_rows):
    C, _, L = x3s[0].shape
    K = pos_ts[0].shape[0]
    ns = len(x3s)

    @functools.partial(pl.kernel, out_type=jax.ShapeDtypeStruct((C, n_rows, L), x3s[0].dtype), mesh=_sc_mesh(),
                       scratch_types=[])
    def scatter(*refs):
        o_hbm = refs[2 * ns]
        for s in range(ns):
            x_hbm, i_hbm = refs[s], refs[ns + s]
            nwin = x3s[s].shape[1] // SC_WINDOW
            for c in range(C):
                def body(x_vmem, i_vmem, c=c):
                    for k in range(K):
                        pltpu.sync_copy(x_vmem, o_hbm.at[c].at[i_vmem.at[k]])

                pltpu.emit_pipeline(
                    body, grid=(nwin,),
                    in_specs=[pl.BlockSpec((SC_WINDOW, L), lambda i, c=c, nwin=nwin: (c * nwin + i, 0)),
                              pl.BlockSpec((K, SC_WINDOW), lambda i: (0, i))],
                    out_specs=[], core_axis_name=("core", "subcore"), dimension_semantics=(pltpu.PARALLEL,),
                )(x_hbm, i_hbm)

    return scatter(*[x.reshape(-1, L) for x in x3s], *pos_ts)


def _sc_gather_rows(table3, idxs):
    C, _, L = table3.shape
    ns = len(idxs)
    out_type = [jax.ShapeDtypeStruct((C * i.shape[0], L), table3.dtype) for i in idxs]

    @functools.partial(pl.kernel, out_type=out_type, mesh=_sc_mesh(), scratch_types=[])
    def gather(t_hbm, *refs):
        for s in range(ns):
            i_hbm, o_hbm = refs[s], refs[ns + s]
            nwin = idxs[s].shape[0] // SC_WINDOW
            for c in range(C):
                def body(i_vmem, o_vmem, c=c):
                    pltpu.sync_copy(t_hbm.at[c].at[i_vmem.at[0]], o_vmem)

                pltpu.emit_pipeline(
                    body, grid=(nwin,),
                    in_specs=[pl.BlockSpec((1, SC_WINDOW), lambda i: (0, i))],
                    out_specs=[pl.BlockSpec((SC_WINDOW, L), lambda i, c=c, nwin=nwin: (c * nwin + i, 0))],
                    core_axis_name=("core", "subcore"), dimension_semantics=(pltpu.PARALLEL,),
                )(i_hbm, o_hbm)

    outs = gather(table3, *[i.reshape(1, -1) for i in idxs])
    return [o.reshape(C, -1, L) for o in outs]


def _combine_kernel(x2_ref, g_ref, gate_ref, *rest):
    o_ref = rest[-1]
    gate = gate_ref[...]
    half = D_MODEL // 2
    for c in range(PK_CHUNKS):
        lo_s = slice(c * LANES, (c + 1) * LANES)
        hi_s = slice(half + c * LANES, half + (c + 1) * LANES)
        acc_lo = x2_ref[:, lo_s]
        acc_hi = x2_ref[:, hi_s]
        for k in range(TOP_K):
            w = g_ref[c, k]
            gk = gate[:, k:k + 1]
            acc_lo = acc_lo + pltpu.bitcast(w << 16, F32) * gk
            acc_hi = acc_hi + pltpu.bitcast(w & jnp.uint32(0xFFFF0000), F32) * gk
        o_ref[:, lo_s] = acc_lo
        o_ref[:, hi_s] = acc_hi


def _combine(x2, g4, gate, tm, out_buf=None, row0=0, t_total=None):
    T = x2.shape[0]
    t_total = T if t_total is None else t_total
    blk0 = row0 // tm
    in_specs = [pl.BlockSpec((tm, D_MODEL), lambda i: (i, 0)),
                pl.BlockSpec((PK_CHUNKS, TOP_K, tm, LANES), lambda i: (0, 0, i, 0)),
                pl.BlockSpec((tm, LANES), lambda i: (i, 0))]
    args = [x2, g4, gate]
    aliases = {}
    if out_buf is not None:
        in_specs.append(pl.BlockSpec(memory_space=pl.ANY))
        args.append(out_buf)
        aliases = {3: 0}
    return pl.pallas_call(
        _combine_kernel,
        grid=(T // tm,),
        in_specs=in_specs,
        out_specs=pl.BlockSpec((tm, D_MODEL), lambda i: (i + blk0, 0)),
        out_shape=jax.ShapeDtypeStruct((t_total, D_MODEL), F32),
        input_output_aliases=aliases,
        compiler_params=_cparams(("parallel",)),
        name="moe_combine",
    )(*args)


def _plan_kernel(idxr_ref, base_ref, pos_ref):
    rows = 2 * TOP_K
    base = jnp.broadcast_to(base_ref[...], (rows, LANES))
    is_id = lax.broadcasted_iota(jnp.int32, (rows, LANES), 0) < TOP_K
    for j in range(idxr_ref.shape[1] // LANES):
        cols = slice(j * LANES, (j + 1) * LANES)
        blk = idxr_ref[:, cols]
        start = jnp.take_along_axis(base, jnp.where(is_id, blk, 0), axis=1)
        pos_ref[:, cols] = (start + pltpu.roll(blk, TOP_K, 0))[:TOP_K, :]


def _plan(idxr, base):
    T = idxr.shape[1]
    tm = _pick(T, (2048, 1024, 512, 256, 128))
    return pl.pallas_call(
        _plan_kernel,
        grid=(T // tm,),
        in_specs=[pl.BlockSpec((2 * TOP_K, tm), lambda i: (0, i)), _const_spec((1, LANES))],
        out_specs=pl.BlockSpec((TOP_K, tm), lambda i: (0, i)),
        out_shape=jax.ShapeDtypeStruct((TOP_K, T), jnp.int32),
        compiler_params=_cparams(("parallel",)),
        name="moe_plan",
    )(idxr, base)


def _moe(streams, places, ew):
    cnts = [st[3][0, :N_EXPERTS] for st in streams]
    total = sum(cnts)
    padded = (total + MOE_ROWS - 1) // MOE_ROWS * MOE_ROWS
    pad_end = jnp.cumsum(padded)
    pad_start = pad_end - padded
    pos_ts = []
    base = pad_start
    for (h3p, idxr, gate, counts, x2), cnt in zip(streams, cnts):
        pos_ts.append(_plan(idxr, _pad_lanes(base)))
        base = base + cnt
    n_assign = sum(st[0].shape[1] for st in streams) * TOP_K
    nb = -(-n_assign // MOE_ROWS) + N_EXPERTS
    starts = jnp.arange(nb, dtype=jnp.int32) * MOE_ROWS
    block_e = jnp.minimum(jnp.sum((pad_end[None, :] <= starts[:, None]).astype(jnp.int32), axis=1), N_EXPERTS - 1)
    n_used = (pad_end[-1] // MOE_ROWS).astype(jnp.int32).reshape(1)
    ar = jnp.arange(N_EXPERTS, dtype=jnp.int32)
    later = (padded > 0)[None, :] & (ar[None, :] > ar[:, None])
    nxt = jnp.min(jnp.where(later, ar[None, :], N_EXPERTS), axis=1)
    nxt = jnp.where(nxt >= N_EXPERTS, -1, nxt)
    mine = block_e[:, None] == ar[None, :]
    next_e = jnp.sum(jnp.where(mine, nxt[None, :], 0), axis=1).astype(jnp.int32)
    seg_end = jnp.sum(jnp.where(mine, (pad_start + total)[None, :], 0), axis=1)
    block_rows = jnp.clip(seg_end - starts, 0, MOE_ROWS).astype(jnp.int32)
    rows = _sc_scatter_rows([st[0] for st in streams], pos_ts, nb * MOE_ROWS)
    y_rows = _experts(block_e, next_e, block_rows, n_used, rows, ew['w_gu'], ew['b_gu'], ew['w_down'], ew['b_down'])
    gs = _sc_gather_rows(y_rows, [p.reshape(-1) for p in pos_ts])
    outs = []
    for (h3p, idxr, gate, counts, x2), g, place in zip(streams, gs, places):
        T = x2.shape[0]
        outs.append(_combine(x2, g.reshape(PK_CHUNKS, TOP_K, T, LANES), gate, _pick(T, (512, 256, 128, 64)), *place))
    return outs


def _pad_lanes(v, n=LANES, fill=0.0):
    return jnp.pad(v, (0, n - v.shape[0]), constant_values=fill).reshape(1, n)


def _prep_weights(norm_mix, w_in, conv_w, a_log, dt_bias, gdn_norm, q_a_norm, w_qb, kv_a_norm, w_kvb, q_norm,
                  k_nope_norm, k_rope_norm, w_out, norm_mem, mem_norm, w_mq, w_mk, w_mv, mq_norm, mk_norm, w_mo,
                  norm_ffn, w_router, b_router, w_gu, b_gu, w_down, b_down):
    c = np.cumsum([CONV_DIM, GDN_VW, GDN_HEADS, GDN_HEADS, Q_RANK, KV_RANK])
    w_u, w_z, w_a, w_b, w_cq, w_ckv, w_kpe = [w_in[:, lo:hi] for lo, hi in
                                              zip([0, *c], [*c, w_in.shape[1]])]
    w_s = jnp.concatenate([w_kpe, w_a, w_b], axis=1)
    w_s = jnp.pad(w_s, ((0, 0), (0, LANES - w_s.shape[1])))
    wq = w_qb.reshape(Q_RANK, MLA_HEADS, QK_DIM)
    wq = jnp.pad(wq, ((0, 0), (0, 0), (0, Q_SLAB - QK_DIM))).reshape(Q_RANK, MLA_HEADS * Q_SLAB)
    wkv = w_kvb.reshape(KV_RANK, MLA_HEADS, NOPE_DIM + V_DIM)
    w_k = wkv[:, :, :NOPE_DIM].reshape(KV_RANK, -1).astype(BF16)
    w_v = wkv[:, :, NOPE_DIM:].reshape(KV_RANK, -1).astype(BF16)
    wr = jnp.pad(w_router, ((0, 0), (0, LANES - N_EXPERTS)))
    wr_hi = wr.astype(BF16)
    wr_lo = (wr - wr_hi.astype(F32)).astype(BF16)
    row = lambda v: v.reshape(1, -1)
    gpad = ROPE_DIM
    pw = dict(
        nmix=row(norm_mix), w_u=w_u.astype(BF16), w_z=w_z.astype(BF16), w_cq=w_cq.astype(BF16),
        w_ckv=w_ckv.astype(BF16), w_s=w_s.astype(BF16), qan=row(q_a_norm), w_qb=wq.astype(BF16),
        qn=_pad_lanes(q_norm, Q_SLAB), kvan=row(kv_a_norm), krn=_pad_lanes(k_rope_norm),
        alog=jnp.pad(a_log, (gpad, LANES - gpad - GDN_HEADS)).reshape(1, LANES),
        dtb=jnp.pad(dt_bias, (gpad, LANES - gpad - GDN_HEADS)).reshape(1, LANES),
        conv_w=conv_w, gnorm=row(gdn_norm), w_k=w_k, w_v=w_v, w_vt=w_v.T, knn=row(k_nope_norm),
        w_out=w_out.astype(BF16), nmem=row(norm_mem), w_mq=w_mq.astype(BF16), mqn=row(mq_norm),
        w_mo=w_mo.astype(BF16), nffn=row(norm_ffn), wr_hi=wr_hi, wr_lo=wr_lo,
        b_r=_pad_lanes(b_router, LANES, NEG_BIG),
        mnorm=row(mem_norm), w_mk=w_mk.astype(BF16), w_mv=w_mv.astype(BF16), mkn=row(mk_norm),
    )
    ew = dict(w_gu=w_gu, b_gu=b_gu.reshape(N_EXPERTS, 1, 2 * D_FF), w_down=w_down,
              b_down=b_down.reshape(N_EXPERTS, 1, D_MODEL))
    return pw, ew


def _rope_tables(P, S):
    half = ROPE_DIM // 2
    inv = ROPE_THETA ** (-jnp.arange(half, dtype=F32) / half)
    ang = (P + jnp.arange(S, dtype=jnp.int32)).astype(F32)[:, None] * inv[None, :]
    cos, sin = jnp.cos(ang), jnp.sin(ang)
    zh = jnp.zeros((S, half), F32)
    zz = jnp.zeros((S, LANES - ROPE_DIM), F32)
    return (jnp.concatenate([cos, cos, zz], 1), jnp.concatenate([-sin, zh, zz], 1),
            jnp.concatenate([zh, sin, zz], 1))


def _pick(n, prefs):
    for t in prefs:
        if n % t == 0:
            return t
    return n


def _trunk_front(x, lat_past, kpe_past, s0, conv_past, mem_k, mem_v, pw, n_groups):
    B, S, D = x.shape
    P = lat_past.shape[1]
    T = B * S
    tm = _pick(S, (512, 256, 128, 64))
    tm_in = 512 if (T % 512 == 0 and (512 % S == 0 or S % 512 == 0)) else tm
    u, z, q, lat_new, small = _inproj(x.reshape(T, D), S, tm_in, pw, _rope_tables(P, S))

    LC = _pick(S, (256, 128, 64))
    NB = _pick(B, tuple(n for n in (8, 4, 2, 1) if n * (LC // CHUNK) <= GDN_UNITS))
    o_gdn, s_new, conv_new = _gdn(u.reshape(B, S, CONV_DIM), small.reshape(B, S, LANES), z.reshape(B, S, GDN_VW),
                                  pw['conv_w'], conv_past, s0, pw['gnorm'], NB, LC)

    kpe_new = small[:, :ROPE_DIM].reshape(B, S, ROPE_DIM)
    lat3 = lat_new.reshape(B, S, KV_RANK)
    small3 = small.reshape(B, S, LANES)
    q3 = q.reshape(B, S, MLA_HEADS * Q_SLAB)
    tq = _pick(S, (512, 256, 128, 64))
    if P == 0 and S % 512 == 0:
        kn, kr, vt = _kvproj(lat3, small3, pw, _pick(S, (2048, 1024, 512)), True)
        o_mla = _flash_t(q3, kn, kr, vt, P, S, tq, 512)
    elif P > 0 and P % CHUNK == 0 and S == CHUNK:
        new = _kvproj(lat3, small3, pw, S, False)
        o_mla = _attn_dec(q3, lat_past, kpe_past, new, pw)
    else:
        sk = P + S
        tk = 512 if S >= 512 else -(-sk // LANES) * LANES
        skp = -(-sk // tk) * tk
        lat_all = jnp.pad(jnp.concatenate([lat_past, lat3], axis=1), ((0, 0), (0, skp - sk), (0, 0)))
        kpe_all = jnp.pad(jnp.concatenate([kpe_past, kpe_new], axis=1), ((0, 0), (0, skp - sk), (0, 0)))
        key_major = tq >= LANES
        kn, kr, v = _kvproj(lat_all, kpe_all, pw, _pick(skp, (512,)), key_major)
        o_mla = (_flash_t if key_major else _flash)(q3, kn, kr, v, P, S, tq, tk)

    streams = []
    gb = B // n_groups
    for gi in range(n_groups):
        x2, h3p, idxr, gate, counts = _mix(x, o_gdn, o_mla, mem_k, mem_v, pw, tm, gi * gb, gb)
        tg = gb * S
        streams.append((h3p, idxr, gate.reshape(tg, LANES), counts, x2.reshape(tg, D)))
    return streams, (lat_new.reshape(B, S, KV_RANK), kpe_new, s_new, conv_new)


def kernel(x_prompt, x_sample, cache_kv_latent, cache_k_rope, state_gdn, state_conv, cache_mem_k, cache_mem_v, mem_prompt, norm_mix, w_in, conv_w, a_log, dt_bias, gdn_norm, q_a_norm, w_qb, kv_a_norm, w_kvb, q_norm, k_nope_norm, k_rope_norm, w_out, norm_mem, mem_norm, w_mq, w_mk, w_mv, mq_norm, mk_norm, w_mo, norm_ffn, w_router, b_router, w_gu, b_gu, w_down, b_down):
    depth = norm_mix.shape[0]
    yp, ys = x_prompt, x_sample
    bp = x_prompt.shape[0]
    hw = MEM_HEADS * MEM_HEAD_DIM
    outs = [[] for _ in range(10)]
    for l in range(depth):
        pw, ew = _prep_weights(norm_mix[l], w_in[l], conv_w[l], a_log[l], dt_bias[l], gdn_norm[l], q_a_norm[l],
                               w_qb[l], kv_a_norm[l], w_kvb[l], q_norm[l], k_nope_norm[l], k_rope_norm[l], w_out[l],
                               norm_mem[l], mem_norm[l], w_mq[l], w_mk[l], w_mv[l], mq_norm[l], mk_norm[l], w_mo[l],
                               norm_ffn[l], w_router[l], b_router[l], w_gu[l], b_gu[l], w_down[l], b_down[l])
        nm = mem_prompt.shape[1]
        mk, mv = _memkv(mem_prompt.reshape(bp * nm, D_MODEL), pw['mnorm'], pw['w_mk'], pw['w_mv'], pw['mkn'],
                        _pick(bp * nm, (512, 256)))
        mk = mk.reshape(bp, nm, hw)
        mv = mv.reshape(bp, nm, hw)
        n_groups = 2 if bp % 2 == 0 else 1
        streams_p, (lat, kpe, s_fin, cv) = _trunk_front(
            yp, jnp.zeros((bp, 0, KV_RANK), F32), jnp.zeros((bp, 0, ROPE_DIM), F32),
            jnp.zeros((bp, GDN_HEADS, GDN_DK, GDN_DV), F32), jnp.zeros((bp, CONV_W - 1, CONV_DIM), F32), mk, mv, pw,
            n_groups)
        bs = x_sample.shape[0]
        (stream_s,), (lat2, kpe2, s_fin2, cv2) = _trunk_front(
            ys, cache_kv_latent[l], cache_k_rope[l], state_gdn[l], state_conv[l],
            cache_mem_k[l], cache_mem_v[l], pw, 1)
        tp = yp.shape[0] * yp.shape[1]
        tg = tp // n_groups
        ybuf = None
        for gi, st in enumerate(streams_p):
            last = gi == n_groups - 1
            res = _moe([st] + ([stream_s] if last else []),
                       [(ybuf, gi * tg, tp)] + ([(None, 0, None)] if last else []), ew)
            ybuf = res[0]
        yp, ys = ybuf.reshape(yp.shape), res[1].reshape(ys.shape)
        for lst, val in zip(outs, (lat, kpe, s_fin, cv, mk.reshape(bp, nm, MEM_HEADS, MEM_HEAD_DIM),
                                   mv.reshape(bp, nm, MEM_HEADS, MEM_HEAD_DIM), lat2, kpe2, s_fin2, cv2)):
            lst.append(val)
    return (yp, ys) + tuple(jnp.stack(o) for o in outs)
```

```python
import functools
import math

import numpy as np
import jax
import jax.numpy as jnp
from jax import lax
from jax.experimental import pallas as pl
from jax.experimental.pallas import tpu as pltpu
from jax.experimental.pallas import tpu_sc as plsc

F32 = jnp.float32
BF16 = jnp.bfloat16

D_MODEL = 1024
CHUNK = 64
EPS = 1e-6
GDN_HEADS = 4
GDN_DK = 128
GDN_DV = 128
CONV_W = 4
GDN_QK = GDN_HEADS * GDN_DK
GDN_VW = GDN_HEADS * GDN_DV
CONV_DIM = 2 * GDN_QK + GDN_VW
MLA_HEADS = 4
Q_RANK = 384
KV_RANK = 256
NOPE_DIM = 128
ROPE_DIM = 64
V_DIM = 128
QK_DIM = NOPE_DIM + ROPE_DIM
ROPE_THETA = 10000.0
N_MEM = 256
MEM_HEADS = 4
MEM_HEAD_DIM = 128
N_EXPERTS = 32
TOP_K = 4
D_FF = D_MODEL
SWIGLU_ALPHA = 1.702
SWIGLU_LIMIT = 7.0

LANES = 128
Q_SLAB = 2 * LANES
NEG_BIG = -1e30
VMEM_LIMIT = 56 * 1024 * 1024
MOE_ROWS = 512
GDN_UNITS = 8
GDN_GROUP = 2
FF_CHUNK = 512
SC_WINDOW = 128
PK_CHUNKS = D_MODEL // 2 // LANES
SUBTILE_ROWS = 256
ONES_ROWS = 16
MIX_ROWS = 512


def _cparams(sem):
    return pltpu.CompilerParams(dimension_semantics=sem, vmem_limit_bytes=VMEM_LIMIT)


def _dot(a, b):
    return jnp.dot(a, b, preferred_element_type=F32)


def _dot_nt(a, b):
    return lax.dot_general(a, b, (((1,), (1,)), ((), ())), preferred_element_type=F32)


def _dot_tn(a, b):
    return lax.dot_general(a, b, (((0,), (0,)), ((), ())), preferred_element_type=F32)


def _rms(x, gain, n=None):
    n = x.shape[-1] if n is None else n
    ss = jnp.sum(x * x, axis=-1, keepdims=True) * (1.0 / n)
    return (x * lax.rsqrt(ss + EPS)) * gain


def _sigmoid(x):
    return 1.0 / (1.0 + jnp.exp(-x))


def _rope128(r, cos, sna, snb):
    return r * cos + pltpu.roll(r, 96, 1) * sna + pltpu.roll(r, 32, 1) * snb


def _pack_bf16_pairs(x):
    n = x.shape[1] // 2
    lo = pltpu.bitcast(x[:, :n], jnp.uint32) >> 16
    hi = pltpu.bitcast(x[:, n:], jnp.uint32) & jnp.uint32(0xFFFF0000)
    return lo | hi


def _unpack_bf16_pairs(p):
    lo = pltpu.bitcast(p << 16, F32)
    hi = pltpu.bitcast(p & jnp.uint32(0xFFFF0000), F32)
    return jnp.concatenate([lo, hi], axis=1)


def _subtiles(rows):
    n = rows // SUBTILE_ROWS if rows % SUBTILE_ROWS == 0 else 1
    step = rows // n
    return [slice(i * step, (i + 1) * step) for i in range(n)]


def _const_spec(shape):
    nd = len(shape)
    return pl.BlockSpec(shape, lambda *_: (0,) * nd)


def _inproj_kernel(x_ref, nmix_ref, wu_ref, wz_ref, wcq_ref, wckv_ref, ws_ref, qan_ref, wqb_ref, qn_ref,
                   kvan_ref, krn_ref, alog_ref, dtb_ref, cos_ref, sna_ref, snb_ref,
                   u_ref, z_ref, q_ref, lat_ref, small_ref):
    for rs in _subtiles(x_ref.shape[0]):
        x = x_ref[rs, :]
        hb = _rms(x, nmix_ref[...]).astype(BF16)
        u_ref[rs, :] = _dot(hb, wu_ref[...])
        z_ref[rs, :] = _dot(hb, wz_ref[...])
        cq_raw = _dot(hb, wcq_ref[...])
        ckv_raw = _dot(hb, wckv_ref[...])
        sm = _dot(hb, ws_ref[...])
        cos, sna, snb = cos_ref[rs, :], sna_ref[rs, :], snb_ref[rs, :]

        cq = _rms(cq_raw, qan_ref[...]).astype(BF16)
        qf = _dot(cq, wqb_ref[...])
        scale = QK_DIM ** -0.5
        for h in range(MLA_HEADS):
            slab = qf[:, h * Q_SLAB:(h + 1) * Q_SLAB]
            slab = _rms(slab, qn_ref[...], n=QK_DIM)
            nope = slab[:, :LANES]
            ropd = _rope128(slab[:, LANES:], cos, sna, snb)
            q_ref[rs, h * Q_SLAB:h * Q_SLAB + LANES] = (nope * scale).astype(BF16)
            q_ref[rs, h * Q_SLAB + LANES:(h + 1) * Q_SLAB] = (ropd * scale).astype(BF16)

        lat_ref[rs, :] = _rms(ckv_raw, kvan_ref[...])

        lane = lax.broadcasted_iota(jnp.int32, sm.shape, 1)
        kp = jnp.where(lane < ROPE_DIM, sm, 0.0)
        kpe = _rope128(_rms(kp, krn_ref[...], n=ROPE_DIM), cos, sna, snb)
        sp = sm + dtb_ref[...]
        softplus = jnp.maximum(sp, 0.0) + jnp.log1p(jnp.exp(-jnp.abs(sp)))
        g = -jnp.exp(alog_ref[...]) * softplus
        beta = _sigmoid(sm)
        small_ref[rs, :] = jnp.where(lane < ROPE_DIM, kpe,
                                     jnp.where(lane < ROPE_DIM + GDN_HEADS, g,
                                               jnp.where(lane < ROPE_DIM + 2 * GDN_HEADS, beta, 0.0)))


def _inproj(x2d, S, tm, pw, tabs):
    T = x2d.shape[0]
    if tm > S:
        tabs = [jnp.tile(t, (tm // S, 1)) for t in tabs]
    nblk_s = max(S // tm, 1)
    row = lambda n: pl.BlockSpec((tm, n), lambda i: (i, 0))
    tab = pl.BlockSpec((tm, LANES), lambda i: (i % nblk_s, 0))
    consts = [pw['nmix'], pw['w_u'], pw['w_z'], pw['w_cq'], pw['w_ckv'], pw['w_s'], pw['qan'], pw['w_qb'],
              pw['qn'], pw['kvan'], pw['krn'], pw['alog'], pw['dtb']]
    return pl.pallas_call(
        _inproj_kernel,
        grid=(T // tm,),
        in_specs=[row(D_MODEL)] + [_const_spec(c.shape) for c in consts] + [tab, tab, tab],
        out_specs=[row(CONV_DIM), row(GDN_VW), row(MLA_HEADS * Q_SLAB), row(KV_RANK), row(LANES)],
        out_shape=[jax.ShapeDtypeStruct((T, CONV_DIM), F32), jax.ShapeDtypeStruct((T, GDN_VW), F32),
                   jax.ShapeDtypeStruct((T, MLA_HEADS * Q_SLAB), BF16), jax.ShapeDtypeStruct((T, KV_RANK), F32),
                   jax.ShapeDtypeStruct((T, LANES), F32)],
        compiler_params=_cparams(("parallel",)),
        name="inproj",
    )(x2d, *consts, *tabs)


def _split3(x):
    hi = x.astype(BF16)
    r = x - hi.astype(F32)
    mid = r.astype(BF16)
    lo = (r - mid.astype(F32)).astype(BF16)
    return hi, mid, lo


def _gdn_kernel(u_ref, small_ref, z_ref, convw_ref, cpast_ref, s0_ref, gnorm_ref,
                to_ref, trilm_ref, strictm_ref, same2_ref, lvl_ref,
                o_ref, sfin_ref, cnew_ref, ext_ref, uc_ref, state_ref,
                qf_ref, kf_ref, vb_ref, bt_ref, gcum_ref, glast_ref, kk_ref, qk_ref, mb_ref, x_ref, qkb_ref,
                kbe_ref, qg_ref, kdec_ref, egl_ref, t1_ref, uu_ref, ww_ref, vn_ref, qs_ref, *, NB, LC):
    j = pl.program_id(1)
    nj = pl.num_programs(1)
    PADR = 8
    C = LC // CHUNK
    NG = GDN_HEADS // GDN_GROUP
    U = NB * C * NG
    HR = GDN_GROUP * CHUNK

    def unit(nb, c, g):
        return (nb * C + c) * NG + g

    def head_rows(h):
        return slice((h % GDN_GROUP) * CHUNK, (h % GDN_GROUP + 1) * CHUNK)

    @pl.when(j == 0)
    def _():
        state_ref[...] = s0_ref[...]
        ext_ref[:, PADR - (CONV_W - 1):PADR, :] = cpast_ref[...]

    w = convw_ref[...]
    for nb in range(NB):
        ext_ref[nb, PADR:PADR + LC, :] = u_ref[nb]
        acc = ext_ref[nb, PADR:PADR + LC, :] * w[CONV_W - 1:CONV_W, :]
        for t in range(1, CONV_W):
            acc = acc + ext_ref[nb, PADR - t:PADR - t + LC, :] * w[CONV_W - 1 - t:CONV_W - t, :]
        uc_ref[nb] = acc * _sigmoid(acc)
        ext_ref[nb, 0:PADR, :] = ext_ref[nb, LC:LC + PADR, :]

    @pl.when(j == nj - 1)
    def _():
        cnew_ref[...] = ext_ref[:, PADR - (CONV_W - 1):PADR, :]

    g0 = ROPE_DIM
    b0 = ROPE_DIM + GDN_HEADS
    to = to_ref[...]
    for nb, c in [(nb, c) for nb in range(NB) for c in range(C)]:
        rows = slice(c * CHUNK, (c + 1) * CHUNK)
        sm = small_ref[nb, rows, :]
        gl = sum(_dot(to, part) for part in _split3(sm))
        for h in range(GDN_HEADS):
            u = unit(nb, c, h // GDN_GROUP)
            hr = head_rows(h)
            q = uc_ref[nb, rows, h * GDN_DK:(h + 1) * GDN_DK]
            k = uc_ref[nb, rows, GDN_QK + h * GDN_DK:GDN_QK + (h + 1) * GDN_DK]
            v = uc_ref[nb, rows, 2 * GDN_QK + h * GDN_DV:2 * GDN_QK + (h + 1) * GDN_DV]
            beta = jnp.broadcast_to(sm[:, b0 + h:b0 + h + 1], (CHUNK, LANES))
            qf_ref[u, hr, :] = (q * lax.rsqrt(jnp.sum(q * q, -1, keepdims=True) + EPS)) * (GDN_DK ** -0.5)
            kf_ref[u, hr, :] = k * lax.rsqrt(jnp.sum(k * k, -1, keepdims=True) + EPS)
            vb_ref[u, hr, :] = (v * beta).astype(BF16)
            bt_ref[u, hr, :] = beta
            gcum_ref[u, hr, :] = jnp.broadcast_to(gl[:CHUNK, g0 + h:g0 + h + 1], (CHUNK, LANES))
            glast_ref[u, hr, :] = jnp.broadcast_to(gl[CHUNK:, g0 + h:g0 + h + 1], (CHUNK, LANES))

    for u in range(U):
        k = kf_ref[u]
        kbf = k.astype(BF16)
        kk_ref[u] = _dot_nt((k * bt_ref[u]).astype(BF16), kbf)
        qk_ref[u] = _dot_nt(qf_ref[u].astype(BF16), kbf)

    trilm = trilm_ref[...]
    eye = trilm - strictm_ref[...]
    for u in range(U):
        gcum = gcum_ref[u]
        grow = gcum.T[0:1, :]
        gcol = gcum if HR == LANES else jnp.concatenate([gcum] * (HR // LANES), axis=1)
        decay = jnp.exp(jnp.minimum(gcol - grow, 0.0)) * trilm
        m = kk_ref[u] * (decay * strictm_ref[...])
        mb_ref[u] = m.astype(BF16)
        x_ref[u] = eye - m * same2_ref[...]
        qkb_ref[u] = (qk_ref[u] * decay).astype(BF16)
        egc = jnp.exp(gcum)
        k = kf_ref[u]
        kbe_ref[u] = (k * bt_ref[u] * egc).astype(BF16)
        qg_ref[u] = (qf_ref[u] * egc).astype(BF16)
        kdec_ref[u] = (k * jnp.exp(glast_ref[u] - gcum)).astype(BF16)
        egl_ref[u] = jnp.exp(glast_ref[u])

    for lvl in range(lvl_ref.shape[0]):
        lm = lvl_ref[lvl]
        for u in range(U):
            t1_ref[u] = _dot(mb_ref[u] * lm, x_ref[u].astype(BF16)).astype(BF16)
        for u in range(U):
            x = x_ref[u]
            x_ref[u] = x - _dot(x.astype(BF16), t1_ref[u])

    for u in range(U):
        xb = x_ref[u].astype(BF16)
        uu_ref[u] = _dot(xb, vb_ref[u])
        ww_ref[u] = _dot(xb, kbe_ref[u]).astype(BF16)

    gnorm = gnorm_ref[...]
    for c in range(C):
        rows = slice(c * CHUNK, (c + 1) * CHUNK)
        heads = [(nb, h, unit(nb, c, h // GDN_GROUP), head_rows(h)) for nb in range(NB) for h in range(GDN_HEADS)]
        groups = [(nb, g, unit(nb, c, g)) for nb in range(NB) for g in range(NG)]
        for nb, h, u, hr in heads:
            stb = state_ref[nb, h].astype(BF16)
            r = _dot(jnp.concatenate([ww_ref[u, hr, :], qg_ref[u, hr, :]], axis=0), stb)
            vn_ref[u, hr, :] = (uu_ref[u, hr, :] - r[:CHUNK]).astype(BF16)
            qs_ref[u, hr, :] = r[CHUNK:]
        outs = {(nb, g): qs_ref[u] + _dot(qkb_ref[u], vn_ref[u]) for nb, g, u in groups}
        for nb, h, u, hr in heads:
            state_ref[nb, h] = (state_ref[nb, h] * egl_ref[u, hr.start:hr.start + 1, :]
                                + _dot_tn(kdec_ref[u, hr, :], vn_ref[u, hr, :]))
        for nb, h, u, hr in heads:
            zz = z_ref[nb, rows, h * GDN_DV:(h + 1) * GDN_DV]
            og = _rms(outs[(nb, h // GDN_GROUP)][hr, :], gnorm) * (zz * _sigmoid(zz))
            o_ref[nb, rows, h * GDN_DV:(h + 1) * GDN_DV] = og.astype(BF16)

    @pl.when(j == nj - 1)
    def _():
        sfin_ref[...] = state_ref[...]


def _gdn_masks():
    hr = GDN_GROUP * CHUNK
    i = np.arange(hr)[:, None]
    j = np.arange(hr)[None, :]
    same_head = (i // CHUNK) == (j // CHUNK)
    tril = same_head & (i >= j)
    strict = same_head & (i > j)
    same2 = strict & ((i // 2) == (j // 2))
    lvls = []
    blk = 2
    while blk < CHUNK:
        lvls.append(strict & ((i // (2 * blk)) == (j // (2 * blk))) & ((i // blk) != (j // blk)))
        blk *= 2
    fr = np.arange(CHUNK)
    to = np.concatenate([fr[:, None] >= fr[None, :], np.ones((CHUNK, CHUNK), bool)], axis=0)
    f = lambda a: jnp.asarray(a.astype(np.float32))
    return (jnp.asarray(to.astype(np.float32), dtype=BF16), f(tril), f(strict), f(same2),
            jnp.asarray(np.stack(lvls).astype(np.float32), dtype=BF16))


def _gdn(u3, small3, z3, conv_w, conv_past, s0, gnorm, NB, LC):
    B, S, _ = u3.shape
    C = LC // CHUNK
    U = NB * C * (GDN_HEADS // GDN_GROUP)
    HR = GDN_GROUP * CHUNK
    masks = _gdn_masks()
    tile = lambda n: pl.BlockSpec((NB, LC, n), lambda b, j: (b, j, 0))
    stspec = pl.BlockSpec((NB, GDN_HEADS, GDN_DK, GDN_DV), lambda b, j: (b, 0, 0, 0))
    cvspec = pl.BlockSpec((NB, CONV_W - 1, CONV_DIM), lambda b, j: (b, 0, 0))
    vm = lambda shape, dt: pltpu.VMEM(shape, dt)
    return pl.pallas_call(
        functools.partial(_gdn_kernel, NB=NB, LC=LC),
        grid=(B // NB, S // LC),
        in_specs=[tile(CONV_DIM), tile(LANES), tile(GDN_VW), _const_spec(conv_w.shape), cvspec, stspec,
                  _const_spec(gnorm.shape)] + [_const_spec(m.shape) for m in masks],
        out_specs=[tile(GDN_VW), stspec, cvspec],
        out_shape=[jax.ShapeDtypeStruct((B, S, GDN_VW), BF16),
                   jax.ShapeDtypeStruct((B, GDN_HEADS, GDN_DK, GDN_DV), F32),
                   jax.ShapeDtypeStruct((B, CONV_W - 1, CONV_DIM), F32)],
        scratch_shapes=[vm((NB, LC + 8, CONV_DIM), F32), vm((NB, LC, CONV_DIM), F32),
                        vm((NB, GDN_HEADS, GDN_DK, GDN_DV), F32),
                        vm((U, HR, LANES), F32), vm((U, HR, LANES), F32), vm((U, HR, LANES), BF16),
                        vm((U, HR, LANES), F32),
                        vm((U, HR, LANES), F32), vm((U, HR, LANES), F32),
                        vm((U, HR, HR), F32), vm((U, HR, HR), F32),
                        vm((U, HR, HR), BF16), vm((U, HR, HR), F32), vm((U, HR, HR), BF16),
                        vm((U, HR, LANES), BF16), vm((U, HR, LANES), BF16), vm((U, HR, LANES), BF16),
                        vm((U, HR, LANES), F32), vm((U, HR, HR), BF16),
                        vm((U, HR, LANES), F32), vm((U, HR, LANES), BF16),
                        vm((U, HR, LANES), BF16), vm((U, HR, LANES), F32)],
        compiler_params=_cparams(("parallel", "arbitrary")),
        name="gdn",
    )(u3, small3, z3, conv_w, conv_past, s0, gnorm, *masks)


def _kvproj_kernel(lat_ref, kpe_ref, wk_ref, wv_ref, knn_ref, kn_ref, kr_ref, v_ref, *, v_transposed):
    lb = lat_ref[...].astype(BF16)
    kf = _dot(lb, wk_ref[...])
    for h in range(MLA_HEADS):
        sl = slice(h * NOPE_DIM, (h + 1) * NOPE_DIM)
        kn_ref[:, sl] = _rms(kf[:, sl], knn_ref[...]).astype(BF16)
    if v_transposed:
        v_ref[...] = _dot_nt(wv_ref[...], lb).astype(BF16)
    else:
        v_ref[...] = _dot(lb, wv_ref[...]).astype(BF16)
    kp = kpe_ref[...]
    if kp.shape[1] == ROPE_DIM:
        kr_ref[...] = jnp.concatenate([kp, jnp.zeros((kp.shape[0], LANES - ROPE_DIM), F32)], axis=1).astype(BF16)
    else:
        lane = lax.broadcasted_iota(jnp.int32, kp.shape, 1)
        kr_ref[...] = jnp.where(lane < ROPE_DIM, kp, 0.0).astype(BF16)


def _kvproj(lat3, kpe3, pw, tm, v_transposed):
    B, sk, _ = lat3.shape
    hw = MLA_HEADS * V_DIM
    row = lambda n: pl.BlockSpec((None, tm, n), lambda b, i: (b, i, 0))
    if v_transposed:
        vspec, vshape, w_v = pl.BlockSpec((None, hw, tm), lambda b, i: (b, 0, i)), (B, hw, sk), pw['w_vt']
    else:
        vspec, vshape, w_v = row(hw), (B, sk, hw), pw['w_v']
    return pl.pallas_call(
        functools.partial(_kvproj_kernel, v_transposed=v_transposed),
        grid=(B, sk // tm),
        in_specs=[row(KV_RANK), row(kpe3.shape[-1]), _const_spec(pw['w_k'].shape), _const_spec(w_v.shape),
                  _const_spec(pw['knn'].shape)],
        out_specs=[row(MLA_HEADS * NOPE_DIM), row(LANES), vspec],
        out_shape=[jax.ShapeDtypeStruct((B, sk, MLA_HEADS * NOPE_DIM), BF16),
                   jax.ShapeDtypeStruct((B, sk, LANES), BF16), jax.ShapeDtypeStruct(vshape, BF16)],
        compiler_params=_cparams(("parallel", "parallel")),
        name="kvproj",
    )(lat3, kpe3, pw['w_k'], w_v, pw['knn'])


def _last_kblock(qi, tq, tk, P, nk):
    last_key = ((P + qi * tq + tq - 1) // CHUNK) * CHUNK + CHUNK - 1
    return jnp.minimum(last_key // tk, nk - 1)


def _flash_kernel(q_ref, kn_ref, kr_ref, v_ref, o_ref, m_ref, l_ref, acc_ref, *, tq, tk, P, S, nk):
    qi = pl.program_id(1)
    ki = pl.program_id(2)

    @pl.when(ki == 0)
    def _():
        m_ref[...] = jnp.full(m_ref.shape, NEG_BIG, F32)
        l_ref[...] = jnp.zeros(l_ref.shape, F32)
        acc_ref[...] = jnp.zeros(acc_ref.shape, F32)

    @pl.when(ki <= _last_kblock(qi, tq, tk, P, nk))
    def _():
        qpos = P + qi * tq + lax.broadcasted_iota(jnp.int32, (tq, tk), 0)
        kpos = ki * tk + lax.broadcasted_iota(jnp.int32, (tq, tk), 1)
        mask = ((kpos // CHUNK) <= (qpos // CHUNK)) & (kpos < P + S)
        kr = kr_ref[...]
        for h in range(MLA_HEADS):
            qh = q_ref[:, h * Q_SLAB:(h + 1) * Q_SLAB]
            kh = jnp.concatenate([kn_ref[:, h * NOPE_DIM:(h + 1) * NOPE_DIM], kr], axis=1)
            s = jnp.where(mask, _dot_nt(qh, kh), NEG_BIG)
            m_prev = m_ref[h]
            m_new = jnp.maximum(m_prev, jnp.max(s, axis=-1, keepdims=True))
            alpha = jnp.exp(m_prev - m_new)
            p = jnp.exp(s - m_new)
            l_ref[h] = alpha * l_ref[h] + jnp.sum(p, axis=-1, keepdims=True)
            acc_ref[h] = alpha * acc_ref[h] + _dot(p.astype(BF16), v_ref[:, h * V_DIM:(h + 1) * V_DIM])
            m_ref[h] = m_new

    @pl.when(ki == nk - 1)
    def _():
        for h in range(MLA_HEADS):
            o_ref[:, h * V_DIM:(h + 1) * V_DIM] = (acc_ref[h] / l_ref[h]).astype(BF16)


def _flash(q3, kn3, kr3, v3, P, S, tq, tk):
    B = q3.shape[0]
    skp = kn3.shape[1]
    nk = skp // tk
    kmap = lambda b, qi, ki: (b, jnp.minimum(ki, _last_kblock(qi, tq, tk, P, nk)), 0)
    return pl.pallas_call(
        functools.partial(_flash_kernel, tq=tq, tk=tk, P=P, S=S, nk=nk),
        grid=(B, S // tq, nk),
        in_specs=[pl.BlockSpec((None, tq, MLA_HEADS * Q_SLAB), lambda b, qi, ki: (b, qi, 0)),
                  pl.BlockSpec((None, tk, MLA_HEADS * NOPE_DIM), kmap),
                  pl.BlockSpec((None, tk, LANES), kmap),
                  pl.BlockSpec((None, tk, MLA_HEADS * V_DIM), kmap)],
        out_specs=pl.BlockSpec((None, tq, MLA_HEADS * V_DIM), lambda b, qi, ki: (b, qi, 0)),
        out_shape=jax.ShapeDtypeStruct((B, S, MLA_HEADS * V_DIM), BF16),
        scratch_shapes=[pltpu.VMEM((MLA_HEADS, tq, 1), F32), pltpu.VMEM((MLA_HEADS, tq, 1), F32),
                        pltpu.VMEM((MLA_HEADS, tq, V_DIM), F32)],
        compiler_params=_cparams(("parallel", "parallel", "arbitrary")),
        name="mla_attn",
    )(q3, kn3, kr3, v3)


def _attn_dec_kernel(q_ref, lat_ref, kpe_ref, knn_ref, krn_ref, vn_ref, wk_ref, wv_ref, gain_ref, o_ref):
    lb = lat_ref[...].astype(BF16)
    kf = _dot(lb, wk_ref[...])
    vp = _dot(lb, wv_ref[...]).astype(BF16)
    kp = kpe_ref[...]
    krp = jnp.concatenate([kp, jnp.zeros((kp.shape[0], LANES - ROPE_DIM), F32)], axis=1).astype(BF16)
    krn = krn_ref[...]
    hsl = [slice(h * NOPE_DIM, (h + 1) * NOPE_DIM) for h in range(MLA_HEADS)]
    qs = [q_ref[:, h * Q_SLAB:(h + 1) * Q_SLAB] for h in range(MLA_HEADS)]
    knp = [_rms(kf[:, sl], gain_ref[...]).astype(BF16) for sl in hsl]
    sps = [_dot_nt(q, jnp.concatenate([k, krp], axis=1)) for q, k in zip(qs, knp)]
    sns = [_dot_nt(q, jnp.concatenate([knn_ref[:, sl], krn], axis=1)) for q, sl in zip(qs, hsl)]
    for h, (sp, sn) in enumerate(zip(sps, sns)):
        vsl = slice(h * V_DIM, (h + 1) * V_DIM)
        m = jnp.maximum(jnp.max(sp, axis=-1, keepdims=True), jnp.max(sn, axis=-1, keepdims=True))
        pp = jnp.exp(sp - m)
        pn = jnp.exp(sn - m)
        l = jnp.sum(pp, axis=-1, keepdims=True) + jnp.sum(pn, axis=-1, keepdims=True)
        o = _dot(pp.astype(BF16), vp[:, vsl]) + _dot(pn.astype(BF16), vn_ref[:, vsl])
        o_ref[:, vsl] = (o / l).astype(BF16)


def _attn_dec(q3, lat_past, kpe_past, new, pw):
    B, S, _ = q3.shape
    consts = [pw['w_k'], pw['w_v'], pw['knn']]
    specs = [pl.BlockSpec((None, a.shape[1], a.shape[2]), lambda b: (b, 0, 0)) for a in (q3, lat_past, kpe_past, *new)]
    return pl.pallas_call(
        _attn_dec_kernel,
        grid=(B,),
        in_specs=specs + [_const_spec(c.shape) for c in consts],
        out_specs=pl.BlockSpec((None, S, MLA_HEADS * V_DIM), lambda b: (b, 0, 0)),
        out_shape=jax.ShapeDtypeStruct((B, S, MLA_HEADS * V_DIM), BF16),
        compiler_params=_cparams(("parallel",)),
        name="mla_attn_dec",
    )(q3, lat_past, kpe_past, *new, *consts)


def _flash_t_kernel(qt_ref, kt_ref, lt_ref, q_ref, kn_ref, kr_ref, vt_ref, o_ref, m_ref, acc_ref, *, tq, tk, P, S):
    j = pl.program_id(1)
    qi = qt_ref[j]
    ki = kt_ref[j]
    q0 = P + qi * tq
    k0 = ki * tk

    @pl.when(ki == 0)
    def _():
        m_ref[...] = jnp.full(m_ref.shape, NEG_BIG, F32)
        acc_ref[...] = jnp.zeros(acc_ref.shape, F32)

    def step(masked):
        kr = kr_ref[...]
        ones = jnp.ones((ONES_ROWS, tk), BF16)
        if masked:
            kpos = k0 + lax.broadcasted_iota(jnp.int32, (tk, 1), 0)
            qpos = q0 + lax.broadcasted_iota(jnp.int32, (1, tq), 1)
            mask = ((kpos // CHUNK) <= (qpos // CHUNK)) & (kpos < P + S)
        sts = []
        for h in range(MLA_HEADS):
            kh = jnp.concatenate([kn_ref[:, h * NOPE_DIM:(h + 1) * NOPE_DIM], kr], axis=1)
            sts.append(_dot_nt(kh, q_ref[:, h * Q_SLAB:(h + 1) * Q_SLAB]))
        ps, alphas = [], []
        for h in range(MLA_HEADS):
            st = sts[h]
            if masked:
                st = jnp.where(mask, st, NEG_BIG)
            m_prev = m_ref[h]
            m_new = jnp.maximum(m_prev, jnp.max(st, axis=0, keepdims=True))
            alphas.append(jnp.exp(m_prev - m_new))
            ps.append(jnp.exp(st - m_new).astype(BF16))
            m_ref[h] = m_new
        for h in range(MLA_HEADS):
            vt1 = jnp.concatenate([vt_ref[h * V_DIM:(h + 1) * V_DIM, :], ones], axis=0)
            acc_ref[h] = alphas[h] * acc_ref[h] + _dot(vt1, ps[h])

    full = ((k0 + tk - 1) // CHUNK <= q0 // CHUNK) & (k0 + tk <= P + S)
    pl.when(full)(functools.partial(step, False))
    pl.when(jnp.logical_not(full))(functools.partial(step, True))

    @pl.when(lt_ref[j] == 1)
    def _():
        for h in range(MLA_HEADS):
            acc = acc_ref[h]
            o_ref[:, h * V_DIM:(h + 1) * V_DIM] = (acc[:V_DIM] / acc[V_DIM:V_DIM + 1]).T.astype(BF16)


def _flash_t(q3, kn3, kr3, vt3, P, S, tq, tk):
    B = q3.shape[0]
    nk = kn3.shape[1] // tk
    pairs = []
    for qi in range(S // tq):
        last = min((((P + qi * tq + tq - 1) // CHUNK) * CHUNK + CHUNK - 1) // tk, nk - 1)
        pairs += [(qi, ki, int(ki == last)) for ki in range(last + 1)]
    qt, kt, lt = (jnp.asarray(np.array(col, np.int32)) for col in zip(*pairs))
    kmap = lambda b, j, qt, kt, lt: (b, kt[j], 0)
    gs = pltpu.PrefetchScalarGridSpec(
        num_scalar_prefetch=3,
        grid=(B, len(pairs)),
        in_specs=[pl.BlockSpec((None, tq, MLA_HEADS * Q_SLAB), lambda b, j, qt, kt, lt: (b, qt[j], 0)),
                  pl.BlockSpec((None, tk, MLA_HEADS * NOPE_DIM), kmap),
                  pl.BlockSpec((None, tk, LANES), kmap),
                  pl.BlockSpec((None, MLA_HEADS * V_DIM, tk), lambda b, j, qt, kt, lt: (b, 0, kt[j]))],
        out_specs=pl.BlockSpec((None, tq, MLA_HEADS * V_DIM), lambda b, j, qt, kt, lt: (b, qt[j], 0)),
        scratch_shapes=[pltpu.VMEM((MLA_HEADS, 1, tq), F32), pltpu.VMEM((MLA_HEADS, V_DIM + ONES_ROWS, tq), F32)],
    )
    return pl.pallas_call(
        functools.partial(_flash_t_kernel, tq=tq, tk=tk, P=P, S=S),
        grid_spec=gs,
        out_shape=jax.ShapeDtypeStruct((B, S, MLA_HEADS * V_DIM), BF16),
        compiler_params=_cparams(("parallel", "arbitrary")),
        name="mla_attn_t",
    )(qt, kt, lt, q3, kn3, kr3, vt3)


def _memkv_kernel(mem_ref, mnorm_ref, wmk_ref, wmv_ref, mkn_ref, k_ref, v_ref):
    mb = _rms(mem_ref[...], mnorm_ref[...]).astype(BF16)
    kf = _dot(mb, wmk_ref[...])
    for h in range(MEM_HEADS):
        sl = slice(h * MEM_HEAD_DIM, (h + 1) * MEM_HEAD_DIM)
        k_ref[:, sl] = _rms(kf[:, sl], mkn_ref[...])
    v_ref[...] = _dot(mb, wmv_ref[...])


def _memkv(mem2d, mnorm, w_mk, w_mv, mkn, tm):
    T = mem2d.shape[0]
    hw = MEM_HEADS * MEM_HEAD_DIM
    row = lambda n: pl.BlockSpec((tm, n), lambda i: (i, 0))
    return pl.pallas_call(
        _memkv_kernel,
        grid=(T // tm,),
        in_specs=[row(D_MODEL), _const_spec(mnorm.shape), _const_spec(w_mk.shape), _const_spec(w_mv.shape),
                  _const_spec(mkn.shape)],
        out_specs=[row(hw), row(hw)],
        out_shape=[jax.ShapeDtypeStruct((T, hw), F32), jax.ShapeDtypeStruct((T, hw), F32)],
        compiler_params=_cparams(("parallel",)),
        name="mem_kv",
    )(mem2d, mnorm, w_mk, w_mv, mkn)


def _mix_kernel(x_ref, og_ref, om_ref, mk_ref, mv_ref, wout_ref, nmem_ref, wmq_ref, mqn_ref, wmo_ref, nffn_ref,
                wrh_ref, wrl_ref, br_ref, x2_ref, h3_ref, idx_ref, gate_ref, counts_ref, cnt_ref):
    def mem_head(ref, b, h):
        if len(ref.shape) == 3:
            return ref[b, :, h * MEM_HEAD_DIM:(h + 1) * MEM_HEAD_DIM].astype(BF16)
        return ref[b, :, h, :].astype(BF16)

    first = (pl.program_id(0) == 0) & (pl.program_id(1) == 0)

    @pl.when(first)
    def _():
        cnt_ref[...] = jnp.zeros(cnt_ref.shape, F32)

    nbm, tm, _ = x_ref.shape
    rows = nbm * tm
    flat = lambda ref: ref[...].reshape(rows, ref.shape[-1])

    x1 = flat(x_ref) + _dot(flat(og_ref), wout_ref[0:GDN_VW, :]) + _dot(flat(om_ref), wout_ref[GDN_VW:, :])
    hb = _rms(x1, nmem_ref[...]).astype(BF16)
    qm = _dot(hb, wmq_ref[...])
    per_batch = []
    hsl = [slice(h * MEM_HEAD_DIM, (h + 1) * MEM_HEAD_DIM) for h in range(MEM_HEADS)]
    for b in range(nbm):
        br = slice(b * tm, (b + 1) * tm)
        qhs = [(_rms(qm[br, sl], mqn_ref[...]) * (MEM_HEAD_DIM ** -0.5)).astype(BF16) for sl in hsl]
        ss = [_dot_nt(qh, mem_head(mk_ref, b, h)) for h, qh in enumerate(qhs)]
        ps = []
        for s in ss:
            p = jnp.exp(s - jnp.max(s, axis=-1, keepdims=True))
            ps.append((p / jnp.sum(p, axis=-1, keepdims=True)).astype(BF16))
        heads = [_dot(p, mem_head(mv_ref, b, h)).astype(BF16) for h, p in enumerate(ps)]
        per_batch.append(jnp.concatenate(heads, axis=1))
    om = per_batch[0] if nbm == 1 else jnp.concatenate(per_batch, axis=0)
    x2 = x1 + _dot(om, wmo_ref[...])
    x2_ref[...] = x2.reshape(x2_ref.shape)
    h3 = _rms(x2, nffn_ref[...])
    hi = h3.astype(BF16)
    packed = _pack_bf16_pairs(hi.astype(F32))
    for c in range(h3_ref.shape[0]):
        h3_ref[c] = packed[:, c * LANES:(c + 1) * LANES]
    lo = (h3 - hi.astype(F32)).astype(BF16)
    wrh = wrh_ref[...]
    logits = _dot(hi, wrh) + _dot(lo, wrh) + _dot(hi, wrl_ref[...]) + br_ref[...]

    lane = lax.broadcasted_iota(jnp.int32, logits.shape, 1).astype(F32)
    vals, idxs = [], []
    cur = logits
    for _ in range(TOP_K):
        mx = jnp.max(cur, axis=-1, keepdims=True)
        ix = jnp.min(jnp.where(cur == mx, lane, float(LANES)), axis=-1, keepdims=True)
        vals.append(mx)
        idxs.append(ix)
        cur = jnp.where(lane == ix, -3e38, cur)
    es = [jnp.exp(v - vals[0]) for v in vals]
    den = es[0] + es[1] + es[2] + es[3]

    sel = jnp.zeros(logits.shape, F32)
    for k in range(TOP_K):
        sel = sel + jnp.where(lane == idxs[k], 1.0, 0.0)
    ri = lax.broadcasted_iota(jnp.int32, (rows, rows), 0)
    ci = lax.broadcasted_iota(jnp.int32, (rows, rows), 1)
    before = jnp.where(ri > ci, 1.0, 0.0).astype(BF16)
    excl = _dot(before, sel.astype(BF16)) + cnt_ref[...]
    cnt_ref[...] = cnt_ref[...] + jnp.sum(sel, axis=0, keepdims=True)
    counts_ref[...] = cnt_ref[...].astype(jnp.int32)

    idx_out = jnp.zeros(logits.shape, F32)
    gate_out = jnp.zeros(logits.shape, F32)
    for k in range(TOP_K):
        rank = jnp.sum(jnp.where(lane == idxs[k], excl, 0.0), axis=-1, keepdims=True)
        idx_out = jnp.where(lane == float(k), idxs[k], idx_out)
        idx_out = jnp.where(lane == float(TOP_K + k), rank, idx_out)
        gate_out = jnp.where(lane == float(k), es[k] / den, gate_out)
    idx_ref[...] = idx_out.T[:2 * TOP_K, :].astype(jnp.int32)
    gate_ref[...] = gate_out.reshape(gate_ref.shape)


def _mix(x3, og3, om3, mk3, mv3, pw, tm, b0, B):
    S = x3.shape[1]
    hw = MEM_HEADS * MEM_HEAD_DIM
    nbm = _pick(B, tuple(n for n in (8, 4, 2, 1) if n * tm <= MIX_ROWS and b0 % n == 0)) if tm == S else 1
    nsb = S // tm
    boff = b0 // nbm
    tile_in = lambda n: pl.BlockSpec((nbm, tm, n), lambda b, i: (b + boff, i, 0))
    tile = lambda n: pl.BlockSpec((nbm, tm, n), lambda b, i: (b, i, 0))
    if mk3.ndim == 3:
        memspec = pl.BlockSpec((nbm, N_MEM, hw), lambda b, i: (b + boff, 0, 0))
    else:
        memspec = pl.BlockSpec((nbm, N_MEM, MEM_HEADS, MEM_HEAD_DIM), lambda b, i: (b + boff, 0, 0, 0))
    consts = [pw['w_out'], pw['nmem'], pw['w_mq'], pw['mqn'], pw['w_mo'], pw['nffn'], pw['wr_hi'], pw['wr_lo'],
              pw['b_r']]
    return pl.pallas_call(
        _mix_kernel,
        grid=(B // nbm, nsb),
        in_specs=[tile_in(D_MODEL), tile_in(GDN_VW), tile_in(MLA_HEADS * V_DIM), memspec, memspec]
                 + [_const_spec(c.shape) for c in consts],
        out_specs=[tile(D_MODEL), pl.BlockSpec((PK_CHUNKS, nbm * tm, LANES), lambda b, i: (0, b * nsb + i, 0)),
                   pl.BlockSpec((2 * TOP_K, nbm * tm), lambda b, i: (0, b * nsb + i)),
                   tile(LANES), _const_spec((1, LANES))],
        out_shape=[jax.ShapeDtypeStruct((B, S, D_MODEL), F32),
                   jax.ShapeDtypeStruct((PK_CHUNKS, B * S, LANES), jnp.uint32),
                   jax.ShapeDtypeStruct((2 * TOP_K, B * S), jnp.int32), jax.ShapeDtypeStruct((B, S, LANES), F32),
                   jax.ShapeDtypeStruct((1, LANES), jnp.int32)],
        scratch_shapes=[pltpu.VMEM((1, LANES), F32)],
        compiler_params=_cparams(("arbitrary", "arbitrary")),
        name="mix_mem_router",
    )(x3, og3, om3, mk3, mv3, *consts)


def _expert_kernel(be_ref, nx_ref, nv_ref, nu_ref, rows_ref, wgu_hbm, bgu_ref, wd_hbm, bd_ref, y_ref,
                   wgus_ref, wds_ref, wgub_ref, wdb_ref, sem_ref):
    i = pl.program_id(0)
    used = i < nu_ref[0]
    new_expert = (i == 0) | (be_ref[i] != be_ref[jnp.maximum(i - 1, 0)])

    def weight_copies(e):
        return (pltpu.make_async_copy(wgu_hbm.at[e], wgus_ref, sem_ref.at[0]),
                pltpu.make_async_copy(wd_hbm.at[e], wds_ref, sem_ref.at[1]))

    @pl.when(i == 0)
    def _():
        for cp in weight_copies(be_ref[0]):
            cp.start()

    @pl.when(used & new_expert)
    def _():
        for cp in weight_copies(be_ref[i]):
            cp.wait()

        def cast(r, carry):
            rs = pl.ds(pl.multiple_of(r * LANES, LANES), LANES)
            wgub_ref[rs, :] = wgus_ref[rs, :].astype(BF16)
            wdb_ref[rs, :] = wds_ref[rs, :].astype(BF16)
            return carry
        lax.fori_loop(0, D_MODEL // LANES, cast, 0)

        @pl.when(nx_ref[i] >= 0)
        def _():
            for cp in weight_copies(nx_ref[i]):
                cp.start()

    def ffn(nrows):
        packed = jnp.concatenate([rows_ref[c, :nrows, :] for c in range(PK_CHUNKS)], axis=1)
        x = _unpack_bf16_pairs(packed).astype(BF16)
        acc = None
        for c in range(D_FF // FF_CHUNK):
            gs_ = slice(c * FF_CHUNK, (c + 1) * FF_CHUNK)
            us_ = slice(D_FF + c * FF_CHUNK, D_FF + (c + 1) * FF_CHUNK)
            gt = jnp.minimum(_dot(x, wgub_ref[:, gs_]) + bgu_ref[:, gs_], SWIGLU_LIMIT)
            up = jnp.clip(_dot(x, wgub_ref[:, us_]) + bgu_ref[:, us_], -SWIGLU_LIMIT, SWIGLU_LIMIT)
            act = gt * _sigmoid(SWIGLU_ALPHA * gt) * (up + 1.0)
            part = _dot(act.astype(BF16), wdb_ref[gs_, :])
            acc = part if acc is None else acc + part
        ypk = _pack_bf16_pairs((acc + bd_ref[...]).astype(BF16).astype(F32))
        for c in range(PK_CHUNKS):
            y_ref[c, :nrows, :] = ypk[:, c * LANES:(c + 1) * LANES]
        if nrows < MOE_ROWS:
            y_ref[:, nrows:, :] = jnp.zeros((PK_CHUNKS, MOE_ROWS - nrows, LANES), y_ref.dtype)

    half_full = nv_ref[i] <= MOE_ROWS // 2
    pl.when(used & jnp.logical_not(half_full))(functools.partial(ffn, MOE_ROWS))
    pl.when(used & half_full)(functools.partial(ffn, MOE_ROWS // 2))

    @pl.when(jnp.logical_not(used))
    def _():
        y_ref[...] = jnp.zeros(y_ref.shape, y_ref.dtype)


def _experts(block_e, next_e, block_rows, n_used, rows, w_gu, b_gu, w_down, b_down):
    n_rows = rows.shape[1]
    nb = n_rows // MOE_ROWS
    gs = pltpu.PrefetchScalarGridSpec(
        num_scalar_prefetch=4,
        grid=(nb,),
        in_specs=[pl.BlockSpec((PK_CHUNKS, MOE_ROWS, LANES), lambda i, be, nx, nv, nu: (0, i, 0)),
                  pl.BlockSpec(memory_space=pl.ANY),
                  pl.BlockSpec((None, 1, 2 * D_FF), lambda i, be, nx, nv, nu: (be[i], 0, 0)),
                  pl.BlockSpec(memory_space=pl.ANY),
                  pl.BlockSpec((None, 1, D_MODEL), lambda i, be, nx, nv, nu: (be[i], 0, 0))],
        out_specs=pl.BlockSpec((PK_CHUNKS, MOE_ROWS, LANES), lambda i, be, nx, nv, nu: (0, i, 0)),
        scratch_shapes=[pltpu.VMEM((D_MODEL, 2 * D_FF), F32), pltpu.VMEM((D_FF, D_MODEL), F32),
                        pltpu.VMEM((D_MODEL, 2 * D_FF), BF16), pltpu.VMEM((D_FF, D_MODEL), BF16),
                        pltpu.SemaphoreType.DMA((2,))],
    )
    return pl.pallas_call(
        _expert_kernel,
        grid_spec=gs,
        out_shape=jax.ShapeDtypeStruct((PK_CHUNKS, n_rows, LANES), jnp.uint32),
        compiler_params=_cparams(("arbitrary",)),
        name="moe_experts",
    )(block_e, next_e, block_rows, n_used, rows, w_gu, b_gu, w_down, b_down)


def _sc_mesh():
    return plsc.VectorSubcoreMesh(core_axis_name="core", subcore_axis_name="subcore")


def _sc_scatter_rows(x3s, pos_ts, n_rows):
    C, _, L = x3s[0].shape
    K = pos_ts[0].shape[0]
    ns = len(x3s)

    @functools.partial(pl.kernel, out_type=jax.ShapeDtypeStruct((C, n_rows, L), x3s[0].dtype), mesh=_sc_mesh(),
                       scratch_types=[])
    def scatter(*refs):
        o_hbm = refs[2 * ns]
        for s in range(ns):
            x_hbm, i_hbm = refs[s], refs[ns + s]
            nwin = x3s[s].shape[1] // SC_WINDOW
            for c in range(C):
                def body(x_vmem, i_vmem, c=c):
                    for k in range(K):
                        pltpu.sync_copy(x_vmem, o_hbm.at[c].at[i_vmem.at[k]])

                pltpu.emit_pipeline(
                    body, grid=(nwin,),
                    in_specs=[pl.BlockSpec((SC_WINDOW, L), lambda i, c=c, nwin=nwin: (c * nwin + i, 0)),
                              pl.BlockSpec((K, SC_WINDOW), lambda i: (0, i))],
                    out_specs=[], core_axis_name=("core", "subcore"), dimension_semantics=(pltpu.PARALLEL,),
                )(x_hbm, i_hbm)

    return scatter(*[x.reshape(-1, L) for x in x3s], *pos_ts)


def _sc_gather_rows(table3, idxs):
    C, _, L = table3.shape
    ns = len(idxs)
    out_type = [jax.ShapeDtypeStruct((C * i.shape[0], L), table3.dtype) for i in idxs]

    @functools.partial(pl.kernel, out_type=out_type, mesh=_sc_mesh(), scratch_types=[])
    def gather(t_hbm, *refs):
        for s in range(ns):
            i_hbm, o_hbm = refs[s], refs[ns + s]
            nwin = idxs[s].shape[0] // SC_WINDOW
            for c in range(C):
                def body(i_vmem, o_vmem, c=c):
                    pltpu.sync_copy(t_hbm.at[c].at[i_vmem.at[0]], o_vmem)

                pltpu.emit_pipeline(
                    body, grid=(nwin,),
                    in_specs=[pl.BlockSpec((1, SC_WINDOW), lambda i: (0, i))],
                    out_specs=[pl.BlockSpec((SC_WINDOW, L), lambda i, c=c, nwin=nwin: (c * nwin + i, 0))],
                    core_axis_name=("core", "subcore"), dimension_semantics=(pltpu.PARALLEL,),
                )(i_hbm, o_hbm)

    outs = gather(table3, *[i.reshape(1, -1) for i in idxs])
    return [o.reshape(C, -1, L) for o in outs]


def _combine_kernel(x2_ref, g_ref, gate_ref, *rest):
    o_ref = rest[-1]
    gate = gate_ref[...]
    half = D_MODEL // 2
    for c in range(PK_CHUNKS):
        lo_s = slice(c * LANES, (c + 1) * LANES)
        hi_s = slice(half + c * LANES, half + (c + 1) * LANES)
        acc_lo = x2_ref[:, lo_s]
        acc_hi = x2_ref[:, hi_s]
        for k in range(TOP_K):
            w = g_ref[c, k]
            gk = gate[:, k:k + 1]
            acc_lo = acc_lo + pltpu.bitcast(w << 16, F32) * gk
            acc_hi = acc_hi + pltpu.bitcast(w & jnp.uint32(0xFFFF0000), F32) * gk
        o_ref[:, lo_s] = acc_lo
        o_ref[:, hi_s] = acc_hi


def _combine(x2, g4, gate, tm, out_buf=None, row0=0, t_total=None):
    T = x2.shape[0]
    t_total = T if t_total is None else t_total
    blk0 = row0 // tm
    in_specs = [pl.BlockSpec((tm, D_MODEL), lambda i: (i, 0)),
                pl.BlockSpec((PK_CHUNKS, TOP_K, tm, LANES), lambda i: (0, 0, i, 0)),
                pl.BlockSpec((tm, LANES), lambda i: (i, 0))]
    args = [x2, g4, gate]
    aliases = {}
    if out_buf is not None:
        in_specs.append(pl.BlockSpec(memory_space=pl.ANY))
        args.append(out_buf)
        aliases = {3: 0}
    return pl.pallas_call(
        _combine_kernel,
        grid=(T // tm,),
        in_specs=in_specs,
        out_specs=pl.BlockSpec((tm, D_MODEL), lambda i: (i + blk0, 0)),
        out_shape=jax.ShapeDtypeStruct((t_total, D_MODEL), F32),
        input_output_aliases=aliases,
        compiler_params=_cparams(("parallel",)),
        name="moe_combine",
    )(*args)


def _plan_kernel(idxr_ref, base_ref, pos_ref):
    rows = 2 * TOP_K
    base = jnp.broadcast_to(base_ref[...], (rows, LANES))
    is_id = lax.broadcasted_iota(jnp.int32, (rows, LANES), 0) < TOP_K
    for j in range(idxr_ref.shape[1] // LANES):
        cols = slice(j * LANES, (j + 1) * LANES)
        blk = idxr_ref[:, cols]
        start = jnp.take_along_axis(base, jnp.where(is_id, blk, 0), axis=1)
        pos_ref[:, cols] = (start + pltpu.roll(blk, TOP_K, 0))[:TOP_K, :]


def _plan(idxr, base):
    T = idxr.shape[1]
    tm = _pick(T, (2048, 1024, 512, 256, 128))
    return pl.pallas_call(
        _plan_kernel,
        grid=(T // tm,),
        in_specs=[pl.BlockSpec((2 * TOP_K, tm), lambda i: (0, i)), _const_spec((1, LANES))],
        out_specs=pl.BlockSpec((TOP_K, tm), lambda i: (0, i)),
        out_shape=jax.ShapeDtypeStruct((TOP_K, T), jnp.int32),
        compiler_params=_cparams(("parallel",)),
        name="moe_plan",
    )(idxr, base)


def _moe(streams, places, ew):
    cnts = [st[3][0, :N_EXPERTS] for st in streams]
    total = sum(cnts)
    padded = (total + MOE_ROWS - 1) // MOE_ROWS * MOE_ROWS
    pad_end = jnp.cumsum(padded)
    pad_start = pad_end - padded
    pos_ts = []
    base = pad_start
    for (h3p, idxr, gate, counts, x2), cnt in zip(streams, cnts):
        pos_ts.append(_plan(idxr, _pad_lanes(base)))
        base = base + cnt
    n_assign = sum(st[0].shape[1] for st in streams) * TOP_K
    nb = -(-n_assign // MOE_ROWS) + N_EXPERTS
    starts = jnp.arange(nb, dtype=jnp.int32) * MOE_ROWS
    block_e = jnp.minimum(jnp.sum((pad_end[None, :] <= starts[:, None]).astype(jnp.int32), axis=1), N_EXPERTS - 1)
    n_used = (pad_end[-1] // MOE_ROWS).astype(jnp.int32).reshape(1)
    ar = jnp.arange(N_EXPERTS, dtype=jnp.int32)
    later = (padded > 0)[None, :] & (ar[None, :] > ar[:, None])
    nxt = jnp.min(jnp.where(later, ar[None, :], N_EXPERTS), axis=1)
    nxt = jnp.where(nxt >= N_EXPERTS, -1, nxt)
    mine = block_e[:, None] == ar[None, :]
    next_e = jnp.sum(jnp.where(mine, nxt[None, :], 0), axis=1).astype(jnp.int32)
    seg_end = jnp.sum(jnp.where(mine, (pad_start + total)[None, :], 0), axis=1)
    block_rows = jnp.clip(seg_end - starts, 0, MOE_ROWS).astype(jnp.int32)
    rows = _sc_scatter_rows([st[0] for st in streams], pos_ts, nb * MOE_ROWS)
    y_rows = _experts(block_e, next_e, block_rows, n_used, rows, ew['w_gu'], ew['b_gu'], ew['w_down'], ew['b_down'])
    gs = _sc_gather_rows(y_rows, [p.reshape(-1) for p in pos_ts])
    outs = []
    for (h3p, idxr, gate, counts, x2), g, place in zip(streams, gs, places):
        T = x2.shape[0]
        outs.append(_combine(x2, g.reshape(PK_CHUNKS, TOP_K, T, LANES), gate, _pick(T, (512, 256, 128, 64)), *place))
    return outs


def _pad_lanes(v, n=LANES, fill=0.0):
    return jnp.pad(v, (0, n - v.shape[0]), constant_values=fill).reshape(1, n)


def _prep_weights(norm_mix, w_in, conv_w, a_log, dt_bias, gdn_norm, q_a_norm, w_qb, kv_a_norm, w_kvb, q_norm,
                  k_nope_norm, k_rope_norm, w_out, norm_mem, mem_norm, w_mq, w_mk, w_mv, mq_norm, mk_norm, w_mo,
                  norm_ffn, w_router, b_router, w_gu, b_gu, w_down, b_down):
    c = np.cumsum([CONV_DIM, GDN_VW, GDN_HEADS, GDN_HEADS, Q_RANK, KV_RANK])
    w_u, w_z, w_a, w_b, w_cq, w_ckv, w_kpe = [w_in[:, lo:hi] for lo, hi in
                                              zip([0, *c], [*c, w_in.shape[1]])]
    w_s = jnp.concatenate([w_kpe, w_a, w_b], axis=1)
    w_s = jnp.pad(w_s, ((0, 0), (0, LANES - w_s.shape[1])))
    wq = w_qb.reshape(Q_RANK, MLA_HEADS, QK_DIM)
    wq = jnp.pad(wq, ((0, 0), (0, 0), (0, Q_SLAB - QK_DIM))).reshape(Q_RANK, MLA_HEADS * Q_SLAB)
    wkv = w_kvb.reshape(KV_RANK, MLA_HEADS, NOPE_DIM + V_DIM)
    w_k = wkv[:, :, :NOPE_DIM].reshape(KV_RANK, -1).astype(BF16)
    w_v = wkv[:, :, NOPE_DIM:].reshape(KV_RANK, -1).astype(BF16)
    wr = jnp.pad(w_router, ((0, 0), (0, LANES - N_EXPERTS)))
    wr_hi = wr.astype(BF16)
    wr_lo = (wr - wr_hi.astype(F32)).astype(BF16)
    row = lambda v: v.reshape(1, -1)
    gpad = ROPE_DIM
    pw = dict(
        nmix=row(norm_mix), w_u=w_u.astype(BF16), w_z=w_z.astype(BF16), w_cq=w_cq.astype(BF16),
        w_ckv=w_ckv.astype(BF16), w_s=w_s.astype(BF16), qan=row(q_a_norm), w_qb=wq.astype(BF16),
        qn=_pad_lanes(q_norm, Q_SLAB), kvan=row(kv_a_norm), krn=_pad_lanes(k_rope_norm),
        alog=jnp.pad(a_log, (gpad, LANES - gpad - GDN_HEADS)).reshape(1, LANES),
        dtb=jnp.pad(dt_bias, (gpad, LANES - gpad - GDN_HEADS)).reshape(1, LANES),
        conv_w=conv_w, gnorm=row(gdn_norm), w_k=w_k, w_v=w_v, w_vt=w_v.T, knn=row(k_nope_norm),
        w_out=w_out.astype(BF16), nmem=row(norm_mem), w_mq=w_mq.astype(BF16), mqn=row(mq_norm),
        w_mo=w_mo.astype(BF16), nffn=row(norm_ffn), wr_hi=wr_hi, wr_lo=wr_lo,
        b_r=_pad_lanes(b_router, LANES, NEG_BIG),
        mnorm=row(mem_norm), w_mk=w_mk.astype(BF16), w_mv=w_mv.astype(BF16), mkn=row(mk_norm),
    )
    ew = dict(w_gu=w_gu, b_gu=b_gu.reshape(N_EXPERTS, 1, 2 * D_FF), w_down=w_down,
              b_down=b_down.reshape(N_EXPERTS, 1, D_MODEL))
    return pw, ew


def _rope_tables(P, S):
    half = ROPE_DIM // 2
    inv = ROPE_THETA ** (-jnp.arange(half, dtype=F32) / half)
    ang = (P + jnp.arange(S, dtype=jnp.int32)).astype(F32)[:, None] * inv[None, :]
    cos, sin = jnp.cos(ang), jnp.sin(ang)
    zh = jnp.zeros((S, half), F32)
    zz = jnp.zeros((S, LANES - ROPE_DIM), F32)
    return (jnp.concatenate([cos, cos, zz], 1), jnp.concatenate([-sin, zh, zz], 1),
            jnp.concatenate([zh, sin, zz], 1))


def _pick(n, prefs):
    for t in prefs:
        if n % t == 0:
            return t
    return n


def _trunk_front(x, lat_past, kpe_past, s0, conv_past, mem_k, mem_v, pw, n_groups):
    B, S, D = x.shape
    P = lat_past.shape[1]
    T = B * S
    tm = _pick(S, (512, 256, 128, 64))
    tm_in = 512 if (T % 512 == 0 and (512 % S == 0 or S % 512 == 0)) else tm
    if T % 1024 == 0 and S % 1024 == 0:
        tm_in = 1024
    u, z, q, lat_new, small = _inproj(x.reshape(T, D), S, tm_in, pw, _rope_tables(P, S))

    LC = _pick(S, (256, 128, 64))
    NB = _pick(B, tuple(n for n in (8, 4, 2, 1) if n * (LC // CHUNK) <= GDN_UNITS))
    o_gdn, s_new, conv_new = _gdn(u.reshape(B, S, CONV_DIM), small.reshape(B, S, LANES), z.reshape(B, S, GDN_VW),
                                  pw['conv_w'], conv_past, s0, pw['gnorm'], NB, LC)

    kpe_new = small[:, :ROPE_DIM].reshape(B, S, ROPE_DIM)
    lat3 = lat_new.reshape(B, S, KV_RANK)
    small3 = small.reshape(B, S, LANES)
    q3 = q.reshape(B, S, MLA_HEADS * Q_SLAB)
    tq = _pick(S, (512, 256, 128, 64))
    if P == 0 and S % 512 == 0:
        kn, kr, vt = _kvproj(lat3, small3, pw, _pick(S, (2048, 1024, 512)), True)
        o_mla = _flash_t(q3, kn, kr, vt, P, S, tq, 512)
    elif P > 0 and P % CHUNK == 0 and S == CHUNK:
        new = _kvproj(lat3, small3, pw, S, False)
        o_mla = _attn_dec(q3, lat_past, kpe_past, new, pw)
    else:
        sk = P + S
        tk = 512 if S >= 512 else -(-sk // LANES) * LANES
        skp = -(-sk // tk) * tk
        lat_all = jnp.pad(jnp.concatenate([lat_past, lat3], axis=1), ((0, 0), (0, skp - sk), (0, 0)))
        kpe_all = jnp.pad(jnp.concatenate([kpe_past, kpe_new], axis=1), ((0, 0), (0, skp - sk), (0, 0)))
        key_major = tq >= LANES
        kn, kr, v = _kvproj(lat_all, kpe_all, pw, _pick(skp, (512,)), key_major)
        o_mla = (_flash_t if key_major else _flash)(q3, kn, kr, v, P, S, tq, tk)

    streams = []
    gb = B // n_groups
    for gi in range(n_groups):
        x2, h3p, idxr, gate, counts = _mix(x, o_gdn, o_mla, mem_k, mem_v, pw, tm, gi * gb, gb)
        tg = gb * S
        streams.append((h3p, idxr, gate.reshape(tg, LANES), counts, x2.reshape(tg, D)))
    return streams, (lat_new.reshape(B, S, KV_RANK), kpe_new, s_new, conv_new)


def kernel(x_prompt, x_sample, cache_kv_latent, cache_k_rope, state_gdn, state_conv, cache_mem_k, cache_mem_v, mem_prompt, norm_mix, w_in, conv_w, a_log, dt_bias, gdn_norm, q_a_norm, w_qb, kv_a_norm, w_kvb, q_norm, k_nope_norm, k_rope_norm, w_out, norm_mem, mem_norm, w_mq, w_mk, w_mv, mq_norm, mk_norm, w_mo, norm_ffn, w_router, b_router, w_gu, b_gu, w_down, b_down):
    depth = norm_mix.shape[0]
    yp, ys = x_prompt, x_sample
    bp = x_prompt.shape[0]
    hw = MEM_HEADS * MEM_HEAD_DIM
    outs = [[] for _ in range(10)]
    for l in range(depth):
        pw, ew = _prep_weights(norm_mix[l], w_in[l], conv_w[l], a_log[l], dt_bias[l], gdn_norm[l], q_a_norm[l],
                               w_qb[l], kv_a_norm[l], w_kvb[l], q_norm[l], k_nope_norm[l], k_rope_norm[l], w_out[l],
                               norm_mem[l], mem_norm[l], w_mq[l], w_mk[l], w_mv[l], mq_norm[l], mk_norm[l], w_mo[l],
                               norm_ffn[l], w_router[l], b_router[l], w_gu[l], b_gu[l], w_down[l], b_down[l])
        nm = mem_prompt.shape[1]
        mk, mv = _memkv(mem_prompt.reshape(bp * nm, D_MODEL), pw['mnorm'], pw['w_mk'], pw['w_mv'], pw['mkn'],
                        _pick(bp * nm, (512, 256)))
        mk = mk.reshape(bp, nm, hw)
        mv = mv.reshape(bp, nm, hw)
        n_groups = 2 if bp % 2 == 0 else 1
        streams_p, (lat, kpe, s_fin, cv) = _trunk_front(
            yp, jnp.zeros((bp, 0, KV_RANK), F32), jnp.zeros((bp, 0, ROPE_DIM), F32),
            jnp.zeros((bp, GDN_HEADS, GDN_DK, GDN_DV), F32), jnp.zeros((bp, CONV_W - 1, CONV_DIM), F32), mk, mv, pw,
            n_groups)
        bs = x_sample.shape[0]
        (stream_s,), (lat2, kpe2, s_fin2, cv2) = _trunk_front(
            ys, cache_kv_latent[l], cache_k_rope[l], state_gdn[l], state_conv[l],
            cache_mem_k[l], cache_mem_v[l], pw, 1)
        tp = yp.shape[0] * yp.shape[1]
        tg = tp // n_groups
        ybuf = None
        for gi, st in enumerate(streams_p):
            last = gi == n_groups - 1
            res = _moe([st] + ([stream_s] if last else []),
                       [(ybuf, gi * tg, tp)] + ([(None, 0, None)] if last else []), ew)
            ybuf = res[0]
        yp, ys = ybuf.reshape(yp.shape), res[1].reshape(ys.shape)
        for lst, val in zip(outs, (lat, kpe, s_fin, cv, mk.reshape(bp, nm, MEM_HEADS, MEM_HEAD_DIM),
                                   mv.reshape(bp, nm, MEM_HEADS, MEM_HEAD_DIM), lat2, kpe2, s_fin2, cv2)):
            lst.append(val)
    return (yp, ys) + tuple(jnp.stack(o) for o in outs)
```

```python
import functools
import math

import numpy as np
import jax
import jax.numpy as jnp
from jax import lax
from jax.experimental import pallas as pl
from jax.experimental.pallas import tpu as pltpu
from jax.experimental.pallas import tpu_sc as plsc

F32 = jnp.float32
BF16 = jnp.bfloat16

D_MODEL = 1024
CHUNK = 64
EPS = 1e-6
GDN_HEADS = 4
GDN_DK = 128
GDN_DV = 128
CONV_W = 4
GDN_QK = GDN_HEADS * GDN_DK
GDN_VW = GDN_HEADS * GDN_DV
CONV_DIM = 2 * GDN_QK + GDN_VW
MLA_HEADS = 4
Q_RANK = 384
KV_RANK = 256
NOPE_DIM = 128
ROPE_DIM = 64
V_DIM = 128
QK_DIM = NOPE_DIM + ROPE_DIM
ROPE_THETA = 10000.0
N_MEM = 256
MEM_HEADS = 4
MEM_HEAD_DIM = 128
N_EXPERTS = 32
TOP_K = 4
D_FF = D_MODEL
SWIGLU_ALPHA = 1.702
SWIGLU_LIMIT = 7.0

LANES = 128
Q_SLAB = 2 * LANES
NEG_BIG = -1e30
VMEM_LIMIT = 56 * 1024 * 1024
MOE_ROWS = 512
GDN_UNITS = 8
GDN_GROUP = 2
FF_CHUNK = 512
SC_WINDOW = 128
PK_CHUNKS = D_MODEL // 2 // LANES
SUBTILE_ROWS = 256
ONES_ROWS = 16
MIX_ROWS = 512


def _cparams(sem):
    return pltpu.CompilerParams(dimension_semantics=sem, vmem_limit_bytes=VMEM_LIMIT)


def _dot(a, b):
    return jnp.dot(a, b, preferred_element_type=F32)


def _dot_nt(a, b):
    return lax.dot_general(a, b, (((1,), (1,)), ((), ())), preferred_element_type=F32)


def _dot_tn(a, b):
    return lax.dot_general(a, b, (((0,), (0,)), ((), ())), preferred_element_type=F32)


def _rms(x, gain, n=None):
    n = x.shape[-1] if n is None else n
    ss = jnp.sum(x * x, axis=-1, keepdims=True) * (1.0 / n)
    return (x * lax.rsqrt(ss + EPS)) * gain


def _sigmoid(x):
    return 1.0 / (1.0 + jnp.exp(-x))


def _rope128(r, cos, sna, snb):
    return r * cos + pltpu.roll(r, 96, 1) * sna + pltpu.roll(r, 32, 1) * snb


def _pack_bf16_pairs(x):
    n = x.shape[1] // 2
    lo = pltpu.bitcast(x[:, :n], jnp.uint32) >> 16
    hi = pltpu.bitcast(x[:, n:], jnp.uint32) & jnp.uint32(0xFFFF0000)
    return lo | hi


def _unpack_bf16_pairs(p):
    lo = pltpu.bitcast(p << 16, F32)
    hi = pltpu.bitcast(p & jnp.uint32(0xFFFF0000), F32)
    return jnp.concatenate([lo, hi], axis=1)


def _subtiles(rows):
    n = rows // SUBTILE_ROWS if rows % SUBTILE_ROWS == 0 else 1
    step = rows // n
    return [slice(i * step, (i + 1) * step) for i in range(n)]


def _const_spec(shape):
    nd = len(shape)
    return pl.BlockSpec(shape, lambda *_: (0,) * nd)


def _inproj_kernel(x_ref, nmix_ref, wu_ref, wz_ref, wcq_ref, wckv_ref, ws_ref, qan_ref, wqb_ref, qn_ref,
                   kvan_ref, krn_ref, alog_ref, dtb_ref, cos_ref, sna_ref, snb_ref,
                   u_ref, z_ref, q_ref, lat_ref, small_ref):
    for rs in _subtiles(x_ref.shape[0]):
        x = x_ref[rs, :]
        hb = _rms(x, nmix_ref[...]).astype(BF16)
        u_ref[rs, :] = _dot(hb, wu_ref[...])
        z_ref[rs, :] = _dot(hb, wz_ref[...])
        cq_raw = _dot(hb, wcq_ref[...])
        ckv_raw = _dot(hb, wckv_ref[...])
        sm = _dot(hb, ws_ref[...])
        cos, sna, snb = cos_ref[rs, :], sna_ref[rs, :], snb_ref[rs, :]

        cq = _rms(cq_raw, qan_ref[...]).astype(BF16)
        qf = _dot(cq, wqb_ref[...])
        scale = QK_DIM ** -0.5
        for h in range(MLA_HEADS):
            slab = qf[:, h * Q_SLAB:(h + 1) * Q_SLAB]
            slab = _rms(slab, qn_ref[...], n=QK_DIM)
            nope = slab[:, :LANES]
            ropd = _rope128(slab[:, LANES:], cos, sna, snb)
            q_ref[rs, h * Q_SLAB:h * Q_SLAB + LANES] = (nope * scale).astype(BF16)
            q_ref[rs, h * Q_SLAB + LANES:(h + 1) * Q_SLAB] = (ropd * scale).astype(BF16)

        lat_ref[rs, :] = _rms(ckv_raw, kvan_ref[...])

        lane = lax.broadcasted_iota(jnp.int32, sm.shape, 1)
        kp = jnp.where(lane < ROPE_DIM, sm, 0.0)
        kpe = _rope128(_rms(kp, krn_ref[...], n=ROPE_DIM), cos, sna, snb)
        sp = sm + dtb_ref[...]
        softplus = jnp.maximum(sp, 0.0) + jnp.log1p(jnp.exp(-jnp.abs(sp)))
        g = -jnp.exp(alog_ref[...]) * softplus
        beta = _sigmoid(sm)
        small_ref[rs, :] = jnp.where(lane < ROPE_DIM, kpe,
                                     jnp.where(lane < ROPE_DIM + GDN_HEADS, g,
                                               jnp.where(lane < ROPE_DIM + 2 * GDN_HEADS, beta, 0.0)))


def _inproj(x2d, S, tm, pw, tabs):
    T = x2d.shape[0]
    if tm > S:
        tabs = [jnp.tile(t, (tm // S, 1)) for t in tabs]
    nblk_s = max(S // tm, 1)
    row = lambda n: pl.BlockSpec((tm, n), lambda i: (i, 0))
    tab = pl.BlockSpec((tm, LANES), lambda i: (i % nblk_s, 0))
    consts = [pw['nmix'], pw['w_u'], pw['w_z'], pw['w_cq'], pw['w_ckv'], pw['w_s'], pw['qan'], pw['w_qb'],
              pw['qn'], pw['kvan'], pw['krn'], pw['alog'], pw['dtb']]
    return pl.pallas_call(
        _inproj_kernel,
        grid=(T // tm,),
        in_specs=[row(D_MODEL)] + [_const_spec(c.shape) for c in consts] + [tab, tab, tab],
        out_specs=[row(CONV_DIM), row(GDN_VW), row(MLA_HEADS * Q_SLAB), row(KV_RANK), row(LANES)],
        out_shape=[jax.ShapeDtypeStruct((T, CONV_DIM), F32), jax.ShapeDtypeStruct((T, GDN_VW), F32),
                   jax.ShapeDtypeStruct((T, MLA_HEADS * Q_SLAB), BF16), jax.ShapeDtypeStruct((T, KV_RANK), F32),
                   jax.ShapeDtypeStruct((T, LANES), F32)],
        compiler_params=_cparams(("parallel",)),
        name="inproj",
    )(x2d, *consts, *tabs)


def _split3(x):
    hi = x.astype(BF16)
    r = x - hi.astype(F32)
    mid = r.astype(BF16)
    lo = (r - mid.astype(F32)).astype(BF16)
    return hi, mid, lo


def _gdn_kernel(u_ref, small_ref, z_ref, convw_ref, cpast_ref, s0_ref, gnorm_ref,
                to_ref, trilm_ref, strictm_ref, same2_ref, lvl_ref,
                o_ref, sfin_ref, cnew_ref, ext_ref, uc_ref, state_ref,
                qf_ref, kf_ref, vb_ref, bt_ref, gcum_ref, glast_ref, kk_ref, qk_ref, mb_ref, x_ref, qkb_ref,
                kbe_ref, qg_ref, kdec_ref, egl_ref, t1_ref, uu_ref, ww_ref, vn_ref, qs_ref, *, NB, LC):
    j = pl.program_id(1)
    nj = pl.num_programs(1)
    PADR = 8
    C = LC // CHUNK
    NG = GDN_HEADS // GDN_GROUP
    U = NB * C * NG
    HR = GDN_GROUP * CHUNK

    def unit(nb, c, g):
        return (nb * C + c) * NG + g

    def head_rows(h):
        return slice((h % GDN_GROUP) * CHUNK, (h % GDN_GROUP + 1) * CHUNK)

    @pl.when(j == 0)
    def _():
        state_ref[...] = s0_ref[...]
        ext_ref[:, PADR - (CONV_W - 1):PADR, :] = cpast_ref[...]

    w = convw_ref[...]
    for nb in range(NB):
        ext_ref[nb, PADR:PADR + LC, :] = u_ref[nb]
        acc = ext_ref[nb, PADR:PADR + LC, :] * w[CONV_W - 1:CONV_W, :]
        for t in range(1, CONV_W):
            acc = acc + ext_ref[nb, PADR - t:PADR - t + LC, :] * w[CONV_W - 1 - t:CONV_W - t, :]
        uc_ref[nb] = acc * _sigmoid(acc)
        ext_ref[nb, 0:PADR, :] = ext_ref[nb, LC:LC + PADR, :]

    @pl.when(j == nj - 1)
    def _():
        cnew_ref[...] = ext_ref[:, PADR - (CONV_W - 1):PADR, :]

    g0 = ROPE_DIM
    b0 = ROPE_DIM + GDN_HEADS
    to = to_ref[...]
    for nb, c in [(nb, c) for nb in range(NB) for c in range(C)]:
        rows = slice(c * CHUNK, (c + 1) * CHUNK)
        sm = small_ref[nb, rows, :]
        gl = sum(_dot(to, part) for part in _split3(sm))
        for h in range(GDN_HEADS):
            u = unit(nb, c, h // GDN_GROUP)
            hr = head_rows(h)
            q = uc_ref[nb, rows, h * GDN_DK:(h + 1) * GDN_DK]
            k = uc_ref[nb, rows, GDN_QK + h * GDN_DK:GDN_QK + (h + 1) * GDN_DK]
            v = uc_ref[nb, rows, 2 * GDN_QK + h * GDN_DV:2 * GDN_QK + (h + 1) * GDN_DV]
            beta = jnp.broadcast_to(sm[:, b0 + h:b0 + h + 1], (CHUNK, LANES))
            qf_ref[u, hr, :] = (q * lax.rsqrt(jnp.sum(q * q, -1, keepdims=True) + EPS)) * (GDN_DK ** -0.5)
            kf_ref[u, hr, :] = k * lax.rsqrt(jnp.sum(k * k, -1, keepdims=True) + EPS)
            vb_ref[u, hr, :] = (v * beta).astype(BF16)
            bt_ref[u, hr, :] = beta
            gcum_ref[u, hr, :] = jnp.broadcast_to(gl[:CHUNK, g0 + h:g0 + h + 1], (CHUNK, LANES))
            glast_ref[u, hr, :] = jnp.broadcast_to(gl[CHUNK:, g0 + h:g0 + h + 1], (CHUNK, LANES))

    for u in range(U):
        k = kf_ref[u]
        kbf = k.astype(BF16)
        kk_ref[u] = _dot_nt((k * bt_ref[u]).astype(BF16), kbf)
        qk_ref[u] = _dot_nt(qf_ref[u].astype(BF16), kbf)

    trilm = trilm_ref[...]
    eye = trilm - strictm_ref[...]
    for u in range(U):
        gcum = gcum_ref[u]
        grow = gcum.T[0:1, :]
        gcol = gcum if HR == LANES else jnp.concatenate([gcum] * (HR // LANES), axis=1)
        decay = jnp.exp(jnp.minimum(gcol - grow, 0.0)) * trilm
        m = kk_ref[u] * (decay * strictm_ref[...])
        mb_ref[u] = m.astype(BF16)
        x_ref[u] = eye - m * same2_ref[...]
        qkb_ref[u] = (qk_ref[u] * decay).astype(BF16)
        egc = jnp.exp(gcum)
        k = kf_ref[u]
        kbe_ref[u] = (k * bt_ref[u] * egc).astype(BF16)
        qg_ref[u] = (qf_ref[u] * egc).astype(BF16)
        kdec_ref[u] = (k * jnp.exp(glast_ref[u] - gcum)).astype(BF16)
        egl_ref[u] = jnp.exp(glast_ref[u])

    for lvl in range(lvl_ref.shape[0]):
        lm = lvl_ref[lvl]
        for u in range(U):
            t1_ref[u] = _dot(mb_ref[u] * lm, x_ref[u].astype(BF16)).astype(BF16)
        for u in range(U):
            x = x_ref[u]
            x_ref[u] = x - _dot(x.astype(BF16), t1_ref[u])

    for u in range(U):
        xb = x_ref[u].astype(BF16)
        uu_ref[u] = _dot(xb, vb_ref[u])
        ww_ref[u] = _dot(xb, kbe_ref[u]).astype(BF16)

    gnorm = gnorm_ref[...]
    for c in range(C):
        rows = slice(c * CHUNK, (c + 1) * CHUNK)
        heads = [(nb, h, unit(nb, c, h // GDN_GROUP), head_rows(h)) for nb in range(NB) for h in range(GDN_HEADS)]
        groups = [(nb, g, unit(nb, c, g)) for nb in range(NB) for g in range(NG)]
        for nb, h, u, hr in heads:
            stb = state_ref[nb, h].astype(BF16)
            r = _dot(jnp.concatenate([ww_ref[u, hr, :], qg_ref[u, hr, :]], axis=0), stb)
            vn_ref[u, hr, :] = (uu_ref[u, hr, :] - r[:CHUNK]).astype(BF16)
            qs_ref[u, hr, :] = r[CHUNK:]
        outs = {(nb, g): qs_ref[u] + _dot(qkb_ref[u], vn_ref[u]) for nb, g, u in groups}
        for nb, h, u, hr in heads:
            state_ref[nb, h] = (state_ref[nb, h] * egl_ref[u, hr.start:hr.start + 1, :]
                                + _dot_tn(kdec_ref[u, hr, :], vn_ref[u, hr, :]))
        for nb, h, u, hr in heads:
            zz = z_ref[nb, rows, h * GDN_DV:(h + 1) * GDN_DV]
            og = _rms(outs[(nb, h // GDN_GROUP)][hr, :], gnorm) * (zz * _sigmoid(zz))
            o_ref[nb, rows, h * GDN_DV:(h + 1) * GDN_DV] = og.astype(BF16)

    @pl.when(j == nj - 1)
    def _():
        sfin_ref[...] = state_ref[...]


def _gdn_masks():
    hr = GDN_GROUP * CHUNK
    i = np.arange(hr)[:, None]
    j = np.arange(hr)[None, :]
    same_head = (i // CHUNK) == (j // CHUNK)
    tril = same_head & (i >= j)
    strict = same_head & (i > j)
    same2 = strict & ((i // 2) == (j // 2))
    lvls = []
    blk = 2
    while blk < CHUNK:
        lvls.append(strict & ((i // (2 * blk)) == (j // (2 * blk))) & ((i // blk) != (j // blk)))
        blk *= 2
    fr = np.arange(CHUNK)
    to = np.concatenate([fr[:, None] >= fr[None, :], np.ones((CHUNK, CHUNK), bool)], axis=0)
    f = lambda a: jnp.asarray(a.astype(np.float32))
    return (jnp.asarray(to.astype(np.float32), dtype=BF16), f(tril), f(strict), f(same2),
            jnp.asarray(np.stack(lvls).astype(np.float32), dtype=BF16))


def _gdn(u3, small3, z3, conv_w, conv_past, s0, gnorm, NB, LC):
    B, S, _ = u3.shape
    C = LC // CHUNK
    U = NB * C * (GDN_HEADS // GDN_GROUP)
    HR = GDN_GROUP * CHUNK
    masks = _gdn_masks()
    tile = lambda n: pl.BlockSpec((NB, LC, n), lambda b, j: (b, j, 0))
    stspec = pl.BlockSpec((NB, GDN_HEADS, GDN_DK, GDN_DV), lambda b, j: (b, 0, 0, 0))
    cvspec = pl.BlockSpec((NB, CONV_W - 1, CONV_DIM), lambda b, j: (b, 0, 0))
    vm = lambda shape, dt: pltpu.VMEM(shape, dt)
    return pl.pallas_call(
        functools.partial(_gdn_kernel, NB=NB, LC=LC),
        grid=(B // NB, S // LC),
        in_specs=[tile(CONV_DIM), tile(LANES), tile(GDN_VW), _const_spec(conv_w.shape), cvspec, stspec,
                  _const_spec(gnorm.shape)] + [_const_spec(m.shape) for m in masks],
        out_specs=[tile(GDN_VW), stspec, cvspec],
        out_shape=[jax.ShapeDtypeStruct((B, S, GDN_VW), BF16),
                   jax.ShapeDtypeStruct((B, GDN_HEADS, GDN_DK, GDN_DV), F32),
                   jax.ShapeDtypeStruct((B, CONV_W - 1, CONV_DIM), F32)],
        scratch_shapes=[vm((NB, LC + 8, CONV_DIM), F32), vm((NB, LC, CONV_DIM), F32),
                        vm((NB, GDN_HEADS, GDN_DK, GDN_DV), F32),
                        vm((U, HR, LANES), F32), vm((U, HR, LANES), F32), vm((U, HR, LANES), BF16),
                        vm((U, HR, LANES), F32),
                        vm((U, HR, LANES), F32), vm((U, HR, LANES), F32),
                        vm((U, HR, HR), F32), vm((U, HR, HR), F32),
                        vm((U, HR, HR), BF16), vm((U, HR, HR), F32), vm((U, HR, HR), BF16),
                        vm((U, HR, LANES), BF16), vm((U, HR, LANES), BF16), vm((U, HR, LANES), BF16),
                        vm((U, HR, LANES), F32), vm((U, HR, HR), BF16),
                        vm((U, HR, LANES), F32), vm((U, HR, LANES), BF16),
                        vm((U, HR, LANES), BF16), vm((U, HR, LANES), F32)],
        compiler_params=_cparams(("parallel", "arbitrary")),
        name="gdn",
    )(u3, small3, z3, conv_w, conv_past, s0, gnorm, *masks)


def _kvproj_kernel(lat_ref, kpe_ref, wk_ref, wv_ref, knn_ref, kn_ref, kr_ref, v_ref, *, v_transposed):
    lb = lat_ref[...].astype(BF16)
    kf = _dot(lb, wk_ref[...])
    for h in range(MLA_HEADS):
        sl = slice(h * NOPE_DIM, (h + 1) * NOPE_DIM)
        kn_ref[:, sl] = _rms(kf[:, sl], knn_ref[...]).astype(BF16)
    if v_transposed:
        v_ref[...] = _dot_nt(wv_ref[...], lb).astype(BF16)
    else:
        v_ref[...] = _dot(lb, wv_ref[...]).astype(BF16)
    kp = kpe_ref[...]
    if kp.shape[1] == ROPE_DIM:
        kr_ref[...] = jnp.concatenate([kp, jnp.zeros((kp.shape[0], LANES - ROPE_DIM), F32)], axis=1).astype(BF16)
    else:
        lane = lax.broadcasted_iota(jnp.int32, kp.shape, 1)
        kr_ref[...] = jnp.where(lane < ROPE_DIM, kp, 0.0).astype(BF16)


def _kvproj(lat3, kpe3, pw, tm, v_transposed):
    B, sk, _ = lat3.shape
    hw = MLA_HEADS * V_DIM
    row = lambda n: pl.BlockSpec((None, tm, n), lambda b, i: (b, i, 0))
    if v_transposed:
        vspec, vshape, w_v = pl.BlockSpec((None, hw, tm), lambda b, i: (b, 0, i)), (B, hw, sk), pw['w_vt']
    else:
        vspec, vshape, w_v = row(hw), (B, sk, hw), pw['w_v']
    return pl.pallas_call(
        functools.partial(_kvproj_kernel, v_transposed=v_transposed),
        grid=(B, sk // tm),
        in_specs=[row(KV_RANK), row(kpe3.shape[-1]), _const_spec(pw['w_k'].shape), _const_spec(w_v.shape),
                  _const_spec(pw['knn'].shape)],
        out_specs=[row(MLA_HEADS * NOPE_DIM), row(LANES), vspec],
        out_shape=[jax.ShapeDtypeStruct((B, sk, MLA_HEADS * NOPE_DIM), BF16),
                   jax.ShapeDtypeStruct((B, sk, LANES), BF16), jax.ShapeDtypeStruct(vshape, BF16)],
        compiler_params=_cparams(("parallel", "parallel")),
        name="kvproj",
    )(lat3, kpe3, pw['w_k'], w_v, pw['knn'])


def _last_kblock(qi, tq, tk, P, nk):
    last_key = ((P + qi * tq + tq - 1) // CHUNK) * CHUNK + CHUNK - 1
    return jnp.minimum(last_key // tk, nk - 1)


def _flash_kernel(q_ref, kn_ref, kr_ref, v_ref, o_ref, m_ref, l_ref, acc_ref, *, tq, tk, P, S, nk):
    qi = pl.program_id(1)
    ki = pl.program_id(2)

    @pl.when(ki == 0)
    def _():
        m_ref[...] = jnp.full(m_ref.shape, NEG_BIG, F32)
        l_ref[...] = jnp.zeros(l_ref.shape, F32)
        acc_ref[...] = jnp.zeros(acc_ref.shape, F32)

    @pl.when(ki <= _last_kblock(qi, tq, tk, P, nk))
    def _():
        qpos = P + qi * tq + lax.broadcasted_iota(jnp.int32, (tq, tk), 0)
        kpos = ki * tk + lax.broadcasted_iota(jnp.int32, (tq, tk), 1)
        mask = ((kpos // CHUNK) <= (qpos // CHUNK)) & (kpos < P + S)
        kr = kr_ref[...]
        for h in range(MLA_HEADS):
            qh = q_ref[:, h * Q_SLAB:(h + 1) * Q_SLAB]
            kh = jnp.concatenate([kn_ref[:, h * NOPE_DIM:(h + 1) * NOPE_DIM], kr], axis=1)
            s = jnp.where(mask, _dot_nt(qh, kh), NEG_BIG)
            m_prev = m_ref[h]
            m_new = jnp.maximum(m_prev, jnp.max(s, axis=-1, keepdims=True))
            alpha = jnp.exp(m_prev - m_new)
            p = jnp.exp(s - m_new)
            l_ref[h] = alpha * l_ref[h] + jnp.sum(p, axis=-1, keepdims=True)
            acc_ref[h] = alpha * acc_ref[h] + _dot(p.astype(BF16), v_ref[:, h * V_DIM:(h + 1) * V_DIM])
            m_ref[h] = m_new

    @pl.when(ki == nk - 1)
    def _():
        for h in range(MLA_HEADS):
            o_ref[:, h * V_DIM:(h + 1) * V_DIM] = (acc_ref[h] / l_ref[h]).astype(BF16)


def _flash(q3, kn3, kr3, v3, P, S, tq, tk):
    B = q3.shape[0]
    skp = kn3.shape[1]
    nk = skp // tk
    kmap = lambda b, qi, ki: (b, jnp.minimum(ki, _last_kblock(qi, tq, tk, P, nk)), 0)
    return pl.pallas_call(
        functools.partial(_flash_kernel, tq=tq, tk=tk, P=P, S=S, nk=nk),
        grid=(B, S // tq, nk),
        in_specs=[pl.BlockSpec((None, tq, MLA_HEADS * Q_SLAB), lambda b, qi, ki: (b, qi, 0)),
                  pl.BlockSpec((None, tk, MLA_HEADS * NOPE_DIM), kmap),
                  pl.BlockSpec((None, tk, LANES), kmap),
                  pl.BlockSpec((None, tk, MLA_HEADS * V_DIM), kmap)],
        out_specs=pl.BlockSpec((None, tq, MLA_HEADS * V_DIM), lambda b, qi, ki: (b, qi, 0)),
        out_shape=jax.ShapeDtypeStruct((B, S, MLA_HEADS * V_DIM), BF16),
        scratch_shapes=[pltpu.VMEM((MLA_HEADS, tq, 1), F32), pltpu.VMEM((MLA_HEADS, tq, 1), F32),
                        pltpu.VMEM((MLA_HEADS, tq, V_DIM), F32)],
        compiler_params=_cparams(("parallel", "parallel", "arbitrary")),
        name="mla_attn",
    )(q3, kn3, kr3, v3)


def _attn_dec_kernel(q_ref, lat_ref, kpe_ref, knn_ref, krn_ref, vn_ref, wk_ref, wv_ref, gain_ref, o_ref):
    lb = lat_ref[...].astype(BF16)
    kf = _dot(lb, wk_ref[...])
    vp = _dot(lb, wv_ref[...]).astype(BF16)
    kp = kpe_ref[...]
    krp = jnp.concatenate([kp, jnp.zeros((kp.shape[0], LANES - ROPE_DIM), F32)], axis=1).astype(BF16)
    krn = krn_ref[...]
    hsl = [slice(h * NOPE_DIM, (h + 1) * NOPE_DIM) for h in range(MLA_HEADS)]
    qs = [q_ref[:, h * Q_SLAB:(h + 1) * Q_SLAB] for h in range(MLA_HEADS)]
    knp = [_rms(kf[:, sl], gain_ref[...]).astype(BF16) for sl in hsl]
    sps = [_dot_nt(q, jnp.concatenate([k, krp], axis=1)) for q, k in zip(qs, knp)]
    sns = [_dot_nt(q, jnp.concatenate([knn_ref[:, sl], krn], axis=1)) for q, sl in zip(qs, hsl)]
    for h, (sp, sn) in enumerate(zip(sps, sns)):
        vsl = slice(h * V_DIM, (h + 1) * V_DIM)
        m = jnp.maximum(jnp.max(sp, axis=-1, keepdims=True), jnp.max(sn, axis=-1, keepdims=True))
        pp = jnp.exp(sp - m)
        pn = jnp.exp(sn - m)
        l = jnp.sum(pp, axis=-1, keepdims=True) + jnp.sum(pn, axis=-1, keepdims=True)
        o = _dot(pp.astype(BF16), vp[:, vsl]) + _dot(pn.astype(BF16), vn_ref[:, vsl])
        o_ref[:, vsl] = (o / l).astype(BF16)


def _attn_dec(q3, lat_past, kpe_past, new, pw):
    B, S, _ = q3.shape
    consts = [pw['w_k'], pw['w_v'], pw['knn']]
    specs = [pl.BlockSpec((None, a.shape[1], a.shape[2]), lambda b: (b, 0, 0)) for a in (q3, lat_past, kpe_past, *new)]
    return pl.pallas_call(
        _attn_dec_kernel,
        grid=(B,),
        in_specs=specs + [_const_spec(c.shape) for c in consts],
        out_specs=pl.BlockSpec((None, S, MLA_HEADS * V_DIM), lambda b: (b, 0, 0)),
        out_shape=jax.ShapeDtypeStruct((B, S, MLA_HEADS * V_DIM), BF16),
        compiler_params=_cparams(("parallel",)),
        name="mla_attn_dec",
    )(q3, lat_past, kpe_past, *new, *consts)


def _flash_t_kernel(qt_ref, kt_ref, lt_ref, q_ref, kn_ref, kr_ref, vt_ref, o_ref, m_ref, acc_ref, *, tq, tk, P, S):
    j = pl.program_id(1)
    qi = qt_ref[j]
    ki = kt_ref[j]
    q0 = P + qi * tq
    k0 = ki * tk

    @pl.when(ki == 0)
    def _():
        m_ref[...] = jnp.full(m_ref.shape, NEG_BIG, F32)
        acc_ref[...] = jnp.zeros(acc_ref.shape, F32)

    def step(masked):
        kr = kr_ref[...]
        ones = jnp.ones((ONES_ROWS, tk), BF16)
        if masked:
            kpos = k0 + lax.broadcasted_iota(jnp.int32, (tk, 1), 0)
            qpos = q0 + lax.broadcasted_iota(jnp.int32, (1, tq), 1)
            mask = ((kpos // CHUNK) <= (qpos // CHUNK)) & (kpos < P + S)
        sts = []
        for h in range(MLA_HEADS):
            kh = jnp.concatenate([kn_ref[:, h * NOPE_DIM:(h + 1) * NOPE_DIM], kr], axis=1)
            sts.append(_dot_nt(kh, q_ref[:, h * Q_SLAB:(h + 1) * Q_SLAB]))
        ps, alphas = [], []
        for h in range(MLA_HEADS):
            st = sts[h]
            if masked:
                st = jnp.where(mask, st, NEG_BIG)
            m_prev = m_ref[h]
            m_new = jnp.maximum(m_prev, jnp.max(st, axis=0, keepdims=True))
            alphas.append(jnp.exp(m_prev - m_new))
            ps.append(jnp.exp(st - m_new).astype(BF16))
            m_ref[h] = m_new
        for h in range(MLA_HEADS):
            vt1 = jnp.concatenate([vt_ref[h * V_DIM:(h + 1) * V_DIM, :], ones], axis=0)
            acc_ref[h] = alphas[h] * acc_ref[h] + _dot(vt1, ps[h])

    full = ((k0 + tk - 1) // CHUNK <= q0 // CHUNK) & (k0 + tk <= P + S)
    pl.when(full)(functools.partial(step, False))
    pl.when(jnp.logical_not(full))(functools.partial(step, True))

    @pl.when(lt_ref[j] == 1)
    def _():
        for h in range(MLA_HEADS):
            acc = acc_ref[h]
            o_ref[:, h * V_DIM:(h + 1) * V_DIM] = (acc[:V_DIM] / acc[V_DIM:V_DIM + 1]).T.astype(BF16)


def _flash_t(q3, kn3, kr3, vt3, P, S, tq, tk):
    B = q3.shape[0]
    nk = kn3.shape[1] // tk
    pairs = []
    for qi in range(S // tq):
        last = min((((P + qi * tq + tq - 1) // CHUNK) * CHUNK + CHUNK - 1) // tk, nk - 1)
        pairs += [(qi, ki, int(ki == last)) for ki in range(last + 1)]
    qt, kt, lt = (jnp.asarray(np.array(col, np.int32)) for col in zip(*pairs))
    kmap = lambda b, j, qt, kt, lt: (b, kt[j], 0)
    gs = pltpu.PrefetchScalarGridSpec(
        num_scalar_prefetch=3,
        grid=(B, len(pairs)),
        in_specs=[pl.BlockSpec((None, tq, MLA_HEADS * Q_SLAB), lambda b, j, qt, kt, lt: (b, qt[j], 0)),
                  pl.BlockSpec((None, tk, MLA_HEADS * NOPE_DIM), kmap),
                  pl.BlockSpec((None, tk, LANES), kmap),
                  pl.BlockSpec((None, MLA_HEADS * V_DIM, tk), lambda b, j, qt, kt, lt: (b, 0, kt[j]))],
        out_specs=pl.BlockSpec((None, tq, MLA_HEADS * V_DIM), lambda b, j, qt, kt, lt: (b, qt[j], 0)),
        scratch_shapes=[pltpu.VMEM((MLA_HEADS, 1, tq), F32), pltpu.VMEM((MLA_HEADS, V_DIM + ONES_ROWS, tq), F32)],
    )
    return pl.pallas_call(
        functools.partial(_flash_t_kernel, tq=tq, tk=tk, P=P, S=S),
        grid_spec=gs,
        out_shape=jax.ShapeDtypeStruct((B, S, MLA_HEADS * V_DIM), BF16),
        compiler_params=_cparams(("parallel", "arbitrary")),
        name="mla_attn_t",
    )(qt, kt, lt, q3, kn3, kr3, vt3)


def _memkv_kernel(mem_ref, mnorm_ref, wmk_ref, wmv_ref, mkn_ref, k_ref, v_ref):
    mb = _rms(mem_ref[...], mnorm_ref[...]).astype(BF16)
    kf = _dot(mb, wmk_ref[...])
    for h in range(MEM_HEADS):
        sl = slice(h * MEM_HEAD_DIM, (h + 1) * MEM_HEAD_DIM)
        k_ref[:, sl] = _rms(kf[:, sl], mkn_ref[...])
    v_ref[...] = _dot(mb, wmv_ref[...])


def _memkv(mem2d, mnorm, w_mk, w_mv, mkn, tm):
    T = mem2d.shape[0]
    hw = MEM_HEADS * MEM_HEAD_DIM
    row = lambda n: pl.BlockSpec((tm, n), lambda i: (i, 0))
    return pl.pallas_call(
        _memkv_kernel,
        grid=(T // tm,),
        in_specs=[row(D_MODEL), _const_spec(mnorm.shape), _const_spec(w_mk.shape), _const_spec(w_mv.shape),
                  _const_spec(mkn.shape)],
        out_specs=[row(hw), row(hw)],
        out_shape=[jax.ShapeDtypeStruct((T, hw), F32), jax.ShapeDtypeStruct((T, hw), F32)],
        compiler_params=_cparams(("parallel",)),
        name="mem_kv",
    )(mem2d, mnorm, w_mk, w_mv, mkn)


def _mix_kernel(x_ref, og_ref, om_ref, mk_ref, mv_ref, wout_ref, nmem_ref, wmq_ref, mqn_ref, wmo_ref, nffn_ref,
                wrh_ref, wrl_ref, br_ref, before_ref, x2_ref, h3_ref, idx_ref, gate_ref, counts_ref, cnt_ref):
    def mem_head(ref, b, h):
        if len(ref.shape) == 3:
            return ref[b, :, h * MEM_HEAD_DIM:(h + 1) * MEM_HEAD_DIM].astype(BF16)
        return ref[b, :, h, :].astype(BF16)

    first = (pl.program_id(0) == 0) & (pl.program_id(1) == 0)

    @pl.when(first)
    def _():
        cnt_ref[...] = jnp.zeros(cnt_ref.shape, F32)

    nbm, tm, _ = x_ref.shape
    rows = nbm * tm
    flat = lambda ref: ref[...].reshape(rows, ref.shape[-1])

    x1 = flat(x_ref) + _dot(flat(og_ref), wout_ref[0:GDN_VW, :]) + _dot(flat(om_ref), wout_ref[GDN_VW:, :])
    hb = _rms(x1, nmem_ref[...]).astype(BF16)
    qm = _dot(hb, wmq_ref[...])
    per_batch = []
    hsl = [slice(h * MEM_HEAD_DIM, (h + 1) * MEM_HEAD_DIM) for h in range(MEM_HEADS)]
    for b in range(nbm):
        br = slice(b * tm, (b + 1) * tm)
        qhs = [(_rms(qm[br, sl], mqn_ref[...]) * (MEM_HEAD_DIM ** -0.5)).astype(BF16) for sl in hsl]
        ss = [_dot_nt(qh, mem_head(mk_ref, b, h)) for h, qh in enumerate(qhs)]
        ps = []
        for s in ss:
            p = jnp.exp(s - jnp.max(s, axis=-1, keepdims=True))
            ps.append((p / jnp.sum(p, axis=-1, keepdims=True)).astype(BF16))
        heads = [_dot(p, mem_head(mv_ref, b, h)).astype(BF16) for h, p in enumerate(ps)]
        per_batch.append(jnp.concatenate(heads, axis=1))
    om = per_batch[0] if nbm == 1 else jnp.concatenate(per_batch, axis=0)
    x2 = x1 + _dot(om, wmo_ref[...])
    x2_ref[...] = x2.reshape(x2_ref.shape)
    h3 = _rms(x2, nffn_ref[...])
    hi = h3.astype(BF16)
    packed = _pack_bf16_pairs(hi.astype(F32))
    for c in range(h3_ref.shape[0]):
        h3_ref[c] = packed[:, c * LANES:(c + 1) * LANES]
    lo = (h3 - hi.astype(F32)).astype(BF16)
    wrh = wrh_ref[...]
    logits = _dot(hi, wrh) + _dot(lo, wrh) + _dot(hi, wrl_ref[...]) + br_ref[...]

    lane = lax.broadcasted_iota(jnp.int32, logits.shape, 1).astype(F32)
    vals, idxs = [], []
    cur = logits
    for _ in range(TOP_K):
        mx = jnp.max(cur, axis=-1, keepdims=True)
        ix = jnp.min(jnp.where(cur == mx, lane, float(LANES)), axis=-1, keepdims=True)
        vals.append(mx)
        idxs.append(ix)
        cur = jnp.where(lane == ix, -3e38, cur)
    es = [jnp.exp(v - vals[0]) for v in vals]
    den = es[0] + es[1] + es[2] + es[3]

    sel = jnp.zeros(logits.shape, F32)
    for k in range(TOP_K):
        sel = sel + jnp.where(lane == idxs[k], 1.0, 0.0)
    excl = _dot(before_ref[...], sel.astype(BF16)) + cnt_ref[...]
    cnt_ref[...] = cnt_ref[...] + jnp.sum(sel, axis=0, keepdims=True)
    counts_ref[...] = cnt_ref[...].astype(jnp.int32)

    idx_out = jnp.zeros(logits.shape, F32)
    gate_out = jnp.zeros(logits.shape, F32)
    for k in range(TOP_K):
        rank = jnp.sum(jnp.where(lane == idxs[k], excl, 0.0), axis=-1, keepdims=True)
        idx_out = jnp.where(lane == float(k), idxs[k], idx_out)
        idx_out = jnp.where(lane == float(TOP_K + k), rank, idx_out)
        gate_out = jnp.where(lane == float(k), es[k] / den, gate_out)
    idx_ref[...] = idx_out.T[:2 * TOP_K, :].astype(jnp.int32)
    gate_ref[...] = gate_out.reshape(gate_ref.shape)


def _mix(x3, og3, om3, mk3, mv3, pw, tm, b0, B):
    S = x3.shape[1]
    hw = MEM_HEADS * MEM_HEAD_DIM
    nbm = _pick(B, tuple(n for n in (8, 4, 2, 1) if n * tm <= MIX_ROWS and b0 % n == 0)) if tm == S else 1
    nsb = S // tm
    boff = b0 // nbm
    tile_in = lambda n: pl.BlockSpec((nbm, tm, n), lambda b, i: (b + boff, i, 0))
    tile = lambda n: pl.BlockSpec((nbm, tm, n), lambda b, i: (b, i, 0))
    if mk3.ndim == 3:
        memspec = pl.BlockSpec((nbm, N_MEM, hw), lambda b, i: (b + boff, 0, 0))
    else:
        memspec = pl.BlockSpec((nbm, N_MEM, MEM_HEADS, MEM_HEAD_DIM), lambda b, i: (b + boff, 0, 0, 0))
    consts = [pw['w_out'], pw['nmem'], pw['w_mq'], pw['mqn'], pw['w_mo'], pw['nffn'], pw['wr_hi'], pw['wr_lo'],
              pw['b_r'], jnp.asarray(np.tri(nbm * tm, k=-1, dtype=np.float32), dtype=BF16)]
    return pl.pallas_call(
        _mix_kernel,
        grid=(B // nbm, nsb),
        in_specs=[tile_in(D_MODEL), tile_in(GDN_VW), tile_in(MLA_HEADS * V_DIM), memspec, memspec]
                 + [_const_spec(c.shape) for c in consts],
        out_specs=[tile(D_MODEL), pl.BlockSpec((PK_CHUNKS, nbm * tm, LANES), lambda b, i: (0, b * nsb + i, 0)),
                   pl.BlockSpec((2 * TOP_K, nbm * tm), lambda b, i: (0, b * nsb + i)),
                   tile(LANES), _const_spec((1, LANES))],
        out_shape=[jax.ShapeDtypeStruct((B, S, D_MODEL), F32),
                   jax.ShapeDtypeStruct((PK_CHUNKS, B * S, LANES), jnp.uint32),
                   jax.ShapeDtypeStruct((2 * TOP_K, B * S), jnp.int32), jax.ShapeDtypeStruct((B, S, LANES), F32),
                   jax.ShapeDtypeStruct((1, LANES), jnp.int32)],
        scratch_shapes=[pltpu.VMEM((1, LANES), F32)],
        compiler_params=_cparams(("arbitrary", "arbitrary")),
        name="mix_mem_router",
    )(x3, og3, om3, mk3, mv3, *consts)


def _expert_kernel(be_ref, nx_ref, nv_ref, nu_ref, rows_ref, wgu_hbm, bgu_ref, wd_hbm, bd_ref, y_ref,
                   wgus_ref, wds_ref, wgub_ref, wdb_ref, sem_ref):
    i = pl.program_id(0)
    used = i < nu_ref[0]
    new_expert = (i == 0) | (be_ref[i] != be_ref[jnp.maximum(i - 1, 0)])

    def weight_copies(e):
        return (pltpu.make_async_copy(wgu_hbm.at[e], wgus_ref, sem_ref.at[0]),
                pltpu.make_async_copy(wd_hbm.at[e], wds_ref, sem_ref.at[1]))

    @pl.when(i == 0)
    def _():
        for cp in weight_copies(be_ref[0]):
            cp.start()

    @pl.when(used & new_expert)
    def _():
        for cp in weight_copies(be_ref[i]):
            cp.wait()

        def cast(r, carry):
            rs = pl.ds(pl.multiple_of(r * LANES, LANES), LANES)
            wgub_ref[rs, :] = wgus_ref[rs, :].astype(BF16)
            wdb_ref[rs, :] = wds_ref[rs, :].astype(BF16)
            return carry
        lax.fori_loop(0, D_MODEL // LANES, cast, 0)

        @pl.when(nx_ref[i] >= 0)
        def _():
            for cp in weight_copies(nx_ref[i]):
                cp.start()

    def ffn(nrows):
        packed = jnp.concatenate([rows_ref[c, :nrows, :] for c in range(PK_CHUNKS)], axis=1)
        x = _unpack_bf16_pairs(packed).astype(BF16)
        acc = None
        for c in range(D_FF // FF_CHUNK):
            gs_ = slice(c * FF_CHUNK, (c + 1) * FF_CHUNK)
            us_ = slice(D_FF + c * FF_CHUNK, D_FF + (c + 1) * FF_CHUNK)
            gt = jnp.minimum(_dot(x, wgub_ref[:, gs_]) + bgu_ref[:, gs_], SWIGLU_LIMIT)
            up = jnp.clip(_dot(x, wgub_ref[:, us_]) + bgu_ref[:, us_], -SWIGLU_LIMIT, SWIGLU_LIMIT)
            act = gt * _sigmoid(SWIGLU_ALPHA * gt) * (up + 1.0)
            part = _dot(act.astype(BF16), wdb_ref[gs_, :])
            acc = part if acc is None else acc + part
        ypk = _pack_bf16_pairs((acc + bd_ref[...]).astype(BF16).astype(F32))
        for c in range(PK_CHUNKS):
            y_ref[c, :nrows, :] = ypk[:, c * LANES:(c + 1) * LANES]
        if nrows < MOE_ROWS:
            y_ref[:, nrows:, :] = jnp.zeros((PK_CHUNKS, MOE_ROWS - nrows, LANES), y_ref.dtype)

    half_full = nv_ref[i] <= MOE_ROWS // 2
    pl.when(used & jnp.logical_not(half_full))(functools.partial(ffn, MOE_ROWS))
    pl.when(used & half_full)(functools.partial(ffn, MOE_ROWS // 2))

    @pl.when(jnp.logical_not(used))
    def _():
        y_ref[...] = jnp.zeros(y_ref.shape, y_ref.dtype)


def _experts(block_e, next_e, block_rows, n_used, rows, w_gu, b_gu, w_down, b_down):
    n_rows = rows.shape[1]
    nb = n_rows // MOE_ROWS
    gs = pltpu.PrefetchScalarGridSpec(
        num_scalar_prefetch=4,
        grid=(nb,),
        in_specs=[pl.BlockSpec((PK_CHUNKS, MOE_ROWS, LANES), lambda i, be, nx, nv, nu: (0, i, 0)),
                  pl.BlockSpec(memory_space=pl.ANY),
                  pl.BlockSpec((None, 1, 2 * D_FF), lambda i, be, nx, nv, nu: (be[i], 0, 0)),
                  pl.BlockSpec(memory_space=pl.ANY),
                  pl.BlockSpec((None, 1, D_MODEL), lambda i, be, nx, nv, nu: (be[i], 0, 0))],
        out_specs=pl.BlockSpec((PK_CHUNKS, MOE_ROWS, LANES), lambda i, be, nx, nv, nu: (0, i, 0)),
        scratch_shapes=[pltpu.VMEM((D_MODEL, 2 * D_FF), F32), pltpu.VMEM((D_FF, D_MODEL), F32),
                        pltpu.VMEM((D_MODEL, 2 * D_FF), BF16), pltpu.VMEM((D_FF, D_MODEL), BF16),
                        pltpu.SemaphoreType.DMA((2,))],
    )
    return pl.pallas_call(
        _expert_kernel,
        grid_spec=gs,
        out_shape=jax.ShapeDtypeStruct((PK_CHUNKS, n_rows, LANES), jnp.uint32),
        compiler_params=_cparams(("arbitrary",)),
        name="moe_experts",
    )(block_e, next_e, block_rows, n_used, rows, w_gu, b_gu, w_down, b_down)


def _sc_mesh():
    return plsc.VectorSubcoreMesh(core_axis_name="core", subcore_axis_name="subcore")


def _sc_scatter_rows(x3s, pos_ts, n_rows):
    C, _, L = x3s[0].shape
    K = pos_ts[0].shape[0]
    ns = len(x3s)

    @functools.partial(pl.kernel, out_type=jax.ShapeDtypeStruct((C, n_rows, L), x3s[0].dtype), mesh=_sc_mesh(),
                       scratch_types=[])
    def scatter(*refs):
        o_hbm = refs[2 * ns]
        for s in range(ns):
            x_hbm, i_hbm = refs[s], refs[ns + s]
            nwin = x3s[s].shape[1] // SC_WINDOW
            for c in range(C):
                def body(x_vmem, i_vmem, c=c):
                    for k in range(K):
                        pltpu.sync_copy(x_vmem, o_hbm.at[c].at[i_vmem.at[k]])

                pltpu.emit_pipeline(
                    body, grid=(nwin,),
                    in_specs=[pl.BlockSpec((SC_WINDOW, L), lambda i, c=c, nwin=nwin: (c * nwin + i, 0)),
                              pl.BlockSpec((K, SC_WINDOW), lambda i: (0, i))],
                    out_specs=[], core_axis_name=("core", "subcore"), dimension_semantics=(pltpu.PARALLEL,),
                )(x_hbm, i_hbm)

    return scatter(*[x.reshape(-1, L) for x in x3s], *pos_ts)


def _sc_gather_rows(table3, idxs):
    C, _, L = table3.shape
    ns = len(idxs)
    out_type = [jax.ShapeDtypeStruct((C * i.shape[0], L), table3.dtype) for i in idxs]

    @functools.partial(pl.kernel, out_type=out_type, mesh=_sc_mesh(), scratch_types=[])
    def gather(t_hbm, *refs):
        for s in range(ns):
            i_hbm, o_hbm = refs[s], refs[ns + s]
            nwin = idxs[s].shape[0] // SC_WINDOW
            for c in range(C):
                def body(i_vmem, o_vmem, c=c):
                    pltpu.sync_copy(t_hbm.at[c].at[i_vmem.at[0]], o_vmem)

                pltpu.emit_pipeline(
                    body, grid=(nwin,),
                    in_specs=[pl.BlockSpec((1, SC_WINDOW), lambda i: (0, i))],
                    out_specs=[pl.BlockSpec((SC_WINDOW, L), lambda i, c=c, nwin=nwin: (c * nwin + i, 0))],
                    core_axis_name=("core", "subcore"), dimension_semantics=(pltpu.PARALLEL,),
                )(i_hbm, o_hbm)

    outs = gather(table3, *[i.reshape(1, -1) for i in idxs])
    return [o.reshape(C, -1, L) for o in outs]


def _combine_kernel(x2_ref, g_ref, gate_ref, *rest):
    o_ref = rest[-1]
    gate = gate_ref[...]
    half = D_MODEL // 2
    for c in range(PK_CHUNKS):
        lo_s = slice(c * LANES, (c + 1) * LANES)
        hi_s = slice(half + c * LANES, half + (c + 1) * LANES)
        acc_lo = x2_ref[:, lo_s]
        acc_hi = x2_ref[:, hi_s]
        for k in range(TOP_K):
            w = g_ref[c, k]
            gk = gate[:, k:k + 1]
            acc_lo = acc_lo + pltpu.bitcast(w << 16, F32) * gk
            acc_hi = acc_hi + pltpu.bitcast(w & jnp.uint32(0xFFFF0000), F32) * gk
        o_ref[:, lo_s] = acc_lo
        o_ref[:, hi_s] = acc_hi


def _combine(x2, g4, gate, tm, out_buf=None, row0=0, t_total=None):
    T = x2.shape[0]
    t_total = T if t_total is None else t_total
    blk0 = row0 // tm
    in_specs = [pl.BlockSpec((tm, D_MODEL), lambda i: (i, 0)),
                pl.BlockSpec((PK_CHUNKS, TOP_K, tm, LANES), lambda i: (0, 0, i, 0)),
                pl.BlockSpec((tm, LANES), lambda i: (i, 0))]
    args = [x2, g4, gate]
    aliases = {}
    if out_buf is not None:
        in_specs.append(pl.BlockSpec(memory_space=pl.ANY))
        args.append(out_buf)
        aliases = {3: 0}
    return pl.pallas_call(
        _combine_kernel,
        grid=(T // tm,),
        in_specs=in_specs,
        out_specs=pl.BlockSpec((tm, D_MODEL), lambda i: (i + blk0, 0)),
        out_shape=jax.ShapeDtypeStruct((t_total, D_MODEL), F32),
        input_output_aliases=aliases,
        compiler_params=_cparams(("parallel",)),
        name="moe_combine",
    )(*args)


def _plan_kernel(idxr_ref, base_ref, pos_ref):
    rows = 2 * TOP_K
    base = jnp.broadcast_to(base_ref[...], (rows, LANES))
    is_id = lax.broadcasted_iota(jnp.int32, (rows, LANES), 0) < TOP_K
    for j in range(idxr_ref.shape[1] // LANES):
        cols = slice(j * LANES, (j + 1) * LANES)
        blk = idxr_ref[:, cols]
        start = jnp.take_along_axis(base, jnp.where(is_id, blk, 0), axis=1)
        pos_ref[:, cols] = (start + pltpu.roll(blk, TOP_K, 0))[:TOP_K, :]


def _plan(idxr, base):
    T = idxr.shape[1]
    tm = _pick(T, (2048, 1024, 512, 256, 128))
    return pl.pallas_call(
        _plan_kernel,
        grid=(T // tm,),
        in_specs=[pl.BlockSpec((2 * TOP_K, tm), lambda i: (0, i)), _const_spec((1, LANES))],
        out_specs=pl.BlockSpec((TOP_K, tm), lambda i: (0, i)),
        out_shape=jax.ShapeDtypeStruct((TOP_K, T), jnp.int32),
        compiler_params=_cparams(("parallel",)),
        name="moe_plan",
    )(idxr, base)


def _moe(streams, places, ew):
    cnts = [st[3][0, :N_EXPERTS] for st in streams]
    total = sum(cnts)
    padded = (total + MOE_ROWS - 1) // MOE_ROWS * MOE_ROWS
    pad_end = jnp.cumsum(padded)
    pad_start = pad_end - padded
    pos_ts = []
    base = pad_start
    for (h3p, idxr, gate, counts, x2), cnt in zip(streams, cnts):
        pos_ts.append(_plan(idxr, _pad_lanes(base)))
        base = base + cnt
    n_assign = sum(st[0].shape[1] for st in streams) * TOP_K
    nb = -(-n_assign // MOE_ROWS) + N_EXPERTS
    starts = jnp.arange(nb, dtype=jnp.int32) * MOE_ROWS
    block_e = jnp.minimum(jnp.sum((pad_end[None, :] <= starts[:, None]).astype(jnp.int32), axis=1), N_EXPERTS - 1)
    n_used = (pad_end[-1] // MOE_ROWS).astype(jnp.int32).reshape(1)
    ar = jnp.arange(N_EXPERTS, dtype=jnp.int32)
    later = (padded > 0)[None, :] & (ar[None, :] > ar[:, None])
    nxt = jnp.min(jnp.where(later, ar[None, :], N_EXPERTS), axis=1)
    nxt = jnp.where(nxt >= N_EXPERTS, -1, nxt)
    mine = block_e[:, None] == ar[None, :]
    next_e = jnp.sum(jnp.where(mine, nxt[None, :], 0), axis=1).astype(jnp.int32)
    seg_end = jnp.sum(jnp.where(mine, (pad_start + total)[None, :], 0), axis=1)
    block_rows = jnp.clip(seg_end - starts, 0, MOE_ROWS).astype(jnp.int32)
    rows = _sc_scatter_rows([st[0] for st in streams], pos_ts, nb * MOE_ROWS)
    y_rows = _experts(block_e, next_e, block_rows, n_used, rows, ew['w_gu'], ew['b_gu'], ew['w_down'], ew['b_down'])
    gs = _sc_gather_rows(y_rows, [p.reshape(-1) for p in pos_ts])
    outs = []
    for (h3p, idxr, gate, counts, x2), g, place in zip(streams, gs, places):
        T = x2.shape[0]
        outs.append(_combine(x2, g.reshape(PK_CHUNKS, TOP_K, T, LANES), gate, _pick(T, (512, 256, 128, 64)), *place))
    return outs


def _pad_lanes(v, n=LANES, fill=0.0):
    return jnp.pad(v, (0, n - v.shape[0]), constant_values=fill).reshape(1, n)


def _prep_weights(norm_mix, w_in, conv_w, a_log, dt_bias, gdn_norm, q_a_norm, w_qb, kv_a_norm, w_kvb, q_norm,
                  k_nope_norm, k_rope_norm, w_out, norm_mem, mem_norm, w_mq, w_mk, w_mv, mq_norm, mk_norm, w_mo,
                  norm_ffn, w_router, b_router, w_gu, b_gu, w_down, b_down):
    c = np.cumsum([CONV_DIM, GDN_VW, GDN_HEADS, GDN_HEADS, Q_RANK, KV_RANK])
    w_u, w_z, w_a, w_b, w_cq, w_ckv, w_kpe = [w_in[:, lo:hi] for lo, hi in
                                              zip([0, *c], [*c, w_in.shape[1]])]
    w_s = jnp.concatenate([w_kpe, w_a, w_b], axis=1)
    w_s = jnp.pad(w_s, ((0, 0), (0, LANES - w_s.shape[1])))
    wq = w_qb.reshape(Q_RANK, MLA_HEADS, QK_DIM)
    wq = jnp.pad(wq, ((0, 0), (0, 0), (0, Q_SLAB - QK_DIM))).reshape(Q_RANK, MLA_HEADS * Q_SLAB)
    wkv = w_kvb.reshape(KV_RANK, MLA_HEADS, NOPE_DIM + V_DIM)
    w_k = wkv[:, :, :NOPE_DIM].reshape(KV_RANK, -1).astype(BF16)
    w_v = wkv[:, :, NOPE_DIM:].reshape(KV_RANK, -1).astype(BF16)
    wr = jnp.pad(w_router, ((0, 0), (0, LANES - N_EXPERTS)))
    wr_hi = wr.astype(BF16)
    wr_lo = (wr - wr_hi.astype(F32)).astype(BF16)
    row = lambda v: v.reshape(1, -1)
    gpad = ROPE_DIM
    pw = dict(
        nmix=row(norm_mix), w_u=w_u.astype(BF16), w_z=w_z.astype(BF16), w_cq=w_cq.astype(BF16),
        w_ckv=w_ckv.astype(BF16), w_s=w_s.astype(BF16), qan=row(q_a_norm), w_qb=wq.astype(BF16),
        qn=_pad_lanes(q_norm, Q_SLAB), kvan=row(kv_a_norm), krn=_pad_lanes(k_rope_norm),
        alog=jnp.pad(a_log, (gpad, LANES - gpad - GDN_HEADS)).reshape(1, LANES),
        dtb=jnp.pad(dt_bias, (gpad, LANES - gpad - GDN_HEADS)).reshape(1, LANES),
        conv_w=conv_w, gnorm=row(gdn_norm), w_k=w_k, w_v=w_v, w_vt=w_v.T, knn=row(k_nope_norm),
        w_out=w_out.astype(BF16), nmem=row(norm_mem), w_mq=w_mq.astype(BF16), mqn=row(mq_norm),
        w_mo=w_mo.astype(BF16), nffn=row(norm_ffn), wr_hi=wr_hi, wr_lo=wr_lo,
        b_r=_pad_lanes(b_router, LANES, NEG_BIG),
        mnorm=row(mem_norm), w_mk=w_mk.astype(BF16), w_mv=w_mv.astype(BF16), mkn=row(mk_norm),
    )
    ew = dict(w_gu=w_gu, b_gu=b_gu.reshape(N_EXPERTS, 1, 2 * D_FF), w_down=w_down,
              b_down=b_down.reshape(N_EXPERTS, 1, D_MODEL))
    return pw, ew


def _rope_tables(P, S):
    half = ROPE_DIM // 2
    inv = ROPE_THETA ** (-jnp.arange(half, dtype=F32) / half)
    ang = (P + jnp.arange(S, dtype=jnp.int32)).astype(F32)[:, None] * inv[None, :]
    cos, sin = jnp.cos(ang), jnp.sin(ang)
    zh = jnp.zeros((S, half), F32)
    zz = jnp.zeros((S, LANES - ROPE_DIM), F32)
    return (jnp.concatenate([cos, cos, zz], 1), jnp.concatenate([-sin, zh, zz], 1),
            jnp.concatenate([zh, sin, zz], 1))


def _pick(n, prefs):
    for t in prefs:
        if n % t == 0:
            return t
    return n


def _trunk_front(x, lat_past, kpe_past, s0, conv_past, mem_k, mem_v, pw, n_groups):
    B, S, D = x.shape
    P = lat_past.shape[1]
    T = B * S
    tm = _pick(S, (512, 256, 128, 64))
    tm_in = 512 if (T % 512 == 0 and (512 % S == 0 or S % 512 == 0)) else tm
    if T % 1024 == 0 and S % 1024 == 0:
        tm_in = 1024
    u, z, q, lat_new, small = _inproj(x.reshape(T, D), S, tm_in, pw, _rope_tables(P, S))

    LC = _pick(S, (256, 128, 64))
    NB = _pick(B, tuple(n for n in (8, 4, 2, 1) if n * (LC // CHUNK) <= GDN_UNITS))
    o_gdn, s_new, conv_new = _gdn(u.reshape(B, S, CONV_DIM), small.reshape(B, S, LANES), z.reshape(B, S, GDN_VW),
                                  pw['conv_w'], conv_past, s0, pw['gnorm'], NB, LC)

    kpe_new = small[:, :ROPE_DIM].reshape(B, S, ROPE_DIM)
    lat3 = lat_new.reshape(B, S, KV_RANK)
    small3 = small.reshape(B, S, LANES)
    q3 = q.reshape(B, S, MLA_HEADS * Q_SLAB)
    tq = _pick(S, (512, 256, 128, 64))
    if P == 0 and S % 512 == 0:
        kn, kr, vt = _kvproj(lat3, small3, pw, _pick(S, (2048, 1024, 512)), True)
        o_mla = _flash_t(q3, kn, kr, vt, P, S, tq, 512)
    elif P > 0 and P % CHUNK == 0 and S == CHUNK:
        new = _kvproj(lat3, small3, pw, S, False)
        o_mla = _attn_dec(q3, lat_past, kpe_past, new, pw)
    else:
        sk = P + S
        tk = 512 if S >= 512 else -(-sk // LANES) * LANES
        skp = -(-sk // tk) * tk
        lat_all = jnp.pad(jnp.concatenate([lat_past, lat3], axis=1), ((0, 0), (0, skp - sk), (0, 0)))
        kpe_all = jnp.pad(jnp.concatenate([kpe_past, kpe_new], axis=1), ((0, 0), (0, skp - sk), (0, 0)))
        key_major = tq >= LANES
        kn, kr, v = _kvproj(lat_all, kpe_all, pw, _pick(skp, (512,)), key_major)
        o_mla = (_flash_t if key_major else _flash)(q3, kn, kr, v, P, S, tq, tk)

    streams = []
    gb = B // n_groups
    for gi in range(n_groups):
        x2, h3p, idxr, gate, counts = _mix(x, o_gdn, o_mla, mem_k, mem_v, pw, tm, gi * gb, gb)
        tg = gb * S
        streams.append((h3p, idxr, gate.reshape(tg, LANES), counts, x2.reshape(tg, D)))
    return streams, (lat_new.reshape(B, S, KV_RANK), kpe_new, s_new, conv_new)


def kernel(x_prompt, x_sample, cache_kv_latent, cache_k_rope, state_gdn, state_conv, cache_mem_k, cache_mem_v, mem_prompt, norm_mix, w_in, conv_w, a_log, dt_bias, gdn_norm, q_a_norm, w_qb, kv_a_norm, w_kvb, q_norm, k_nope_norm, k_rope_norm, w_out, norm_mem, mem_norm, w_mq, w_mk, w_mv, mq_norm, mk_norm, w_mo, norm_ffn, w_router, b_router, w_gu, b_gu, w_down, b_down):
    depth = norm_mix.shape[0]
    yp, ys = x_prompt, x_sample
    bp = x_prompt.shape[0]
    hw = MEM_HEADS * MEM_HEAD_DIM
    outs = [[] for _ in range(10)]
    for l in range(depth):
        pw, ew = _prep_weights(norm_mix[l], w_in[l], conv_w[l], a_log[l], dt_bias[l], gdn_norm[l], q_a_norm[l],
                               w_qb[l], kv_a_norm[l], w_kvb[l], q_norm[l], k_nope_norm[l], k_rope_norm[l], w_out[l],
                               norm_mem[l], mem_norm[l], w_mq[l], w_mk[l], w_mv[l], mq_norm[l], mk_norm[l], w_mo[l],
                               norm_ffn[l], w_router[l], b_router[l], w_gu[l], b_gu[l], w_down[l], b_down[l])
        nm = mem_prompt.shape[1]
        mk, mv = _memkv(mem_prompt.reshape(bp * nm, D_MODEL), pw['mnorm'], pw['w_mk'], pw['w_mv'], pw['mkn'],
                        _pick(bp * nm, (512, 256)))
        mk = mk.reshape(bp, nm, hw)
        mv = mv.reshape(bp, nm, hw)
        n_groups = 2 if bp % 2 == 0 else 1
        streams_p, (lat, kpe, s_fin, cv) = _trunk_front(
            yp, jnp.zeros((bp, 0, KV_RANK), F32), jnp.zeros((bp, 0, ROPE_DIM), F32),
            jnp.zeros((bp, GDN_HEADS, GDN_DK, GDN_DV), F32), jnp.zeros((bp, CONV_W - 1, CONV_DIM), F32), mk, mv, pw,
            n_groups)
        bs = x_sample.shape[0]
        (stream_s,), (lat2, kpe2, s_fin2, cv2) = _trunk_front(
            ys, cache_kv_latent[l], cache_k_rope[l], state_gdn[l], state_conv[l],
            cache_mem_k[l], cache_mem_v[l], pw, 1)
        tp = yp.shape[0] * yp.shape[1]
        tg = tp // n_groups
        ybuf = None
        for gi, st in enumerate(streams_p):
            last = gi == n_groups - 1
            res = _moe([st] + ([stream_s] if last else []),
                       [(ybuf, gi * tg, tp)] + ([(None, 0, None)] if last else []), ew)
            ybuf = res[0]
        yp, ys = ybuf.reshape(yp.shape), res[1].reshape(ys.shape)
        for lst, val in zip(outs, (lat, kpe, s_fin, cv, mk.reshape(bp, nm, MEM_HEADS, MEM_HEAD_DIM),
                                   mv.reshape(bp, nm, MEM_HEADS, MEM_HEAD_DIM), lat2, kpe2, s_fin2, cv2)):
            lst.append(val)
    return (yp, ys) + tuple(jnp.stack(o) for o in outs)
```
